```python
import math
import jax, jax.numpy as jnp
from jax import lax
import numpy as np

D_MODEL = 1024
BATCH = 8
SEQ = 2048
DEPTH = 2
DEC_BATCH = 128
DEC_SEQ = 8
PAST_LEN = 16384
PAGE_SIZE = 128

EPS = 1e-6
A_HEADS = 4
A_DK = 128
A_DV = 128
A_CONV = 4
A_CHUNK = 64
B_HEADS = 8
B_N = 64
B_LORA_W = 64
B_LORA_A = 64
B_GN_EPS = 64e-5
B_DECAY_SCALE = math.exp(-0.5)
C_HEADS = 8
C_DQK = 64
C_DV = 128
C_CHUNK = 64
C_GATE_CAP = 15.0

A_KEY = A_HEADS * A_DK
A_VAL = A_HEADS * A_DV
A_QKV = 2 * A_KEY + A_VAL
B_W = B_HEADS * B_N
C_QK = C_HEADS * C_DQK
C_V = C_HEADS * C_DV
A_COLS = A_QKV + A_VAL + 2 * A_HEADS
B_COLS = 4 * B_W + B_LORA_W + B_LORA_A
P0 = A_COLS + B_COLS
MIX0 = A_VAL + B_W
P1 = 2 * C_QK + 3 * C_V + 2 * C_HEADS
MIX1 = C_V

kernel_name = 'hybrid_gdn_rwkv7_mlstm_decode_step'


def rms_norm(x, w):
    xf = x.astype(jnp.float32)
    y = xf * lax.rsqrt(jnp.mean(xf * xf, axis=-1, keepdims=True) + EPS)
    return (y * w.astype(jnp.float32)).astype(x.dtype)


def l2_normalize(x):
    xf = x.astype(jnp.float32)
    return xf * lax.rsqrt(jnp.sum(xf * xf, axis=-1, keepdims=True) + EPS)


def soft_cap(t):
    return C_GATE_CAP * jnp.tanh(t / C_GATE_CAP)


def _to_blocks(t, c):
    bsz, L = t.shape[0], t.shape[1]
    t = t.astype(jnp.float32).reshape((bsz, L // c, c) + t.shape[2:])
    return jnp.moveaxis(t, 3, 1)


def _from_blocks(o):
    n, bsz, h, c, d = o.shape
    return jnp.transpose(o, (1, 0, 3, 2, 4)).reshape(bsz, n * c, h, d)


def causal_conv_silu(u, buf, w):
    K = w.shape[0]
    L = u.shape[1]
    ext = jnp.concatenate([buf.astype(u.dtype), u], axis=1)
    y = sum(ext[:, j:j + L] * w[j] for j in range(K))
    return jax.nn.silu(y.astype(jnp.float32)), ext[:, L:]


def gated_delta_chunked(q, k, v, beta, logd, s0):
    L = q.shape[1]
    c = math.gcd(L, A_CHUNK)
    qc, kc, vc = _to_blocks(q, c), _to_blocks(k, c), _to_blocks(v, c)
    bc = _to_blocks(beta, c)
    g = jnp.cumsum(_to_blocks(logd, c), axis=-1)
    idx = jnp.arange(c)
    causal = idx[:, None] >= idx[None, :]
    strict = idx[:, None] > idx[None, :]
    decay = jnp.exp(jnp.where(causal, g[..., :, None] - g[..., None, :], -jnp.inf))
    amat = jnp.where(strict, bc[..., :, None] * decay * jnp.einsum('bhntd,bhnsd->bhnts', kc, kc), 0.0)
    rhs = jnp.concatenate([bc[..., None] * vc, (bc * jnp.exp(g))[..., None] * kc], axis=-1)
    sol = lax.linalg.triangular_solve(amat + jnp.eye(c, dtype=jnp.float32), rhs,
                                      left_side=True, lower=True, unit_diagonal=True)
    dv = v.shape[-1]
    u_blk, w_blk = sol[..., :dv], sol[..., dv:]
    attn = decay * jnp.einsum('bhntd,bhnsd->bhnts', qc, kc)
    qg = qc * jnp.exp(g)[..., None]
    kd = kc * jnp.exp(g[..., -1:] - g)[..., None]
    g_end = jnp.exp(g[..., -1])

    def step(s, blk):
        u_, w_, attn_, qg_, kd_, ge_ = blk
        nv = u_ - jnp.einsum('bhtk,bhkv->bhtv', w_, s)
        o = jnp.einsum('bhtk,bhkv->bhtv', qg_, s) + jnp.einsum('bhts,bhsv->bhtv', attn_, nv)
        s = ge_[..., None, None] * s + jnp.einsum('bhsk,bhsv->bhkv', kd_, nv)
        return s, o

    xs = tuple(jnp.moveaxis(t, 2, 0) for t in (u_blk, w_blk, attn, qg, kd, g_end))
    s, o = lax.scan(step, s0.astype(jnp.float32), xs)
    return _from_blocks(o), s


def rwkv7_scan(r, w, k, v, a, b, s0):
    def step(s, inp):
        r_, w_, k_, v_, a_, b_ = inp
        sa = jnp.einsum('bhvk,bhk->bhv', s, a_)
        s = s * w_[:, :, None, :] + sa[..., None] * b_[:, :, None, :] + v_[..., None] * k_[:, :, None, :]
        return s, jnp.einsum('bhvk,bhk->bhv', s, r_)

    xs = tuple(jnp.moveaxis(t.astype(jnp.float32), 1, 0) for t in (r, w, k, v, a, b))
    s, o = lax.scan(step, s0.astype(jnp.float32), xs)
    return jnp.moveaxis(o, 0, 1), s


def mlstm_chunked(q, k, v, logi, logf, c0, n0, m0):
    L = q.shape[1]
    c = math.gcd(L, C_CHUNK)
    qc, kc, vc = _to_blocks(q, c), _to_blocks(k, c), _to_blocks(v, c)
    ic = _to_blocks(logi, c)
    bcum = jnp.cumsum(_to_blocks(logf, c), axis=-1)
    idx = jnp.arange(c)
    causal = idx[:, None] >= idx[None, :]
    dmat = jnp.where(causal, bcum[..., :, None] - bcum[..., None, :] + ic[..., None, :], -jnp.inf)
    dmax = jnp.max(dmat, axis=-1)
    qk = jnp.einsum('bhntd,bhnsd->bhnts', qc, kc)

    def step(carry, blk):
        cm, nm, m = carry
        q_, k_, v_, b_, i_, d_, dmax_, qk_ = blk
        inter = b_ + m[..., None]
        mt = jnp.maximum(inter, dmax_)
        pm = jnp.exp(d_ - mt[..., None]) * qk_
        e_inter = jnp.exp(inter - mt)
        num = e_inter[..., None] * jnp.einsum('bhtk,bhkv->bhtv', q_, cm) + jnp.einsum('bhts,bhsv->bhtv', pm, v_)
        den = e_inter * jnp.einsum('bhtk,bhk->bht', q_, nm) + jnp.sum(pm, axis=-1)
        h = num / jnp.maximum(jnp.abs(den), jnp.exp(-mt))[..., None]
        m_new = mt[..., -1]
        w_s = jnp.exp(b_[..., -1:] - b_ + i_ - m_new[..., None])
        e_c = jnp.exp(b_[..., -1] + m - m_new)
        cm = e_c[..., None, None] * cm + jnp.einsum('bhs,bhsk,bhsv->bhkv', w_s, k_, v_)
        nm = e_c[..., None] * nm + jnp.einsum('bhs,bhsk->bhk', w_s, k_)
        return (cm, nm, m_new), h

    xs = tuple(jnp.moveaxis(t, 2, 0) for t in (qc, kc, vc, bcum, ic, dmat, dmax, qk))
    init = (c0.astype(jnp.float32), n0.astype(jnp.float32), m0.astype(jnp.float32))
    (cm, nm, m), h = lax.scan(step, init, xs)
    return _from_blocks(h), cm, nm, m


def mixer_ab_layer(x, conv_buf, s_gdn, x_prev, s_rwkv, p):
    f32 = jnp.float32
    bsz, L, _ = x.shape
    xn = rms_norm(x, p['norm0_w'])
    proj = xn @ p['w_in0']
    pa, pb = proj[..., :A_COLS], proj[..., A_COLS:]
    qkv, conv_new = causal_conv_silu(pa[..., :A_QKV], conv_buf, p['gdn_conv_w'])
    q = l2_normalize(qkv[..., :A_KEY].reshape(bsz, L, A_HEADS, A_DK)) * (A_DK ** -0.5)
    k = l2_normalize(qkv[..., A_KEY:2 * A_KEY].reshape(bsz, L, A_HEADS, A_DK))
    v = qkv[..., 2 * A_KEY:].reshape(bsz, L, A_HEADS, A_DV)
    z_a = pa[..., A_QKV:A_QKV + A_VAL].astype(f32)
    beta = jax.nn.sigmoid(pa[..., A_QKV + A_VAL:A_QKV + A_VAL + A_HEADS].astype(f32))
    logd = -jnp.exp(p['gdn_a_log'].astype(f32)) * jax.nn.softplus(
        pa[..., A_QKV + A_VAL + A_HEADS:].astype(f32) + p['gdn_dt_bias'].astype(f32))
    o_a, s_gdn_new = gated_delta_chunked(q, k, v, beta, logd, s_gdn)
    o_a = rms_norm(o_a, p['gdn_norm_w']).reshape(bsz, L, A_VAL) * jax.nn.silu(z_a)
    prev_row = (x_prev.astype(xn.dtype) @ p['w_in0'][:, A_COLS:])[:, None]
    pb_prev = jnp.concatenate([prev_row, pb[:, :-1]], axis=1)
    pb = (pb + (pb_prev - pb) * p['rwkv_mu']).astype(f32)
    r, kb, vb, gb = (pb[..., i * B_W:(i + 1) * B_W] for i in range(4))
    w_lo = pb[..., 4 * B_W:4 * B_W + B_LORA_W]
    a_lo = pb[..., 4 * B_W + B_LORA_W:]
    logw = -B_DECAY_SCALE * jax.nn.sigmoid(p['rwkv_w0'] + jnp.tanh(w_lo) @ p['rwkv_w2'])
    a = jax.nn.sigmoid(p['rwkv_a0'] + a_lo @ p['rwkv_a2'])
    heads = lambda t: t.reshape(bsz, L, B_HEADS, B_N)
    kk = l2_normalize(heads(kb * p['rwkv_k_k']))
    kb = kb * (1.0 + (a - 1.0) * p['rwkv_k_a'])
    rh, kh, vh = heads(r), heads(kb), heads(vb)
    o_b, s_rwkv_new = rwkv7_scan(rh, heads(jnp.exp(logw)), kh, vh, -kk, kk * heads(a), s_rwkv)
    gmean = jnp.mean(o_b, axis=-1, keepdims=True)
    gvar = jnp.mean(jnp.square(o_b - gmean), axis=-1, keepdims=True)
    o_b = ((o_b - gmean) * lax.rsqrt(gvar + B_GN_EPS)).reshape(bsz, L, B_W)
    bonus = (jnp.sum(rh * kh * p['rwkv_r_k'], axis=-1, keepdims=True) * vh).reshape(bsz, L, B_W)
    o_b = (o_b * p['rwkv_ln_w'] + p['rwkv_ln_b'] + bonus) * jax.nn.silu(gb)
    y = jnp.concatenate([o_a, o_b], axis=-1).astype(x.dtype) @ p['w_out0']
    return x + y, conv_new, s_gdn_new, xn[:, -1], s_rwkv_new


def mlstm_layer(x, c0, n0, m0, p):
    bsz, L, _ = x.shape
    xn = rms_norm(x, p['norm1_w'])
    proj = (xn @ p['w_in1']).astype(jnp.float32)
    q = proj[..., :C_QK].reshape(bsz, L, C_HEADS, C_DQK) * (C_DQK ** -0.5)
    k = proj[..., C_QK:2 * C_QK].reshape(bsz, L, C_HEADS, C_DQK)
    off = 2 * C_QK
    v = proj[..., off:off + C_V].reshape(bsz, L, C_HEADS, C_DV)
    o_gate = proj[..., off + C_V:off + 2 * C_V]
    z = proj[..., off + 2 * C_V:off + 3 * C_V]
    gi = proj[..., off + 3 * C_V:off + 3 * C_V + C_HEADS] + p['mlstm_b_i']
    gf = proj[..., off + 3 * C_V + C_HEADS:] + p['mlstm_b_f']
    logi = soft_cap(gi)
    logf = jax.nn.log_sigmoid(soft_cap(gf))
    h, cm, nm, m = mlstm_chunked(q, k, v, logi, logf, c0, n0, m0)
    h = h.reshape(bsz, L, C_V) * jax.nn.sigmoid(o_gate) * jax.nn.silu(z)
    return x + h.astype(x.dtype) @ p['w_out1'], cm, nm, m


def trunk(x, state, p):
    conv_buf, s_gdn, x_prev, s_rwkv, c0, n0, m0 = state
    for layer in range(DEPTH):
        if layer % 2 == 0:
            x, conv_buf, s_gdn, x_prev, s_rwkv = mixer_ab_layer(x, conv_buf, s_gdn, x_prev, s_rwkv, p)
        else:
            x, c0, n0, m0 = mlstm_layer(x, c0, n0, m0, p)
    return rms_norm(x, p['norm_f_w']), (conv_buf, s_gdn, x_prev, s_rwkv, c0, n0, m0)


def setup_inputs(seed: int = 0) -> dict:
    key = jax.random.key(seed)
    ks = jax.random.split(key, 32)
    f32 = jnp.float32
    nrm = lambda i, shape, scale: scale * jax.random.normal(ks[i], shape, f32)
    dt = jnp.exp(jax.random.uniform(ks[14], (A_HEADS,), f32, math.log(1e-3), math.log(1e-1)))
    return {
        'x_prompt': nrm(0, (BATCH, SEQ, D_MODEL), 1.0),
        'x_sample': nrm(1, (DEC_BATCH, DEC_SEQ, D_MODEL), 1.0),
        'state_gdn_conv': nrm(2, (DEC_BATCH, A_CONV - 1, A_QKV), 1.0),
        'state_gdn': nrm(3, (DEC_BATCH, A_HEADS, A_DK, A_DV), 0.5),
        'state_rwkv_shift': nrm(4, (DEC_BATCH, D_MODEL), 1.0),
        'state_rwkv': nrm(5, (DEC_BATCH, B_HEADS, B_N, B_N), 0.3),
        'state_mlstm_c': nrm(6, (DEC_BATCH, C_HEADS, C_DQK, C_DV), 0.3),
        'state_mlstm_n': nrm(7, (DEC_BATCH, C_HEADS, C_DQK), 0.3),
        'state_mlstm_m': nrm(8, (DEC_BATCH, C_HEADS), 0.5),
        'norm0_w': 1.0 + nrm(9, (D_MODEL,), 0.02),
        'w_in0': nrm(10, (D_MODEL, P0), D_MODEL ** -0.5),
        'w_out0': nrm(11, (MIX0, D_MODEL), MIX0 ** -0.5),
        'gdn_conv_w': nrm(12, (A_CONV, A_QKV), A_CONV ** -0.5),
        'gdn_a_log': jnp.log(jax.random.uniform(ks[13], (A_HEADS,), f32, 1.0, 16.0)),
        'gdn_dt_bias': dt + jnp.log(-jnp.expm1(-dt)),
        'gdn_norm_w': 1.0 + nrm(15, (A_DV,), 0.02),
        'rwkv_mu': jax.random.uniform(ks[16], (B_COLS,), f32, 0.0, 1.0),
        'rwkv_w0': nrm(17, (B_W,), 0.5),
        'rwkv_w2': nrm(18, (B_LORA_W, B_W), 0.5 * B_LORA_W ** -0.5),
        'rwkv_a0': nrm(19, (B_W,), 0.1),
        'rwkv_a2': nrm(20, (B_LORA_A, B_W), 0.5 * B_LORA_A ** -0.5),
        'rwkv_k_k': 0.85 + nrm(21, (B_W,), 0.05),
        'rwkv_k_a': 1.0 + nrm(22, (B_W,), 0.05),
        'rwkv_r_k': nrm(23, (B_HEADS, B_N), 0.1),
        'rwkv_ln_w': 1.0 + nrm(24, (B_W,), 0.02),
        'rwkv_ln_b': nrm(25, (B_W,), 0.01),
        'norm1_w': 1.0 + nrm(26, (D_MODEL,), 0.02),
        'w_in1': nrm(27, (D_MODEL, P1), D_MODEL ** -0.5),
        'w_out1': nrm(28, (MIX1, D_MODEL), MIX1 ** -0.5),
        'mlstm_b_i': nrm(29, (C_HEADS,), 0.1),
        'mlstm_b_f': jnp.linspace(3.0, 6.0, C_HEADS, dtype=f32) + nrm(30, (C_HEADS,), 0.1),
        'norm_f_w': 1.0 + nrm(31, (D_MODEL,), 0.02),
    }


def reference(x_prompt, x_sample, state_gdn_conv, state_gdn, state_rwkv_shift, state_rwkv,
              state_mlstm_c, state_mlstm_n, state_mlstm_m,
              norm0_w, w_in0, w_out0, gdn_conv_w, gdn_a_log, gdn_dt_bias, gdn_norm_w,
              rwkv_mu, rwkv_w0, rwkv_w2, rwkv_a0, rwkv_a2, rwkv_k_k, rwkv_k_a, rwkv_r_k,
              rwkv_ln_w, rwkv_ln_b, norm1_w, w_in1, w_out1, mlstm_b_i, mlstm_b_f, norm_f_w):
    p = {
        'norm0_w': norm0_w, 'w_in0': w_in0, 'w_out0': w_out0, 'gdn_conv_w': gdn_conv_w,
        'gdn_a_log': gdn_a_log, 'gdn_dt_bias': gdn_dt_bias, 'gdn_norm_w': gdn_norm_w,
        'rwkv_mu': rwkv_mu, 'rwkv_w0': rwkv_w0, 'rwkv_w2': rwkv_w2, 'rwkv_a0': rwkv_a0,
        'rwkv_a2': rwkv_a2, 'rwkv_k_k': rwkv_k_k, 'rwkv_k_a': rwkv_k_a, 'rwkv_r_k': rwkv_r_k,
        'rwkv_ln_w': rwkv_ln_w, 'rwkv_ln_b': rwkv_ln_b, 'norm1_w': norm1_w, 'w_in1': w_in1,
        'w_out1': w_out1, 'mlstm_b_i': mlstm_b_i, 'mlstm_b_f': mlstm_b_f, 'norm_f_w': norm_f_w,
    }
    f32 = jnp.float32
    bsz = x_prompt.shape[0]
    prompt_state = (
        jnp.zeros((bsz, A_CONV - 1, A_QKV), x_prompt.dtype),
        jnp.zeros((bsz, A_HEADS, A_DK, A_DV), f32),
        jnp.zeros((bsz, D_MODEL), x_prompt.dtype),
        jnp.zeros((bsz, B_HEADS, B_N, B_N), f32),
        jnp.zeros((bsz, C_HEADS, C_DQK, C_DV), f32),
        jnp.zeros((bsz, C_HEADS, C_DQK), f32),
        jnp.zeros((bsz, C_HEADS), f32),
    )
    y_prompt, (p_conv, p_gdn, p_shift, p_rwkv, p_c, p_n, p_m) = trunk(x_prompt, prompt_state, p)
    sample_state = (state_gdn_conv, state_gdn, state_rwkv_shift, state_rwkv,
                    state_mlstm_c, state_mlstm_n, state_mlstm_m)
    y_sample, (s_conv, s_gdn, s_shift, s_rwkv, s_c, s_n, s_m) = trunk(x_sample, sample_state, p)
    return (y_prompt, y_sample, p_conv, p_gdn, p_shift, p_rwkv, p_c, p_n, p_m,
            s_conv, s_gdn, s_shift, s_rwkv, s_c, s_n, s_m)
```

```python
import functools
import math

import jax
import jax.numpy as jnp
from jax import lax
from jax.experimental import pallas as pl
from jax.experimental.pallas import tpu as pltpu

F32 = jnp.float32
BF16 = jnp.bfloat16

D_MODEL = 1024
EPS = 1e-6
LANES = 128
SUBLANES = 8
A_HEADS, A_DK, A_DV, A_CONV = 4, 128, 128, 4
A_KEY = A_HEADS * A_DK
A_VAL = A_HEADS * A_DV
A_QKV = 2 * A_KEY + A_VAL
A_MAIN = A_QKV + A_VAL
B_HEADS, B_N, B_LORA = 8, 64, 64
B_W = B_HEADS * B_N
B_COLS = 4 * B_W + 2 * B_LORA
B_PAIRS = B_HEADS // 2
B_GN_EPS = 64e-5
B_DECAY_SCALE = math.exp(-0.5)
C_HEADS, C_DQK, C_DV = 8, 64, 128
C_QK = C_HEADS * C_DQK
C_V = C_HEADS * C_DV
C_GATE_CAP = 15.0
P0_GATES = A_MAIN
P0_B = A_MAIN + LANES
P0_PAD = P0_B + B_COLS
P1_GATES = 2 * C_QK + 3 * C_V
P1_PAD = P1_GATES + LANES
MAX_CHUNK = 64
VMEM_LIMIT = 56 * 1024 * 1024


def _dot(a, b):
    return jnp.dot(a.astype(BF16), b.astype(BF16), preferred_element_type=F32)


def _dot_nt(a, b):
    return lax.dot_general(a.astype(BF16), b.astype(BF16), (((1,), (1,)), ((), ())),
                           preferred_element_type=F32)


def _dot_tn(a, b):
    return lax.dot_general(a.astype(BF16), b.astype(BF16), (((0,), (0,)), ((), ())),
                           preferred_element_type=F32)


def _split3(x):
    hi = x.astype(BF16)
    r1 = x - hi.astype(F32)
    mid = r1.astype(BF16)
    lo = (r1 - mid.astype(F32)).astype(BF16)
    return hi, mid, lo


def _dot_exact_lhs(sel, x):
    hi, mid, lo = _split3(x)
    s = sel.astype(BF16)
    return jnp.dot(jnp.concatenate([s, s, s], axis=1), jnp.concatenate([hi, mid, lo], axis=0),
                   preferred_element_type=F32)


def _dot_exact_rhs(x, sel):
    hi, mid, lo = _split3(x)
    s = sel.astype(BF16)
    return jnp.dot(jnp.concatenate([hi, mid, lo], axis=1), jnp.concatenate([s, s, s], axis=0),
                   preferred_element_type=F32)


def _iota2(shape, dim):
    return lax.broadcasted_iota(jnp.int32, shape, dim)


def _tri_incl(n):
    return (_iota2((n, n), 0) >= _iota2((n, n), 1))


def _col_to_row(col):
    n = col.shape[0]
    eye = _iota2((n, n), 0) == _iota2((n, n), 1)
    return jnp.sum(jnp.where(eye, col, 0.0), axis=0, keepdims=True)


def _row_to_col(row):
    n = row.shape[1]
    eye = _iota2((n, n), 0) == _iota2((n, n), 1)
    return jnp.sum(jnp.where(eye, row, 0.0), axis=1, keepdims=True)


def _inv_unit_lower(p, levels):
    n = p.shape[0]
    eye = (_iota2((n, n), 0) == _iota2((n, n), 1)).astype(F32)
    t = eye + p
    m = 2
    while m < levels:
        p = _dot(p, p)
        t = t + _dot(t, p)
        m *= 2
    return t


def _sigmoid(x):
    return 1.0 / (1.0 + jnp.exp(-x))


def _silu(x):
    return x * _sigmoid(x)


def _softplus(x):
    return jnp.maximum(x, 0.0) + jnp.log1p(jnp.exp(-jnp.abs(x)))


def _rms(x, w):
    return x * lax.rsqrt(jnp.mean(x * x, axis=-1, keepdims=True) + EPS) * w


def _chunk_loop(n_chunks, body):
    if n_chunks == 1:
        body(0)
    else:
        def step(c, carry):
            body(c)
            return carry
        lax.fori_loop(0, n_chunks, step, 0)


def _row_start(c, chunk):
    if isinstance(c, int):
        return c * chunk
    return pl.multiple_of(c * chunk, chunk)


def _norm_proj_kernel(x_ref, nw_ref, w_ref, o_ref):
    xn = _rms(x_ref[...], nw_ref[...])
    o_ref[...] = jnp.dot(xn.astype(BF16), w_ref[...], preferred_element_type=F32)


def _norm_proj(x2d, norm_w, w_bf16, tm):
    m = x2d.shape[0]
    n = w_bf16.shape[1]
    return pl.pallas_call(
        _norm_proj_kernel,
        grid=(m // tm,),
        in_specs=[pl.BlockSpec((tm, D_MODEL), lambda i: (i, 0)),
                  pl.BlockSpec((1, D_MODEL), lambda i: (0, 0)),
                  pl.BlockSpec((D_MODEL, n), lambda i: (0, 0))],
        out_specs=pl.BlockSpec((tm, n), lambda i: (i, 0)),
        out_shape=jax.ShapeDtypeStruct((m, n), F32),
        compiler_params=pltpu.CompilerParams(dimension_semantics=("parallel",),
                                             vmem_limit_bytes=VMEM_LIMIT),
        name="norm_proj0",
    )(x2d, norm_w, w_bf16)


def _mid_kernel(x_ref, oa_ref, ob_ref, woa_ref, wob_ref, nw_ref, wi_ref, x1_ref, p_ref):
    y = jnp.dot(oa_ref[...].astype(BF16), woa_ref[...], preferred_element_type=F32)
    y = y + jnp.dot(ob_ref[...].astype(BF16), wob_ref[...], preferred_element_type=F32)
    x1 = x_ref[...] + y
    x1_ref[...] = x1
    xn = _rms(x1, nw_ref[...])
    p_ref[...] = jnp.dot(xn.astype(BF16), wi_ref[...], preferred_element_type=F32)


def _mid(x2d, oa, ob, woa, wob, norm_w, wi, tm):
    m = x2d.shape[0]
    n = wi.shape[1]
    row = lambda i: (i, 0)
    fix = lambda i: (0, 0)
    return pl.pallas_call(
        _mid_kernel,
        grid=(m // tm,),
        in_specs=[pl.BlockSpec((tm, D_MODEL), row),
                  pl.BlockSpec((tm, A_VAL), row),
                  pl.BlockSpec((tm, B_W), row),
                  pl.BlockSpec((A_VAL, D_MODEL), fix),
                  pl.BlockSpec((B_W, D_MODEL), fix),
                  pl.BlockSpec((1, D_MODEL), fix),
                  pl.BlockSpec((D_MODEL, n), fix)],
        out_specs=[pl.BlockSpec((tm, D_MODEL), row), pl.BlockSpec((tm, n), row)],
        out_shape=[jax.ShapeDtypeStruct((m, D_MODEL), F32), jax.ShapeDtypeStruct((m, n), F32)],
        compiler_params=pltpu.CompilerParams(dimension_semantics=("parallel",),
                                             vmem_limit_bytes=VMEM_LIMIT),
        name="out0_norm_proj1",
    )(x2d, oa, ob, woa, wob, norm_w, wi)


def _final_kernel(x_ref, h_ref, wo_ref, nw_ref, y_ref):
    x2 = x_ref[...] + jnp.dot(h_ref[...].astype(BF16), wo_ref[...], preferred_element_type=F32)
    y_ref[...] = _rms(x2, nw_ref[...])


def _final(x2d, h, wo, norm_w, tm):
    m = x2d.shape[0]
    row = lambda i: (i, 0)
    fix = lambda i: (0, 0)
    return pl.pallas_call(
        _final_kernel,
        grid=(m // tm,),
        in_specs=[pl.BlockSpec((tm, D_MODEL), row),
                  pl.BlockSpec((tm, C_V), row),
                  pl.BlockSpec((C_V, D_MODEL), fix),
                  pl.BlockSpec((1, D_MODEL), fix)],
        out_specs=pl.BlockSpec((tm, D_MODEL), row),
        out_shape=jax.ShapeDtypeStruct((m, D_MODEL), F32),
        compiler_params=pltpu.CompilerParams(dimension_semantics=("parallel",),
                                             vmem_limit_bytes=VMEM_LIMIT),
        name="out1_norm_f",
    )(x2d, h, wo, norm_w)


def _aux_kernel(xl_ref, xp_ref, nw_ref, wb_ref, xn_ref, prev_ref):
    xn_ref[...] = _rms(xl_ref[...], nw_ref[...])
    prev_ref[...] = jnp.dot(xp_ref[...].astype(BF16), wb_ref[...], preferred_element_type=F32)


def _aux(x_last, x_prev, norm_w, wb):
    b = x_last.shape[0]
    return pl.pallas_call(
        _aux_kernel,
        out_shape=[jax.ShapeDtypeStruct((b, D_MODEL), F32), jax.ShapeDtypeStruct((b, B_COLS), F32)],
        compiler_params=pltpu.CompilerParams(vmem_limit_bytes=VMEM_LIMIT),
        name="shift_rows",
    )(x_last, x_prev, norm_w, wb)


def _gdn_kernel(main_ref, gates_ref, conv0_ref, s0_ref, convw_ref, alog_ref, dtb_ref, gnw_ref,
                o_ref, convn_ref, sout_ref, ext_s, qkv_s, gsc_s, st_s, *, tb, chunk):
    i = pl.program_id(1)
    n_chunks = tb // chunk
    tail = SUBLANES

    @pl.when(i == 0)
    def _():
        ext_s[0:tail, :] = jnp.zeros((tail, A_QKV), F32)
        ext_s[tail - (A_CONV - 1):tail, :] = conv0_ref[0]
        st_s[...] = s0_ref[0]

    @pl.when(i > 0)
    def _():
        ext_s[0:tail, :] = ext_s[tb:tb + tail, :]

    ext_s[tail:tail + tb, :] = main_ref[:, 0:A_QKV]
    cw = convw_ref[...]
    for c in range(n_chunks):
        r0 = c * chunk
        y = ext_s[tail + r0:tail + r0 + chunk, :] * cw[A_CONV - 1:A_CONV, :]
        for j in range(A_CONV - 1):
            sh = A_CONV - 1 - j
            y = y + ext_s[tail + r0 - sh:tail + r0 - sh + chunk, :] * cw[j:j + 1, :]
        y = _silu(y)
        for h in range(A_HEADS):
            q = y[:, h * A_DK:(h + 1) * A_DK]
            qkv_s[r0:r0 + chunk, h * A_DK:(h + 1) * A_DK] = (
                q * lax.rsqrt(jnp.sum(q * q, axis=-1, keepdims=True) + EPS) * (A_DK ** -0.5))
            k = y[:, A_KEY + h * A_DK:A_KEY + (h + 1) * A_DK]
            qkv_s[r0:r0 + chunk, A_KEY + h * A_DK:A_KEY + (h + 1) * A_DK] = (
                k * lax.rsqrt(jnp.sum(k * k, axis=-1, keepdims=True) + EPS))
        qkv_s[r0:r0 + chunk, 2 * A_KEY:A_QKV] = y[:, 2 * A_KEY:A_QKV]
    g = gates_ref[...]
    gsc_s[:, 0:LANES] = _sigmoid(g)
    gsc_s[:, LANES:2 * LANES] = -jnp.exp(alog_ref[...]) * _softplus(g + dtb_ref[...])

    tri = _tri_incl(chunk)
    strict = _iota2((chunk, chunk), 0) > _iota2((chunk, chunk), 1)
    gnw = gnw_ref[...]

    def body(c):
        r0 = _row_start(c, chunk)
        rows = pl.ds(r0, chunk)
        beta = gsc_s[rows, 0:LANES]
        gcum = _dot_exact_lhs(tri, gsc_s[rows, LANES:2 * LANES])
        for h in range(A_HEADS):
            q = qkv_s[rows, h * A_DK:(h + 1) * A_DK]
            k = qkv_s[rows, A_KEY + h * A_DK:A_KEY + (h + 1) * A_DK]
            v = qkv_s[rows, 2 * A_KEY + h * A_DV:2 * A_KEY + (h + 1) * A_DV]
            b_col = beta[:, h:h + 1]
            g_col = gcum[:, A_HEADS + h:A_HEADS + h + 1]
            g_row = _col_to_row(g_col)
            g_last = g_col[chunk - 1:chunk, :]
            decay = jnp.where(tri, jnp.exp(jnp.where(tri, g_col - g_row, 0.0)), 0.0)
            eg = jnp.exp(g_col)
            amat = jnp.where(strict, b_col * decay * _dot_nt(k, k), 0.0)
            t_inv = _inv_unit_lower(-amat, chunk)
            u = _dot(t_inv, b_col * v)
            w = _dot(t_inv, (b_col * eg) * k)
            attn = decay * _dot_nt(q, k)
            s = st_s[h]
            nv = u - _dot(w, s)
            o = _dot(q * eg, s) + _dot(attn, nv)
            st_s[h] = jnp.exp(g_last) * s + _dot_tn(k * jnp.exp(g_last - g_col), nv)
            z = main_ref[rows, A_QKV + h * A_DV:A_QKV + (h + 1) * A_DV]
            o_ref[rows, h * A_DV:(h + 1) * A_DV] = _rms(o, gnw) * _silu(z)

    _chunk_loop(n_chunks, body)

    @pl.when(i == pl.num_programs(1) - 1)
    def _():
        convn_ref[0] = ext_s[tb + tail - (A_CONV - 1):tb + tail, :]
        sout_ref[0] = st_s[...]


def _gdn(proj2d, conv0, s0, conv_w, alog_row, dtb_row, gn_w, bsz, seq, tb, chunk):
    nblk = seq // tb
    row = lambda b, i: (b * nblk + i, 0)
    fix2 = lambda b, i: (0, 0)
    kern = functools.partial(_gdn_kernel, tb=tb, chunk=chunk)
    return pl.pallas_call(
        kern,
        grid=(bsz, nblk),
        in_specs=[pl.BlockSpec((tb, A_MAIN), row),
                  pl.BlockSpec((tb, LANES), lambda b, i: (b * nblk + i, P0_GATES // LANES)),
                  pl.BlockSpec((1, A_CONV - 1, A_QKV), lambda b, i: (b, 0, 0)),
                  pl.BlockSpec((1, A_HEADS, A_DK, A_DV), lambda b, i: (b, 0, 0, 0)),
                  pl.BlockSpec((A_CONV, A_QKV), fix2),
                  pl.BlockSpec((1, LANES), fix2),
                  pl.BlockSpec((1, LANES), fix2),
                  pl.BlockSpec((1, A_DV), fix2)],
        out_specs=[pl.BlockSpec((tb, A_VAL), row),
                   pl.BlockSpec((1, A_CONV - 1, A_QKV), lambda b, i: (b, 0, 0)),
                   pl.BlockSpec((1, A_HEADS, A_DK, A_DV), lambda b, i: (b, 0, 0, 0))],
        out_shape=[jax.ShapeDtypeStruct((bsz * seq, A_VAL), F32),
                   jax.ShapeDtypeStruct((bsz, A_CONV - 1, A_QKV), F32),
                   jax.ShapeDtypeStruct((bsz, A_HEADS, A_DK, A_DV), F32)],
        scratch_shapes=[pltpu.VMEM((tb + SUBLANES, A_QKV), F32),
                        pltpu.VMEM((tb, A_QKV), F32),
                        pltpu.VMEM((tb, 2 * LANES), F32),
                        pltpu.VMEM((A_HEADS, A_DK, A_DV), F32)],
        compiler_params=pltpu.CompilerParams(dimension_semantics=("parallel", "arbitrary"),
                                             vmem_limit_bytes=VMEM_LIMIT),
        name="gdn_mixer",
    )(proj2d, proj2d, conv0, s0, conv_w, alog_row, dtb_row, gn_w)


_FR, _FLW, _FK, _FV, _FA, _FB, _FG = range(7)


def _rwkv_kernel(pb_ref, prev_ref, s0_ref, mu_ref, w0_ref, a0_ref, w2a2_ref, kk_ref, ka_ref,
                 rk_ref, lnw_ref, lnb_ref, o_ref, sout_ref, ext_s, feat_s, st_s, tmp_s,
                 *, tb, chunk):
    i = pl.program_id(1)
    n_chunks = tb // chunk
    tail = SUBLANES
    lane = _iota2((1, LANES), 1)
    head0 = lane < B_N
    ones_blk = (_iota2((LANES, LANES), 0) // B_N == _iota2((LANES, LANES), 1) // B_N)

    @pl.when(i == 0)
    def _():
        ext_s[0:tail, :] = jnp.zeros((tail, B_COLS), F32)
        ext_s[tail - 1:tail, :] = prev_ref[0]
        for p in range(B_PAIRS):
            tmp_s[...] = jnp.zeros((B_N, LANES), F32)
            tmp_s[:, 0:B_N] = s0_ref[0, 2 * p + 1]
            shifted = pltpu.roll(tmp_s[...], B_N, 1)
            st_s[p, B_N:LANES, :] = jnp.where(head0, 0.0, shifted)
            tmp_s[:, 0:B_N] = s0_ref[0, 2 * p]
            st_s[p, 0:B_N, :] = tmp_s[...]

    @pl.when(i > 0)
    def _():
        ext_s[0:tail, :] = ext_s[tb:tb + tail, :]

    ext_s[tail:tail + tb, :] = pb_ref[...]
    mu = mu_ref[...]
    for c in range(n_chunks):
        r0 = c * chunk
        pb = ext_s[tail + r0:tail + r0 + chunk, :]
        pb_prev = ext_s[tail + r0 - 1:tail + r0 - 1 + chunk, :]
        pbs = pb + (pb_prev - pb) * mu
        r = pbs[:, 0:B_W]
        kb = pbs[:, B_W:2 * B_W]
        vb = pbs[:, 2 * B_W:3 * B_W]
        gb = pbs[:, 3 * B_W:4 * B_W]
        lo = pbs[:, 4 * B_W:B_COLS]
        lo = jnp.where(head0, jnp.tanh(lo), lo)
        wa = _dot(lo, w2a2_ref[...])
        logw = -B_DECAY_SCALE * _sigmoid(w0_ref[...] + wa[:, 0:B_W])
        a = _sigmoid(a0_ref[...] + wa[:, B_W:2 * B_W])
        kkx = kb * kk_ref[...]
        for p in range(B_PAIRS):
            sl = slice(p * LANES, (p + 1) * LANES)
            kp = kkx[:, sl]
            ss = _dot_exact_rhs(kp * kp, ones_blk)
            kkn = kp * lax.rsqrt(ss + EPS)
            feat_s[r0:r0 + chunk, _FA * B_W + p * LANES:_FA * B_W + (p + 1) * LANES] = -kkn
            feat_s[r0:r0 + chunk, _FB * B_W + p * LANES:_FB * B_W + (p + 1) * LANES] = kkn * a[:, sl]
        feat_s[r0:r0 + chunk, _FR * B_W:(_FR + 1) * B_W] = r
        feat_s[r0:r0 + chunk, _FLW * B_W:(_FLW + 1) * B_W] = logw
        feat_s[r0:r0 + chunk, _FK * B_W:(_FK + 1) * B_W] = kb * (1.0 + (a - 1.0) * ka_ref[...])
        feat_s[r0:r0 + chunk, _FV * B_W:(_FV + 1) * B_W] = vb
        feat_s[r0:r0 + chunk, _FG * B_W:(_FG + 1) * B_W] = gb

    tri = _tri_incl(chunk)
    c2 = 2 * chunk
    ri = _iota2((c2, c2), 0)
    ci = _iota2((c2, c2), 1)
    same_head = (ri // chunk) == (ci // chunk)
    incl_bd = same_head & (ri >= ci)
    strict_bd = same_head & (ri > ci)
    lane_st = (_iota2((c2, LANES), 0) // chunk) == (_iota2((c2, LANES), 1) // B_N)

    def body(c):
        r0 = _row_start(c, chunk)
        rows = pl.ds(r0, chunk)
        for p in range(B_PAIRS):
            def feat(slot):
                return feat_s[rows, slot * B_W + p * LANES:slot * B_W + (p + 1) * LANES]
            r, lw, k, v, a_, b_, gate = (feat(s) for s in (_FR, _FLW, _FK, _FV, _FA, _FB, _FG))
            cs = _dot_exact_lhs(tri, lw)
            cs_last = cs[chunk - 1:chunk, :]
            e_inv = jnp.exp(-cs)
            e_rem = jnp.exp(cs_last - cs)
            at = a_ * jnp.exp(cs - lw)
            rt = r * jnp.exp(cs)
            bt = b_ * e_inv
            kt = k * e_inv
            a_st = jnp.concatenate([jnp.where(head0, at, 0.0), jnp.where(head0, 0.0, at)], axis=0)
            r_st = jnp.concatenate([jnp.where(head0, rt, 0.0), jnp.where(head0, 0.0, rt)], axis=0)
            b2 = jnp.concatenate([bt, bt], axis=0)
            k2 = jnp.concatenate([kt, kt], axis=0)
            v2 = jnp.concatenate([v, v], axis=0)
            l_ab = jnp.where(strict_bd, _dot_nt(a_st, b2), 0.0)
            l_ak = jnp.where(strict_bd, _dot_nt(a_st, k2), 0.0)
            m_rb = jnp.where(incl_bd, _dot_nt(r_st, b2), 0.0)
            m_rk = jnp.where(incl_bd, _dot_nt(r_st, k2), 0.0)
            t_inv = _inv_unit_lower(l_ab, chunk)
            s = st_s[p]
            ar = _dot_nt(jnp.concatenate([at, rt], axis=0), s)
            ah = ar[0:chunk, :]
            rh = ar[chunk:c2, :]
            x_st = jnp.where(lane_st, jnp.concatenate([ah, ah], axis=0) + _dot(l_ak, v2), 0.0)
            u_st = _dot(t_inv, x_st)
            o_st = jnp.where(lane_st, _dot(m_rb, u_st) + _dot(m_rk, v2), 0.0)
            u = u_st[0:chunk, :] + u_st[chunk:c2, :]
            o = rh + o_st[0:chunk, :] + o_st[chunk:c2, :]
            upd = _dot_tn(jnp.concatenate([u, v], axis=0),
                          jnp.concatenate([b_ * e_rem, k * e_rem], axis=0))
            st_s[p] = s * jnp.exp(cs_last) + jnp.where(ones_blk, upd, 0.0)
            mean = _dot_exact_rhs(o, ones_blk) * (1.0 / B_N)
            d = o - mean
            var = _dot_exact_rhs(d * d, ones_blk) * (1.0 / B_N)
            sl = slice(p * LANES, (p + 1) * LANES)
            bonus = _dot_exact_rhs(r * k * rk_ref[:, sl], ones_blk) * v
            ob = d * lax.rsqrt(var + B_GN_EPS) * lnw_ref[:, sl] + lnb_ref[:, sl] + bonus
            o_ref[rows, sl] = ob * _silu(gate)

    _chunk_loop(n_chunks, body)

    @pl.when(i == pl.num_programs(1) - 1)
    def _():
        for p in range(B_PAIRS):
            sout_ref[0, 2 * p] = st_s[p, 0:B_N, 0:B_N]
            shifted = pltpu.roll(st_s[p, B_N:LANES, :], B_N, 1)
            sout_ref[0, 2 * p + 1] = shifted[:, 0:B_N]


def _rwkv(proj2d, prev_row, s0, mu, w0, a0, w2a2, k_k, k_a, r_k, ln_w, ln_b, bsz, seq, tb, chunk):
    nblk = seq // tb
    row = lambda b, i: (b * nblk + i, 0)
    fix2 = lambda b, i: (0, 0)
    kern = functools.partial(_rwkv_kernel, tb=tb, chunk=chunk)
    vec = pl.BlockSpec((1, B_W), fix2)
    return pl.pallas_call(
        kern,
        grid=(bsz, nblk),
        in_specs=[pl.BlockSpec((tb, B_COLS), lambda b, i: (b * nblk + i, P0_B // B_COLS)),
                  pl.BlockSpec((1, 1, B_COLS), lambda b, i: (b, 0, 0)),
                  pl.BlockSpec((1, B_HEADS, B_N, B_N), lambda b, i: (b, 0, 0, 0)),
                  pl.BlockSpec((1, B_COLS), fix2),
                  vec, vec,
                  pl.BlockSpec((2 * B_LORA, 2 * B_W), fix2),
                  vec, vec, vec, vec, vec],
        out_specs=[pl.BlockSpec((tb, B_W), row),
                   pl.BlockSpec((1, B_HEADS, B_N, B_N), lambda b, i: (b, 0, 0, 0))],
        out_shape=[jax.ShapeDtypeStruct((bsz * seq, B_W), F32),
                   jax.ShapeDtypeStruct((bsz, B_HEADS, B_N, B_N), F32)],
        scratch_shapes=[pltpu.VMEM((tb + SUBLANES, B_COLS), F32),
                        pltpu.VMEM((tb, 7 * B_W), F32),
                        pltpu.VMEM((B_PAIRS, LANES, LANES), F32),
                        pltpu.VMEM((B_N, LANES), F32)],
        compiler_params=pltpu.CompilerParams(dimension_semantics=("parallel", "arbitrary"),
                                             vmem_limit_bytes=VMEM_LIMIT),
        name="rwkv7_mixer",
    )(proj2d, prev_row, s0, mu, w0, a0, w2a2, k_k, k_a, r_k, ln_w, ln_b)


def _mlstm_kernel(p_ref, c0_ref, n0_ref, m0_ref, bias_ref, o_ref, cout_ref, nout_ref, mout_ref,
                  gsc_s, caug_s, m_s, *, tb, chunk):
    i = pl.program_id(1)
    n_chunks = tb // chunk
    lane = _iota2((1, LANES), 1)
    krow = _iota2((LANES, 1), 0)
    ncol_mask = _iota2((LANES, LANES), 1) == 0

    @pl.when(i == 0)
    def _():
        m0 = m0_ref[0]
        for h in range(C_HEADS):
            j = h % 2
            caug_s[h] = jnp.zeros((LANES, 2 * C_DV), F32)
            caug_s[h, j * C_DQK:(j + 1) * C_DQK, 0:C_DV] = c0_ref[0, h]
            n_col = _row_to_col(n0_ref[0, h // 2:h // 2 + 1, :])
            own = (krow // C_DQK) == j
            caug_s[h, :, C_DV:2 * C_DV] = jnp.where(ncol_mask & own, n_col, 0.0)
            m_s[h:h + 1, :] = jnp.broadcast_to(m0[:, h:h + 1], (1, LANES))

    g = p_ref[:, P1_GATES:P1_PAD] + bias_ref[...]
    cap = C_GATE_CAP * jnp.tanh(g / C_GATE_CAP)
    gsc_s[...] = jnp.where(lane < C_HEADS, cap, -_softplus(-cap))

    tri = _tri_incl(chunk)
    ones_col = jnp.where(_iota2((chunk, C_DV), 1) == 0, 1.0, 0.0)
    off = 2 * C_QK

    def body(c):
        r0 = _row_start(c, chunk)
        rows = pl.ds(r0, chunk)
        gl = gsc_s[rows, :]
        bcum = _dot_exact_lhs(tri, gl)
        for h in range(C_HEADS):
            p, j = h // 2, h % 2
            own_lane = (lane // C_DQK) == j
            own_row = (krow // C_DQK) == j
            qp = p_ref[rows, p * LANES:(p + 1) * LANES] * (C_DQK ** -0.5)
            kp = p_ref[rows, C_QK + p * LANES:C_QK + (p + 1) * LANES]
            v = p_ref[rows, off + h * C_DV:off + (h + 1) * C_DV]
            vaug = jnp.concatenate([v, ones_col], axis=1)
            b_col = bcum[:, C_HEADS + h:C_HEADS + h + 1]
            i_col = gl[:, h:h + 1]
            b_row = _col_to_row(b_col)
            i_row = _col_to_row(i_col)
            dmat = jnp.where(tri, b_col - b_row + i_row, -jnp.inf)
            dmax = jnp.max(dmat, axis=-1, keepdims=True)
            m = m_s[h:h + 1, 0:1]
            inter = b_col + m
            mt = jnp.maximum(inter, dmax)
            pm = jnp.exp(dmat - mt) * _dot_nt(jnp.where(own_lane, qp, 0.0), kp)
            e_inter = jnp.exp(inter - mt)
            cm = caug_s[h]
            num = e_inter * _dot(qp, cm) + _dot(pm, vaug)
            den = num[:, C_DV:C_DV + 1]
            hh = num[:, 0:C_DV] / jnp.maximum(jnp.abs(den), jnp.exp(-mt))
            m_new = mt[chunk - 1:chunk, :]
            b_last = b_col[chunk - 1:chunk, :]
            w_s = jnp.exp(b_last - b_col + i_col - m_new)
            e_c = jnp.exp(b_last + m - m_new)
            caug_s[h] = e_c * cm + jnp.where(own_row, _dot_tn(w_s * kp, vaug), 0.0)
            m_s[h:h + 1, :] = jnp.broadcast_to(m_new, (1, LANES))
            og = p_ref[rows, off + C_V + h * C_DV:off + C_V + (h + 1) * C_DV]
            z = p_ref[rows, off + 2 * C_V + h * C_DV:off + 2 * C_V + (h + 1) * C_DV]
            o_ref[rows, h * C_DV:(h + 1) * C_DV] = hh * _sigmoid(og) * _silu(z)

    _chunk_loop(n_chunks, body)

    @pl.when(i == pl.num_programs(1) - 1)
    def _():
        for h in range(C_HEADS):
            j = h % 2
            cout_ref[0, h] = caug_s[h, j * C_DQK:(j + 1) * C_DQK, 0:C_DV]
        for p in range(C_HEADS // 2):
            n_col = caug_s[2 * p, :, C_DV:C_DV + 1] + caug_s[2 * p + 1, :, C_DV:C_DV + 1]
            nout_ref[0, p:p + 1, :] = _col_to_row(n_col)
        mout_ref[0] = m_s[...]


def _mlstm(proj2d, c0, n0, m0, bias_row, bsz, seq, tb, chunk):
    nblk = seq // tb
    row = lambda b, i: (b * nblk + i, 0)
    kern = functools.partial(_mlstm_kernel, tb=tb, chunk=chunk)
    pairs = C_HEADS // 2
    return pl.pallas_call(
        kern,
        grid=(bsz, nblk),
        in_specs=[pl.BlockSpec((tb, P1_PAD), row),
                  pl.BlockSpec((1, C_HEADS, C_DQK, C_DV), lambda b, i: (b, 0, 0, 0)),
                  pl.BlockSpec((1, pairs, LANES), lambda b, i: (b, 0, 0)),
                  pl.BlockSpec((1, 1, C_HEADS), lambda b, i: (b, 0, 0)),
                  pl.BlockSpec((1, LANES), lambda b, i: (0, 0))],
        out_specs=[pl.BlockSpec((tb, C_V), row),
                   pl.BlockSpec((1, C_HEADS, C_DQK, C_DV), lambda b, i: (b, 0, 0, 0)),
                   pl.BlockSpec((1, pairs, LANES), lambda b, i: (b, 0, 0)),
                   pl.BlockSpec((1, C_HEADS, LANES), lambda b, i: (b, 0, 0))],
        out_shape=[jax.ShapeDtypeStruct((bsz * seq, C_V), F32),
                   jax.ShapeDtypeStruct((bsz, C_HEADS, C_DQK, C_DV), F32),
                   jax.ShapeDtypeStruct((bsz, pairs, LANES), F32),
                   jax.ShapeDtypeStruct((bsz, C_HEADS, LANES), F32)],
        scratch_shapes=[pltpu.VMEM((tb, LANES), F32),
                        pltpu.VMEM((C_HEADS, LANES, 2 * C_DV), F32),
                        pltpu.VMEM((C_HEADS, LANES), F32)],
        compiler_params=pltpu.CompilerParams(dimension_semantics=("parallel", "arbitrary"),
                                             vmem_limit_bytes=VMEM_LIMIT),
        name="mlstm_mixer",
    )(proj2d, c0, n0, m0, bias_row)


def _pad_lanes(vec, offset):
    out = jnp.zeros((1, LANES), F32)
    return out.at[0, offset:offset + vec.shape[0]].set(vec.astype(F32))


def _prepare_params(p):
    w_in0 = p["w_in0"]
    a_cols = A_MAIN + 2 * A_HEADS
    w0 = jnp.zeros((D_MODEL, P0_PAD), F32)
    w0 = w0.at[:, 0:a_cols].set(w_in0[:, 0:a_cols])
    w0 = w0.at[:, P0_B:P0_PAD].set(w_in0[:, a_cols:])
    w_in1 = p["w_in1"]
    w1 = jnp.zeros((D_MODEL, P1_PAD), F32).at[:, 0:w_in1.shape[1]].set(w_in1)
    w2a2 = jnp.zeros((2 * B_LORA, 2 * B_W), F32)
    w2a2 = w2a2.at[0:B_LORA, 0:B_W].set(p["rwkv_w2"]).at[B_LORA:, B_W:].set(p["rwkv_a2"])
    row = lambda v: v.reshape(1, -1).astype(F32)
    return dict(
        w0=w0.astype(BF16), w1=w1.astype(BF16), wb=w_in0[:, a_cols:].astype(BF16),
        woa=p["w_out0"][0:A_VAL].astype(BF16), wob=p["w_out0"][A_VAL:].astype(BF16),
        wo1=p["w_out1"].astype(BF16), w2a2=w2a2.astype(BF16),
        norm0=row(p["norm0_w"]), norm1=row(p["norm1_w"]), normf=row(p["norm_f_w"]),
        conv_w=p["gdn_conv_w"].astype(F32),
        alog=_pad_lanes(p["gdn_a_log"], A_HEADS), dtb=_pad_lanes(p["gdn_dt_bias"], A_HEADS),
        gnw=row(p["gdn_norm_w"]),
        mu=row(p["rwkv_mu"]), rw0=row(p["rwkv_w0"]), ra0=row(p["rwkv_a0"]),
        k_k=row(p["rwkv_k_k"]), k_a=row(p["rwkv_k_a"]), r_k=row(p["rwkv_r_k"]),
        ln_w=row(p["rwkv_ln_w"]), ln_b=row(p["rwkv_ln_b"]),
        gate_bias=_pad_lanes(jnp.concatenate([p["mlstm_b_i"], p["mlstm_b_f"]]), 0),
    )


def _trunk(x, state, q):
    conv0, s_gdn, x_prev, s_rwkv, c0, n0, m0 = state
    bsz, seq, _ = x.shape
    chunk = math.gcd(seq, MAX_CHUNK)
    tb = min(seq, 256)
    m = bsz * seq
    tm = min(m, 256)
    x2d = x.reshape(m, D_MODEL)

    proj0 = _norm_proj(x2d, q["norm0"], q["w0"], tm)
    xn_last, prev_row = _aux(x[:, -1, :], x_prev, q["norm0"], q["wb"])
    o_a, conv_new, s_gdn_new = _gdn(proj0, conv0, s_gdn, q["conv_w"], q["alog"], q["dtb"], q["gnw"],
                                    bsz, seq, tb, chunk)
    o_b, s_rwkv_new = _rwkv(proj0, prev_row.reshape(bsz, 1, B_COLS), s_rwkv, q["mu"], q["rw0"], q["ra0"],
                            q["w2a2"], q["k_k"], q["k_a"], q["r_k"], q["ln_w"], q["ln_b"],
                            bsz, seq, tb, chunk)
    x1, proj1 = _mid(x2d, o_a, o_b, q["woa"], q["wob"], q["norm1"], q["w1"], tm)
    h, c_new, n_new, m_new = _mlstm(proj1, c0, n0.reshape(bsz, C_HEADS // 2, LANES),
                                    m0.reshape(bsz, 1, C_HEADS), q["gate_bias"], bsz, seq, tb, chunk)
    y = _final(x1, h, q["wo1"], q["normf"], tm)
    return y.reshape(bsz, seq, D_MODEL), (conv_new, s_gdn_new, xn_last, s_rwkv_new, c_new,
                                          n_new.reshape(bsz, C_HEADS, C_DQK), m_new[:, :, 0])


def kernel(x_prompt, x_sample, state_gdn_conv, state_gdn, state_rwkv_shift, state_rwkv,
           state_mlstm_c, state_mlstm_n, state_mlstm_m,
           norm0_w, w_in0, w_out0, gdn_conv_w, gdn_a_log, gdn_dt_bias, gdn_norm_w,
           rwkv_mu, rwkv_w0, rwkv_w2, rwkv_a0, rwkv_a2, rwkv_k_k, rwkv_k_a, rwkv_r_k,
           rwkv_ln_w, rwkv_ln_b, norm1_w, w_in1, w_out1, mlstm_b_i, mlstm_b_f, norm_f_w):
    q = _prepare_params(dict(
        norm0_w=norm0_w, w_in0=w_in0, w_out0=w_out0, gdn_conv_w=gdn_conv_w, gdn_a_log=gdn_a_log,
        gdn_dt_bias=gdn_dt_bias, gdn_norm_w=gdn_norm_w, rwkv_mu=rwkv_mu, rwkv_w0=rwkv_w0,
        rwkv_w2=rwkv_w2, rwkv_a0=rwkv_a0, rwkv_a2=rwkv_a2, rwkv_k_k=rwkv_k_k, rwkv_k_a=rwkv_k_a,
        rwkv_r_k=rwkv_r_k, rwkv_ln_w=rwkv_ln_w, rwkv_ln_b=rwkv_ln_b, norm1_w=norm1_w, w_in1=w_in1,
        w_out1=w_out1, mlstm_b_i=mlstm_b_i, mlstm_b_f=mlstm_b_f, norm_f_w=norm_f_w))
    bsz = x_prompt.shape[0]
    prompt_state = (
        jnp.zeros((bsz, A_CONV - 1, A_QKV), F32),
        jnp.zeros((bsz, A_HEADS, A_DK, A_DV), F32),
        jnp.zeros((bsz, D_MODEL), F32),
        jnp.zeros((bsz, B_HEADS, B_N, B_N), F32),
        jnp.zeros((bsz, C_HEADS, C_DQK, C_DV), F32),
        jnp.zeros((bsz, C_HEADS, C_DQK), F32),
        jnp.zeros((bsz, C_HEADS), F32),
    )
    y_p, sp = _trunk(x_prompt, prompt_state, q)
    sample_state = (state_gdn_conv, state_gdn, state_rwkv_shift, state_rwkv,
                    state_mlstm_c, state_mlstm_n, state_mlstm_m)
    y_s, ss = _trunk(x_sample, sample_state, q)
    return (y_p, y_s) + tuple(sp) + tuple(ss)
```

```python
import functools
import math

import jax
import jax.numpy as jnp
from jax import lax
from jax.experimental import pallas as pl
from jax.experimental.pallas import tpu as pltpu

F32 = jnp.float32
BF16 = jnp.bfloat16

D_MODEL = 1024
EPS = 1e-6
LANES = 128
SUBLANES = 8
A_HEADS, A_DK, A_DV, A_CONV = 4, 128, 128, 4
A_KEY = A_HEADS * A_DK
A_VAL = A_HEADS * A_DV
A_QKV = 2 * A_KEY + A_VAL
A_MAIN = A_QKV + A_VAL
B_HEADS, B_N, B_LORA = 8, 64, 64
B_W = B_HEADS * B_N
B_COLS = 4 * B_W + 2 * B_LORA
B_PAIRS = B_HEADS // 2
B_GN_EPS = 64e-5
B_DECAY_SCALE = math.exp(-0.5)
C_HEADS, C_DQK, C_DV = 8, 64, 128
C_QK = C_HEADS * C_DQK
C_V = C_HEADS * C_DV
C_GATE_CAP = 15.0
P0_GATES = A_MAIN
P0_B = A_MAIN + LANES
P0_PAD = P0_B + B_COLS
P1_GATES = 2 * C_QK + 3 * C_V
P1_PAD = P1_GATES + LANES
MAX_CHUNK = 64
VMEM_LIMIT = 56 * 1024 * 1024


def _dot(a, b):
    return jnp.dot(a.astype(BF16), b.astype(BF16), preferred_element_type=F32)


def _dot_nt(a, b):
    return lax.dot_general(a.astype(BF16), b.astype(BF16), (((1,), (1,)), ((), ())),
                           preferred_element_type=F32)


def _dot_tn(a, b):
    return lax.dot_general(a.astype(BF16), b.astype(BF16), (((0,), (0,)), ((), ())),
                           preferred_element_type=F32)


def _split3(x):
    hi = x.astype(BF16)
    r1 = x - hi.astype(F32)
    mid = r1.astype(BF16)
    lo = (r1 - mid.astype(F32)).astype(BF16)
    return hi, mid, lo


def _dot_exact_lhs(sel, x):
    hi, mid, lo = _split3(x)
    s = sel.astype(BF16)
    return jnp.dot(jnp.concatenate([s, s, s], axis=1), jnp.concatenate([hi, mid, lo], axis=0),
                   preferred_element_type=F32)


def _dot_exact_rhs(x, sel):
    hi, mid, lo = _split3(x)
    s = sel.astype(BF16)
    return jnp.dot(jnp.concatenate([hi, mid, lo], axis=1), jnp.concatenate([s, s, s], axis=0),
                   preferred_element_type=F32)


def _iota2(shape, dim):
    return lax.broadcasted_iota(jnp.int32, shape, dim)


def _tri_incl(n):
    return (_iota2((n, n), 0) >= _iota2((n, n), 1))


def _col_to_row(col):
    n = col.shape[0]
    eye = _iota2((n, n), 0) == _iota2((n, n), 1)
    return jnp.sum(jnp.where(eye, col, 0.0), axis=0, keepdims=True)


def _row_to_col(row):
    n = row.shape[1]
    eye = _iota2((n, n), 0) == _iota2((n, n), 1)
    return jnp.sum(jnp.where(eye, row, 0.0), axis=1, keepdims=True)


def _inv_unit_lower(p, levels):
    n = p.shape[0]
    eye = (_iota2((n, n), 0) == _iota2((n, n), 1)).astype(F32)
    t = eye + p
    m = 2
    if m < levels:
        p = _dot(p, p)
        yield
    while m < levels:
        m *= 2
        if m < levels:
            both = _dot(jnp.concatenate([p, t], axis=0), p)
            p, t = both[0:n, :], t + both[n:2 * n, :]
            yield
        else:
            t = t + _dot(t, p)
    return t


def _interleave(chains):
    live = list(chains)
    while live:
        nxt = []
        for g in live:
            try:
                next(g)
                nxt.append(g)
            except StopIteration:
                pass
        live = nxt


def _sigmoid(x):
    return 1.0 / (1.0 + jnp.exp(-x))


def _silu(x):
    return x * _sigmoid(x)


def _softplus(x):
    return jnp.maximum(x, 0.0) + jnp.log1p(jnp.exp(-jnp.abs(x)))


def _rms(x, w):
    return x * lax.rsqrt(jnp.mean(x * x, axis=-1, keepdims=True) + EPS) * w


def _chunk_loop(n_chunks, body):
    if n_chunks == 1:
        body(0)
    else:
        def step(c, carry):
            body(c)
            return carry
        lax.fori_loop(0, n_chunks, step, 0)


def _row_start(c, chunk):
    if isinstance(c, int):
        return c * chunk
    return pl.multiple_of(c * chunk, chunk)


def _norm_proj_kernel(x_ref, nw_ref, w_ref, o_ref):
    xn = _rms(x_ref[...], nw_ref[...])
    o_ref[...] = jnp.dot(xn.astype(BF16), w_ref[...], preferred_element_type=F32)


def _norm_proj(x2d, norm_w, w_bf16, tm):
    m = x2d.shape[0]
    n = w_bf16.shape[1]
    return pl.pallas_call(
        _norm_proj_kernel,
        grid=(m // tm,),
        in_specs=[pl.BlockSpec((tm, D_MODEL), lambda i: (i, 0)),
                  pl.BlockSpec((1, D_MODEL), lambda i: (0, 0)),
                  pl.BlockSpec((D_MODEL, n), lambda i: (0, 0))],
        out_specs=pl.BlockSpec((tm, n), lambda i: (i, 0)),
        out_shape=jax.ShapeDtypeStruct((m, n), F32),
        compiler_params=pltpu.CompilerParams(dimension_semantics=("parallel",),
                                             vmem_limit_bytes=VMEM_LIMIT),
        name="norm_proj0",
    )(x2d, norm_w, w_bf16)


def _mid_kernel(x_ref, oa_ref, ob_ref, woa_ref, wob_ref, nw_ref, wi_ref, x1_ref, p_ref):
    y = jnp.dot(oa_ref[...].astype(BF16), woa_ref[...], preferred_element_type=F32)
    y = y + jnp.dot(ob_ref[...].astype(BF16), wob_ref[...], preferred_element_type=F32)
    x1 = x_ref[...] + y
    x1_ref[...] = x1
    xn = _rms(x1, nw_ref[...])
    p_ref[...] = jnp.dot(xn.astype(BF16), wi_ref[...], preferred_element_type=F32)


def _mid(x2d, oa, ob, woa, wob, norm_w, wi, tm):
    m = x2d.shape[0]
    n = wi.shape[1]
    row = lambda i: (i, 0)
    fix = lambda i: (0, 0)
    return pl.pallas_call(
        _mid_kernel,
        grid=(m // tm,),
        in_specs=[pl.BlockSpec((tm, D_MODEL), row),
                  pl.BlockSpec((tm, A_VAL), row),
                  pl.BlockSpec((tm, B_W), row),
                  pl.BlockSpec((A_VAL, D_MODEL), fix),
                  pl.BlockSpec((B_W, D_MODEL), fix),
                  pl.BlockSpec((1, D_MODEL), fix),
                  pl.BlockSpec((D_MODEL, n), fix)],
        out_specs=[pl.BlockSpec((tm, D_MODEL), row), pl.BlockSpec((tm, n), row)],
        out_shape=[jax.ShapeDtypeStruct((m, D_MODEL), F32), jax.ShapeDtypeStruct((m, n), F32)],
        compiler_params=pltpu.CompilerParams(dimension_semantics=("parallel",),
                                             vmem_limit_bytes=VMEM_LIMIT),
        name="out0_norm_proj1",
    )(x2d, oa, ob, woa, wob, norm_w, wi)


def _final_kernel(x_ref, h_ref, wo_ref, nw_ref, y_ref):
    x2 = x_ref[...] + jnp.dot(h_ref[...].astype(BF16), wo_ref[...], preferred_element_type=F32)
    y_ref[...] = _rms(x2, nw_ref[...])


def _final(x2d, h, wo, norm_w, tm):
    m = x2d.shape[0]
    row = lambda i: (i, 0)
    fix = lambda i: (0, 0)
    return pl.pallas_call(
        _final_kernel,
        grid=(m // tm,),
        in_specs=[pl.BlockSpec((tm, D_MODEL), row),
                  pl.BlockSpec((tm, C_V), row),
                  pl.BlockSpec((C_V, D_MODEL), fix),
                  pl.BlockSpec((1, D_MODEL), fix)],
        out_specs=pl.BlockSpec((tm, D_MODEL), row),
        out_shape=jax.ShapeDtypeStruct((m, D_MODEL), F32),
        compiler_params=pltpu.CompilerParams(dimension_semantics=("parallel",),
                                             vmem_limit_bytes=VMEM_LIMIT),
        name="out1_norm_f",
    )(x2d, h, wo, norm_w)


def _aux_kernel(xl_ref, xp_ref, nw_ref, wb_ref, xn_ref, prev_ref):
    xn_ref[...] = _rms(xl_ref[...], nw_ref[...])
    prev_ref[...] = jnp.dot(xp_ref[...].astype(BF16), wb_ref[...], preferred_element_type=F32)


def _aux(x_last, x_prev, norm_w, wb):
    b = x_last.shape[0]
    return pl.pallas_call(
        _aux_kernel,
        out_shape=[jax.ShapeDtypeStruct((b, D_MODEL), F32), jax.ShapeDtypeStruct((b, B_COLS), F32)],
        compiler_params=pltpu.CompilerParams(vmem_limit_bytes=VMEM_LIMIT),
        name="shift_rows",
    )(x_last, x_prev, norm_w, wb)


def _gdn_kernel(main_ref, gates_ref, conv0_ref, s0_ref, convw_ref, alog_ref, dtb_ref, gnw_ref,
                o_ref, convn_ref, sout_ref, ext_s, qkv_s, gsc_s, st_s, *, tb, chunk):
    i = pl.program_id(1)
    n_chunks = tb // chunk
    tail = SUBLANES

    @pl.when(i == 0)
    def _():
        ext_s[0:tail, :] = jnp.zeros((tail, A_QKV), F32)
        ext_s[tail - (A_CONV - 1):tail, :] = conv0_ref[0]
        st_s[...] = s0_ref[0]

    @pl.when(i > 0)
    def _():
        ext_s[0:tail, :] = ext_s[tb:tb + tail, :]

    ext_s[tail:tail + tb, :] = main_ref[:, 0:A_QKV]
    cw = convw_ref[...]
    for c in range(n_chunks):
        r0 = c * chunk
        y = ext_s[tail + r0:tail + r0 + chunk, :] * cw[A_CONV - 1:A_CONV, :]
        for j in range(A_CONV - 1):
            sh = A_CONV - 1 - j
            y = y + ext_s[tail + r0 - sh:tail + r0 - sh + chunk, :] * cw[j:j + 1, :]
        y = _silu(y)
        for h in range(A_HEADS):
            q = y[:, h * A_DK:(h + 1) * A_DK]
            qkv_s[r0:r0 + chunk, h * A_DK:(h + 1) * A_DK] = (
                q * lax.rsqrt(jnp.sum(q * q, axis=-1, keepdims=True) + EPS) * (A_DK ** -0.5))
            k = y[:, A_KEY + h * A_DK:A_KEY + (h + 1) * A_DK]
            qkv_s[r0:r0 + chunk, A_KEY + h * A_DK:A_KEY + (h + 1) * A_DK] = (
                k * lax.rsqrt(jnp.sum(k * k, axis=-1, keepdims=True) + EPS))
        qkv_s[r0:r0 + chunk, 2 * A_KEY:A_QKV] = y[:, 2 * A_KEY:A_QKV]
    g = gates_ref[...]
    gsc_s[:, 0:LANES] = _sigmoid(g)
    gsc_s[:, LANES:2 * LANES] = -jnp.exp(alog_ref[...]) * _softplus(g + dtb_ref[...])

    tri = _tri_incl(chunk)
    strict = _iota2((chunk, chunk), 0) > _iota2((chunk, chunk), 1)
    gnw = gnw_ref[...]

    def body(c):
        r0 = _row_start(c, chunk)
        rows = pl.ds(r0, chunk)
        beta = gsc_s[rows, 0:LANES]
        gcum = _dot_exact_lhs(tri, gsc_s[rows, LANES:2 * LANES])
        def head_chain(h):
            q = qkv_s[rows, h * A_DK:(h + 1) * A_DK]
            k = qkv_s[rows, A_KEY + h * A_DK:A_KEY + (h + 1) * A_DK]
            v = qkv_s[rows, 2 * A_KEY + h * A_DV:2 * A_KEY + (h + 1) * A_DV]
            b_col = beta[:, h:h + 1]
            g_col = gcum[:, A_HEADS + h:A_HEADS + h + 1]
            g_row = _col_to_row(g_col)
            g_last = g_col[chunk - 1:chunk, :]
            decay = jnp.where(tri, jnp.exp(jnp.where(tri, g_col - g_row, 0.0)), 0.0)
            eg = jnp.exp(g_col)
            qk_k = _dot_nt(jnp.concatenate([q, k], axis=0), k)
            s = st_s[h]
            qs = _dot(q * eg, s)
            yield
            attn = decay * qk_k[0:chunk, :]
            amat = jnp.where(strict, b_col * decay * qk_k[chunk:2 * chunk, :], 0.0)
            t_inv = yield from _inv_unit_lower(-amat, chunk)
            sol = _dot(t_inv, jnp.concatenate([b_col * v, (b_col * eg) * k], axis=1))
            yield
            nv = sol[:, 0:A_DV] - _dot(sol[:, A_DV:A_DV + A_DK], s)
            yield
            o = qs + _dot(attn, nv)
            st_s[h] = jnp.exp(g_last) * s + _dot_tn(k * jnp.exp(g_last - g_col), nv)
            z = main_ref[rows, A_QKV + h * A_DV:A_QKV + (h + 1) * A_DV]
            o_ref[rows, h * A_DV:(h + 1) * A_DV] = _rms(o, gnw) * _silu(z)

        _interleave(head_chain(h) for h in range(A_HEADS))

    _chunk_loop(n_chunks, body)

    @pl.when(i == pl.num_programs(1) - 1)
    def _():
        convn_ref[0] = ext_s[tb + tail - (A_CONV - 1):tb + tail, :]
        sout_ref[0] = st_s[...]


def _gdn(proj2d, conv0, s0, conv_w, alog_row, dtb_row, gn_w, bsz, seq, tb, chunk):
    nblk = seq // tb
    row = lambda b, i: (b * nblk + i, 0)
    fix2 = lambda b, i: (0, 0)
    kern = functools.partial(_gdn_kernel, tb=tb, chunk=chunk)
    return pl.pallas_call(
        kern,
        grid=(bsz, nblk),
        in_specs=[pl.BlockSpec((tb, A_MAIN), row),
                  pl.BlockSpec((tb, LANES), lambda b, i: (b * nblk + i, P0_GATES // LANES)),
                  pl.BlockSpec((1, A_CONV - 1, A_QKV), lambda b, i: (b, 0, 0)),
                  pl.BlockSpec((1, A_HEADS, A_DK, A_DV), lambda b, i: (b, 0, 0, 0)),
                  pl.BlockSpec((A_CONV, A_QKV), fix2),
                  pl.BlockSpec((1, LANES), fix2),
                  pl.BlockSpec((1, LANES), fix2),
                  pl.BlockSpec((1, A_DV), fix2)],
        out_specs=[pl.BlockSpec((tb, A_VAL), row),
                   pl.BlockSpec((1, A_CONV - 1, A_QKV), lambda b, i: (b, 0, 0)),
                   pl.BlockSpec((1, A_HEADS, A_DK, A_DV), lambda b, i: (b, 0, 0, 0))],
        out_shape=[jax.ShapeDtypeStruct((bsz * seq, A_VAL), F32),
                   jax.ShapeDtypeStruct((bsz, A_CONV - 1, A_QKV), F32),
                   jax.ShapeDtypeStruct((bsz, A_HEADS, A_DK, A_DV), F32)],
        scratch_shapes=[pltpu.VMEM((tb + SUBLANES, A_QKV), F32),
                        pltpu.VMEM((tb, A_QKV), F32),
                        pltpu.VMEM((tb, 2 * LANES), F32),
                        pltpu.VMEM((A_HEADS, A_DK, A_DV), F32)],
        compiler_params=pltpu.CompilerParams(dimension_semantics=("parallel", "arbitrary"),
                                             vmem_limit_bytes=VMEM_LIMIT),
        name="gdn_mixer",
    )(proj2d, proj2d, conv0, s0, conv_w, alog_row, dtb_row, gn_w)


_FR, _FLW, _FK, _FV, _FA, _FB, _FG = range(7)


def _rwkv_kernel(pb_ref, prev_ref, s0_ref, mu_ref, w0_ref, a0_ref, w2a2_ref, kk_ref, ka_ref,
                 rk_ref, lnw_ref, lnb_ref, o_ref, sout_ref, ext_s, feat_s, st_s, tmp_s,
                 *, tb, chunk):
    i = pl.program_id(1)
    n_chunks = tb // chunk
    tail = SUBLANES
    lane = _iota2((1, LANES), 1)
    head0 = lane < B_N
    ones_blk = (_iota2((LANES, LANES), 0) // B_N == _iota2((LANES, LANES), 1) // B_N)

    @pl.when(i == 0)
    def _():
        ext_s[0:tail, :] = jnp.zeros((tail, B_COLS), F32)
        ext_s[tail - 1:tail, :] = prev_ref[0]
        for p in range(B_PAIRS):
            tmp_s[...] = jnp.zeros((B_N, LANES), F32)
            tmp_s[:, 0:B_N] = s0_ref[0, 2 * p + 1]
            shifted = pltpu.roll(tmp_s[...], B_N, 1)
            st_s[p, B_N:LANES, :] = jnp.where(head0, 0.0, shifted)
            tmp_s[:, 0:B_N] = s0_ref[0, 2 * p]
            st_s[p, 0:B_N, :] = tmp_s[...]

    @pl.when(i > 0)
    def _():
        ext_s[0:tail, :] = ext_s[tb:tb + tail, :]

    ext_s[tail:tail + tb, :] = pb_ref[...]
    mu = mu_ref[...]
    for c in range(n_chunks):
        r0 = c * chunk
        pb = ext_s[tail + r0:tail + r0 + chunk, :]
        pb_prev = ext_s[tail + r0 - 1:tail + r0 - 1 + chunk, :]
        pbs = pb + (pb_prev - pb) * mu
        r = pbs[:, 0:B_W]
        kb = pbs[:, B_W:2 * B_W]
        vb = pbs[:, 2 * B_W:3 * B_W]
        gb = pbs[:, 3 * B_W:4 * B_W]
        lo = pbs[:, 4 * B_W:B_COLS]
        lo = jnp.where(head0, jnp.tanh(lo), lo)
        wa = _dot(lo, w2a2_ref[...])
        logw = -B_DECAY_SCALE * _sigmoid(w0_ref[...] + wa[:, 0:B_W])
        a = _sigmoid(a0_ref[...] + wa[:, B_W:2 * B_W])
        kkx = kb * kk_ref[...]
        for p in range(B_PAIRS):
            sl = slice(p * LANES, (p + 1) * LANES)
            kp = kkx[:, sl]
            ss = _dot_exact_rhs(kp * kp, ones_blk)
            kkn = kp * lax.rsqrt(ss + EPS)
            feat_s[r0:r0 + chunk, _FA * B_W + p * LANES:_FA * B_W + (p + 1) * LANES] = -kkn
            feat_s[r0:r0 + chunk, _FB * B_W + p * LANES:_FB * B_W + (p + 1) * LANES] = kkn * a[:, sl]
        feat_s[r0:r0 + chunk, _FR * B_W:(_FR + 1) * B_W] = r
        feat_s[r0:r0 + chunk, _FLW * B_W:(_FLW + 1) * B_W] = logw
        feat_s[r0:r0 + chunk, _FK * B_W:(_FK + 1) * B_W] = kb * (1.0 + (a - 1.0) * ka_ref[...])
        feat_s[r0:r0 + chunk, _FV * B_W:(_FV + 1) * B_W] = vb
        feat_s[r0:r0 + chunk, _FG * B_W:(_FG + 1) * B_W] = gb

    tri = _tri_incl(chunk)
    c2 = 2 * chunk
    ri = _iota2((c2, c2), 0)
    ci = _iota2((c2, c2), 1)
    same_head = (ri // chunk) == (ci // chunk)
    incl_bd = same_head & (ri >= ci)
    strict_bd = same_head & (ri > ci)
    ri4 = _iota2((2 * c2, c2), 0)
    ci4 = _iota2((2 * c2, c2), 1)
    rr4 = ri4 % c2
    lm_mask = ((rr4 // chunk) == (ci4 // chunk)) & ((rr4 > ci4) | ((ri4 >= c2) & (rr4 == ci4)))
    lane_st = (_iota2((c2, LANES), 0) // chunk) == (_iota2((c2, LANES), 1) // B_N)

    def body(c):
        r0 = _row_start(c, chunk)
        rows = pl.ds(r0, chunk)
        def pair_chain(p):
            def feat(slot):
                return feat_s[rows, slot * B_W + p * LANES:slot * B_W + (p + 1) * LANES]
            r, lw, k, v, a_, b_, gate = (feat(s) for s in (_FR, _FLW, _FK, _FV, _FA, _FB, _FG))
            cs = _dot_exact_lhs(tri, lw)
            yield
            cs_last = cs[chunk - 1:chunk, :]
            e_inv = jnp.exp(-cs)
            e_rem = jnp.exp(cs_last - cs)
            at = a_ * jnp.exp(cs - lw)
            rt = r * jnp.exp(cs)
            bt = b_ * e_inv
            kt = k * e_inv
            a_st = jnp.concatenate([jnp.where(head0, at, 0.0), jnp.where(head0, 0.0, at)], axis=0)
            r_st = jnp.concatenate([jnp.where(head0, rt, 0.0), jnp.where(head0, 0.0, rt)], axis=0)
            b2 = jnp.concatenate([bt, bt], axis=0)
            k2 = jnp.concatenate([kt, kt], axis=0)
            v2 = jnp.concatenate([v, v], axis=0)
            ar_st = jnp.concatenate([a_st, r_st], axis=0)
            ar_b = _dot_nt(ar_st, b2)
            ar_k = _dot_nt(ar_st, k2)
            s = st_s[p]
            ar = _dot_nt(jnp.concatenate([at, rt], axis=0), s)
            yield
            l_ab = jnp.where(strict_bd, ar_b[0:c2, :], 0.0)
            m_rb = jnp.where(incl_bd, ar_b[c2:2 * c2, :], 0.0)
            lm_k = jnp.where(lm_mask, ar_k, 0.0)
            lm_v = _dot(lm_k, v2)
            t_inv = yield from _inv_unit_lower(l_ab, chunk)
            ah = ar[0:chunk, :]
            rh = ar[chunk:c2, :]
            x_st = jnp.where(lane_st, jnp.concatenate([ah, ah], axis=0) + lm_v[0:c2, :], 0.0)
            u_st = _dot(t_inv, x_st)
            yield
            o_st = jnp.where(lane_st, _dot(m_rb, u_st) + lm_v[c2:2 * c2, :], 0.0)
            u = u_st[0:chunk, :] + u_st[chunk:c2, :]
            upd = _dot_tn(jnp.concatenate([u, v], axis=0),
                          jnp.concatenate([b_ * e_rem, k * e_rem], axis=0))
            yield
            o = rh + o_st[0:chunk, :] + o_st[chunk:c2, :]
            st_s[p] = s * jnp.exp(cs_last) + jnp.where(ones_blk, upd, 0.0)
            sl = slice(p * LANES, (p + 1) * LANES)
            mean = _dot_exact_rhs(o, ones_blk) * (1.0 / B_N)
            bonus = _dot_exact_rhs(r * k * rk_ref[:, sl], ones_blk) * v
            yield
            d = o - mean
            var = _dot_exact_rhs(d * d, ones_blk) * (1.0 / B_N)
            yield
            ob = d * lax.rsqrt(var + B_GN_EPS) * lnw_ref[:, sl] + lnb_ref[:, sl] + bonus
            o_ref[rows, sl] = ob * _silu(gate)

        _interleave(pair_chain(p) for p in range(B_PAIRS))

    _chunk_loop(n_chunks, body)

    @pl.when(i == pl.num_programs(1) - 1)
    def _():
        for p in range(B_PAIRS):
            sout_ref[0, 2 * p] = st_s[p, 0:B_N, 0:B_N]
            shifted = pltpu.roll(st_s[p, B_N:LANES, :], B_N, 1)
            sout_ref[0, 2 * p + 1] = shifted[:, 0:B_N]


def _rwkv(proj2d, prev_row, s0, mu, w0, a0, w2a2, k_k, k_a, r_k, ln_w, ln_b, bsz, seq, tb, chunk):
    nblk = seq // tb
    row = lambda b, i: (b * nblk + i, 0)
    fix2 = lambda b, i: (0, 0)
    kern = functools.partial(_rwkv_kernel, tb=tb, chunk=chunk)
    vec = pl.BlockSpec((1, B_W), fix2)
    return pl.pallas_call(
        kern,
        grid=(bsz, nblk),
        in_specs=[pl.BlockSpec((tb, B_COLS), lambda b, i: (b * nblk + i, P0_B // B_COLS)),
                  pl.BlockSpec((1, 1, B_COLS), lambda b, i: (b, 0, 0)),
                  pl.BlockSpec((1, B_HEADS, B_N, B_N), lambda b, i: (b, 0, 0, 0)),
                  pl.BlockSpec((1, B_COLS), fix2),
                  vec, vec,
                  pl.BlockSpec((2 * B_LORA, 2 * B_W), fix2),
                  vec, vec, vec, vec, vec],
        out_specs=[pl.BlockSpec((tb, B_W), row),
                   pl.BlockSpec((1, B_HEADS, B_N, B_N), lambda b, i: (b, 0, 0, 0))],
        out_shape=[jax.ShapeDtypeStruct((bsz * seq, B_W), F32),
                   jax.ShapeDtypeStruct((bsz, B_HEADS, B_N, B_N), F32)],
        scratch_shapes=[pltpu.VMEM((tb + SUBLANES, B_COLS), F32),
                        pltpu.VMEM((tb, 7 * B_W), F32),
                        pltpu.VMEM((B_PAIRS, LANES, LANES), F32),
                        pltpu.VMEM((B_N, LANES), F32)],
        compiler_params=pltpu.CompilerParams(dimension_semantics=("parallel", "arbitrary"),
                                             vmem_limit_bytes=VMEM_LIMIT),
        name="rwkv7_mixer",
    )(proj2d, prev_row, s0, mu, w0, a0, w2a2, k_k, k_a, r_k, ln_w, ln_b)


def _mlstm_kernel(p_ref, c0_ref, n0_ref, m0_ref, bias_ref, o_ref, cout_ref, nout_ref, mout_ref,
                  gsc_s, caug_s, m_s, *, tb, chunk):
    i = pl.program_id(1)
    n_chunks = tb // chunk
    lane = _iota2((1, LANES), 1)
    krow = _iota2((LANES, 1), 0)
    ncol_mask = _iota2((LANES, LANES), 1) == 0

    @pl.when(i == 0)
    def _():
        m0 = m0_ref[0]
        for h in range(C_HEADS):
            j = h % 2
            caug_s[h] = jnp.zeros((LANES, 2 * C_DV), F32)
            caug_s[h, j * C_DQK:(j + 1) * C_DQK, 0:C_DV] = c0_ref[0, h]
            n_col = _row_to_col(n0_ref[0, h // 2:h // 2 + 1, :])
            own = (krow // C_DQK) == j
            caug_s[h, :, C_DV:2 * C_DV] = jnp.where(ncol_mask & own, n_col, 0.0)
            m_s[h:h + 1, :] = jnp.broadcast_to(m0[:, h:h + 1], (1, LANES))

    g = p_ref[:, P1_GATES:P1_PAD] + bias_ref[...]
    cap = C_GATE_CAP * jnp.tanh(g / C_GATE_CAP)
    gsc_s[...] = jnp.where(lane < C_HEADS, cap, -_softplus(-cap))

    tri = _tri_incl(chunk)
    ones_col = jnp.where(_iota2((chunk, C_DV), 1) == 0, 1.0, 0.0)
    off = 2 * C_QK

    def body(c):
        r0 = _row_start(c, chunk)
        rows = pl.ds(r0, chunk)
        gl = gsc_s[rows, :]
        bcum = _dot_exact_lhs(tri, gl)
        def head_chain(h):
            p, j = h // 2, h % 2
            own_lane = (lane // C_DQK) == j
            own_row = (krow // C_DQK) == j
            qp = p_ref[rows, p * LANES:(p + 1) * LANES] * (C_DQK ** -0.5)
            kp = p_ref[rows, C_QK + p * LANES:C_QK + (p + 1) * LANES]
            v = p_ref[rows, off + h * C_DV:off + (h + 1) * C_DV]
            vaug = jnp.concatenate([v, ones_col], axis=1)
            qk = _dot_nt(jnp.where(own_lane, qp, 0.0), kp)
            cm = caug_s[h]
            qc = _dot(qp, cm)
            b_col = bcum[:, C_HEADS + h:C_HEADS + h + 1]
            i_col = gl[:, h:h + 1]
            b_row = _col_to_row(b_col)
            i_row = _col_to_row(i_col)
            dmat = jnp.where(tri, b_col - b_row + i_row, -jnp.inf)
            dmax = jnp.max(dmat, axis=-1, keepdims=True)
            m = m_s[h:h + 1, 0:1]
            inter = b_col + m
            mt = jnp.maximum(inter, dmax)
            m_new = mt[chunk - 1:chunk, :]
            b_last = b_col[chunk - 1:chunk, :]
            w_s = jnp.exp(b_last - b_col + i_col - m_new)
            e_c = jnp.exp(b_last + m - m_new)
            upd = _dot_tn(w_s * kp, vaug)
            yield
            pm = jnp.exp(dmat - mt) * qk
            num = jnp.exp(inter - mt) * qc + _dot(pm, vaug)
            caug_s[h] = e_c * cm + jnp.where(own_row, upd, 0.0)
            m_s[h:h + 1, :] = jnp.broadcast_to(m_new, (1, LANES))
            yield
            den = num[:, C_DV:C_DV + 1]
            hh = num[:, 0:C_DV] / jnp.maximum(jnp.abs(den), jnp.exp(-mt))
            og = p_ref[rows, off + C_V + h * C_DV:off + C_V + (h + 1) * C_DV]
            z = p_ref[rows, off + 2 * C_V + h * C_DV:off + 2 * C_V + (h + 1) * C_DV]
            o_ref[rows, h * C_DV:(h + 1) * C_DV] = hh * _sigmoid(og) * _silu(z)

        _interleave(head_chain(h) for h in range(C_HEADS))

    _chunk_loop(n_chunks, body)

    @pl.when(i == pl.num_programs(1) - 1)
    def _():
        for h in range(C_HEADS):
            j = h % 2
            cout_ref[0, h] = caug_s[h, j * C_DQK:(j + 1) * C_DQK, 0:C_DV]
        for p in range(C_HEADS // 2):
            n_col = caug_s[2 * p, :, C_DV:C_DV + 1] + caug_s[2 * p + 1, :, C_DV:C_DV + 1]
            nout_ref[0, p:p + 1, :] = _col_to_row(n_col)
        mout_ref[0] = m_s[...]


def _mlstm(proj2d, c0, n0, m0, bias_row, bsz, seq, tb, chunk):
    nblk = seq // tb
    row = lambda b, i: (b * nblk + i, 0)
    kern = functools.partial(_mlstm_kernel, tb=tb, chunk=chunk)
    pairs = C_HEADS // 2
    return pl.pallas_call(
        kern,
        grid=(bsz, nblk),
        in_specs=[pl.BlockSpec((tb, P1_PAD), row),
                  pl.BlockSpec((1, C_HEADS, C_DQK, C_DV), lambda b, i: (b, 0, 0, 0)),
                  pl.BlockSpec((1, pairs, LANES), lambda b, i: (b, 0, 0)),
                  pl.BlockSpec((1, 1, C_HEADS), lambda b, i: (b, 0, 0)),
                  pl.BlockSpec((1, LANES), lambda b, i: (0, 0))],
        out_specs=[pl.BlockSpec((tb, C_V), row),
                   pl.BlockSpec((1, C_HEADS, C_DQK, C_DV), lambda b, i: (b, 0, 0, 0)),
                   pl.BlockSpec((1, pairs, LANES), lambda b, i: (b, 0, 0)),
                   pl.BlockSpec((1, C_HEADS, LANES), lambda b, i: (b, 0, 0))],
        out_shape=[jax.ShapeDtypeStruct((bsz * seq, C_V), F32),
                   jax.ShapeDtypeStruct((bsz, C_HEADS, C_DQK, C_DV), F32),
                   jax.ShapeDtypeStruct((bsz, pairs, LANES), F32),
                   jax.ShapeDtypeStruct((bsz, C_HEADS, LANES), F32)],
        scratch_shapes=[pltpu.VMEM((tb, LANES), F32),
                        pltpu.VMEM((C_HEADS, LANES, 2 * C_DV), F32),
                        pltpu.VMEM((C_HEADS, LANES), F32)],
        compiler_params=pltpu.CompilerParams(dimension_semantics=("parallel", "arbitrary"),
                                             vmem_limit_bytes=VMEM_LIMIT),
        name="mlstm_mixer",
    )(proj2d, c0, n0, m0, bias_row)


def _pad_lanes(vec, offset):
    out = jnp.zeros((1, LANES), F32)
    return out.at[0, offset:offset + vec.shape[0]].set(vec.astype(F32))


def _prepare_params(p):
    w_in0 = p["w_in0"]
    a_cols = A_MAIN + 2 * A_HEADS
    w0 = jnp.zeros((D_MODEL, P0_PAD), F32)
    w0 = w0.at[:, 0:a_cols].set(w_in0[:, 0:a_cols])
    w0 = w0.at[:, P0_B:P0_PAD].set(w_in0[:, a_cols:])
    w_in1 = p["w_in1"]
    w1 = jnp.zeros((D_MODEL, P1_PAD), F32).at[:, 0:w_in1.shape[1]].set(w_in1)
    w2a2 = jnp.zeros((2 * B_LORA, 2 * B_W), F32)
    w2a2 = w2a2.at[0:B_LORA, 0:B_W].set(p["rwkv_w2"]).at[B_LORA:, B_W:].set(p["rwkv_a2"])
    row = lambda v: v.reshape(1, -1).astype(F32)
    return dict(
        w0=w0.astype(BF16), w1=w1.astype(BF16), wb=w_in0[:, a_cols:].astype(BF16),
        woa=p["w_out0"][0:A_VAL].astype(BF16), wob=p["w_out0"][A_VAL:].astype(BF16),
        wo1=p["w_out1"].astype(BF16), w2a2=w2a2.astype(BF16),
        norm0=row(p["norm0_w"]), norm1=row(p["norm1_w"]), normf=row(p["norm_f_w"]),
        conv_w=p["gdn_conv_w"].astype(F32),
        alog=_pad_lanes(p["gdn_a_log"], A_HEADS), dtb=_pad_lanes(p["gdn_dt_bias"], A_HEADS),
        gnw=row(p["gdn_norm_w"]),
        mu=row(p["rwkv_mu"]), rw0=row(p["rwkv_w0"]), ra0=row(p["rwkv_a0"]),
        k_k=row(p["rwkv_k_k"]), k_a=row(p["rwkv_k_a"]), r_k=row(p["rwkv_r_k"]),
        ln_w=row(p["rwkv_ln_w"]), ln_b=row(p["rwkv_ln_b"]),
        gate_bias=_pad_lanes(jnp.concatenate([p["mlstm_b_i"], p["mlstm_b_f"]]), 0),
    )


def _trunk(x, state, q):
    conv0, s_gdn, x_prev, s_rwkv, c0, n0, m0 = state
    bsz, seq, _ = x.shape
    chunk = math.gcd(seq, MAX_CHUNK)
    tb = min(seq, 256)
    m = bsz * seq
    tm = min(m, 256)
    x2d = x.reshape(m, D_MODEL)

    proj0 = _norm_proj(x2d, q["norm0"], q["w0"], tm)
    xn_last, prev_row = _aux(x[:, -1, :], x_prev, q["norm0"], q["wb"])
    o_a, conv_new, s_gdn_new = _gdn(proj0, conv0, s_gdn, q["conv_w"], q["alog"], q["dtb"], q["gnw"],
                                    bsz, seq, tb, chunk)
    o_b, s_rwkv_new = _rwkv(proj0, prev_row.reshape(bsz, 1, B_COLS), s_rwkv, q["mu"], q["rw0"], q["ra0"],
                            q["w2a2"], q["k_k"], q["k_a"], q["r_k"], q["ln_w"], q["ln_b"],
                            bsz, seq, tb, chunk)
    x1, proj1 = _mid(x2d, o_a, o_b, q["woa"], q["wob"], q["norm1"], q["w1"], tm)
    h, c_new, n_new, m_new = _mlstm(proj1, c0, n0.reshape(bsz, C_HEADS // 2, LANES),
                                    m0.reshape(bsz, 1, C_HEADS), q["gate_bias"], bsz, seq, tb, chunk)
    y = _final(x1, h, q["wo1"], q["normf"], tm)
    return y.reshape(bsz, seq, D_MODEL), (conv_new, s_gdn_new, xn_last, s_rwkv_new, c_new,
                                          n_new.reshape(bsz, C_HEADS, C_DQK), m_new[:, :, 0])


def kernel(x_prompt, x_sample, state_gdn_conv, state_gdn, state_rwkv_shift, state_rwkv,
           state_mlstm_c, state_mlstm_n, state_mlstm_m,
           norm0_w, w_in0, w_out0, gdn_conv_w, gdn_a_log, gdn_dt_bias, gdn_norm_w,
           rwkv_mu, rwkv_w0, rwkv_w2, rwkv_a0, rwkv_a2, rwkv_k_k, rwkv_k_a, rwkv_r_k,
           rwkv_ln_w, rwkv_ln_b, norm1_w, w_in1, w_out1, mlstm_b_i, mlstm_b_f, norm_f_w):
    q = _prepare_params(dict(
        norm0_w=norm0_w, w_in0=w_in0, w_out0=w_out0, gdn_conv_w=gdn_conv_w, gdn_a_log=gdn_a_log,
        gdn_dt_bias=gdn_dt_bias, gdn_norm_w=gdn_norm_w, rwkv_mu=rwkv_mu, rwkv_w0=rwkv_w0,
        rwkv_w2=rwkv_w2, rwkv_a0=rwkv_a0, rwkv_a2=rwkv_a2, rwkv_k_k=rwkv_k_k, rwkv_k_a=rwkv_k_a,
        rwkv_r_k=rwkv_r_k, rwkv_ln_w=rwkv_ln_w, rwkv_ln_b=rwkv_ln_b, norm1_w=norm1_w, w_in1=w_in1,
        w_out1=w_out1, mlstm_b_i=mlstm_b_i, mlstm_b_f=mlstm_b_f, norm_f_w=norm_f_w))
    bsz = x_prompt.shape[0]
    prompt_state = (
        jnp.zeros((bsz, A_CONV - 1, A_QKV), F32),
        jnp.zeros((bsz, A_HEADS, A_DK, A_DV), F32),
        jnp.zeros((bsz, D_MODEL), F32),
        jnp.zeros((bsz, B_HEADS, B_N, B_N), F32),
        jnp.zeros((bsz, C_HEADS, C_DQK, C_DV), F32),
        jnp.zeros((bsz, C_HEADS, C_DQK), F32),
        jnp.zeros((bsz, C_HEADS), F32),
    )
    y_p, sp = _trunk(x_prompt, prompt_state, q)
    sample_state = (state_gdn_conv, state_gdn, state_rwkv_shift, state_rwkv,
                    state_mlstm_c, state_mlstm_n, state_mlstm_m)
    y_s, ss = _trunk(x_sample, sample_state, q)
    return (y_p, y_s) + tuple(sp) + tuple(ss)
```

```python
import functools
import math

import jax
import jax.numpy as jnp
from jax import lax
from jax.experimental import pallas as pl
from jax.experimental.pallas import tpu as pltpu

F32 = jnp.float32
BF16 = jnp.bfloat16

D_MODEL = 1024
EPS = 1e-6
LANES = 128
SUBLANES = 8
A_HEADS, A_DK, A_DV, A_CONV = 4, 128, 128, 4
A_KEY = A_HEADS * A_DK
A_VAL = A_HEADS * A_DV
A_QKV = 2 * A_KEY + A_VAL
A_MAIN = A_QKV + A_VAL
B_HEADS, B_N, B_LORA = 8, 64, 64
B_W = B_HEADS * B_N
B_COLS = 4 * B_W + 2 * B_LORA
B_PAIRS = B_HEADS // 2
B_GN_EPS = 64e-5
B_DECAY_SCALE = math.exp(-0.5)
C_HEADS, C_DQK, C_DV = 8, 64, 128
C_QK = C_HEADS * C_DQK
C_V = C_HEADS * C_DV
C_GATE_CAP = 15.0
P0_GATES = A_MAIN
P0_B = A_MAIN + LANES
P0_PAD = P0_B + B_COLS
P1_GATES = 2 * C_QK + 3 * C_V
P1_PAD = P1_GATES + LANES
MAX_CHUNK = 64
VMEM_LIMIT = 56 * 1024 * 1024


def _dot(a, b):
    return jnp.dot(a.astype(BF16), b.astype(BF16), preferred_element_type=F32)


def _dot_nt(a, b):
    return lax.dot_general(a.astype(BF16), b.astype(BF16), (((1,), (1,)), ((), ())),
                           preferred_element_type=F32)


def _dot_tn(a, b):
    return lax.dot_general(a.astype(BF16), b.astype(BF16), (((0,), (0,)), ((), ())),
                           preferred_element_type=F32)


def _split(x, terms):
    out = []
    for _ in range(terms - 1):
        hi = x.astype(BF16)
        out.append(hi)
        x = x - hi.astype(F32)
    out.append(x.astype(BF16))
    return out


def _dot_exact_lhs(sel, x, terms=3):
    s = sel.astype(BF16)
    return jnp.dot(jnp.concatenate([s] * terms, axis=1), jnp.concatenate(_split(x, terms), axis=0),
                   preferred_element_type=F32)


def _dot_exact_rhs(x, sel, terms=2):
    s = sel.astype(BF16)
    return jnp.dot(jnp.concatenate(_split(x, terms), axis=1), jnp.concatenate([s] * terms, axis=0),
                   preferred_element_type=F32)


def _iota2(shape, dim):
    return lax.broadcasted_iota(jnp.int32, shape, dim)


def _tri_incl(n):
    return (_iota2((n, n), 0) >= _iota2((n, n), 1))


def _col_to_row(col):
    n = col.shape[0]
    eye = _iota2((n, n), 0) == _iota2((n, n), 1)
    return jnp.sum(jnp.where(eye, col, 0.0), axis=0, keepdims=True)


def _row_to_col(row):
    n = row.shape[1]
    eye = _iota2((n, n), 0) == _iota2((n, n), 1)
    return jnp.sum(jnp.where(eye, row, 0.0), axis=1, keepdims=True)


def _inv_unit_lower(p, levels):
    n = p.shape[0]
    eye = (_iota2((n, n), 0) == _iota2((n, n), 1)).astype(F32)
    t = eye + p
    m = 2
    if m < levels:
        p = _dot(p, p)
        yield
    while m < levels:
        m *= 2
        if m < levels:
            both = _dot(jnp.concatenate([p, t], axis=0), p)
            p, t = both[0:n, :], t + both[n:2 * n, :]
            yield
        else:
            t = t + _dot(t, p)
    return t


def _interleave(chains):
    live = list(chains)
    while live:
        nxt = []
        for g in live:
            try:
                next(g)
                nxt.append(g)
            except StopIteration:
                pass
        live = nxt


def _sigmoid(x):
    return 1.0 / (1.0 + jnp.exp(-x))


def _silu(x):
    return x * _sigmoid(x)


def _softplus(x):
    return jnp.maximum(x, 0.0) + jnp.log1p(jnp.exp(-jnp.abs(x)))


def _rms(x, w):
    return x * lax.rsqrt(jnp.mean(x * x, axis=-1, keepdims=True) + EPS) * w


def _chunk_loop(n_chunks, body):
    if n_chunks == 1:
        body(0)
    else:
        def step(c, carry):
            body(c)
            return carry
        lax.fori_loop(0, n_chunks, step, 0)


def _row_start(c, chunk):
    if isinstance(c, int):
        return c * chunk
    return pl.multiple_of(c * chunk, chunk)


def _resident(shape):
    zeros = (0,) * len(shape)
    return pl.BlockSpec(shape, lambda *_: zeros, pipeline_mode=pl.Buffered(1))


def _norm_proj_kernel(x_ref, nw_ref, w_ref, o_ref):
    xn = _rms(x_ref[...], nw_ref[...])
    o_ref[...] = jnp.dot(xn.astype(BF16), w_ref[...], preferred_element_type=F32)


def _norm_proj(x2d, norm_w, w_bf16, tm):
    m = x2d.shape[0]
    n = w_bf16.shape[1]
    return pl.pallas_call(
        _norm_proj_kernel,
        grid=(m // tm,),
        in_specs=[pl.BlockSpec((tm, D_MODEL), lambda i: (i, 0)),
                  _resident((1, D_MODEL)),
                  _resident((D_MODEL, n))],
        out_specs=pl.BlockSpec((tm, n), lambda i: (i, 0)),
        out_shape=jax.ShapeDtypeStruct((m, n), F32),
        compiler_params=pltpu.CompilerParams(dimension_semantics=("parallel",),
                                             vmem_limit_bytes=VMEM_LIMIT),
        name="norm_proj0",
    )(x2d, norm_w, w_bf16)


def _mid_kernel(x_ref, oa_ref, ob_ref, woa_ref, wob_ref, nw_ref, wi_ref, x1_ref, p_ref):
    y = jnp.dot(oa_ref[...].astype(BF16), woa_ref[...], preferred_element_type=F32)
    y = y + jnp.dot(ob_ref[...].astype(BF16), wob_ref[...], preferred_element_type=F32)
    x1 = x_ref[...] + y
    x1_ref[...] = x1
    xn = _rms(x1, nw_ref[...])
    p_ref[...] = jnp.dot(xn.astype(BF16), wi_ref[...], preferred_element_type=F32)


def _mid(x2d, oa, ob, woa, wob, norm_w, wi, tm):
    m = x2d.shape[0]
    n = wi.shape[1]
    row = lambda i: (i, 0)
    return pl.pallas_call(
        _mid_kernel,
        grid=(m // tm,),
        in_specs=[pl.BlockSpec((tm, D_MODEL), row),
                  pl.BlockSpec((tm, A_VAL), row),
                  pl.BlockSpec((tm, B_W), row),
                  _resident((A_VAL, D_MODEL)),
                  _resident((B_W, D_MODEL)),
                  _resident((1, D_MODEL)),
                  _resident((D_MODEL, n))],
        out_specs=[pl.BlockSpec((tm, D_MODEL), row), pl.BlockSpec((tm, n), row)],
        out_shape=[jax.ShapeDtypeStruct((m, D_MODEL), F32), jax.ShapeDtypeStruct((m, n), F32)],
        compiler_params=pltpu.CompilerParams(dimension_semantics=("parallel",),
                                             vmem_limit_bytes=VMEM_LIMIT),
        name="out0_norm_proj1",
    )(x2d, oa, ob, woa, wob, norm_w, wi)


def _final_kernel(x_ref, h_ref, wo_ref, nw_ref, y_ref):
    x2 = x_ref[...] + jnp.dot(h_ref[...].astype(BF16), wo_ref[...], preferred_element_type=F32)
    y_ref[...] = _rms(x2, nw_ref[...])


def _final(x2d, h, wo, norm_w, tm):
    m = x2d.shape[0]
    row = lambda i: (i, 0)
    return pl.pallas_call(
        _final_kernel,
        grid=(m // tm,),
        in_specs=[pl.BlockSpec((tm, D_MODEL), row),
                  pl.BlockSpec((tm, C_V), row),
                  _resident((C_V, D_MODEL)),
                  _resident((1, D_MODEL))],
        out_specs=pl.BlockSpec((tm, D_MODEL), row),
        out_shape=jax.ShapeDtypeStruct((m, D_MODEL), F32),
        compiler_params=pltpu.CompilerParams(dimension_semantics=("parallel",),
                                             vmem_limit_bytes=VMEM_LIMIT),
        name="out1_norm_f",
    )(x2d, h, wo, norm_w)


def _aux_kernel(xl_ref, xp_ref, nw_ref, wb_ref, xn_ref, prev_ref):
    xn_ref[...] = _rms(xl_ref[...], nw_ref[...])
    prev_ref[...] = jnp.dot(xp_ref[...].astype(BF16), wb_ref[...], preferred_element_type=F32)


def _aux(x_last, x_prev, norm_w, wb):
    b = x_last.shape[0]
    return pl.pallas_call(
        _aux_kernel,
        out_shape=[jax.ShapeDtypeStruct((b, D_MODEL), F32), jax.ShapeDtypeStruct((b, B_COLS), F32)],
        compiler_params=pltpu.CompilerParams(vmem_limit_bytes=VMEM_LIMIT),
        name="shift_rows",
    )(x_last, x_prev, norm_w, wb)


def _gdn_kernel(main_ref, gates_ref, conv0_ref, s0_ref, convw_ref, alog_ref, dtb_ref, gnw_ref,
                o_ref, convn_ref, sout_ref, ext_s, qkv_s, gsc_s, st_s, *, nb, tb, chunk):
    i = pl.program_id(1)
    n_chunks = tb // chunk
    tail = SUBLANES

    @pl.when(i == 0)
    def _():
        for bi in range(nb):
            ext_s[bi, 0:tail, :] = jnp.zeros((tail, A_QKV), F32)
            ext_s[bi, tail - (A_CONV - 1):tail, :] = conv0_ref[bi]
        st_s[...] = s0_ref[...]

    @pl.when(i > 0)
    def _():
        for bi in range(nb):
            ext_s[bi, 0:tail, :] = ext_s[bi, tb:tb + tail, :]

    cw = convw_ref[...]
    for bi in range(nb):
        ext_s[bi, tail:tail + tb, :] = main_ref[bi, :, 0:A_QKV]
        for c in range(n_chunks):
            r0 = c * chunk
            y = ext_s[bi, tail + r0:tail + r0 + chunk, :] * cw[A_CONV - 1:A_CONV, :]
            for j in range(A_CONV - 1):
                sh = A_CONV - 1 - j
                y = y + ext_s[bi, tail + r0 - sh:tail + r0 - sh + chunk, :] * cw[j:j + 1, :]
            y = _silu(y)
            for h in range(A_HEADS):
                q = y[:, h * A_DK:(h + 1) * A_DK]
                qkv_s[bi, r0:r0 + chunk, h * A_DK:(h + 1) * A_DK] = (
                    q * lax.rsqrt(jnp.sum(q * q, axis=-1, keepdims=True) + EPS) * (A_DK ** -0.5))
                k = y[:, A_KEY + h * A_DK:A_KEY + (h + 1) * A_DK]
                qkv_s[bi, r0:r0 + chunk, A_KEY + h * A_DK:A_KEY + (h + 1) * A_DK] = (
                    k * lax.rsqrt(jnp.sum(k * k, axis=-1, keepdims=True) + EPS))
            qkv_s[bi, r0:r0 + chunk, 2 * A_KEY:A_QKV] = y[:, 2 * A_KEY:A_QKV]
        g = gates_ref[bi]
        gsc_s[bi, :, 0:LANES] = _sigmoid(g)
        gsc_s[bi, :, LANES:2 * LANES] = -jnp.exp(alog_ref[...]) * _softplus(g + dtb_ref[...])

    tri = _tri_incl(chunk)
    strict = _iota2((chunk, chunk), 0) > _iota2((chunk, chunk), 1)
    gnw = gnw_ref[...]

    def body(c):
        r0 = _row_start(c, chunk)
        rows = pl.ds(r0, chunk)
        gates = [(gsc_s[bi, rows, 0:LANES],
                  _dot_exact_lhs(tri, gsc_s[bi, rows, LANES:2 * LANES]))
                 for bi in range(nb)]

        def head_chain(bi, h):
            beta, gcum = gates[bi]
            q = qkv_s[bi, rows, h * A_DK:(h + 1) * A_DK]
            k = qkv_s[bi, rows, A_KEY + h * A_DK:A_KEY + (h + 1) * A_DK]
            v = qkv_s[bi, rows, 2 * A_KEY + h * A_DV:2 * A_KEY + (h + 1) * A_DV]
            qk_k = _dot_nt(jnp.concatenate([q, k], axis=0), k)
            yield
            b_col = beta[:, h:h + 1]
            g_col = gcum[:, A_HEADS + h:A_HEADS + h + 1]
            g_row = _col_to_row(g_col)
            g_last = g_col[chunk - 1:chunk, :]
            decay = jnp.where(tri, jnp.exp(jnp.where(tri, g_col - g_row, 0.0)), 0.0)
            eg = jnp.exp(g_col)
            s = st_s[bi, h]
            qs = _dot(q * eg, s)
            attn = decay * qk_k[0:chunk, :]
            amat = jnp.where(strict, b_col * decay * qk_k[chunk:2 * chunk, :], 0.0)
            t_inv = yield from _inv_unit_lower(-amat, chunk)
            sol = _dot(t_inv, jnp.concatenate([b_col * v, (b_col * eg) * k], axis=1))
            yield
            nv = sol[:, 0:A_DV] - _dot(sol[:, A_DV:A_DV + A_DK], s)
            yield
            o = qs + _dot(attn, nv)
            st_s[bi, h] = jnp.exp(g_last) * s + _dot_tn(k * jnp.exp(g_last - g_col), nv)
            z = main_ref[bi, rows, A_QKV + h * A_DV:A_QKV + (h + 1) * A_DV]
            o_ref[bi, rows, h * A_DV:(h + 1) * A_DV] = _rms(o, gnw) * _silu(z)

        _interleave(head_chain(bi, h) for bi in range(nb) for h in range(A_HEADS))

    _chunk_loop(n_chunks, body)

    @pl.when(i == pl.num_programs(1) - 1)
    def _():
        for bi in range(nb):
            convn_ref[bi] = ext_s[bi, tb + tail - (A_CONV - 1):tb + tail, :]
        sout_ref[...] = st_s[...]


def _gdn(proj3d, conv0, s0, conv_w, alog_row, dtb_row, gn_w, nb, tb, chunk):
    bsz, seq, _ = proj3d.shape
    blk = lambda b, i: (b, i, 0)
    per_b3 = lambda b, i: (b, 0, 0)
    per_b4 = lambda b, i: (b, 0, 0, 0)
    fix2 = lambda b, i: (0, 0)
    kern = functools.partial(_gdn_kernel, nb=nb, tb=tb, chunk=chunk)
    return pl.pallas_call(
        kern,
        grid=(bsz // nb, seq // tb),
        in_specs=[pl.BlockSpec((nb, tb, A_MAIN), blk),
                  pl.BlockSpec((nb, tb, LANES), lambda b, i: (b, i, P0_GATES // LANES)),
                  pl.BlockSpec((nb, A_CONV - 1, A_QKV), per_b3),
                  pl.BlockSpec((nb, A_HEADS, A_DK, A_DV), per_b4),
                  pl.BlockSpec((A_CONV, A_QKV), fix2),
                  pl.BlockSpec((1, LANES), fix2),
                  pl.BlockSpec((1, LANES), fix2),
                  pl.BlockSpec((1, A_DV), fix2)],
        out_specs=[pl.BlockSpec((nb, tb, A_VAL), blk),
                   pl.BlockSpec((nb, A_CONV - 1, A_QKV), per_b3),
                   pl.BlockSpec((nb, A_HEADS, A_DK, A_DV), per_b4)],
        out_shape=[jax.ShapeDtypeStruct((bsz, seq, A_VAL), F32),
                   jax.ShapeDtypeStruct((bsz, A_CONV - 1, A_QKV), F32),
                   jax.ShapeDtypeStruct((bsz, A_HEADS, A_DK, A_DV), F32)],
        scratch_shapes=[pltpu.VMEM((nb, tb + SUBLANES, A_QKV), F32),
                        pltpu.VMEM((nb, tb, A_QKV), F32),
                        pltpu.VMEM((nb, tb, 2 * LANES), F32),
                        pltpu.VMEM((nb, A_HEADS, A_DK, A_DV), F32)],
        compiler_params=pltpu.CompilerParams(dimension_semantics=("parallel", "arbitrary"),
                                             vmem_limit_bytes=VMEM_LIMIT),
        name="gdn_mixer",
    )(proj3d, proj3d, conv0, s0, conv_w, alog_row, dtb_row, gn_w)


_FR, _FLW, _FK, _FV, _FA, _FB, _FG = range(7)


def _rwkv_kernel(pb_ref, prev_ref, s0_ref, mu_ref, w0_ref, a0_ref, w2a2_ref, kk_ref, ka_ref,
                 rk_ref, lnw_ref, lnb_ref, o_ref, sout_ref, ext_s, feat_s, st_s, tmp_s,
                 *, nb, tb, chunk):
    i = pl.program_id(1)
    n_chunks = tb // chunk
    tail = SUBLANES
    lane = _iota2((1, LANES), 1)
    head0 = lane < B_N
    ones_blk = (_iota2((LANES, LANES), 0) // B_N == _iota2((LANES, LANES), 1) // B_N)

    @pl.when(i == 0)
    def _():
        for bi in range(nb):
            ext_s[bi, 0:tail, :] = jnp.zeros((tail, B_COLS), F32)
            ext_s[bi, tail - 1:tail, :] = prev_ref[bi]
            for p in range(B_PAIRS):
                tmp_s[...] = jnp.zeros((B_N, LANES), F32)
                tmp_s[:, 0:B_N] = s0_ref[bi, 2 * p + 1]
                shifted = pltpu.roll(tmp_s[...], B_N, 1)
                st_s[bi, p, B_N:LANES, :] = jnp.where(head0, 0.0, shifted)
                tmp_s[:, 0:B_N] = s0_ref[bi, 2 * p]
                st_s[bi, p, 0:B_N, :] = tmp_s[...]

    @pl.when(i > 0)
    def _():
        for bi in range(nb):
            ext_s[bi, 0:tail, :] = ext_s[bi, tb:tb + tail, :]

    mu = mu_ref[...]
    for bi in range(nb):
        ext_s[bi, tail:tail + tb, :] = pb_ref[bi]
        for c in range(n_chunks):
            r0 = c * chunk
            pb = ext_s[bi, tail + r0:tail + r0 + chunk, :]
            pb_prev = ext_s[bi, tail + r0 - 1:tail + r0 - 1 + chunk, :]
            pbs = pb + (pb_prev - pb) * mu
            r = pbs[:, 0:B_W]
            kb = pbs[:, B_W:2 * B_W]
            vb = pbs[:, 2 * B_W:3 * B_W]
            gb = pbs[:, 3 * B_W:4 * B_W]
            lo = pbs[:, 4 * B_W:B_COLS]
            lo = jnp.where(head0, jnp.tanh(lo), lo)
            wa = _dot(lo, w2a2_ref[...])
            logw = -B_DECAY_SCALE * _sigmoid(w0_ref[...] + wa[:, 0:B_W])
            a = _sigmoid(a0_ref[...] + wa[:, B_W:2 * B_W])
            kkx = kb * kk_ref[...]
            rws = slice(r0, r0 + chunk)
            for p in range(B_PAIRS):
                sl = slice(p * LANES, (p + 1) * LANES)
                kp = kkx[:, sl]
                ss = _dot_exact_rhs(kp * kp, ones_blk)
                kkn = kp * lax.rsqrt(ss + EPS)
                feat_s[bi, rws, _FA * B_W + p * LANES:_FA * B_W + (p + 1) * LANES] = -kkn
                feat_s[bi, rws, _FB * B_W + p * LANES:_FB * B_W + (p + 1) * LANES] = kkn * a[:, sl]
            feat_s[bi, rws, _FR * B_W:(_FR + 1) * B_W] = r
            feat_s[bi, rws, _FLW * B_W:(_FLW + 1) * B_W] = logw
            feat_s[bi, rws, _FK * B_W:(_FK + 1) * B_W] = kb * (1.0 + (a - 1.0) * ka_ref[...])
            feat_s[bi, rws, _FV * B_W:(_FV + 1) * B_W] = vb
            feat_s[bi, rws, _FG * B_W:(_FG + 1) * B_W] = gb

    tri = _tri_incl(chunk)
    c2 = 2 * chunk
    ri = _iota2((c2, c2), 0)
    ci = _iota2((c2, c2), 1)
    same_head = (ri // chunk) == (ci // chunk)
    incl_bd = same_head & (ri >= ci)
    strict_bd = same_head & (ri > ci)
    ri4 = _iota2((2 * c2, c2), 0)
    ci4 = _iota2((2 * c2, c2), 1)
    rr4 = ri4 % c2
    lm_mask = ((rr4 // chunk) == (ci4 // chunk)) & ((rr4 > ci4) | ((ri4 >= c2) & (rr4 == ci4)))
    lane_st = (_iota2((c2, LANES), 0) // chunk) == (_iota2((c2, LANES), 1) // B_N)

    def body(c):
        r0 = _row_start(c, chunk)
        rows = pl.ds(r0, chunk)

        def pair_chain(bi, p):
            def feat(slot):
                return feat_s[bi, rows, slot * B_W + p * LANES:slot * B_W + (p + 1) * LANES]
            r, lw, k, v, a_, b_, gate = (feat(s) for s in (_FR, _FLW, _FK, _FV, _FA, _FB, _FG))
            cs = _dot_exact_lhs(tri, lw)
            yield
            cs_last = cs[chunk - 1:chunk, :]
            e_inv = jnp.exp(-cs)
            e_rem = jnp.exp(cs_last - cs)
            at = a_ * jnp.exp(cs - lw)
            rt = r * jnp.exp(cs)
            bt = b_ * e_inv
            kt = k * e_inv
            a_st = jnp.concatenate([jnp.where(head0, at, 0.0), jnp.where(head0, 0.0, at)], axis=0)
            r_st = jnp.concatenate([jnp.where(head0, rt, 0.0), jnp.where(head0, 0.0, rt)], axis=0)
            b2 = jnp.concatenate([bt, bt], axis=0)
            k2 = jnp.concatenate([kt, kt], axis=0)
            v2 = jnp.concatenate([v, v], axis=0)
            ar_st = jnp.concatenate([a_st, r_st], axis=0)
            ar_b = _dot_nt(ar_st, b2)
            ar_k = _dot_nt(ar_st, k2)
            s = st_s[bi, p]
            ar = _dot_nt(jnp.concatenate([at, rt], axis=0), s)
            yield
            l_ab = jnp.where(strict_bd, ar_b[0:c2, :], 0.0)
            m_rb = jnp.where(incl_bd, ar_b[c2:2 * c2, :], 0.0)
            lm_k = jnp.where(lm_mask, ar_k, 0.0)
            lm_v = _dot(lm_k, v2)
            t_inv = yield from _inv_unit_lower(l_ab, chunk)
            ah = ar[0:chunk, :]
            rh = ar[chunk:c2, :]
            x_st = jnp.where(lane_st, jnp.concatenate([ah, ah], axis=0) + lm_v[0:c2, :], 0.0)
            u_st = _dot(t_inv, x_st)
            yield
            o_st = jnp.where(lane_st, _dot(m_rb, u_st) + lm_v[c2:2 * c2, :], 0.0)
            u = u_st[0:chunk, :] + u_st[chunk:c2, :]
            upd = _dot_tn(jnp.concatenate([u, v], axis=0),
                          jnp.concatenate([b_ * e_rem, k * e_rem], axis=0))
            yield
            o = rh + o_st[0:chunk, :] + o_st[chunk:c2, :]
            st_s[bi, p] = s * jnp.exp(cs_last) + jnp.where(ones_blk, upd, 0.0)
            sl = slice(p * LANES, (p + 1) * LANES)
            mean = _dot_exact_rhs(o, ones_blk) * (1.0 / B_N)
            bonus = _dot_exact_rhs(r * k * rk_ref[:, sl], ones_blk) * v
            yield
            d = o - mean
            var = _dot_exact_rhs(d * d, ones_blk) * (1.0 / B_N)
            yield
            ob = d * lax.rsqrt(var + B_GN_EPS) * lnw_ref[:, sl] + lnb_ref[:, sl] + bonus
            o_ref[bi, rows, sl] = ob * _silu(gate)

        _interleave(pair_chain(bi, p) for bi in range(nb) for p in range(B_PAIRS))

    _chunk_loop(n_chunks, body)

    @pl.when(i == pl.num_programs(1) - 1)
    def _():
        for bi in range(nb):
            for p in range(B_PAIRS):
                sout_ref[bi, 2 * p] = st_s[bi, p, 0:B_N, 0:B_N]
                shifted = pltpu.roll(st_s[bi, p, B_N:LANES, :], B_N, 1)
                sout_ref[bi, 2 * p + 1] = shifted[:, 0:B_N]


def _rwkv(proj3d, prev_row, s0, mu, w0, a0, w2a2, k_k, k_a, r_k, ln_w, ln_b, nb, tb, chunk):
    bsz, seq, _ = proj3d.shape
    blk = lambda b, i: (b, i, 0)
    per_b3 = lambda b, i: (b, 0, 0)
    per_b4 = lambda b, i: (b, 0, 0, 0)
    fix2 = lambda b, i: (0, 0)
    kern = functools.partial(_rwkv_kernel, nb=nb, tb=tb, chunk=chunk)
    vec = pl.BlockSpec((1, B_W), fix2)
    return pl.pallas_call(
        kern,
        grid=(bsz // nb, seq // tb),
        in_specs=[pl.BlockSpec((nb, tb, B_COLS), lambda b, i: (b, i, P0_B // B_COLS)),
                  pl.BlockSpec((nb, 1, B_COLS), per_b3),
                  pl.BlockSpec((nb, B_HEADS, B_N, B_N), per_b4),
                  pl.BlockSpec((1, B_COLS), fix2),
                  vec, vec,
                  pl.BlockSpec((2 * B_LORA, 2 * B_W), fix2),
                  vec, vec, vec, vec, vec],
        out_specs=[pl.BlockSpec((nb, tb, B_W), blk),
                   pl.BlockSpec((nb, B_HEADS, B_N, B_N), per_b4)],
        out_shape=[jax.ShapeDtypeStruct((bsz, seq, B_W), F32),
                   jax.ShapeDtypeStruct((bsz, B_HEADS, B_N, B_N), F32)],
        scratch_shapes=[pltpu.VMEM((nb, tb + SUBLANES, B_COLS), F32),
                        pltpu.VMEM((nb, tb, 7 * B_W), F32),
                        pltpu.VMEM((nb, B_PAIRS, LANES, LANES), F32),
                        pltpu.VMEM((B_N, LANES), F32)],
        compiler_params=pltpu.CompilerParams(dimension_semantics=("parallel", "arbitrary"),
                                             vmem_limit_bytes=VMEM_LIMIT),
        name="rwkv7_mixer",
    )(proj3d, prev_row, s0, mu, w0, a0, w2a2, k_k, k_a, r_k, ln_w, ln_b)


def _mlstm_kernel(p_ref, c0_ref, n0_ref, m0_ref, bias_ref, o_ref, cout_ref, nout_ref, mout_ref,
                  gsc_s, caug_s, m_s, *, nb, tb, chunk):
    i = pl.program_id(1)
    n_chunks = tb // chunk
    lane = _iota2((1, LANES), 1)
    krow = _iota2((LANES, 1), 0)
    ncol_mask = _iota2((LANES, LANES), 1) == 0

    @pl.when(i == 0)
    def _():
        for bi in range(nb):
            m0 = m0_ref[bi]
            for h in range(C_HEADS):
                j = h % 2
                caug_s[bi, h] = jnp.zeros((LANES, 2 * C_DV), F32)
                caug_s[bi, h, j * C_DQK:(j + 1) * C_DQK, 0:C_DV] = c0_ref[bi, h]
                n_col = _row_to_col(n0_ref[bi, h // 2:h // 2 + 1, :])
                own = (krow // C_DQK) == j
                caug_s[bi, h, :, C_DV:2 * C_DV] = jnp.where(ncol_mask & own, n_col, 0.0)
                m_s[bi, h:h + 1, :] = jnp.broadcast_to(m0[:, h:h + 1], (1, LANES))

    for bi in range(nb):
        g = p_ref[bi, :, P1_GATES:P1_PAD] + bias_ref[...]
        cap = C_GATE_CAP * jnp.tanh(g / C_GATE_CAP)
        gsc_s[bi] = jnp.where(lane < C_HEADS, cap, -_softplus(-cap))

    tri = _tri_incl(chunk)
    ones_col = jnp.where(_iota2((chunk, C_DV), 1) == 0, 1.0, 0.0)
    off = 2 * C_QK

    def body(c):
        r0 = _row_start(c, chunk)
        rows = pl.ds(r0, chunk)
        gates = [(gsc_s[bi, rows, :], _dot_exact_lhs(tri, gsc_s[bi, rows, :])) for bi in range(nb)]

        def head_chain(bi, h):
            gl, bcum = gates[bi]
            p, j = h // 2, h % 2
            own_lane = (lane // C_DQK) == j
            own_row = (krow // C_DQK) == j
            qp = p_ref[bi, rows, p * LANES:(p + 1) * LANES] * (C_DQK ** -0.5)
            kp = p_ref[bi, rows, C_QK + p * LANES:C_QK + (p + 1) * LANES]
            v = p_ref[bi, rows, off + h * C_DV:off + (h + 1) * C_DV]
            vaug = jnp.concatenate([v, ones_col], axis=1)
            qk = _dot_nt(jnp.where(own_lane, qp, 0.0), kp)
            cm = caug_s[bi, h]
            qc = _dot(qp, cm)
            b_col = bcum[:, C_HEADS + h:C_HEADS + h + 1]
            i_col = gl[:, h:h + 1]
            b_row = _col_to_row(b_col)
            i_row = _col_to_row(i_col)
            yield
            dmat = jnp.where(tri, b_col - b_row + i_row, -jnp.inf)
            dmax = jnp.max(dmat, axis=-1, keepdims=True)
            yield
            m = m_s[bi, h:h + 1, 0:1]
            inter = b_col + m
            mt = jnp.maximum(inter, dmax)
            m_new = mt[chunk - 1:chunk, :]
            b_last = b_col[chunk - 1:chunk, :]
            w_s = jnp.exp(b_last - b_col + i_col - m_new)
            e_c = jnp.exp(b_last + m - m_new)
            pm = jnp.exp(dmat - mt) * qk
            e_inter = jnp.exp(inter - mt)
            yield
            upd = _dot_tn(w_s * kp, vaug)
            num = e_inter * qc + _dot(pm, vaug)
            yield
            caug_s[bi, h] = e_c * cm + jnp.where(own_row, upd, 0.0)
            m_s[bi, h:h + 1, :] = jnp.broadcast_to(m_new, (1, LANES))
            den = num[:, C_DV:C_DV + 1]
            hh = num[:, 0:C_DV] / jnp.maximum(jnp.abs(den), jnp.exp(-mt))
            og = p_ref[bi, rows, off + C_V + h * C_DV:off + C_V + (h + 1) * C_DV]
            z = p_ref[bi, rows, off + 2 * C_V + h * C_DV:off + 2 * C_V + (h + 1) * C_DV]
            o_ref[bi, rows, h * C_DV:(h + 1) * C_DV] = hh * _sigmoid(og) * _silu(z)

        _interleave(head_chain(bi, h) for bi in range(nb) for h in range(C_HEADS))

    _chunk_loop(n_chunks, body)

    @pl.when(i == pl.num_programs(1) - 1)
    def _():
        for bi in range(nb):
            for h in range(C_HEADS):
                j = h % 2
                cout_ref[bi, h] = caug_s[bi, h, j * C_DQK:(j + 1) * C_DQK, 0:C_DV]
            for p in range(C_HEADS // 2):
                n_col = caug_s[bi, 2 * p, :, C_DV:C_DV + 1] + caug_s[bi, 2 * p + 1, :, C_DV:C_DV + 1]
                nout_ref[bi, p:p + 1, :] = _col_to_row(n_col)
        mout_ref[...] = m_s[...]


def _mlstm(proj3d, c0, n0, m0, bias_row, nb, tb, chunk):
    bsz, seq, _ = proj3d.shape
    blk = lambda b, i: (b, i, 0)
    per_b3 = lambda b, i: (b, 0, 0)
    per_b4 = lambda b, i: (b, 0, 0, 0)
    kern = functools.partial(_mlstm_kernel, nb=nb, tb=tb, chunk=chunk)
    pairs = C_HEADS // 2
    return pl.pallas_call(
        kern,
        grid=(bsz // nb, seq // tb),
        in_specs=[pl.BlockSpec((nb, tb, P1_PAD), blk),
                  pl.BlockSpec((nb, C_HEADS, C_DQK, C_DV), per_b4),
                  pl.BlockSpec((nb, pairs, LANES), per_b3),
                  pl.BlockSpec((nb, 1, C_HEADS), per_b3),
                  pl.BlockSpec((1, LANES), lambda b, i: (0, 0))],
        out_specs=[pl.BlockSpec((nb, tb, C_V), blk),
                   pl.BlockSpec((nb, C_HEADS, C_DQK, C_DV), per_b4),
                   pl.BlockSpec((nb, pairs, LANES), per_b3),
                   pl.BlockSpec((nb, C_HEADS, LANES), per_b3)],
        out_shape=[jax.ShapeDtypeStruct((bsz, seq, C_V), F32),
                   jax.ShapeDtypeStruct((bsz, C_HEADS, C_DQK, C_DV), F32),
                   jax.ShapeDtypeStruct((bsz, pairs, LANES), F32),
                   jax.ShapeDtypeStruct((bsz, C_HEADS, LANES), F32)],
        scratch_shapes=[pltpu.VMEM((nb, tb, LANES), F32),
                        pltpu.VMEM((nb, C_HEADS, LANES, 2 * C_DV), F32),
                        pltpu.VMEM((nb, C_HEADS, LANES), F32)],
        compiler_params=pltpu.CompilerParams(dimension_semantics=("parallel", "arbitrary"),
                                             vmem_limit_bytes=VMEM_LIMIT),
        name="mlstm_mixer",
    )(proj3d, c0, n0, m0, bias_row)


def _pad_lanes(vec, offset):
    out = jnp.zeros((1, LANES), F32)
    return out.at[0, offset:offset + vec.shape[0]].set(vec.astype(F32))


def _prepare_params(p):
    w_in0 = p["w_in0"]
    a_cols = A_MAIN + 2 * A_HEADS
    w0 = jnp.zeros((D_MODEL, P0_PAD), F32)
    w0 = w0.at[:, 0:a_cols].set(w_in0[:, 0:a_cols])
    w0 = w0.at[:, P0_B:P0_PAD].set(w_in0[:, a_cols:])
    w_in1 = p["w_in1"]
    w1 = jnp.zeros((D_MODEL, P1_PAD), F32).at[:, 0:w_in1.shape[1]].set(w_in1)
    w2a2 = jnp.zeros((2 * B_LORA, 2 * B_W), F32)
    w2a2 = w2a2.at[0:B_LORA, 0:B_W].set(p["rwkv_w2"]).at[B_LORA:, B_W:].set(p["rwkv_a2"])
    row = lambda v: v.reshape(1, -1).astype(F32)
    return dict(
        w0=w0.astype(BF16), w1=w1.astype(BF16), wb=w_in0[:, a_cols:].astype(BF16),
        woa=p["w_out0"][0:A_VAL].astype(BF16), wob=p["w_out0"][A_VAL:].astype(BF16),
        wo1=p["w_out1"].astype(BF16), w2a2=w2a2.astype(BF16),
        norm0=row(p["norm0_w"]), norm1=row(p["norm1_w"]), normf=row(p["norm_f_w"]),
        conv_w=p["gdn_conv_w"].astype(F32),
        alog=_pad_lanes(p["gdn_a_log"], A_HEADS), dtb=_pad_lanes(p["gdn_dt_bias"], A_HEADS),
        gnw=row(p["gdn_norm_w"]),
        mu=row(p["rwkv_mu"]), rw0=row(p["rwkv_w0"]), ra0=row(p["rwkv_a0"]),
        k_k=row(p["rwkv_k_k"]), k_a=row(p["rwkv_k_a"]), r_k=row(p["rwkv_r_k"]),
        ln_w=row(p["rwkv_ln_w"]), ln_b=row(p["rwkv_ln_b"]),
        gate_bias=_pad_lanes(jnp.concatenate([p["mlstm_b_i"], p["mlstm_b_f"]]), 0),
    )


def _tiles(bsz, seq):
    chunk = math.gcd(seq, MAX_CHUNK)
    tb = min(seq, 256)
    nb = 2 if seq > MAX_CHUNK else 4
    nb = math.gcd(nb, bsz)
    tm = min(bsz * seq, 512)
    return chunk, tb, nb, tm


def _trunk(x, state, q):
    conv0, s_gdn, x_prev, s_rwkv, c0, n0, m0 = state
    bsz, seq, _ = x.shape
    chunk, tb, nb, tm = _tiles(bsz, seq)
    m = bsz * seq
    x2d = x.reshape(m, D_MODEL)

    proj0 = _norm_proj(x2d, q["norm0"], q["w0"], tm).reshape(bsz, seq, P0_PAD)
    xn_last, prev_row = _aux(x[:, -1, :], x_prev, q["norm0"], q["wb"])
    o_a, conv_new, s_gdn_new = _gdn(proj0, conv0, s_gdn, q["conv_w"], q["alog"], q["dtb"], q["gnw"],
                                    nb, tb, chunk)
    o_b, s_rwkv_new = _rwkv(proj0, prev_row.reshape(bsz, 1, B_COLS), s_rwkv, q["mu"], q["rw0"], q["ra0"],
                            q["w2a2"], q["k_k"], q["k_a"], q["r_k"], q["ln_w"], q["ln_b"],
                            nb, tb, chunk)
    x1, proj1 = _mid(x2d, o_a.reshape(m, A_VAL), o_b.reshape(m, B_W), q["woa"], q["wob"],
                     q["norm1"], q["w1"], tm)
    h, c_new, n_new, m_new = _mlstm(proj1.reshape(bsz, seq, P1_PAD), c0,
                                    n0.reshape(bsz, C_HEADS // 2, LANES),
                                    m0.reshape(bsz, 1, C_HEADS), q["gate_bias"], nb, tb, chunk)
    y = _final(x1, h.reshape(m, C_V), q["wo1"], q["normf"], tm)
    return y.reshape(bsz, seq, D_MODEL), (conv_new, s_gdn_new, xn_last, s_rwkv_new, c_new,
                                          n_new.reshape(bsz, C_HEADS, C_DQK), m_new[:, :, 0])


def kernel(x_prompt, x_sample, state_gdn_conv, state_gdn, state_rwkv_shift, state_rwkv,
           state_mlstm_c, state_mlstm_n, state_mlstm_m,
           norm0_w, w_in0, w_out0, gdn_conv_w, gdn_a_log, gdn_dt_bias, gdn_norm_w,
           rwkv_mu, rwkv_w0, rwkv_w2, rwkv_a0, rwkv_a2, rwkv_k_k, rwkv_k_a, rwkv_r_k,
           rwkv_ln_w, rwkv_ln_b, norm1_w, w_in1, w_out1, mlstm_b_i, mlstm_b_f, norm_f_w):
    q = _prepare_params(dict(
        norm0_w=norm0_w, w_in0=w_in0, w_out0=w_out0, gdn_conv_w=gdn_conv_w, gdn_a_log=gdn_a_log,
        gdn_dt_bias=gdn_dt_bias, gdn_norm_w=gdn_norm_w, rwkv_mu=rwkv_mu, rwkv_w0=rwkv_w0,
        rwkv_w2=rwkv_w2, rwkv_a0=rwkv_a0, rwkv_a2=rwkv_a2, rwkv_k_k=rwkv_k_k, rwkv_k_a=rwkv_k_a,
        rwkv_r_k=rwkv_r_k, rwkv_ln_w=rwkv_ln_w, rwkv_ln_b=rwkv_ln_b, norm1_w=norm1_w, w_in1=w_in1,
        w_out1=w_out1, mlstm_b_i=mlstm_b_i, mlstm_b_f=mlstm_b_f, norm_f_w=norm_f_w))
    bsz = x_prompt.shape[0]
    prompt_state = (
        jnp.zeros((bsz, A_CONV - 1, A_QKV), F32),
        jnp.zeros((bsz, A_HEADS, A_DK, A_DV), F32),
        jnp.zeros((bsz, D_MODEL), F32),
        jnp.zeros((bsz, B_HEADS, B_N, B_N), F32),
        jnp.zeros((bsz, C_HEADS, C_DQK, C_DV), F32),
        jnp.zeros((bsz, C_HEADS, C_DQK), F32),
        jnp.zeros((bsz, C_HEADS), F32),
    )
    y_p, sp = _trunk(x_prompt, prompt_state, q)
    sample_state = (state_gdn_conv, state_gdn, state_rwkv_shift, state_rwkv,
                    state_mlstm_c, state_mlstm_n, state_mlstm_m)
    y_s, ss = _trunk(x_sample, sample_state, q)
    return (y_p, y_s) + tuple(sp) + tuple(ss)
```

```python
import functools
import math

import jax
import jax.numpy as jnp
from jax import lax
from jax.experimental import pallas as pl
from jax.experimental.pallas import tpu as pltpu

F32 = jnp.float32
BF16 = jnp.bfloat16

D_MODEL = 1024
EPS = 1e-6
LANES = 128
SUBLANES = 8
A_HEADS, A_DK, A_DV, A_CONV = 4, 128, 128, 4
A_KEY = A_HEADS * A_DK
A_VAL = A_HEADS * A_DV
A_QKV = 2 * A_KEY + A_VAL
A_MAIN = A_QKV + A_VAL
B_HEADS, B_N, B_LORA = 8, 64, 64
B_W = B_HEADS * B_N
B_COLS = 4 * B_W + 2 * B_LORA
B_PAIRS = B_HEADS // 2
B_GN_EPS = 64e-5
B_DECAY_SCALE = math.exp(-0.5)
C_HEADS, C_DQK, C_DV = 8, 64, 128
C_QK = C_HEADS * C_DQK
C_V = C_HEADS * C_DV
C_GATE_CAP = 15.0
P0_GATES = A_MAIN
P0_B = A_MAIN + LANES
P0_PAD = P0_B + B_COLS
P1_GATES = 2 * C_QK + 3 * C_V
P1_PAD = P1_GATES + LANES
MAX_CHUNK = 64
VMEM_LIMIT = 56 * 1024 * 1024


def _dot(a, b):
    return jnp.dot(a.astype(BF16), b.astype(BF16), preferred_element_type=F32)


def _dot_nt(a, b):
    return lax.dot_general(a.astype(BF16), b.astype(BF16), (((1,), (1,)), ((), ())),
                           preferred_element_type=F32)


def _dot_tn(a, b):
    return lax.dot_general(a.astype(BF16), b.astype(BF16), (((0,), (0,)), ((), ())),
                           preferred_element_type=F32)


def _split(x, terms):
    out = []
    for _ in range(terms - 1):
        hi = x.astype(BF16)
        out.append(hi)
        x = x - hi.astype(F32)
    out.append(x.astype(BF16))
    return out


def _dot_exact_lhs(sel, x, terms=3):
    s = sel.astype(BF16)
    return jnp.dot(jnp.concatenate([s] * terms, axis=1), jnp.concatenate(_split(x, terms), axis=0),
                   preferred_element_type=F32)


def _dot_exact_rhs(x, sel, terms=2):
    s = sel.astype(BF16)
    return jnp.dot(jnp.concatenate(_split(x, terms), axis=1), jnp.concatenate([s] * terms, axis=0),
                   preferred_element_type=F32)


def _iota2(shape, dim):
    return lax.broadcasted_iota(jnp.int32, shape, dim)


def _tri_incl(n):
    return (_iota2((n, n), 0) >= _iota2((n, n), 1))


def _col_to_row(col):
    n = col.shape[0]
    eye = _iota2((n, n), 0) == _iota2((n, n), 1)
    return jnp.sum(jnp.where(eye, col, 0.0), axis=0, keepdims=True)


def _row_to_col(row):
    n = row.shape[1]
    eye = _iota2((n, n), 0) == _iota2((n, n), 1)
    return jnp.sum(jnp.where(eye, row, 0.0), axis=1, keepdims=True)


GATE_COPIES = 3


def _replicate(x, sel, group):
    hi, mid, lo = _split(x, GATE_COPIES)
    lane = _iota2(x.shape, 1)
    packed = jnp.where(lane < group, hi, jnp.where(lane < 2 * group, mid, lo))
    packed = jnp.where(lane < GATE_COPIES * group, packed, jnp.zeros_like(packed))
    return jnp.dot(packed, sel, preferred_element_type=F32)


def _replicate_selector(group):
    assert group & (group - 1) == 0 and LANES == 128
    k = _iota2((LANES, group * LANES), 0)
    col = _iota2((LANES, group * LANES), 1)
    hit = (k < GATE_COPIES * group) & (jnp.bitwise_and(k, group - 1) == jnp.right_shift(col, 7))
    return jnp.where(hit, 1.0, 0.0).astype(BF16)


def _cummax_rows(x):
    n = x.shape[0]
    row = _iota2(x.shape, 0)
    sh = 1
    while sh < n:
        x = jnp.maximum(x, jnp.where(row >= sh, pltpu.roll(x, sh, 0), -jnp.inf))
        sh *= 2
    return x


def _inv_unit_lower(p, levels):
    n = p.shape[0]
    eye = (_iota2((n, n), 0) == _iota2((n, n), 1)).astype(F32)
    t = eye + p
    m = 2
    if m < levels:
        p = _dot(p, p)
        yield
    while m < levels:
        m *= 2
        if m < levels:
            both = _dot(jnp.concatenate([p, t], axis=0), p)
            p, t = both[0:n, :], t + both[n:2 * n, :]
            yield
        else:
            t = t + _dot(t, p)
    return t


def _interleave(chains):
    live = list(chains)
    while live:
        nxt = []
        for g in live:
            try:
                next(g)
                nxt.append(g)
            except StopIteration:
                pass
        live = nxt


def _sigmoid(x):
    return 1.0 / (1.0 + jnp.exp(-x))


def _silu(x):
    return x * _sigmoid(x)


def _softplus(x):
    return jnp.maximum(x, 0.0) + jnp.log1p(jnp.exp(-jnp.abs(x)))


def _rms(x, w):
    return x * lax.rsqrt(jnp.mean(x * x, axis=-1, keepdims=True) + EPS) * w


def _chunk_loop(n_chunks, body):
    if n_chunks == 1:
        body(0)
    else:
        def step(c, carry):
            body(c)
            return carry
        lax.fori_loop(0, n_chunks, step, 0)


def _row_start(c, chunk):
    if isinstance(c, int):
        return c * chunk
    return pl.multiple_of(c * chunk, chunk)


def _resident(shape):
    zeros = (0,) * len(shape)
    return pl.BlockSpec(shape, lambda *_: zeros, pipeline_mode=pl.Buffered(1))


def _norm_proj_kernel(x_ref, nw_ref, w_ref, o_ref):
    xn = _rms(x_ref[...], nw_ref[...])
    o_ref[...] = jnp.dot(xn.astype(BF16), w_ref[...], preferred_element_type=F32)


def _norm_proj(x2d, norm_w, w_bf16, tm):
    m = x2d.shape[0]
    n = w_bf16.shape[1]
    return pl.pallas_call(
        _norm_proj_kernel,
        grid=(m // tm,),
        in_specs=[pl.BlockSpec((tm, D_MODEL), lambda i: (i, 0)),
                  _resident((1, D_MODEL)),
                  _resident((D_MODEL, n))],
        out_specs=pl.BlockSpec((tm, n), lambda i: (i, 0)),
        out_shape=jax.ShapeDtypeStruct((m, n), F32),
        compiler_params=pltpu.CompilerParams(dimension_semantics=("parallel",),
                                             vmem_limit_bytes=VMEM_LIMIT),
        name="norm_proj0",
    )(x2d, norm_w, w_bf16)


def _mid_kernel(x_ref, oa_ref, ob_ref, woa_ref, wob_ref, nw_ref, wi_ref, x1_ref, p_ref):
    y = jnp.dot(oa_ref[...].astype(BF16), woa_ref[...], preferred_element_type=F32)
    y = y + jnp.dot(ob_ref[...].astype(BF16), wob_ref[...], preferred_element_type=F32)
    x1 = x_ref[...] + y
    x1_ref[...] = x1
    xn = _rms(x1, nw_ref[...])
    p_ref[...] = jnp.dot(xn.astype(BF16), wi_ref[...], preferred_element_type=F32)


def _mid(x2d, oa, ob, woa, wob, norm_w, wi, tm):
    m = x2d.shape[0]
    n = wi.shape[1]
    row = lambda i: (i, 0)
    return pl.pallas_call(
        _mid_kernel,
        grid=(m // tm,),
        in_specs=[pl.BlockSpec((tm, D_MODEL), row),
                  pl.BlockSpec((tm, A_VAL), row),
                  pl.BlockSpec((tm, B_W), row),
                  _resident((A_VAL, D_MODEL)),
                  _resident((B_W, D_MODEL)),
                  _resident((1, D_MODEL)),
                  _resident((D_MODEL, n))],
        out_specs=[pl.BlockSpec((tm, D_MODEL), row), pl.BlockSpec((tm, n), row)],
        out_shape=[jax.ShapeDtypeStruct((m, D_MODEL), F32), jax.ShapeDtypeStruct((m, n), F32)],
        compiler_params=pltpu.CompilerParams(dimension_semantics=("parallel",),
                                             vmem_limit_bytes=VMEM_LIMIT),
        name="out0_norm_proj1",
    )(x2d, oa, ob, woa, wob, norm_w, wi)


def _final_kernel(x_ref, h_ref, wo_ref, nw_ref, y_ref):
    x2 = x_ref[...] + jnp.dot(h_ref[...].astype(BF16), wo_ref[...], preferred_element_type=F32)
    y_ref[...] = _rms(x2, nw_ref[...])


def _final(x2d, h, wo, norm_w, tm):
    m = x2d.shape[0]
    row = lambda i: (i, 0)
    return pl.pallas_call(
        _final_kernel,
        grid=(m // tm,),
        in_specs=[pl.BlockSpec((tm, D_MODEL), row),
                  pl.BlockSpec((tm, C_V), row),
                  _resident((C_V, D_MODEL)),
                  _resident((1, D_MODEL))],
        out_specs=pl.BlockSpec((tm, D_MODEL), row),
        out_shape=jax.ShapeDtypeStruct((m, D_MODEL), F32),
        compiler_params=pltpu.CompilerParams(dimension_semantics=("parallel",),
                                             vmem_limit_bytes=VMEM_LIMIT),
        name="out1_norm_f",
    )(x2d, h, wo, norm_w)


def _aux_kernel(xl_ref, xp_ref, nw_ref, wb_ref, xn_ref, prev_ref):
    xn_ref[...] = _rms(xl_ref[...], nw_ref[...])
    prev_ref[...] = jnp.dot(xp_ref[...].astype(BF16), wb_ref[...], preferred_element_type=F32)


def _aux(x_last, x_prev, norm_w, wb):
    b = x_last.shape[0]
    return pl.pallas_call(
        _aux_kernel,
        out_shape=[jax.ShapeDtypeStruct((b, D_MODEL), F32), jax.ShapeDtypeStruct((b, B_COLS), F32)],
        compiler_params=pltpu.CompilerParams(vmem_limit_bytes=VMEM_LIMIT),
        name="shift_rows",
    )(x_last, x_prev, norm_w, wb)


def _gdn_kernel(main_ref, gates_ref, conv0_ref, s0_ref, convw_ref, alog_ref, dtb_ref, gnw_ref,
                o_ref, convn_ref, sout_ref, ext_s, qkv_s, gsc_s, st_s, *, nb, tb, chunk):
    i = pl.program_id(1)
    n_chunks = tb // chunk
    tail = SUBLANES

    @pl.when(i == 0)
    def _():
        for bi in range(nb):
            ext_s[bi, 0:tail, :] = jnp.zeros((tail, A_QKV), F32)
            ext_s[bi, tail - (A_CONV - 1):tail, :] = conv0_ref[bi]
        st_s[...] = s0_ref[...]

    @pl.when(i > 0)
    def _():
        for bi in range(nb):
            ext_s[bi, 0:tail, :] = ext_s[bi, tb:tb + tail, :]

    cw = convw_ref[...]
    for bi in range(nb):
        ext_s[bi, tail:tail + tb, :] = main_ref[bi, :, 0:A_QKV]
        for c in range(n_chunks):
            r0 = c * chunk
            y = ext_s[bi, tail + r0:tail + r0 + chunk, :] * cw[A_CONV - 1:A_CONV, :]
            for j in range(A_CONV - 1):
                sh = A_CONV - 1 - j
                y = y + ext_s[bi, tail + r0 - sh:tail + r0 - sh + chunk, :] * cw[j:j + 1, :]
            y = _silu(y)
            for h in range(A_HEADS):
                q = y[:, h * A_DK:(h + 1) * A_DK]
                qkv_s[bi, r0:r0 + chunk, h * A_DK:(h + 1) * A_DK] = (
                    q * lax.rsqrt(jnp.sum(q * q, axis=-1, keepdims=True) + EPS) * (A_DK ** -0.5))
                k = y[:, A_KEY + h * A_DK:A_KEY + (h + 1) * A_DK]
                qkv_s[bi, r0:r0 + chunk, A_KEY + h * A_DK:A_KEY + (h + 1) * A_DK] = (
                    k * lax.rsqrt(jnp.sum(k * k, axis=-1, keepdims=True) + EPS))
            qkv_s[bi, r0:r0 + chunk, 2 * A_KEY:A_QKV] = y[:, 2 * A_KEY:A_QKV]
        g = gates_ref[bi]
        gsc_s[bi, :, 0:LANES] = _sigmoid(g)
        gsc_s[bi, :, LANES:2 * LANES] = -jnp.exp(alog_ref[...]) * _softplus(g + dtb_ref[...])

    n_gates = 2 * A_HEADS
    is_decay = jnp.bitwise_and(_iota2((1, LANES), 1), n_gates - 1) >= A_HEADS
    sel = _replicate_selector(n_gates)
    tri = _tri_incl(chunk)
    strict = _iota2((chunk, chunk), 0) > _iota2((chunk, chunk), 1)
    gnw = gnw_ref[...]

    def body(c):
        r0 = _row_start(c, chunk)
        rows = pl.ds(r0, chunk)
        beta = [gsc_s[bi, rows, 0:LANES] for bi in range(nb)]
        gcum = [_dot_exact_lhs(tri, gsc_s[bi, rows, LANES:2 * LANES]) for bi in range(nb)]
        rep = [_replicate(jnp.where(is_decay, g, b), sel, n_gates) for b, g in zip(beta, gcum)]

        def head_chain(bi, h):
            q = qkv_s[bi, rows, h * A_DK:(h + 1) * A_DK]
            k = qkv_s[bi, rows, A_KEY + h * A_DK:A_KEY + (h + 1) * A_DK]
            v = qkv_s[bi, rows, 2 * A_KEY + h * A_DV:2 * A_KEY + (h + 1) * A_DV]
            qk_k = _dot_nt(jnp.concatenate([q, k], axis=0), k)
            b_rep = rep[bi][:, h * LANES:(h + 1) * LANES]
            g_rep = rep[bi][:, (A_HEADS + h) * LANES:(A_HEADS + h + 1) * LANES]
            g_row = _col_to_row(g_rep[:, 0:chunk])
            g_last = g_rep[chunk - 1:chunk, :]
            decay = jnp.where(tri, jnp.exp(jnp.where(tri, g_rep[:, 0:chunk] - g_row, 0.0)), 0.0)
            eg = jnp.exp(g_rep)
            s = st_s[bi, h]
            qs = _dot(q * eg, s)
            yield
            attn = decay * qk_k[0:chunk, :]
            amat = jnp.where(strict, b_rep[:, 0:chunk] * decay * qk_k[chunk:2 * chunk, :], 0.0)
            t_inv = yield from _inv_unit_lower(-amat, chunk)
            sol = _dot(t_inv, jnp.concatenate([b_rep * v, (b_rep * eg) * k], axis=1))
            yield
            nv = sol[:, 0:A_DV] - _dot(sol[:, A_DV:A_DV + A_DK], s)
            yield
            o = qs + _dot(attn, nv)
            st_s[bi, h] = jnp.exp(g_last) * s + _dot_tn(k * jnp.exp(g_last - g_rep), nv)
            z = main_ref[bi, rows, A_QKV + h * A_DV:A_QKV + (h + 1) * A_DV]
            o_ref[bi, rows, h * A_DV:(h + 1) * A_DV] = _rms(o, gnw) * _silu(z)

        _interleave(head_chain(bi, h) for bi in range(nb) for h in range(A_HEADS))

    _chunk_loop(n_chunks, body)

    @pl.when(i == pl.num_programs(1) - 1)
    def _():
        for bi in range(nb):
            convn_ref[bi] = ext_s[bi, tb + tail - (A_CONV - 1):tb + tail, :]
        sout_ref[...] = st_s[...]


def _gdn(proj3d, conv0, s0, conv_w, alog_row, dtb_row, gn_w, nb, tb, chunk):
    bsz, seq, _ = proj3d.shape
    blk = lambda b, i: (b, i, 0)
    per_b3 = lambda b, i: (b, 0, 0)
    per_b4 = lambda b, i: (b, 0, 0, 0)
    fix2 = lambda b, i: (0, 0)
    kern = functools.partial(_gdn_kernel, nb=nb, tb=tb, chunk=chunk)
    return pl.pallas_call(
        kern,
        grid=(bsz // nb, seq // tb),
        in_specs=[pl.BlockSpec((nb, tb, A_MAIN), blk),
                  pl.BlockSpec((nb, tb, LANES), lambda b, i: (b, i, P0_GATES // LANES)),
                  pl.BlockSpec((nb, A_CONV - 1, A_QKV), per_b3),
                  pl.BlockSpec((nb, A_HEADS, A_DK, A_DV), per_b4),
                  pl.BlockSpec((A_CONV, A_QKV), fix2),
                  pl.BlockSpec((1, LANES), fix2),
                  pl.BlockSpec((1, LANES), fix2),
                  pl.BlockSpec((1, A_DV), fix2)],
        out_specs=[pl.BlockSpec((nb, tb, A_VAL), blk),
                   pl.BlockSpec((nb, A_CONV - 1, A_QKV), per_b3),
                   pl.BlockSpec((nb, A_HEADS, A_DK, A_DV), per_b4)],
        out_shape=[jax.ShapeDtypeStruct((bsz, seq, A_VAL), F32),
                   jax.ShapeDtypeStruct((bsz, A_CONV - 1, A_QKV), F32),
                   jax.ShapeDtypeStruct((bsz, A_HEADS, A_DK, A_DV), F32)],
        scratch_shapes=[pltpu.VMEM((nb, tb + SUBLANES, A_QKV), F32),
                        pltpu.VMEM((nb, tb, A_QKV), F32),
                        pltpu.VMEM((nb, tb, 2 * LANES), F32),
                        pltpu.VMEM((nb, A_HEADS, A_DK, A_DV), F32)],
        compiler_params=pltpu.CompilerParams(dimension_semantics=("parallel", "arbitrary"),
                                             vmem_limit_bytes=VMEM_LIMIT),
        name="gdn_mixer",
    )(proj3d, proj3d, conv0, s0, conv_w, alog_row, dtb_row, gn_w)


_FR, _FLW, _FK, _FV, _FA, _FB, _FG = range(7)


def _rwkv_kernel(pb_ref, prev_ref, s0_ref, mu_ref, w0_ref, a0_ref, w2a2_ref, kk_ref, ka_ref,
                 rk_ref, lnw_ref, lnb_ref, o_ref, sout_ref, ext_s, feat_s, st_s, tmp_s,
                 *, nb, tb, chunk):
    i = pl.program_id(1)
    n_chunks = tb // chunk
    tail = SUBLANES
    lane = _iota2((1, LANES), 1)
    head0 = lane < B_N
    ones_blk = (_iota2((LANES, LANES), 0) // B_N == _iota2((LANES, LANES), 1) // B_N)

    @pl.when(i == 0)
    def _():
        for bi in range(nb):
            ext_s[bi, 0:tail, :] = jnp.zeros((tail, B_COLS), F32)
            ext_s[bi, tail - 1:tail, :] = prev_ref[bi]
            for p in range(B_PAIRS):
                tmp_s[...] = jnp.zeros((B_N, LANES), F32)
                tmp_s[:, 0:B_N] = s0_ref[bi, 2 * p + 1]
                shifted = pltpu.roll(tmp_s[...], B_N, 1)
                st_s[bi, p, B_N:LANES, :] = jnp.where(head0, 0.0, shifted)
                tmp_s[:, 0:B_N] = s0_ref[bi, 2 * p]
                st_s[bi, p, 0:B_N, :] = tmp_s[...]

    @pl.when(i > 0)
    def _():
        for bi in range(nb):
            ext_s[bi, 0:tail, :] = ext_s[bi, tb:tb + tail, :]

    mu = mu_ref[...]
    for bi in range(nb):
        ext_s[bi, tail:tail + tb, :] = pb_ref[bi]
        for c in range(n_chunks):
            r0 = c * chunk
            pb = ext_s[bi, tail + r0:tail + r0 + chunk, :]
            pb_prev = ext_s[bi, tail + r0 - 1:tail + r0 - 1 + chunk, :]
            pbs = pb + (pb_prev - pb) * mu
            r = pbs[:, 0:B_W]
            kb = pbs[:, B_W:2 * B_W]
            vb = pbs[:, 2 * B_W:3 * B_W]
            gb = pbs[:, 3 * B_W:4 * B_W]
            lo = pbs[:, 4 * B_W:B_COLS]
            lo = jnp.where(head0, jnp.tanh(lo), lo)
            wa = _dot(lo, w2a2_ref[...])
            logw = -B_DECAY_SCALE * _sigmoid(w0_ref[...] + wa[:, 0:B_W])
            a = _sigmoid(a0_ref[...] + wa[:, B_W:2 * B_W])
            kkx = kb * kk_ref[...]
            rws = slice(r0, r0 + chunk)
            for p in range(B_PAIRS):
                sl = slice(p * LANES, (p + 1) * LANES)
                kp = kkx[:, sl]
                ss = _dot_exact_rhs(kp * kp, ones_blk)
                kkn = kp * lax.rsqrt(ss + EPS)
                feat_s[bi, rws, _FA * B_W + p * LANES:_FA * B_W + (p + 1) * LANES] = -kkn
                feat_s[bi, rws, _FB * B_W + p * LANES:_FB * B_W + (p + 1) * LANES] = kkn * a[:, sl]
            feat_s[bi, rws, _FR * B_W:(_FR + 1) * B_W] = r
            feat_s[bi, rws, _FLW * B_W:(_FLW + 1) * B_W] = logw
            feat_s[bi, rws, _FK * B_W:(_FK + 1) * B_W] = kb * (1.0 + (a - 1.0) * ka_ref[...])
            feat_s[bi, rws, _FV * B_W:(_FV + 1) * B_W] = vb
            feat_s[bi, rws, _FG * B_W:(_FG + 1) * B_W] = gb

    tri = _tri_incl(chunk)
    c2 = 2 * chunk
    ri = _iota2((c2, c2), 0)
    ci = _iota2((c2, c2), 1)
    same_head = (ri // chunk) == (ci // chunk)
    incl_bd = same_head & (ri >= ci)
    strict_bd = same_head & (ri > ci)
    ri4 = _iota2((2 * c2, c2), 0)
    ci4 = _iota2((2 * c2, c2), 1)
    rr4 = ri4 % c2
    lm_mask = ((rr4 // chunk) == (ci4 // chunk)) & ((rr4 > ci4) | ((ri4 >= c2) & (rr4 == ci4)))
    lane_st = (_iota2((c2, LANES), 0) // chunk) == (_iota2((c2, LANES), 1) // B_N)

    def body(c):
        r0 = _row_start(c, chunk)
        rows = pl.ds(r0, chunk)

        def pair_chain(bi, p):
            def feat(slot):
                return feat_s[bi, rows, slot * B_W + p * LANES:slot * B_W + (p + 1) * LANES]
            r, lw, k, v, a_, b_, gate = (feat(s) for s in (_FR, _FLW, _FK, _FV, _FA, _FB, _FG))
            cs = _dot_exact_lhs(tri, lw)
            yield
            cs_last = cs[chunk - 1:chunk, :]
            e_inv = jnp.exp(-cs)
            e_rem = jnp.exp(cs_last - cs)
            at = a_ * jnp.exp(cs - lw)
            rt = r * jnp.exp(cs)
            bt = b_ * e_inv
            kt = k * e_inv
            a_st = jnp.concatenate([jnp.where(head0, at, 0.0), jnp.where(head0, 0.0, at)], axis=0)
            r_st = jnp.concatenate([jnp.where(head0, rt, 0.0), jnp.where(head0, 0.0, rt)], axis=0)
            b2 = jnp.concatenate([bt, bt], axis=0)
            k2 = jnp.concatenate([kt, kt], axis=0)
            v2 = jnp.concatenate([v, v], axis=0)
            ar_st = jnp.concatenate([a_st, r_st], axis=0)
            ar_b = _dot_nt(ar_st, b2)
            ar_k = _dot_nt(ar_st, k2)
            s = st_s[bi, p]
            ar = _dot_nt(jnp.concatenate([at, rt], axis=0), s)
            yield
            l_ab = jnp.where(strict_bd, ar_b[0:c2, :], 0.0)
            m_rb = jnp.where(incl_bd, ar_b[c2:2 * c2, :], 0.0)
            lm_k = jnp.where(lm_mask, ar_k, 0.0)
            lm_v = _dot(lm_k, v2)
            t_inv = yield from _inv_unit_lower(l_ab, chunk)
            ah = ar[0:chunk, :]
            rh = ar[chunk:c2, :]
            x_st = jnp.where(lane_st, jnp.concatenate([ah, ah], axis=0) + lm_v[0:c2, :], 0.0)
            u_st = _dot(t_inv, x_st)
            yield
            o_st = jnp.where(lane_st, _dot(m_rb, u_st) + lm_v[c2:2 * c2, :], 0.0)
            u = u_st[0:chunk, :] + u_st[chunk:c2, :]
            upd = _dot_tn(jnp.concatenate([u, v], axis=0),
                          jnp.concatenate([b_ * e_rem, k * e_rem], axis=0))
            yield
            o = rh + o_st[0:chunk, :] + o_st[chunk:c2, :]
            st_s[bi, p] = s * jnp.exp(cs_last) + jnp.where(ones_blk, upd, 0.0)
            sl = slice(p * LANES, (p + 1) * LANES)
            mean = _dot_exact_rhs(o, ones_blk) * (1.0 / B_N)
            bonus = _dot_exact_rhs(r * k * rk_ref[:, sl], ones_blk) * v
            yield
            d = o - mean
            var = _dot_exact_rhs(d * d, ones_blk) * (1.0 / B_N)
            yield
            ob = d * lax.rsqrt(var + B_GN_EPS) * lnw_ref[:, sl] + lnb_ref[:, sl] + bonus
            o_ref[bi, rows, sl] = ob * _silu(gate)

        _interleave(pair_chain(bi, p) for bi in range(nb) for p in range(B_PAIRS))

    _chunk_loop(n_chunks, body)

    @pl.when(i == pl.num_programs(1) - 1)
    def _():
        for bi in range(nb):
            for p in range(B_PAIRS):
                sout_ref[bi, 2 * p] = st_s[bi, p, 0:B_N, 0:B_N]
                shifted = pltpu.roll(st_s[bi, p, B_N:LANES, :], B_N, 1)
                sout_ref[bi, 2 * p + 1] = shifted[:, 0:B_N]


def _rwkv(proj3d, prev_row, s0, mu, w0, a0, w2a2, k_k, k_a, r_k, ln_w, ln_b, nb, tb, chunk):
    bsz, seq, _ = proj3d.shape
    blk = lambda b, i: (b, i, 0)
    per_b3 = lambda b, i: (b, 0, 0)
    per_b4 = lambda b, i: (b, 0, 0, 0)
    fix2 = lambda b, i: (0, 0)
    kern = functools.partial(_rwkv_kernel, nb=nb, tb=tb, chunk=chunk)
    vec = pl.BlockSpec((1, B_W), fix2)
    return pl.pallas_call(
        kern,
        grid=(bsz // nb, seq // tb),
        in_specs=[pl.BlockSpec((nb, tb, B_COLS), lambda b, i: (b, i, P0_B // B_COLS)),
                  pl.BlockSpec((nb, 1, B_COLS), per_b3),
                  pl.BlockSpec((nb, B_HEADS, B_N, B_N), per_b4),
                  pl.BlockSpec((1, B_COLS), fix2),
                  vec, vec,
                  pl.BlockSpec((2 * B_LORA, 2 * B_W), fix2),
                  vec, vec, vec, vec, vec],
        out_specs=[pl.BlockSpec((nb, tb, B_W), blk),
                   pl.BlockSpec((nb, B_HEADS, B_N, B_N), per_b4)],
        out_shape=[jax.ShapeDtypeStruct((bsz, seq, B_W), F32),
                   jax.ShapeDtypeStruct((bsz, B_HEADS, B_N, B_N), F32)],
        scratch_shapes=[pltpu.VMEM((nb, tb + SUBLANES, B_COLS), F32),
                        pltpu.VMEM((nb, tb, 7 * B_W), F32),
                        pltpu.VMEM((nb, B_PAIRS, LANES, LANES), F32),
                        pltpu.VMEM((B_N, LANES), F32)],
        compiler_params=pltpu.CompilerParams(dimension_semantics=("parallel", "arbitrary"),
                                             vmem_limit_bytes=VMEM_LIMIT),
        name="rwkv7_mixer",
    )(proj3d, prev_row, s0, mu, w0, a0, w2a2, k_k, k_a, r_k, ln_w, ln_b)


def _mlstm_kernel(p_ref, c0_ref, n0_ref, m0_ref, bias_ref, o_ref, cout_ref, nout_ref, mout_ref,
                  gsc_s, caug_s, m_s, *, nb, tb, chunk):
    i = pl.program_id(1)
    n_chunks = tb // chunk
    lane = _iota2((1, LANES), 1)
    krow = _iota2((LANES, 1), 0)
    n_gates = 2 * C_HEADS
    is_f = jnp.bitwise_and(lane, n_gates - 1) >= C_HEADS

    @pl.when(i == 0)
    def _():
        for bi in range(nb):
            m0 = m0_ref[bi]
            for h in range(C_HEADS):
                j = h % 2
                caug_s[bi, h] = jnp.zeros((LANES, 2 * C_DV), F32)
                caug_s[bi, h, j * C_DQK:(j + 1) * C_DQK, 0:C_DV] = c0_ref[bi, h]
                n_col = _row_to_col(n0_ref[bi, h // 2:h // 2 + 1, :])
                own = (krow // C_DQK) == j
                caug_s[bi, h, :, C_DV:2 * C_DV] = jnp.where(own, jnp.broadcast_to(n_col, (LANES, C_DV)), 0.0)
                m_s[bi, h:h + 1, :] = jnp.broadcast_to(m0[:, h:h + 1], (1, LANES))

    for bi in range(nb):
        g = p_ref[bi, :, P1_GATES:P1_PAD] + bias_ref[...]
        cap = C_GATE_CAP * jnp.tanh(g / C_GATE_CAP)
        gsc_s[bi] = jnp.where(is_f, -_softplus(-cap), cap)

    tri = _tri_incl(chunk)
    sel = _replicate_selector(n_gates)
    ones_blk = jnp.ones((chunk, C_DV), F32)
    off = 2 * C_QK

    def body(c):
        r0 = _row_start(c, chunk)
        rows = pl.ds(r0, chunk)
        gl = [gsc_s[bi, rows, :] for bi in range(nb)]
        bcum = [_dot_exact_lhs(tri, g) for g in gl]
        rep = [_replicate(jnp.where(is_f, b, g), sel, n_gates) for g, b in zip(gl, bcum)]

        def head_chain(bi, h):
            p, j = h // 2, h % 2
            own_lane = (lane // C_DQK) == j
            own_row = (krow // C_DQK) == j
            qp = p_ref[bi, rows, p * LANES:(p + 1) * LANES] * (C_DQK ** -0.5)
            kp = p_ref[bi, rows, C_QK + p * LANES:C_QK + (p + 1) * LANES]
            v = p_ref[bi, rows, off + h * C_DV:off + (h + 1) * C_DV]
            vaug = jnp.concatenate([v, ones_blk], axis=1)
            qk = _dot_nt(jnp.where(own_lane, qp, 0.0), kp)
            cm = caug_s[bi, h]
            qc = _dot(qp, cm)
            i_rep = rep[bi][:, h * LANES:(h + 1) * LANES]
            b_rep = rep[bi][:, (C_HEADS + h) * LANES:(C_HEADS + h + 1) * LANES]
            b_row = _col_to_row(b_rep[:, 0:chunk])
            i_row = _col_to_row(i_rep[:, 0:chunk])
            dmat = jnp.where(tri, b_rep[:, 0:chunk] - b_row + i_row, -jnp.inf)
            dmax = b_rep + _cummax_rows(i_rep - b_rep)
            m = m_s[bi, h:h + 1, :]
            inter = b_rep + m
            mt = jnp.maximum(inter, dmax)
            m_new = mt[chunk - 1:chunk, :]
            b_last = b_rep[chunk - 1:chunk, :]
            w_s = jnp.exp(b_last - b_rep + i_rep - m_new)
            e_c = jnp.exp(b_last + m - m_new)
            e_inter = jnp.exp(inter - mt)
            upd = _dot_tn(w_s * kp, vaug)
            yield
            pm = jnp.exp(dmat - mt[:, 0:chunk]) * qk
            pv = _dot(pm, vaug)
            caug_s[bi, h] = jnp.concatenate([e_c, e_c], axis=1) * cm + jnp.where(own_row, upd, 0.0)
            m_s[bi, h:h + 1, :] = m_new
            yield
            num = e_inter * qc[:, 0:C_DV] + pv[:, 0:C_DV]
            den = e_inter * qc[:, C_DV:2 * C_DV] + pv[:, C_DV:2 * C_DV]
            hh = num / jnp.maximum(jnp.abs(den), jnp.exp(-mt))
            og = p_ref[bi, rows, off + C_V + h * C_DV:off + C_V + (h + 1) * C_DV]
            z = p_ref[bi, rows, off + 2 * C_V + h * C_DV:off + 2 * C_V + (h + 1) * C_DV]
            o_ref[bi, rows, h * C_DV:(h + 1) * C_DV] = hh * _sigmoid(og) * _silu(z)

        _interleave(head_chain(bi, h) for bi in range(nb) for h in range(C_HEADS))

    _chunk_loop(n_chunks, body)

    @pl.when(i == pl.num_programs(1) - 1)
    def _():
        for bi in range(nb):
            for h in range(C_HEADS):
                j = h % 2
                cout_ref[bi, h] = caug_s[bi, h, j * C_DQK:(j + 1) * C_DQK, 0:C_DV]
            for p in range(C_HEADS // 2):
                n_col = caug_s[bi, 2 * p, :, C_DV:C_DV + 1] + caug_s[bi, 2 * p + 1, :, C_DV:C_DV + 1]
                nout_ref[bi, p:p + 1, :] = _col_to_row(n_col)
        mout_ref[...] = m_s[...]


def _mlstm(proj3d, c0, n0, m0, bias_row, nb, tb, chunk):
    bsz, seq, _ = proj3d.shape
    blk = lambda b, i: (b, i, 0)
    per_b3 = lambda b, i: (b, 0, 0)
    per_b4 = lambda b, i: (b, 0, 0, 0)
    kern = functools.partial(_mlstm_kernel, nb=nb, tb=tb, chunk=chunk)
    pairs = C_HEADS // 2
    return pl.pallas_call(
        kern,
        grid=(bsz // nb, seq // tb),
        in_specs=[pl.BlockSpec((nb, tb, P1_PAD), blk),
                  pl.BlockSpec((nb, C_HEADS, C_DQK, C_DV), per_b4),
                  pl.BlockSpec((nb, pairs, LANES), per_b3),
                  pl.BlockSpec((nb, 1, C_HEADS), per_b3),
                  pl.BlockSpec((1, LANES), lambda b, i: (0, 0))],
        out_specs=[pl.BlockSpec((nb, tb, C_V), blk),
                   pl.BlockSpec((nb, C_HEADS, C_DQK, C_DV), per_b4),
                   pl.BlockSpec((nb, pairs, LANES), per_b3),
                   pl.BlockSpec((nb, C_HEADS, LANES), per_b3)],
        out_shape=[jax.ShapeDtypeStruct((bsz, seq, C_V), F32),
                   jax.ShapeDtypeStruct((bsz, C_HEADS, C_DQK, C_DV), F32),
                   jax.ShapeDtypeStruct((bsz, pairs, LANES), F32),
                   jax.ShapeDtypeStruct((bsz, C_HEADS, LANES), F32)],
        scratch_shapes=[pltpu.VMEM((nb, tb, LANES), F32),
                        pltpu.VMEM((nb, C_HEADS, LANES, 2 * C_DV), F32),
                        pltpu.VMEM((nb, C_HEADS, LANES), F32)],
        compiler_params=pltpu.CompilerParams(dimension_semantics=("parallel", "arbitrary"),
                                             vmem_limit_bytes=VMEM_LIMIT),
        name="mlstm_mixer",
    )(proj3d, c0, n0, m0, bias_row)


def _gate_tile(cols):
    tiled = jnp.concatenate([cols.astype(F32)] * GATE_COPIES, axis=1)
    return jnp.pad(tiled, ((0, 0), (0, LANES - tiled.shape[1])))


def _prepare_params(p):
    w_in0 = p["w_in0"]
    a_cols = A_MAIN + 2 * A_HEADS
    w0 = jnp.concatenate([w_in0[:, 0:A_MAIN], _gate_tile(w_in0[:, A_MAIN:a_cols]), w_in0[:, a_cols:]], axis=1)
    w_in1 = p["w_in1"]
    w1 = jnp.concatenate([w_in1[:, 0:P1_GATES], _gate_tile(w_in1[:, P1_GATES:])], axis=1)
    zeros_h = jnp.zeros((A_HEADS,), F32)
    alog = _gate_tile(jnp.concatenate([zeros_h, p["gdn_a_log"]]).reshape(1, -1))
    dtb = _gate_tile(jnp.concatenate([zeros_h, p["gdn_dt_bias"]]).reshape(1, -1))
    gate_bias = _gate_tile(jnp.concatenate([p["mlstm_b_i"], p["mlstm_b_f"]]).reshape(1, -1))
    w2a2 = jnp.zeros((2 * B_LORA, 2 * B_W), F32)
    w2a2 = w2a2.at[0:B_LORA, 0:B_W].set(p["rwkv_w2"]).at[B_LORA:, B_W:].set(p["rwkv_a2"])
    row = lambda v: v.reshape(1, -1).astype(F32)
    return dict(
        w0=w0.astype(BF16), w1=w1.astype(BF16), wb=w_in0[:, a_cols:].astype(BF16),
        woa=p["w_out0"][0:A_VAL].astype(BF16), wob=p["w_out0"][A_VAL:].astype(BF16),
        wo1=p["w_out1"].astype(BF16), w2a2=w2a2.astype(BF16),
        norm0=row(p["norm0_w"]), norm1=row(p["norm1_w"]), normf=row(p["norm_f_w"]),
        conv_w=p["gdn_conv_w"].astype(F32),
        alog=alog, dtb=dtb,
        gnw=row(p["gdn_norm_w"]),
        mu=row(p["rwkv_mu"]), rw0=row(p["rwkv_w0"]), ra0=row(p["rwkv_a0"]),
        k_k=row(p["rwkv_k_k"]), k_a=row(p["rwkv_k_a"]), r_k=row(p["rwkv_r_k"]),
        ln_w=row(p["rwkv_ln_w"]), ln_b=row(p["rwkv_ln_b"]),
        gate_bias=gate_bias,
    )


def _tiles(bsz, seq):
    chunk = math.gcd(seq, MAX_CHUNK)
    tb = min(seq, 256)
    nb = 2 if seq > MAX_CHUNK else 4
    nb = math.gcd(nb, bsz)
    tm = min(bsz * seq, 512)
    return chunk, tb, nb, tm


def _trunk(x, state, q):
    conv0, s_gdn, x_prev, s_rwkv, c0, n0, m0 = state
    bsz, seq, _ = x.shape
    chunk, tb, nb, tm = _tiles(bsz, seq)
    m = bsz * seq
    x2d = x.reshape(m, D_MODEL)

    proj0 = _norm_proj(x2d, q["norm0"], q["w0"], tm).reshape(bsz, seq, P0_PAD)
    xn_last, prev_row = _aux(x[:, -1, :], x_prev, q["norm0"], q["wb"])
    o_a, conv_new, s_gdn_new = _gdn(proj0, conv0, s_gdn, q["conv_w"], q["alog"], q["dtb"], q["gnw"],
                                    nb, tb, chunk)
    o_b, s_rwkv_new = _rwkv(proj0, prev_row.reshape(bsz, 1, B_COLS), s_rwkv, q["mu"], q["rw0"], q["ra0"],
                            q["w2a2"], q["k_k"], q["k_a"], q["r_k"], q["ln_w"], q["ln_b"],
                            nb, tb, chunk)
    x1, proj1 = _mid(x2d, o_a.reshape(m, A_VAL), o_b.reshape(m, B_W), q["woa"], q["wob"],
                     q["norm1"], q["w1"], tm)
    h, c_new, n_new, m_new = _mlstm(proj1.reshape(bsz, seq, P1_PAD), c0,
                                    n0.reshape(bsz, C_HEADS // 2, LANES),
                                    m0.reshape(bsz, 1, C_HEADS), q["gate_bias"], nb, tb, chunk)
    y = _final(x1, h.reshape(m, C_V), q["wo1"], q["normf"], tm)
    return y.reshape(bsz, seq, D_MODEL), (conv_new, s_gdn_new, xn_last, s_rwkv_new, c_new,
                                          n_new.reshape(bsz, C_HEADS, C_DQK), m_new[:, :, 0])


def kernel(x_prompt, x_sample, state_gdn_conv, state_gdn, state_rwkv_shift, state_rwkv,
           state_mlstm_c, state_mlstm_n, state_mlstm_m,
           norm0_w, w_in0, w_out0, gdn_conv_w, gdn_a_log, gdn_dt_bias, gdn_norm_w,
           rwkv_mu, rwkv_w0, rwkv_w2, rwkv_a0, rwkv_a2, rwkv_k_k, rwkv_k_a, rwkv_r_k,
           rwkv_ln_w, rwkv_ln_b, norm1_w, w_in1, w_out1, mlstm_b_i, mlstm_b_f, norm_f_w):
    q = _prepare_params(dict(
        norm0_w=norm0_w, w_in0=w_in0, w_out0=w_out0, gdn_conv_w=gdn_conv_w, gdn_a_log=gdn_a_log,
        gdn_dt_bias=gdn_dt_bias, gdn_norm_w=gdn_norm_w, rwkv_mu=rwkv_mu, rwkv_w0=rwkv_w0,
        rwkv_w2=rwkv_w2, rwkv_a0=rwkv_a0, rwkv_a2=rwkv_a2, rwkv_k_k=rwkv_k_k, rwkv_k_a=rwkv_k_a,
        rwkv_r_k=rwkv_r_k, rwkv_ln_w=rwkv_ln_w, rwkv_ln_b=rwkv_ln_b, norm1_w=norm1_w, w_in1=w_in1,
        w_out1=w_out1, mlstm_b_i=mlstm_b_i, mlstm_b_f=mlstm_b_f, norm_f_w=norm_f_w))
    bsz = x_prompt.shape[0]
    prompt_state = (
        jnp.zeros((bsz, A_CONV - 1, A_QKV), F32),
        jnp.zeros((bsz, A_HEADS, A_DK, A_DV), F32),
        jnp.zeros((bsz, D_MODEL), F32),
        jnp.zeros((bsz, B_HEADS, B_N, B_N), F32),
        jnp.zeros((bsz, C_HEADS, C_DQK, C_DV), F32),
        jnp.zeros((bsz, C_HEADS, C_DQK), F32),
        jnp.zeros((bsz, C_HEADS), F32),
    )
    y_p, sp = _trunk(x_prompt, prompt_state, q)
    sample_state = (state_gdn_conv, state_gdn, state_rwkv_shift, state_rwkv,
                    state_mlstm_c, state_mlstm_n, state_mlstm_m)
    y_s, ss = _trunk(x_sample, sample_state, q)
    return (y_p, y_s) + tuple(sp) + tuple(ss)
```

```python
import functools
import math

import jax
import jax.numpy as jnp
from jax import lax
from jax.experimental import pallas as pl
from jax.experimental.pallas import tpu as pltpu

F32 = jnp.float32
BF16 = jnp.bfloat16

D_MODEL = 1024
EPS = 1e-6
LANES = 128
SUBLANES = 8
A_HEADS, A_DK, A_DV, A_CONV = 4, 128, 128, 4
A_KEY = A_HEADS * A_DK
A_VAL = A_HEADS * A_DV
A_QKV = 2 * A_KEY + A_VAL
A_MAIN = A_QKV + A_VAL
B_HEADS, B_N, B_LORA = 8, 64, 64
B_W = B_HEADS * B_N
B_COLS = 4 * B_W + 2 * B_LORA
B_PAIRS = B_HEADS // 2
B_GN_EPS = 64e-5
B_DECAY_SCALE = math.exp(-0.5)
C_HEADS, C_DQK, C_DV = 8, 64, 128
C_QK = C_HEADS * C_DQK
C_V = C_HEADS * C_DV
C_GATE_CAP = 15.0
P0_GATES = A_MAIN
P0_B = A_MAIN + LANES
P0_PAD = P0_B + B_COLS
P1_GATES = 2 * C_QK + 3 * C_V
P1_PAD = P1_GATES + LANES
MAX_CHUNK = 64
VMEM_LIMIT = 56 * 1024 * 1024


def _dot(a, b):
    return jnp.dot(a.astype(BF16), b.astype(BF16), preferred_element_type=F32)


def _dot_nt(a, b):
    return lax.dot_general(a.astype(BF16), b.astype(BF16), (((1,), (1,)), ((), ())),
                           preferred_element_type=F32)


def _dot_tn(a, b):
    return lax.dot_general(a.astype(BF16), b.astype(BF16), (((0,), (0,)), ((), ())),
                           preferred_element_type=F32)


def _split(x, terms):
    out = []
    for _ in range(terms - 1):
        hi = x.astype(BF16)
        out.append(hi)
        x = x - hi.astype(F32)
    out.append(x.astype(BF16))
    return out


def _dot_exact_lhs(sel, x, terms=3):
    s = sel.astype(BF16)
    return jnp.dot(jnp.concatenate([s] * terms, axis=1), jnp.concatenate(_split(x, terms), axis=0),
                   preferred_element_type=F32)


def _dot_exact_rhs(x, sel, terms=2):
    s = sel.astype(BF16)
    return jnp.dot(jnp.concatenate(_split(x, terms), axis=1), jnp.concatenate([s] * terms, axis=0),
                   preferred_element_type=F32)


def _iota2(shape, dim):
    return lax.broadcasted_iota(jnp.int32, shape, dim)


def _tri_incl(n):
    return (_iota2((n, n), 0) >= _iota2((n, n), 1))


def _col_to_row(col):
    n = col.shape[0]
    eye = _iota2((n, n), 0) == _iota2((n, n), 1)
    return jnp.sum(jnp.where(eye, col, 0.0), axis=0, keepdims=True)


def _row_to_col(row):
    n = row.shape[1]
    eye = _iota2((n, n), 0) == _iota2((n, n), 1)
    return jnp.sum(jnp.where(eye, row, 0.0), axis=1, keepdims=True)


GATE_COPIES = 3


def _replicate(x, sel, group):
    hi, mid, lo = _split(x, GATE_COPIES)
    lane = _iota2(x.shape, 1)
    packed = jnp.where(lane < group, hi, jnp.where(lane < 2 * group, mid, lo))
    packed = jnp.where(lane < GATE_COPIES * group, packed, jnp.zeros_like(packed))
    return jnp.dot(packed, sel, preferred_element_type=F32)


def _replicate_selector(group):
    assert group & (group - 1) == 0 and LANES == 128
    k = _iota2((LANES, group * LANES), 0)
    col = _iota2((LANES, group * LANES), 1)
    hit = (k < GATE_COPIES * group) & (jnp.bitwise_and(k, group - 1) == jnp.right_shift(col, 7))
    return jnp.where(hit, 1.0, 0.0).astype(BF16)


def _cummax_rows(x):
    n = x.shape[0]
    row = _iota2(x.shape, 0)
    sh = 1
    while sh < n:
        x = jnp.maximum(x, jnp.where(row >= sh, pltpu.roll(x, sh, 0), -jnp.inf))
        sh *= 2
    return x


def _inv_unit_lower(p, levels):
    n = p.shape[0]
    eye = (_iota2((n, n), 0) == _iota2((n, n), 1)).astype(F32)
    t = eye + p
    m = 2
    if m < levels:
        p = _dot(p, p)
        yield
    while m < levels:
        m *= 2
        if m >= levels:
            t = t + _dot(p, t)
        elif n % LANES == 0:
            both = _dot(p, jnp.concatenate([p, t], axis=1))
            p, t = both[:, 0:n], t + both[:, n:2 * n]
            yield
        else:
            both = _dot(jnp.concatenate([p, t], axis=0), p)
            p, t = both[0:n, :], t + both[n:2 * n, :]
            yield
    return t


def _interleave(chains):
    live = list(chains)
    while live:
        nxt = []
        for g in live:
            try:
                next(g)
                nxt.append(g)
            except StopIteration:
                pass
        live = nxt


def _sigmoid(x):
    return 1.0 / (1.0 + jnp.exp(-x))


def _silu(x):
    return x * _sigmoid(x)


def _softplus(x):
    return jnp.maximum(x, 0.0) + jnp.log1p(jnp.exp(-jnp.abs(x)))


def _rms(x, w):
    return x * lax.rsqrt(jnp.mean(x * x, axis=-1, keepdims=True) + EPS) * w


def _chunk_loop(n_chunks, body):
    if n_chunks == 1:
        body(0)
    else:
        def step(c, carry):
            body(c)
            return carry
        lax.fori_loop(0, n_chunks, step, 0)


def _row_start(c, chunk):
    if isinstance(c, int):
        return c * chunk
    return pl.multiple_of(c * chunk, chunk)


def _resident(shape):
    zeros = (0,) * len(shape)
    return pl.BlockSpec(shape, lambda *_: zeros, pipeline_mode=pl.Buffered(1))


def _norm_proj_kernel(x_ref, nw_ref, w_ref, o_ref):
    xn = _rms(x_ref[...], nw_ref[...])
    o_ref[...] = jnp.dot(xn.astype(BF16), w_ref[...], preferred_element_type=F32)


def _norm_proj(x2d, norm_w, w_bf16, tm):
    m = x2d.shape[0]
    n = w_bf16.shape[1]
    return pl.pallas_call(
        _norm_proj_kernel,
        grid=(m // tm,),
        in_specs=[pl.BlockSpec((tm, D_MODEL), lambda i: (i, 0)),
                  _resident((1, D_MODEL)),
                  _resident((D_MODEL, n))],
        out_specs=pl.BlockSpec((tm, n), lambda i: (i, 0)),
        out_shape=jax.ShapeDtypeStruct((m, n), F32),
        compiler_params=pltpu.CompilerParams(dimension_semantics=("parallel",),
                                             vmem_limit_bytes=VMEM_LIMIT),
        name="norm_proj0",
    )(x2d, norm_w, w_bf16)


def _mid_kernel(x_ref, oa_ref, ob_ref, woa_ref, wob_ref, nw_ref, wi_ref, x1_ref, p_ref):
    y = jnp.dot(oa_ref[...].astype(BF16), woa_ref[...], preferred_element_type=F32)
    y = y + jnp.dot(ob_ref[...].astype(BF16), wob_ref[...], preferred_element_type=F32)
    x1 = x_ref[...] + y
    x1_ref[...] = x1
    xn = _rms(x1, nw_ref[...])
    p_ref[...] = jnp.dot(xn.astype(BF16), wi_ref[...], preferred_element_type=F32)


def _mid(x2d, oa, ob, woa, wob, norm_w, wi, tm):
    m = x2d.shape[0]
    n = wi.shape[1]
    row = lambda i: (i, 0)
    return pl.pallas_call(
        _mid_kernel,
        grid=(m // tm,),
        in_specs=[pl.BlockSpec((tm, D_MODEL), row),
                  pl.BlockSpec((tm, A_VAL), row),
                  pl.BlockSpec((tm, B_W), row),
                  _resident((A_VAL, D_MODEL)),
                  _resident((B_W, D_MODEL)),
                  _resident((1, D_MODEL)),
                  _resident((D_MODEL, n))],
        out_specs=[pl.BlockSpec((tm, D_MODEL), row), pl.BlockSpec((tm, n), row)],
        out_shape=[jax.ShapeDtypeStruct((m, D_MODEL), F32), jax.ShapeDtypeStruct((m, n), F32)],
        compiler_params=pltpu.CompilerParams(dimension_semantics=("parallel",),
                                             vmem_limit_bytes=VMEM_LIMIT),
        name="out0_norm_proj1",
    )(x2d, oa, ob, woa, wob, norm_w, wi)


def _final_kernel(x_ref, h_ref, wo_ref, nw_ref, y_ref):
    x2 = x_ref[...] + jnp.dot(h_ref[...].astype(BF16), wo_ref[...], preferred_element_type=F32)
    y_ref[...] = _rms(x2, nw_ref[...])


def _final(x2d, h, wo, norm_w, tm):
    m = x2d.shape[0]
    row = lambda i: (i, 0)
    return pl.pallas_call(
        _final_kernel,
        grid=(m // tm,),
        in_specs=[pl.BlockSpec((tm, D_MODEL), row),
                  pl.BlockSpec((tm, C_V), row),
                  _resident((C_V, D_MODEL)),
                  _resident((1, D_MODEL))],
        out_specs=pl.BlockSpec((tm, D_MODEL), row),
        out_shape=jax.ShapeDtypeStruct((m, D_MODEL), F32),
        compiler_params=pltpu.CompilerParams(dimension_semantics=("parallel",),
                                             vmem_limit_bytes=VMEM_LIMIT),
        name="out1_norm_f",
    )(x2d, h, wo, norm_w)


def _aux_kernel(xl_ref, xp_ref, nw_ref, wb_ref, xn_ref, prev_ref):
    xn_ref[...] = _rms(xl_ref[...], nw_ref[...])
    prev_ref[...] = jnp.dot(xp_ref[...].astype(BF16), wb_ref[...], preferred_element_type=F32)


def _aux(x_last, x_prev, norm_w, wb):
    b = x_last.shape[0]
    return pl.pallas_call(
        _aux_kernel,
        out_shape=[jax.ShapeDtypeStruct((b, D_MODEL), F32), jax.ShapeDtypeStruct((b, B_COLS), F32)],
        compiler_params=pltpu.CompilerParams(vmem_limit_bytes=VMEM_LIMIT),
        name="shift_rows",
    )(x_last, x_prev, norm_w, wb)


def _gdn_kernel(main_ref, gates_ref, conv0_ref, s0_ref, convw_ref, alog_ref, dtb_ref, gnw_ref,
                o_ref, convn_ref, sout_ref, ext_s, qkv_s, gsc_s, st_s, uw_s, attn_s, qg_s, kd_s, ge_s,
                *, nb, tb, chunk):
    i = pl.program_id(1)
    n_chunks = tb // chunk
    tail = SUBLANES

    @pl.when(i == 0)
    def _():
        for bi in range(nb):
            ext_s[bi, 0:tail, :] = jnp.zeros((tail, A_QKV), F32)
            ext_s[bi, tail - (A_CONV - 1):tail, :] = conv0_ref[bi]
        st_s[...] = s0_ref[...]

    @pl.when(i > 0)
    def _():
        for bi in range(nb):
            ext_s[bi, 0:tail, :] = ext_s[bi, tb:tb + tail, :]

    cw = convw_ref[...]
    for bi in range(nb):
        ext_s[bi, tail:tail + tb, :] = main_ref[bi, :, 0:A_QKV]
        for c in range(n_chunks):
            r0 = c * chunk
            y = ext_s[bi, tail + r0:tail + r0 + chunk, :] * cw[A_CONV - 1:A_CONV, :]
            for j in range(A_CONV - 1):
                sh = A_CONV - 1 - j
                y = y + ext_s[bi, tail + r0 - sh:tail + r0 - sh + chunk, :] * cw[j:j + 1, :]
            y = _silu(y)
            for h in range(A_HEADS):
                q = y[:, h * A_DK:(h + 1) * A_DK]
                qkv_s[bi, r0:r0 + chunk, h * A_DK:(h + 1) * A_DK] = (
                    q * lax.rsqrt(jnp.sum(q * q, axis=-1, keepdims=True) + EPS) * (A_DK ** -0.5))
                k = y[:, A_KEY + h * A_DK:A_KEY + (h + 1) * A_DK]
                qkv_s[bi, r0:r0 + chunk, A_KEY + h * A_DK:A_KEY + (h + 1) * A_DK] = (
                    k * lax.rsqrt(jnp.sum(k * k, axis=-1, keepdims=True) + EPS))
            qkv_s[bi, r0:r0 + chunk, 2 * A_KEY:A_QKV] = y[:, 2 * A_KEY:A_QKV]
        g = gates_ref[bi]
        gsc_s[bi, :, 0:LANES] = _sigmoid(g)
        gsc_s[bi, :, LANES:2 * LANES] = -jnp.exp(alog_ref[...]) * _softplus(g + dtb_ref[...])

    n_gates = 2 * A_HEADS
    is_decay = jnp.bitwise_and(_iota2((1, LANES), 1), n_gates - 1) >= A_HEADS
    sel = _replicate_selector(n_gates)
    tri = _tri_incl(chunk)
    strict = _iota2((chunk, chunk), 0) > _iota2((chunk, chunk), 1)
    gnw = gnw_ref[...]

    blocks = [(bi, c) for c in range(n_chunks) for bi in range(nb)]
    gcum = {(bi, c): _dot_exact_lhs(tri, gsc_s[bi, c * chunk:(c + 1) * chunk, LANES:2 * LANES])
            for bi, c in blocks}
    rep = {(bi, c): _replicate(jnp.where(is_decay, gcum[bi, c], gsc_s[bi, c * chunk:(c + 1) * chunk, 0:LANES]),
                               sel, n_gates) for bi, c in blocks}

    def local_chain(bi, c, h):
        rows = slice(c * chunk, (c + 1) * chunk)
        q = qkv_s[bi, rows, h * A_DK:(h + 1) * A_DK]
        k = qkv_s[bi, rows, A_KEY + h * A_DK:A_KEY + (h + 1) * A_DK]
        v = qkv_s[bi, rows, 2 * A_KEY + h * A_DV:2 * A_KEY + (h + 1) * A_DV]
        qk_k = _dot_nt(jnp.concatenate([q, k], axis=0), k)
        yield
        b_rep = rep[bi, c][:, h * LANES:(h + 1) * LANES]
        g_rep = rep[bi, c][:, (A_HEADS + h) * LANES:(A_HEADS + h + 1) * LANES]
        g_row = _col_to_row(g_rep[:, 0:chunk])
        g_last = g_rep[chunk - 1:chunk, :]
        decay = jnp.where(tri, jnp.exp(jnp.where(tri, g_rep[:, 0:chunk] - g_row, 0.0)), 0.0)
        eg = jnp.exp(g_rep)
        qg_s[bi, rows, h * A_DK:(h + 1) * A_DK] = q * eg
        kd_s[bi, rows, h * A_DK:(h + 1) * A_DK] = k * jnp.exp(g_last - g_rep)
        ge_s[bi, c, h:h + 1, :] = jnp.exp(g_last)
        attn_s[bi, rows, h * LANES:h * LANES + chunk] = decay * qk_k[0:chunk, :]
        amat = jnp.where(strict, b_rep[:, 0:chunk] * decay * qk_k[chunk:2 * chunk, :], 0.0)
        t_inv = yield from _inv_unit_lower(-amat, chunk)
        uw_s[bi, rows, 2 * h * LANES:2 * (h + 1) * LANES] = _dot(
            t_inv, jnp.concatenate([b_rep * v, (b_rep * eg) * k], axis=1))

    _interleave(local_chain(bi, c, h) for c in range(n_chunks) for bi in range(nb) for h in range(A_HEADS))

    def body(c):
        r0 = _row_start(c, chunk)
        rows = pl.ds(r0, chunk)

        def state_chain(bi, h):
            uw = uw_s[bi, rows, 2 * h * LANES:2 * (h + 1) * LANES]
            s = st_s[bi, h]
            wq_s = _dot(jnp.concatenate([uw[:, A_DV:A_DV + A_DK], qg_s[bi, rows, h * A_DK:(h + 1) * A_DK]], axis=0), s)
            yield
            nv = uw[:, 0:A_DV] - wq_s[0:chunk, :]
            o = wq_s[chunk:2 * chunk, :] + _dot(attn_s[bi, rows, h * LANES:h * LANES + chunk], nv)
            st_s[bi, h] = ge_s[bi, c, h:h + 1, :] * s + _dot_tn(kd_s[bi, rows, h * A_DK:(h + 1) * A_DK], nv)
            yield
            z = main_ref[bi, rows, A_QKV + h * A_DV:A_QKV + (h + 1) * A_DV]
            o_ref[bi, rows, h * A_DV:(h + 1) * A_DV] = _rms(o, gnw) * _silu(z)

        _interleave(state_chain(bi, h) for bi in range(nb) for h in range(A_HEADS))

    _chunk_loop(n_chunks, body)

    @pl.when(i == pl.num_programs(1) - 1)
    def _():
        for bi in range(nb):
            convn_ref[bi] = ext_s[bi, tb + tail - (A_CONV - 1):tb + tail, :]
        sout_ref[...] = st_s[...]


def _gdn(proj3d, conv0, s0, conv_w, alog_row, dtb_row, gn_w, nb, tb, chunk):
    bsz, seq, _ = proj3d.shape
    blk = lambda b, i: (b, i, 0)
    per_b3 = lambda b, i: (b, 0, 0)
    per_b4 = lambda b, i: (b, 0, 0, 0)
    fix2 = lambda b, i: (0, 0)
    kern = functools.partial(_gdn_kernel, nb=nb, tb=tb, chunk=chunk)
    return pl.pallas_call(
        kern,
        grid=(bsz // nb, seq // tb),
        in_specs=[pl.BlockSpec((nb, tb, A_MAIN), blk),
                  pl.BlockSpec((nb, tb, LANES), lambda b, i: (b, i, P0_GATES // LANES)),
                  pl.BlockSpec((nb, A_CONV - 1, A_QKV), per_b3),
                  pl.BlockSpec((nb, A_HEADS, A_DK, A_DV), per_b4),
                  pl.BlockSpec((A_CONV, A_QKV), fix2),
                  pl.BlockSpec((1, LANES), fix2),
                  pl.BlockSpec((1, LANES), fix2),
                  pl.BlockSpec((1, A_DV), fix2)],
        out_specs=[pl.BlockSpec((nb, tb, A_VAL), blk),
                   pl.BlockSpec((nb, A_CONV - 1, A_QKV), per_b3),
                   pl.BlockSpec((nb, A_HEADS, A_DK, A_DV), per_b4)],
        out_shape=[jax.ShapeDtypeStruct((bsz, seq, A_VAL), F32),
                   jax.ShapeDtypeStruct((bsz, A_CONV - 1, A_QKV), F32),
                   jax.ShapeDtypeStruct((bsz, A_HEADS, A_DK, A_DV), F32)],
        scratch_shapes=[pltpu.VMEM((nb, tb + SUBLANES, A_QKV), F32),
                        pltpu.VMEM((nb, tb, A_QKV), F32),
                        pltpu.VMEM((nb, tb, 2 * LANES), F32),
                        pltpu.VMEM((nb, A_HEADS, A_DK, A_DV), F32),
                        pltpu.VMEM((nb, tb, A_HEADS * (A_DV + A_DK)), F32),
                        pltpu.VMEM((nb, tb, A_HEADS * LANES), F32),
                        pltpu.VMEM((nb, tb, A_KEY), F32),
                        pltpu.VMEM((nb, tb, A_KEY), F32),
                        pltpu.VMEM((nb, tb // chunk, SUBLANES, LANES), F32)],
        compiler_params=pltpu.CompilerParams(dimension_semantics=("parallel", "arbitrary"),
                                             vmem_limit_bytes=VMEM_LIMIT),
        name="gdn_mixer",
    )(proj3d, proj3d, conv0, s0, conv_w, alog_row, dtb_row, gn_w)


_FR, _FLW, _FK, _FV, _FA, _FB, _FG = range(7)


def _rwkv_kernel(pb_ref, prev_ref, s0_ref, mu_ref, w0_ref, a0_ref, w2a2_ref, kk_ref, ka_ref,
                 rk_ref, lnw_ref, lnb_ref, o_ref, sout_ref, ext_s, feat_s, st_s, tmp_s,
                 *, nb, tb, chunk):
    i = pl.program_id(1)
    n_chunks = tb // chunk
    tail = SUBLANES
    lane = _iota2((1, LANES), 1)
    head0 = lane < B_N
    ones_blk = (_iota2((LANES, LANES), 0) // B_N == _iota2((LANES, LANES), 1) // B_N)

    @pl.when(i == 0)
    def _():
        for bi in range(nb):
            ext_s[bi, 0:tail, :] = jnp.zeros((tail, B_COLS), F32)
            ext_s[bi, tail - 1:tail, :] = prev_ref[bi]
            for p in range(B_PAIRS):
                tmp_s[...] = jnp.zeros((B_N, LANES), F32)
                tmp_s[:, 0:B_N] = s0_ref[bi, 2 * p + 1]
                shifted = pltpu.roll(tmp_s[...], B_N, 1)
                st_s[bi, p, B_N:LANES, :] = jnp.where(head0, 0.0, shifted)
                tmp_s[:, 0:B_N] = s0_ref[bi, 2 * p]
                st_s[bi, p, 0:B_N, :] = tmp_s[...]

    @pl.when(i > 0)
    def _():
        for bi in range(nb):
            ext_s[bi, 0:tail, :] = ext_s[bi, tb:tb + tail, :]

    mu = mu_ref[...]
    for bi in range(nb):
        ext_s[bi, tail:tail + tb, :] = pb_ref[bi]
        for c in range(n_chunks):
            r0 = c * chunk
            pb = ext_s[bi, tail + r0:tail + r0 + chunk, :]
            pb_prev = ext_s[bi, tail + r0 - 1:tail + r0 - 1 + chunk, :]
            pbs = pb + (pb_prev - pb) * mu
            r = pbs[:, 0:B_W]
            kb = pbs[:, B_W:2 * B_W]
            vb = pbs[:, 2 * B_W:3 * B_W]
            gb = pbs[:, 3 * B_W:4 * B_W]
            lo = pbs[:, 4 * B_W:B_COLS]
            lo = jnp.where(head0, jnp.tanh(lo), lo)
            wa = _dot(lo, w2a2_ref[...])
            logw = -B_DECAY_SCALE * _sigmoid(w0_ref[...] + wa[:, 0:B_W])
            a = _sigmoid(a0_ref[...] + wa[:, B_W:2 * B_W])
            kkx = kb * kk_ref[...]
            rws = slice(r0, r0 + chunk)
            for p in range(B_PAIRS):
                sl = slice(p * LANES, (p + 1) * LANES)
                kp = kkx[:, sl]
                ss = _dot_exact_rhs(kp * kp, ones_blk)
                kkn = kp * lax.rsqrt(ss + EPS)
                feat_s[bi, rws, _FA * B_W + p * LANES:_FA * B_W + (p + 1) * LANES] = -kkn
                feat_s[bi, rws, _FB * B_W + p * LANES:_FB * B_W + (p + 1) * LANES] = kkn * a[:, sl]
            feat_s[bi, rws, _FR * B_W:(_FR + 1) * B_W] = r
            feat_s[bi, rws, _FLW * B_W:(_FLW + 1) * B_W] = logw
            feat_s[bi, rws, _FK * B_W:(_FK + 1) * B_W] = kb * (1.0 + (a - 1.0) * ka_ref[...])
            feat_s[bi, rws, _FV * B_W:(_FV + 1) * B_W] = vb
            feat_s[bi, rws, _FG * B_W:(_FG + 1) * B_W] = gb

    tri = _tri_incl(chunk)
    c2 = 2 * chunk
    ri = _iota2((c2, c2), 0)
    ci = _iota2((c2, c2), 1)
    same_head = (ri // chunk) == (ci // chunk)
    incl_bd = same_head & (ri >= ci)
    strict_bd = same_head & (ri > ci)
    ri4 = _iota2((2 * c2, c2), 0)
    ci4 = _iota2((2 * c2, c2), 1)
    rr4 = ri4 % c2
    lm_mask = ((rr4 // chunk) == (ci4 // chunk)) & ((rr4 > ci4) | ((ri4 >= c2) & (rr4 == ci4)))
    lane_st = (_iota2((c2, LANES), 0) // chunk) == (_iota2((c2, LANES), 1) // B_N)

    def body(c):
        r0 = _row_start(c, chunk)
        rows = pl.ds(r0, chunk)

        def pair_chain(bi, p):
            def feat(slot):
                return feat_s[bi, rows, slot * B_W + p * LANES:slot * B_W + (p + 1) * LANES]
            r, lw, k, v, a_, b_, gate = (feat(s) for s in (_FR, _FLW, _FK, _FV, _FA, _FB, _FG))
            cs = _dot_exact_lhs(tri, lw)
            yield
            cs_last = cs[chunk - 1:chunk, :]
            e_inv = jnp.exp(-cs)
            e_rem = jnp.exp(cs_last - cs)
            at = a_ * jnp.exp(cs - lw)
            rt = r * jnp.exp(cs)
            bt = b_ * e_inv
            kt = k * e_inv
            a_st = jnp.concatenate([jnp.where(head0, at, 0.0), jnp.where(head0, 0.0, at)], axis=0)
            r_st = jnp.concatenate([jnp.where(head0, rt, 0.0), jnp.where(head0, 0.0, rt)], axis=0)
            b2 = jnp.concatenate([bt, bt], axis=0)
            k2 = jnp.concatenate([kt, kt], axis=0)
            v2 = jnp.concatenate([v, v], axis=0)
            ar_st = jnp.concatenate([a_st, r_st], axis=0)
            if c2 % LANES == 0:
                ar_bk = _dot_nt(ar_st, jnp.concatenate([b2, k2], axis=0))
                ar_b, ar_k = ar_bk[:, 0:c2], ar_bk[:, c2:2 * c2]
            else:
                ar_b = _dot_nt(ar_st, b2)
                ar_k = _dot_nt(ar_st, k2)
            s = st_s[bi, p]
            ar = _dot_nt(jnp.concatenate([at, rt], axis=0), s)
            yield
            l_ab = jnp.where(strict_bd, ar_b[0:c2, :], 0.0)
            m_rb = jnp.where(incl_bd, ar_b[c2:2 * c2, :], 0.0)
            lm_k = jnp.where(lm_mask, ar_k, 0.0)
            lm_v = _dot(lm_k, v2)
            t_inv = yield from _inv_unit_lower(l_ab, chunk)
            ah = ar[0:chunk, :]
            rh = ar[chunk:c2, :]
            x_st = jnp.where(lane_st, jnp.concatenate([ah, ah], axis=0) + lm_v[0:c2, :], 0.0)
            u_st = _dot(t_inv, x_st)
            yield
            o_st = jnp.where(lane_st, _dot(m_rb, u_st) + lm_v[c2:2 * c2, :], 0.0)
            u = u_st[0:chunk, :] + u_st[chunk:c2, :]
            upd = _dot_tn(jnp.concatenate([u, v], axis=0),
                          jnp.concatenate([b_ * e_rem, k * e_rem], axis=0))
            yield
            o = rh + o_st[0:chunk, :] + o_st[chunk:c2, :]
            st_s[bi, p] = s * jnp.exp(cs_last) + jnp.where(ones_blk, upd, 0.0)
            sl = slice(p * LANES, (p + 1) * LANES)
            mean = _dot_exact_rhs(o, ones_blk) * (1.0 / B_N)
            bonus = _dot_exact_rhs(r * k * rk_ref[:, sl], ones_blk) * v
            yield
            d = o - mean
            var = _dot_exact_rhs(d * d, ones_blk) * (1.0 / B_N)
            yield
            ob = d * lax.rsqrt(var + B_GN_EPS) * lnw_ref[:, sl] + lnb_ref[:, sl] + bonus
            o_ref[bi, rows, sl] = ob * _silu(gate)

        _interleave(pair_chain(bi, p) for bi in range(nb) for p in range(B_PAIRS))

    _chunk_loop(n_chunks, body)

    @pl.when(i == pl.num_programs(1) - 1)
    def _():
        for bi in range(nb):
            for p in range(B_PAIRS):
                sout_ref[bi, 2 * p] = st_s[bi, p, 0:B_N, 0:B_N]
                shifted = pltpu.roll(st_s[bi, p, B_N:LANES, :], B_N, 1)
                sout_ref[bi, 2 * p + 1] = shifted[:, 0:B_N]


def _rwkv(proj3d, prev_row, s0, mu, w0, a0, w2a2, k_k, k_a, r_k, ln_w, ln_b, nb, tb, chunk):
    bsz, seq, _ = proj3d.shape
    blk = lambda b, i: (b, i, 0)
    per_b3 = lambda b, i: (b, 0, 0)
    per_b4 = lambda b, i: (b, 0, 0, 0)
    fix2 = lambda b, i: (0, 0)
    kern = functools.partial(_rwkv_kernel, nb=nb, tb=tb, chunk=chunk)
    vec = pl.BlockSpec((1, B_W), fix2)
    return pl.pallas_call(
        kern,
        grid=(bsz // nb, seq // tb),
        in_specs=[pl.BlockSpec((nb, tb, B_COLS), lambda b, i: (b, i, P0_B // B_COLS)),
                  pl.BlockSpec((nb, 1, B_COLS), per_b3),
                  pl.BlockSpec((nb, B_HEADS, B_N, B_N), per_b4),
                  pl.BlockSpec((1, B_COLS), fix2),
                  vec, vec,
                  pl.BlockSpec((2 * B_LORA, 2 * B_W), fix2),
                  vec, vec, vec, vec, vec],
        out_specs=[pl.BlockSpec((nb, tb, B_W), blk),
                   pl.BlockSpec((nb, B_HEADS, B_N, B_N), per_b4)],
        out_shape=[jax.ShapeDtypeStruct((bsz, seq, B_W), F32),
                   jax.ShapeDtypeStruct((bsz, B_HEADS, B_N, B_N), F32)],
        scratch_shapes=[pltpu.VMEM((nb, tb + SUBLANES, B_COLS), F32),
                        pltpu.VMEM((nb, tb, 7 * B_W), F32),
                        pltpu.VMEM((nb, B_PAIRS, LANES, LANES), F32),
                        pltpu.VMEM((B_N, LANES), F32)],
        compiler_params=pltpu.CompilerParams(dimension_semantics=("parallel", "arbitrary"),
                                             vmem_limit_bytes=VMEM_LIMIT),
        name="rwkv7_mixer",
    )(proj3d, prev_row, s0, mu, w0, a0, w2a2, k_k, k_a, r_k, ln_w, ln_b)


def _mlstm_kernel(p_ref, c0_ref, n0_ref, m0_ref, bias_ref, o_ref, cout_ref, nout_ref, mout_ref,
                  gsc_s, caug_s, m_s, *, nb, tb, chunk):
    i = pl.program_id(1)
    n_chunks = tb // chunk
    lane = _iota2((1, LANES), 1)
    krow = _iota2((LANES, 1), 0)
    n_gates = 2 * C_HEADS
    is_f = jnp.bitwise_and(lane, n_gates - 1) >= C_HEADS

    @pl.when(i == 0)
    def _():
        for bi in range(nb):
            m0 = m0_ref[bi]
            for h in range(C_HEADS):
                j = h % 2
                caug_s[bi, h] = jnp.zeros((LANES, 2 * C_DV), F32)
                caug_s[bi, h, j * C_DQK:(j + 1) * C_DQK, 0:C_DV] = c0_ref[bi, h]
                n_col = _row_to_col(n0_ref[bi, h // 2:h // 2 + 1, :])
                own = (krow // C_DQK) == j
                caug_s[bi, h, :, C_DV:2 * C_DV] = jnp.where(own, jnp.broadcast_to(n_col, (LANES, C_DV)), 0.0)
                m_s[bi, h:h + 1, :] = jnp.broadcast_to(m0[:, h:h + 1], (1, LANES))

    for bi in range(nb):
        g = p_ref[bi, :, P1_GATES:P1_PAD] + bias_ref[...]
        cap = C_GATE_CAP * jnp.tanh(g / C_GATE_CAP)
        gsc_s[bi] = jnp.where(is_f, -_softplus(-cap), cap)

    tri = _tri_incl(chunk)
    sel = _replicate_selector(n_gates)
    ones_blk = jnp.ones((chunk, C_DV), F32)
    off = 2 * C_QK

    def body(c):
        r0 = _row_start(c, chunk)
        rows = pl.ds(r0, chunk)
        gl = [gsc_s[bi, rows, :] for bi in range(nb)]
        bcum = [_dot_exact_lhs(tri, g) for g in gl]
        rep = [_replicate(jnp.where(is_f, b, g), sel, n_gates) for g, b in zip(gl, bcum)]

        def head_chain(bi, h):
            p, j = h // 2, h % 2
            own_lane = (lane // C_DQK) == j
            own_row = (krow // C_DQK) == j
            qp = p_ref[bi, rows, p * LANES:(p + 1) * LANES] * (C_DQK ** -0.5)
            kp = p_ref[bi, rows, C_QK + p * LANES:C_QK + (p + 1) * LANES]
            v = p_ref[bi, rows, off + h * C_DV:off + (h + 1) * C_DV]
            vaug = jnp.concatenate([v, ones_blk], axis=1)
            qk = _dot_nt(jnp.where(own_lane, qp, 0.0), kp)
            cm = caug_s[bi, h]
            qc = _dot(qp, cm)
            i_rep = rep[bi][:, h * LANES:(h + 1) * LANES]
            b_rep = rep[bi][:, (C_HEADS + h) * LANES:(C_HEADS + h + 1) * LANES]
            b_row = _col_to_row(b_rep[:, 0:chunk])
            i_row = _col_to_row(i_rep[:, 0:chunk])
            dmat = jnp.where(tri, b_rep[:, 0:chunk] - b_row + i_row, -jnp.inf)
            dmax = b_rep + _cummax_rows(i_rep - b_rep)
            m = m_s[bi, h:h + 1, :]
            inter = b_rep + m
            mt = jnp.maximum(inter, dmax)
            m_new = mt[chunk - 1:chunk, :]
            b_last = b_rep[chunk - 1:chunk, :]
            w_s = jnp.exp(b_last - b_rep + i_rep - m_new)
            e_c = jnp.exp(b_last + m - m_new)
            e_inter = jnp.exp(inter - mt)
            upd = _dot_tn(w_s * kp, vaug)
            yield
            pm = jnp.exp(dmat - mt[:, 0:chunk]) * qk
            pv = _dot(pm, vaug)
            caug_s[bi, h] = jnp.concatenate([e_c, e_c], axis=1) * cm + jnp.where(own_row, upd, 0.0)
            m_s[bi, h:h + 1, :] = m_new
            yield
            num = e_inter * qc[:, 0:C_DV] + pv[:, 0:C_DV]
            den = e_inter * qc[:, C_DV:2 * C_DV] + pv[:, C_DV:2 * C_DV]
            hh = num / jnp.maximum(jnp.abs(den), jnp.exp(-mt))
            og = p_ref[bi, rows, off + C_V + h * C_DV:off + C_V + (h + 1) * C_DV]
            z = p_ref[bi, rows, off + 2 * C_V + h * C_DV:off + 2 * C_V + (h + 1) * C_DV]
            o_ref[bi, rows, h * C_DV:(h + 1) * C_DV] = hh * _sigmoid(og) * _silu(z)

        _interleave(head_chain(bi, h) for bi in range(nb) for h in range(C_HEADS))

    _chunk_loop(n_chunks, body)

    @pl.when(i == pl.num_programs(1) - 1)
    def _():
        for bi in range(nb):
            for h in range(C_HEADS):
                j = h % 2
                cout_ref[bi, h] = caug_s[bi, h, j * C_DQK:(j + 1) * C_DQK, 0:C_DV]
            for p in range(C_HEADS // 2):
                n_col = caug_s[bi, 2 * p, :, C_DV:C_DV + 1] + caug_s[bi, 2 * p + 1, :, C_DV:C_DV + 1]
                nout_ref[bi, p:p + 1, :] = _col_to_row(n_col)
        mout_ref[...] = m_s[...]


def _mlstm(proj3d, c0, n0, m0, bias_row, nb, tb, chunk):
    bsz, seq, _ = proj3d.shape
    blk = lambda b, i: (b, i, 0)
    per_b3 = lambda b, i: (b, 0, 0)
    per_b4 = lambda b, i: (b, 0, 0, 0)
    kern = functools.partial(_mlstm_kernel, nb=nb, tb=tb, chunk=chunk)
    pairs = C_HEADS // 2
    return pl.pallas_call(
        kern,
        grid=(bsz // nb, seq // tb),
        in_specs=[pl.BlockSpec((nb, tb, P1_PAD), blk),
                  pl.BlockSpec((nb, C_HEADS, C_DQK, C_DV), per_b4),
                  pl.BlockSpec((nb, pairs, LANES), per_b3),
                  pl.BlockSpec((nb, 1, C_HEADS), per_b3),
                  pl.BlockSpec((1, LANES), lambda b, i: (0, 0))],
        out_specs=[pl.BlockSpec((nb, tb, C_V), blk),
                   pl.BlockSpec((nb, C_HEADS, C_DQK, C_DV), per_b4),
                   pl.BlockSpec((nb, pairs, LANES), per_b3),
                   pl.BlockSpec((nb, C_HEADS, LANES), per_b3)],
        out_shape=[jax.ShapeDtypeStruct((bsz, seq, C_V), F32),
                   jax.ShapeDtypeStruct((bsz, C_HEADS, C_DQK, C_DV), F32),
                   jax.ShapeDtypeStruct((bsz, pairs, LANES), F32),
                   jax.ShapeDtypeStruct((bsz, C_HEADS, LANES), F32)],
        scratch_shapes=[pltpu.VMEM((nb, tb, LANES), F32),
                        pltpu.VMEM((nb, C_HEADS, LANES, 2 * C_DV), F32),
                        pltpu.VMEM((nb, C_HEADS, LANES), F32)],
        compiler_params=pltpu.CompilerParams(dimension_semantics=("parallel", "arbitrary"),
                                             vmem_limit_bytes=VMEM_LIMIT),
        name="mlstm_mixer",
    )(proj3d, c0, n0, m0, bias_row)


def _gate_tile(cols):
    tiled = jnp.concatenate([cols.astype(F32)] * GATE_COPIES, axis=1)
    return jnp.pad(tiled, ((0, 0), (0, LANES - tiled.shape[1])))


def _prepare_params(p):
    w_in0 = p["w_in0"]
    a_cols = A_MAIN + 2 * A_HEADS
    w0 = jnp.concatenate([w_in0[:, 0:A_MAIN], _gate_tile(w_in0[:, A_MAIN:a_cols]), w_in0[:, a_cols:]], axis=1)
    w_in1 = p["w_in1"]
    w1 = jnp.concatenate([w_in1[:, 0:P1_GATES], _gate_tile(w_in1[:, P1_GATES:])], axis=1)
    zeros_h = jnp.zeros((A_HEADS,), F32)
    alog = _gate_tile(jnp.concatenate([zeros_h, p["gdn_a_log"]]).reshape(1, -1))
    dtb = _gate_tile(jnp.concatenate([zeros_h, p["gdn_dt_bias"]]).reshape(1, -1))
    gate_bias = _gate_tile(jnp.concatenate([p["mlstm_b_i"], p["mlstm_b_f"]]).reshape(1, -1))
    w2a2 = jnp.zeros((2 * B_LORA, 2 * B_W), F32)
    w2a2 = w2a2.at[0:B_LORA, 0:B_W].set(p["rwkv_w2"]).at[B_LORA:, B_W:].set(p["rwkv_a2"])
    row = lambda v: v.reshape(1, -1).astype(F32)
    return dict(
        w0=w0.astype(BF16), w1=w1.astype(BF16), wb=w_in0[:, a_cols:].astype(BF16),
        woa=p["w_out0"][0:A_VAL].astype(BF16), wob=p["w_out0"][A_VAL:].astype(BF16),
        wo1=p["w_out1"].astype(BF16), w2a2=w2a2.astype(BF16),
        norm0=row(p["norm0_w"]), norm1=row(p["norm1_w"]), normf=row(p["norm_f_w"]),
        conv_w=p["gdn_conv_w"].astype(F32),
        alog=alog, dtb=dtb,
        gnw=row(p["gdn_norm_w"]),
        mu=row(p["rwkv_mu"]), rw0=row(p["rwkv_w0"]), ra0=row(p["rwkv_a0"]),
        k_k=row(p["rwkv_k_k"]), k_a=row(p["rwkv_k_a"]), r_k=row(p["rwkv_r_k"]),
        ln_w=row(p["rwkv_ln_w"]), ln_b=row(p["rwkv_ln_b"]),
        gate_bias=gate_bias,
    )


def _tiles(bsz, seq):
    chunk = math.gcd(seq, MAX_CHUNK)
    tb = min(seq, 256)
    nb = 2 if seq > MAX_CHUNK else 8
    nb = math.gcd(nb, bsz)
    tm = min(bsz * seq, 512)
    return chunk, tb, nb, tm


def _trunk(x, state, q):
    conv0, s_gdn, x_prev, s_rwkv, c0, n0, m0 = state
    bsz, seq, _ = x.shape
    chunk, tb, nb, tm = _tiles(bsz, seq)
    m = bsz * seq
    x2d = x.reshape(m, D_MODEL)

    proj0 = _norm_proj(x2d, q["norm0"], q["w0"], tm).reshape(bsz, seq, P0_PAD)
    xn_last, prev_row = _aux(x[:, -1, :], x_prev, q["norm0"], q["wb"])
    o_a, conv_new, s_gdn_new = _gdn(proj0, conv0, s_gdn, q["conv_w"], q["alog"], q["dtb"], q["gnw"],
                                    nb, tb, chunk)
    o_b, s_rwkv_new = _rwkv(proj0, prev_row.reshape(bsz, 1, B_COLS), s_rwkv, q["mu"], q["rw0"], q["ra0"],
                            q["w2a2"], q["k_k"], q["k_a"], q["r_k"], q["ln_w"], q["ln_b"],
                            nb, tb, chunk)
    x1, proj1 = _mid(x2d, o_a.reshape(m, A_VAL), o_b.reshape(m, B_W), q["woa"], q["wob"],
                     q["norm1"], q["w1"], tm)
    h, c_new, n_new, m_new = _mlstm(proj1.reshape(bsz, seq, P1_PAD), c0,
                                    n0.reshape(bsz, C_HEADS // 2, LANES),
                                    m0.reshape(bsz, 1, C_HEADS), q["gate_bias"], nb, tb, chunk)
    y = _final(x1, h.reshape(m, C_V), q["wo1"], q["normf"], tm)
    return y.reshape(bsz, seq, D_MODEL), (conv_new, s_gdn_new, xn_last, s_rwkv_new, c_new,
                                          n_new.reshape(bsz, C_HEADS, C_DQK), m_new[:, :, 0])


def kernel(x_prompt, x_sample, state_gdn_conv, state_gdn, state_rwkv_shift, state_rwkv,
           state_mlstm_c, state_mlstm_n, state_mlstm_m,
           norm0_w, w_in0, w_out0, gdn_conv_w, gdn_a_log, gdn_dt_bias, gdn_norm_w,
           rwkv_mu, rwkv_w0, rwkv_w2, rwkv_a0, rwkv_a2, rwkv_k_k, rwkv_k_a, rwkv_r_k,
           rwkv_ln_w, rwkv_ln_b, norm1_w, w_in1, w_out1, mlstm_b_i, mlstm_b_f, norm_f_w):
    q = _prepare_params(dict(
        norm0_w=norm0_w, w_in0=w_in0, w_out0=w_out0, gdn_conv_w=gdn_conv_w, gdn_a_log=gdn_a_log,
        gdn_dt_bias=gdn_dt_bias, gdn_norm_w=gdn_norm_w, rwkv_mu=rwkv_mu, rwkv_w0=rwkv_w0,
        rwkv_w2=rwkv_w2, rwkv_a0=rwkv_a0, rwkv_a2=rwkv_a2, rwkv_k_k=rwkv_k_k, rwkv_k_a=rwkv_k_a,
        rwkv_r_k=rwkv_r_k, rwkv_ln_w=rwkv_ln_w, rwkv_ln_b=rwkv_ln_b, norm1_w=norm1_w, w_in1=w_in1,
        w_out1=w_out1, mlstm_b_i=mlstm_b_i, mlstm_b_f=mlstm_b_f, norm_f_w=norm_f_w))
    bsz = x_prompt.shape[0]
    prompt_state = (
        jnp.zeros((bsz, A_CONV - 1, A_QKV), F32),
        jnp.zeros((bsz, A_HEADS, A_DK, A_DV), F32),
        jnp.zeros((bsz, D_MODEL), F32),
        jnp.zeros((bsz, B_HEADS, B_N, B_N), F32),
        jnp.zeros((bsz, C_HEADS, C_DQK, C_DV), F32),
        jnp.zeros((bsz, C_HEADS, C_DQK), F32),
        jnp.zeros((bsz, C_HEADS), F32),
    )
    y_p, sp = _trunk(x_prompt, prompt_state, q)
    sample_state = (state_gdn_conv, state_gdn, state_rwkv_shift, state_rwkv,
                    state_mlstm_c, state_mlstm_n, state_mlstm_m)
    y_s, ss = _trunk(x_sample, sample_state, q)
    return (y_p, y_s) + tuple(sp) + tuple(ss)
```

```python
import functools
import math

import jax
import jax.numpy as jnp
from jax import lax
from jax.experimental import pallas as pl
from jax.experimental.pallas import tpu as pltpu

F32 = jnp.float32
BF16 = jnp.bfloat16

D_MODEL = 1024
EPS = 1e-6
LANES = 128
SUBLANES = 8
A_HEADS, A_DK, A_DV, A_CONV = 4, 128, 128, 4
A_KEY = A_HEADS * A_DK
A_VAL = A_HEADS * A_DV
A_QKV = 2 * A_KEY + A_VAL
A_MAIN = A_QKV + A_VAL
B_HEADS, B_N, B_LORA = 8, 64, 64
B_W = B_HEADS * B_N
B_COLS = 4 * B_W + 2 * B_LORA
B_PAIRS = B_HEADS // 2
B_GN_EPS = 64e-5
B_DECAY_SCALE = math.exp(-0.5)
C_HEADS, C_DQK, C_DV = 8, 64, 128
C_QK = C_HEADS * C_DQK
C_V = C_HEADS * C_DV
C_GATE_CAP = 15.0
P0_GATES = A_MAIN
P0_B = A_MAIN + LANES
P0_PAD = P0_B + B_COLS
P1_GATES = 2 * C_QK + 3 * C_V
P1_PAD = P1_GATES + LANES
MAX_CHUNK = 64
VMEM_LIMIT = 56 * 1024 * 1024


def _dot(a, b):
    return jnp.dot(a.astype(BF16), b.astype(BF16), preferred_element_type=F32)


def _dot_nt(a, b):
    return lax.dot_general(a.astype(BF16), b.astype(BF16), (((1,), (1,)), ((), ())),
                           preferred_element_type=F32)


def _dot_tn(a, b):
    return lax.dot_general(a.astype(BF16), b.astype(BF16), (((0,), (0,)), ((), ())),
                           preferred_element_type=F32)


def _split(x, terms):
    out = []
    for _ in range(terms - 1):
        hi = x.astype(BF16)
        out.append(hi)
        x = x - hi.astype(F32)
    out.append(x.astype(BF16))
    return out


def _dot_exact_lhs(sel, x, terms=3):
    s = sel.astype(BF16)
    return jnp.dot(jnp.concatenate([s] * terms, axis=1), jnp.concatenate(_split(x, terms), axis=0),
                   preferred_element_type=F32)


def _dot_exact_rhs(x, sel, terms=2):
    s = sel.astype(BF16)
    return jnp.dot(jnp.concatenate(_split(x, terms), axis=1), jnp.concatenate([s] * terms, axis=0),
                   preferred_element_type=F32)


def _iota2(shape, dim):
    return lax.broadcasted_iota(jnp.int32, shape, dim)


def _tri_incl(n):
    return (_iota2((n, n), 0) >= _iota2((n, n), 1))


def _col_to_row(col):
    n = col.shape[0]
    eye = _iota2((n, n), 0) == _iota2((n, n), 1)
    return jnp.sum(jnp.where(eye, col, 0.0), axis=0, keepdims=True)


def _row_to_col(row):
    n = row.shape[1]
    eye = _iota2((n, n), 0) == _iota2((n, n), 1)
    return jnp.sum(jnp.where(eye, row, 0.0), axis=1, keepdims=True)


GATE_COPIES = 3


def _replicate(x, sel, group):
    hi, mid, lo = _split(x, GATE_COPIES)
    lane = _iota2(x.shape, 1)
    packed = jnp.where(lane < group, hi, jnp.where(lane < 2 * group, mid, lo))
    packed = jnp.where(lane < GATE_COPIES * group, packed, jnp.zeros_like(packed))
    return jnp.dot(packed, sel, preferred_element_type=F32)


def _replicate_selector(group):
    assert group & (group - 1) == 0 and LANES == 128
    k = _iota2((LANES, group * LANES), 0)
    col = _iota2((LANES, group * LANES), 1)
    hit = (k < GATE_COPIES * group) & (jnp.bitwise_and(k, group - 1) == jnp.right_shift(col, 7))
    return jnp.where(hit, 1.0, 0.0).astype(BF16)


def _cummax_rows(x):
    n = x.shape[0]
    row = _iota2(x.shape, 0)
    sh = 1
    while sh < n:
        x = jnp.maximum(x, jnp.where(row >= sh, pltpu.roll(x, sh, 0), -jnp.inf))
        sh *= 2
    return x


def _inv_unit_lower(p, levels):
    n = p.shape[0]
    eye = (_iota2((n, n), 0) == _iota2((n, n), 1)).astype(F32)
    t = eye + p
    m = 2
    if m < levels:
        p = _dot(p, p)
        yield
    while m < levels:
        m *= 2
        if m >= levels:
            t = t + _dot(p, t)
        elif n % LANES == 0:
            both = _dot(p, jnp.concatenate([p, t], axis=1))
            p, t = both[:, 0:n], t + both[:, n:2 * n]
            yield
        else:
            both = _dot(jnp.concatenate([p, t], axis=0), p)
            p, t = both[0:n, :], t + both[n:2 * n, :]
            yield
    return t


def _interleave(chains):
    live = list(chains)
    while live:
        nxt = []
        for g in live:
            try:
                next(g)
                nxt.append(g)
            except StopIteration:
                pass
        live = nxt


def _sigmoid(x):
    return 1.0 / (1.0 + jnp.exp(-x))


def _silu(x):
    return x * _sigmoid(x)


def _softplus(x):
    return jnp.maximum(x, 0.0) + jnp.log1p(jnp.exp(-jnp.abs(x)))


def _rms(x, w):
    return x * lax.rsqrt(jnp.mean(x * x, axis=-1, keepdims=True) + EPS) * w


def _chunk_loop(n_chunks, body):
    if n_chunks == 1:
        body(0)
    else:
        def step(c, carry):
            body(c)
            return carry
        lax.fori_loop(0, n_chunks, step, 0)


def _row_start(c, chunk):
    if isinstance(c, int):
        return c * chunk
    return pl.multiple_of(c * chunk, chunk)


def _resident(shape):
    zeros = (0,) * len(shape)
    return pl.BlockSpec(shape, lambda *_: zeros, pipeline_mode=pl.Buffered(1))


def _norm_proj_kernel(x_ref, nw_ref, w_ref, o_ref):
    xn = _rms(x_ref[...], nw_ref[...])
    o_ref[...] = jnp.dot(xn.astype(BF16), w_ref[...], preferred_element_type=F32)


def _norm_proj(x2d, norm_w, w_bf16, tm):
    m = x2d.shape[0]
    n = w_bf16.shape[1]
    return pl.pallas_call(
        _norm_proj_kernel,
        grid=(m // tm,),
        in_specs=[pl.BlockSpec((tm, D_MODEL), lambda i: (i, 0)),
                  _resident((1, D_MODEL)),
                  _resident((D_MODEL, n))],
        out_specs=pl.BlockSpec((tm, n), lambda i: (i, 0)),
        out_shape=jax.ShapeDtypeStruct((m, n), F32),
        compiler_params=pltpu.CompilerParams(dimension_semantics=("parallel",),
                                             vmem_limit_bytes=VMEM_LIMIT),
        name="norm_proj0",
    )(x2d, norm_w, w_bf16)


def _mid_kernel(x_ref, oa_ref, ob_ref, woa_ref, wob_ref, nw_ref, wi_ref, x1_ref, p_ref):
    y = jnp.dot(oa_ref[...].astype(BF16), woa_ref[...], preferred_element_type=F32)
    y = y + jnp.dot(ob_ref[...].astype(BF16), wob_ref[...], preferred_element_type=F32)
    x1 = x_ref[...] + y
    x1_ref[...] = x1
    xn = _rms(x1, nw_ref[...])
    p_ref[...] = jnp.dot(xn.astype(BF16), wi_ref[...], preferred_element_type=F32)


def _mid(x2d, oa, ob, woa, wob, norm_w, wi, tm):
    m = x2d.shape[0]
    n = wi.shape[1]
    row = lambda i: (i, 0)
    return pl.pallas_call(
        _mid_kernel,
        grid=(m // tm,),
        in_specs=[pl.BlockSpec((tm, D_MODEL), row),
                  pl.BlockSpec((tm, A_VAL), row),
                  pl.BlockSpec((tm, B_W), lambda i: (i, A_VAL // B_W)),
                  _resident((A_VAL, D_MODEL)),
                  _resident((B_W, D_MODEL)),
                  _resident((1, D_MODEL)),
                  _resident((D_MODEL, n))],
        out_specs=[pl.BlockSpec((tm, D_MODEL), row), pl.BlockSpec((tm, n), row)],
        out_shape=[jax.ShapeDtypeStruct((m, D_MODEL), F32), jax.ShapeDtypeStruct((m, n), F32)],
        compiler_params=pltpu.CompilerParams(dimension_semantics=("parallel",),
                                             vmem_limit_bytes=VMEM_LIMIT),
        name="out0_norm_proj1",
    )(x2d, oa, ob, woa, wob, norm_w, wi)


def _final_kernel(x_ref, h_ref, wo_ref, nw_ref, y_ref):
    x2 = x_ref[...] + jnp.dot(h_ref[...].astype(BF16), wo_ref[...], preferred_element_type=F32)
    y_ref[...] = _rms(x2, nw_ref[...])


def _final(x2d, h, wo, norm_w, tm):
    m = x2d.shape[0]
    row = lambda i: (i, 0)
    return pl.pallas_call(
        _final_kernel,
        grid=(m // tm,),
        in_specs=[pl.BlockSpec((tm, D_MODEL), row),
                  pl.BlockSpec((tm, C_V), row),
                  _resident((C_V, D_MODEL)),
                  _resident((1, D_MODEL))],
        out_specs=pl.BlockSpec((tm, D_MODEL), row),
        out_shape=jax.ShapeDtypeStruct((m, D_MODEL), F32),
        compiler_params=pltpu.CompilerParams(dimension_semantics=("parallel",),
                                             vmem_limit_bytes=VMEM_LIMIT),
        name="out1_norm_f",
    )(x2d, h, wo, norm_w)


def _aux_kernel(xl_ref, xp_ref, nw_ref, wb_ref, xn_ref, prev_ref):
    xn_ref[...] = _rms(xl_ref[...], nw_ref[...])
    prev_ref[...] = jnp.dot(xp_ref[...].astype(BF16), wb_ref[...], preferred_element_type=F32)


def _aux(x_last, x_prev, norm_w, wb):
    b = x_last.shape[0]
    return pl.pallas_call(
        _aux_kernel,
        out_shape=[jax.ShapeDtypeStruct((b, D_MODEL), F32), jax.ShapeDtypeStruct((b, B_COLS), F32)],
        compiler_params=pltpu.CompilerParams(vmem_limit_bytes=VMEM_LIMIT),
        name="shift_rows",
    )(x_last, x_prev, norm_w, wb)


def _gdn_kernel(main_ref, gates_ref, conv0_ref, s0_ref, convw_ref, alog_ref, dtb_ref, gnw_ref,
                o_ref, convn_ref, sout_ref, ext_s, qkv_s, gsc_s, st_s, uw_s, attn_s, qg_s, kd_s, ge_s,
                *, nb, tb, chunk):
    i = pl.program_id(1)
    n_chunks = tb // chunk
    tail = SUBLANES

    @pl.when(i == 0)
    def _():
        for bi in range(nb):
            ext_s[bi, 0:tail, :] = jnp.zeros((tail, A_QKV), F32)
            ext_s[bi, tail - (A_CONV - 1):tail, :] = conv0_ref[bi]
        st_s[...] = s0_ref[...]

    @pl.when(i > 0)
    def _():
        for bi in range(nb):
            ext_s[bi, 0:tail, :] = ext_s[bi, tb:tb + tail, :]

    cw = convw_ref[...]
    for bi in range(nb):
        ext_s[bi, tail:tail + tb, :] = main_ref[bi, :, 0:A_QKV]
        for c in range(n_chunks):
            r0 = c * chunk
            y = ext_s[bi, tail + r0:tail + r0 + chunk, :] * cw[A_CONV - 1:A_CONV, :]
            for j in range(A_CONV - 1):
                sh = A_CONV - 1 - j
                y = y + ext_s[bi, tail + r0 - sh:tail + r0 - sh + chunk, :] * cw[j:j + 1, :]
            y = _silu(y)
            for h in range(A_HEADS):
                q = y[:, h * A_DK:(h + 1) * A_DK]
                qkv_s[bi, r0:r0 + chunk, h * A_DK:(h + 1) * A_DK] = (
                    q * lax.rsqrt(jnp.sum(q * q, axis=-1, keepdims=True) + EPS) * (A_DK ** -0.5))
                k = y[:, A_KEY + h * A_DK:A_KEY + (h + 1) * A_DK]
                qkv_s[bi, r0:r0 + chunk, A_KEY + h * A_DK:A_KEY + (h + 1) * A_DK] = (
                    k * lax.rsqrt(jnp.sum(k * k, axis=-1, keepdims=True) + EPS))
            qkv_s[bi, r0:r0 + chunk, 2 * A_KEY:A_QKV] = y[:, 2 * A_KEY:A_QKV]
        g = gates_ref[bi]
        gsc_s[bi, :, 0:LANES] = _sigmoid(g)
        gsc_s[bi, :, LANES:2 * LANES] = -jnp.exp(alog_ref[...]) * _softplus(g + dtb_ref[...])

    n_gates = 2 * A_HEADS
    is_decay = jnp.bitwise_and(_iota2((1, LANES), 1), n_gates - 1) >= A_HEADS
    sel = _replicate_selector(n_gates)
    tri = _tri_incl(chunk)
    strict = _iota2((chunk, chunk), 0) > _iota2((chunk, chunk), 1)
    gnw = gnw_ref[...]

    blocks = [(bi, c) for c in range(n_chunks) for bi in range(nb)]
    gcum = {(bi, c): _dot_exact_lhs(tri, gsc_s[bi, c * chunk:(c + 1) * chunk, LANES:2 * LANES])
            for bi, c in blocks}
    rep = {(bi, c): _replicate(jnp.where(is_decay, gcum[bi, c], gsc_s[bi, c * chunk:(c + 1) * chunk, 0:LANES]),
                               sel, n_gates) for bi, c in blocks}

    def local_chain(bi, c, h):
        rows = slice(c * chunk, (c + 1) * chunk)
        q = qkv_s[bi, rows, h * A_DK:(h + 1) * A_DK]
        k = qkv_s[bi, rows, A_KEY + h * A_DK:A_KEY + (h + 1) * A_DK]
        v = qkv_s[bi, rows, 2 * A_KEY + h * A_DV:2 * A_KEY + (h + 1) * A_DV]
        qk_k = _dot_nt(jnp.concatenate([q, k], axis=0), k)
        yield
        b_rep = rep[bi, c][:, h * LANES:(h + 1) * LANES]
        g_rep = rep[bi, c][:, (A_HEADS + h) * LANES:(A_HEADS + h + 1) * LANES]
        g_row = _col_to_row(g_rep[:, 0:chunk])
        g_last = g_rep[chunk - 1:chunk, :]
        decay = jnp.where(tri, jnp.exp(jnp.where(tri, g_rep[:, 0:chunk] - g_row, 0.0)), 0.0)
        eg = jnp.exp(g_rep)
        qg_s[bi, rows, h * A_DK:(h + 1) * A_DK] = q * eg
        kd_s[bi, rows, h * A_DK:(h + 1) * A_DK] = k * jnp.exp(g_last - g_rep)
        ge_s[bi, c, h:h + 1, :] = jnp.exp(g_last)
        attn_s[bi, rows, h * LANES:h * LANES + chunk] = decay * qk_k[0:chunk, :]
        amat = jnp.where(strict, b_rep[:, 0:chunk] * decay * qk_k[chunk:2 * chunk, :], 0.0)
        t_inv = yield from _inv_unit_lower(-amat, chunk)
        uw_s[bi, rows, 2 * h * LANES:2 * (h + 1) * LANES] = _dot(
            t_inv, jnp.concatenate([b_rep * v, (b_rep * eg) * k], axis=1))

    _interleave(local_chain(bi, c, h) for c in range(n_chunks) for bi in range(nb) for h in range(A_HEADS))

    def body(c):
        r0 = _row_start(c, chunk)
        rows = pl.ds(r0, chunk)

        def state_chain(bi, h):
            uw = uw_s[bi, rows, 2 * h * LANES:2 * (h + 1) * LANES]
            s = st_s[bi, h]
            wq_s = _dot(jnp.concatenate([uw[:, A_DV:A_DV + A_DK], qg_s[bi, rows, h * A_DK:(h + 1) * A_DK]], axis=0), s)
            yield
            nv = uw[:, 0:A_DV] - wq_s[0:chunk, :]
            o = wq_s[chunk:2 * chunk, :] + _dot(attn_s[bi, rows, h * LANES:h * LANES + chunk], nv)
            st_s[bi, h] = ge_s[bi, c, h:h + 1, :] * s + _dot_tn(kd_s[bi, rows, h * A_DK:(h + 1) * A_DK], nv)
            yield
            z = main_ref[bi, rows, A_QKV + h * A_DV:A_QKV + (h + 1) * A_DV]
            o_ref[bi, rows, h * A_DV:(h + 1) * A_DV] = _rms(o, gnw) * _silu(z)

        _interleave(state_chain(bi, h) for bi in range(nb) for h in range(A_HEADS))

    _chunk_loop(n_chunks, body)

    @pl.when(i == pl.num_programs(1) - 1)
    def _():
        for bi in range(nb):
            convn_ref[bi] = ext_s[bi, tb + tail - (A_CONV - 1):tb + tail, :]
        sout_ref[...] = st_s[...]


def _gdn(proj3d, conv0, s0, conv_w, alog_row, dtb_row, gn_w, nb, tb, chunk):
    bsz, seq, _ = proj3d.shape
    blk = lambda b, i: (b, i, 0)
    per_b3 = lambda b, i: (b, 0, 0)
    per_b4 = lambda b, i: (b, 0, 0, 0)
    fix2 = lambda b, i: (0, 0)
    kern = functools.partial(_gdn_kernel, nb=nb, tb=tb, chunk=chunk)
    return pl.pallas_call(
        kern,
        grid=(bsz // nb, seq // tb),
        in_specs=[pl.BlockSpec((nb, tb, A_MAIN), blk),
                  pl.BlockSpec((nb, tb, LANES), lambda b, i: (b, i, P0_GATES // LANES)),
                  pl.BlockSpec((nb, A_CONV - 1, A_QKV), per_b3),
                  pl.BlockSpec((nb, A_HEADS, A_DK, A_DV), per_b4),
                  pl.BlockSpec((A_CONV, A_QKV), fix2),
                  pl.BlockSpec((1, LANES), fix2),
                  pl.BlockSpec((1, LANES), fix2),
                  pl.BlockSpec((1, A_DV), fix2)],
        out_specs=[pl.BlockSpec((nb, tb, A_VAL), blk),
                   pl.BlockSpec((nb, A_CONV - 1, A_QKV), per_b3),
                   pl.BlockSpec((nb, A_HEADS, A_DK, A_DV), per_b4)],
        out_shape=[jax.ShapeDtypeStruct((bsz, seq, A_VAL), F32),
                   jax.ShapeDtypeStruct((bsz, A_CONV - 1, A_QKV), F32),
                   jax.ShapeDtypeStruct((bsz, A_HEADS, A_DK, A_DV), F32)],
        scratch_shapes=[pltpu.VMEM((nb, tb + SUBLANES, A_QKV), F32),
                        pltpu.VMEM((nb, tb, A_QKV), F32),
                        pltpu.VMEM((nb, tb, 2 * LANES), F32),
                        pltpu.VMEM((nb, A_HEADS, A_DK, A_DV), F32),
                        pltpu.VMEM((nb, tb, A_HEADS * (A_DV + A_DK)), F32),
                        pltpu.VMEM((nb, tb, A_HEADS * LANES), F32),
                        pltpu.VMEM((nb, tb, A_KEY), F32),
                        pltpu.VMEM((nb, tb, A_KEY), F32),
                        pltpu.VMEM((nb, tb // chunk, SUBLANES, LANES), F32)],
        compiler_params=pltpu.CompilerParams(dimension_semantics=("parallel", "arbitrary"),
                                             vmem_limit_bytes=VMEM_LIMIT),
        name="gdn_mixer",
    )(proj3d, proj3d, conv0, s0, conv_w, alog_row, dtb_row, gn_w)


_FR, _FLW, _FK, _FV, _FA, _FB, _FG = range(7)


def _rwkv_kernel(pb_ref, prev_ref, s0_ref, mu_ref, w0_ref, a0_ref, w2a2_ref, kk_ref, ka_ref,
                 rk_ref, lnw_ref, lnb_ref, o_ref, sout_ref, ext_s, feat_s, st_s, tmp_s,
                 *, nb, tb, chunk):
    i = pl.program_id(1)
    n_chunks = tb // chunk
    tail = SUBLANES
    lane = _iota2((1, LANES), 1)
    head0 = lane < B_N
    ones_blk = (_iota2((LANES, LANES), 0) // B_N == _iota2((LANES, LANES), 1) // B_N)

    @pl.when(i == 0)
    def _():
        for bi in range(nb):
            ext_s[bi, 0:tail, :] = jnp.zeros((tail, B_COLS), F32)
            ext_s[bi, tail - 1:tail, :] = prev_ref[bi]
            for p in range(B_PAIRS):
                tmp_s[...] = jnp.zeros((B_N, LANES), F32)
                tmp_s[:, 0:B_N] = s0_ref[bi, 2 * p + 1]
                shifted = pltpu.roll(tmp_s[...], B_N, 1)
                st_s[bi, p, B_N:LANES, :] = jnp.where(head0, 0.0, shifted)
                tmp_s[:, 0:B_N] = s0_ref[bi, 2 * p]
                st_s[bi, p, 0:B_N, :] = tmp_s[...]

    @pl.when(i > 0)
    def _():
        for bi in range(nb):
            ext_s[bi, 0:tail, :] = ext_s[bi, tb:tb + tail, :]

    mu = mu_ref[...]
    for bi in range(nb):
        ext_s[bi, tail:tail + tb, :] = pb_ref[bi]
        for c in range(n_chunks):
            r0 = c * chunk
            pb = ext_s[bi, tail + r0:tail + r0 + chunk, :]
            pb_prev = ext_s[bi, tail + r0 - 1:tail + r0 - 1 + chunk, :]
            pbs = pb + (pb_prev - pb) * mu
            r = pbs[:, 0:B_W]
            kb = pbs[:, B_W:2 * B_W]
            vb = pbs[:, 2 * B_W:3 * B_W]
            gb = pbs[:, 3 * B_W:4 * B_W]
            lo = pbs[:, 4 * B_W:B_COLS]
            lo = jnp.where(head0, jnp.tanh(lo), lo)
            wa = _dot(lo, w2a2_ref[...])
            logw = -B_DECAY_SCALE * _sigmoid(w0_ref[...] + wa[:, 0:B_W])
            a = _sigmoid(a0_ref[...] + wa[:, B_W:2 * B_W])
            kkx = kb * kk_ref[...]
            rws = slice(r0, r0 + chunk)
            for p in range(B_PAIRS):
                sl = slice(p * LANES, (p + 1) * LANES)
                kp = kkx[:, sl]
                ss = _dot_exact_rhs(kp * kp, ones_blk)
                kkn = kp * lax.rsqrt(ss + EPS)
                feat_s[bi, rws, _FA * B_W + p * LANES:_FA * B_W + (p + 1) * LANES] = -kkn
                feat_s[bi, rws, _FB * B_W + p * LANES:_FB * B_W + (p + 1) * LANES] = kkn * a[:, sl]
            feat_s[bi, rws, _FR * B_W:(_FR + 1) * B_W] = r
            feat_s[bi, rws, _FLW * B_W:(_FLW + 1) * B_W] = logw
            feat_s[bi, rws, _FK * B_W:(_FK + 1) * B_W] = kb * (1.0 + (a - 1.0) * ka_ref[...])
            feat_s[bi, rws, _FV * B_W:(_FV + 1) * B_W] = vb
            feat_s[bi, rws, _FG * B_W:(_FG + 1) * B_W] = gb

    tri = _tri_incl(chunk)
    c2 = 2 * chunk
    ri = _iota2((c2, c2), 0)
    ci = _iota2((c2, c2), 1)
    same_head = (ri // chunk) == (ci // chunk)
    incl_bd = same_head & (ri >= ci)
    strict_bd = same_head & (ri > ci)
    ri4 = _iota2((2 * c2, c2), 0)
    ci4 = _iota2((2 * c2, c2), 1)
    rr4 = ri4 % c2
    lm_mask = ((rr4 // chunk) == (ci4 // chunk)) & ((rr4 > ci4) | ((ri4 >= c2) & (rr4 == ci4)))
    lane_st = (_iota2((c2, LANES), 0) // chunk) == (_iota2((c2, LANES), 1) // B_N)

    def body(c):
        r0 = _row_start(c, chunk)
        rows = pl.ds(r0, chunk)

        def pair_chain(bi, p):
            def feat(slot):
                return feat_s[bi, rows, slot * B_W + p * LANES:slot * B_W + (p + 1) * LANES]
            r, lw, k, v, a_, b_, gate = (feat(s) for s in (_FR, _FLW, _FK, _FV, _FA, _FB, _FG))
            cs = _dot_exact_lhs(tri, lw)
            yield
            cs_last = cs[chunk - 1:chunk, :]
            e_inv = jnp.exp(-cs)
            e_rem = jnp.exp(cs_last - cs)
            at = a_ * jnp.exp(cs - lw)
            rt = r * jnp.exp(cs)
            bt = b_ * e_inv
            kt = k * e_inv
            a_st = jnp.concatenate([jnp.where(head0, at, 0.0), jnp.where(head0, 0.0, at)], axis=0)
            r_st = jnp.concatenate([jnp.where(head0, rt, 0.0), jnp.where(head0, 0.0, rt)], axis=0)
            b2 = jnp.concatenate([bt, bt], axis=0)
            k2 = jnp.concatenate([kt, kt], axis=0)
            v2 = jnp.concatenate([v, v], axis=0)
            ar_st = jnp.concatenate([a_st, r_st], axis=0)
            if c2 % LANES == 0:
                ar_bk = _dot_nt(ar_st, jnp.concatenate([b2, k2], axis=0))
                ar_b, ar_k = ar_bk[:, 0:c2], ar_bk[:, c2:2 * c2]
            else:
                ar_b = _dot_nt(ar_st, b2)
                ar_k = _dot_nt(ar_st, k2)
            s = st_s[bi, p]
            ar = _dot_nt(jnp.concatenate([at, rt], axis=0), s)
            yield
            l_ab = jnp.where(strict_bd, ar_b[0:c2, :], 0.0)
            m_rb = jnp.where(incl_bd, ar_b[c2:2 * c2, :], 0.0)
            lm_k = jnp.where(lm_mask, ar_k, 0.0)
            lm_v = _dot(lm_k, v2)
            t_inv = yield from _inv_unit_lower(l_ab, chunk)
            ah = ar[0:chunk, :]
            rh = ar[chunk:c2, :]
            x_st = jnp.where(lane_st, jnp.concatenate([ah, ah], axis=0) + lm_v[0:c2, :], 0.0)
            u_st = _dot(t_inv, x_st)
            yield
            o_st = jnp.where(lane_st, _dot(m_rb, u_st) + lm_v[c2:2 * c2, :], 0.0)
            u = u_st[0:chunk, :] + u_st[chunk:c2, :]
            upd = _dot_tn(jnp.concatenate([u, v], axis=0),
                          jnp.concatenate([b_ * e_rem, k * e_rem], axis=0))
            yield
            o = rh + o_st[0:chunk, :] + o_st[chunk:c2, :]
            st_s[bi, p] = s * jnp.exp(cs_last) + jnp.where(ones_blk, upd, 0.0)
            sl = slice(p * LANES, (p + 1) * LANES)
            mean = _dot_exact_rhs(o, ones_blk) * (1.0 / B_N)
            bonus = _dot_exact_rhs(r * k * rk_ref[:, sl], ones_blk) * v
            yield
            d = o - mean
            var = _dot_exact_rhs(d * d, ones_blk) * (1.0 / B_N)
            yield
            ob = d * lax.rsqrt(var + B_GN_EPS) * lnw_ref[:, sl] + lnb_ref[:, sl] + bonus
            o_ref[bi, rows, sl] = ob * _silu(gate)

        _interleave(pair_chain(bi, p) for bi in range(nb) for p in range(B_PAIRS))

    _chunk_loop(n_chunks, body)

    @pl.when(i == pl.num_programs(1) - 1)
    def _():
        for bi in range(nb):
            for p in range(B_PAIRS):
                sout_ref[bi, 2 * p] = st_s[bi, p, 0:B_N, 0:B_N]
                shifted = pltpu.roll(st_s[bi, p, B_N:LANES, :], B_N, 1)
                sout_ref[bi, 2 * p + 1] = shifted[:, 0:B_N]


def _rwkv(proj3d, prev_row, s0, mu, w0, a0, w2a2, k_k, k_a, r_k, ln_w, ln_b, nb, tb, chunk):
    bsz, seq, _ = proj3d.shape
    blk = lambda b, i: (b, i, 0)
    per_b3 = lambda b, i: (b, 0, 0)
    per_b4 = lambda b, i: (b, 0, 0, 0)
    fix2 = lambda b, i: (0, 0)
    kern = functools.partial(_rwkv_kernel, nb=nb, tb=tb, chunk=chunk)
    vec = pl.BlockSpec((1, B_W), fix2)
    return pl.pallas_call(
        kern,
        grid=(bsz // nb, seq // tb),
        in_specs=[pl.BlockSpec((nb, tb, B_COLS), lambda b, i: (b, i, P0_B // B_COLS)),
                  pl.BlockSpec((nb, 1, B_COLS), per_b3),
                  pl.BlockSpec((nb, B_HEADS, B_N, B_N), per_b4),
                  pl.BlockSpec((1, B_COLS), fix2),
                  vec, vec,
                  pl.BlockSpec((2 * B_LORA, 2 * B_W), fix2),
                  vec, vec, vec, vec, vec],
        out_specs=[pl.BlockSpec((nb, tb, B_W), blk),
                   pl.BlockSpec((nb, B_HEADS, B_N, B_N), per_b4)],
        out_shape=[jax.ShapeDtypeStruct((bsz, seq, B_W), F32),
                   jax.ShapeDtypeStruct((bsz, B_HEADS, B_N, B_N), F32)],
        scratch_shapes=[pltpu.VMEM((nb, tb + SUBLANES, B_COLS), F32),
                        pltpu.VMEM((nb, tb, 7 * B_W), F32),
                        pltpu.VMEM((nb, B_PAIRS, LANES, LANES), F32),
                        pltpu.VMEM((B_N, LANES), F32)],
        compiler_params=pltpu.CompilerParams(dimension_semantics=("parallel", "arbitrary"),
                                             vmem_limit_bytes=VMEM_LIMIT),
        name="rwkv7_mixer",
    )(proj3d, prev_row, s0, mu, w0, a0, w2a2, k_k, k_a, r_k, ln_w, ln_b)


def _run_tasks(tasks, caps):
    done, live, pending = set(), [], list(tasks)
    while pending or live:
        count = {}
        for _, group, _ in live:
            count[group] = count.get(group, 0) + 1
        still = []
        for key, group, deps, make in pending:
            if all(d in done for d in deps) and count.get(group, 0) < caps.get(group, len(tasks)):
                live.append((key, group, make()))
                count[group] = count.get(group, 0) + 1
            else:
                still.append((key, group, deps, make))
        assert live, "task graph has a cycle or a missing dependency"
        pending = still
        nxt = []
        for key, group, gen in live:
            try:
                next(gen)
                nxt.append((key, group, gen))
            except StopIteration:
                done.add(key)
        live = nxt


def _layer0_kernel(main_ref, gates_ref, pb_ref, conv0_ref, sg0_ref, prev_ref, sr0_ref,
                   convw_ref, alog_ref, dtb_ref, gnw_ref, mu_ref, w0_ref, a0_ref, w2a2_ref, kk_ref, ka_ref,
                   rk_ref, lnw_ref, lnb_ref,
                   o_ref, convn_ref, sgout_ref, srout_ref,
                   ext_a, qkv_s, gsc_s, stg_s, uw_s, attn_s, qg_s, kd_s, ge_s,
                   ext_b, feat_s, str_s, tmp_s, *, nb, tb, chunk):
    i = pl.program_id(1)
    n_chunks = tb // chunk
    tail = SUBLANES
    lane = _iota2((1, LANES), 1)
    head0 = lane < B_N
    ones_blk = (_iota2((LANES, LANES), 0) // B_N == _iota2((LANES, LANES), 1) // B_N)

    @pl.when(i == 0)
    def _():
        for bi in range(nb):
            ext_a[bi, 0:tail, :] = jnp.zeros((tail, A_QKV), F32)
            ext_a[bi, tail - (A_CONV - 1):tail, :] = conv0_ref[bi]
            ext_b[bi, 0:tail, :] = jnp.zeros((tail, B_COLS), F32)
            ext_b[bi, tail - 1:tail, :] = prev_ref[bi]
            for p in range(B_PAIRS):
                tmp_s[...] = jnp.zeros((B_N, LANES), F32)
                tmp_s[:, 0:B_N] = sr0_ref[bi, 2 * p + 1]
                shifted = pltpu.roll(tmp_s[...], B_N, 1)
                str_s[bi, p, B_N:LANES, :] = jnp.where(head0, 0.0, shifted)
                tmp_s[:, 0:B_N] = sr0_ref[bi, 2 * p]
                str_s[bi, p, 0:B_N, :] = tmp_s[...]
        stg_s[...] = sg0_ref[...]

    @pl.when(i > 0)
    def _():
        for bi in range(nb):
            ext_a[bi, 0:tail, :] = ext_a[bi, tb:tb + tail, :]
            ext_b[bi, 0:tail, :] = ext_b[bi, tb:tb + tail, :]

    for bi in range(nb):
        ext_a[bi, tail:tail + tb, :] = main_ref[bi, :, 0:A_QKV]
        ext_b[bi, tail:tail + tb, :] = pb_ref[bi]
        g = gates_ref[bi]
        gsc_s[bi, :, 0:LANES] = _sigmoid(g)
        gsc_s[bi, :, LANES:2 * LANES] = -jnp.exp(alog_ref[...]) * _softplus(g + dtb_ref[...])

    cw = convw_ref[...]
    n_gates = 2 * A_HEADS
    is_decay = jnp.bitwise_and(lane, n_gates - 1) >= A_HEADS
    sel = _replicate_selector(n_gates)
    tri = _tri_incl(chunk)
    strict = _iota2((chunk, chunk), 0) > _iota2((chunk, chunk), 1)
    gnw = gnw_ref[...]
    rep = {}

    def gdn_pre(bi, c):
        r0 = c * chunk
        y = ext_a[bi, tail + r0:tail + r0 + chunk, :] * cw[A_CONV - 1:A_CONV, :]
        for j in range(A_CONV - 1):
            sh = A_CONV - 1 - j
            y = y + ext_a[bi, tail + r0 - sh:tail + r0 - sh + chunk, :] * cw[j:j + 1, :]
        yield
        y = _silu(y)
        for h in range(A_HEADS):
            q = y[:, h * A_DK:(h + 1) * A_DK]
            qkv_s[bi, r0:r0 + chunk, h * A_DK:(h + 1) * A_DK] = (
                q * lax.rsqrt(jnp.sum(q * q, axis=-1, keepdims=True) + EPS) * (A_DK ** -0.5))
            k = y[:, A_KEY + h * A_DK:A_KEY + (h + 1) * A_DK]
            qkv_s[bi, r0:r0 + chunk, A_KEY + h * A_DK:A_KEY + (h + 1) * A_DK] = (
                k * lax.rsqrt(jnp.sum(k * k, axis=-1, keepdims=True) + EPS))
        qkv_s[bi, r0:r0 + chunk, 2 * A_KEY:A_QKV] = y[:, 2 * A_KEY:A_QKV]

    def gdn_gates(bi, c):
        rows = slice(c * chunk, (c + 1) * chunk)
        gcum = _dot_exact_lhs(tri, gsc_s[bi, rows, LANES:2 * LANES])
        yield
        rep[bi, c] = _replicate(jnp.where(is_decay, gcum, gsc_s[bi, rows, 0:LANES]), sel, n_gates)

    def gdn_local(bi, c, h):
        rows = slice(c * chunk, (c + 1) * chunk)
        q = qkv_s[bi, rows, h * A_DK:(h + 1) * A_DK]
        k = qkv_s[bi, rows, A_KEY + h * A_DK:A_KEY + (h + 1) * A_DK]
        v = qkv_s[bi, rows, 2 * A_KEY + h * A_DV:2 * A_KEY + (h + 1) * A_DV]
        qk_k = _dot_nt(jnp.concatenate([q, k], axis=0), k)
        yield
        b_rep = rep[bi, c][:, h * LANES:(h + 1) * LANES]
        g_rep = rep[bi, c][:, (A_HEADS + h) * LANES:(A_HEADS + h + 1) * LANES]
        g_row = _col_to_row(g_rep[:, 0:chunk])
        g_last = g_rep[chunk - 1:chunk, :]
        decay = jnp.where(tri, jnp.exp(jnp.where(tri, g_rep[:, 0:chunk] - g_row, 0.0)), 0.0)
        eg = jnp.exp(g_rep)
        qg_s[bi, rows, h * A_DK:(h + 1) * A_DK] = q * eg
        kd_s[bi, rows, h * A_DK:(h + 1) * A_DK] = k * jnp.exp(g_last - g_rep)
        ge_s[bi, c, h:h + 1, :] = jnp.exp(g_last)
        attn_s[bi, rows, h * LANES:h * LANES + chunk] = decay * qk_k[0:chunk, :]
        amat = jnp.where(strict, b_rep[:, 0:chunk] * decay * qk_k[chunk:2 * chunk, :], 0.0)
        t_inv = yield from _inv_unit_lower(-amat, chunk)
        uw_s[bi, rows, 2 * h * LANES:2 * (h + 1) * LANES] = _dot(
            t_inv, jnp.concatenate([b_rep * v, (b_rep * eg) * k], axis=1))

    def gdn_state(bi, c, h):
        rows = slice(c * chunk, (c + 1) * chunk)
        uw = uw_s[bi, rows, 2 * h * LANES:2 * (h + 1) * LANES]
        s = stg_s[bi, h]
        wq_s = _dot(jnp.concatenate([uw[:, A_DV:A_DV + A_DK], qg_s[bi, rows, h * A_DK:(h + 1) * A_DK]], axis=0), s)
        yield
        nv = uw[:, 0:A_DV] - wq_s[0:chunk, :]
        o = wq_s[chunk:2 * chunk, :] + _dot(attn_s[bi, rows, h * LANES:h * LANES + chunk], nv)
        stg_s[bi, h] = ge_s[bi, c, h:h + 1, :] * s + _dot_tn(kd_s[bi, rows, h * A_DK:(h + 1) * A_DK], nv)
        yield
        z = main_ref[bi, rows, A_QKV + h * A_DV:A_QKV + (h + 1) * A_DV]
        o_ref[bi, rows, h * A_DV:(h + 1) * A_DV] = _rms(o, gnw) * _silu(z)

    mu = mu_ref[...]
    c2 = 2 * chunk
    ri = _iota2((c2, c2), 0)
    ci = _iota2((c2, c2), 1)
    same_head = (ri // chunk) == (ci // chunk)
    incl_bd = same_head & (ri >= ci)
    strict_bd = same_head & (ri > ci)
    ri4 = _iota2((2 * c2, c2), 0)
    ci4 = _iota2((2 * c2, c2), 1)
    rr4 = ri4 % c2
    lm_mask = ((rr4 // chunk) == (ci4 // chunk)) & ((rr4 > ci4) | ((ri4 >= c2) & (rr4 == ci4)))
    lane_st = (_iota2((c2, LANES), 0) // chunk) == (_iota2((c2, LANES), 1) // B_N)

    def rwkv_pre(bi, c):
        r0 = c * chunk
        rws = slice(r0, r0 + chunk)
        pb = ext_b[bi, tail + r0:tail + r0 + chunk, :]
        pb_prev = ext_b[bi, tail + r0 - 1:tail + r0 - 1 + chunk, :]
        pbs = pb + (pb_prev - pb) * mu
        lo = pbs[:, 4 * B_W:B_COLS]
        lo = jnp.where(head0, jnp.tanh(lo), lo)
        wa = _dot(lo, w2a2_ref[...])
        kb = pbs[:, B_W:2 * B_W]
        kkx = kb * kk_ref[...]
        ss = [_dot_exact_rhs(kkx[:, p * LANES:(p + 1) * LANES] ** 2, ones_blk) for p in range(B_PAIRS)]
        feat_s[bi, rws, _FR * B_W:(_FR + 1) * B_W] = pbs[:, 0:B_W]
        feat_s[bi, rws, _FV * B_W:(_FV + 1) * B_W] = pbs[:, 2 * B_W:3 * B_W]
        feat_s[bi, rws, _FG * B_W:(_FG + 1) * B_W] = pbs[:, 3 * B_W:4 * B_W]
        yield
        logw = -B_DECAY_SCALE * _sigmoid(w0_ref[...] + wa[:, 0:B_W])
        a = _sigmoid(a0_ref[...] + wa[:, B_W:2 * B_W])
        for p in range(B_PAIRS):
            sl = slice(p * LANES, (p + 1) * LANES)
            kkn = kkx[:, sl] * lax.rsqrt(ss[p] + EPS)
            feat_s[bi, rws, _FA * B_W + p * LANES:_FA * B_W + (p + 1) * LANES] = -kkn
            feat_s[bi, rws, _FB * B_W + p * LANES:_FB * B_W + (p + 1) * LANES] = kkn * a[:, sl]
        feat_s[bi, rws, _FLW * B_W:(_FLW + 1) * B_W] = logw
        feat_s[bi, rws, _FK * B_W:(_FK + 1) * B_W] = kb * (1.0 + (a - 1.0) * ka_ref[...])

    local = {}

    def rwkv_local(bi, c, p):
        rows = slice(c * chunk, (c + 1) * chunk)

        def feat(slot):
            return feat_s[bi, rows, slot * B_W + p * LANES:slot * B_W + (p + 1) * LANES]
        r, lw, k, v, a_, b_ = (feat(s) for s in (_FR, _FLW, _FK, _FV, _FA, _FB))
        cs = _dot_exact_lhs(tri, lw)
        yield
        cs_last = cs[chunk - 1:chunk, :]
        e_inv = jnp.exp(-cs)
        e_rem = jnp.exp(cs_last - cs)
        at = a_ * jnp.exp(cs - lw)
        rt = r * jnp.exp(cs)
        bt = b_ * e_inv
        kt = k * e_inv
        a_st = jnp.concatenate([jnp.where(head0, at, 0.0), jnp.where(head0, 0.0, at)], axis=0)
        r_st = jnp.concatenate([jnp.where(head0, rt, 0.0), jnp.where(head0, 0.0, rt)], axis=0)
        b2 = jnp.concatenate([bt, bt], axis=0)
        k2 = jnp.concatenate([kt, kt], axis=0)
        v2 = jnp.concatenate([v, v], axis=0)
        ar_st = jnp.concatenate([a_st, r_st], axis=0)
        if c2 % LANES == 0:
            ar_bk = _dot_nt(ar_st, jnp.concatenate([b2, k2], axis=0))
            ar_b, ar_k = ar_bk[:, 0:c2], ar_bk[:, c2:2 * c2]
        else:
            ar_b = _dot_nt(ar_st, b2)
            ar_k = _dot_nt(ar_st, k2)
        yield
        l_ab = jnp.where(strict_bd, ar_b[0:c2, :], 0.0)
        m_rb = jnp.where(incl_bd, ar_b[c2:2 * c2, :], 0.0)
        lm_k = jnp.where(lm_mask, ar_k, 0.0)
        lm_v = _dot(lm_k, v2)
        sl = slice(p * LANES, (p + 1) * LANES)
        bonus = _dot_exact_rhs(r * k * rk_ref[:, sl], ones_blk) * v
        t_inv = yield from _inv_unit_lower(l_ab, chunk)
        local[bi, c, p] = (t_inv, lm_v, m_rb, jnp.concatenate([at, rt], axis=0),
                           jnp.concatenate([b_ * e_rem, k * e_rem], axis=0), jnp.exp(cs_last), bonus)

    def rwkv_state(bi, c, p):
        rows = slice(c * chunk, (c + 1) * chunk)
        sl = slice(p * LANES, (p + 1) * LANES)
        t_inv, lm_v, m_rb, at_rt, bk_rem, decay_end, bonus = local.pop((bi, c, p))
        v = feat_s[bi, rows, _FV * B_W + p * LANES:_FV * B_W + (p + 1) * LANES]
        gate = feat_s[bi, rows, _FG * B_W + p * LANES:_FG * B_W + (p + 1) * LANES]
        s = str_s[bi, p]
        ar = _dot_nt(at_rt, s)
        yield
        ah = ar[0:chunk, :]
        rh = ar[chunk:c2, :]
        x_st = jnp.where(lane_st, jnp.concatenate([ah, ah], axis=0) + lm_v[0:c2, :], 0.0)
        u_st = _dot(t_inv, x_st)
        yield
        o_st = jnp.where(lane_st, _dot(m_rb, u_st) + lm_v[c2:2 * c2, :], 0.0)
        u = u_st[0:chunk, :] + u_st[chunk:c2, :]
        upd = _dot_tn(jnp.concatenate([u, v], axis=0), bk_rem)
        yield
        o = rh + o_st[0:chunk, :] + o_st[chunk:c2, :]
        str_s[bi, p] = s * decay_end + jnp.where(ones_blk, upd, 0.0)
        mean = _dot_exact_rhs(o, ones_blk) * (1.0 / B_N)
        yield
        d = o - mean
        var = _dot_exact_rhs(d * d, ones_blk) * (1.0 / B_N)
        yield
        ob = d * lax.rsqrt(var + B_GN_EPS) * lnw_ref[:, sl] + lnb_ref[:, sl] + bonus
        o_ref[bi, rows, A_VAL + p * LANES:A_VAL + (p + 1) * LANES] = ob * _silu(gate)

    tasks = []
    blocks = [(bi, c) for c in range(n_chunks) for bi in range(nb)]
    for bi, c in blocks:
        tasks.append((("rp", bi, c), "rwkv_pre", [], functools.partial(rwkv_pre, bi, c)))
    for bi, c in blocks:
        for p in range(B_PAIRS):
            deps = [("rl", bi, c, p)] + ([("rs", bi, c - 1, p)] if c else [])
            tasks.append((("rs", bi, c, p), "rwkv_state", deps, functools.partial(rwkv_state, bi, c, p)))
        for h in range(A_HEADS):
            deps = [("gl", bi, c, h)] + ([("gs", bi, c - 1, h)] if c else [])
            tasks.append((("gs", bi, c, h), "gdn_state", deps, functools.partial(gdn_state, bi, c, h)))
    for bi, c in blocks:
        for p in range(B_PAIRS):
            tasks.append((("rl", bi, c, p), "rwkv_local", [("rp", bi, c)], functools.partial(rwkv_local, bi, c, p)))
        for h in range(A_HEADS):
            tasks.append((("gl", bi, c, h), "gdn_local", [("gp", bi, c), ("gg", bi, c)],
                          functools.partial(gdn_local, bi, c, h)))
    for bi, c in blocks:
        tasks.append((("gg", bi, c), "gdn_gates", [], functools.partial(gdn_gates, bi, c)))
        tasks.append((("gp", bi, c), "gdn_pre", [], functools.partial(gdn_pre, bi, c)))
    _run_tasks(tasks, {"rwkv_pre": 2, "gdn_pre": 2, "gdn_gates": 2,
                       "rwkv_local": 2 * B_PAIRS, "gdn_local": 2 * A_HEADS})

    @pl.when(i == pl.num_programs(1) - 1)
    def _():
        for bi in range(nb):
            convn_ref[bi] = ext_a[bi, tb + tail - (A_CONV - 1):tb + tail, :]
            for p in range(B_PAIRS):
                srout_ref[bi, 2 * p] = str_s[bi, p, 0:B_N, 0:B_N]
                shifted = pltpu.roll(str_s[bi, p, B_N:LANES, :], B_N, 1)
                srout_ref[bi, 2 * p + 1] = shifted[:, 0:B_N]
        sgout_ref[...] = stg_s[...]


def _layer0_mixers(proj3d, conv0, s_gdn, prev_row, s_rwkv, q, nb, tb, chunk):
    bsz, seq, _ = proj3d.shape
    blk = lambda b, i: (b, i, 0)
    per_b3 = lambda b, i: (b, 0, 0)
    per_b4 = lambda b, i: (b, 0, 0, 0)
    fix2 = lambda b, i: (0, 0)
    kern = functools.partial(_layer0_kernel, nb=nb, tb=tb, chunk=chunk)
    lanes = pl.BlockSpec((1, LANES), fix2)
    vec = pl.BlockSpec((1, B_W), fix2)
    return pl.pallas_call(
        kern,
        grid=(bsz // nb, seq // tb),
        in_specs=[pl.BlockSpec((nb, tb, A_MAIN), blk),
                  pl.BlockSpec((nb, tb, LANES), lambda b, i: (b, i, P0_GATES // LANES)),
                  pl.BlockSpec((nb, tb, B_COLS), lambda b, i: (b, i, P0_B // B_COLS)),
                  pl.BlockSpec((nb, A_CONV - 1, A_QKV), per_b3),
                  pl.BlockSpec((nb, A_HEADS, A_DK, A_DV), per_b4),
                  pl.BlockSpec((nb, 1, B_COLS), per_b3),
                  pl.BlockSpec((nb, B_HEADS, B_N, B_N), per_b4),
                  pl.BlockSpec((A_CONV, A_QKV), fix2), lanes, lanes, pl.BlockSpec((1, A_DV), fix2),
                  pl.BlockSpec((1, B_COLS), fix2), vec, vec,
                  pl.BlockSpec((2 * B_LORA, 2 * B_W), fix2),
                  vec, vec, vec, vec, vec],
        out_specs=[pl.BlockSpec((nb, tb, A_VAL + B_W), blk),
                   pl.BlockSpec((nb, A_CONV - 1, A_QKV), per_b3),
                   pl.BlockSpec((nb, A_HEADS, A_DK, A_DV), per_b4),
                   pl.BlockSpec((nb, B_HEADS, B_N, B_N), per_b4)],
        out_shape=[jax.ShapeDtypeStruct((bsz, seq, A_VAL + B_W), F32),
                   jax.ShapeDtypeStruct((bsz, A_CONV - 1, A_QKV), F32),
                   jax.ShapeDtypeStruct((bsz, A_HEADS, A_DK, A_DV), F32),
                   jax.ShapeDtypeStruct((bsz, B_HEADS, B_N, B_N), F32)],
        scratch_shapes=[pltpu.VMEM((nb, tb + SUBLANES, A_QKV), F32),
                        pltpu.VMEM((nb, tb, A_QKV), F32),
                        pltpu.VMEM((nb, tb, 2 * LANES), F32),
                        pltpu.VMEM((nb, A_HEADS, A_DK, A_DV), F32),
                        pltpu.VMEM((nb, tb, A_HEADS * (A_DV + A_DK)), F32),
                        pltpu.VMEM((nb, tb, A_HEADS * LANES), F32),
                        pltpu.VMEM((nb, tb, A_KEY), F32),
                        pltpu.VMEM((nb, tb, A_KEY), F32),
                        pltpu.VMEM((nb, tb // chunk, SUBLANES, LANES), F32),
                        pltpu.VMEM((nb, tb + SUBLANES, B_COLS), F32),
                        pltpu.VMEM((nb, tb, 7 * B_W), F32),
                        pltpu.VMEM((nb, B_PAIRS, LANES, LANES), F32),
                        pltpu.VMEM((B_N, LANES), F32)],
        compiler_params=pltpu.CompilerParams(dimension_semantics=("parallel", "arbitrary"),
                                             vmem_limit_bytes=VMEM_LIMIT),
        name="layer0_mixers",
    )(proj3d, proj3d, proj3d, conv0, s_gdn, prev_row, s_rwkv,
      q["conv_w"], q["alog"], q["dtb"], q["gnw"], q["mu"], q["rw0"], q["ra0"], q["w2a2"],
      q["k_k"], q["k_a"], q["r_k"], q["ln_w"], q["ln_b"])


def _mlstm_kernel(p_ref, c0_ref, n0_ref, m0_ref, bias_ref, o_ref, cout_ref, nout_ref, mout_ref,
                  gsc_s, caug_s, m_s, *, nb, tb, chunk):
    i = pl.program_id(1)
    n_chunks = tb // chunk
    lane = _iota2((1, LANES), 1)
    krow = _iota2((LANES, 1), 0)
    n_gates = 2 * C_HEADS
    is_f = jnp.bitwise_and(lane, n_gates - 1) >= C_HEADS

    @pl.when(i == 0)
    def _():
        for bi in range(nb):
            m0 = m0_ref[bi]
            for h in range(C_HEADS):
                j = h % 2
                caug_s[bi, h] = jnp.zeros((LANES, 2 * C_DV), F32)
                caug_s[bi, h, j * C_DQK:(j + 1) * C_DQK, 0:C_DV] = c0_ref[bi, h]
                n_col = _row_to_col(n0_ref[bi, h // 2:h // 2 + 1, :])
                own = (krow // C_DQK) == j
                caug_s[bi, h, :, C_DV:2 * C_DV] = jnp.where(own, jnp.broadcast_to(n_col, (LANES, C_DV)), 0.0)
                m_s[bi, h:h + 1, :] = jnp.broadcast_to(m0[:, h:h + 1], (1, LANES))

    for bi in range(nb):
        g = p_ref[bi, :, P1_GATES:P1_PAD] + bias_ref[...]
        cap = C_GATE_CAP * jnp.tanh(g / C_GATE_CAP)
        gsc_s[bi] = jnp.where(is_f, -_softplus(-cap), cap)

    tri = _tri_incl(chunk)
    sel = _replicate_selector(n_gates)
    ones_blk = jnp.ones((chunk, C_DV), F32)
    off = 2 * C_QK

    def body(c):
        r0 = _row_start(c, chunk)
        rows = pl.ds(r0, chunk)
        gl = [gsc_s[bi, rows, :] for bi in range(nb)]
        bcum = [_dot_exact_lhs(tri, g) for g in gl]
        rep = [_replicate(jnp.where(is_f, b, g), sel, n_gates) for g, b in zip(gl, bcum)]

        def head_chain(bi, h):
            p, j = h // 2, h % 2
            own_lane = (lane // C_DQK) == j
            own_row = (krow // C_DQK) == j
            qp = p_ref[bi, rows, p * LANES:(p + 1) * LANES] * (C_DQK ** -0.5)
            kp = p_ref[bi, rows, C_QK + p * LANES:C_QK + (p + 1) * LANES]
            v = p_ref[bi, rows, off + h * C_DV:off + (h + 1) * C_DV]
            vaug = jnp.concatenate([v, ones_blk], axis=1)
            qk = _dot_nt(jnp.where(own_lane, qp, 0.0), kp)
            cm = caug_s[bi, h]
            qc = _dot(qp, cm)
            i_rep = rep[bi][:, h * LANES:(h + 1) * LANES]
            b_rep = rep[bi][:, (C_HEADS + h) * LANES:(C_HEADS + h + 1) * LANES]
            b_row = _col_to_row(b_rep[:, 0:chunk])
            i_row = _col_to_row(i_rep[:, 0:chunk])
            dmat = jnp.where(tri, b_rep[:, 0:chunk] - b_row + i_row, -jnp.inf)
            dmax = b_rep + _cummax_rows(i_rep - b_rep)
            m = m_s[bi, h:h + 1, :]
            inter = b_rep + m
            mt = jnp.maximum(inter, dmax)
            m_new = mt[chunk - 1:chunk, :]
            b_last = b_rep[chunk - 1:chunk, :]
            w_s = jnp.exp(b_last - b_rep + i_rep - m_new)
            e_c = jnp.exp(b_last + m - m_new)
            e_inter = jnp.exp(inter - mt)
            upd = _dot_tn(w_s * kp, vaug)
            yield
            pm = jnp.exp(dmat - mt[:, 0:chunk]) * qk
            pv = _dot(pm, vaug)
            caug_s[bi, h] = jnp.concatenate([e_c, e_c], axis=1) * cm + jnp.where(own_row, upd, 0.0)
            m_s[bi, h:h + 1, :] = m_new
            yield
            num = e_inter * qc[:, 0:C_DV] + pv[:, 0:C_DV]
            den = e_inter * qc[:, C_DV:2 * C_DV] + pv[:, C_DV:2 * C_DV]
            hh = num / jnp.maximum(jnp.abs(den), jnp.exp(-mt))
            og = p_ref[bi, rows, off + C_V + h * C_DV:off + C_V + (h + 1) * C_DV]
            z = p_ref[bi, rows, off + 2 * C_V + h * C_DV:off + 2 * C_V + (h + 1) * C_DV]
            o_ref[bi, rows, h * C_DV:(h + 1) * C_DV] = hh * _sigmoid(og) * _silu(z)

        _interleave(head_chain(bi, h) for bi in range(nb) for h in range(C_HEADS))

    _chunk_loop(n_chunks, body)

    @pl.when(i == pl.num_programs(1) - 1)
    def _():
        for bi in range(nb):
            for h in range(C_HEADS):
                j = h % 2
                cout_ref[bi, h] = caug_s[bi, h, j * C_DQK:(j + 1) * C_DQK, 0:C_DV]
            for p in range(C_HEADS // 2):
                n_col = caug_s[bi, 2 * p, :, C_DV:C_DV + 1] + caug_s[bi, 2 * p + 1, :, C_DV:C_DV + 1]
                nout_ref[bi, p:p + 1, :] = _col_to_row(n_col)
        mout_ref[...] = m_s[...]


def _mlstm(proj3d, c0, n0, m0, bias_row, nb, tb, chunk):
    bsz, seq, _ = proj3d.shape
    blk = lambda b, i: (b, i, 0)
    per_b3 = lambda b, i: (b, 0, 0)
    per_b4 = lambda b, i: (b, 0, 0, 0)
    kern = functools.partial(_mlstm_kernel, nb=nb, tb=tb, chunk=chunk)
    pairs = C_HEADS // 2
    return pl.pallas_call(
        kern,
        grid=(bsz // nb, seq // tb),
        in_specs=[pl.BlockSpec((nb, tb, P1_PAD), blk),
                  pl.BlockSpec((nb, C_HEADS, C_DQK, C_DV), per_b4),
                  pl.BlockSpec((nb, pairs, LANES), per_b3),
                  pl.BlockSpec((nb, 1, C_HEADS), per_b3),
                  pl.BlockSpec((1, LANES), lambda b, i: (0, 0))],
        out_specs=[pl.BlockSpec((nb, tb, C_V), blk),
                   pl.BlockSpec((nb, C_HEADS, C_DQK, C_DV), per_b4),
                   pl.BlockSpec((nb, pairs, LANES), per_b3),
                   pl.BlockSpec((nb, C_HEADS, LANES), per_b3)],
        out_shape=[jax.ShapeDtypeStruct((bsz, seq, C_V), F32),
                   jax.ShapeDtypeStruct((bsz, C_HEADS, C_DQK, C_DV), F32),
                   jax.ShapeDtypeStruct((bsz, pairs, LANES), F32),
                   jax.ShapeDtypeStruct((bsz, C_HEADS, LANES), F32)],
        scratch_shapes=[pltpu.VMEM((nb, tb, LANES), F32),
                        pltpu.VMEM((nb, C_HEADS, LANES, 2 * C_DV), F32),
                        pltpu.VMEM((nb, C_HEADS, LANES), F32)],
        compiler_params=pltpu.CompilerParams(dimension_semantics=("parallel", "arbitrary"),
                                             vmem_limit_bytes=VMEM_LIMIT),
        name="mlstm_mixer",
    )(proj3d, c0, n0, m0, bias_row)


def _gate_tile(cols):
    tiled = jnp.concatenate([cols.astype(F32)] * GATE_COPIES, axis=1)
    return jnp.pad(tiled, ((0, 0), (0, LANES - tiled.shape[1])))


def _prepare_params(p):
    w_in0 = p["w_in0"]
    a_cols = A_MAIN + 2 * A_HEADS
    w0 = jnp.concatenate([w_in0[:, 0:A_MAIN], _gate_tile(w_in0[:, A_MAIN:a_cols]), w_in0[:, a_cols:]], axis=1)
    w_in1 = p["w_in1"]
    w1 = jnp.concatenate([w_in1[:, 0:P1_GATES], _gate_tile(w_in1[:, P1_GATES:])], axis=1)
    zeros_h = jnp.zeros((A_HEADS,), F32)
    alog = _gate_tile(jnp.concatenate([zeros_h, p["gdn_a_log"]]).reshape(1, -1))
    dtb = _gate_tile(jnp.concatenate([zeros_h, p["gdn_dt_bias"]]).reshape(1, -1))
    gate_bias = _gate_tile(jnp.concatenate([p["mlstm_b_i"], p["mlstm_b_f"]]).reshape(1, -1))
    w2a2 = jnp.zeros((2 * B_LORA, 2 * B_W), F32)
    w2a2 = w2a2.at[0:B_LORA, 0:B_W].set(p["rwkv_w2"]).at[B_LORA:, B_W:].set(p["rwkv_a2"])
    row = lambda v: v.reshape(1, -1).astype(F32)
    return dict(
        w0=w0.astype(BF16), w1=w1.astype(BF16), wb=w_in0[:, a_cols:].astype(BF16),
        woa=p["w_out0"][0:A_VAL].astype(BF16), wob=p["w_out0"][A_VAL:].astype(BF16),
        wo1=p["w_out1"].astype(BF16), w2a2=w2a2.astype(BF16),
        norm0=row(p["norm0_w"]), norm1=row(p["norm1_w"]), normf=row(p["norm_f_w"]),
        conv_w=p["gdn_conv_w"].astype(F32),
        alog=alog, dtb=dtb,
        gnw=row(p["gdn_norm_w"]),
        mu=row(p["rwkv_mu"]), rw0=row(p["rwkv_w0"]), ra0=row(p["rwkv_a0"]),
        k_k=row(p["rwkv_k_k"]), k_a=row(p["rwkv_k_a"]), r_k=row(p["rwkv_r_k"]),
        ln_w=row(p["rwkv_ln_w"]), ln_b=row(p["rwkv_ln_b"]),
        gate_bias=gate_bias,
    )


def _tiles(bsz, seq):
    chunk = math.gcd(seq, MAX_CHUNK)
    tb = min(seq, 256)
    nb = 2 if seq > MAX_CHUNK else 8
    nb = math.gcd(nb, bsz)
    tm = min(bsz * seq, 512)
    return chunk, tb, nb, tm


def _trunk(x, state, q):
    conv0, s_gdn, x_prev, s_rwkv, c0, n0, m0 = state
    bsz, seq, _ = x.shape
    chunk, tb, nb, tm = _tiles(bsz, seq)
    m = bsz * seq
    x2d = x.reshape(m, D_MODEL)

    proj0 = _norm_proj(x2d, q["norm0"], q["w0"], tm).reshape(bsz, seq, P0_PAD)
    xn_last, prev_row = _aux(x[:, -1, :], x_prev, q["norm0"], q["wb"])
    mix, conv_new, s_gdn_new, s_rwkv_new = _layer0_mixers(
        proj0, conv0, s_gdn, prev_row.reshape(bsz, 1, B_COLS), s_rwkv, q, nb, tb, chunk)
    mix2d = mix.reshape(m, A_VAL + B_W)
    x1, proj1 = _mid(x2d, mix2d, mix2d, q["woa"], q["wob"], q["norm1"], q["w1"], tm)
    h, c_new, n_new, m_new = _mlstm(proj1.reshape(bsz, seq, P1_PAD), c0,
                                    n0.reshape(bsz, C_HEADS // 2, LANES),
                                    m0.reshape(bsz, 1, C_HEADS), q["gate_bias"], nb, tb, chunk)
    y = _final(x1, h.reshape(m, C_V), q["wo1"], q["normf"], tm)
    return y.reshape(bsz, seq, D_MODEL), (conv_new, s_gdn_new, xn_last, s_rwkv_new, c_new,
                                          n_new.reshape(bsz, C_HEADS, C_DQK), m_new[:, :, 0])


def kernel(x_prompt, x_sample, state_gdn_conv, state_gdn, state_rwkv_shift, state_rwkv,
           state_mlstm_c, state_mlstm_n, state_mlstm_m,
           norm0_w, w_in0, w_out0, gdn_conv_w, gdn_a_log, gdn_dt_bias, gdn_norm_w,
           rwkv_mu, rwkv_w0, rwkv_w2, rwkv_a0, rwkv_a2, rwkv_k_k, rwkv_k_a, rwkv_r_k,
           rwkv_ln_w, rwkv_ln_b, norm1_w, w_in1, w_out1, mlstm_b_i, mlstm_b_f, norm_f_w):
    q = _prepare_params(dict(
        norm0_w=norm0_w, w_in0=w_in0, w_out0=w_out0, gdn_conv_w=gdn_conv_w, gdn_a_log=gdn_a_log,
        gdn_dt_bias=gdn_dt_bias, gdn_norm_w=gdn_norm_w, rwkv_mu=rwkv_mu, rwkv_w0=rwkv_w0,
        rwkv_w2=rwkv_w2, rwkv_a0=rwkv_a0, rwkv_a2=rwkv_a2, rwkv_k_k=rwkv_k_k, rwkv_k_a=rwkv_k_a,
        rwkv_r_k=rwkv_r_k, rwkv_ln_w=rwkv_ln_w, rwkv_ln_b=rwkv_ln_b, norm1_w=norm1_w, w_in1=w_in1,
        w_out1=w_out1, mlstm_b_i=mlstm_b_i, mlstm_b_f=mlstm_b_f, norm_f_w=norm_f_w))
    bsz = x_prompt.shape[0]
    prompt_state = (
        jnp.zeros((bsz, A_CONV - 1, A_QKV), F32),
        jnp.zeros((bsz, A_HEADS, A_DK, A_DV), F32),
        jnp.zeros((bsz, D_MODEL), F32),
        jnp.zeros((bsz, B_HEADS, B_N, B_N), F32),
        jnp.zeros((bsz, C_HEADS, C_DQK, C_DV), F32),
        jnp.zeros((bsz, C_HEADS, C_DQK), F32),
        jnp.zeros((bsz, C_HEADS), F32),
    )
    y_p, sp = _trunk(x_prompt, prompt_state, q)
    sample_state = (state_gdn_conv, state_gdn, state_rwkv_shift, state_rwkv,
                    state_mlstm_c, state_mlstm_n, state_mlstm_m)
    y_s, ss = _trunk(x_sample, sample_state, q)
    return (y_p, y_s) + tuple(sp) + tuple(ss)
```

```python
import functools
import math

import jax
import jax.numpy as jnp
from jax import lax
from jax.experimental import pallas as pl
from jax.experimental.pallas import tpu as pltpu

F32 = jnp.float32
BF16 = jnp.bfloat16

D_MODEL = 1024
EPS = 1e-6
LANES = 128
SUBLANES = 8
A_HEADS, A_DK, A_DV, A_CONV = 4, 128, 128, 4
A_KEY = A_HEADS * A_DK
A_VAL = A_HEADS * A_DV
A_QKV = 2 * A_KEY + A_VAL
A_MAIN = A_QKV + A_VAL
B_HEADS, B_N, B_LORA = 8, 64, 64
B_W = B_HEADS * B_N
B_COLS = 4 * B_W + 2 * B_LORA
B_PAIRS = B_HEADS // 2
B_GN_EPS = 64e-5
B_DECAY_SCALE = math.exp(-0.5)
C_HEADS, C_DQK, C_DV = 8, 64, 128
C_QK = C_HEADS * C_DQK
C_V = C_HEADS * C_DV
C_GATE_CAP = 15.0
P0_GATES = A_MAIN
P0_B = A_MAIN + LANES
P0_PAD = P0_B + B_COLS
P1_GATES = 2 * C_QK + 3 * C_V
P1_PAD = P1_GATES + LANES
MAX_CHUNK = 64
VMEM_LIMIT = 56 * 1024 * 1024


def _dot(a, b):
    return jnp.dot(a.astype(BF16), b.astype(BF16), preferred_element_type=F32)


def _dot_nt(a, b):
    return lax.dot_general(a.astype(BF16), b.astype(BF16), (((1,), (1,)), ((), ())),
                           preferred_element_type=F32)


def _dot_tn(a, b):
    return lax.dot_general(a.astype(BF16), b.astype(BF16), (((0,), (0,)), ((), ())),
                           preferred_element_type=F32)


def _split(x, terms):
    out = []
    for _ in range(terms - 1):
        hi = x.astype(BF16)
        out.append(hi)
        x = x - hi.astype(F32)
    out.append(x.astype(BF16))
    return out


def _dot_exact_lhs(sel, x, terms=3):
    s = sel.astype(BF16)
    return jnp.dot(jnp.concatenate([s] * terms, axis=1), jnp.concatenate(_split(x, terms), axis=0),
                   preferred_element_type=F32)


def _dot_exact_rhs(x, sel, terms=2):
    s = sel.astype(BF16)
    return jnp.dot(jnp.concatenate(_split(x, terms), axis=1), jnp.concatenate([s] * terms, axis=0),
                   preferred_element_type=F32)


def _iota2(shape, dim):
    return lax.broadcasted_iota(jnp.int32, shape, dim)


def _tri_incl(n):
    return (_iota2((n, n), 0) >= _iota2((n, n), 1))


def _col_to_row(col):
    n = col.shape[0]
    eye = _iota2((n, n), 0) == _iota2((n, n), 1)
    return jnp.sum(jnp.where(eye, col, 0.0), axis=0, keepdims=True)


def _row_to_col(row):
    n = row.shape[1]
    eye = _iota2((n, n), 0) == _iota2((n, n), 1)
    return jnp.sum(jnp.where(eye, row, 0.0), axis=1, keepdims=True)


GATE_COPIES = 3


def _replicate(x, sel, group):
    hi, mid, lo = _split(x, GATE_COPIES)
    lane = _iota2(x.shape, 1)
    packed = jnp.where(lane < group, hi, jnp.where(lane < 2 * group, mid, lo))
    packed = jnp.where(lane < GATE_COPIES * group, packed, jnp.zeros_like(packed))
    return jnp.dot(packed, sel, preferred_element_type=F32)


def _replicate_selector(group):
    assert group & (group - 1) == 0 and LANES == 128
    k = _iota2((LANES, group * LANES), 0)
    col = _iota2((LANES, group * LANES), 1)
    hit = (k < GATE_COPIES * group) & (jnp.bitwise_and(k, group - 1) == jnp.right_shift(col, 7))
    return jnp.where(hit, 1.0, 0.0).astype(BF16)


def _cummax_rows(x):
    n = x.shape[0]
    row = _iota2(x.shape, 0)
    sh = 1
    while sh < n:
        x = jnp.maximum(x, jnp.where(row >= sh, pltpu.roll(x, sh, 0), -jnp.inf))
        sh *= 2
    return x


def _inv_unit_lower(p, levels):
    n = p.shape[0]
    eye = (_iota2((n, n), 0) == _iota2((n, n), 1)).astype(F32)
    t = eye + p
    m = 2
    if m < levels:
        p = _dot(p, p)
        yield
    while m < levels:
        m *= 2
        if m >= levels:
            t = t + _dot(p, t)
        elif n % LANES == 0:
            both = _dot(p, jnp.concatenate([p, t], axis=1))
            p, t = both[:, 0:n], t + both[:, n:2 * n]
            yield
        else:
            both = _dot(jnp.concatenate([p, t], axis=0), p)
            p, t = both[0:n, :], t + both[n:2 * n, :]
            yield
    return t


def _interleave(chains):
    live = list(chains)
    while live:
        nxt = []
        for g in live:
            try:
                next(g)
                nxt.append(g)
            except StopIteration:
                pass
        live = nxt


def _sigmoid(x):
    return 1.0 / (1.0 + jnp.exp(-x))


def _silu(x):
    return x * _sigmoid(x)


def _softplus(x):
    return jnp.maximum(x, 0.0) + jnp.log1p(jnp.exp(-jnp.abs(x)))


def _rms(x, w):
    return x * lax.rsqrt(jnp.mean(x * x, axis=-1, keepdims=True) + EPS) * w


def _chunk_loop(n_chunks, body):
    if n_chunks == 1:
        body(0)
    else:
        def step(c, carry):
            body(c)
            return carry
        lax.fori_loop(0, n_chunks, step, 0)


def _row_start(c, chunk):
    if isinstance(c, int):
        return c * chunk
    return pl.multiple_of(c * chunk, chunk)


def _resident(shape):
    zeros = (0,) * len(shape)
    return pl.BlockSpec(shape, lambda *_: zeros, pipeline_mode=pl.Buffered(1))


def _norm_proj_kernel(x_ref, nw_ref, w_ref, o_ref):
    xn = _rms(x_ref[...], nw_ref[...])
    o_ref[...] = jnp.dot(xn.astype(BF16), w_ref[...], preferred_element_type=F32)


def _norm_proj(x2d, norm_w, w_bf16, tm):
    m = x2d.shape[0]
    n = w_bf16.shape[1]
    return pl.pallas_call(
        _norm_proj_kernel,
        grid=(m // tm,),
        in_specs=[pl.BlockSpec((tm, D_MODEL), lambda i: (i, 0)),
                  _resident((1, D_MODEL)),
                  _resident((D_MODEL, n))],
        out_specs=pl.BlockSpec((tm, n), lambda i: (i, 0)),
        out_shape=jax.ShapeDtypeStruct((m, n), F32),
        compiler_params=pltpu.CompilerParams(dimension_semantics=("parallel",),
                                             vmem_limit_bytes=VMEM_LIMIT),
        name="norm_proj0",
    )(x2d, norm_w, w_bf16)


def _mid_kernel(x_ref, oa_ref, ob_ref, woa_ref, wob_ref, nw_ref, wi_ref, x1_ref, p_ref):
    y = jnp.dot(oa_ref[...].astype(BF16), woa_ref[...], preferred_element_type=F32)
    y = y + jnp.dot(ob_ref[...].astype(BF16), wob_ref[...], preferred_element_type=F32)
    x1 = x_ref[...] + y
    x1_ref[...] = x1
    xn = _rms(x1, nw_ref[...])
    p_ref[...] = jnp.dot(xn.astype(BF16), wi_ref[...], preferred_element_type=F32)


def _mid(x2d, oa, ob, woa, wob, norm_w, wi, tm):
    m = x2d.shape[0]
    n = wi.shape[1]
    row = lambda i: (i, 0)
    return pl.pallas_call(
        _mid_kernel,
        grid=(m // tm,),
        in_specs=[pl.BlockSpec((tm, D_MODEL), row),
                  pl.BlockSpec((tm, A_VAL), row),
                  pl.BlockSpec((tm, B_W), lambda i: (i, A_VAL // B_W)),
                  _resident((A_VAL, D_MODEL)),
                  _resident((B_W, D_MODEL)),
                  _resident((1, D_MODEL)),
                  _resident((D_MODEL, n))],
        out_specs=[pl.BlockSpec((tm, D_MODEL), row), pl.BlockSpec((tm, n), row)],
        out_shape=[jax.ShapeDtypeStruct((m, D_MODEL), F32), jax.ShapeDtypeStruct((m, n), F32)],
        compiler_params=pltpu.CompilerParams(dimension_semantics=("parallel",),
                                             vmem_limit_bytes=VMEM_LIMIT),
        name="out0_norm_proj1",
    )(x2d, oa, ob, woa, wob, norm_w, wi)


def _final_kernel(x_ref, h_ref, wo_ref, nw_ref, y_ref):
    x2 = x_ref[...] + jnp.dot(h_ref[...].astype(BF16), wo_ref[...], preferred_element_type=F32)
    y_ref[...] = _rms(x2, nw_ref[...])


def _final(x2d, h, wo, norm_w, tm):
    m = x2d.shape[0]
    row = lambda i: (i, 0)
    return pl.pallas_call(
        _final_kernel,
        grid=(m // tm,),
        in_specs=[pl.BlockSpec((tm, D_MODEL), row),
                  pl.BlockSpec((tm, C_V), row),
                  _resident((C_V, D_MODEL)),
                  _resident((1, D_MODEL))],
        out_specs=pl.BlockSpec((tm, D_MODEL), row),
        out_shape=jax.ShapeDtypeStruct((m, D_MODEL), F32),
        compiler_params=pltpu.CompilerParams(dimension_semantics=("parallel",),
                                             vmem_limit_bytes=VMEM_LIMIT),
        name="out1_norm_f",
    )(x2d, h, wo, norm_w)


def _aux_kernel(xl_ref, xp_ref, nw_ref, wb_ref, xn_ref, prev_ref):
    xn_ref[...] = _rms(xl_ref[...], nw_ref[...])
    prev_ref[...] = jnp.dot(xp_ref[...].astype(BF16), wb_ref[...], preferred_element_type=F32)


def _aux(x_last, x_prev, norm_w, wb):
    b = x_last.shape[0]
    return pl.pallas_call(
        _aux_kernel,
        out_shape=[jax.ShapeDtypeStruct((b, D_MODEL), F32), jax.ShapeDtypeStruct((b, B_COLS), F32)],
        compiler_params=pltpu.CompilerParams(vmem_limit_bytes=VMEM_LIMIT),
        name="shift_rows",
    )(x_last, x_prev, norm_w, wb)


def _gdn_kernel(main_ref, gates_ref, conv0_ref, s0_ref, convw_ref, alog_ref, dtb_ref, gnw_ref,
                o_ref, convn_ref, sout_ref, ext_s, qkv_s, gsc_s, st_s, uw_s, attn_s, qg_s, kd_s, ge_s,
                *, nb, tb, chunk):
    i = pl.program_id(1)
    n_chunks = tb // chunk
    tail = SUBLANES

    @pl.when(i == 0)
    def _():
        for bi in range(nb):
            ext_s[bi, 0:tail, :] = jnp.zeros((tail, A_QKV), F32)
            ext_s[bi, tail - (A_CONV - 1):tail, :] = conv0_ref[bi]
        st_s[...] = s0_ref[...]

    @pl.when(i > 0)
    def _():
        for bi in range(nb):
            ext_s[bi, 0:tail, :] = ext_s[bi, tb:tb + tail, :]

    cw = convw_ref[...]
    for bi in range(nb):
        ext_s[bi, tail:tail + tb, :] = main_ref[bi, :, 0:A_QKV]
        for c in range(n_chunks):
            r0 = c * chunk
            y = ext_s[bi, tail + r0:tail + r0 + chunk, :] * cw[A_CONV - 1:A_CONV, :]
            for j in range(A_CONV - 1):
                sh = A_CONV - 1 - j
                y = y + ext_s[bi, tail + r0 - sh:tail + r0 - sh + chunk, :] * cw[j:j + 1, :]
            y = _silu(y)
            for h in range(A_HEADS):
                q = y[:, h * A_DK:(h + 1) * A_DK]
                qkv_s[bi, r0:r0 + chunk, h * A_DK:(h + 1) * A_DK] = (
                    q * lax.rsqrt(jnp.sum(q * q, axis=-1, keepdims=True) + EPS) * (A_DK ** -0.5))
                k = y[:, A_KEY + h * A_DK:A_KEY + (h + 1) * A_DK]
                qkv_s[bi, r0:r0 + chunk, A_KEY + h * A_DK:A_KEY + (h + 1) * A_DK] = (
                    k * lax.rsqrt(jnp.sum(k * k, axis=-1, keepdims=True) + EPS))
            qkv_s[bi, r0:r0 + chunk, 2 * A_KEY:A_QKV] = y[:, 2 * A_KEY:A_QKV]
        g = gates_ref[bi]
        gsc_s[bi, :, 0:LANES] = _sigmoid(g)
        gsc_s[bi, :, LANES:2 * LANES] = -jnp.exp(alog_ref[...]) * _softplus(g + dtb_ref[...])

    n_gates = 2 * A_HEADS
    is_decay = jnp.bitwise_and(_iota2((1, LANES), 1), n_gates - 1) >= A_HEADS
    sel = _replicate_selector(n_gates)
    tri = _tri_incl(chunk)
    strict = _iota2((chunk, chunk), 0) > _iota2((chunk, chunk), 1)
    gnw = gnw_ref[...]

    blocks = [(bi, c) for c in range(n_chunks) for bi in range(nb)]
    gcum = {(bi, c): _dot_exact_lhs(tri, gsc_s[bi, c * chunk:(c + 1) * chunk, LANES:2 * LANES])
            for bi, c in blocks}
    rep = {(bi, c): _replicate(jnp.where(is_decay, gcum[bi, c], gsc_s[bi, c * chunk:(c + 1) * chunk, 0:LANES]),
                               sel, n_gates) for bi, c in blocks}

    def local_chain(bi, c, h):
        rows = slice(c * chunk, (c + 1) * chunk)
        q = qkv_s[bi, rows, h * A_DK:(h + 1) * A_DK]
        k = qkv_s[bi, rows, A_KEY + h * A_DK:A_KEY + (h + 1) * A_DK]
        v = qkv_s[bi, rows, 2 * A_KEY + h * A_DV:2 * A_KEY + (h + 1) * A_DV]
        qk_k = _dot_nt(jnp.concatenate([q, k], axis=0), k)
        yield
        b_rep = rep[bi, c][:, h * LANES:(h + 1) * LANES]
        g_rep = rep[bi, c][:, (A_HEADS + h) * LANES:(A_HEADS + h + 1) * LANES]
        g_row = _col_to_row(g_rep[:, 0:chunk])
        g_last = g_rep[chunk - 1:chunk, :]
        decay = jnp.where(tri, jnp.exp(jnp.where(tri, g_rep[:, 0:chunk] - g_row, 0.0)), 0.0)
        eg = jnp.exp(g_rep)
        qg_s[bi, rows, h * A_DK:(h + 1) * A_DK] = q * eg
        kd_s[bi, rows, h * A_DK:(h + 1) * A_DK] = k * jnp.exp(g_last - g_rep)
        ge_s[bi, c, h:h + 1, :] = jnp.exp(g_last)
        attn_s[bi, rows, h * LANES:h * LANES + chunk] = decay * qk_k[0:chunk, :]
        amat = jnp.where(strict, b_rep[:, 0:chunk] * decay * qk_k[chunk:2 * chunk, :], 0.0)
        t_inv = yield from _inv_unit_lower(-amat, chunk)
        uw_s[bi, rows, 2 * h * LANES:2 * (h + 1) * LANES] = _dot(
            t_inv, jnp.concatenate([b_rep * v, (b_rep * eg) * k], axis=1))

    _interleave(local_chain(bi, c, h) for c in range(n_chunks) for bi in range(nb) for h in range(A_HEADS))

    def body(c):
        r0 = _row_start(c, chunk)
        rows = pl.ds(r0, chunk)

        def state_chain(bi, h):
            uw = uw_s[bi, rows, 2 * h * LANES:2 * (h + 1) * LANES]
            s = st_s[bi, h]
            wq_s = _dot(jnp.concatenate([uw[:, A_DV:A_DV + A_DK], qg_s[bi, rows, h * A_DK:(h + 1) * A_DK]], axis=0), s)
            yield
            nv = uw[:, 0:A_DV] - wq_s[0:chunk, :]
            o = wq_s[chunk:2 * chunk, :] + _dot(attn_s[bi, rows, h * LANES:h * LANES + chunk], nv)
            st_s[bi, h] = ge_s[bi, c, h:h + 1, :] * s + _dot_tn(kd_s[bi, rows, h * A_DK:(h + 1) * A_DK], nv)
            yield
            z = main_ref[bi, rows, A_QKV + h * A_DV:A_QKV + (h + 1) * A_DV]
            o_ref[bi, rows, h * A_DV:(h + 1) * A_DV] = _rms(o, gnw) * _silu(z)

        _interleave(state_chain(bi, h) for bi in range(nb) for h in range(A_HEADS))

    _chunk_loop(n_chunks, body)

    @pl.when(i == pl.num_programs(1) - 1)
    def _():
        for bi in range(nb):
            convn_ref[bi] = ext_s[bi, tb + tail - (A_CONV - 1):tb + tail, :]
        sout_ref[...] = st_s[...]


def _gdn(proj3d, conv0, s0, conv_w, alog_row, dtb_row, gn_w, nb, tb, chunk):
    bsz, seq, _ = proj3d.shape
    blk = lambda b, i: (b, i, 0)
    per_b3 = lambda b, i: (b, 0, 0)
    per_b4 = lambda b, i: (b, 0, 0, 0)
    fix2 = lambda b, i: (0, 0)
    kern = functools.partial(_gdn_kernel, nb=nb, tb=tb, chunk=chunk)
    return pl.pallas_call(
        kern,
        grid=(bsz // nb, seq // tb),
        in_specs=[pl.BlockSpec((nb, tb, A_MAIN), blk),
                  pl.BlockSpec((nb, tb, LANES), lambda b, i: (b, i, P0_GATES // LANES)),
                  pl.BlockSpec((nb, A_CONV - 1, A_QKV), per_b3),
                  pl.BlockSpec((nb, A_HEADS, A_DK, A_DV), per_b4),
                  pl.BlockSpec((A_CONV, A_QKV), fix2),
                  pl.BlockSpec((1, LANES), fix2),
                  pl.BlockSpec((1, LANES), fix2),
                  pl.BlockSpec((1, A_DV), fix2)],
        out_specs=[pl.BlockSpec((nb, tb, A_VAL), blk),
                   pl.BlockSpec((nb, A_CONV - 1, A_QKV), per_b3),
                   pl.BlockSpec((nb, A_HEADS, A_DK, A_DV), per_b4)],
        out_shape=[jax.ShapeDtypeStruct((bsz, seq, A_VAL), F32),
                   jax.ShapeDtypeStruct((bsz, A_CONV - 1, A_QKV), F32),
                   jax.ShapeDtypeStruct((bsz, A_HEADS, A_DK, A_DV), F32)],
        scratch_shapes=[pltpu.VMEM((nb, tb + SUBLANES, A_QKV), F32),
                        pltpu.VMEM((nb, tb, A_QKV), F32),
                        pltpu.VMEM((nb, tb, 2 * LANES), F32),
                        pltpu.VMEM((nb, A_HEADS, A_DK, A_DV), F32),
                        pltpu.VMEM((nb, tb, A_HEADS * (A_DV + A_DK)), F32),
                        pltpu.VMEM((nb, tb, A_HEADS * LANES), F32),
                        pltpu.VMEM((nb, tb, A_KEY), F32),
                        pltpu.VMEM((nb, tb, A_KEY), F32),
                        pltpu.VMEM((nb, tb // chunk, SUBLANES, LANES), F32)],
        compiler_params=pltpu.CompilerParams(dimension_semantics=("parallel", "arbitrary"),
                                             vmem_limit_bytes=VMEM_LIMIT),
        name="gdn_mixer",
    )(proj3d, proj3d, conv0, s0, conv_w, alog_row, dtb_row, gn_w)


_FR, _FLW, _FK, _FV, _FA, _FB, _FG = range(7)


def _rwkv_kernel(pb_ref, prev_ref, s0_ref, mu_ref, w0_ref, a0_ref, w2a2_ref, kk_ref, ka_ref,
                 rk_ref, lnw_ref, lnb_ref, o_ref, sout_ref, ext_s, feat_s, st_s, tmp_s,
                 *, nb, tb, chunk):
    i = pl.program_id(1)
    n_chunks = tb // chunk
    tail = SUBLANES
    lane = _iota2((1, LANES), 1)
    head0 = lane < B_N
    ones_blk = (_iota2((LANES, LANES), 0) // B_N == _iota2((LANES, LANES), 1) // B_N)

    @pl.when(i == 0)
    def _():
        for bi in range(nb):
            ext_s[bi, 0:tail, :] = jnp.zeros((tail, B_COLS), F32)
            ext_s[bi, tail - 1:tail, :] = prev_ref[bi]
            for p in range(B_PAIRS):
                tmp_s[...] = jnp.zeros((B_N, LANES), F32)
                tmp_s[:, 0:B_N] = s0_ref[bi, 2 * p + 1]
                shifted = pltpu.roll(tmp_s[...], B_N, 1)
                st_s[bi, p, B_N:LANES, :] = jnp.where(head0, 0.0, shifted)
                tmp_s[:, 0:B_N] = s0_ref[bi, 2 * p]
                st_s[bi, p, 0:B_N, :] = tmp_s[...]

    @pl.when(i > 0)
    def _():
        for bi in range(nb):
            ext_s[bi, 0:tail, :] = ext_s[bi, tb:tb + tail, :]

    mu = mu_ref[...]
    for bi in range(nb):
        ext_s[bi, tail:tail + tb, :] = pb_ref[bi]
        for c in range(n_chunks):
            r0 = c * chunk
            pb = ext_s[bi, tail + r0:tail + r0 + chunk, :]
            pb_prev = ext_s[bi, tail + r0 - 1:tail + r0 - 1 + chunk, :]
            pbs = pb + (pb_prev - pb) * mu
            r = pbs[:, 0:B_W]
            kb = pbs[:, B_W:2 * B_W]
            vb = pbs[:, 2 * B_W:3 * B_W]
            gb = pbs[:, 3 * B_W:4 * B_W]
            lo = pbs[:, 4 * B_W:B_COLS]
            lo = jnp.where(head0, jnp.tanh(lo), lo)
            wa = _dot(lo, w2a2_ref[...])
            logw = -B_DECAY_SCALE * _sigmoid(w0_ref[...] + wa[:, 0:B_W])
            a = _sigmoid(a0_ref[...] + wa[:, B_W:2 * B_W])
            kkx = kb * kk_ref[...]
            rws = slice(r0, r0 + chunk)
            for p in range(B_PAIRS):
                sl = slice(p * LANES, (p + 1) * LANES)
                kp = kkx[:, sl]
                ss = _dot_exact_rhs(kp * kp, ones_blk)
                kkn = kp * lax.rsqrt(ss + EPS)
                feat_s[bi, rws, _FA * B_W + p * LANES:_FA * B_W + (p + 1) * LANES] = -kkn
                feat_s[bi, rws, _FB * B_W + p * LANES:_FB * B_W + (p + 1) * LANES] = kkn * a[:, sl]
            feat_s[bi, rws, _FR * B_W:(_FR + 1) * B_W] = r
            feat_s[bi, rws, _FLW * B_W:(_FLW + 1) * B_W] = logw
            feat_s[bi, rws, _FK * B_W:(_FK + 1) * B_W] = kb * (1.0 + (a - 1.0) * ka_ref[...])
            feat_s[bi, rws, _FV * B_W:(_FV + 1) * B_W] = vb
            feat_s[bi, rws, _FG * B_W:(_FG + 1) * B_W] = gb

    tri = _tri_incl(chunk)
    c2 = 2 * chunk
    ri = _iota2((c2, c2), 0)
    ci = _iota2((c2, c2), 1)
    same_head = (ri // chunk) == (ci // chunk)
    incl_bd = same_head & (ri >= ci)
    strict_bd = same_head & (ri > ci)
    ri4 = _iota2((2 * c2, c2), 0)
    ci4 = _iota2((2 * c2, c2), 1)
    rr4 = ri4 % c2
    lm_mask = ((rr4 // chunk) == (ci4 // chunk)) & ((rr4 > ci4) | ((ri4 >= c2) & (rr4 == ci4)))
    lane_st = (_iota2((c2, LANES), 0) // chunk) == (_iota2((c2, LANES), 1) // B_N)

    def body(c):
        r0 = _row_start(c, chunk)
        rows = pl.ds(r0, chunk)

        def pair_chain(bi, p):
            def feat(slot):
                return feat_s[bi, rows, slot * B_W + p * LANES:slot * B_W + (p + 1) * LANES]
            r, lw, k, v, a_, b_, gate = (feat(s) for s in (_FR, _FLW, _FK, _FV, _FA, _FB, _FG))
            cs = _dot_exact_lhs(tri, lw)
            yield
            cs_last = cs[chunk - 1:chunk, :]
            e_inv = jnp.exp(-cs)
            e_rem = jnp.exp(cs_last - cs)
            at = a_ * jnp.exp(cs - lw)
            rt = r * jnp.exp(cs)
            bt = b_ * e_inv
            kt = k * e_inv
            a_st = jnp.concatenate([jnp.where(head0, at, 0.0), jnp.where(head0, 0.0, at)], axis=0)
            r_st = jnp.concatenate([jnp.where(head0, rt, 0.0), jnp.where(head0, 0.0, rt)], axis=0)
            b2 = jnp.concatenate([bt, bt], axis=0)
            k2 = jnp.concatenate([kt, kt], axis=0)
            v2 = jnp.concatenate([v, v], axis=0)
            ar_st = jnp.concatenate([a_st, r_st], axis=0)
            if c2 % LANES == 0:
                ar_bk = _dot_nt(ar_st, jnp.concatenate([b2, k2], axis=0))
                ar_b, ar_k = ar_bk[:, 0:c2], ar_bk[:, c2:2 * c2]
            else:
                ar_b = _dot_nt(ar_st, b2)
                ar_k = _dot_nt(ar_st, k2)
            s = st_s[bi, p]
            ar = _dot_nt(jnp.concatenate([at, rt], axis=0), s)
            yield
            l_ab = jnp.where(strict_bd, ar_b[0:c2, :], 0.0)
            m_rb = jnp.where(incl_bd, ar_b[c2:2 * c2, :], 0.0)
            lm_k = jnp.where(lm_mask, ar_k, 0.0)
            lm_v = _dot(lm_k, v2)
            t_inv = yield from _inv_unit_lower(l_ab, chunk)
            ah = ar[0:chunk, :]
            rh = ar[chunk:c2, :]
            x_st = jnp.where(lane_st, jnp.concatenate([ah, ah], axis=0) + lm_v[0:c2, :], 0.0)
            u_st = _dot(t_inv, x_st)
            yield
            o_st = jnp.where(lane_st, _dot(m_rb, u_st) + lm_v[c2:2 * c2, :], 0.0)
            u = u_st[0:chunk, :] + u_st[chunk:c2, :]
            upd = _dot_tn(jnp.concatenate([u, v], axis=0),
                          jnp.concatenate([b_ * e_rem, k * e_rem], axis=0))
            yield
            o = rh + o_st[0:chunk, :] + o_st[chunk:c2, :]
            st_s[bi, p] = s * jnp.exp(cs_last) + jnp.where(ones_blk, upd, 0.0)
            sl = slice(p * LANES, (p + 1) * LANES)
            mean = _dot_exact_rhs(o, ones_blk) * (1.0 / B_N)
            bonus = _dot_exact_rhs(r * k * rk_ref[:, sl], ones_blk) * v
            yield
            d = o - mean
            var = _dot_exact_rhs(d * d, ones_blk) * (1.0 / B_N)
            yield
            ob = d * lax.rsqrt(var + B_GN_EPS) * lnw_ref[:, sl] + lnb_ref[:, sl] + bonus
            o_ref[bi, rows, sl] = ob * _silu(gate)

        _interleave(pair_chain(bi, p) for bi in range(nb) for p in range(B_PAIRS))

    _chunk_loop(n_chunks, body)

    @pl.when(i == pl.num_programs(1) - 1)
    def _():
        for bi in range(nb):
            for p in range(B_PAIRS):
                sout_ref[bi, 2 * p] = st_s[bi, p, 0:B_N, 0:B_N]
                shifted = pltpu.roll(st_s[bi, p, B_N:LANES, :], B_N, 1)
                sout_ref[bi, 2 * p + 1] = shifted[:, 0:B_N]


def _rwkv(proj3d, prev_row, s0, mu, w0, a0, w2a2, k_k, k_a, r_k, ln_w, ln_b, nb, tb, chunk):
    bsz, seq, _ = proj3d.shape
    blk = lambda b, i: (b, i, 0)
    per_b3 = lambda b, i: (b, 0, 0)
    per_b4 = lambda b, i: (b, 0, 0, 0)
    fix2 = lambda b, i: (0, 0)
    kern = functools.partial(_rwkv_kernel, nb=nb, tb=tb, chunk=chunk)
    vec = pl.BlockSpec((1, B_W), fix2)
    return pl.pallas_call(
        kern,
        grid=(bsz // nb, seq // tb),
        in_specs=[pl.BlockSpec((nb, tb, B_COLS), lambda b, i: (b, i, P0_B // B_COLS)),
                  pl.BlockSpec((nb, 1, B_COLS), per_b3),
                  pl.BlockSpec((nb, B_HEADS, B_N, B_N), per_b4),
                  pl.BlockSpec((1, B_COLS), fix2),
                  vec, vec,
                  pl.BlockSpec((2 * B_LORA, 2 * B_W), fix2),
                  vec, vec, vec, vec, vec],
        out_specs=[pl.BlockSpec((nb, tb, B_W), blk),
                   pl.BlockSpec((nb, B_HEADS, B_N, B_N), per_b4)],
        out_shape=[jax.ShapeDtypeStruct((bsz, seq, B_W), F32),
                   jax.ShapeDtypeStruct((bsz, B_HEADS, B_N, B_N), F32)],
        scratch_shapes=[pltpu.VMEM((nb, tb + SUBLANES, B_COLS), F32),
                        pltpu.VMEM((nb, tb, 7 * B_W), F32),
                        pltpu.VMEM((nb, B_PAIRS, LANES, LANES), F32),
                        pltpu.VMEM((B_N, LANES), F32)],
        compiler_params=pltpu.CompilerParams(dimension_semantics=("parallel", "arbitrary"),
                                             vmem_limit_bytes=VMEM_LIMIT),
        name="rwkv7_mixer",
    )(proj3d, prev_row, s0, mu, w0, a0, w2a2, k_k, k_a, r_k, ln_w, ln_b)


def _run_tasks(tasks, caps):
    done, live, pending = set(), [], list(tasks)
    while pending or live:
        count = {}
        for _, group, _ in live:
            count[group] = count.get(group, 0) + 1
        still = []
        for key, group, deps, make in pending:
            if all(d in done for d in deps) and count.get(group, 0) < caps.get(group, len(tasks)):
                live.append((key, group, make()))
                count[group] = count.get(group, 0) + 1
            else:
                still.append((key, group, deps, make))
        assert live, "task graph has a cycle or a missing dependency"
        pending = still
        nxt = []
        for key, group, gen in live:
            try:
                next(gen)
                nxt.append((key, group, gen))
            except StopIteration:
                done.add(key)
        live = nxt


def _layer0_kernel(main_ref, gates_ref, pb_ref, conv0_ref, sg0_ref, prev_ref, sr0_ref,
                   convw_ref, alog_ref, dtb_ref, gnw_ref, mu_ref, w0_ref, a0_ref, w2a2_ref, kk_ref, ka_ref,
                   rk_ref, lnw_ref, lnb_ref,
                   o_ref, convn_ref, sgout_ref, srout_ref,
                   ext_a, qkv_s, gsc_s, stg_s, uw_s, attn_s, qg_s, kd_s, ge_s,
                   ext_b, feat_s, str_s, tmp_s, *, nb, tb, chunk):
    i = pl.program_id(1)
    n_chunks = tb // chunk
    tail = SUBLANES
    lane = _iota2((1, LANES), 1)
    head0 = lane < B_N
    ones_blk = (_iota2((LANES, LANES), 0) // B_N == _iota2((LANES, LANES), 1) // B_N)

    @pl.when(i == 0)
    def _():
        for bi in range(nb):
            ext_a[bi, 0:tail, :] = jnp.zeros((tail, A_QKV), F32)
            ext_a[bi, tail - (A_CONV - 1):tail, :] = conv0_ref[bi]
            ext_b[bi, 0:tail, :] = jnp.zeros((tail, B_COLS), F32)
            ext_b[bi, tail - 1:tail, :] = prev_ref[bi]
            for p in range(B_PAIRS):
                tmp_s[...] = jnp.zeros((B_N, LANES), F32)
                tmp_s[:, 0:B_N] = sr0_ref[bi, 2 * p + 1]
                shifted = pltpu.roll(tmp_s[...], B_N, 1)
                str_s[bi, p, B_N:LANES, :] = jnp.where(head0, 0.0, shifted)
                tmp_s[:, 0:B_N] = sr0_ref[bi, 2 * p]
                str_s[bi, p, 0:B_N, :] = tmp_s[...]
        stg_s[...] = sg0_ref[...]

    @pl.when(i > 0)
    def _():
        for bi in range(nb):
            ext_a[bi, 0:tail, :] = ext_a[bi, tb:tb + tail, :]
            ext_b[bi, 0:tail, :] = ext_b[bi, tb:tb + tail, :]

    for bi in range(nb):
        ext_a[bi, tail:tail + tb, :] = main_ref[bi, :, 0:A_QKV]
        ext_b[bi, tail:tail + tb, :] = pb_ref[bi]
        g = gates_ref[bi]
        gsc_s[bi, :, 0:LANES] = _sigmoid(g)
        gsc_s[bi, :, LANES:2 * LANES] = -jnp.exp(alog_ref[...]) * _softplus(g + dtb_ref[...])

    cw = convw_ref[...]
    n_gates = 2 * A_HEADS
    is_decay = jnp.bitwise_and(lane, n_gates - 1) >= A_HEADS
    sel = _replicate_selector(n_gates)
    tri = _tri_incl(chunk)
    strict = _iota2((chunk, chunk), 0) > _iota2((chunk, chunk), 1)
    gnw = gnw_ref[...]
    rep = {}

    def gdn_pre(bi, c):
        r0 = c * chunk
        y = ext_a[bi, tail + r0:tail + r0 + chunk, :] * cw[A_CONV - 1:A_CONV, :]
        for j in range(A_CONV - 1):
            sh = A_CONV - 1 - j
            y = y + ext_a[bi, tail + r0 - sh:tail + r0 - sh + chunk, :] * cw[j:j + 1, :]
        yield
        y = _silu(y)
        for h in range(A_HEADS):
            q = y[:, h * A_DK:(h + 1) * A_DK]
            qkv_s[bi, r0:r0 + chunk, h * A_DK:(h + 1) * A_DK] = (
                q * lax.rsqrt(jnp.sum(q * q, axis=-1, keepdims=True) + EPS) * (A_DK ** -0.5))
            k = y[:, A_KEY + h * A_DK:A_KEY + (h + 1) * A_DK]
            qkv_s[bi, r0:r0 + chunk, A_KEY + h * A_DK:A_KEY + (h + 1) * A_DK] = (
                k * lax.rsqrt(jnp.sum(k * k, axis=-1, keepdims=True) + EPS))
        qkv_s[bi, r0:r0 + chunk, 2 * A_KEY:A_QKV] = y[:, 2 * A_KEY:A_QKV]

    def gdn_gates(bi, c):
        rows = slice(c * chunk, (c + 1) * chunk)
        gcum = _dot_exact_lhs(tri, gsc_s[bi, rows, LANES:2 * LANES])
        yield
        rep[bi, c] = _replicate(jnp.where(is_decay, gcum, gsc_s[bi, rows, 0:LANES]), sel, n_gates)

    def gdn_local(bi, c, h):
        rows = slice(c * chunk, (c + 1) * chunk)
        q = qkv_s[bi, rows, h * A_DK:(h + 1) * A_DK]
        k = qkv_s[bi, rows, A_KEY + h * A_DK:A_KEY + (h + 1) * A_DK]
        v = qkv_s[bi, rows, 2 * A_KEY + h * A_DV:2 * A_KEY + (h + 1) * A_DV]
        qk_k = _dot_nt(jnp.concatenate([q, k], axis=0), k)
        yield
        b_rep = rep[bi, c][:, h * LANES:(h + 1) * LANES]
        g_rep = rep[bi, c][:, (A_HEADS + h) * LANES:(A_HEADS + h + 1) * LANES]
        g_row = _col_to_row(g_rep[:, 0:chunk])
        g_last = g_rep[chunk - 1:chunk, :]
        decay = jnp.where(tri, jnp.exp(jnp.where(tri, g_rep[:, 0:chunk] - g_row, 0.0)), 0.0)
        eg = jnp.exp(g_rep)
        qg_s[bi, rows, h * A_DK:(h + 1) * A_DK] = q * eg
        kd_s[bi, rows, h * A_DK:(h + 1) * A_DK] = k * jnp.exp(g_last - g_rep)
        ge_s[bi, c, h:h + 1, :] = jnp.exp(g_last)
        attn_s[bi, rows, h * LANES:h * LANES + chunk] = decay * qk_k[0:chunk, :]
        amat = jnp.where(strict, b_rep[:, 0:chunk] * decay * qk_k[chunk:2 * chunk, :], 0.0)
        t_inv = yield from _inv_unit_lower(-amat, chunk)
        uw_s[bi, rows, 2 * h * LANES:2 * (h + 1) * LANES] = _dot(
            t_inv, jnp.concatenate([b_rep * v, (b_rep * eg) * k], axis=1))

    def gdn_state(bi, c, h):
        rows = slice(c * chunk, (c + 1) * chunk)
        uw = uw_s[bi, rows, 2 * h * LANES:2 * (h + 1) * LANES]
        s = stg_s[bi, h]
        wq_s = _dot(jnp.concatenate([uw[:, A_DV:A_DV + A_DK], qg_s[bi, rows, h * A_DK:(h + 1) * A_DK]], axis=0), s)
        yield
        nv = uw[:, 0:A_DV] - wq_s[0:chunk, :]
        o = wq_s[chunk:2 * chunk, :] + _dot(attn_s[bi, rows, h * LANES:h * LANES + chunk], nv)
        stg_s[bi, h] = ge_s[bi, c, h:h + 1, :] * s + _dot_tn(kd_s[bi, rows, h * A_DK:(h + 1) * A_DK], nv)
        yield
        z = main_ref[bi, rows, A_QKV + h * A_DV:A_QKV + (h + 1) * A_DV]
        o_ref[bi, rows, h * A_DV:(h + 1) * A_DV] = _rms(o, gnw) * _silu(z)

    mu = mu_ref[...]
    c2 = 2 * chunk
    ri = _iota2((c2, c2), 0)
    ci = _iota2((c2, c2), 1)
    same_head = (ri // chunk) == (ci // chunk)
    incl_bd = same_head & (ri >= ci)
    strict_bd = same_head & (ri > ci)
    ri4 = _iota2((2 * c2, c2), 0)
    ci4 = _iota2((2 * c2, c2), 1)
    rr4 = ri4 % c2
    lm_mask = ((rr4 // chunk) == (ci4 // chunk)) & ((rr4 > ci4) | ((ri4 >= c2) & (rr4 == ci4)))
    lane_st = (_iota2((c2, LANES), 0) // chunk) == (_iota2((c2, LANES), 1) // B_N)

    def rwkv_pre(bi, c):
        r0 = c * chunk
        rws = slice(r0, r0 + chunk)
        pb = ext_b[bi, tail + r0:tail + r0 + chunk, :]
        pb_prev = ext_b[bi, tail + r0 - 1:tail + r0 - 1 + chunk, :]
        pbs = pb + (pb_prev - pb) * mu
        lo = pbs[:, 4 * B_W:B_COLS]
        lo = jnp.where(head0, jnp.tanh(lo), lo)
        wa = _dot(lo, w2a2_ref[...])
        kb = pbs[:, B_W:2 * B_W]
        kkx = kb * kk_ref[...]
        ss = [_dot_exact_rhs(kkx[:, p * LANES:(p + 1) * LANES] ** 2, ones_blk) for p in range(B_PAIRS)]
        feat_s[bi, rws, _FR * B_W:(_FR + 1) * B_W] = pbs[:, 0:B_W]
        feat_s[bi, rws, _FV * B_W:(_FV + 1) * B_W] = pbs[:, 2 * B_W:3 * B_W]
        feat_s[bi, rws, _FG * B_W:(_FG + 1) * B_W] = pbs[:, 3 * B_W:4 * B_W]
        yield
        logw = -B_DECAY_SCALE * _sigmoid(w0_ref[...] + wa[:, 0:B_W])
        a = _sigmoid(a0_ref[...] + wa[:, B_W:2 * B_W])
        for p in range(B_PAIRS):
            sl = slice(p * LANES, (p + 1) * LANES)
            kkn = kkx[:, sl] * lax.rsqrt(ss[p] + EPS)
            feat_s[bi, rws, _FA * B_W + p * LANES:_FA * B_W + (p + 1) * LANES] = -kkn
            feat_s[bi, rws, _FB * B_W + p * LANES:_FB * B_W + (p + 1) * LANES] = kkn * a[:, sl]
        feat_s[bi, rws, _FLW * B_W:(_FLW + 1) * B_W] = logw
        feat_s[bi, rws, _FK * B_W:(_FK + 1) * B_W] = kb * (1.0 + (a - 1.0) * ka_ref[...])

    local = {}

    def rwkv_local(bi, c, p):
        rows = slice(c * chunk, (c + 1) * chunk)

        def feat(slot):
            return feat_s[bi, rows, slot * B_W + p * LANES:slot * B_W + (p + 1) * LANES]
        r, lw, k, v, a_, b_ = (feat(s) for s in (_FR, _FLW, _FK, _FV, _FA, _FB))
        cs = _dot_exact_lhs(tri, lw)
        yield
        cs_last = cs[chunk - 1:chunk, :]
        e_inv = jnp.exp(-cs)
        e_rem = jnp.exp(cs_last - cs)
        at = a_ * jnp.exp(cs - lw)
        rt = r * jnp.exp(cs)
        bt = b_ * e_inv
        kt = k * e_inv
        a_st = jnp.concatenate([jnp.where(head0, at, 0.0), jnp.where(head0, 0.0, at)], axis=0)
        r_st = jnp.concatenate([jnp.where(head0, rt, 0.0), jnp.where(head0, 0.0, rt)], axis=0)
        b2 = jnp.concatenate([bt, bt], axis=0)
        k2 = jnp.concatenate([kt, kt], axis=0)
        v2 = jnp.concatenate([v, v], axis=0)
        ar_st = jnp.concatenate([a_st, r_st], axis=0)
        if c2 % LANES == 0:
            ar_bk = _dot_nt(ar_st, jnp.concatenate([b2, k2], axis=0))
            ar_b, ar_k = ar_bk[:, 0:c2], ar_bk[:, c2:2 * c2]
        else:
            ar_b = _dot_nt(ar_st, b2)
            ar_k = _dot_nt(ar_st, k2)
        yield
        l_ab = jnp.where(strict_bd, ar_b[0:c2, :], 0.0)
        m_rb = jnp.where(incl_bd, ar_b[c2:2 * c2, :], 0.0)
        lm_k = jnp.where(lm_mask, ar_k, 0.0)
        lm_v = _dot(lm_k, v2)
        sl = slice(p * LANES, (p + 1) * LANES)
        bonus = _dot_exact_rhs(r * k * rk_ref[:, sl], ones_blk) * v
        t_inv = yield from _inv_unit_lower(l_ab, chunk)
        local[bi, c, p] = (t_inv, lm_v, m_rb, jnp.concatenate([at, rt], axis=0),
                           jnp.concatenate([b_ * e_rem, k * e_rem], axis=0), jnp.exp(cs_last), bonus)

    def rwkv_state(bi, c, p):
        rows = slice(c * chunk, (c + 1) * chunk)
        sl = slice(p * LANES, (p + 1) * LANES)
        t_inv, lm_v, m_rb, at_rt, bk_rem, decay_end, bonus = local.pop((bi, c, p))
        v = feat_s[bi, rows, _FV * B_W + p * LANES:_FV * B_W + (p + 1) * LANES]
        gate = feat_s[bi, rows, _FG * B_W + p * LANES:_FG * B_W + (p + 1) * LANES]
        s = str_s[bi, p]
        ar = _dot_nt(at_rt, s)
        yield
        ah = ar[0:chunk, :]
        rh = ar[chunk:c2, :]
        x_st = jnp.where(lane_st, jnp.concatenate([ah, ah], axis=0) + lm_v[0:c2, :], 0.0)
        u_st = _dot(t_inv, x_st)
        yield
        o_st = jnp.where(lane_st, _dot(m_rb, u_st) + lm_v[c2:2 * c2, :], 0.0)
        u = u_st[0:chunk, :] + u_st[chunk:c2, :]
        upd = _dot_tn(jnp.concatenate([u, v], axis=0), bk_rem)
        yield
        o = rh + o_st[0:chunk, :] + o_st[chunk:c2, :]
        str_s[bi, p] = s * decay_end + jnp.where(ones_blk, upd, 0.0)
        mean = _dot_exact_rhs(o, ones_blk) * (1.0 / B_N)
        yield
        d = o - mean
        var = _dot_exact_rhs(d * d, ones_blk) * (1.0 / B_N)
        yield
        ob = d * lax.rsqrt(var + B_GN_EPS) * lnw_ref[:, sl] + lnb_ref[:, sl] + bonus
        o_ref[bi, rows, A_VAL + p * LANES:A_VAL + (p + 1) * LANES] = ob * _silu(gate)

    tasks = []
    blocks = [(bi, c) for c in range(n_chunks) for bi in range(nb)]
    for bi, c in blocks:
        tasks.append((("rp", bi, c), "rwkv_pre", [], functools.partial(rwkv_pre, bi, c)))
    for bi, c in blocks:
        for p in range(B_PAIRS):
            deps = [("rl", bi, c, p)] + ([("rs", bi, c - 1, p)] if c else [])
            tasks.append((("rs", bi, c, p), "rwkv_state", deps, functools.partial(rwkv_state, bi, c, p)))
        for h in range(A_HEADS):
            deps = [("gl", bi, c, h)] + ([("gs", bi, c - 1, h)] if c else [])
            tasks.append((("gs", bi, c, h), "gdn_state", deps, functools.partial(gdn_state, bi, c, h)))
    for bi, c in blocks:
        for p in range(B_PAIRS):
            tasks.append((("rl", bi, c, p), "rwkv_local", [("rp", bi, c)], functools.partial(rwkv_local, bi, c, p)))
        for h in range(A_HEADS):
            tasks.append((("gl", bi, c, h), "gdn_local", [("gp", bi, c), ("gg", bi, c)],
                          functools.partial(gdn_local, bi, c, h)))
    for bi, c in blocks:
        tasks.append((("gg", bi, c), "gdn_gates", [], functools.partial(gdn_gates, bi, c)))
        tasks.append((("gp", bi, c), "gdn_pre", [], functools.partial(gdn_pre, bi, c)))
    _run_tasks(tasks, {"rwkv_pre": 2, "gdn_pre": 2, "gdn_gates": 2,
                       "rwkv_local": 2 * B_PAIRS, "gdn_local": 2 * A_HEADS})

    @pl.when(i == pl.num_programs(1) - 1)
    def _():
        for bi in range(nb):
            convn_ref[bi] = ext_a[bi, tb + tail - (A_CONV - 1):tb + tail, :]
            for p in range(B_PAIRS):
                srout_ref[bi, 2 * p] = str_s[bi, p, 0:B_N, 0:B_N]
                shifted = pltpu.roll(str_s[bi, p, B_N:LANES, :], B_N, 1)
                srout_ref[bi, 2 * p + 1] = shifted[:, 0:B_N]
        sgout_ref[...] = stg_s[...]


def _layer0_mixers(proj3d, conv0, s_gdn, prev_row, s_rwkv, q, nb, tb, chunk):
    bsz, seq, _ = proj3d.shape
    blk = lambda b, i: (b, i, 0)
    per_b3 = lambda b, i: (b, 0, 0)
    per_b4 = lambda b, i: (b, 0, 0, 0)
    fix2 = lambda b, i: (0, 0)
    kern = functools.partial(_layer0_kernel, nb=nb, tb=tb, chunk=chunk)
    lanes = pl.BlockSpec((1, LANES), fix2)
    vec = pl.BlockSpec((1, B_W), fix2)
    return pl.pallas_call(
        kern,
        grid=(bsz // nb, seq // tb),
        in_specs=[pl.BlockSpec((nb, tb, A_MAIN), blk),
                  pl.BlockSpec((nb, tb, LANES), lambda b, i: (b, i, P0_GATES // LANES)),
                  pl.BlockSpec((nb, tb, B_COLS), lambda b, i: (b, i, P0_B // B_COLS)),
                  pl.BlockSpec((nb, A_CONV - 1, A_QKV), per_b3),
                  pl.BlockSpec((nb, A_HEADS, A_DK, A_DV), per_b4),
                  pl.BlockSpec((nb, 1, B_COLS), per_b3),
                  pl.BlockSpec((nb, B_HEADS, B_N, B_N), per_b4),
                  pl.BlockSpec((A_CONV, A_QKV), fix2), lanes, lanes, pl.BlockSpec((1, A_DV), fix2),
                  pl.BlockSpec((1, B_COLS), fix2), vec, vec,
                  pl.BlockSpec((2 * B_LORA, 2 * B_W), fix2),
                  vec, vec, vec, vec, vec],
        out_specs=[pl.BlockSpec((nb, tb, A_VAL + B_W), blk),
                   pl.BlockSpec((nb, A_CONV - 1, A_QKV), per_b3),
                   pl.BlockSpec((nb, A_HEADS, A_DK, A_DV), per_b4),
                   pl.BlockSpec((nb, B_HEADS, B_N, B_N), per_b4)],
        out_shape=[jax.ShapeDtypeStruct((bsz, seq, A_VAL + B_W), F32),
                   jax.ShapeDtypeStruct((bsz, A_CONV - 1, A_QKV), F32),
                   jax.ShapeDtypeStruct((bsz, A_HEADS, A_DK, A_DV), F32),
                   jax.ShapeDtypeStruct((bsz, B_HEADS, B_N, B_N), F32)],
        scratch_shapes=[pltpu.VMEM((nb, tb + SUBLANES, A_QKV), F32),
                        pltpu.VMEM((nb, tb, A_QKV), F32),
                        pltpu.VMEM((nb, tb, 2 * LANES), F32),
                        pltpu.VMEM((nb, A_HEADS, A_DK, A_DV), F32),
                        pltpu.VMEM((nb, tb, A_HEADS * (A_DV + A_DK)), F32),
                        pltpu.VMEM((nb, tb, A_HEADS * LANES), F32),
                        pltpu.VMEM((nb, tb, A_KEY), F32),
                        pltpu.VMEM((nb, tb, A_KEY), F32),
                        pltpu.VMEM((nb, tb // chunk, SUBLANES, LANES), F32),
                        pltpu.VMEM((nb, tb + SUBLANES, B_COLS), F32),
                        pltpu.VMEM((nb, tb, 7 * B_W), F32),
                        pltpu.VMEM((nb, B_PAIRS, LANES, LANES), F32),
                        pltpu.VMEM((B_N, LANES), F32)],
        compiler_params=pltpu.CompilerParams(dimension_semantics=("parallel", "arbitrary"),
                                             vmem_limit_bytes=VMEM_LIMIT),
        name="layer0_mixers",
    )(proj3d, proj3d, proj3d, conv0, s_gdn, prev_row, s_rwkv,
      q["conv_w"], q["alog"], q["dtb"], q["gnw"], q["mu"], q["rw0"], q["ra0"], q["w2a2"],
      q["k_k"], q["k_a"], q["r_k"], q["ln_w"], q["ln_b"])


def _mlstm_kernel(p_ref, c0_ref, n0_ref, m0_ref, bias_ref, o_ref, cout_ref, nout_ref, mout_ref,
                  gsc_s, caug_s, m_s, *, nb, tb, chunk):
    i = pl.program_id(1)
    n_chunks = tb // chunk
    lane = _iota2((1, LANES), 1)
    krow = _iota2((LANES, 1), 0)
    n_gates = 2 * C_HEADS
    is_f = jnp.bitwise_and(lane, n_gates - 1) >= C_HEADS

    @pl.when(i == 0)
    def _():
        for bi in range(nb):
            m0 = m0_ref[bi]
            for h in range(C_HEADS):
                j = h % 2
                caug_s[bi, h] = jnp.zeros((LANES, 2 * C_DV), F32)
                caug_s[bi, h, j * C_DQK:(j + 1) * C_DQK, 0:C_DV] = c0_ref[bi, h]
                n_col = _row_to_col(n0_ref[bi, h // 2:h // 2 + 1, :])
                own = (krow // C_DQK) == j
                caug_s[bi, h, :, C_DV:2 * C_DV] = jnp.where(own, jnp.broadcast_to(n_col, (LANES, C_DV)), 0.0)
                m_s[bi, h:h + 1, :] = jnp.broadcast_to(m0[:, h:h + 1], (1, LANES))

    for bi in range(nb):
        g = p_ref[bi, :, P1_GATES:P1_PAD] + bias_ref[...]
        cap = C_GATE_CAP * jnp.tanh(g / C_GATE_CAP)
        gsc_s[bi] = jnp.where(is_f, -_softplus(-cap), cap)

    tri = _tri_incl(chunk)
    sel = _replicate_selector(n_gates)
    ones_blk = jnp.ones((chunk, C_DV), F32)
    off = 2 * C_QK

    def body(c):
        r0 = _row_start(c, chunk)
        rows = pl.ds(r0, chunk)
        gl = [gsc_s[bi, rows, :] for bi in range(nb)]
        bcum = [_dot_exact_lhs(tri, g) for g in gl]
        rep = [_replicate(jnp.where(is_f, b, g), sel, n_gates) for g, b in zip(gl, bcum)]

        def head_chain(bi, h):
            p, j = h // 2, h % 2
            own_lane = (lane // C_DQK) == j
            own_row = (krow // C_DQK) == j
            qp = p_ref[bi, rows, p * LANES:(p + 1) * LANES] * (C_DQK ** -0.5)
            kp = p_ref[bi, rows, C_QK + p * LANES:C_QK + (p + 1) * LANES]
            v = p_ref[bi, rows, off + h * C_DV:off + (h + 1) * C_DV]
            vaug = jnp.concatenate([v, ones_blk], axis=1)
            qk = _dot_nt(jnp.where(own_lane, qp, 0.0), kp)
            cm = caug_s[bi, h]
            qc = _dot(qp, cm)
            i_rep = rep[bi][:, h * LANES:(h + 1) * LANES]
            b_rep = rep[bi][:, (C_HEADS + h) * LANES:(C_HEADS + h + 1) * LANES]
            b_row = _col_to_row(b_rep[:, 0:chunk])
            i_row = _col_to_row(i_rep[:, 0:chunk])
            dmat = jnp.where(tri, b_rep[:, 0:chunk] - b_row + i_row, -jnp.inf)
            dmax = b_rep + _cummax_rows(i_rep - b_rep)
            m = m_s[bi, h:h + 1, :]
            inter = b_rep + m
            mt = jnp.maximum(inter, dmax)
            m_new = mt[chunk - 1:chunk, :]
            b_last = b_rep[chunk - 1:chunk, :]
            w_s = jnp.exp(b_last - b_rep + i_rep - m_new)
            e_c = jnp.exp(b_last + m - m_new)
            e_inter = jnp.exp(inter - mt)
            upd = _dot_tn(w_s * kp, vaug)
            yield
            pm = jnp.exp(dmat - mt[:, 0:chunk]) * qk
            pv = _dot(pm, vaug)
            caug_s[bi, h] = jnp.concatenate([e_c, e_c], axis=1) * cm + jnp.where(own_row, upd, 0.0)
            m_s[bi, h:h + 1, :] = m_new
            yield
            num = e_inter * qc[:, 0:C_DV] + pv[:, 0:C_DV]
            den = e_inter * qc[:, C_DV:2 * C_DV] + pv[:, C_DV:2 * C_DV]
            hh = num / jnp.maximum(jnp.abs(den), jnp.exp(-mt))
            og = p_ref[bi, rows, off + C_V + h * C_DV:off + C_V + (h + 1) * C_DV]
            z = p_ref[bi, rows, off + 2 * C_V + h * C_DV:off + 2 * C_V + (h + 1) * C_DV]
            o_ref[bi, rows, h * C_DV:(h + 1) * C_DV] = hh * _sigmoid(og) * _silu(z)

        _interleave(head_chain(bi, h) for bi in range(nb) for h in range(C_HEADS))

    _chunk_loop(n_chunks, body)

    @pl.when(i == pl.num_programs(1) - 1)
    def _():
        for bi in range(nb):
            for h in range(C_HEADS):
                j = h % 2
                cout_ref[bi, h] = caug_s[bi, h, j * C_DQK:(j + 1) * C_DQK, 0:C_DV]
            for p in range(C_HEADS // 2):
                n_col = caug_s[bi, 2 * p, :, C_DV:C_DV + 1] + caug_s[bi, 2 * p + 1, :, C_DV:C_DV + 1]
                nout_ref[bi, p:p + 1, :] = _col_to_row(n_col)
        mout_ref[...] = m_s[...]


def _mlstm(proj3d, c0, n0, m0, bias_row, nb, tb, chunk):
    bsz, seq, _ = proj3d.shape
    blk = lambda b, i: (b, i, 0)
    per_b3 = lambda b, i: (b, 0, 0)
    per_b4 = lambda b, i: (b, 0, 0, 0)
    kern = functools.partial(_mlstm_kernel, nb=nb, tb=tb, chunk=chunk)
    pairs = C_HEADS // 2
    return pl.pallas_call(
        kern,
        grid=(bsz // nb, seq // tb),
        in_specs=[pl.BlockSpec((nb, tb, P1_PAD), blk),
                  pl.BlockSpec((nb, C_HEADS, C_DQK, C_DV), per_b4),
                  pl.BlockSpec((nb, pairs, LANES), per_b3),
                  pl.BlockSpec((nb, 1, C_HEADS), per_b3),
                  pl.BlockSpec((1, LANES), lambda b, i: (0, 0))],
        out_specs=[pl.BlockSpec((nb, tb, C_V), blk),
                   pl.BlockSpec((nb, C_HEADS, C_DQK, C_DV), per_b4),
                   pl.BlockSpec((nb, pairs, LANES), per_b3),
                   pl.BlockSpec((nb, C_HEADS, LANES), per_b3)],
        out_shape=[jax.ShapeDtypeStruct((bsz, seq, C_V), F32),
                   jax.ShapeDtypeStruct((bsz, C_HEADS, C_DQK, C_DV), F32),
                   jax.ShapeDtypeStruct((bsz, pairs, LANES), F32),
                   jax.ShapeDtypeStruct((bsz, C_HEADS, LANES), F32)],
        scratch_shapes=[pltpu.VMEM((nb, tb, LANES), F32),
                        pltpu.VMEM((nb, C_HEADS, LANES, 2 * C_DV), F32),
                        pltpu.VMEM((nb, C_HEADS, LANES), F32)],
        compiler_params=pltpu.CompilerParams(dimension_semantics=("parallel", "arbitrary"),
                                             vmem_limit_bytes=VMEM_LIMIT),
        name="mlstm_mixer",
    )(proj3d, c0, n0, m0, bias_row)


def _layer1_kernel(x_ref, mix_ref, c0_ref, n0_ref, m0_ref, wo0_ref, n1_ref, w1_ref, bias_ref, wo1_ref, nf_ref,
                   y_ref, cout_ref, nout_ref, mout_ref,
                   proj_s, x1_s, h_s, gsc_s, caug_s, m_s, *, nb, tb, chunk):
    i = pl.program_id(1)
    n_chunks = tb // chunk
    lane = _iota2((1, LANES), 1)
    krow = _iota2((LANES, 1), 0)
    n_gates = 2 * C_HEADS
    is_f = jnp.bitwise_and(lane, n_gates - 1) >= C_HEADS

    @pl.when(i == 0)
    def _():
        for bi in range(nb):
            m0 = m0_ref[bi]
            for h in range(C_HEADS):
                j = h % 2
                caug_s[bi, h] = jnp.zeros((LANES, 2 * C_DV), F32)
                caug_s[bi, h, j * C_DQK:(j + 1) * C_DQK, 0:C_DV] = c0_ref[bi, h]
                n_col = _row_to_col(n0_ref[bi, h // 2:h // 2 + 1, :])
                own = (krow // C_DQK) == j
                caug_s[bi, h, :, C_DV:2 * C_DV] = jnp.where(own, jnp.broadcast_to(n_col, (LANES, C_DV)), 0.0)
                m_s[bi, h:h + 1, :] = jnp.broadcast_to(m0[:, h:h + 1], (1, LANES))

    tri = _tri_incl(chunk)
    sel = _replicate_selector(n_gates)
    ones_blk = jnp.ones((chunk, C_DV), F32)
    off = 2 * C_QK
    piece = 4 * LANES
    rep = {}

    def stack(ref, rows):
        return jnp.concatenate([ref[bi, rows, :] for bi in range(nb)], axis=0)

    def unstack(ref, rows, cols, val):
        for bi in range(nb):
            ref[bi, rows, cols] = val[bi * chunk:(bi + 1) * chunk, :]

    def proj_task(c):
        rows = slice(c * chunk, (c + 1) * chunk)
        x1 = stack(x_ref, rows) + jnp.dot(stack(mix_ref, rows).astype(BF16), wo0_ref[...],
                                          preferred_element_type=F32)
        unstack(x1_s, rows, slice(0, D_MODEL), x1)
        yield
        xn = _rms(x1, n1_ref[...]).astype(BF16)
        g = jnp.dot(xn, w1_ref[:, P1_GATES:P1_PAD], preferred_element_type=F32) + bias_ref[...]
        cap = C_GATE_CAP * jnp.tanh(g / C_GATE_CAP)
        unstack(gsc_s, rows, slice(0, LANES), jnp.where(is_f, -_softplus(-cap), cap))
        yield
        for j in range(0, P1_GATES, piece):
            unstack(proj_s, rows, slice(j, j + piece),
                    jnp.dot(xn, w1_ref[:, j:j + piece], preferred_element_type=F32))
            yield

    def gate_task(bi, c):
        rows = slice(c * chunk, (c + 1) * chunk)
        gl = gsc_s[bi, rows, :]
        bcum = _dot_exact_lhs(tri, gl)
        yield
        rep[bi, c] = _replicate(jnp.where(is_f, bcum, gl), sel, n_gates)

    def head_chain(bi, c, h):
        rows = slice(c * chunk, (c + 1) * chunk)
        p, j = h // 2, h % 2
        own_lane = (lane // C_DQK) == j
        own_row = (krow // C_DQK) == j
        qp = proj_s[bi, rows, p * LANES:(p + 1) * LANES] * (C_DQK ** -0.5)
        kp = proj_s[bi, rows, C_QK + p * LANES:C_QK + (p + 1) * LANES]
        v = proj_s[bi, rows, off + h * C_DV:off + (h + 1) * C_DV]
        vaug = jnp.concatenate([v, ones_blk], axis=1)
        qk = _dot_nt(jnp.where(own_lane, qp, 0.0), kp)
        cm = caug_s[bi, h]
        qc = _dot(qp, cm)
        i_rep = rep[bi, c][:, h * LANES:(h + 1) * LANES]
        b_rep = rep[bi, c][:, (C_HEADS + h) * LANES:(C_HEADS + h + 1) * LANES]
        b_row = _col_to_row(b_rep[:, 0:chunk])
        i_row = _col_to_row(i_rep[:, 0:chunk])
        dmat = jnp.where(tri, b_rep[:, 0:chunk] - b_row + i_row, -jnp.inf)
        dmax = b_rep + _cummax_rows(i_rep - b_rep)
        m = m_s[bi, h:h + 1, :]
        inter = b_rep + m
        mt = jnp.maximum(inter, dmax)
        m_new = mt[chunk - 1:chunk, :]
        b_last = b_rep[chunk - 1:chunk, :]
        w_s = jnp.exp(b_last - b_rep + i_rep - m_new)
        e_c = jnp.exp(b_last + m - m_new)
        e_inter = jnp.exp(inter - mt)
        upd = _dot_tn(w_s * kp, vaug)
        yield
        pm = jnp.exp(dmat - mt[:, 0:chunk]) * qk
        pv = _dot(pm, vaug)
        caug_s[bi, h] = jnp.concatenate([e_c, e_c], axis=1) * cm + jnp.where(own_row, upd, 0.0)
        m_s[bi, h:h + 1, :] = m_new
        yield
        num = e_inter * qc[:, 0:C_DV] + pv[:, 0:C_DV]
        den = e_inter * qc[:, C_DV:2 * C_DV] + pv[:, C_DV:2 * C_DV]
        hh = num / jnp.maximum(jnp.abs(den), jnp.exp(-mt))
        og = proj_s[bi, rows, off + C_V + h * C_DV:off + C_V + (h + 1) * C_DV]
        z = proj_s[bi, rows, off + 2 * C_V + h * C_DV:off + 2 * C_V + (h + 1) * C_DV]
        h_s[bi, rows, h * C_DV:(h + 1) * C_DV] = hh * _sigmoid(og) * _silu(z)

    def final_task(c):
        rows = slice(c * chunk, (c + 1) * chunk)
        x2 = stack(x1_s, rows) + jnp.dot(stack(h_s, rows).astype(BF16), wo1_ref[...],
                                         preferred_element_type=F32)
        yield
        unstack(y_ref, rows, slice(0, D_MODEL), _rms(x2, nf_ref[...]))

    tasks = []
    for c in range(n_chunks):
        tasks.append((("p", c), "proj", [], functools.partial(proj_task, c)))
    for c in range(n_chunks):
        heads = [("h", bi, c, h) for bi in range(nb) for h in range(C_HEADS)]
        tasks.append((("f", c), "final", heads, functools.partial(final_task, c)))
        for bi in range(nb):
            tasks.append((("g", bi, c), "gates", [("p", c)], functools.partial(gate_task, bi, c)))
            for h in range(C_HEADS):
                deps = [("p", c), ("g", bi, c)] + ([("h", bi, c - 1, h)] if c else [])
                tasks.append((("h", bi, c, h), "heads", deps, functools.partial(head_chain, bi, c, h)))
    _run_tasks(tasks, {"proj": 1, "final": 1})

    @pl.when(i == pl.num_programs(1) - 1)
    def _():
        for bi in range(nb):
            for h in range(C_HEADS):
                j = h % 2
                cout_ref[bi, h] = caug_s[bi, h, j * C_DQK:(j + 1) * C_DQK, 0:C_DV]
            for p in range(C_HEADS // 2):
                n_col = caug_s[bi, 2 * p, :, C_DV:C_DV + 1] + caug_s[bi, 2 * p + 1, :, C_DV:C_DV + 1]
                nout_ref[bi, p:p + 1, :] = _col_to_row(n_col)
        mout_ref[...] = m_s[...]


def _layer1(x3d, mix3d, c0, n0, m0, q, nb, tb, chunk):
    bsz, seq, _ = x3d.shape
    blk = lambda b, i: (b, i, 0)
    per_b3 = lambda b, i: (b, 0, 0)
    per_b4 = lambda b, i: (b, 0, 0, 0)
    kern = functools.partial(_layer1_kernel, nb=nb, tb=tb, chunk=chunk)
    pairs = C_HEADS // 2
    return pl.pallas_call(
        kern,
        grid=(bsz // nb, seq // tb),
        in_specs=[pl.BlockSpec((nb, tb, D_MODEL), blk),
                  pl.BlockSpec((nb, tb, A_VAL + B_W), blk),
                  pl.BlockSpec((nb, C_HEADS, C_DQK, C_DV), per_b4),
                  pl.BlockSpec((nb, pairs, LANES), per_b3),
                  pl.BlockSpec((nb, 1, C_HEADS), per_b3),
                  _resident((A_VAL + B_W, D_MODEL)),
                  _resident((1, D_MODEL)),
                  _resident((D_MODEL, P1_PAD)),
                  _resident((1, LANES)),
                  _resident((C_V, D_MODEL)),
                  _resident((1, D_MODEL))],
        out_specs=[pl.BlockSpec((nb, tb, D_MODEL), blk),
                   pl.BlockSpec((nb, C_HEADS, C_DQK, C_DV), per_b4),
                   pl.BlockSpec((nb, pairs, LANES), per_b3),
                   pl.BlockSpec((nb, C_HEADS, LANES), per_b3)],
        out_shape=[jax.ShapeDtypeStruct((bsz, seq, D_MODEL), F32),
                   jax.ShapeDtypeStruct((bsz, C_HEADS, C_DQK, C_DV), F32),
                   jax.ShapeDtypeStruct((bsz, pairs, LANES), F32),
                   jax.ShapeDtypeStruct((bsz, C_HEADS, LANES), F32)],
        scratch_shapes=[pltpu.VMEM((nb, tb, P1_PAD), F32),
                        pltpu.VMEM((nb, tb, D_MODEL), F32),
                        pltpu.VMEM((nb, tb, C_V), F32),
                        pltpu.VMEM((nb, tb, LANES), F32),
                        pltpu.VMEM((nb, C_HEADS, LANES, 2 * C_DV), F32),
                        pltpu.VMEM((nb, C_HEADS, LANES), F32)],
        compiler_params=pltpu.CompilerParams(dimension_semantics=("parallel", "arbitrary"),
                                             vmem_limit_bytes=VMEM_LIMIT),
        name="layer1",
    )(x3d, mix3d, c0, n0, m0, q["wo0"], q["norm1"], q["w1"], q["gate_bias"], q["wo1"], q["normf"])


def _gate_tile(cols):
    tiled = jnp.concatenate([cols.astype(F32)] * GATE_COPIES, axis=1)
    return jnp.pad(tiled, ((0, 0), (0, LANES - tiled.shape[1])))


def _prepare_params(p):
    w_in0 = p["w_in0"]
    a_cols = A_MAIN + 2 * A_HEADS
    w0 = jnp.concatenate([w_in0[:, 0:A_MAIN], _gate_tile(w_in0[:, A_MAIN:a_cols]), w_in0[:, a_cols:]], axis=1)
    w_in1 = p["w_in1"]
    w1 = jnp.concatenate([w_in1[:, 0:P1_GATES], _gate_tile(w_in1[:, P1_GATES:])], axis=1)
    zeros_h = jnp.zeros((A_HEADS,), F32)
    alog = _gate_tile(jnp.concatenate([zeros_h, p["gdn_a_log"]]).reshape(1, -1))
    dtb = _gate_tile(jnp.concatenate([zeros_h, p["gdn_dt_bias"]]).reshape(1, -1))
    gate_bias = _gate_tile(jnp.concatenate([p["mlstm_b_i"], p["mlstm_b_f"]]).reshape(1, -1))
    w2a2 = jnp.zeros((2 * B_LORA, 2 * B_W), F32)
    w2a2 = w2a2.at[0:B_LORA, 0:B_W].set(p["rwkv_w2"]).at[B_LORA:, B_W:].set(p["rwkv_a2"])
    row = lambda v: v.reshape(1, -1).astype(F32)
    return dict(
        w0=w0.astype(BF16), w1=w1.astype(BF16), wb=w_in0[:, a_cols:].astype(BF16),
        wo0=p["w_out0"].astype(BF16),
        wo1=p["w_out1"].astype(BF16), w2a2=w2a2.astype(BF16),
        norm0=row(p["norm0_w"]), norm1=row(p["norm1_w"]), normf=row(p["norm_f_w"]),
        conv_w=p["gdn_conv_w"].astype(F32),
        alog=alog, dtb=dtb,
        gnw=row(p["gdn_norm_w"]),
        mu=row(p["rwkv_mu"]), rw0=row(p["rwkv_w0"]), ra0=row(p["rwkv_a0"]),
        k_k=row(p["rwkv_k_k"]), k_a=row(p["rwkv_k_a"]), r_k=row(p["rwkv_r_k"]),
        ln_w=row(p["rwkv_ln_w"]), ln_b=row(p["rwkv_ln_b"]),
        gate_bias=gate_bias,
    )


def _tiles(bsz, seq):
    chunk = math.gcd(seq, MAX_CHUNK)
    tb = min(seq, 256)
    nb = 2 if seq > MAX_CHUNK else 8
    nb = math.gcd(nb, bsz)
    tm = min(bsz * seq, 512)
    return chunk, tb, nb, tm


def _trunk(x, state, q):
    conv0, s_gdn, x_prev, s_rwkv, c0, n0, m0 = state
    bsz, seq, _ = x.shape
    chunk, tb, nb, tm = _tiles(bsz, seq)
    m = bsz * seq
    x2d = x.reshape(m, D_MODEL)

    proj0 = _norm_proj(x2d, q["norm0"], q["w0"], tm).reshape(bsz, seq, P0_PAD)
    xn_last, prev_row = _aux(x[:, -1, :], x_prev, q["norm0"], q["wb"])
    mix, conv_new, s_gdn_new, s_rwkv_new = _layer0_mixers(
        proj0, conv0, s_gdn, prev_row.reshape(bsz, 1, B_COLS), s_rwkv, q, nb, tb, chunk)
    y, c_new, n_new, m_new = _layer1(x, mix, c0, n0.reshape(bsz, C_HEADS // 2, LANES),
                                     m0.reshape(bsz, 1, C_HEADS), q, nb, tb, chunk)
    return y.reshape(bsz, seq, D_MODEL), (conv_new, s_gdn_new, xn_last, s_rwkv_new, c_new,
                                          n_new.reshape(bsz, C_HEADS, C_DQK), m_new[:, :, 0])


def kernel(x_prompt, x_sample, state_gdn_conv, state_gdn, state_rwkv_shift, state_rwkv,
           state_mlstm_c, state_mlstm_n, state_mlstm_m,
           norm0_w, w_in0, w_out0, gdn_conv_w, gdn_a_log, gdn_dt_bias, gdn_norm_w,
           rwkv_mu, rwkv_w0, rwkv_w2, rwkv_a0, rwkv_a2, rwkv_k_k, rwkv_k_a, rwkv_r_k,
           rwkv_ln_w, rwkv_ln_b, norm1_w, w_in1, w_out1, mlstm_b_i, mlstm_b_f, norm_f_w):
    q = _prepare_params(dict(
        norm0_w=norm0_w, w_in0=w_in0, w_out0=w_out0, gdn_conv_w=gdn_conv_w, gdn_a_log=gdn_a_log,
        gdn_dt_bias=gdn_dt_bias, gdn_norm_w=gdn_norm_w, rwkv_mu=rwkv_mu, rwkv_w0=rwkv_w0,
        rwkv_w2=rwkv_w2, rwkv_a0=rwkv_a0, rwkv_a2=rwkv_a2, rwkv_k_k=rwkv_k_k, rwkv_k_a=rwkv_k_a,
        rwkv_r_k=rwkv_r_k, rwkv_ln_w=rwkv_ln_w, rwkv_ln_b=rwkv_ln_b, norm1_w=norm1_w, w_in1=w_in1,
        w_out1=w_out1, mlstm_b_i=mlstm_b_i, mlstm_b_f=mlstm_b_f, norm_f_w=norm_f_w))
    bsz = x_prompt.shape[0]
    prompt_state = (
        jnp.zeros((bsz, A_CONV - 1, A_QKV), F32),
        jnp.zeros((bsz, A_HEADS, A_DK, A_DV), F32),
        jnp.zeros((bsz, D_MODEL), F32),
        jnp.zeros((bsz, B_HEADS, B_N, B_N), F32),
        jnp.zeros((bsz, C_HEADS, C_DQK, C_DV), F32),
        jnp.zeros((bsz, C_HEADS, C_DQK), F32),
        jnp.zeros((bsz, C_HEADS), F32),
    )
    y_p, sp = _trunk(x_prompt, prompt_state, q)
    sample_state = (state_gdn_conv, state_gdn, state_rwkv_shift, state_rwkv,
                    state_mlstm_c, state_mlstm_n, state_mlstm_m)
    y_s, ss = _trunk(x_sample, sample_state, q)
    return (y_p, y_s) + tuple(sp) + tuple(ss)
```

```python
import functools
import math

import jax
import jax.numpy as jnp
from jax import lax
from jax.experimental import pallas as pl
from jax.experimental.pallas import tpu as pltpu

F32 = jnp.float32
BF16 = jnp.bfloat16

D_MODEL = 1024
EPS = 1e-6
LANES = 128
SUBLANES = 8
A_HEADS, A_DK, A_DV, A_CONV = 4, 128, 128, 4
A_KEY = A_HEADS * A_DK
A_VAL = A_HEADS * A_DV
A_QKV = 2 * A_KEY + A_VAL
A_MAIN = A_QKV + A_VAL
B_HEADS, B_N, B_LORA = 8, 64, 64
B_W = B_HEADS * B_N
B_COLS = 4 * B_W + 2 * B_LORA
B_PAIRS = B_HEADS // 2
B_GN_EPS = 64e-5
B_DECAY_SCALE = math.exp(-0.5)
C_HEADS, C_DQK, C_DV = 8, 64, 128
C_QK = C_HEADS * C_DQK
C_V = C_HEADS * C_DV
C_GATE_CAP = 15.0
P0_GATES = A_MAIN
P0_B = A_MAIN + LANES
P0_PAD = P0_B + B_COLS
P1_GATES = 2 * C_QK + 3 * C_V
P1_PAD = P1_GATES + LANES
MAX_CHUNK = 64
VMEM_LIMIT = 56 * 1024 * 1024


def _dot(a, b):
    return jnp.dot(a.astype(BF16), b.astype(BF16), preferred_element_type=F32)


def _dot_nt(a, b):
    return lax.dot_general(a.astype(BF16), b.astype(BF16), (((1,), (1,)), ((), ())),
                           preferred_element_type=F32)


def _dot_tn(a, b):
    return lax.dot_general(a.astype(BF16), b.astype(BF16), (((0,), (0,)), ((), ())),
                           preferred_element_type=F32)


def _split(x, terms):
    out = []
    for _ in range(terms - 1):
        hi = x.astype(BF16)
        out.append(hi)
        x = x - hi.astype(F32)
    out.append(x.astype(BF16))
    return out


def _dot_exact_lhs(sel, x, terms=3):
    s = sel.astype(BF16)
    return jnp.dot(jnp.concatenate([s] * terms, axis=1), jnp.concatenate(_split(x, terms), axis=0),
                   preferred_element_type=F32)


def _dot_exact_rhs(x, sel, terms=2):
    s = sel.astype(BF16)
    return jnp.dot(jnp.concatenate(_split(x, terms), axis=1), jnp.concatenate([s] * terms, axis=0),
                   preferred_element_type=F32)


def _iota2(shape, dim):
    return lax.broadcasted_iota(jnp.int32, shape, dim)


def _tri_incl(n):
    return (_iota2((n, n), 0) >= _iota2((n, n), 1))


def _col_to_row(col):
    n = col.shape[0]
    eye = _iota2((n, n), 0) == _iota2((n, n), 1)
    return jnp.sum(jnp.where(eye, col, 0.0), axis=0, keepdims=True)


def _row_to_col(row):
    n = row.shape[1]
    eye = _iota2((n, n), 0) == _iota2((n, n), 1)
    return jnp.sum(jnp.where(eye, row, 0.0), axis=1, keepdims=True)


GATE_COPIES = 3


def _replicate(x, sel, group):
    hi, mid, lo = _split(x, GATE_COPIES)
    lane = _iota2(x.shape, 1)
    packed = jnp.where(lane < group, hi, jnp.where(lane < 2 * group, mid, lo))
    packed = jnp.where(lane < GATE_COPIES * group, packed, jnp.zeros_like(packed))
    return jnp.dot(packed, sel, preferred_element_type=F32)


def _replicate_selector(group):
    assert group & (group - 1) == 0 and LANES == 128
    k = _iota2((LANES, group * LANES), 0)
    col = _iota2((LANES, group * LANES), 1)
    hit = (k < GATE_COPIES * group) & (jnp.bitwise_and(k, group - 1) == jnp.right_shift(col, 7))
    return jnp.where(hit, 1.0, 0.0).astype(BF16)


def _cummax_rows(x):
    n = x.shape[0]
    row = _iota2(x.shape, 0)
    sh = 1
    while sh < n:
        x = jnp.maximum(x, jnp.where(row >= sh, pltpu.roll(x, sh, 0), -jnp.inf))
        sh *= 2
    return x


def _inv_unit_lower(p, levels):
    n = p.shape[0]
    eye = (_iota2((n, n), 0) == _iota2((n, n), 1)).astype(F32)
    t = eye + p
    m = 2
    if m < levels:
        p = _dot(p, p)
        yield
    while m < levels:
        m *= 2
        if m >= levels:
            t = t + _dot(p, t)
        elif n % LANES == 0:
            both = _dot(p, jnp.concatenate([p, t], axis=1))
            p, t = both[:, 0:n], t + both[:, n:2 * n]
            yield
        else:
            both = _dot(jnp.concatenate([p, t], axis=0), p)
            p, t = both[0:n, :], t + both[n:2 * n, :]
            yield
    return t


def _interleave(chains):
    live = list(chains)
    while live:
        nxt = []
        for g in live:
            try:
                next(g)
                nxt.append(g)
            except StopIteration:
                pass
        live = nxt


def _sigmoid(x):
    return 1.0 / (1.0 + jnp.exp(-x))


def _silu(x):
    return x * _sigmoid(x)


def _softplus(x):
    return jnp.maximum(x, 0.0) + jnp.log1p(jnp.exp(-jnp.abs(x)))


def _rms(x, w):
    return x * lax.rsqrt(jnp.mean(x * x, axis=-1, keepdims=True) + EPS) * w


def _chunk_loop(n_chunks, body):
    if n_chunks == 1:
        body(0)
    else:
        def step(c, carry):
            body(c)
            return carry
        lax.fori_loop(0, n_chunks, step, 0)


def _row_start(c, chunk):
    if isinstance(c, int):
        return c * chunk
    return pl.multiple_of(c * chunk, chunk)


def _resident(shape):
    zeros = (0,) * len(shape)
    return pl.BlockSpec(shape, lambda *_: zeros, pipeline_mode=pl.Buffered(1))


def _norm_proj_kernel(x_ref, nw_ref, w_ref, o_ref):
    xn = _rms(x_ref[...], nw_ref[...])
    o_ref[...] = jnp.dot(xn.astype(BF16), w_ref[...], preferred_element_type=F32)


def _norm_proj(x2d, norm_w, w_bf16, tm):
    m = x2d.shape[0]
    n = w_bf16.shape[1]
    return pl.pallas_call(
        _norm_proj_kernel,
        grid=(m // tm,),
        in_specs=[pl.BlockSpec((tm, D_MODEL), lambda i: (i, 0)),
                  _resident((1, D_MODEL)),
                  _resident((D_MODEL, n))],
        out_specs=pl.BlockSpec((tm, n), lambda i: (i, 0)),
        out_shape=jax.ShapeDtypeStruct((m, n), F32),
        compiler_params=pltpu.CompilerParams(dimension_semantics=("parallel",),
                                             vmem_limit_bytes=VMEM_LIMIT),
        name="norm_proj0",
    )(x2d, norm_w, w_bf16)


def _mid_kernel(x_ref, oa_ref, ob_ref, woa_ref, wob_ref, nw_ref, wi_ref, x1_ref, p_ref):
    y = jnp.dot(oa_ref[...].astype(BF16), woa_ref[...], preferred_element_type=F32)
    y = y + jnp.dot(ob_ref[...].astype(BF16), wob_ref[...], preferred_element_type=F32)
    x1 = x_ref[...] + y
    x1_ref[...] = x1
    xn = _rms(x1, nw_ref[...])
    p_ref[...] = jnp.dot(xn.astype(BF16), wi_ref[...], preferred_element_type=F32)


def _mid(x2d, oa, ob, woa, wob, norm_w, wi, tm):
    m = x2d.shape[0]
    n = wi.shape[1]
    row = lambda i: (i, 0)
    return pl.pallas_call(
        _mid_kernel,
        grid=(m // tm,),
        in_specs=[pl.BlockSpec((tm, D_MODEL), row),
                  pl.BlockSpec((tm, A_VAL), row),
                  pl.BlockSpec((tm, B_W), lambda i: (i, A_VAL // B_W)),
                  _resident((A_VAL, D_MODEL)),
                  _resident((B_W, D_MODEL)),
                  _resident((1, D_MODEL)),
                  _resident((D_MODEL, n))],
        out_specs=[pl.BlockSpec((tm, D_MODEL), row), pl.BlockSpec((tm, n), row)],
        out_shape=[jax.ShapeDtypeStruct((m, D_MODEL), F32), jax.ShapeDtypeStruct((m, n), F32)],
        compiler_params=pltpu.CompilerParams(dimension_semantics=("parallel",),
                                             vmem_limit_bytes=VMEM_LIMIT),
        name="out0_norm_proj1",
    )(x2d, oa, ob, woa, wob, norm_w, wi)


def _final_kernel(x_ref, h_ref, wo_ref, nw_ref, y_ref):
    x2 = x_ref[...] + jnp.dot(h_ref[...].astype(BF16), wo_ref[...], preferred_element_type=F32)
    y_ref[...] = _rms(x2, nw_ref[...])


def _final(x2d, h, wo, norm_w, tm):
    m = x2d.shape[0]
    row = lambda i: (i, 0)
    return pl.pallas_call(
        _final_kernel,
        grid=(m // tm,),
        in_specs=[pl.BlockSpec((tm, D_MODEL), row),
                  pl.BlockSpec((tm, C_V), row),
                  _resident((C_V, D_MODEL)),
                  _resident((1, D_MODEL))],
        out_specs=pl.BlockSpec((tm, D_MODEL), row),
        out_shape=jax.ShapeDtypeStruct((m, D_MODEL), F32),
        compiler_params=pltpu.CompilerParams(dimension_semantics=("parallel",),
                                             vmem_limit_bytes=VMEM_LIMIT),
        name="out1_norm_f",
    )(x2d, h, wo, norm_w)


def _aux_kernel(xl_ref, xp_ref, nw_ref, wb_ref, xn_ref, prev_ref):
    xn_ref[...] = _rms(xl_ref[...], nw_ref[...])
    prev_ref[...] = jnp.dot(xp_ref[...].astype(BF16), wb_ref[...], preferred_element_type=F32)


def _aux(x_last, x_prev, norm_w, wb):
    b = x_last.shape[0]
    return pl.pallas_call(
        _aux_kernel,
        out_shape=[jax.ShapeDtypeStruct((b, D_MODEL), F32), jax.ShapeDtypeStruct((b, B_COLS), F32)],
        compiler_params=pltpu.CompilerParams(vmem_limit_bytes=VMEM_LIMIT),
        name="shift_rows",
    )(x_last, x_prev, norm_w, wb)


def _gdn_kernel(main_ref, gates_ref, conv0_ref, s0_ref, convw_ref, alog_ref, dtb_ref, gnw_ref,
                o_ref, convn_ref, sout_ref, ext_s, qkv_s, gsc_s, st_s, uw_s, attn_s, qg_s, kd_s, ge_s,
                *, nb, tb, chunk):
    i = pl.program_id(1)
    n_chunks = tb // chunk
    tail = SUBLANES

    @pl.when(i == 0)
    def _():
        for bi in range(nb):
            ext_s[bi, 0:tail, :] = jnp.zeros((tail, A_QKV), F32)
            ext_s[bi, tail - (A_CONV - 1):tail, :] = conv0_ref[bi]
        st_s[...] = s0_ref[...]

    @pl.when(i > 0)
    def _():
        for bi in range(nb):
            ext_s[bi, 0:tail, :] = ext_s[bi, tb:tb + tail, :]

    cw = convw_ref[...]
    for bi in range(nb):
        ext_s[bi, tail:tail + tb, :] = main_ref[bi, :, 0:A_QKV]
        for c in range(n_chunks):
            r0 = c * chunk
            y = ext_s[bi, tail + r0:tail + r0 + chunk, :] * cw[A_CONV - 1:A_CONV, :]
            for j in range(A_CONV - 1):
                sh = A_CONV - 1 - j
                y = y + ext_s[bi, tail + r0 - sh:tail + r0 - sh + chunk, :] * cw[j:j + 1, :]
            y = _silu(y)
            for h in range(A_HEADS):
                q = y[:, h * A_DK:(h + 1) * A_DK]
                qkv_s[bi, r0:r0 + chunk, h * A_DK:(h + 1) * A_DK] = (
                    q * lax.rsqrt(jnp.sum(q * q, axis=-1, keepdims=True) + EPS) * (A_DK ** -0.5))
                k = y[:, A_KEY + h * A_DK:A_KEY + (h + 1) * A_DK]
                qkv_s[bi, r0:r0 + chunk, A_KEY + h * A_DK:A_KEY + (h + 1) * A_DK] = (
                    k * lax.rsqrt(jnp.sum(k * k, axis=-1, keepdims=True) + EPS))
            qkv_s[bi, r0:r0 + chunk, 2 * A_KEY:A_QKV] = y[:, 2 * A_KEY:A_QKV]
        g = gates_ref[bi]
        gsc_s[bi, :, 0:LANES] = _sigmoid(g)
        gsc_s[bi, :, LANES:2 * LANES] = -jnp.exp(alog_ref[...]) * _softplus(g + dtb_ref[...])

    n_gates = 2 * A_HEADS
    is_decay = jnp.bitwise_and(_iota2((1, LANES), 1), n_gates - 1) >= A_HEADS
    sel = _replicate_selector(n_gates)
    tri = _tri_incl(chunk)
    strict = _iota2((chunk, chunk), 0) > _iota2((chunk, chunk), 1)
    gnw = gnw_ref[...]

    blocks = [(bi, c) for c in range(n_chunks) for bi in range(nb)]
    gcum = {(bi, c): _dot_exact_lhs(tri, gsc_s[bi, c * chunk:(c + 1) * chunk, LANES:2 * LANES])
            for bi, c in blocks}
    rep = {(bi, c): _replicate(jnp.where(is_decay, gcum[bi, c], gsc_s[bi, c * chunk:(c + 1) * chunk, 0:LANES]),
                               sel, n_gates) for bi, c in blocks}

    def local_chain(bi, c, h):
        rows = slice(c * chunk, (c + 1) * chunk)
        q = qkv_s[bi, rows, h * A_DK:(h + 1) * A_DK]
        k = qkv_s[bi, rows, A_KEY + h * A_DK:A_KEY + (h + 1) * A_DK]
        v = qkv_s[bi, rows, 2 * A_KEY + h * A_DV:2 * A_KEY + (h + 1) * A_DV]
        qk_k = _dot_nt(jnp.concatenate([q, k], axis=0), k)
        yield
        b_rep = rep[bi, c][:, h * LANES:(h + 1) * LANES]
        g_rep = rep[bi, c][:, (A_HEADS + h) * LANES:(A_HEADS + h + 1) * LANES]
        g_row = _col_to_row(g_rep[:, 0:chunk])
        g_last = g_rep[chunk - 1:chunk, :]
        decay = jnp.where(tri, jnp.exp(jnp.where(tri, g_rep[:, 0:chunk] - g_row, 0.0)), 0.0)
        eg = jnp.exp(g_rep)
        qg_s[bi, rows, h * A_DK:(h + 1) * A_DK] = q * eg
        kd_s[bi, rows, h * A_DK:(h + 1) * A_DK] = k * jnp.exp(g_last - g_rep)
        ge_s[bi, c, h:h + 1, :] = jnp.exp(g_last)
        attn_s[bi, rows, h * LANES:h * LANES + chunk] = decay * qk_k[0:chunk, :]
        amat = jnp.where(strict, b_rep[:, 0:chunk] * decay * qk_k[chunk:2 * chunk, :], 0.0)
        t_inv = yield from _inv_unit_lower(-amat, chunk)
        uw_s[bi, rows, 2 * h * LANES:2 * (h + 1) * LANES] = _dot(
            t_inv, jnp.concatenate([b_rep * v, (b_rep * eg) * k], axis=1))

    _interleave(local_chain(bi, c, h) for c in range(n_chunks) for bi in range(nb) for h in range(A_HEADS))

    def body(c):
        r0 = _row_start(c, chunk)
        rows = pl.ds(r0, chunk)

        def state_chain(bi, h):
            uw = uw_s[bi, rows, 2 * h * LANES:2 * (h + 1) * LANES]
            s = st_s[bi, h]
            wq_s = _dot(jnp.concatenate([uw[:, A_DV:A_DV + A_DK], qg_s[bi, rows, h * A_DK:(h + 1) * A_DK]], axis=0), s)
            yield
            nv = uw[:, 0:A_DV] - wq_s[0:chunk, :]
            o = wq_s[chunk:2 * chunk, :] + _dot(attn_s[bi, rows, h * LANES:h * LANES + chunk], nv)
            st_s[bi, h] = ge_s[bi, c, h:h + 1, :] * s + _dot_tn(kd_s[bi, rows, h * A_DK:(h + 1) * A_DK], nv)
            yield
            z = main_ref[bi, rows, A_QKV + h * A_DV:A_QKV + (h + 1) * A_DV]
            o_ref[bi, rows, h * A_DV:(h + 1) * A_DV] = _rms(o, gnw) * _silu(z)

        _interleave(state_chain(bi, h) for bi in range(nb) for h in range(A_HEADS))

    _chunk_loop(n_chunks, body)

    @pl.when(i == pl.num_programs(1) - 1)
    def _():
        for bi in range(nb):
            convn_ref[bi] = ext_s[bi, tb + tail - (A_CONV - 1):tb + tail, :]
        sout_ref[...] = st_s[...]


def _gdn(proj3d, conv0, s0, conv_w, alog_row, dtb_row, gn_w, nb, tb, chunk):
    bsz, seq, _ = proj3d.shape
    blk = lambda b, i: (b, i, 0)
    per_b3 = lambda b, i: (b, 0, 0)
    per_b4 = lambda b, i: (b, 0, 0, 0)
    fix2 = lambda b, i: (0, 0)
    kern = functools.partial(_gdn_kernel, nb=nb, tb=tb, chunk=chunk)
    return pl.pallas_call(
        kern,
        grid=(bsz // nb, seq // tb),
        in_specs=[pl.BlockSpec((nb, tb, A_MAIN), blk),
                  pl.BlockSpec((nb, tb, LANES), lambda b, i: (b, i, P0_GATES // LANES)),
                  pl.BlockSpec((nb, A_CONV - 1, A_QKV), per_b3),
                  pl.BlockSpec((nb, A_HEADS, A_DK, A_DV), per_b4),
                  pl.BlockSpec((A_CONV, A_QKV), fix2),
                  pl.BlockSpec((1, LANES), fix2),
                  pl.BlockSpec((1, LANES), fix2),
                  pl.BlockSpec((1, A_DV), fix2)],
        out_specs=[pl.BlockSpec((nb, tb, A_VAL), blk),
                   pl.BlockSpec((nb, A_CONV - 1, A_QKV), per_b3),
                   pl.BlockSpec((nb, A_HEADS, A_DK, A_DV), per_b4)],
        out_shape=[jax.ShapeDtypeStruct((bsz, seq, A_VAL), F32),
                   jax.ShapeDtypeStruct((bsz, A_CONV - 1, A_QKV), F32),
                   jax.ShapeDtypeStruct((bsz, A_HEADS, A_DK, A_DV), F32)],
        scratch_shapes=[pltpu.VMEM((nb, tb + SUBLANES, A_QKV), F32),
                        pltpu.VMEM((nb, tb, A_QKV), F32),
                        pltpu.VMEM((nb, tb, 2 * LANES), F32),
                        pltpu.VMEM((nb, A_HEADS, A_DK, A_DV), F32),
                        pltpu.VMEM((nb, tb, A_HEADS * (A_DV + A_DK)), F32),
                        pltpu.VMEM((nb, tb, A_HEADS * LANES), F32),
                        pltpu.VMEM((nb, tb, A_KEY), F32),
                        pltpu.VMEM((nb, tb, A_KEY), F32),
                        pltpu.VMEM((nb, tb // chunk, SUBLANES, LANES), F32)],
        compiler_params=pltpu.CompilerParams(dimension_semantics=("parallel", "arbitrary"),
                                             vmem_limit_bytes=VMEM_LIMIT),
        name="gdn_mixer",
    )(proj3d, proj3d, conv0, s0, conv_w, alog_row, dtb_row, gn_w)


_FR, _FLW, _FK, _FV, _FA, _FB, _FG = range(7)


def _rwkv_kernel(pb_ref, prev_ref, s0_ref, mu_ref, w0_ref, a0_ref, w2a2_ref, kk_ref, ka_ref,
                 rk_ref, lnw_ref, lnb_ref, o_ref, sout_ref, ext_s, feat_s, st_s, tmp_s,
                 *, nb, tb, chunk):
    i = pl.program_id(1)
    n_chunks = tb // chunk
    tail = SUBLANES
    lane = _iota2((1, LANES), 1)
    head0 = lane < B_N
    ones_blk = (_iota2((LANES, LANES), 0) // B_N == _iota2((LANES, LANES), 1) // B_N)

    @pl.when(i == 0)
    def _():
        for bi in range(nb):
            ext_s[bi, 0:tail, :] = jnp.zeros((tail, B_COLS), F32)
            ext_s[bi, tail - 1:tail, :] = prev_ref[bi]
            for p in range(B_PAIRS):
                tmp_s[...] = jnp.zeros((B_N, LANES), F32)
                tmp_s[:, 0:B_N] = s0_ref[bi, 2 * p + 1]
                shifted = pltpu.roll(tmp_s[...], B_N, 1)
                st_s[bi, p, B_N:LANES, :] = jnp.where(head0, 0.0, shifted)
                tmp_s[:, 0:B_N] = s0_ref[bi, 2 * p]
                st_s[bi, p, 0:B_N, :] = tmp_s[...]

    @pl.when(i > 0)
    def _():
        for bi in range(nb):
            ext_s[bi, 0:tail, :] = ext_s[bi, tb:tb + tail, :]

    mu = mu_ref[...]
    for bi in range(nb):
        ext_s[bi, tail:tail + tb, :] = pb_ref[bi]
        for c in range(n_chunks):
            r0 = c * chunk
            pb = ext_s[bi, tail + r0:tail + r0 + chunk, :]
            pb_prev = ext_s[bi, tail + r0 - 1:tail + r0 - 1 + chunk, :]
            pbs = pb + (pb_prev - pb) * mu
            r = pbs[:, 0:B_W]
            kb = pbs[:, B_W:2 * B_W]
            vb = pbs[:, 2 * B_W:3 * B_W]
            gb = pbs[:, 3 * B_W:4 * B_W]
            lo = pbs[:, 4 * B_W:B_COLS]
            lo = jnp.where(head0, jnp.tanh(lo), lo)
            wa = _dot(lo, w2a2_ref[...])
            logw = -B_DECAY_SCALE * _sigmoid(w0_ref[...] + wa[:, 0:B_W])
            a = _sigmoid(a0_ref[...] + wa[:, B_W:2 * B_W])
            kkx = kb * kk_ref[...]
            rws = slice(r0, r0 + chunk)
            for p in range(B_PAIRS):
                sl = slice(p * LANES, (p + 1) * LANES)
                kp = kkx[:, sl]
                ss = _dot_exact_rhs(kp * kp, ones_blk)
                kkn = kp * lax.rsqrt(ss + EPS)
                feat_s[bi, rws, _FA * B_W + p * LANES:_FA * B_W + (p + 1) * LANES] = -kkn
                feat_s[bi, rws, _FB * B_W + p * LANES:_FB * B_W + (p + 1) * LANES] = kkn * a[:, sl]
            feat_s[bi, rws, _FR * B_W:(_FR + 1) * B_W] = r
            feat_s[bi, rws, _FLW * B_W:(_FLW + 1) * B_W] = logw
            feat_s[bi, rws, _FK * B_W:(_FK + 1) * B_W] = kb * (1.0 + (a - 1.0) * ka_ref[...])
            feat_s[bi, rws, _FV * B_W:(_FV + 1) * B_W] = vb
            feat_s[bi, rws, _FG * B_W:(_FG + 1) * B_W] = gb

    tri = _tri_incl(chunk)
    c2 = 2 * chunk
    ri = _iota2((c2, c2), 0)
    ci = _iota2((c2, c2), 1)
    same_head = (ri // chunk) == (ci // chunk)
    incl_bd = same_head & (ri >= ci)
    strict_bd = same_head & (ri > ci)
    ri4 = _iota2((2 * c2, c2), 0)
    ci4 = _iota2((2 * c2, c2), 1)
    rr4 = ri4 % c2
    lm_mask = ((rr4 // chunk) == (ci4 // chunk)) & ((rr4 > ci4) | ((ri4 >= c2) & (rr4 == ci4)))
    lane_st = (_iota2((c2, LANES), 0) // chunk) == (_iota2((c2, LANES), 1) // B_N)

    def body(c):
        r0 = _row_start(c, chunk)
        rows = pl.ds(r0, chunk)

        def pair_chain(bi, p):
            def feat(slot):
                return feat_s[bi, rows, slot * B_W + p * LANES:slot * B_W + (p + 1) * LANES]
            r, lw, k, v, a_, b_, gate = (feat(s) for s in (_FR, _FLW, _FK, _FV, _FA, _FB, _FG))
            cs = _dot_exact_lhs(tri, lw)
            yield
            cs_last = cs[chunk - 1:chunk, :]
            e_inv = jnp.exp(-cs)
            e_rem = jnp.exp(cs_last - cs)
            at = a_ * jnp.exp(cs - lw)
            rt = r * jnp.exp(cs)
            bt = b_ * e_inv
            kt = k * e_inv
            a_st = jnp.concatenate([jnp.where(head0, at, 0.0), jnp.where(head0, 0.0, at)], axis=0)
            r_st = jnp.concatenate([jnp.where(head0, rt, 0.0), jnp.where(head0, 0.0, rt)], axis=0)
            b2 = jnp.concatenate([bt, bt], axis=0)
            k2 = jnp.concatenate([kt, kt], axis=0)
            v2 = jnp.concatenate([v, v], axis=0)
            ar_st = jnp.concatenate([a_st, r_st], axis=0)
            if c2 % LANES == 0:
                ar_bk = _dot_nt(ar_st, jnp.concatenate([b2, k2], axis=0))
                ar_b, ar_k = ar_bk[:, 0:c2], ar_bk[:, c2:2 * c2]
            else:
                ar_b = _dot_nt(ar_st, b2)
                ar_k = _dot_nt(ar_st, k2)
            s = st_s[bi, p]
            ar = _dot_nt(jnp.concatenate([at, rt], axis=0), s)
            yield
            l_ab = jnp.where(strict_bd, ar_b[0:c2, :], 0.0)
            m_rb = jnp.where(incl_bd, ar_b[c2:2 * c2, :], 0.0)
            lm_k = jnp.where(lm_mask, ar_k, 0.0)
            lm_v = _dot(lm_k, v2)
            t_inv = yield from _inv_unit_lower(l_ab, chunk)
            ah = ar[0:chunk, :]
            rh = ar[chunk:c2, :]
            x_st = jnp.where(lane_st, jnp.concatenate([ah, ah], axis=0) + lm_v[0:c2, :], 0.0)
            u_st = _dot(t_inv, x_st)
            yield
            o_st = jnp.where(lane_st, _dot(m_rb, u_st) + lm_v[c2:2 * c2, :], 0.0)
            u = u_st[0:chunk, :] + u_st[chunk:c2, :]
            upd = _dot_tn(jnp.concatenate([u, v], axis=0),
                          jnp.concatenate([b_ * e_rem, k * e_rem], axis=0))
            yield
            o = rh + o_st[0:chunk, :] + o_st[chunk:c2, :]
            st_s[bi, p] = s * jnp.exp(cs_last) + jnp.where(ones_blk, upd, 0.0)
            sl = slice(p * LANES, (p + 1) * LANES)
            mean = _dot_exact_rhs(o, ones_blk) * (1.0 / B_N)
            bonus = _dot_exact_rhs(r * k * rk_ref[:, sl], ones_blk) * v
            yield
            d = o - mean
            var = _dot_exact_rhs(d * d, ones_blk) * (1.0 / B_N)
            yield
            ob = d * lax.rsqrt(var + B_GN_EPS) * lnw_ref[:, sl] + lnb_ref[:, sl] + bonus
            o_ref[bi, rows, sl] = ob * _silu(gate)

        _interleave(pair_chain(bi, p) for bi in range(nb) for p in range(B_PAIRS))

    _chunk_loop(n_chunks, body)

    @pl.when(i == pl.num_programs(1) - 1)
    def _():
        for bi in range(nb):
            for p in range(B_PAIRS):
                sout_ref[bi, 2 * p] = st_s[bi, p, 0:B_N, 0:B_N]
                shifted = pltpu.roll(st_s[bi, p, B_N:LANES, :], B_N, 1)
                sout_ref[bi, 2 * p + 1] = shifted[:, 0:B_N]


def _rwkv(proj3d, prev_row, s0, mu, w0, a0, w2a2, k_k, k_a, r_k, ln_w, ln_b, nb, tb, chunk):
    bsz, seq, _ = proj3d.shape
    blk = lambda b, i: (b, i, 0)
    per_b3 = lambda b, i: (b, 0, 0)
    per_b4 = lambda b, i: (b, 0, 0, 0)
    fix2 = lambda b, i: (0, 0)
    kern = functools.partial(_rwkv_kernel, nb=nb, tb=tb, chunk=chunk)
    vec = pl.BlockSpec((1, B_W), fix2)
    return pl.pallas_call(
        kern,
        grid=(bsz // nb, seq // tb),
        in_specs=[pl.BlockSpec((nb, tb, B_COLS), lambda b, i: (b, i, P0_B // B_COLS)),
                  pl.BlockSpec((nb, 1, B_COLS), per_b3),
                  pl.BlockSpec((nb, B_HEADS, B_N, B_N), per_b4),
                  pl.BlockSpec((1, B_COLS), fix2),
                  vec, vec,
                  pl.BlockSpec((2 * B_LORA, 2 * B_W), fix2),
                  vec, vec, vec, vec, vec],
        out_specs=[pl.BlockSpec((nb, tb, B_W), blk),
                   pl.BlockSpec((nb, B_HEADS, B_N, B_N), per_b4)],
        out_shape=[jax.ShapeDtypeStruct((bsz, seq, B_W), F32),
                   jax.ShapeDtypeStruct((bsz, B_HEADS, B_N, B_N), F32)],
        scratch_shapes=[pltpu.VMEM((nb, tb + SUBLANES, B_COLS), F32),
                        pltpu.VMEM((nb, tb, 7 * B_W), F32),
                        pltpu.VMEM((nb, B_PAIRS, LANES, LANES), F32),
                        pltpu.VMEM((B_N, LANES), F32)],
        compiler_params=pltpu.CompilerParams(dimension_semantics=("parallel", "arbitrary"),
                                             vmem_limit_bytes=VMEM_LIMIT),
        name="rwkv7_mixer",
    )(proj3d, prev_row, s0, mu, w0, a0, w2a2, k_k, k_a, r_k, ln_w, ln_b)


def _run_tasks(tasks, caps):
    done, live, pending = set(), [], list(tasks)
    while pending or live:
        count = {}
        for _, group, _ in live:
            count[group] = count.get(group, 0) + 1
        still = []
        for key, group, deps, make in pending:
            if all(d in done for d in deps) and count.get(group, 0) < caps.get(group, len(tasks)):
                live.append((key, group, make()))
                count[group] = count.get(group, 0) + 1
            else:
                still.append((key, group, deps, make))
        assert live, "task graph has a cycle or a missing dependency"
        pending = still
        nxt = []
        for key, group, gen in live:
            try:
                next(gen)
                nxt.append((key, group, gen))
            except StopIteration:
                done.add(key)
        live = nxt


def _layer0_kernel(x_ref, conv0_ref, sg0_ref, prev_ref, sr0_ref, n0_ref, win_ref,
                   convw_ref, alog_ref, dtb_ref, gnw_ref, mu_ref, w0_ref, a0_ref, w2a2_ref, kk_ref, ka_ref,
                   rk_ref, lnw_ref, lnb_ref,
                   o_ref, convn_ref, sgout_ref, srout_ref,
                   ext_a, z_s, qkv_s, gsc_s, stg_s, uw_s, attn_s, qg_s, kd_s, ge_s,
                   ext_b, feat_s, str_s, tmp_s, *, nb, tb, chunk):
    i = pl.program_id(1)
    n_chunks = tb // chunk
    tail = SUBLANES
    lane = _iota2((1, LANES), 1)
    head0 = lane < B_N
    ones_blk = (_iota2((LANES, LANES), 0) // B_N == _iota2((LANES, LANES), 1) // B_N)

    @pl.when(i == 0)
    def _():
        for bi in range(nb):
            ext_a[bi, 0:tail, :] = jnp.zeros((tail, A_QKV), F32)
            ext_a[bi, tail - (A_CONV - 1):tail, :] = conv0_ref[bi]
            ext_b[bi, 0:tail, :] = jnp.zeros((tail, B_COLS), F32)
            ext_b[bi, tail - 1:tail, :] = prev_ref[bi]
            for p in range(B_PAIRS):
                tmp_s[...] = jnp.zeros((B_N, LANES), F32)
                tmp_s[:, 0:B_N] = sr0_ref[bi, 2 * p + 1]
                shifted = pltpu.roll(tmp_s[...], B_N, 1)
                str_s[bi, p, B_N:LANES, :] = jnp.where(head0, 0.0, shifted)
                tmp_s[:, 0:B_N] = sr0_ref[bi, 2 * p]
                str_s[bi, p, 0:B_N, :] = tmp_s[...]
        stg_s[...] = sg0_ref[...]

    @pl.when(i > 0)
    def _():
        for bi in range(nb):
            ext_a[bi, 0:tail, :] = ext_a[bi, tb:tb + tail, :]
            ext_b[bi, 0:tail, :] = ext_b[bi, tb:tb + tail, :]

    piece = 4 * LANES

    def proj_task(c):
        rows = slice(c * chunk, (c + 1) * chunk)
        ext_rows = slice(tail + c * chunk, tail + (c + 1) * chunk)
        xn = _rms(jnp.concatenate([x_ref[bi, rows, :] for bi in range(nb)], axis=0), n0_ref[...]).astype(BF16)

        def cols(lo, hi):
            return jnp.dot(xn, win_ref[:, lo:hi], preferred_element_type=F32)

        def unstack(ref, rws, cls, val):
            for bi in range(nb):
                ref[bi, rws, cls] = val[bi * chunk:(bi + 1) * chunk, :]

        for lo in range(0, B_COLS, piece):
            hi = min(lo + piece, B_COLS)
            unstack(ext_b, ext_rows, slice(lo, hi), cols(P0_B + lo, P0_B + hi))
            yield
        g = cols(P0_GATES, P0_B)
        unstack(gsc_s, rows, slice(0, LANES), _sigmoid(g))
        unstack(gsc_s, rows, slice(LANES, 2 * LANES),
                -jnp.exp(alog_ref[...]) * _softplus(g + dtb_ref[...]))
        yield
        for lo in range(0, A_QKV, piece):
            unstack(ext_a, ext_rows, slice(lo, lo + piece), cols(lo, lo + piece))
            yield
        unstack(z_s, rows, slice(0, A_VAL), cols(A_QKV, A_MAIN))

    cw = convw_ref[...]
    n_gates = 2 * A_HEADS
    is_decay = jnp.bitwise_and(lane, n_gates - 1) >= A_HEADS
    sel = _replicate_selector(n_gates)
    tri = _tri_incl(chunk)
    strict = _iota2((chunk, chunk), 0) > _iota2((chunk, chunk), 1)
    gnw = gnw_ref[...]
    rep = {}

    def gdn_pre(bi, c):
        r0 = c * chunk
        y = ext_a[bi, tail + r0:tail + r0 + chunk, :] * cw[A_CONV - 1:A_CONV, :]
        for j in range(A_CONV - 1):
            sh = A_CONV - 1 - j
            y = y + ext_a[bi, tail + r0 - sh:tail + r0 - sh + chunk, :] * cw[j:j + 1, :]
        yield
        y = _silu(y)
        for h in range(A_HEADS):
            q = y[:, h * A_DK:(h + 1) * A_DK]
            qkv_s[bi, r0:r0 + chunk, h * A_DK:(h + 1) * A_DK] = (
                q * lax.rsqrt(jnp.sum(q * q, axis=-1, keepdims=True) + EPS) * (A_DK ** -0.5))
            k = y[:, A_KEY + h * A_DK:A_KEY + (h + 1) * A_DK]
            qkv_s[bi, r0:r0 + chunk, A_KEY + h * A_DK:A_KEY + (h + 1) * A_DK] = (
                k * lax.rsqrt(jnp.sum(k * k, axis=-1, keepdims=True) + EPS))
        qkv_s[bi, r0:r0 + chunk, 2 * A_KEY:A_QKV] = y[:, 2 * A_KEY:A_QKV]

    def gdn_gates(bi, c):
        rows = slice(c * chunk, (c + 1) * chunk)
        gcum = _dot_exact_lhs(tri, gsc_s[bi, rows, LANES:2 * LANES])
        yield
        rep[bi, c] = _replicate(jnp.where(is_decay, gcum, gsc_s[bi, rows, 0:LANES]), sel, n_gates)

    def gdn_local(bi, c, h):
        rows = slice(c * chunk, (c + 1) * chunk)
        q = qkv_s[bi, rows, h * A_DK:(h + 1) * A_DK]
        k = qkv_s[bi, rows, A_KEY + h * A_DK:A_KEY + (h + 1) * A_DK]
        v = qkv_s[bi, rows, 2 * A_KEY + h * A_DV:2 * A_KEY + (h + 1) * A_DV]
        qk_k = _dot_nt(jnp.concatenate([q, k], axis=0), k)
        yield
        b_rep = rep[bi, c][:, h * LANES:(h + 1) * LANES]
        g_rep = rep[bi, c][:, (A_HEADS + h) * LANES:(A_HEADS + h + 1) * LANES]
        g_row = _col_to_row(g_rep[:, 0:chunk])
        g_last = g_rep[chunk - 1:chunk, :]
        decay = jnp.where(tri, jnp.exp(jnp.where(tri, g_rep[:, 0:chunk] - g_row, 0.0)), 0.0)
        eg = jnp.exp(g_rep)
        qg_s[bi, rows, h * A_DK:(h + 1) * A_DK] = q * eg
        kd_s[bi, rows, h * A_DK:(h + 1) * A_DK] = k * jnp.exp(g_last - g_rep)
        ge_s[bi, c, h:h + 1, :] = jnp.exp(g_last)
        attn_s[bi, rows, h * LANES:h * LANES + chunk] = decay * qk_k[0:chunk, :]
        amat = jnp.where(strict, b_rep[:, 0:chunk] * decay * qk_k[chunk:2 * chunk, :], 0.0)
        t_inv = yield from _inv_unit_lower(-amat, chunk)
        uw_s[bi, rows, 2 * h * LANES:2 * (h + 1) * LANES] = _dot(
            t_inv, jnp.concatenate([b_rep * v, (b_rep * eg) * k], axis=1))

    def gdn_state(bi, c, h):
        rows = slice(c * chunk, (c + 1) * chunk)
        uw = uw_s[bi, rows, 2 * h * LANES:2 * (h + 1) * LANES]
        s = stg_s[bi, h]
        wq_s = _dot(jnp.concatenate([uw[:, A_DV:A_DV + A_DK], qg_s[bi, rows, h * A_DK:(h + 1) * A_DK]], axis=0), s)
        yield
        nv = uw[:, 0:A_DV] - wq_s[0:chunk, :]
        o = wq_s[chunk:2 * chunk, :] + _dot(attn_s[bi, rows, h * LANES:h * LANES + chunk], nv)
        stg_s[bi, h] = ge_s[bi, c, h:h + 1, :] * s + _dot_tn(kd_s[bi, rows, h * A_DK:(h + 1) * A_DK], nv)
        yield
        z = z_s[bi, rows, h * A_DV:(h + 1) * A_DV]
        o_ref[bi, rows, h * A_DV:(h + 1) * A_DV] = _rms(o, gnw) * _silu(z)

    mu = mu_ref[...]
    c2 = 2 * chunk
    ri = _iota2((c2, c2), 0)
    ci = _iota2((c2, c2), 1)
    same_head = (ri // chunk) == (ci // chunk)
    incl_bd = same_head & (ri >= ci)
    strict_bd = same_head & (ri > ci)
    ri4 = _iota2((2 * c2, c2), 0)
    ci4 = _iota2((2 * c2, c2), 1)
    rr4 = ri4 % c2
    lm_mask = ((rr4 // chunk) == (ci4 // chunk)) & ((rr4 > ci4) | ((ri4 >= c2) & (rr4 == ci4)))
    lane_st = (_iota2((c2, LANES), 0) // chunk) == (_iota2((c2, LANES), 1) // B_N)

    def rwkv_pre(bi, c):
        r0 = c * chunk
        rws = slice(r0, r0 + chunk)
        pb = ext_b[bi, tail + r0:tail + r0 + chunk, :]
        pb_prev = ext_b[bi, tail + r0 - 1:tail + r0 - 1 + chunk, :]
        pbs = pb + (pb_prev - pb) * mu
        lo = pbs[:, 4 * B_W:B_COLS]
        lo = jnp.where(head0, jnp.tanh(lo), lo)
        wa = _dot(lo, w2a2_ref[...])
        kb = pbs[:, B_W:2 * B_W]
        kkx = kb * kk_ref[...]
        ss = [_dot_exact_rhs(kkx[:, p * LANES:(p + 1) * LANES] ** 2, ones_blk) for p in range(B_PAIRS)]
        feat_s[bi, rws, _FR * B_W:(_FR + 1) * B_W] = pbs[:, 0:B_W]
        feat_s[bi, rws, _FV * B_W:(_FV + 1) * B_W] = pbs[:, 2 * B_W:3 * B_W]
        feat_s[bi, rws, _FG * B_W:(_FG + 1) * B_W] = pbs[:, 3 * B_W:4 * B_W]
        yield
        logw = -B_DECAY_SCALE * _sigmoid(w0_ref[...] + wa[:, 0:B_W])
        a = _sigmoid(a0_ref[...] + wa[:, B_W:2 * B_W])
        for p in range(B_PAIRS):
            sl = slice(p * LANES, (p + 1) * LANES)
            kkn = kkx[:, sl] * lax.rsqrt(ss[p] + EPS)
            feat_s[bi, rws, _FA * B_W + p * LANES:_FA * B_W + (p + 1) * LANES] = -kkn
            feat_s[bi, rws, _FB * B_W + p * LANES:_FB * B_W + (p + 1) * LANES] = kkn * a[:, sl]
        feat_s[bi, rws, _FLW * B_W:(_FLW + 1) * B_W] = logw
        feat_s[bi, rws, _FK * B_W:(_FK + 1) * B_W] = kb * (1.0 + (a - 1.0) * ka_ref[...])

    local = {}

    def rwkv_local(bi, c, p):
        rows = slice(c * chunk, (c + 1) * chunk)

        def feat(slot):
            return feat_s[bi, rows, slot * B_W + p * LANES:slot * B_W + (p + 1) * LANES]
        r, lw, k, v, a_, b_ = (feat(s) for s in (_FR, _FLW, _FK, _FV, _FA, _FB))
        cs = _dot_exact_lhs(tri, lw)
        yield
        cs_last = cs[chunk - 1:chunk, :]
        e_inv = jnp.exp(-cs)
        e_rem = jnp.exp(cs_last - cs)
        at = a_ * jnp.exp(cs - lw)
        rt = r * jnp.exp(cs)
        bt = b_ * e_inv
        kt = k * e_inv
        a_st = jnp.concatenate([jnp.where(head0, at, 0.0), jnp.where(head0, 0.0, at)], axis=0)
        r_st = jnp.concatenate([jnp.where(head0, rt, 0.0), jnp.where(head0, 0.0, rt)], axis=0)
        b2 = jnp.concatenate([bt, bt], axis=0)
        k2 = jnp.concatenate([kt, kt], axis=0)
        v2 = jnp.concatenate([v, v], axis=0)
        ar_st = jnp.concatenate([a_st, r_st], axis=0)
        if c2 % LANES == 0:
            ar_bk = _dot_nt(ar_st, jnp.concatenate([b2, k2], axis=0))
            ar_b, ar_k = ar_bk[:, 0:c2], ar_bk[:, c2:2 * c2]
        else:
            ar_b = _dot_nt(ar_st, b2)
            ar_k = _dot_nt(ar_st, k2)
        yield
        l_ab = jnp.where(strict_bd, ar_b[0:c2, :], 0.0)
        m_rb = jnp.where(incl_bd, ar_b[c2:2 * c2, :], 0.0)
        lm_k = jnp.where(lm_mask, ar_k, 0.0)
        lm_v = _dot(lm_k, v2)
        sl = slice(p * LANES, (p + 1) * LANES)
        bonus = _dot_exact_rhs(r * k * rk_ref[:, sl], ones_blk) * v
        t_inv = yield from _inv_unit_lower(l_ab, chunk)
        local[bi, c, p] = (t_inv, lm_v, m_rb, jnp.concatenate([at, rt], axis=0),
                           jnp.concatenate([b_ * e_rem, k * e_rem], axis=0), jnp.exp(cs_last), bonus)

    def rwkv_state(bi, c, p):
        rows = slice(c * chunk, (c + 1) * chunk)
        sl = slice(p * LANES, (p + 1) * LANES)
        t_inv, lm_v, m_rb, at_rt, bk_rem, decay_end, bonus = local.pop((bi, c, p))
        v = feat_s[bi, rows, _FV * B_W + p * LANES:_FV * B_W + (p + 1) * LANES]
        gate = feat_s[bi, rows, _FG * B_W + p * LANES:_FG * B_W + (p + 1) * LANES]
        s = str_s[bi, p]
        ar = _dot_nt(at_rt, s)
        yield
        ah = ar[0:chunk, :]
        rh = ar[chunk:c2, :]
        x_st = jnp.where(lane_st, jnp.concatenate([ah, ah], axis=0) + lm_v[0:c2, :], 0.0)
        u_st = _dot(t_inv, x_st)
        yield
        o_st = jnp.where(lane_st, _dot(m_rb, u_st) + lm_v[c2:2 * c2, :], 0.0)
        u = u_st[0:chunk, :] + u_st[chunk:c2, :]
        upd = _dot_tn(jnp.concatenate([u, v], axis=0), bk_rem)
        yield
        o = rh + o_st[0:chunk, :] + o_st[chunk:c2, :]
        str_s[bi, p] = s * decay_end + jnp.where(ones_blk, upd, 0.0)
        mean = _dot_exact_rhs(o, ones_blk) * (1.0 / B_N)
        yield
        d = o - mean
        var = _dot_exact_rhs(d * d, ones_blk) * (1.0 / B_N)
        yield
        ob = d * lax.rsqrt(var + B_GN_EPS) * lnw_ref[:, sl] + lnb_ref[:, sl] + bonus
        o_ref[bi, rows, A_VAL + p * LANES:A_VAL + (p + 1) * LANES] = ob * _silu(gate)

    tasks = []
    blocks = [(bi, c) for c in range(n_chunks) for bi in range(nb)]
    for c in range(n_chunks):
        tasks.append((("in", c), "proj", [], functools.partial(proj_task, c)))
    for bi, c in blocks:
        tasks.append((("rp", bi, c), "rwkv_pre", [("in", c)], functools.partial(rwkv_pre, bi, c)))
    for bi, c in blocks:
        for p in range(B_PAIRS):
            deps = [("rl", bi, c, p)] + ([("rs", bi, c - 1, p)] if c else [])
            tasks.append((("rs", bi, c, p), "rwkv_state", deps, functools.partial(rwkv_state, bi, c, p)))
        for h in range(A_HEADS):
            deps = [("gl", bi, c, h)] + ([("gs", bi, c - 1, h)] if c else [])
            tasks.append((("gs", bi, c, h), "gdn_state", deps, functools.partial(gdn_state, bi, c, h)))
    for bi, c in blocks:
        for p in range(B_PAIRS):
            tasks.append((("rl", bi, c, p), "rwkv_local", [("rp", bi, c)], functools.partial(rwkv_local, bi, c, p)))
        for h in range(A_HEADS):
            tasks.append((("gl", bi, c, h), "gdn_local", [("gp", bi, c), ("gg", bi, c)],
                          functools.partial(gdn_local, bi, c, h)))
    for bi, c in blocks:
        tasks.append((("gg", bi, c), "gdn_gates", [("in", c)], functools.partial(gdn_gates, bi, c)))
        tasks.append((("gp", bi, c), "gdn_pre", [("in", c)], functools.partial(gdn_pre, bi, c)))
    _run_tasks(tasks, {"proj": 1, "rwkv_pre": 2, "gdn_pre": 2, "gdn_gates": 2,
                       "rwkv_local": 2 * B_PAIRS, "gdn_local": 2 * A_HEADS})

    @pl.when(i == pl.num_programs(1) - 1)
    def _():
        for bi in range(nb):
            convn_ref[bi] = ext_a[bi, tb + tail - (A_CONV - 1):tb + tail, :]
            for p in range(B_PAIRS):
                srout_ref[bi, 2 * p] = str_s[bi, p, 0:B_N, 0:B_N]
                shifted = pltpu.roll(str_s[bi, p, B_N:LANES, :], B_N, 1)
                srout_ref[bi, 2 * p + 1] = shifted[:, 0:B_N]
        sgout_ref[...] = stg_s[...]


def _layer0(x3d, conv0, s_gdn, prev_row, s_rwkv, q, nb, tb, chunk):
    bsz, seq, _ = x3d.shape
    blk = lambda b, i: (b, i, 0)
    per_b3 = lambda b, i: (b, 0, 0)
    per_b4 = lambda b, i: (b, 0, 0, 0)
    fix2 = lambda b, i: (0, 0)
    kern = functools.partial(_layer0_kernel, nb=nb, tb=tb, chunk=chunk)
    lanes = pl.BlockSpec((1, LANES), fix2)
    vec = pl.BlockSpec((1, B_W), fix2)
    return pl.pallas_call(
        kern,
        grid=(bsz // nb, seq // tb),
        in_specs=[pl.BlockSpec((nb, tb, D_MODEL), blk),
                  pl.BlockSpec((nb, A_CONV - 1, A_QKV), per_b3),
                  pl.BlockSpec((nb, A_HEADS, A_DK, A_DV), per_b4),
                  pl.BlockSpec((nb, 1, B_COLS), per_b3),
                  pl.BlockSpec((nb, B_HEADS, B_N, B_N), per_b4),
                  _resident((1, D_MODEL)),
                  _resident((D_MODEL, P0_PAD)),
                  pl.BlockSpec((A_CONV, A_QKV), fix2), lanes, lanes, pl.BlockSpec((1, A_DV), fix2),
                  pl.BlockSpec((1, B_COLS), fix2), vec, vec,
                  pl.BlockSpec((2 * B_LORA, 2 * B_W), fix2),
                  vec, vec, vec, vec, vec],
        out_specs=[pl.BlockSpec((nb, tb, A_VAL + B_W), blk),
                   pl.BlockSpec((nb, A_CONV - 1, A_QKV), per_b3),
                   pl.BlockSpec((nb, A_HEADS, A_DK, A_DV), per_b4),
                   pl.BlockSpec((nb, B_HEADS, B_N, B_N), per_b4)],
        out_shape=[jax.ShapeDtypeStruct((bsz, seq, A_VAL + B_W), F32),
                   jax.ShapeDtypeStruct((bsz, A_CONV - 1, A_QKV), F32),
                   jax.ShapeDtypeStruct((bsz, A_HEADS, A_DK, A_DV), F32),
                   jax.ShapeDtypeStruct((bsz, B_HEADS, B_N, B_N), F32)],
        scratch_shapes=[pltpu.VMEM((nb, tb + SUBLANES, A_QKV), F32),
                        pltpu.VMEM((nb, tb, A_VAL), F32),
                        pltpu.VMEM((nb, tb, A_QKV), F32),
                        pltpu.VMEM((nb, tb, 2 * LANES), F32),
                        pltpu.VMEM((nb, A_HEADS, A_DK, A_DV), F32),
                        pltpu.VMEM((nb, tb, A_HEADS * (A_DV + A_DK)), F32),
                        pltpu.VMEM((nb, tb, A_HEADS * LANES), F32),
                        pltpu.VMEM((nb, tb, A_KEY), F32),
                        pltpu.VMEM((nb, tb, A_KEY), F32),
                        pltpu.VMEM((nb, tb // chunk, SUBLANES, LANES), F32),
                        pltpu.VMEM((nb, tb + SUBLANES, B_COLS), F32),
                        pltpu.VMEM((nb, tb, 7 * B_W), F32),
                        pltpu.VMEM((nb, B_PAIRS, LANES, LANES), F32),
                        pltpu.VMEM((B_N, LANES), F32)],
        compiler_params=pltpu.CompilerParams(dimension_semantics=("parallel", "arbitrary"),
                                             vmem_limit_bytes=VMEM_LIMIT),
        name="layer0",
    )(x3d, conv0, s_gdn, prev_row, s_rwkv, q["norm0"], q["w0"],
      q["conv_w"], q["alog"], q["dtb"], q["gnw"], q["mu"], q["rw0"], q["ra0"], q["w2a2"],
      q["k_k"], q["k_a"], q["r_k"], q["ln_w"], q["ln_b"])


def _mlstm_kernel(p_ref, c0_ref, n0_ref, m0_ref, bias_ref, o_ref, cout_ref, nout_ref, mout_ref,
                  gsc_s, caug_s, m_s, *, nb, tb, chunk):
    i = pl.program_id(1)
    n_chunks = tb // chunk
    lane = _iota2((1, LANES), 1)
    krow = _iota2((LANES, 1), 0)
    n_gates = 2 * C_HEADS
    is_f = jnp.bitwise_and(lane, n_gates - 1) >= C_HEADS

    @pl.when(i == 0)
    def _():
        for bi in range(nb):
            m0 = m0_ref[bi]
            for h in range(C_HEADS):
                j = h % 2
                caug_s[bi, h] = jnp.zeros((LANES, 2 * C_DV), F32)
                caug_s[bi, h, j * C_DQK:(j + 1) * C_DQK, 0:C_DV] = c0_ref[bi, h]
                n_col = _row_to_col(n0_ref[bi, h // 2:h // 2 + 1, :])
                own = (krow // C_DQK) == j
                caug_s[bi, h, :, C_DV:2 * C_DV] = jnp.where(own, jnp.broadcast_to(n_col, (LANES, C_DV)), 0.0)
                m_s[bi, h:h + 1, :] = jnp.broadcast_to(m0[:, h:h + 1], (1, LANES))

    for bi in range(nb):
        g = p_ref[bi, :, P1_GATES:P1_PAD] + bias_ref[...]
        cap = C_GATE_CAP * jnp.tanh(g / C_GATE_CAP)
        gsc_s[bi] = jnp.where(is_f, -_softplus(-cap), cap)

    tri = _tri_incl(chunk)
    sel = _replicate_selector(n_gates)
    ones_blk = jnp.ones((chunk, C_DV), F32)
    off = 2 * C_QK

    def body(c):
        r0 = _row_start(c, chunk)
        rows = pl.ds(r0, chunk)
        gl = [gsc_s[bi, rows, :] for bi in range(nb)]
        bcum = [_dot_exact_lhs(tri, g) for g in gl]
        rep = [_replicate(jnp.where(is_f, b, g), sel, n_gates) for g, b in zip(gl, bcum)]

        def head_chain(bi, h):
            p, j = h // 2, h % 2
            own_lane = (lane // C_DQK) == j
            own_row = (krow // C_DQK) == j
            qp = p_ref[bi, rows, p * LANES:(p + 1) * LANES] * (C_DQK ** -0.5)
            kp = p_ref[bi, rows, C_QK + p * LANES:C_QK + (p + 1) * LANES]
            v = p_ref[bi, rows, off + h * C_DV:off + (h + 1) * C_DV]
            vaug = jnp.concatenate([v, ones_blk], axis=1)
            qk = _dot_nt(jnp.where(own_lane, qp, 0.0), kp)
            cm = caug_s[bi, h]
            qc = _dot(qp, cm)
            i_rep = rep[bi][:, h * LANES:(h + 1) * LANES]
            b_rep = rep[bi][:, (C_HEADS + h) * LANES:(C_HEADS + h + 1) * LANES]
            b_row = _col_to_row(b_rep[:, 0:chunk])
            i_row = _col_to_row(i_rep[:, 0:chunk])
            dmat = jnp.where(tri, b_rep[:, 0:chunk] - b_row + i_row, -jnp.inf)
            dmax = b_rep + _cummax_rows(i_rep - b_rep)
            m = m_s[bi, h:h + 1, :]
            inter = b_rep + m
            mt = jnp.maximum(inter, dmax)
            m_new = mt[chunk - 1:chunk, :]
            b_last = b_rep[chunk - 1:chunk, :]
            w_s = jnp.exp(b_last - b_rep + i_rep - m_new)
            e_c = jnp.exp(b_last + m - m_new)
            e_inter = jnp.exp(inter - mt)
            upd = _dot_tn(w_s * kp, vaug)
            yield
            pm = jnp.exp(dmat - mt[:, 0:chunk]) * qk
            pv = _dot(pm, vaug)
            caug_s[bi, h] = jnp.concatenate([e_c, e_c], axis=1) * cm + jnp.where(own_row, upd, 0.0)
            m_s[bi, h:h + 1, :] = m_new
            yield
            num = e_inter * qc[:, 0:C_DV] + pv[:, 0:C_DV]
            den = e_inter * qc[:, C_DV:2 * C_DV] + pv[:, C_DV:2 * C_DV]
            hh = num / jnp.maximum(jnp.abs(den), jnp.exp(-mt))
            og = p_ref[bi, rows, off + C_V + h * C_DV:off + C_V + (h + 1) * C_DV]
            z = p_ref[bi, rows, off + 2 * C_V + h * C_DV:off + 2 * C_V + (h + 1) * C_DV]
            o_ref[bi, rows, h * C_DV:(h + 1) * C_DV] = hh * _sigmoid(og) * _silu(z)

        _interleave(head_chain(bi, h) for bi in range(nb) for h in range(C_HEADS))

    _chunk_loop(n_chunks, body)

    @pl.when(i == pl.num_programs(1) - 1)
    def _():
        for bi in range(nb):
            for h in range(C_HEADS):
                j = h % 2
                cout_ref[bi, h] = caug_s[bi, h, j * C_DQK:(j + 1) * C_DQK, 0:C_DV]
            for p in range(C_HEADS // 2):
                n_col = caug_s[bi, 2 * p, :, C_DV:C_DV + 1] + caug_s[bi, 2 * p + 1, :, C_DV:C_DV + 1]
                nout_ref[bi, p:p + 1, :] = _col_to_row(n_col)
        mout_ref[...] = m_s[...]


def _mlstm(proj3d, c0, n0, m0, bias_row, nb, tb, chunk):
    bsz, seq, _ = proj3d.shape
    blk = lambda b, i: (b, i, 0)
    per_b3 = lambda b, i: (b, 0, 0)
    per_b4 = lambda b, i: (b, 0, 0, 0)
    kern = functools.partial(_mlstm_kernel, nb=nb, tb=tb, chunk=chunk)
    pairs = C_HEADS // 2
    return pl.pallas_call(
        kern,
        grid=(bsz // nb, seq // tb),
        in_specs=[pl.BlockSpec((nb, tb, P1_PAD), blk),
                  pl.BlockSpec((nb, C_HEADS, C_DQK, C_DV), per_b4),
                  pl.BlockSpec((nb, pairs, LANES), per_b3),
                  pl.BlockSpec((nb, 1, C_HEADS), per_b3),
                  pl.BlockSpec((1, LANES), lambda b, i: (0, 0))],
        out_specs=[pl.BlockSpec((nb, tb, C_V), blk),
                   pl.BlockSpec((nb, C_HEADS, C_DQK, C_DV), per_b4),
                   pl.BlockSpec((nb, pairs, LANES), per_b3),
                   pl.BlockSpec((nb, C_HEADS, LANES), per_b3)],
        out_shape=[jax.ShapeDtypeStruct((bsz, seq, C_V), F32),
                   jax.ShapeDtypeStruct((bsz, C_HEADS, C_DQK, C_DV), F32),
                   jax.ShapeDtypeStruct((bsz, pairs, LANES), F32),
                   jax.ShapeDtypeStruct((bsz, C_HEADS, LANES), F32)],
        scratch_shapes=[pltpu.VMEM((nb, tb, LANES), F32),
                        pltpu.VMEM((nb, C_HEADS, LANES, 2 * C_DV), F32),
                        pltpu.VMEM((nb, C_HEADS, LANES), F32)],
        compiler_params=pltpu.CompilerParams(dimension_semantics=("parallel", "arbitrary"),
                                             vmem_limit_bytes=VMEM_LIMIT),
        name="mlstm_mixer",
    )(proj3d, c0, n0, m0, bias_row)


def _layer1_kernel(x_ref, mix_ref, c0_ref, n0_ref, m0_ref, wo0_ref, n1_ref, w1_ref, bias_ref, wo1_ref, nf_ref,
                   y_ref, cout_ref, nout_ref, mout_ref,
                   proj_s, x1_s, h_s, gsc_s, caug_s, m_s, *, nb, tb, chunk):
    i = pl.program_id(1)
    n_chunks = tb // chunk
    lane = _iota2((1, LANES), 1)
    krow = _iota2((LANES, 1), 0)
    n_gates = 2 * C_HEADS
    is_f = jnp.bitwise_and(lane, n_gates - 1) >= C_HEADS

    @pl.when(i == 0)
    def _():
        for bi in range(nb):
            m0 = m0_ref[bi]
            for h in range(C_HEADS):
                j = h % 2
                caug_s[bi, h] = jnp.zeros((LANES, 2 * C_DV), F32)
                caug_s[bi, h, j * C_DQK:(j + 1) * C_DQK, 0:C_DV] = c0_ref[bi, h]
                n_col = _row_to_col(n0_ref[bi, h // 2:h // 2 + 1, :])
                own = (krow // C_DQK) == j
                caug_s[bi, h, :, C_DV:2 * C_DV] = jnp.where(own, jnp.broadcast_to(n_col, (LANES, C_DV)), 0.0)
                m_s[bi, h:h + 1, :] = jnp.broadcast_to(m0[:, h:h + 1], (1, LANES))

    tri = _tri_incl(chunk)
    sel = _replicate_selector(n_gates)
    ones_blk = jnp.ones((chunk, C_DV), F32)
    off = 2 * C_QK
    piece = 4 * LANES
    rep = {}

    def stack(ref, rows):
        return jnp.concatenate([ref[bi, rows, :] for bi in range(nb)], axis=0)

    def unstack(ref, rows, cols, val):
        for bi in range(nb):
            ref[bi, rows, cols] = val[bi * chunk:(bi + 1) * chunk, :]

    def proj_task(c):
        rows = slice(c * chunk, (c + 1) * chunk)
        x1 = stack(x_ref, rows) + jnp.dot(stack(mix_ref, rows).astype(BF16), wo0_ref[...],
                                          preferred_element_type=F32)
        unstack(x1_s, rows, slice(0, D_MODEL), x1)
        yield
        xn = _rms(x1, n1_ref[...]).astype(BF16)
        g = jnp.dot(xn, w1_ref[:, P1_GATES:P1_PAD], preferred_element_type=F32) + bias_ref[...]
        cap = C_GATE_CAP * jnp.tanh(g / C_GATE_CAP)
        unstack(gsc_s, rows, slice(0, LANES), jnp.where(is_f, -_softplus(-cap), cap))
        yield
        for j in range(0, P1_GATES, piece):
            unstack(proj_s, rows, slice(j, j + piece),
                    jnp.dot(xn, w1_ref[:, j:j + piece], preferred_element_type=F32))
            yield

    def gate_task(bi, c):
        rows = slice(c * chunk, (c + 1) * chunk)
        gl = gsc_s[bi, rows, :]
        bcum = _dot_exact_lhs(tri, gl)
        yield
        rep[bi, c] = _replicate(jnp.where(is_f, bcum, gl), sel, n_gates)

    def head_chain(bi, c, h):
        rows = slice(c * chunk, (c + 1) * chunk)
        p, j = h // 2, h % 2
        own_lane = (lane // C_DQK) == j
        own_row = (krow // C_DQK) == j
        qp = proj_s[bi, rows, p * LANES:(p + 1) * LANES] * (C_DQK ** -0.5)
        kp = proj_s[bi, rows, C_QK + p * LANES:C_QK + (p + 1) * LANES]
        v = proj_s[bi, rows, off + h * C_DV:off + (h + 1) * C_DV]
        vaug = jnp.concatenate([v, ones_blk], axis=1)
        qk = _dot_nt(jnp.where(own_lane, qp, 0.0), kp)
        cm = caug_s[bi, h]
        qc = _dot(qp, cm)
        i_rep = rep[bi, c][:, h * LANES:(h + 1) * LANES]
        b_rep = rep[bi, c][:, (C_HEADS + h) * LANES:(C_HEADS + h + 1) * LANES]
        b_row = _col_to_row(b_rep[:, 0:chunk])
        i_row = _col_to_row(i_rep[:, 0:chunk])
        dmat = jnp.where(tri, b_rep[:, 0:chunk] - b_row + i_row, -jnp.inf)
        dmax = b_rep + _cummax_rows(i_rep - b_rep)
        m = m_s[bi, h:h + 1, :]
        inter = b_rep + m
        mt = jnp.maximum(inter, dmax)
        m_new = mt[chunk - 1:chunk, :]
        b_last = b_rep[chunk - 1:chunk, :]
        w_s = jnp.exp(b_last - b_rep + i_rep - m_new)
        e_c = jnp.exp(b_last + m - m_new)
        e_inter = jnp.exp(inter - mt)
        upd = _dot_tn(w_s * kp, vaug)
        yield
        pm = jnp.exp(dmat - mt[:, 0:chunk]) * qk
        pv = _dot(pm, vaug)
        caug_s[bi, h] = jnp.concatenate([e_c, e_c], axis=1) * cm + jnp.where(own_row, upd, 0.0)
        m_s[bi, h:h + 1, :] = m_new
        yield
        num = e_inter * qc[:, 0:C_DV] + pv[:, 0:C_DV]
        den = e_inter * qc[:, C_DV:2 * C_DV] + pv[:, C_DV:2 * C_DV]
        hh = num / jnp.maximum(jnp.abs(den), jnp.exp(-mt))
        og = proj_s[bi, rows, off + C_V + h * C_DV:off + C_V + (h + 1) * C_DV]
        z = proj_s[bi, rows, off + 2 * C_V + h * C_DV:off + 2 * C_V + (h + 1) * C_DV]
        h_s[bi, rows, h * C_DV:(h + 1) * C_DV] = hh * _sigmoid(og) * _silu(z)

    def final_task(c):
        rows = slice(c * chunk, (c + 1) * chunk)
        x2 = stack(x1_s, rows) + jnp.dot(stack(h_s, rows).astype(BF16), wo1_ref[...],
                                         preferred_element_type=F32)
        yield
        unstack(y_ref, rows, slice(0, D_MODEL), _rms(x2, nf_ref[...]))

    tasks = []
    for c in range(n_chunks):
        tasks.append((("p", c), "proj", [], functools.partial(proj_task, c)))
    for c in range(n_chunks):
        heads = [("h", bi, c, h) for bi in range(nb) for h in range(C_HEADS)]
        tasks.append((("f", c), "final", heads, functools.partial(final_task, c)))
        for bi in range(nb):
            tasks.append((("g", bi, c), "gates", [("p", c)], functools.partial(gate_task, bi, c)))
            for h in range(C_HEADS):
                deps = [("p", c), ("g", bi, c)] + ([("h", bi, c - 1, h)] if c else [])
                tasks.append((("h", bi, c, h), "heads", deps, functools.partial(head_chain, bi, c, h)))
    _run_tasks(tasks, {"proj": 1, "final": 1, "heads": 6 if n_chunks > 1 else len(tasks)})

    @pl.when(i == pl.num_programs(1) - 1)
    def _():
        for bi in range(nb):
            for h in range(C_HEADS):
                j = h % 2
                cout_ref[bi, h] = caug_s[bi, h, j * C_DQK:(j + 1) * C_DQK, 0:C_DV]
            for p in range(C_HEADS // 2):
                n_col = caug_s[bi, 2 * p, :, C_DV:C_DV + 1] + caug_s[bi, 2 * p + 1, :, C_DV:C_DV + 1]
                nout_ref[bi, p:p + 1, :] = _col_to_row(n_col)
        mout_ref[...] = m_s[...]


def _layer1(x3d, mix3d, c0, n0, m0, q, nb, tb, chunk):
    bsz, seq, _ = x3d.shape
    blk = lambda b, i: (b, i, 0)
    per_b3 = lambda b, i: (b, 0, 0)
    per_b4 = lambda b, i: (b, 0, 0, 0)
    kern = functools.partial(_layer1_kernel, nb=nb, tb=tb, chunk=chunk)
    pairs = C_HEADS // 2
    return pl.pallas_call(
        kern,
        grid=(bsz // nb, seq // tb),
        in_specs=[pl.BlockSpec((nb, tb, D_MODEL), blk),
                  pl.BlockSpec((nb, tb, A_VAL + B_W), blk),
                  pl.BlockSpec((nb, C_HEADS, C_DQK, C_DV), per_b4),
                  pl.BlockSpec((nb, pairs, LANES), per_b3),
                  pl.BlockSpec((nb, 1, C_HEADS), per_b3),
                  _resident((A_VAL + B_W, D_MODEL)),
                  _resident((1, D_MODEL)),
                  _resident((D_MODEL, P1_PAD)),
                  _resident((1, LANES)),
                  _resident((C_V, D_MODEL)),
                  _resident((1, D_MODEL))],
        out_specs=[pl.BlockSpec((nb, tb, D_MODEL), blk),
                   pl.BlockSpec((nb, C_HEADS, C_DQK, C_DV), per_b4),
                   pl.BlockSpec((nb, pairs, LANES), per_b3),
                   pl.BlockSpec((nb, C_HEADS, LANES), per_b3)],
        out_shape=[jax.ShapeDtypeStruct((bsz, seq, D_MODEL), F32),
                   jax.ShapeDtypeStruct((bsz, C_HEADS, C_DQK, C_DV), F32),
                   jax.ShapeDtypeStruct((bsz, pairs, LANES), F32),
                   jax.ShapeDtypeStruct((bsz, C_HEADS, LANES), F32)],
        scratch_shapes=[pltpu.VMEM((nb, tb, P1_PAD), F32),
                        pltpu.VMEM((nb, tb, D_MODEL), F32),
                        pltpu.VMEM((nb, tb, C_V), F32),
                        pltpu.VMEM((nb, tb, LANES), F32),
                        pltpu.VMEM((nb, C_HEADS, LANES, 2 * C_DV), F32),
                        pltpu.VMEM((nb, C_HEADS, LANES), F32)],
        compiler_params=pltpu.CompilerParams(dimension_semantics=("parallel", "arbitrary"),
                                             vmem_limit_bytes=VMEM_LIMIT),
        name="layer1",
    )(x3d, mix3d, c0, n0, m0, q["wo0"], q["norm1"], q["w1"], q["gate_bias"], q["wo1"], q["normf"])


def _gate_tile(cols):
    tiled = jnp.concatenate([cols.astype(F32)] * GATE_COPIES, axis=1)
    return jnp.pad(tiled, ((0, 0), (0, LANES - tiled.shape[1])))


def _prepare_params(p):
    w_in0 = p["w_in0"]
    a_cols = A_MAIN + 2 * A_HEADS
    w0 = jnp.concatenate([w_in0[:, 0:A_MAIN], _gate_tile(w_in0[:, A_MAIN:a_cols]), w_in0[:, a_cols:]], axis=1)
    w_in1 = p["w_in1"]
    w1 = jnp.concatenate([w_in1[:, 0:P1_GATES], _gate_tile(w_in1[:, P1_GATES:])], axis=1)
    zeros_h = jnp.zeros((A_HEADS,), F32)
    alog = _gate_tile(jnp.concatenate([zeros_h, p["gdn_a_log"]]).reshape(1, -1))
    dtb = _gate_tile(jnp.concatenate([zeros_h, p["gdn_dt_bias"]]).reshape(1, -1))
    gate_bias = _gate_tile(jnp.concatenate([p["mlstm_b_i"], p["mlstm_b_f"]]).reshape(1, -1))
    w2a2 = jnp.zeros((2 * B_LORA, 2 * B_W), F32)
    w2a2 = w2a2.at[0:B_LORA, 0:B_W].set(p["rwkv_w2"]).at[B_LORA:, B_W:].set(p["rwkv_a2"])
    row = lambda v: v.reshape(1, -1).astype(F32)
    return dict(
        w0=w0.astype(BF16), w1=w1.astype(BF16), wb=w_in0[:, a_cols:].astype(BF16),
        wo0=p["w_out0"].astype(BF16),
        wo1=p["w_out1"].astype(BF16), w2a2=w2a2.astype(BF16),
        norm0=row(p["norm0_w"]), norm1=row(p["norm1_w"]), normf=row(p["norm_f_w"]),
        conv_w=p["gdn_conv_w"].astype(F32),
        alog=alog, dtb=dtb,
        gnw=row(p["gdn_norm_w"]),
        mu=row(p["rwkv_mu"]), rw0=row(p["rwkv_w0"]), ra0=row(p["rwkv_a0"]),
        k_k=row(p["rwkv_k_k"]), k_a=row(p["rwkv_k_a"]), r_k=row(p["rwkv_r_k"]),
        ln_w=row(p["rwkv_ln_w"]), ln_b=row(p["rwkv_ln_b"]),
        gate_bias=gate_bias,
    )


def _tiles(bsz, seq):
    chunk = math.gcd(seq, MAX_CHUNK)
    tb = min(seq, 256)
    nb = 2 if seq > MAX_CHUNK else 8
    nb = math.gcd(nb, bsz)
    return chunk, tb, nb


def _trunk(x, state, q):
    conv0, s_gdn, x_prev, s_rwkv, c0, n0, m0 = state
    bsz, seq, _ = x.shape
    chunk, tb, nb = _tiles(bsz, seq)
    xn_last, prev_row = _aux(x[:, -1, :], x_prev, q["norm0"], q["wb"])
    mix, conv_new, s_gdn_new, s_rwkv_new = _layer0(
        x, conv0, s_gdn, prev_row.reshape(bsz, 1, B_COLS), s_rwkv, q, nb, tb, chunk)
    y, c_new, n_new, m_new = _layer1(x, mix, c0, n0.reshape(bsz, C_HEADS // 2, LANES),
                                     m0.reshape(bsz, 1, C_HEADS), q, nb, tb, chunk)
    return y.reshape(bsz, seq, D_MODEL), (conv_new, s_gdn_new, xn_last, s_rwkv_new, c_new,
                                          n_new.reshape(bsz, C_HEADS, C_DQK), m_new[:, :, 0])


def kernel(x_prompt, x_sample, state_gdn_conv, state_gdn, state_rwkv_shift, state_rwkv,
           state_mlstm_c, state_mlstm_n, state_mlstm_m,
           norm0_w, w_in0, w_out0, gdn_conv_w, gdn_a_log, gdn_dt_bias, gdn_norm_w,
           rwkv_mu, rwkv_w0, rwkv_w2, rwkv_a0, rwkv_a2, rwkv_k_k, rwkv_k_a, rwkv_r_k,
           rwkv_ln_w, rwkv_ln_b, norm1_w, w_in1, w_out1, mlstm_b_i, mlstm_b_f, norm_f_w):
    q = _prepare_params(dict(
        norm0_w=norm0_w, w_in0=w_in0, w_out0=w_out0, gdn_conv_w=gdn_conv_w, gdn_a_log=gdn_a_log,
        gdn_dt_bias=gdn_dt_bias, gdn_norm_w=gdn_norm_w, rwkv_mu=rwkv_mu, rwkv_w0=rwkv_w0,
        rwkv_w2=rwkv_w2, rwkv_a0=rwkv_a0, rwkv_a2=rwkv_a2, rwkv_k_k=rwkv_k_k, rwkv_k_a=rwkv_k_a,
        rwkv_r_k=rwkv_r_k, rwkv_ln_w=rwkv_ln_w, rwkv_ln_b=rwkv_ln_b, norm1_w=norm1_w, w_in1=w_in1,
        w_out1=w_out1, mlstm_b_i=mlstm_b_i, mlstm_b_f=mlstm_b_f, norm_f_w=norm_f_w))
    bsz = x_prompt.shape[0]
    prompt_state = (
        jnp.zeros((bsz, A_CONV - 1, A_QKV), F32),
        jnp.zeros((bsz, A_HEADS, A_DK, A_DV), F32),
        jnp.zeros((bsz, D_MODEL), F32),
        jnp.zeros((bsz, B_HEADS, B_N, B_N), F32),
        jnp.zeros((bsz, C_HEADS, C_DQK, C_DV), F32),
        jnp.zeros((bsz, C_HEADS, C_DQK), F32),
        jnp.zeros((bsz, C_HEADS), F32),
    )
    y_p, sp = _trunk(x_prompt, prompt_state, q)
    sample_state = (state_gdn_conv, state_gdn, state_rwkv_shift, state_rwkv,
                    state_mlstm_c, state_mlstm_n, state_mlstm_m)
    y_s, ss = _trunk(x_sample, sample_state, q)
    return (y_p, y_s) + tuple(sp) + tuple(ss)
```

```python
import functools
import math

import jax
import jax.numpy as jnp
from jax import lax
from jax.experimental import pallas as pl
from jax.experimental.pallas import tpu as pltpu

F32 = jnp.float32
BF16 = jnp.bfloat16

D_MODEL = 1024
EPS = 1e-6
LANES = 128
SUBLANES = 8
A_HEADS, A_DK, A_DV, A_CONV = 4, 128, 128, 4
A_KEY = A_HEADS * A_DK
A_VAL = A_HEADS * A_DV
A_QKV = 2 * A_KEY + A_VAL
A_MAIN = A_QKV + A_VAL
B_HEADS, B_N, B_LORA = 8, 64, 64
B_W = B_HEADS * B_N
B_COLS = 4 * B_W + 2 * B_LORA
B_PAIRS = B_HEADS // 2
B_GN_EPS = 64e-5
B_DECAY_SCALE = math.exp(-0.5)
MIX0 = A_VAL + B_W
C_HEADS, C_DQK, C_DV = 8, 64, 128
C_QK = C_HEADS * C_DQK
C_V = C_HEADS * C_DV
C_GATE_CAP = 15.0
P0_GATES = A_MAIN
P0_B = A_MAIN + LANES
P0_PAD = P0_B + B_COLS
P1_GATES = 2 * C_QK + 3 * C_V
P1_PAD = P1_GATES + LANES
MAX_CHUNK = 64
GATE_COPIES = 3
CUMSUM_TERMS = 3
SUM_TERMS = 2
VMEM_LIMIT = 56 * 1024 * 1024


def _dot(a, b):
    return jnp.dot(a.astype(BF16), b.astype(BF16), preferred_element_type=F32)


def _dot_nt(a, b):
    return lax.dot_general(a.astype(BF16), b.astype(BF16), (((1,), (1,)), ((), ())),
                           preferred_element_type=F32)


def _dot_tn(a, b):
    return lax.dot_general(a.astype(BF16), b.astype(BF16), (((0,), (0,)), ((), ())),
                           preferred_element_type=F32)


def _split(x, terms):
    out = []
    for _ in range(terms - 1):
        hi = x.astype(BF16)
        out.append(hi)
        x = x - hi.astype(F32)
    out.append(x.astype(BF16))
    return out


def _stack_selector(sel, terms, axis):
    s = jnp.where(sel, 1.0, 0.0).astype(BF16)
    return jnp.concatenate([s] * terms, axis=axis)


def _dot_exact_lhs(sel_stacked, x, terms=CUMSUM_TERMS):
    return jnp.dot(sel_stacked, jnp.concatenate(_split(x, terms), axis=0), preferred_element_type=F32)


def _dot_exact_rhs(x, sel_stacked, terms=SUM_TERMS):
    return jnp.dot(jnp.concatenate(_split(x, terms), axis=1), sel_stacked, preferred_element_type=F32)


def _iota2(shape, dim):
    return lax.broadcasted_iota(jnp.int32, shape, dim)


def _tri_incl(n):
    return (_iota2((n, n), 0) >= _iota2((n, n), 1))


def _eye(n):
    return jnp.where(_iota2((n, n), 0) == _iota2((n, n), 1), 1.0, 0.0)


def _col_to_row(col):
    n = col.shape[0]
    eye = _iota2((n, n), 0) == _iota2((n, n), 1)
    return jnp.sum(jnp.where(eye, col, 0.0), axis=0, keepdims=True)


def _row_to_col(row):
    n = row.shape[1]
    eye = _iota2((n, n), 0) == _iota2((n, n), 1)
    return jnp.sum(jnp.where(eye, row, 0.0), axis=1, keepdims=True)


def _replicate(x, sel, group):
    hi, mid, lo = _split(x, GATE_COPIES)
    lane = _iota2(x.shape, 1)
    packed = jnp.where(lane < group, hi, jnp.where(lane < 2 * group, mid, lo))
    packed = jnp.where(lane < GATE_COPIES * group, packed, jnp.zeros_like(packed))
    return jnp.dot(packed, sel, preferred_element_type=F32)


def _replicate_selector(group):
    assert group & (group - 1) == 0 and LANES == 128
    k = _iota2((LANES, group * LANES), 0)
    col = _iota2((LANES, group * LANES), 1)
    hit = (k < GATE_COPIES * group) & (jnp.bitwise_and(k, group - 1) == jnp.right_shift(col, 7))
    return jnp.where(hit, 1.0, 0.0).astype(BF16)


def _cummax_rows(x):
    n = x.shape[0]
    row = _iota2(x.shape, 0)
    sh = 1
    while sh < n:
        x = jnp.maximum(x, jnp.where(row >= sh, pltpu.roll(x, sh, 0), -jnp.inf))
        sh *= 2
    return x


def _inv_unit_lower(p, levels, eye):
    n = p.shape[0]
    t = eye + p
    m = 2
    if m < levels:
        p = _dot(p, p)
        yield
    while m < levels:
        m *= 2
        if m >= levels:
            t = t + _dot(p, t)
        elif n % LANES == 0:
            both = _dot(p, jnp.concatenate([p, t], axis=1))
            p, t = both[:, 0:n], t + both[:, n:2 * n]
            yield
        else:
            both = _dot(jnp.concatenate([p, t], axis=0), p)
            p, t = both[0:n, :], t + both[n:2 * n, :]
            yield
    return t


def _run_tasks(tasks, caps):
    done, live, pending = set(), [], list(tasks)
    while pending or live:
        count = {}
        for _, group, _ in live:
            count[group] = count.get(group, 0) + 1
        still = []
        for key, group, deps, make in pending:
            if all(d in done for d in deps) and count.get(group, 0) < caps.get(group, len(tasks)):
                live.append((key, group, make()))
                count[group] = count.get(group, 0) + 1
            else:
                still.append((key, group, deps, make))
        assert live, "task graph has a cycle or a missing dependency"
        pending = still
        nxt = []
        for key, group, gen in live:
            try:
                next(gen)
                nxt.append((key, group, gen))
            except StopIteration:
                done.add(key)
        live = nxt


def _sigmoid(x):
    return 1.0 / (1.0 + jnp.exp(-x))


def _silu(x):
    return x * _sigmoid(x)


def _softplus(x):
    return jnp.maximum(x, 0.0) + jnp.log1p(jnp.exp(-jnp.abs(x)))


def _rms(x, w):
    return x * lax.rsqrt(jnp.mean(x * x, axis=-1, keepdims=True) + EPS) * w


def _resident(shape):
    zeros = (0,) * len(shape)
    return pl.BlockSpec(shape, lambda *_: zeros, pipeline_mode=pl.Buffered(1))


def _norm_proj_kernel(x_ref, nw_ref, w_ref, o_ref):
    xn = _rms(x_ref[...], nw_ref[...])
    o_ref[...] = jnp.dot(xn.astype(BF16), w_ref[...], preferred_element_type=F32)


def _norm_proj(x2d, norm_w, w_bf16, tm):
    m = x2d.shape[0]
    n = w_bf16.shape[1]
    return pl.pallas_call(
        _norm_proj_kernel,
        grid=(m // tm,),
        in_specs=[pl.BlockSpec((tm, D_MODEL), lambda i: (i, 0)),
                  _resident((1, D_MODEL)),
                  _resident((D_MODEL, n))],
        out_specs=pl.BlockSpec((tm, n), lambda i: (i, 0)),
        out_shape=jax.ShapeDtypeStruct((m, n), F32),
        compiler_params=pltpu.CompilerParams(dimension_semantics=("parallel",),
                                             vmem_limit_bytes=VMEM_LIMIT),
        name="norm_proj0",
    )(x2d, norm_w, w_bf16)


def _aux_kernel(xl_ref, xp_ref, nw_ref, wb_ref, xn_ref, prev_ref):
    xn_ref[...] = _rms(xl_ref[...], nw_ref[...])
    prev_ref[...] = jnp.dot(xp_ref[...].astype(BF16), wb_ref[...], preferred_element_type=F32)


def _aux(x_last, x_prev, norm_w, wb):
    b = x_last.shape[0]
    return pl.pallas_call(
        _aux_kernel,
        out_shape=[jax.ShapeDtypeStruct((b, D_MODEL), F32), jax.ShapeDtypeStruct((b, B_COLS), F32)],
        compiler_params=pltpu.CompilerParams(vmem_limit_bytes=VMEM_LIMIT),
        name="shift_rows",
    )(x_last, x_prev, norm_w, wb)


_FR, _FLW, _FK, _FV, _FA, _FB, _FG = range(7)


def _layer0_kernel(main_ref, gates_ref, pb_ref, conv0_ref, sg0_ref, prev_ref, sr0_ref,
                   convw_ref, alog_ref, dtb_ref, gnw_ref, mu_ref, w0_ref, a0_ref, w2a2_ref, kk_ref, ka_ref,
                   rk_ref, lnw_ref, lnb_ref,
                   o_ref, convn_ref, sgout_ref, srout_ref,
                   ext_a, qkv_s, gsc_s, stg_s, uw_s, attn_s, qg_s, kd_s, ge_s,
                   ext_b, feat_s, str_s, tmp_s, *, nb, tb, chunk):
    i = pl.program_id(1)
    n_chunks = tb // chunk
    tail = SUBLANES
    lane = _iota2((1, LANES), 1)
    head0 = lane < B_N
    ones_blk = (_iota2((LANES, LANES), 0) // B_N == _iota2((LANES, LANES), 1) // B_N)

    @pl.when(i == 0)
    def _():
        for bi in range(nb):
            ext_a[bi, 0:tail, :] = jnp.zeros((tail, A_QKV), F32)
            ext_a[bi, tail - (A_CONV - 1):tail, :] = conv0_ref[bi]
            ext_b[bi, 0:tail, :] = jnp.zeros((tail, B_COLS), F32)
            ext_b[bi, tail - 1:tail, :] = prev_ref[bi]
            for p in range(B_PAIRS):
                tmp_s[...] = jnp.zeros((B_N, LANES), F32)
                tmp_s[:, 0:B_N] = sr0_ref[bi, 2 * p + 1]
                shifted = pltpu.roll(tmp_s[...], B_N, 1)
                str_s[bi, p, B_N:LANES, :] = jnp.where(head0, 0.0, shifted)
                tmp_s[:, 0:B_N] = sr0_ref[bi, 2 * p]
                str_s[bi, p, 0:B_N, :] = tmp_s[...]
        stg_s[...] = sg0_ref[...]

    @pl.when(i > 0)
    def _():
        for bi in range(nb):
            ext_a[bi, 0:tail, :] = ext_a[bi, tb:tb + tail, :]
            ext_b[bi, 0:tail, :] = ext_b[bi, tb:tb + tail, :]

    for bi in range(nb):
        ext_a[bi, tail:tail + tb, :] = main_ref[bi, :, 0:A_QKV]
        ext_b[bi, tail:tail + tb, :] = pb_ref[bi]
        g = gates_ref[bi]
        gsc_s[bi, :, 0:LANES] = _sigmoid(g)
        gsc_s[bi, :, LANES:2 * LANES] = -jnp.exp(alog_ref[...]) * _softplus(g + dtb_ref[...])

    tri = _tri_incl(chunk)
    tri_sel = _stack_selector(tri, CUMSUM_TERMS, 1)
    ones_sel = _stack_selector(ones_blk, SUM_TERMS, 0)
    eye_c = _eye(chunk)
    eye_2c = _eye(2 * chunk)

    cw = convw_ref[...]
    n_gates = 2 * A_HEADS
    is_decay = jnp.bitwise_and(lane, n_gates - 1) >= A_HEADS
    sel = _replicate_selector(n_gates)
    strict = _iota2((chunk, chunk), 0) > _iota2((chunk, chunk), 1)
    gnw = gnw_ref[...]
    rep = {}

    def gdn_pre(bi, c):
        r0 = c * chunk
        y = ext_a[bi, tail + r0:tail + r0 + chunk, :] * cw[A_CONV - 1:A_CONV, :]
        for j in range(A_CONV - 1):
            sh = A_CONV - 1 - j
            y = y + ext_a[bi, tail + r0 - sh:tail + r0 - sh + chunk, :] * cw[j:j + 1, :]
        yield
        y = _silu(y)
        for h in range(A_HEADS):
            q = y[:, h * A_DK:(h + 1) * A_DK]
            qkv_s[bi, r0:r0 + chunk, h * A_DK:(h + 1) * A_DK] = (
                q * lax.rsqrt(jnp.sum(q * q, axis=-1, keepdims=True) + EPS) * (A_DK ** -0.5))
            k = y[:, A_KEY + h * A_DK:A_KEY + (h + 1) * A_DK]
            qkv_s[bi, r0:r0 + chunk, A_KEY + h * A_DK:A_KEY + (h + 1) * A_DK] = (
                k * lax.rsqrt(jnp.sum(k * k, axis=-1, keepdims=True) + EPS))
        qkv_s[bi, r0:r0 + chunk, 2 * A_KEY:A_QKV] = y[:, 2 * A_KEY:A_QKV]

    def gdn_gates(bi, c):
        rows = slice(c * chunk, (c + 1) * chunk)
        gcum = _dot_exact_lhs(tri_sel, gsc_s[bi, rows, LANES:2 * LANES])
        yield
        rep[bi, c] = _replicate(jnp.where(is_decay, gcum, gsc_s[bi, rows, 0:LANES]), sel, n_gates)

    def gdn_local(bi, c, h):
        rows = slice(c * chunk, (c + 1) * chunk)
        q = qkv_s[bi, rows, h * A_DK:(h + 1) * A_DK]
        k = qkv_s[bi, rows, A_KEY + h * A_DK:A_KEY + (h + 1) * A_DK]
        v = qkv_s[bi, rows, 2 * A_KEY + h * A_DV:2 * A_KEY + (h + 1) * A_DV]
        qk_k = _dot_nt(jnp.concatenate([q, k], axis=0), k)
        yield
        b_rep = rep[bi, c][:, h * LANES:(h + 1) * LANES]
        g_rep = rep[bi, c][:, (A_HEADS + h) * LANES:(A_HEADS + h + 1) * LANES]
        g_row = _col_to_row(g_rep[:, 0:chunk])
        g_last = g_rep[chunk - 1:chunk, :]
        decay = jnp.where(tri, jnp.exp(jnp.where(tri, g_rep[:, 0:chunk] - g_row, 0.0)), 0.0)
        eg = jnp.exp(g_rep)
        qg_s[bi, rows, h * A_DK:(h + 1) * A_DK] = q * eg
        kd_s[bi, rows, h * A_DK:(h + 1) * A_DK] = k * jnp.exp(g_last - g_rep)
        ge_s[bi, c, h:h + 1, :] = jnp.exp(g_last)
        attn_s[bi, rows, h * LANES:h * LANES + chunk] = decay * qk_k[0:chunk, :]
        amat = jnp.where(strict, b_rep[:, 0:chunk] * decay * qk_k[chunk:2 * chunk, :], 0.0)
        t_inv = yield from _inv_unit_lower(-amat, chunk, eye_c)
        uw_s[bi, rows, 2 * h * LANES:2 * (h + 1) * LANES] = _dot(
            t_inv, jnp.concatenate([b_rep * v, (b_rep * eg) * k], axis=1))

    def gdn_state(bi, c, h):
        rows = slice(c * chunk, (c + 1) * chunk)
        uw = uw_s[bi, rows, 2 * h * LANES:2 * (h + 1) * LANES]
        s = stg_s[bi, h]
        wq_s = _dot(jnp.concatenate([uw[:, A_DV:A_DV + A_DK], qg_s[bi, rows, h * A_DK:(h + 1) * A_DK]], axis=0), s)
        yield
        nv = uw[:, 0:A_DV] - wq_s[0:chunk, :]
        o = wq_s[chunk:2 * chunk, :] + _dot(attn_s[bi, rows, h * LANES:h * LANES + chunk], nv)
        stg_s[bi, h] = ge_s[bi, c, h:h + 1, :] * s + _dot_tn(kd_s[bi, rows, h * A_DK:(h + 1) * A_DK], nv)
        yield
        z = main_ref[bi, rows, A_QKV + h * A_DV:A_QKV + (h + 1) * A_DV]
        o_ref[bi, rows, h * A_DV:(h + 1) * A_DV] = _rms(o, gnw) * _silu(z)

    mu = mu_ref[...]
    c2 = 2 * chunk
    ri = _iota2((c2, c2), 0)
    ci = _iota2((c2, c2), 1)
    same_head = (ri // chunk) == (ci // chunk)
    incl_bd = same_head & (ri >= ci)
    strict_bd = same_head & (ri > ci)
    ri4 = _iota2((2 * c2, c2), 0)
    ci4 = _iota2((2 * c2, c2), 1)
    rr4 = ri4 % c2
    lm_mask = ((rr4 // chunk) == (ci4 // chunk)) & ((rr4 > ci4) | ((ri4 >= c2) & (rr4 == ci4)))
    lane_st = (_iota2((c2, LANES), 0) // chunk) == (_iota2((c2, LANES), 1) // B_N)
    local = {}

    def rwkv_pre(bi, c):
        r0 = c * chunk
        rws = slice(r0, r0 + chunk)
        pb = ext_b[bi, tail + r0:tail + r0 + chunk, :]
        pb_prev = ext_b[bi, tail + r0 - 1:tail + r0 - 1 + chunk, :]
        pbs = pb + (pb_prev - pb) * mu
        lo = pbs[:, 4 * B_W:B_COLS]
        lo = jnp.where(head0, jnp.tanh(lo), lo)
        wa = _dot(lo, w2a2_ref[...])
        kb = pbs[:, B_W:2 * B_W]
        kkx = kb * kk_ref[...]
        ss = [_dot_exact_rhs(kkx[:, p * LANES:(p + 1) * LANES] ** 2, ones_sel) for p in range(B_PAIRS)]
        feat_s[bi, rws, _FR * B_W:(_FR + 1) * B_W] = pbs[:, 0:B_W]
        feat_s[bi, rws, _FV * B_W:(_FV + 1) * B_W] = pbs[:, 2 * B_W:3 * B_W]
        feat_s[bi, rws, _FG * B_W:(_FG + 1) * B_W] = pbs[:, 3 * B_W:4 * B_W]
        yield
        logw = -B_DECAY_SCALE * _sigmoid(w0_ref[...] + wa[:, 0:B_W])
        a = _sigmoid(a0_ref[...] + wa[:, B_W:2 * B_W])
        for p in range(B_PAIRS):
            sl = slice(p * LANES, (p + 1) * LANES)
            kkn = kkx[:, sl] * lax.rsqrt(ss[p] + EPS)
            feat_s[bi, rws, _FA * B_W + p * LANES:_FA * B_W + (p + 1) * LANES] = -kkn
            feat_s[bi, rws, _FB * B_W + p * LANES:_FB * B_W + (p + 1) * LANES] = kkn * a[:, sl]
        feat_s[bi, rws, _FLW * B_W:(_FLW + 1) * B_W] = logw
        feat_s[bi, rws, _FK * B_W:(_FK + 1) * B_W] = kb * (1.0 + (a - 1.0) * ka_ref[...])

    def rwkv_local(bi, c, p):
        rows = slice(c * chunk, (c + 1) * chunk)

        def feat(slot):
            return feat_s[bi, rows, slot * B_W + p * LANES:slot * B_W + (p + 1) * LANES]
        r, lw, k, v, a_, b_ = (feat(s) for s in (_FR, _FLW, _FK, _FV, _FA, _FB))
        cs = _dot_exact_lhs(tri_sel, lw)
        yield
        cs_last = cs[chunk - 1:chunk, :]
        e_inv = jnp.exp(-cs)
        e_rem = jnp.exp(cs_last - cs)
        at = a_ * jnp.exp(cs - lw)
        rt = r * jnp.exp(cs)
        bt = b_ * e_inv
        kt = k * e_inv
        a_st = jnp.concatenate([jnp.where(head0, at, 0.0), jnp.where(head0, 0.0, at)], axis=0)
        r_st = jnp.concatenate([jnp.where(head0, rt, 0.0), jnp.where(head0, 0.0, rt)], axis=0)
        b2 = jnp.concatenate([bt, bt], axis=0)
        k2 = jnp.concatenate([kt, kt], axis=0)
        v2 = jnp.concatenate([v, v], axis=0)
        ar_st = jnp.concatenate([a_st, r_st], axis=0)
        if c2 % LANES == 0:
            ar_bk = _dot_nt(ar_st, jnp.concatenate([b2, k2], axis=0))
            ar_b, ar_k = ar_bk[:, 0:c2], ar_bk[:, c2:2 * c2]
        else:
            ar_b = _dot_nt(ar_st, b2)
            ar_k = _dot_nt(ar_st, k2)
        yield
        l_ab = jnp.where(strict_bd, ar_b[0:c2, :], 0.0)
        m_rb = jnp.where(incl_bd, ar_b[c2:2 * c2, :], 0.0)
        lm_k = jnp.where(lm_mask, ar_k, 0.0)
        lm_v = _dot(lm_k, v2)
        sl = slice(p * LANES, (p + 1) * LANES)
        bonus = _dot_exact_rhs(r * k * rk_ref[:, sl], ones_sel) * v
        t_inv = yield from _inv_unit_lower(l_ab, chunk, eye_2c)
        local[bi, c, p] = (t_inv, lm_v, m_rb, jnp.concatenate([at, rt], axis=0),
                           jnp.concatenate([b_ * e_rem, k * e_rem], axis=0), jnp.exp(cs_last), bonus)

    def rwkv_state(bi, c, p):
        rows = slice(c * chunk, (c + 1) * chunk)
        sl = slice(p * LANES, (p + 1) * LANES)
        t_inv, lm_v, m_rb, at_rt, bk_rem, decay_end, bonus = local.pop((bi, c, p))
        v = feat_s[bi, rows, _FV * B_W + p * LANES:_FV * B_W + (p + 1) * LANES]
        gate = feat_s[bi, rows, _FG * B_W + p * LANES:_FG * B_W + (p + 1) * LANES]
        s = str_s[bi, p]
        ar = _dot_nt(at_rt, s)
        yield
        ah = ar[0:chunk, :]
        rh = ar[chunk:c2, :]
        x_st = jnp.where(lane_st, jnp.concatenate([ah, ah], axis=0) + lm_v[0:c2, :], 0.0)
        u_st = _dot(t_inv, x_st)
        yield
        o_st = jnp.where(lane_st, _dot(m_rb, u_st) + lm_v[c2:2 * c2, :], 0.0)
        u = u_st[0:chunk, :] + u_st[chunk:c2, :]
        upd = _dot_tn(jnp.concatenate([u, v], axis=0), bk_rem)
        yield
        o = rh + o_st[0:chunk, :] + o_st[chunk:c2, :]
        str_s[bi, p] = s * decay_end + jnp.where(ones_blk, upd, 0.0)
        mean = _dot_exact_rhs(o, ones_sel) * (1.0 / B_N)
        yield
        d = o - mean
        var = _dot_exact_rhs(d * d, ones_sel) * (1.0 / B_N)
        yield
        ob = d * lax.rsqrt(var + B_GN_EPS) * lnw_ref[:, sl] + lnb_ref[:, sl] + bonus
        o_ref[bi, rows, A_VAL + p * LANES:A_VAL + (p + 1) * LANES] = ob * _silu(gate)

    tasks = []
    blocks = [(bi, c) for c in range(n_chunks) for bi in range(nb)]
    for bi, c in blocks:
        tasks.append((("rp", bi, c), "rwkv_pre", [], functools.partial(rwkv_pre, bi, c)))
    for bi, c in blocks:
        for p in range(B_PAIRS):
            deps = [("rl", bi, c, p)] + ([("rs", bi, c - 1, p)] if c else [])
            tasks.append((("rs", bi, c, p), "rwkv_state", deps, functools.partial(rwkv_state, bi, c, p)))
        for h in range(A_HEADS):
            deps = [("gl", bi, c, h)] + ([("gs", bi, c - 1, h)] if c else [])
            tasks.append((("gs", bi, c, h), "gdn_state", deps, functools.partial(gdn_state, bi, c, h)))
    for bi, c in blocks:
        for p in range(B_PAIRS):
            tasks.append((("rl", bi, c, p), "rwkv_local", [("rp", bi, c)], functools.partial(rwkv_local, bi, c, p)))
        for h in range(A_HEADS):
            tasks.append((("gl", bi, c, h), "gdn_local", [("gp", bi, c), ("gg", bi, c)],
                          functools.partial(gdn_local, bi, c, h)))
    first = [("rp", bi, 0) for bi in range(nb)]
    for bi, c in blocks:
        tasks.append((("gg", bi, c), "gdn_gates", first, functools.partial(gdn_gates, bi, c)))
        tasks.append((("gp", bi, c), "gdn_pre", first, functools.partial(gdn_pre, bi, c)))
    _run_tasks(tasks, {"rwkv_pre": 2, "gdn_pre": 2, "gdn_gates": 2,
                       "rwkv_local": 2 * B_PAIRS, "gdn_local": 2 * A_HEADS})

    @pl.when(i == pl.num_programs(1) - 1)
    def _():
        for bi in range(nb):
            convn_ref[bi] = ext_a[bi, tb + tail - (A_CONV - 1):tb + tail, :]
            for p in range(B_PAIRS):
                srout_ref[bi, 2 * p] = str_s[bi, p, 0:B_N, 0:B_N]
                shifted = pltpu.roll(str_s[bi, p, B_N:LANES, :], B_N, 1)
                srout_ref[bi, 2 * p + 1] = shifted[:, 0:B_N]
        sgout_ref[...] = stg_s[...]


def _layer0_mixers(proj3d, conv0, s_gdn, prev_row, s_rwkv, q, nb, tb, chunk):
    bsz, seq, _ = proj3d.shape
    blk = lambda b, i: (b, i, 0)
    per_b3 = lambda b, i: (b, 0, 0)
    per_b4 = lambda b, i: (b, 0, 0, 0)
    fix2 = lambda b, i: (0, 0)
    kern = functools.partial(_layer0_kernel, nb=nb, tb=tb, chunk=chunk)
    lanes = pl.BlockSpec((1, LANES), fix2)
    vec = pl.BlockSpec((1, B_W), fix2)
    return pl.pallas_call(
        kern,
        grid=(bsz // nb, seq // tb),
        in_specs=[pl.BlockSpec((nb, tb, A_MAIN), blk),
                  pl.BlockSpec((nb, tb, LANES), lambda b, i: (b, i, P0_GATES // LANES)),
                  pl.BlockSpec((nb, tb, B_COLS), lambda b, i: (b, i, P0_B // B_COLS)),
                  pl.BlockSpec((nb, A_CONV - 1, A_QKV), per_b3),
                  pl.BlockSpec((nb, A_HEADS, A_DK, A_DV), per_b4),
                  pl.BlockSpec((nb, 1, B_COLS), per_b3),
                  pl.BlockSpec((nb, B_HEADS, B_N, B_N), per_b4),
                  pl.BlockSpec((A_CONV, A_QKV), fix2), lanes, lanes, pl.BlockSpec((1, A_DV), fix2),
                  pl.BlockSpec((1, B_COLS), fix2), vec, vec,
                  pl.BlockSpec((2 * B_LORA, 2 * B_W), fix2),
                  vec, vec, vec, vec, vec],
        out_specs=[pl.BlockSpec((nb, tb, MIX0), blk),
                   pl.BlockSpec((nb, A_CONV - 1, A_QKV), per_b3),
                   pl.BlockSpec((nb, A_HEADS, A_DK, A_DV), per_b4),
                   pl.BlockSpec((nb, B_HEADS, B_N, B_N), per_b4)],
        out_shape=[jax.ShapeDtypeStruct((bsz, seq, MIX0), F32),
                   jax.ShapeDtypeStruct((bsz, A_CONV - 1, A_QKV), F32),
                   jax.ShapeDtypeStruct((bsz, A_HEADS, A_DK, A_DV), F32),
                   jax.ShapeDtypeStruct((bsz, B_HEADS, B_N, B_N), F32)],
        scratch_shapes=[pltpu.VMEM((nb, tb + SUBLANES, A_QKV), F32),
                        pltpu.VMEM((nb, tb, A_QKV), F32),
                        pltpu.VMEM((nb, tb, 2 * LANES), F32),
                        pltpu.VMEM((nb, A_HEADS, A_DK, A_DV), F32),
                        pltpu.VMEM((nb, tb, A_HEADS * (A_DV + A_DK)), F32),
                        pltpu.VMEM((nb, tb, A_HEADS * LANES), F32),
                        pltpu.VMEM((nb, tb, A_KEY), F32),
                        pltpu.VMEM((nb, tb, A_KEY), F32),
                        pltpu.VMEM((nb, tb // chunk, SUBLANES, LANES), F32),
                        pltpu.VMEM((nb, tb + SUBLANES, B_COLS), F32),
                        pltpu.VMEM((nb, tb, 7 * B_W), F32),
                        pltpu.VMEM((nb, B_PAIRS, LANES, LANES), F32),
                        pltpu.VMEM((B_N, LANES), F32)],
        compiler_params=pltpu.CompilerParams(dimension_semantics=("parallel", "arbitrary"),
                                             vmem_limit_bytes=VMEM_LIMIT),
        name="layer0_mixers",
    )(proj3d, proj3d, proj3d, conv0, s_gdn, prev_row, s_rwkv,
      q["conv_w"], q["alog"], q["dtb"], q["gnw"], q["mu"], q["rw0"], q["ra0"], q["w2a2"],
      q["k_k"], q["k_a"], q["r_k"], q["ln_w"], q["ln_b"])


def _layer1_kernel(x_ref, mix_ref, c0_ref, n0_ref, m0_ref, wo0_ref, n1_ref, w1_ref, bias_ref, wo1_ref, nf_ref,
                   y_ref, cout_ref, nout_ref, mout_ref,
                   proj_s, x1_s, h_s, gsc_s, caug_s, m_s, *, nb, tb, chunk):
    i = pl.program_id(1)
    n_chunks = tb // chunk
    lane = _iota2((1, LANES), 1)
    krow = _iota2((LANES, 1), 0)
    n_gates = 2 * C_HEADS
    is_f = jnp.bitwise_and(lane, n_gates - 1) >= C_HEADS

    @pl.when(i == 0)
    def _():
        for bi in range(nb):
            m0 = m0_ref[bi]
            for h in range(C_HEADS):
                j = h % 2
                caug_s[bi, h] = jnp.zeros((LANES, 2 * C_DV), F32)
                caug_s[bi, h, j * C_DQK:(j + 1) * C_DQK, 0:C_DV] = c0_ref[bi, h]
                n_col = _row_to_col(n0_ref[bi, h // 2:h // 2 + 1, :])
                own = (krow // C_DQK) == j
                caug_s[bi, h, :, C_DV:2 * C_DV] = jnp.where(own, jnp.broadcast_to(n_col, (LANES, C_DV)), 0.0)
                m_s[bi, h:h + 1, :] = jnp.broadcast_to(m0[:, h:h + 1], (1, LANES))

    tri = _tri_incl(chunk)
    tri_sel = _stack_selector(tri, CUMSUM_TERMS, 1)
    sel = _replicate_selector(n_gates)
    ones_blk = jnp.ones((chunk, C_DV), F32)
    off = 2 * C_QK
    piece = 4 * LANES
    rep = {}

    def stack(ref, rows):
        return jnp.concatenate([ref[bi, rows, :] for bi in range(nb)], axis=0)

    def unstack(ref, rows, cols, val):
        for bi in range(nb):
            ref[bi, rows, cols] = val[bi * chunk:(bi + 1) * chunk, :]

    def proj_task(c):
        rows = slice(c * chunk, (c + 1) * chunk)
        x1 = stack(x_ref, rows) + jnp.dot(stack(mix_ref, rows).astype(BF16), wo0_ref[...],
                                          preferred_element_type=F32)
        unstack(x1_s, rows, slice(0, D_MODEL), x1)
        yield
        xn = _rms(x1, n1_ref[...]).astype(BF16)
        g = jnp.dot(xn, w1_ref[:, P1_GATES:P1_PAD], preferred_element_type=F32) + bias_ref[...]
        cap = C_GATE_CAP * jnp.tanh(g / C_GATE_CAP)
        unstack(gsc_s, rows, slice(0, LANES), jnp.where(is_f, -_softplus(-cap), cap))
        yield
        for j in range(0, P1_GATES, piece):
            unstack(proj_s, rows, slice(j, j + piece),
                    jnp.dot(xn, w1_ref[:, j:j + piece], preferred_element_type=F32))
            yield

    def gate_task(bi, c):
        rows = slice(c * chunk, (c + 1) * chunk)
        gl = gsc_s[bi, rows, :]
        bcum = _dot_exact_lhs(tri_sel, gl)
        yield
        rep[bi, c] = _replicate(jnp.where(is_f, bcum, gl), sel, n_gates)

    def head_chain(bi, c, h):
        rows = slice(c * chunk, (c + 1) * chunk)
        p, j = h // 2, h % 2
        own_lane = (lane // C_DQK) == j
        own_row = (krow // C_DQK) == j
        qp = proj_s[bi, rows, p * LANES:(p + 1) * LANES] * (C_DQK ** -0.5)
        kp = proj_s[bi, rows, C_QK + p * LANES:C_QK + (p + 1) * LANES]
        v = proj_s[bi, rows, off + h * C_DV:off + (h + 1) * C_DV]
        vaug = jnp.concatenate([v, ones_blk], axis=1)
        qk = _dot_nt(jnp.where(own_lane, qp, 0.0), kp)
        cm = caug_s[bi, h]
        qc = _dot(qp, cm)
        i_rep = rep[bi, c][:, h * LANES:(h + 1) * LANES]
        b_rep = rep[bi, c][:, (C_HEADS + h) * LANES:(C_HEADS + h + 1) * LANES]
        b_row = _col_to_row(b_rep[:, 0:chunk])
        i_row = _col_to_row(i_rep[:, 0:chunk])
        dmat = jnp.where(tri, b_rep[:, 0:chunk] - b_row + i_row, -jnp.inf)
        dmax = b_rep + _cummax_rows(i_rep - b_rep)
        m = m_s[bi, h:h + 1, :]
        inter = b_rep + m
        mt = jnp.maximum(inter, dmax)
        m_new = mt[chunk - 1:chunk, :]
        b_last = b_rep[chunk - 1:chunk, :]
        w_s = jnp.exp(b_last - b_rep + i_rep - m_new)
        e_c = jnp.exp(b_last + m - m_new)
        e_inter = jnp.exp(inter - mt)
        upd = _dot_tn(w_s * kp, vaug)
        yield
        pm = jnp.exp(dmat - mt[:, 0:chunk]) * qk
        pv = _dot(pm, vaug)
        caug_s[bi, h] = jnp.concatenate([e_c, e_c], axis=1) * cm + jnp.where(own_row, upd, 0.0)
        m_s[bi, h:h + 1, :] = m_new
        yield
        num = e_inter * qc[:, 0:C_DV] + pv[:, 0:C_DV]
        den = e_inter * qc[:, C_DV:2 * C_DV] + pv[:, C_DV:2 * C_DV]
        hh = num / jnp.maximum(jnp.abs(den), jnp.exp(-mt))
        og = proj_s[bi, rows, off + C_V + h * C_DV:off + C_V + (h + 1) * C_DV]
        z = proj_s[bi, rows, off + 2 * C_V + h * C_DV:off + 2 * C_V + (h + 1) * C_DV]
        h_s[bi, rows, h * C_DV:(h + 1) * C_DV] = hh * _sigmoid(og) * _silu(z)

    def final_task(c):
        rows = slice(c * chunk, (c + 1) * chunk)
        x2 = stack(x1_s, rows) + jnp.dot(stack(h_s, rows).astype(BF16), wo1_ref[...],
                                         preferred_element_type=F32)
        yield
        unstack(y_ref, rows, slice(0, D_MODEL), _rms(x2, nf_ref[...]))

    tasks = []
    for c in range(n_chunks):
        tasks.append((("p", c), "proj", [], functools.partial(proj_task, c)))
    for c in range(n_chunks):
        heads = [("h", bi, c, h) for bi in range(nb) for h in range(C_HEADS)]
        tasks.append((("f", c), "final", heads, functools.partial(final_task, c)))
        for bi in range(nb):
            tasks.append((("g", bi, c), "gates", [("p", c)], functools.partial(gate_task, bi, c)))
            for h in range(C_HEADS):
                deps = [("p", c), ("g", bi, c)] + ([("h", bi, c - 1, h)] if c else [])
                tasks.append((("h", bi, c, h), "heads", deps, functools.partial(head_chain, bi, c, h)))
    _run_tasks(tasks, {"proj": 1, "final": 1})

    @pl.when(i == pl.num_programs(1) - 1)
    def _():
        for bi in range(nb):
            for h in range(C_HEADS):
                j = h % 2
                cout_ref[bi, h] = caug_s[bi, h, j * C_DQK:(j + 1) * C_DQK, 0:C_DV]
            for p in range(C_HEADS // 2):
                n_col = caug_s[bi, 2 * p, :, C_DV:C_DV + 1] + caug_s[bi, 2 * p + 1, :, C_DV:C_DV + 1]
                nout_ref[bi, p:p + 1, :] = _col_to_row(n_col)
        mout_ref[...] = m_s[...]


def _layer1(x3d, mix3d, c0, n0, m0, q, nb, tb, chunk):
    bsz, seq, _ = x3d.shape
    blk = lambda b, i: (b, i, 0)
    per_b3 = lambda b, i: (b, 0, 0)
    per_b4 = lambda b, i: (b, 0, 0, 0)
    kern = functools.partial(_layer1_kernel, nb=nb, tb=tb, chunk=chunk)
    pairs = C_HEADS // 2
    return pl.pallas_call(
        kern,
        grid=(bsz // nb, seq // tb),
        in_specs=[pl.BlockSpec((nb, tb, D_MODEL), blk),
                  pl.BlockSpec((nb, tb, MIX0), blk),
                  pl.BlockSpec((nb, C_HEADS, C_DQK, C_DV), per_b4),
                  pl.BlockSpec((nb, pairs, LANES), per_b3),
                  pl.BlockSpec((nb, 1, C_HEADS), per_b3),
                  _resident((MIX0, D_MODEL)),
                  _resident((1, D_MODEL)),
                  _resident((D_MODEL, P1_PAD)),
                  _resident((1, LANES)),
                  _resident((C_V, D_MODEL)),
                  _resident((1, D_MODEL))],
        out_specs=[pl.BlockSpec((nb, tb, D_MODEL), blk),
                   pl.BlockSpec((nb, C_HEADS, C_DQK, C_DV), per_b4),
                   pl.BlockSpec((nb, pairs, LANES), per_b3),
                   pl.BlockSpec((nb, C_HEADS, LANES), per_b3)],
        out_shape=[jax.ShapeDtypeStruct((bsz, seq, D_MODEL), F32),
                   jax.ShapeDtypeStruct((bsz, C_HEADS, C_DQK, C_DV), F32),
                   jax.ShapeDtypeStruct((bsz, pairs, LANES), F32),
                   jax.ShapeDtypeStruct((bsz, C_HEADS, LANES), F32)],
        scratch_shapes=[pltpu.VMEM((nb, tb, P1_PAD), F32),
                        pltpu.VMEM((nb, tb, D_MODEL), F32),
                        pltpu.VMEM((nb, tb, C_V), F32),
                        pltpu.VMEM((nb, tb, LANES), F32),
                        pltpu.VMEM((nb, C_HEADS, LANES, 2 * C_DV), F32),
                        pltpu.VMEM((nb, C_HEADS, LANES), F32)],
        compiler_params=pltpu.CompilerParams(dimension_semantics=("parallel", "arbitrary"),
                                             vmem_limit_bytes=VMEM_LIMIT),
        name="layer1",
    )(x3d, mix3d, c0, n0, m0, q["wo0"], q["norm1"], q["w1"], q["gate_bias"], q["wo1"], q["normf"])


def _gate_tile(cols):
    tiled = jnp.concatenate([cols.astype(F32)] * GATE_COPIES, axis=1)
    return jnp.pad(tiled, ((0, 0), (0, LANES - tiled.shape[1])))


def _prepare_params(p):
    w_in0 = p["w_in0"]
    a_cols = A_MAIN + 2 * A_HEADS
    w0 = jnp.concatenate([w_in0[:, 0:A_MAIN], _gate_tile(w_in0[:, A_MAIN:a_cols]), w_in0[:, a_cols:]], axis=1)
    w_in1 = p["w_in1"]
    w1 = jnp.concatenate([w_in1[:, 0:P1_GATES], _gate_tile(w_in1[:, P1_GATES:])], axis=1)
    zeros_h = jnp.zeros((A_HEADS,), F32)
    alog = _gate_tile(jnp.concatenate([zeros_h, p["gdn_a_log"]]).reshape(1, -1))
    dtb = _gate_tile(jnp.concatenate([zeros_h, p["gdn_dt_bias"]]).reshape(1, -1))
    gate_bias = _gate_tile(jnp.concatenate([p["mlstm_b_i"], p["mlstm_b_f"]]).reshape(1, -1))
    w2a2 = jnp.zeros((2 * B_LORA, 2 * B_W), F32)
    w2a2 = w2a2.at[0:B_LORA, 0:B_W].set(p["rwkv_w2"]).at[B_LORA:, B_W:].set(p["rwkv_a2"])
    row = lambda v: v.reshape(1, -1).astype(F32)
    return dict(
        w0=w0.astype(BF16), w1=w1.astype(BF16), wb=w_in0[:, a_cols:].astype(BF16),
        wo0=p["w_out0"].astype(BF16), wo1=p["w_out1"].astype(BF16), w2a2=w2a2.astype(BF16),
        norm0=row(p["norm0_w"]), norm1=row(p["norm1_w"]), normf=row(p["norm_f_w"]),
        conv_w=p["gdn_conv_w"].astype(F32), alog=alog, dtb=dtb, gnw=row(p["gdn_norm_w"]),
        mu=row(p["rwkv_mu"]), rw0=row(p["rwkv_w0"]), ra0=row(p["rwkv_a0"]),
        k_k=row(p["rwkv_k_k"]), k_a=row(p["rwkv_k_a"]), r_k=row(p["rwkv_r_k"]),
        ln_w=row(p["rwkv_ln_w"]), ln_b=row(p["rwkv_ln_b"]), gate_bias=gate_bias,
    )


def _tiles(bsz, seq):
    chunk = math.gcd(seq, MAX_CHUNK)
    tb = min(seq, 256)
    nb = 2 if seq > MAX_CHUNK else 8
    nb = math.gcd(nb, bsz)
    tm = min(bsz * seq, 512)
    return chunk, tb, nb, tm


def _trunk(x, state, q):
    conv0, s_gdn, x_prev, s_rwkv, c0, n0, m0 = state
    bsz, seq, _ = x.shape
    chunk, tb, nb, tm = _tiles(bsz, seq)
    proj0 = _norm_proj(x.reshape(bsz * seq, D_MODEL), q["norm0"], q["w0"], tm).reshape(bsz, seq, P0_PAD)
    xn_last, prev_row = _aux(x[:, -1, :], x_prev, q["norm0"], q["wb"])
    mix, conv_new, s_gdn_new, s_rwkv_new = _layer0_mixers(
        proj0, conv0, s_gdn, prev_row.reshape(bsz, 1, B_COLS), s_rwkv, q, nb, tb, chunk)
    y, c_new, n_new, m_new = _layer1(x, mix, c0, n0.reshape(bsz, C_HEADS // 2, LANES),
                                     m0.reshape(bsz, 1, C_HEADS), q, nb, tb, chunk)
    return y, (conv_new, s_gdn_new, xn_last, s_rwkv_new, c_new,
               n_new.reshape(bsz, C_HEADS, C_DQK), m_new[:, :, 0])


def kernel(x_prompt, x_sample, state_gdn_conv, state_gdn, state_rwkv_shift, state_rwkv,
           state_mlstm_c, state_mlstm_n, state_mlstm_m,
           norm0_w, w_in0, w_out0, gdn_conv_w, gdn_a_log, gdn_dt_bias, gdn_norm_w,
           rwkv_mu, rwkv_w0, rwkv_w2, rwkv_a0, rwkv_a2, rwkv_k_k, rwkv_k_a, rwkv_r_k,
           rwkv_ln_w, rwkv_ln_b, norm1_w, w_in1, w_out1, mlstm_b_i, mlstm_b_f, norm_f_w):
    q = _prepare_params(dict(
        norm0_w=norm0_w, w_in0=w_in0, w_out0=w_out0, gdn_conv_w=gdn_conv_w, gdn_a_log=gdn_a_log,
        gdn_dt_bias=gdn_dt_bias, gdn_norm_w=gdn_norm_w, rwkv_mu=rwkv_mu, rwkv_w0=rwkv_w0,
        rwkv_w2=rwkv_w2, rwkv_a0=rwkv_a0, rwkv_a2=rwkv_a2, rwkv_k_k=rwkv_k_k, rwkv_k_a=rwkv_k_a,
        rwkv_r_k=rwkv_r_k, rwkv_ln_w=rwkv_ln_w, rwkv_ln_b=rwkv_ln_b, norm1_w=norm1_w, w_in1=w_in1,
        w_out1=w_out1, mlstm_b_i=mlstm_b_i, mlstm_b_f=mlstm_b_f, norm_f_w=norm_f_w))
    bsz = x_prompt.shape[0]
    prompt_state = (
        jnp.zeros((bsz, A_CONV - 1, A_QKV), F32),
        jnp.zeros((bsz, A_HEADS, A_DK, A_DV), F32),
        jnp.zeros((bsz, D_MODEL), F32),
        jnp.zeros((bsz, B_HEADS, B_N, B_N), F32),
        jnp.zeros((bsz, C_HEADS, C_DQK, C_DV), F32),
        jnp.zeros((bsz, C_HEADS, C_DQK), F32),
        jnp.zeros((bsz, C_HEADS), F32),
    )
    y_p, sp = _trunk(x_prompt, prompt_state, q)
    sample_state = (state_gdn_conv, state_gdn, state_rwkv_shift, state_rwkv,
                    state_mlstm_c, state_mlstm_n, state_mlstm_m)
    y_s, ss = _trunk(x_sample, sample_state, q)
    return (y_p, y_s) + tuple(sp) + tuple(ss)
```

```python
import functools
import math

import jax
import jax.numpy as jnp
from jax import lax
from jax.experimental import pallas as pl
from jax.experimental.pallas import tpu as pltpu

F32 = jnp.float32
BF16 = jnp.bfloat16

D_MODEL = 1024
EPS = 1e-6
LANES = 128
SUBLANES = 8
A_HEADS, A_DK, A_DV, A_CONV = 4, 128, 128, 4
A_KEY = A_HEADS * A_DK
A_VAL = A_HEADS * A_DV
A_QKV = 2 * A_KEY + A_VAL
A_MAIN = A_QKV + A_VAL
B_HEADS, B_N, B_LORA = 8, 64, 64
B_W = B_HEADS * B_N
B_COLS = 4 * B_W + 2 * B_LORA
B_PAIRS = B_HEADS // 2
B_GN_EPS = 64e-5
B_DECAY_SCALE = math.exp(-0.5)
MIX0 = A_VAL + B_W
C_HEADS, C_DQK, C_DV = 8, 64, 128
C_QK = C_HEADS * C_DQK
C_V = C_HEADS * C_DV
C_GATE_CAP = 15.0
P0_GATES = A_MAIN
P0_B = A_MAIN + LANES
P0_PAD = P0_B + B_COLS
P1_GATES = 2 * C_QK + 3 * C_V
P1_PAD = P1_GATES + LANES
MAX_CHUNK = 64
GATE_COPIES = 3
CUMSUM_TERMS = 3
SUM_TERMS = 2
VMEM_LIMIT = 56 * 1024 * 1024


def _dot(a, b):
    return jnp.dot(a.astype(BF16), b.astype(BF16), preferred_element_type=F32)


def _dot_nt(a, b):
    return lax.dot_general(a.astype(BF16), b.astype(BF16), (((1,), (1,)), ((), ())),
                           preferred_element_type=F32)


def _dot_tn(a, b):
    return lax.dot_general(a.astype(BF16), b.astype(BF16), (((0,), (0,)), ((), ())),
                           preferred_element_type=F32)


def _split(x, terms):
    out = []
    for _ in range(terms - 1):
        hi = x.astype(BF16)
        out.append(hi)
        x = x - hi.astype(F32)
    out.append(x.astype(BF16))
    return out


def _stack_selector(sel, terms, axis):
    s = jnp.where(sel, 1.0, 0.0).astype(BF16)
    return jnp.concatenate([s] * terms, axis=axis)


def _dot_exact_lhs(sel_stacked, x, terms=CUMSUM_TERMS):
    return jnp.dot(sel_stacked, jnp.concatenate(_split(x, terms), axis=0), preferred_element_type=F32)


def _dot_exact_rhs(x, sel_stacked, terms=SUM_TERMS):
    return jnp.dot(jnp.concatenate(_split(x, terms), axis=1), sel_stacked, preferred_element_type=F32)


def _iota2(shape, dim):
    return lax.broadcasted_iota(jnp.int32, shape, dim)


def _tri_incl(n):
    return (_iota2((n, n), 0) >= _iota2((n, n), 1))


def _eye(n):
    return jnp.where(_iota2((n, n), 0) == _iota2((n, n), 1), 1.0, 0.0)


def _col_to_row(col):
    n = col.shape[0]
    eye = _iota2((n, n), 0) == _iota2((n, n), 1)
    return jnp.sum(jnp.where(eye, col, 0.0), axis=0, keepdims=True)


def _row_to_col(row):
    n = row.shape[1]
    eye = _iota2((n, n), 0) == _iota2((n, n), 1)
    return jnp.sum(jnp.where(eye, row, 0.0), axis=1, keepdims=True)


def _replicate(x, sel, group):
    hi, mid, lo = _split(x, GATE_COPIES)
    lane = _iota2(x.shape, 1)
    packed = jnp.where(lane < group, hi, jnp.where(lane < 2 * group, mid, lo))
    packed = jnp.where(lane < GATE_COPIES * group, packed, jnp.zeros_like(packed))
    return jnp.dot(packed, sel, preferred_element_type=F32)


def _replicate_selector(group):
    assert group & (group - 1) == 0 and LANES == 128
    k = _iota2((LANES, group * LANES), 0)
    col = _iota2((LANES, group * LANES), 1)
    hit = (k < GATE_COPIES * group) & (jnp.bitwise_and(k, group - 1) == jnp.right_shift(col, 7))
    return jnp.where(hit, 1.0, 0.0).astype(BF16)


def _cummax_rows(x):
    n = x.shape[0]
    row = _iota2(x.shape, 0)
    sh = 1
    while sh < n:
        x = jnp.maximum(x, jnp.where(row >= sh, pltpu.roll(x, sh, 0), -jnp.inf))
        sh *= 2
    return x


def _inv_unit_lower(p, levels, eye):
    n = p.shape[0]
    t = eye + p
    m = 2
    if m < levels:
        p = _dot(p, p)
        yield
    while m < levels:
        m *= 2
        if m >= levels:
            t = t + _dot(p, t)
        elif n % LANES == 0:
            both = _dot(p, jnp.concatenate([p, t], axis=1))
            p, t = both[:, 0:n], t + both[:, n:2 * n]
            yield
        else:
            both = _dot(jnp.concatenate([p, t], axis=0), p)
            p, t = both[0:n, :], t + both[n:2 * n, :]
            yield
    return t


def _run_tasks(tasks, caps):
    done, live, pending = set(), [], list(tasks)
    while pending or live:
        count = {}
        for _, group, _ in live:
            count[group] = count.get(group, 0) + 1
        still = []
        for key, group, deps, make in pending:
            if all(d in done for d in deps) and count.get(group, 0) < caps.get(group, len(tasks)):
                live.append((key, group, make()))
                count[group] = count.get(group, 0) + 1
            else:
                still.append((key, group, deps, make))
        assert live, "task graph has a cycle or a missing dependency"
        pending = still
        nxt = []
        for key, group, gen in live:
            try:
                next(gen)
                nxt.append((key, group, gen))
            except StopIteration:
                done.add(key)
        live = nxt


def _sigmoid(x):
    return 1.0 / (1.0 + jnp.exp(-x))


def _silu(x):
    return x * _sigmoid(x)


def _softplus(x):
    return jnp.maximum(x, 0.0) + jnp.log1p(jnp.exp(-jnp.abs(x)))


def _rms(x, w):
    return x * lax.rsqrt(jnp.mean(x * x, axis=-1, keepdims=True) + EPS) * w


def _resident(shape):
    zeros = (0,) * len(shape)
    return pl.BlockSpec(shape, lambda *_: zeros, pipeline_mode=pl.Buffered(1))


FA_Z = A_QKV
FA_BETA = A_MAIN
FA_DECAY = A_MAIN + LANES
FA_COLS = A_MAIN + 2 * LANES
_FR, _FLW, _FK, _FV, _FA, _FB, _FG = range(7)
FB_COLS = 7 * B_W
SUB_ROWS = 64
PROJ_ROWS = 512
PROJ_MAX_SEQS = 32


def _proj0_kernel(x_ref, conv0_ref, prev_ref, nw_ref, w_ref, convw_ref, alog_ref, dtb_ref,
                  mu_ref, w0_ref, a0_ref, w2a2_ref, kk_ref, ka_ref,
                  fa_ref, fb_ref, convn_ref, ext_a, ext_b, *, ns, tt):
    i = pl.program_id(1)
    tail = SUBLANES
    rows = ns * tt
    lane = _iota2((1, LANES), 1)
    head0 = lane < B_N
    ones_blk = (_iota2((LANES, LANES), 0) // B_N == _iota2((LANES, LANES), 1) // B_N)
    ones_sel = _stack_selector(ones_blk, SUM_TERMS, 0)

    @pl.when(i == 0)
    def _():
        ext_a[:, 0:tail, :] = jnp.zeros((ns, tail, A_QKV), F32)
        ext_a[:, tail - (A_CONV - 1):tail, :] = conv0_ref[...]
        ext_b[:, 0:tail, :] = jnp.zeros((ns, tail, B_COLS), F32)
        ext_b[:, tail - 1:tail, :] = prev_ref[...]

    @pl.when(i > 0)
    def _():
        ext_a[:, 0:tail, :] = ext_a[:, tt:tt + tail, :]
        ext_b[:, 0:tail, :] = ext_b[:, tt:tt + tail, :]

    xn = _rms(x_ref[...].reshape(rows, D_MODEL), nw_ref[...]).astype(BF16)
    piece = 4 * LANES

    def cols(lo, hi):
        return jnp.dot(xn, w_ref[:, lo:hi], preferred_element_type=F32).reshape(ns, tt, hi - lo)

    for lo in range(0, A_QKV, piece):
        ext_a[:, tail:tail + tt, lo:lo + piece] = cols(lo, lo + piece)
    fa_ref[:, :, FA_Z:FA_Z + A_VAL] = cols(A_QKV, A_MAIN)
    g = cols(P0_GATES, P0_B)
    fa_ref[:, :, FA_BETA:FA_BETA + LANES] = _sigmoid(g)
    fa_ref[:, :, FA_DECAY:FA_DECAY + LANES] = -jnp.exp(alog_ref[...]) * _softplus(g + dtb_ref[...])
    for lo in range(0, B_COLS, piece):
        hi = min(lo + piece, B_COLS)
        ext_b[:, tail:tail + tt, lo:hi] = cols(P0_B + lo, P0_B + hi)

    sub_t = min(tt, SUB_ROWS)
    sub_s = min(ns, SUB_ROWS // sub_t)
    cw = convw_ref[...]
    mu = mu_ref[...]
    for s0 in range(0, ns, sub_s):
        sq = slice(s0, s0 + sub_s)
        for t0 in range(0, tt, sub_t):
            tr = slice(t0, t0 + sub_t)
            y = ext_a[sq, tail + t0:tail + t0 + sub_t, :] * cw[A_CONV - 1:A_CONV, :]
            for j in range(A_CONV - 1):
                sh = A_CONV - 1 - j
                y = y + ext_a[sq, tail + t0 - sh:tail + t0 - sh + sub_t, :] * cw[j:j + 1, :]
            y = _silu(y)
            for h in range(A_HEADS):
                q = y[:, :, h * A_DK:(h + 1) * A_DK]
                fa_ref[sq, tr, h * A_DK:(h + 1) * A_DK] = (
                    q * lax.rsqrt(jnp.sum(q * q, axis=-1, keepdims=True) + EPS) * (A_DK ** -0.5))
                k = y[:, :, A_KEY + h * A_DK:A_KEY + (h + 1) * A_DK]
                fa_ref[sq, tr, A_KEY + h * A_DK:A_KEY + (h + 1) * A_DK] = (
                    k * lax.rsqrt(jnp.sum(k * k, axis=-1, keepdims=True) + EPS))
            fa_ref[sq, tr, 2 * A_KEY:A_QKV] = y[:, :, 2 * A_KEY:A_QKV]
            pb = ext_b[sq, tail + t0:tail + t0 + sub_t, :]
            pb_prev = ext_b[sq, tail + t0 - 1:tail + t0 - 1 + sub_t, :]
            pbs = (pb + (pb_prev - pb) * mu).reshape(sub_s * sub_t, B_COLS)
            lo_ = pbs[:, 4 * B_W:B_COLS]
            lo_ = jnp.where(head0, jnp.tanh(lo_), lo_)
            wa = _dot(lo_, w2a2_ref[...])
            logw = -B_DECAY_SCALE * _sigmoid(w0_ref[...] + wa[:, 0:B_W])
            a = _sigmoid(a0_ref[...] + wa[:, B_W:2 * B_W])
            kb = pbs[:, B_W:2 * B_W]
            kkx = kb * kk_ref[...]

            def put(slot, val):
                fb_ref[sq, tr, slot * B_W:(slot + 1) * B_W] = val.reshape(sub_s, sub_t, B_W)

            kkn = jnp.concatenate(
                [kkx[:, p * LANES:(p + 1) * LANES]
                 * lax.rsqrt(_dot_exact_rhs(kkx[:, p * LANES:(p + 1) * LANES] ** 2, ones_sel) + EPS)
                 for p in range(B_PAIRS)], axis=1)
            put(_FR, pbs[:, 0:B_W])
            put(_FLW, logw)
            put(_FK, kb * (1.0 + (a - 1.0) * ka_ref[...]))
            put(_FV, pbs[:, 2 * B_W:3 * B_W])
            put(_FA, -kkn)
            put(_FB, kkn * a)
            put(_FG, pbs[:, 3 * B_W:4 * B_W])

    @pl.when(i == pl.num_programs(1) - 1)
    def _():
        convn_ref[...] = ext_a[:, tt + tail - (A_CONV - 1):tt + tail, :]


def _proj0(x3d, conv0, prev_row, q, ns, tt):
    bsz, seq, _ = x3d.shape
    blk = lambda b, i: (b, i, 0)
    per_b3 = lambda b, i: (b, 0, 0)
    kern = functools.partial(_proj0_kernel, ns=ns, tt=tt)
    return pl.pallas_call(
        kern,
        grid=(bsz // ns, seq // tt),
        in_specs=[pl.BlockSpec((ns, tt, D_MODEL), blk),
                  pl.BlockSpec((ns, A_CONV - 1, A_QKV), per_b3),
                  pl.BlockSpec((ns, 1, B_COLS), per_b3),
                  _resident((1, D_MODEL)),
                  _resident((D_MODEL, P0_PAD)),
                  _resident((A_CONV, A_QKV)), _resident((1, LANES)), _resident((1, LANES)),
                  _resident((1, B_COLS)), _resident((1, B_W)), _resident((1, B_W)),
                  _resident((2 * B_LORA, 2 * B_W)), _resident((1, B_W)), _resident((1, B_W))],
        out_specs=[pl.BlockSpec((ns, tt, FA_COLS), blk),
                   pl.BlockSpec((ns, tt, FB_COLS), blk),
                   pl.BlockSpec((ns, A_CONV - 1, A_QKV), per_b3)],
        out_shape=[jax.ShapeDtypeStruct((bsz, seq, FA_COLS), F32),
                   jax.ShapeDtypeStruct((bsz, seq, FB_COLS), F32),
                   jax.ShapeDtypeStruct((bsz, A_CONV - 1, A_QKV), F32)],
        scratch_shapes=[pltpu.VMEM((ns, tt + SUBLANES, A_QKV), F32),
                        pltpu.VMEM((ns, tt + SUBLANES, B_COLS), F32)],
        compiler_params=pltpu.CompilerParams(dimension_semantics=("parallel", "arbitrary"),
                                             vmem_limit_bytes=VMEM_LIMIT),
        name="proj0_features",
    )(x3d, conv0, prev_row, q["norm0"], q["w0"], q["conv_w"], q["alog"], q["dtb"],
      q["mu"], q["rw0"], q["ra0"], q["w2a2"], q["k_k"], q["k_a"])


def _aux_kernel(xl_ref, xp_ref, nw_ref, wb_ref, xn_ref, prev_ref):
    xn_ref[...] = _rms(xl_ref[...], nw_ref[...])
    prev_ref[...] = jnp.dot(xp_ref[...].astype(BF16), wb_ref[...], preferred_element_type=F32)


def _aux(x_last, x_prev, norm_w, wb):
    b = x_last.shape[0]
    return pl.pallas_call(
        _aux_kernel,
        out_shape=[jax.ShapeDtypeStruct((b, D_MODEL), F32), jax.ShapeDtypeStruct((b, B_COLS), F32)],
        compiler_params=pltpu.CompilerParams(vmem_limit_bytes=VMEM_LIMIT),
        name="shift_rows",
    )(x_last, x_prev, norm_w, wb)


def _layer0_kernel(fa_ref, fb_ref, sg0_ref, sr0_ref, gnw_ref, rk_ref, lnw_ref, lnb_ref,
                   o_ref, sgout_ref, srout_ref,
                   stg_s, uw_s, attn_s, qg_s, kd_s, ge_s, str_s, tmp_s, *, nb, tb, chunk):
    i = pl.program_id(1)
    n_chunks = tb // chunk
    lane = _iota2((1, LANES), 1)
    head0 = lane < B_N
    ones_blk = (_iota2((LANES, LANES), 0) // B_N == _iota2((LANES, LANES), 1) // B_N)

    @pl.when(i == 0)
    def _():
        for bi in range(nb):
            for p in range(B_PAIRS):
                tmp_s[...] = jnp.zeros((B_N, LANES), F32)
                tmp_s[:, 0:B_N] = sr0_ref[bi, 2 * p + 1]
                shifted = pltpu.roll(tmp_s[...], B_N, 1)
                str_s[bi, p, B_N:LANES, :] = jnp.where(head0, 0.0, shifted)
                tmp_s[:, 0:B_N] = sr0_ref[bi, 2 * p]
                str_s[bi, p, 0:B_N, :] = tmp_s[...]
        stg_s[...] = sg0_ref[...]

    tri = _tri_incl(chunk)
    tri_sel = _stack_selector(tri, CUMSUM_TERMS, 1)
    ones_sel = _stack_selector(ones_blk, SUM_TERMS, 0)
    eye_c = _eye(chunk)
    eye_2c = _eye(2 * chunk)

    n_gates = 2 * A_HEADS
    is_decay = jnp.bitwise_and(lane, n_gates - 1) >= A_HEADS
    sel = _replicate_selector(n_gates)
    strict = _iota2((chunk, chunk), 0) > _iota2((chunk, chunk), 1)
    gnw = gnw_ref[...]
    qkv_s = fa_ref
    rep = {}

    def gdn_gates(bi, c):
        rows = slice(c * chunk, (c + 1) * chunk)
        gcum = _dot_exact_lhs(tri_sel, fa_ref[bi, rows, FA_DECAY:FA_DECAY + LANES])
        yield
        rep[bi, c] = _replicate(jnp.where(is_decay, gcum, fa_ref[bi, rows, FA_BETA:FA_BETA + LANES]),
                                sel, n_gates)

    def gdn_local(bi, c, h):
        rows = slice(c * chunk, (c + 1) * chunk)
        q = qkv_s[bi, rows, h * A_DK:(h + 1) * A_DK]
        k = qkv_s[bi, rows, A_KEY + h * A_DK:A_KEY + (h + 1) * A_DK]
        v = qkv_s[bi, rows, 2 * A_KEY + h * A_DV:2 * A_KEY + (h + 1) * A_DV]
        qk_k = _dot_nt(jnp.concatenate([q, k], axis=0), k)
        yield
        b_rep = rep[bi, c][:, h * LANES:(h + 1) * LANES]
        g_rep = rep[bi, c][:, (A_HEADS + h) * LANES:(A_HEADS + h + 1) * LANES]
        g_row = _col_to_row(g_rep[:, 0:chunk])
        g_last = g_rep[chunk - 1:chunk, :]
        decay = jnp.where(tri, jnp.exp(jnp.where(tri, g_rep[:, 0:chunk] - g_row, 0.0)), 0.0)
        eg = jnp.exp(g_rep)
        qg_s[bi, rows, h * A_DK:(h + 1) * A_DK] = q * eg
        kd_s[bi, rows, h * A_DK:(h + 1) * A_DK] = k * jnp.exp(g_last - g_rep)
        ge_s[bi, c, h:h + 1, :] = jnp.exp(g_last)
        attn_s[bi, rows, h * LANES:h * LANES + chunk] = decay * qk_k[0:chunk, :]
        amat = jnp.where(strict, b_rep[:, 0:chunk] * decay * qk_k[chunk:2 * chunk, :], 0.0)
        t_inv = yield from _inv_unit_lower(-amat, chunk, eye_c)
        uw_s[bi, rows, 2 * h * LANES:2 * (h + 1) * LANES] = _dot(
            t_inv, jnp.concatenate([b_rep * v, (b_rep * eg) * k], axis=1))

    def gdn_state(bi, c, h):
        rows = slice(c * chunk, (c + 1) * chunk)
        uw = uw_s[bi, rows, 2 * h * LANES:2 * (h + 1) * LANES]
        s = stg_s[bi, h]
        wq_s = _dot(jnp.concatenate([uw[:, A_DV:A_DV + A_DK], qg_s[bi, rows, h * A_DK:(h + 1) * A_DK]], axis=0), s)
        yield
        nv = uw[:, 0:A_DV] - wq_s[0:chunk, :]
        o = wq_s[chunk:2 * chunk, :] + _dot(attn_s[bi, rows, h * LANES:h * LANES + chunk], nv)
        stg_s[bi, h] = ge_s[bi, c, h:h + 1, :] * s + _dot_tn(kd_s[bi, rows, h * A_DK:(h + 1) * A_DK], nv)
        yield
        z = fa_ref[bi, rows, FA_Z + h * A_DV:FA_Z + (h + 1) * A_DV]
        o_ref[bi, rows, h * A_DV:(h + 1) * A_DV] = _rms(o, gnw) * _silu(z)

    feat_s = fb_ref
    c2 = 2 * chunk
    ri = _iota2((c2, c2), 0)
    ci = _iota2((c2, c2), 1)
    same_head = (ri // chunk) == (ci // chunk)
    incl_bd = same_head & (ri >= ci)
    strict_bd = same_head & (ri > ci)
    ri4 = _iota2((2 * c2, c2), 0)
    ci4 = _iota2((2 * c2, c2), 1)
    rr4 = ri4 % c2
    lm_mask = ((rr4 // chunk) == (ci4 // chunk)) & ((rr4 > ci4) | ((ri4 >= c2) & (rr4 == ci4)))
    lane_st = (_iota2((c2, LANES), 0) // chunk) == (_iota2((c2, LANES), 1) // B_N)
    local = {}

    def rwkv_local(bi, c, p):
        rows = slice(c * chunk, (c + 1) * chunk)

        def feat(slot):
            return feat_s[bi, rows, slot * B_W + p * LANES:slot * B_W + (p + 1) * LANES]
        r, lw, k, v, a_, b_ = (feat(s) for s in (_FR, _FLW, _FK, _FV, _FA, _FB))
        cs = _dot_exact_lhs(tri_sel, lw)
        yield
        cs_last = cs[chunk - 1:chunk, :]
        e_inv = jnp.exp(-cs)
        e_rem = jnp.exp(cs_last - cs)
        at = a_ * jnp.exp(cs - lw)
        rt = r * jnp.exp(cs)
        bt = b_ * e_inv
        kt = k * e_inv
        a_st = jnp.concatenate([jnp.where(head0, at, 0.0), jnp.where(head0, 0.0, at)], axis=0)
        r_st = jnp.concatenate([jnp.where(head0, rt, 0.0), jnp.where(head0, 0.0, rt)], axis=0)
        b2 = jnp.concatenate([bt, bt], axis=0)
        k2 = jnp.concatenate([kt, kt], axis=0)
        v2 = jnp.concatenate([v, v], axis=0)
        ar_st = jnp.concatenate([a_st, r_st], axis=0)
        if c2 % LANES == 0:
            ar_bk = _dot_nt(ar_st, jnp.concatenate([b2, k2], axis=0))
            ar_b, ar_k = ar_bk[:, 0:c2], ar_bk[:, c2:2 * c2]
        else:
            ar_b = _dot_nt(ar_st, b2)
            ar_k = _dot_nt(ar_st, k2)
        yield
        l_ab = jnp.where(strict_bd, ar_b[0:c2, :], 0.0)
        m_rb = jnp.where(incl_bd, ar_b[c2:2 * c2, :], 0.0)
        lm_k = jnp.where(lm_mask, ar_k, 0.0)
        lm_v = _dot(lm_k, v2)
        sl = slice(p * LANES, (p + 1) * LANES)
        bonus = _dot_exact_rhs(r * k * rk_ref[:, sl], ones_sel) * v
        t_inv = yield from _inv_unit_lower(l_ab, chunk, eye_2c)
        local[bi, c, p] = (t_inv, lm_v, m_rb, jnp.concatenate([at, rt], axis=0),
                           jnp.concatenate([b_ * e_rem, k * e_rem], axis=0), jnp.exp(cs_last), bonus)

    def rwkv_state(bi, c, p):
        rows = slice(c * chunk, (c + 1) * chunk)
        sl = slice(p * LANES, (p + 1) * LANES)
        t_inv, lm_v, m_rb, at_rt, bk_rem, decay_end, bonus = local.pop((bi, c, p))
        v = feat_s[bi, rows, _FV * B_W + p * LANES:_FV * B_W + (p + 1) * LANES]
        gate = feat_s[bi, rows, _FG * B_W + p * LANES:_FG * B_W + (p + 1) * LANES]
        s = str_s[bi, p]
        ar = _dot_nt(at_rt, s)
        yield
        ah = ar[0:chunk, :]
        rh = ar[chunk:c2, :]
        x_st = jnp.where(lane_st, jnp.concatenate([ah, ah], axis=0) + lm_v[0:c2, :], 0.0)
        u_st = _dot(t_inv, x_st)
        yield
        o_st = jnp.where(lane_st, _dot(m_rb, u_st) + lm_v[c2:2 * c2, :], 0.0)
        u = u_st[0:chunk, :] + u_st[chunk:c2, :]
        upd = _dot_tn(jnp.concatenate([u, v], axis=0), bk_rem)
        yield
        o = rh + o_st[0:chunk, :] + o_st[chunk:c2, :]
        str_s[bi, p] = s * decay_end + jnp.where(ones_blk, upd, 0.0)
        mean = _dot_exact_rhs(o, ones_sel) * (1.0 / B_N)
        yield
        d = o - mean
        var = _dot_exact_rhs(d * d, ones_sel) * (1.0 / B_N)
        yield
        ob = d * lax.rsqrt(var + B_GN_EPS) * lnw_ref[:, sl] + lnb_ref[:, sl] + bonus
        o_ref[bi, rows, A_VAL + p * LANES:A_VAL + (p + 1) * LANES] = ob * _silu(gate)

    tasks = []
    blocks = [(bi, c) for c in range(n_chunks) for bi in range(nb)]
    for bi, c in blocks:
        for p in range(B_PAIRS):
            deps = [("rl", bi, c, p)] + ([("rs", bi, c - 1, p)] if c else [])
            tasks.append((("rs", bi, c, p), "rwkv_state", deps, functools.partial(rwkv_state, bi, c, p)))
        for h in range(A_HEADS):
            deps = [("gl", bi, c, h)] + ([("gs", bi, c - 1, h)] if c else [])
            tasks.append((("gs", bi, c, h), "gdn_state", deps, functools.partial(gdn_state, bi, c, h)))
    for bi, c in blocks:
        for p in range(B_PAIRS):
            tasks.append((("rl", bi, c, p), "rwkv_local", [], functools.partial(rwkv_local, bi, c, p)))
        for h in range(A_HEADS):
            tasks.append((("gl", bi, c, h), "gdn_local", [("gg", bi, c)], functools.partial(gdn_local, bi, c, h)))
    for bi, c in blocks:
        tasks.append((("gg", bi, c), "gdn_gates", [], functools.partial(gdn_gates, bi, c)))
    _run_tasks(tasks, {"gdn_gates": 2, "rwkv_local": 2 * B_PAIRS, "gdn_local": 2 * A_HEADS})

    @pl.when(i == pl.num_programs(1) - 1)
    def _():
        for bi in range(nb):
            for p in range(B_PAIRS):
                srout_ref[bi, 2 * p] = str_s[bi, p, 0:B_N, 0:B_N]
                shifted = pltpu.roll(str_s[bi, p, B_N:LANES, :], B_N, 1)
                srout_ref[bi, 2 * p + 1] = shifted[:, 0:B_N]
        sgout_ref[...] = stg_s[...]


def _layer0_mixers(fa, fb, s_gdn, s_rwkv, q, nb, tb, chunk):
    bsz, seq, _ = fa.shape
    blk = lambda b, i: (b, i, 0)
    per_b4 = lambda b, i: (b, 0, 0, 0)
    fix2 = lambda b, i: (0, 0)
    kern = functools.partial(_layer0_kernel, nb=nb, tb=tb, chunk=chunk)
    vec = pl.BlockSpec((1, B_W), fix2)
    return pl.pallas_call(
        kern,
        grid=(bsz // nb, seq // tb),
        in_specs=[pl.BlockSpec((nb, tb, FA_COLS), blk),
                  pl.BlockSpec((nb, tb, FB_COLS), blk),
                  pl.BlockSpec((nb, A_HEADS, A_DK, A_DV), per_b4),
                  pl.BlockSpec((nb, B_HEADS, B_N, B_N), per_b4),
                  pl.BlockSpec((1, A_DV), fix2), vec, vec, vec],
        out_specs=[pl.BlockSpec((nb, tb, MIX0), blk),
                   pl.BlockSpec((nb, A_HEADS, A_DK, A_DV), per_b4),
                   pl.BlockSpec((nb, B_HEADS, B_N, B_N), per_b4)],
        out_shape=[jax.ShapeDtypeStruct((bsz, seq, MIX0), F32),
                   jax.ShapeDtypeStruct((bsz, A_HEADS, A_DK, A_DV), F32),
                   jax.ShapeDtypeStruct((bsz, B_HEADS, B_N, B_N), F32)],
        scratch_shapes=[pltpu.VMEM((nb, A_HEADS, A_DK, A_DV), F32),
                        pltpu.VMEM((nb, tb, A_HEADS * (A_DV + A_DK)), F32),
                        pltpu.VMEM((nb, tb, A_HEADS * LANES), F32),
                        pltpu.VMEM((nb, tb, A_KEY), F32),
                        pltpu.VMEM((nb, tb, A_KEY), F32),
                        pltpu.VMEM((nb, tb // chunk, SUBLANES, LANES), F32),
                        pltpu.VMEM((nb, B_PAIRS, LANES, LANES), F32),
                        pltpu.VMEM((B_N, LANES), F32)],
        compiler_params=pltpu.CompilerParams(dimension_semantics=("parallel", "arbitrary"),
                                             vmem_limit_bytes=VMEM_LIMIT),
        name="layer0_mixers",
    )(fa, fb, s_gdn, s_rwkv, q["gnw"], q["r_k"], q["ln_w"], q["ln_b"])


def _layer1_kernel(x_ref, mix_ref, c0_ref, n0_ref, m0_ref, wo0_ref, n1_ref, w1_ref, bias_ref, wo1_ref, nf_ref,
                   y_ref, cout_ref, nout_ref, mout_ref,
                   proj_s, x1_s, h_s, gsc_s, caug_s, m_s, *, nb, tb, chunk):
    i = pl.program_id(1)
    n_chunks = tb // chunk
    lane = _iota2((1, LANES), 1)
    krow = _iota2((LANES, 1), 0)
    n_gates = 2 * C_HEADS
    is_f = jnp.bitwise_and(lane, n_gates - 1) >= C_HEADS

    @pl.when(i == 0)
    def _():
        for bi in range(nb):
            m0 = m0_ref[bi]
            for h in range(C_HEADS):
                j = h % 2
                caug_s[bi, h] = jnp.zeros((LANES, 2 * C_DV), F32)
                caug_s[bi, h, j * C_DQK:(j + 1) * C_DQK, 0:C_DV] = c0_ref[bi, h]
                n_col = _row_to_col(n0_ref[bi, h // 2:h // 2 + 1, :])
                own = (krow // C_DQK) == j
                caug_s[bi, h, :, C_DV:2 * C_DV] = jnp.where(own, jnp.broadcast_to(n_col, (LANES, C_DV)), 0.0)
                m_s[bi, h:h + 1, :] = jnp.broadcast_to(m0[:, h:h + 1], (1, LANES))

    tri = _tri_incl(chunk)
    tri_sel = _stack_selector(tri, CUMSUM_TERMS, 1)
    sel = _replicate_selector(n_gates)
    ones_blk = jnp.ones((chunk, C_DV), F32)
    off = 2 * C_QK
    piece = 4 * LANES
    rep = {}

    def stack(ref, rows):
        return jnp.concatenate([ref[bi, rows, :] for bi in range(nb)], axis=0)

    def unstack(ref, rows, cols, val):
        for bi in range(nb):
            ref[bi, rows, cols] = val[bi * chunk:(bi + 1) * chunk, :]

    def proj_task(c):
        rows = slice(c * chunk, (c + 1) * chunk)
        x1 = stack(x_ref, rows) + jnp.dot(stack(mix_ref, rows).astype(BF16), wo0_ref[...],
                                          preferred_element_type=F32)
        unstack(x1_s, rows, slice(0, D_MODEL), x1)
        yield
        xn = _rms(x1, n1_ref[...]).astype(BF16)
        g = jnp.dot(xn, w1_ref[:, P1_GATES:P1_PAD], preferred_element_type=F32) + bias_ref[...]
        cap = C_GATE_CAP * jnp.tanh(g / C_GATE_CAP)
        unstack(gsc_s, rows, slice(0, LANES), jnp.where(is_f, -_softplus(-cap), cap))
        yield
        for j in range(0, P1_GATES, piece):
            unstack(proj_s, rows, slice(j, j + piece),
                    jnp.dot(xn, w1_ref[:, j:j + piece], preferred_element_type=F32))
            yield

    def gate_task(bi, c):
        rows = slice(c * chunk, (c + 1) * chunk)
        gl = gsc_s[bi, rows, :]
        bcum = _dot_exact_lhs(tri_sel, gl)
        yield
        rep[bi, c] = _replicate(jnp.where(is_f, bcum, gl), sel, n_gates)

    def head_chain(bi, c, h):
        rows = slice(c * chunk, (c + 1) * chunk)
        p, j = h // 2, h % 2
        own_lane = (lane // C_DQK) == j
        own_row = (krow // C_DQK) == j
        qp = proj_s[bi, rows, p * LANES:(p + 1) * LANES] * (C_DQK ** -0.5)
        kp = proj_s[bi, rows, C_QK + p * LANES:C_QK + (p + 1) * LANES]
        v = proj_s[bi, rows, off + h * C_DV:off + (h + 1) * C_DV]
        vaug = jnp.concatenate([v, ones_blk], axis=1)
        qk = _dot_nt(jnp.where(own_lane, qp, 0.0), kp)
        cm = caug_s[bi, h]
        qc = _dot(qp, cm)
        i_rep = rep[bi, c][:, h * LANES:(h + 1) * LANES]
        b_rep = rep[bi, c][:, (C_HEADS + h) * LANES:(C_HEADS + h + 1) * LANES]
        b_row = _col_to_row(b_rep[:, 0:chunk])
        i_row = _col_to_row(i_rep[:, 0:chunk])
        dmat = jnp.where(tri, b_rep[:, 0:chunk] - b_row + i_row, -jnp.inf)
        dmax = b_rep + _cummax_rows(i_rep - b_rep)
        m = m_s[bi, h:h + 1, :]
        inter = b_rep + m
        mt = jnp.maximum(inter, dmax)
        m_new = mt[chunk - 1:chunk, :]
        b_last = b_rep[chunk - 1:chunk, :]
        w_s = jnp.exp(b_last - b_rep + i_rep - m_new)
        e_c = jnp.exp(b_last + m - m_new)
        e_inter = jnp.exp(inter - mt)
        upd = _dot_tn(w_s * kp, vaug)
        yield
        pm = jnp.exp(dmat - mt[:, 0:chunk]) * qk
        pv = _dot(pm, vaug)
        caug_s[bi, h] = jnp.concatenate([e_c, e_c], axis=1) * cm + jnp.where(own_row, upd, 0.0)
        m_s[bi, h:h + 1, :] = m_new
        yield
        num = e_inter * qc[:, 0:C_DV] + pv[:, 0:C_DV]
        den = e_inter * qc[:, C_DV:2 * C_DV] + pv[:, C_DV:2 * C_DV]
        hh = num / jnp.maximum(jnp.abs(den), jnp.exp(-mt))
        og = proj_s[bi, rows, off + C_V + h * C_DV:off + C_V + (h + 1) * C_DV]
        z = proj_s[bi, rows, off + 2 * C_V + h * C_DV:off + 2 * C_V + (h + 1) * C_DV]
        h_s[bi, rows, h * C_DV:(h + 1) * C_DV] = hh * _sigmoid(og) * _silu(z)

    def final_task(c):
        rows = slice(c * chunk, (c + 1) * chunk)
        x2 = stack(x1_s, rows) + jnp.dot(stack(h_s, rows).astype(BF16), wo1_ref[...],
                                         preferred_element_type=F32)
        yield
        unstack(y_ref, rows, slice(0, D_MODEL), _rms(x2, nf_ref[...]))

    tasks = []
    for c in range(n_chunks):
        tasks.append((("p", c), "proj", [], functools.partial(proj_task, c)))
    for c in range(n_chunks):
        heads = [("h", bi, c, h) for bi in range(nb) for h in range(C_HEADS)]
        tasks.append((("f", c), "final", heads, functools.partial(final_task, c)))
        for bi in range(nb):
            tasks.append((("g", bi, c), "gates", [("p", c)], functools.partial(gate_task, bi, c)))
            for h in range(C_HEADS):
                deps = [("p", c), ("g", bi, c)] + ([("h", bi, c - 1, h)] if c else [])
                tasks.append((("h", bi, c, h), "heads", deps, functools.partial(head_chain, bi, c, h)))
    _run_tasks(tasks, {"proj": 1, "final": 1})

    @pl.when(i == pl.num_programs(1) - 1)
    def _():
        for bi in range(nb):
            for h in range(C_HEADS):
                j = h % 2
                cout_ref[bi, h] = caug_s[bi, h, j * C_DQK:(j + 1) * C_DQK, 0:C_DV]
            for p in range(C_HEADS // 2):
                n_col = caug_s[bi, 2 * p, :, C_DV:C_DV + 1] + caug_s[bi, 2 * p + 1, :, C_DV:C_DV + 1]
                nout_ref[bi, p:p + 1, :] = _col_to_row(n_col)
        mout_ref[...] = m_s[...]


def _layer1(x3d, mix3d, c0, n0, m0, q, nb, tb, chunk):
    bsz, seq, _ = x3d.shape
    blk = lambda b, i: (b, i, 0)
    per_b3 = lambda b, i: (b, 0, 0)
    per_b4 = lambda b, i: (b, 0, 0, 0)
    kern = functools.partial(_layer1_kernel, nb=nb, tb=tb, chunk=chunk)
    pairs = C_HEADS // 2
    return pl.pallas_call(
        kern,
        grid=(bsz // nb, seq // tb),
        in_specs=[pl.BlockSpec((nb, tb, D_MODEL), blk),
                  pl.BlockSpec((nb, tb, MIX0), blk),
                  pl.BlockSpec((nb, C_HEADS, C_DQK, C_DV), per_b4),
                  pl.BlockSpec((nb, pairs, LANES), per_b3),
                  pl.BlockSpec((nb, 1, C_HEADS), per_b3),
                  _resident((MIX0, D_MODEL)),
                  _resident((1, D_MODEL)),
                  _resident((D_MODEL, P1_PAD)),
                  _resident((1, LANES)),
                  _resident((C_V, D_MODEL)),
                  _resident((1, D_MODEL))],
        out_specs=[pl.BlockSpec((nb, tb, D_MODEL), blk),
                   pl.BlockSpec((nb, C_HEADS, C_DQK, C_DV), per_b4),
                   pl.BlockSpec((nb, pairs, LANES), per_b3),
                   pl.BlockSpec((nb, C_HEADS, LANES), per_b3)],
        out_shape=[jax.ShapeDtypeStruct((bsz, seq, D_MODEL), F32),
                   jax.ShapeDtypeStruct((bsz, C_HEADS, C_DQK, C_DV), F32),
                   jax.ShapeDtypeStruct((bsz, pairs, LANES), F32),
                   jax.ShapeDtypeStruct((bsz, C_HEADS, LANES), F32)],
        scratch_shapes=[pltpu.VMEM((nb, tb, P1_PAD), F32),
                        pltpu.VMEM((nb, tb, D_MODEL), F32),
                        pltpu.VMEM((nb, tb, C_V), F32),
                        pltpu.VMEM((nb, tb, LANES), F32),
                        pltpu.VMEM((nb, C_HEADS, LANES, 2 * C_DV), F32),
                        pltpu.VMEM((nb, C_HEADS, LANES), F32)],
        compiler_params=pltpu.CompilerParams(dimension_semantics=("parallel", "arbitrary"),
                                             vmem_limit_bytes=VMEM_LIMIT),
        name="layer1",
    )(x3d, mix3d, c0, n0, m0, q["wo0"], q["norm1"], q["w1"], q["gate_bias"], q["wo1"], q["normf"])


def _gate_tile(cols):
    tiled = jnp.concatenate([cols.astype(F32)] * GATE_COPIES, axis=1)
    return jnp.pad(tiled, ((0, 0), (0, LANES - tiled.shape[1])))


def _prepare_params(p):
    w_in0 = p["w_in0"]
    a_cols = A_MAIN + 2 * A_HEADS
    w0 = jnp.concatenate([w_in0[:, 0:A_MAIN], _gate_tile(w_in0[:, A_MAIN:a_cols]), w_in0[:, a_cols:]], axis=1)
    w_in1 = p["w_in1"]
    w1 = jnp.concatenate([w_in1[:, 0:P1_GATES], _gate_tile(w_in1[:, P1_GATES:])], axis=1)
    zeros_h = jnp.zeros((A_HEADS,), F32)
    alog = _gate_tile(jnp.concatenate([zeros_h, p["gdn_a_log"]]).reshape(1, -1))
    dtb = _gate_tile(jnp.concatenate([zeros_h, p["gdn_dt_bias"]]).reshape(1, -1))
    gate_bias = _gate_tile(jnp.concatenate([p["mlstm_b_i"], p["mlstm_b_f"]]).reshape(1, -1))
    w2a2 = jnp.zeros((2 * B_LORA, 2 * B_W), F32)
    w2a2 = w2a2.at[0:B_LORA, 0:B_W].set(p["rwkv_w2"]).at[B_LORA:, B_W:].set(p["rwkv_a2"])
    row = lambda v: v.reshape(1, -1).astype(F32)
    return dict(
        w0=w0.astype(BF16), w1=w1.astype(BF16), wb=w_in0[:, a_cols:].astype(BF16),
        wo0=p["w_out0"].astype(BF16), wo1=p["w_out1"].astype(BF16), w2a2=w2a2.astype(BF16),
        norm0=row(p["norm0_w"]), norm1=row(p["norm1_w"]), normf=row(p["norm_f_w"]),
        conv_w=p["gdn_conv_w"].astype(F32), alog=alog, dtb=dtb, gnw=row(p["gdn_norm_w"]),
        mu=row(p["rwkv_mu"]), rw0=row(p["rwkv_w0"]), ra0=row(p["rwkv_a0"]),
        k_k=row(p["rwkv_k_k"]), k_a=row(p["rwkv_k_a"]), r_k=row(p["rwkv_r_k"]),
        ln_w=row(p["rwkv_ln_w"]), ln_b=row(p["rwkv_ln_b"]), gate_bias=gate_bias,
    )


def _tiles(bsz, seq):
    chunk = math.gcd(seq, MAX_CHUNK)
    tb = min(seq, 256)
    nb = 2 if seq > MAX_CHUNK else 8
    nb = math.gcd(nb, bsz)
    tt = min(seq, PROJ_ROWS)
    ns = math.gcd(min(PROJ_ROWS // tt, PROJ_MAX_SEQS), bsz)
    return chunk, tb, nb, tt, ns


def _trunk(x, state, q):
    conv0, s_gdn, x_prev, s_rwkv, c0, n0, m0 = state
    bsz, seq, _ = x.shape
    chunk, tb, nb, tt, ns = _tiles(bsz, seq)
    xn_last, prev_row = _aux(x[:, -1, :], x_prev, q["norm0"], q["wb"])
    fa, fb, conv_new = _proj0(x, conv0, prev_row.reshape(bsz, 1, B_COLS), q, ns, tt)
    mix, s_gdn_new, s_rwkv_new = _layer0_mixers(fa, fb, s_gdn, s_rwkv, q, nb, tb, chunk)
    y, c_new, n_new, m_new = _layer1(x, mix, c0, n0.reshape(bsz, C_HEADS // 2, LANES),
                                     m0.reshape(bsz, 1, C_HEADS), q, nb, tb, chunk)
    return y, (conv_new, s_gdn_new, xn_last, s_rwkv_new, c_new,
               n_new.reshape(bsz, C_HEADS, C_DQK), m_new[:, :, 0])


def kernel(x_prompt, x_sample, state_gdn_conv, state_gdn, state_rwkv_shift, state_rwkv,
           state_mlstm_c, state_mlstm_n, state_mlstm_m,
           norm0_w, w_in0, w_out0, gdn_conv_w, gdn_a_log, gdn_dt_bias, gdn_norm_w,
           rwkv_mu, rwkv_w0, rwkv_w2, rwkv_a0, rwkv_a2, rwkv_k_k, rwkv_k_a, rwkv_r_k,
           rwkv_ln_w, rwkv_ln_b, norm1_w, w_in1, w_out1, mlstm_b_i, mlstm_b_f, norm_f_w):
    q = _prepare_params(dict(
        norm0_w=norm0_w, w_in0=w_in0, w_out0=w_out0, gdn_conv_w=gdn_conv_w, gdn_a_log=gdn_a_log,
        gdn_dt_bias=gdn_dt_bias, gdn_norm_w=gdn_norm_w, rwkv_mu=rwkv_mu, rwkv_w0=rwkv_w0,
        rwkv_w2=rwkv_w2, rwkv_a0=rwkv_a0, rwkv_a2=rwkv_a2, rwkv_k_k=rwkv_k_k, rwkv_k_a=rwkv_k_a,
        rwkv_r_k=rwkv_r_k, rwkv_ln_w=rwkv_ln_w, rwkv_ln_b=rwkv_ln_b, norm1_w=norm1_w, w_in1=w_in1,
        w_out1=w_out1, mlstm_b_i=mlstm_b_i, mlstm_b_f=mlstm_b_f, norm_f_w=norm_f_w))
    bsz = x_prompt.shape[0]
    prompt_state = (
        jnp.zeros((bsz, A_CONV - 1, A_QKV), F32),
        jnp.zeros((bsz, A_HEADS, A_DK, A_DV), F32),
        jnp.zeros((bsz, D_MODEL), F32),
        jnp.zeros((bsz, B_HEADS, B_N, B_N), F32),
        jnp.zeros((bsz, C_HEADS, C_DQK, C_DV), F32),
        jnp.zeros((bsz, C_HEADS, C_DQK), F32),
        jnp.zeros((bsz, C_HEADS), F32),
    )
    y_p, sp = _trunk(x_prompt, prompt_state, q)
    sample_state = (state_gdn_conv, state_gdn, state_rwkv_shift, state_rwkv,
                    state_mlstm_c, state_mlstm_n, state_mlstm_m)
    y_s, ss = _trunk(x_sample, sample_state, q)
    return (y_p, y_s) + tuple(sp) + tuple(ss)
```

```python
import functools
import math

import jax
import jax.numpy as jnp
from jax import lax
from jax.experimental import pallas as pl
from jax.experimental.pallas import tpu as pltpu

F32 = jnp.float32
BF16 = jnp.bfloat16

D_MODEL = 1024
EPS = 1e-6
LANES = 128
SUBLANES = 8
A_HEADS, A_DK, A_DV, A_CONV = 4, 128, 128, 4
A_KEY = A_HEADS * A_DK
A_VAL = A_HEADS * A_DV
A_QKV = 2 * A_KEY + A_VAL
A_MAIN = A_QKV + A_VAL
B_HEADS, B_N, B_LORA = 8, 64, 64
B_W = B_HEADS * B_N
B_COLS = 4 * B_W + 2 * B_LORA
B_PAIRS = B_HEADS // 2
B_GN_EPS = 64e-5
B_DECAY_SCALE = math.exp(-0.5)
MIX0 = A_VAL + B_W
C_HEADS, C_DQK, C_DV = 8, 64, 128
C_QK = C_HEADS * C_DQK
C_V = C_HEADS * C_DV
C_GATE_CAP = 15.0
P0_GATES = A_MAIN
P0_B = A_MAIN + LANES
P0_PAD = P0_B + B_COLS
P1_GATES = 2 * C_QK + 3 * C_V
P1_PAD = P1_GATES + LANES
MAX_CHUNK = 64
GATE_COPIES = 3
CUMSUM_TERMS = 3
SUM_TERMS = 2
VMEM_LIMIT = 56 * 1024 * 1024


def _dot(a, b):
    return jnp.dot(a.astype(BF16), b.astype(BF16), preferred_element_type=F32)


def _dot_nt(a, b):
    return lax.dot_general(a.astype(BF16), b.astype(BF16), (((1,), (1,)), ((), ())),
                           preferred_element_type=F32)


def _dot_tn(a, b):
    return lax.dot_general(a.astype(BF16), b.astype(BF16), (((0,), (0,)), ((), ())),
                           preferred_element_type=F32)


def _split(x, terms):
    out = []
    for _ in range(terms - 1):
        hi = x.astype(BF16)
        out.append(hi)
        x = x - hi.astype(F32)
    out.append(x.astype(BF16))
    return out


def _stack_selector(sel, terms, axis):
    s = jnp.where(sel, 1.0, 0.0).astype(BF16)
    return jnp.concatenate([s] * terms, axis=axis)


def _dot_exact_lhs(sel_stacked, x, terms=CUMSUM_TERMS):
    return jnp.dot(sel_stacked, jnp.concatenate(_split(x, terms), axis=0), preferred_element_type=F32)


def _dot_exact_rhs(x, sel_stacked, terms=SUM_TERMS):
    return jnp.dot(jnp.concatenate(_split(x, terms), axis=1), sel_stacked, preferred_element_type=F32)


def _iota2(shape, dim):
    return lax.broadcasted_iota(jnp.int32, shape, dim)


def _tri_incl(n):
    return (_iota2((n, n), 0) >= _iota2((n, n), 1))


def _eye(n):
    return jnp.where(_iota2((n, n), 0) == _iota2((n, n), 1), 1.0, 0.0)


def _col_to_row(col):
    n = col.shape[0]
    eye = _iota2((n, n), 0) == _iota2((n, n), 1)
    return jnp.sum(jnp.where(eye, col, 0.0), axis=0, keepdims=True)


def _row_to_col(row):
    n = row.shape[1]
    eye = _iota2((n, n), 0) == _iota2((n, n), 1)
    return jnp.sum(jnp.where(eye, row, 0.0), axis=1, keepdims=True)


def _replicate(x, sel, group):
    hi, mid, lo = _split(x, GATE_COPIES)
    lane = _iota2(x.shape, 1)
    packed = jnp.where(lane < group, hi, jnp.where(lane < 2 * group, mid, lo))
    packed = jnp.where(lane < GATE_COPIES * group, packed, jnp.zeros_like(packed))
    return jnp.dot(packed, sel, preferred_element_type=F32)


def _replicate_selector(group):
    assert group & (group - 1) == 0 and LANES == 128
    k = _iota2((LANES, group * LANES), 0)
    col = _iota2((LANES, group * LANES), 1)
    hit = (k < GATE_COPIES * group) & (jnp.bitwise_and(k, group - 1) == jnp.right_shift(col, 7))
    return jnp.where(hit, 1.0, 0.0).astype(BF16)


def _cummax_rows(x):
    n = x.shape[0]
    row = _iota2(x.shape, 0)
    sh = 1
    while sh < n:
        x = jnp.maximum(x, jnp.where(row >= sh, pltpu.roll(x, sh, 0), -jnp.inf))
        sh *= 2
    return x


def _inv_unit_lower(p, levels, eye):
    n = p.shape[0]
    t = eye + p
    m = 2
    if m < levels:
        p = _dot(p, p)
        yield
    while m < levels:
        m *= 2
        if m >= levels:
            t = t + _dot(p, t)
        elif n % LANES == 0:
            both = _dot(p, jnp.concatenate([p, t], axis=1))
            p, t = both[:, 0:n], t + both[:, n:2 * n]
            yield
        else:
            both = _dot(jnp.concatenate([p, t], axis=0), p)
            p, t = both[0:n, :], t + both[n:2 * n, :]
            yield
    return t


def _run_tasks(tasks, caps):
    done, live, pending = set(), [], list(tasks)
    while pending or live:
        count = {}
        for _, group, _ in live:
            count[group] = count.get(group, 0) + 1
        still = []
        for key, group, deps, make in pending:
            if all(d in done for d in deps) and count.get(group, 0) < caps.get(group, len(tasks)):
                live.append((key, group, make()))
                count[group] = count.get(group, 0) + 1
            else:
                still.append((key, group, deps, make))
        assert live, "task graph has a cycle or a missing dependency"
        pending = still
        nxt = []
        for key, group, gen in live:
            try:
                next(gen)
                nxt.append((key, group, gen))
            except StopIteration:
                done.add(key)
        live = nxt


def _sigmoid(x):
    return 1.0 / (1.0 + jnp.exp(-x))


def _silu(x):
    return x * _sigmoid(x)


def _softplus(x):
    return jnp.maximum(x, 0.0) + jnp.log1p(jnp.exp(-jnp.abs(x)))


def _rms(x, w):
    return x * lax.rsqrt(jnp.mean(x * x, axis=-1, keepdims=True) + EPS) * w


def _resident(shape):
    zeros = (0,) * len(shape)
    return pl.BlockSpec(shape, lambda *_: zeros, pipeline_mode=pl.Buffered(1))


FA_Z = A_QKV
FA_BETA = A_MAIN
FA_DECAY = A_MAIN + LANES
FA_COLS = A_MAIN + 2 * LANES
_FR, _FLW, _FK, _FV, _FA, _FB, _FG = range(7)
FB_COLS = 7 * B_W
SUB_ROWS = 64
PROJ_ROWS = 512
PROJ_MAX_SEQS = 32


def _proj0_kernel(x_ref, conv0_ref, prev_ref, nw_ref, w_ref, convw_ref, alog_ref, dtb_ref,
                  mu_ref, w0_ref, a0_ref, w2a2_ref, kk_ref, ka_ref,
                  fa_ref, fb_ref, convn_ref, ext_a, ext_b, *, ns, tt):
    i = pl.program_id(1)
    tail = SUBLANES
    rows = ns * tt
    lane = _iota2((1, LANES), 1)
    head0 = lane < B_N
    ones_blk = (_iota2((LANES, LANES), 0) // B_N == _iota2((LANES, LANES), 1) // B_N)
    ones_sel = _stack_selector(ones_blk, SUM_TERMS, 0)

    @pl.when(i == 0)
    def _():
        ext_a[:, 0:tail, :] = jnp.zeros((ns, tail, A_QKV), F32)
        ext_a[:, tail - (A_CONV - 1):tail, :] = conv0_ref[...]
        ext_b[:, 0:tail, :] = jnp.zeros((ns, tail, B_COLS), F32)
        ext_b[:, tail - 1:tail, :] = prev_ref[...]

    @pl.when(i > 0)
    def _():
        ext_a[:, 0:tail, :] = ext_a[:, tt:tt + tail, :]
        ext_b[:, 0:tail, :] = ext_b[:, tt:tt + tail, :]

    xn = _rms(x_ref[...].reshape(rows, D_MODEL), nw_ref[...]).astype(BF16)
    piece = 4 * LANES

    def cols(lo, hi):
        return jnp.dot(xn, w_ref[:, lo:hi], preferred_element_type=F32).reshape(ns, tt, hi - lo)

    for lo in range(0, A_QKV, piece):
        ext_a[:, tail:tail + tt, lo:lo + piece] = cols(lo, lo + piece)
    fa_ref[:, :, FA_Z:FA_Z + A_VAL] = cols(A_QKV, A_MAIN)
    g = cols(P0_GATES, P0_B)
    fa_ref[:, :, FA_BETA:FA_BETA + LANES] = _sigmoid(g)
    fa_ref[:, :, FA_DECAY:FA_DECAY + LANES] = -jnp.exp(alog_ref[...]) * _softplus(g + dtb_ref[...])
    for lo in range(0, B_COLS, piece):
        hi = min(lo + piece, B_COLS)
        ext_b[:, tail:tail + tt, lo:hi] = cols(P0_B + lo, P0_B + hi)

    sub_t = min(tt, SUB_ROWS)
    sub_s = min(ns, SUB_ROWS // sub_t)
    cw = convw_ref[...]
    mu = mu_ref[...]
    for s0 in range(0, ns, sub_s):
        sq = slice(s0, s0 + sub_s)
        for t0 in range(0, tt, sub_t):
            tr = slice(t0, t0 + sub_t)
            y = ext_a[sq, tail + t0:tail + t0 + sub_t, :] * cw[A_CONV - 1:A_CONV, :]
            for j in range(A_CONV - 1):
                sh = A_CONV - 1 - j
                y = y + ext_a[sq, tail + t0 - sh:tail + t0 - sh + sub_t, :] * cw[j:j + 1, :]
            y = _silu(y)
            for h in range(A_HEADS):
                q = y[:, :, h * A_DK:(h + 1) * A_DK]
                fa_ref[sq, tr, h * A_DK:(h + 1) * A_DK] = (
                    q * lax.rsqrt(jnp.sum(q * q, axis=-1, keepdims=True) + EPS) * (A_DK ** -0.5))
                k = y[:, :, A_KEY + h * A_DK:A_KEY + (h + 1) * A_DK]
                fa_ref[sq, tr, A_KEY + h * A_DK:A_KEY + (h + 1) * A_DK] = (
                    k * lax.rsqrt(jnp.sum(k * k, axis=-1, keepdims=True) + EPS))
            fa_ref[sq, tr, 2 * A_KEY:A_QKV] = y[:, :, 2 * A_KEY:A_QKV]
            pb = ext_b[sq, tail + t0:tail + t0 + sub_t, :]
            pb_prev = ext_b[sq, tail + t0 - 1:tail + t0 - 1 + sub_t, :]
            pbs = (pb + (pb_prev - pb) * mu).reshape(sub_s * sub_t, B_COLS)
            lo_ = pbs[:, 4 * B_W:B_COLS]
            lo_ = jnp.where(head0, jnp.tanh(lo_), lo_)
            wa = _dot(lo_, w2a2_ref[...])
            logw = -B_DECAY_SCALE * _sigmoid(w0_ref[...] + wa[:, 0:B_W])
            a = _sigmoid(a0_ref[...] + wa[:, B_W:2 * B_W])
            kb = pbs[:, B_W:2 * B_W]
            kkx = kb * kk_ref[...]

            def put(slot, val):
                fb_ref[sq, tr, slot * B_W:(slot + 1) * B_W] = val.reshape(sub_s, sub_t, B_W)

            kkn = jnp.concatenate(
                [kkx[:, p * LANES:(p + 1) * LANES]
                 * lax.rsqrt(_dot_exact_rhs(kkx[:, p * LANES:(p + 1) * LANES] ** 2, ones_sel) + EPS)
                 for p in range(B_PAIRS)], axis=1)
            put(_FR, pbs[:, 0:B_W])
            put(_FLW, logw)
            put(_FK, kb * (1.0 + (a - 1.0) * ka_ref[...]))
            put(_FV, pbs[:, 2 * B_W:3 * B_W])
            put(_FA, -kkn)
            put(_FB, kkn * a)
            put(_FG, pbs[:, 3 * B_W:4 * B_W])

    @pl.when(i == pl.num_programs(1) - 1)
    def _():
        convn_ref[...] = ext_a[:, tt + tail - (A_CONV - 1):tt + tail, :]


def _proj0(x3d, conv0, prev_row, q, ns, tt):
    bsz, seq, _ = x3d.shape
    blk = lambda b, i: (b, i, 0)
    per_b3 = lambda b, i: (b, 0, 0)
    kern = functools.partial(_proj0_kernel, ns=ns, tt=tt)
    return pl.pallas_call(
        kern,
        grid=(bsz // ns, seq // tt),
        in_specs=[pl.BlockSpec((ns, tt, D_MODEL), blk),
                  pl.BlockSpec((ns, A_CONV - 1, A_QKV), per_b3),
                  pl.BlockSpec((ns, 1, B_COLS), per_b3),
                  _resident((1, D_MODEL)),
                  _resident((D_MODEL, P0_PAD)),
                  _resident((A_CONV, A_QKV)), _resident((1, LANES)), _resident((1, LANES)),
                  _resident((1, B_COLS)), _resident((1, B_W)), _resident((1, B_W)),
                  _resident((2 * B_LORA, 2 * B_W)), _resident((1, B_W)), _resident((1, B_W))],
        out_specs=[pl.BlockSpec((ns, tt, FA_COLS), blk),
                   pl.BlockSpec((ns, tt, FB_COLS), blk),
                   pl.BlockSpec((ns, A_CONV - 1, A_QKV), per_b3)],
        out_shape=[jax.ShapeDtypeStruct((bsz, seq, FA_COLS), F32),
                   jax.ShapeDtypeStruct((bsz, seq, FB_COLS), F32),
                   jax.ShapeDtypeStruct((bsz, A_CONV - 1, A_QKV), F32)],
        scratch_shapes=[pltpu.VMEM((ns, tt + SUBLANES, A_QKV), F32),
                        pltpu.VMEM((ns, tt + SUBLANES, B_COLS), F32)],
        compiler_params=pltpu.CompilerParams(dimension_semantics=("parallel", "arbitrary"),
                                             vmem_limit_bytes=VMEM_LIMIT),
        name="proj0_features",
    )(x3d, conv0, prev_row, q["norm0"], q["w0"], q["conv_w"], q["alog"], q["dtb"],
      q["mu"], q["rw0"], q["ra0"], q["w2a2"], q["k_k"], q["k_a"])


def _aux_kernel(xl_ref, xp_ref, nw_ref, wb_ref, xn_ref, prev_ref):
    xn_ref[...] = _rms(xl_ref[...], nw_ref[...])
    prev_ref[...] = jnp.dot(xp_ref[...].astype(BF16), wb_ref[...], preferred_element_type=F32)


def _aux(x_last, x_prev, norm_w, wb):
    b = x_last.shape[0]
    return pl.pallas_call(
        _aux_kernel,
        out_shape=[jax.ShapeDtypeStruct((b, D_MODEL), F32), jax.ShapeDtypeStruct((b, B_COLS), F32)],
        compiler_params=pltpu.CompilerParams(vmem_limit_bytes=VMEM_LIMIT),
        name="shift_rows",
    )(x_last, x_prev, norm_w, wb)


def _layer0_kernel(fa_ref, fb_ref, sg0_ref, sr0_ref, gnw_ref, rk_ref, lnw_ref, lnb_ref,
                   o_ref, sgout_ref, srout_ref,
                   stg_s, uw_s, attn_s, qg_s, kd_s, ge_s, str_s, tmp_s, *, nb, tb, chunk):
    i = pl.program_id(1)
    n_chunks = tb // chunk
    lane = _iota2((1, LANES), 1)
    head0 = lane < B_N
    ones_blk = (_iota2((LANES, LANES), 0) // B_N == _iota2((LANES, LANES), 1) // B_N)

    @pl.when(i == 0)
    def _():
        for bi in range(nb):
            for p in range(B_PAIRS):
                tmp_s[...] = jnp.zeros((B_N, LANES), F32)
                tmp_s[:, 0:B_N] = sr0_ref[bi, 2 * p + 1]
                shifted = pltpu.roll(tmp_s[...], B_N, 1)
                str_s[bi, p, B_N:LANES, :] = jnp.where(head0, 0.0, shifted)
                tmp_s[:, 0:B_N] = sr0_ref[bi, 2 * p]
                str_s[bi, p, 0:B_N, :] = tmp_s[...]
        stg_s[...] = sg0_ref[...]

    tri = _tri_incl(chunk)
    tri_sel = _stack_selector(tri, CUMSUM_TERMS, 1)
    ones_sel = _stack_selector(ones_blk, SUM_TERMS, 0)
    eye_c = _eye(chunk)
    eye_2c = _eye(2 * chunk)

    n_gates = 2 * A_HEADS
    is_decay = jnp.bitwise_and(lane, n_gates - 1) >= A_HEADS
    sel = _replicate_selector(n_gates)
    strict = _iota2((chunk, chunk), 0) > _iota2((chunk, chunk), 1)
    gnw = gnw_ref[...]
    qkv_s = fa_ref
    rep = {}

    def gdn_gates(bi, c):
        rows = slice(c * chunk, (c + 1) * chunk)
        gcum = _dot_exact_lhs(tri_sel, fa_ref[bi, rows, FA_DECAY:FA_DECAY + LANES])
        yield
        rep[bi, c] = _replicate(jnp.where(is_decay, gcum, fa_ref[bi, rows, FA_BETA:FA_BETA + LANES]),
                                sel, n_gates)

    wpk = A_HEADS * chunk
    head_pk = _iota2((1, wpk), 1) // chunk
    s_pk = _iota2((1, wpk), 1) - head_pk * chunk
    t_pk = _iota2((chunk, wpk), 0)
    diag_pk = t_pk == s_pk
    tri_pk = t_pk >= s_pk
    strict_pk = t_pk > s_pk
    head_of_klane = _iota2((1, A_KEY), 1) // A_DK
    head_of_uwlane = _iota2((1, A_HEADS * (A_DV + A_DK)), 1) // (A_DV + A_DK)

    def widen(x):
        reps = -(-wpk // LANES)
        w = jnp.concatenate([x] * reps, axis=1) if reps > 1 else x
        return w if w.shape[1] == wpk else w[:, 0:wpk]

    def pack_heads(xs):
        out = xs[A_HEADS - 1]
        for h in range(A_HEADS - 2, -1, -1):
            out = jnp.where(head_pk == h, xs[h], out)
        return out

    def block_diag(x):
        return jnp.concatenate([jnp.where(head_pk == h, x, 0.0) for h in range(A_HEADS)], axis=0)

    def gdn_local(bi, c):
        rows = slice(c * chunk, (c + 1) * chunk)
        heads = range(A_HEADS)
        q = [qkv_s[bi, rows, h * A_DK:(h + 1) * A_DK] for h in heads]
        k = [qkv_s[bi, rows, A_KEY + h * A_DK:A_KEY + (h + 1) * A_DK] for h in heads]
        v = [qkv_s[bi, rows, 2 * A_KEY + h * A_DV:2 * A_KEY + (h + 1) * A_DV] for h in heads]
        qk = [_dot_nt(q[h], k[h]) for h in heads]
        k_bd = jnp.concatenate([jnp.where(head_of_klane == h, jnp.concatenate([k[h]] * A_HEADS, axis=1), 0.0)
                                for h in heads], axis=0)
        kk_pk = _dot_nt(jnp.concatenate(k, axis=1), k_bd)
        yield
        b_rep = [rep[bi, c][:, h * LANES:(h + 1) * LANES] for h in heads]
        g_rep = [rep[bi, c][:, (A_HEADS + h) * LANES:(A_HEADS + h + 1) * LANES] for h in heads]
        eg = [jnp.exp(g) for g in g_rep]
        for h in heads:
            g_row = _col_to_row(g_rep[h][:, 0:chunk])
            g_last = g_rep[h][chunk - 1:chunk, :]
            decay = jnp.where(tri, jnp.exp(jnp.where(tri, g_rep[h][:, 0:chunk] - g_row, 0.0)), 0.0)
            qg_s[bi, rows, h * A_DK:(h + 1) * A_DK] = q[h] * eg[h]
            kd_s[bi, rows, h * A_DK:(h + 1) * A_DK] = k[h] * jnp.exp(g_last - g_rep[h])
            ge_s[bi, c, h:h + 1, :] = jnp.exp(g_last)
            attn_s[bi, rows, h * LANES:h * LANES + chunk] = decay * qk[h]
        g_col_pk = pack_heads([widen(g) for g in g_rep])
        b_pk = pack_heads([widen(b) for b in b_rep])
        g_row_pk = jnp.sum(jnp.where(diag_pk, g_col_pk, 0.0), axis=0, keepdims=True)
        decay_pk = jnp.where(tri_pk, jnp.exp(jnp.where(tri_pk, g_col_pk - g_row_pk, 0.0)), 0.0)
        p = -jnp.where(strict_pk, b_pk * decay_pk * kk_pk, 0.0)
        t = jnp.where(diag_pk, 1.0, 0.0) + p
        m = 2
        if m < chunk:
            p = _dot(p, block_diag(p))
            yield
        while m < chunk:
            m *= 2
            if m >= chunk:
                t = t + _dot(t, block_diag(p))
            else:
                both = _dot(jnp.concatenate([p, t], axis=0), block_diag(p))
                p, t = both[0:chunk, :], t + both[chunk:2 * chunk, :]
                yield
        rhs = jnp.concatenate(
            [jnp.where(head_of_uwlane == h,
                       jnp.concatenate([b_rep[h] * v[h], (b_rep[h] * eg[h]) * k[h]] * A_HEADS, axis=1), 0.0)
             for h in heads], axis=0)
        uw_s[bi, rows, :] = _dot(t, rhs)

    def gdn_state(bi, c, h):
        rows = slice(c * chunk, (c + 1) * chunk)
        uw = uw_s[bi, rows, 2 * h * LANES:2 * (h + 1) * LANES]
        s = stg_s[bi, h]
        wq_s = _dot(jnp.concatenate([uw[:, A_DV:A_DV + A_DK], qg_s[bi, rows, h * A_DK:(h + 1) * A_DK]], axis=0), s)
        yield
        nv = uw[:, 0:A_DV] - wq_s[0:chunk, :]
        o = wq_s[chunk:2 * chunk, :] + _dot(attn_s[bi, rows, h * LANES:h * LANES + chunk], nv)
        stg_s[bi, h] = ge_s[bi, c, h:h + 1, :] * s + _dot_tn(kd_s[bi, rows, h * A_DK:(h + 1) * A_DK], nv)
        yield
        z = fa_ref[bi, rows, FA_Z + h * A_DV:FA_Z + (h + 1) * A_DV]
        o_ref[bi, rows, h * A_DV:(h + 1) * A_DV] = _rms(o, gnw) * _silu(z)

    feat_s = fb_ref
    c2 = 2 * chunk
    ri = _iota2((c2, c2), 0)
    ci = _iota2((c2, c2), 1)
    same_head = (ri // chunk) == (ci // chunk)
    incl_bd = same_head & (ri >= ci)
    strict_bd = same_head & (ri > ci)
    ri4 = _iota2((2 * c2, c2), 0)
    ci4 = _iota2((2 * c2, c2), 1)
    rr4 = ri4 % c2
    lm_mask = ((rr4 // chunk) == (ci4 // chunk)) & ((rr4 > ci4) | ((ri4 >= c2) & (rr4 == ci4)))
    lane_st = (_iota2((c2, LANES), 0) // chunk) == (_iota2((c2, LANES), 1) // B_N)
    local = {}

    def rwkv_local(bi, c, p):
        rows = slice(c * chunk, (c + 1) * chunk)

        def feat(slot):
            return feat_s[bi, rows, slot * B_W + p * LANES:slot * B_W + (p + 1) * LANES]
        r, lw, k, v, a_, b_ = (feat(s) for s in (_FR, _FLW, _FK, _FV, _FA, _FB))
        cs = _dot_exact_lhs(tri_sel, lw)
        yield
        cs_last = cs[chunk - 1:chunk, :]
        e_inv = jnp.exp(-cs)
        e_rem = jnp.exp(cs_last - cs)
        at = a_ * jnp.exp(cs - lw)
        rt = r * jnp.exp(cs)
        bt = b_ * e_inv
        kt = k * e_inv
        a_st = jnp.concatenate([jnp.where(head0, at, 0.0), jnp.where(head0, 0.0, at)], axis=0)
        r_st = jnp.concatenate([jnp.where(head0, rt, 0.0), jnp.where(head0, 0.0, rt)], axis=0)
        b2 = jnp.concatenate([bt, bt], axis=0)
        k2 = jnp.concatenate([kt, kt], axis=0)
        v2 = jnp.concatenate([v, v], axis=0)
        ar_st = jnp.concatenate([a_st, r_st], axis=0)
        if c2 % LANES == 0:
            ar_bk = _dot_nt(ar_st, jnp.concatenate([b2, k2], axis=0))
            ar_b, ar_k = ar_bk[:, 0:c2], ar_bk[:, c2:2 * c2]
        else:
            ar_b = _dot_nt(ar_st, b2)
            ar_k = _dot_nt(ar_st, k2)
        yield
        l_ab = jnp.where(strict_bd, ar_b[0:c2, :], 0.0)
        m_rb = jnp.where(incl_bd, ar_b[c2:2 * c2, :], 0.0)
        lm_k = jnp.where(lm_mask, ar_k, 0.0)
        lm_v = _dot(lm_k, v2)
        sl = slice(p * LANES, (p + 1) * LANES)
        bonus = _dot_exact_rhs(r * k * rk_ref[:, sl], ones_sel) * v
        t_inv = yield from _inv_unit_lower(l_ab, chunk, eye_2c)
        local[bi, c, p] = (t_inv, lm_v, m_rb, jnp.concatenate([at, rt], axis=0),
                           jnp.concatenate([b_ * e_rem, k * e_rem], axis=0), jnp.exp(cs_last), bonus)

    def rwkv_state(bi, c, p):
        rows = slice(c * chunk, (c + 1) * chunk)
        sl = slice(p * LANES, (p + 1) * LANES)
        t_inv, lm_v, m_rb, at_rt, bk_rem, decay_end, bonus = local.pop((bi, c, p))
        v = feat_s[bi, rows, _FV * B_W + p * LANES:_FV * B_W + (p + 1) * LANES]
        gate = feat_s[bi, rows, _FG * B_W + p * LANES:_FG * B_W + (p + 1) * LANES]
        s = str_s[bi, p]
        ar = _dot_nt(at_rt, s)
        yield
        ah = ar[0:chunk, :]
        rh = ar[chunk:c2, :]
        x_st = jnp.where(lane_st, jnp.concatenate([ah, ah], axis=0) + lm_v[0:c2, :], 0.0)
        u_st = _dot(t_inv, x_st)
        yield
        o_st = jnp.where(lane_st, _dot(m_rb, u_st) + lm_v[c2:2 * c2, :], 0.0)
        u = u_st[0:chunk, :] + u_st[chunk:c2, :]
        upd = _dot_tn(jnp.concatenate([u, v], axis=0), bk_rem)
        yield
        o = rh + o_st[0:chunk, :] + o_st[chunk:c2, :]
        str_s[bi, p] = s * decay_end + jnp.where(ones_blk, upd, 0.0)
        mean = _dot_exact_rhs(o, ones_sel) * (1.0 / B_N)
        yield
        d = o - mean
        var = _dot_exact_rhs(d * d, ones_sel) * (1.0 / B_N)
        yield
        ob = d * lax.rsqrt(var + B_GN_EPS) * lnw_ref[:, sl] + lnb_ref[:, sl] + bonus
        o_ref[bi, rows, A_VAL + p * LANES:A_VAL + (p + 1) * LANES] = ob * _silu(gate)

    tasks = []
    blocks = [(bi, c) for c in range(n_chunks) for bi in range(nb)]
    for bi, c in blocks:
        for p in range(B_PAIRS):
            deps = [("rl", bi, c, p)] + ([("rs", bi, c - 1, p)] if c else [])
            tasks.append((("rs", bi, c, p), "rwkv_state", deps, functools.partial(rwkv_state, bi, c, p)))
        for h in range(A_HEADS):
            deps = [("gl", bi, c)] + ([("gs", bi, c - 1, h)] if c else [])
            tasks.append((("gs", bi, c, h), "gdn_state", deps, functools.partial(gdn_state, bi, c, h)))
    for bi, c in blocks:
        for p in range(B_PAIRS):
            tasks.append((("rl", bi, c, p), "rwkv_local", [], functools.partial(rwkv_local, bi, c, p)))
        tasks.append((("gl", bi, c), "gdn_local", [("gg", bi, c)], functools.partial(gdn_local, bi, c)))
    for bi, c in blocks:
        tasks.append((("gg", bi, c), "gdn_gates", [], functools.partial(gdn_gates, bi, c)))
    _run_tasks(tasks, {"gdn_gates": 2, "rwkv_local": 2 * B_PAIRS, "gdn_local": 4})

    @pl.when(i == pl.num_programs(1) - 1)
    def _():
        for bi in range(nb):
            for p in range(B_PAIRS):
                srout_ref[bi, 2 * p] = str_s[bi, p, 0:B_N, 0:B_N]
                shifted = pltpu.roll(str_s[bi, p, B_N:LANES, :], B_N, 1)
                srout_ref[bi, 2 * p + 1] = shifted[:, 0:B_N]
        sgout_ref[...] = stg_s[...]


def _layer0_mixers(fa, fb, s_gdn, s_rwkv, q, nb, tb, chunk):
    bsz, seq, _ = fa.shape
    blk = lambda b, i: (b, i, 0)
    per_b4 = lambda b, i: (b, 0, 0, 0)
    fix2 = lambda b, i: (0, 0)
    kern = functools.partial(_layer0_kernel, nb=nb, tb=tb, chunk=chunk)
    vec = pl.BlockSpec((1, B_W), fix2)
    return pl.pallas_call(
        kern,
        grid=(bsz // nb, seq // tb),
        in_specs=[pl.BlockSpec((nb, tb, FA_COLS), blk),
                  pl.BlockSpec((nb, tb, FB_COLS), blk),
                  pl.BlockSpec((nb, A_HEADS, A_DK, A_DV), per_b4),
                  pl.BlockSpec((nb, B_HEADS, B_N, B_N), per_b4),
                  pl.BlockSpec((1, A_DV), fix2), vec, vec, vec],
        out_specs=[pl.BlockSpec((nb, tb, MIX0), blk),
                   pl.BlockSpec((nb, A_HEADS, A_DK, A_DV), per_b4),
                   pl.BlockSpec((nb, B_HEADS, B_N, B_N), per_b4)],
        out_shape=[jax.ShapeDtypeStruct((bsz, seq, MIX0), F32),
                   jax.ShapeDtypeStruct((bsz, A_HEADS, A_DK, A_DV), F32),
                   jax.ShapeDtypeStruct((bsz, B_HEADS, B_N, B_N), F32)],
        scratch_shapes=[pltpu.VMEM((nb, A_HEADS, A_DK, A_DV), F32),
                        pltpu.VMEM((nb, tb, A_HEADS * (A_DV + A_DK)), F32),
                        pltpu.VMEM((nb, tb, A_HEADS * LANES), F32),
                        pltpu.VMEM((nb, tb, A_KEY), F32),
                        pltpu.VMEM((nb, tb, A_KEY), F32),
                        pltpu.VMEM((nb, tb // chunk, SUBLANES, LANES), F32),
                        pltpu.VMEM((nb, B_PAIRS, LANES, LANES), F32),
                        pltpu.VMEM((B_N, LANES), F32)],
        compiler_params=pltpu.CompilerParams(dimension_semantics=("parallel", "arbitrary"),
                                             vmem_limit_bytes=VMEM_LIMIT),
        name="layer0_mixers",
    )(fa, fb, s_gdn, s_rwkv, q["gnw"], q["r_k"], q["ln_w"], q["ln_b"])


def _layer1_kernel(x_ref, mix_ref, c0_ref, n0_ref, m0_ref, wo0_ref, n1_ref, w1_ref, bias_ref, wo1_ref, nf_ref,
                   y_ref, cout_ref, nout_ref, mout_ref,
                   proj_s, x1_s, h_s, gsc_s, caug_s, m_s, *, nb, tb, chunk):
    i = pl.program_id(1)
    n_chunks = tb // chunk
    lane = _iota2((1, LANES), 1)
    krow = _iota2((LANES, 1), 0)
    n_gates = 2 * C_HEADS
    is_f = jnp.bitwise_and(lane, n_gates - 1) >= C_HEADS

    @pl.when(i == 0)
    def _():
        for bi in range(nb):
            m0 = m0_ref[bi]
            for h in range(C_HEADS):
                j = h % 2
                caug_s[bi, h] = jnp.zeros((LANES, 2 * C_DV), F32)
                caug_s[bi, h, j * C_DQK:(j + 1) * C_DQK, 0:C_DV] = c0_ref[bi, h]
                n_col = _row_to_col(n0_ref[bi, h // 2:h // 2 + 1, :])
                own = (krow // C_DQK) == j
                caug_s[bi, h, :, C_DV:2 * C_DV] = jnp.where(own, jnp.broadcast_to(n_col, (LANES, C_DV)), 0.0)
                m_s[bi, h:h + 1, :] = jnp.broadcast_to(m0[:, h:h + 1], (1, LANES))

    tri = _tri_incl(chunk)
    tri_sel = _stack_selector(tri, CUMSUM_TERMS, 1)
    sel = _replicate_selector(n_gates)
    ones_blk = jnp.ones((chunk, C_DV), F32)
    off = 2 * C_QK
    piece = 4 * LANES
    rep = {}

    def stack(ref, rows):
        return jnp.concatenate([ref[bi, rows, :] for bi in range(nb)], axis=0)

    def unstack(ref, rows, cols, val):
        for bi in range(nb):
            ref[bi, rows, cols] = val[bi * chunk:(bi + 1) * chunk, :]

    def proj_task(c):
        rows = slice(c * chunk, (c + 1) * chunk)
        x1 = stack(x_ref, rows) + jnp.dot(stack(mix_ref, rows).astype(BF16), wo0_ref[...],
                                          preferred_element_type=F32)
        unstack(x1_s, rows, slice(0, D_MODEL), x1)
        yield
        xn = _rms(x1, n1_ref[...]).astype(BF16)
        g = jnp.dot(xn, w1_ref[:, P1_GATES:P1_PAD], preferred_element_type=F32) + bias_ref[...]
        cap = C_GATE_CAP * jnp.tanh(g / C_GATE_CAP)
        unstack(gsc_s, rows, slice(0, LANES), jnp.where(is_f, -_softplus(-cap), cap))
        yield
        for j in range(0, P1_GATES, piece):
            unstack(proj_s, rows, slice(j, j + piece),
                    jnp.dot(xn, w1_ref[:, j:j + piece], preferred_element_type=F32))
            yield

    def gate_task(bi, c):
        rows = slice(c * chunk, (c + 1) * chunk)
        gl = gsc_s[bi, rows, :]
        bcum = _dot_exact_lhs(tri_sel, gl)
        yield
        rep[bi, c] = _replicate(jnp.where(is_f, bcum, gl), sel, n_gates)

    def head_chain(bi, c, h):
        rows = slice(c * chunk, (c + 1) * chunk)
        p, j = h // 2, h % 2
        own_lane = (lane // C_DQK) == j
        own_row = (krow // C_DQK) == j
        qp = proj_s[bi, rows, p * LANES:(p + 1) * LANES] * (C_DQK ** -0.5)
        kp = proj_s[bi, rows, C_QK + p * LANES:C_QK + (p + 1) * LANES]
        v = proj_s[bi, rows, off + h * C_DV:off + (h + 1) * C_DV]
        vaug = jnp.concatenate([v, ones_blk], axis=1)
        qk = _dot_nt(jnp.where(own_lane, qp, 0.0), kp)
        cm = caug_s[bi, h]
        qc = _dot(qp, cm)
        i_rep = rep[bi, c][:, h * LANES:(h + 1) * LANES]
        b_rep = rep[bi, c][:, (C_HEADS + h) * LANES:(C_HEADS + h + 1) * LANES]
        b_row = _col_to_row(b_rep[:, 0:chunk])
        i_row = _col_to_row(i_rep[:, 0:chunk])
        dmat = jnp.where(tri, b_rep[:, 0:chunk] - b_row + i_row, -jnp.inf)
        dmax = b_rep + _cummax_rows(i_rep - b_rep)
        m = m_s[bi, h:h + 1, :]
        inter = b_rep + m
        mt = jnp.maximum(inter, dmax)
        m_new = mt[chunk - 1:chunk, :]
        b_last = b_rep[chunk - 1:chunk, :]
        w_s = jnp.exp(b_last - b_rep + i_rep - m_new)
        e_c = jnp.exp(b_last + m - m_new)
        e_inter = jnp.exp(inter - mt)
        own_k = slice(j * C_DQK, (j + 1) * C_DQK)
        upd = _dot((w_s * kp).T[own_k, :], vaug)
        yield
        pm = jnp.exp(dmat - mt[:, 0:chunk]) * qk
        pv = _dot(pm, vaug)
        caug_s[bi, h, own_k, :] = jnp.concatenate([e_c, e_c], axis=1) * cm[own_k, :] + upd
        m_s[bi, h:h + 1, :] = m_new
        yield
        num = e_inter * qc[:, 0:C_DV] + pv[:, 0:C_DV]
        den = e_inter * qc[:, C_DV:2 * C_DV] + pv[:, C_DV:2 * C_DV]
        hh = num / jnp.maximum(jnp.abs(den), jnp.exp(-mt))
        og = proj_s[bi, rows, off + C_V + h * C_DV:off + C_V + (h + 1) * C_DV]
        z = proj_s[bi, rows, off + 2 * C_V + h * C_DV:off + 2 * C_V + (h + 1) * C_DV]
        h_s[bi, rows, h * C_DV:(h + 1) * C_DV] = hh * _sigmoid(og) * _silu(z)

    def final_task(c):
        rows = slice(c * chunk, (c + 1) * chunk)
        x2 = stack(x1_s, rows) + jnp.dot(stack(h_s, rows).astype(BF16), wo1_ref[...],
                                         preferred_element_type=F32)
        yield
        unstack(y_ref, rows, slice(0, D_MODEL), _rms(x2, nf_ref[...]))

    tasks = []
    for c in range(n_chunks):
        tasks.append((("p", c), "proj", [], functools.partial(proj_task, c)))
    for c in range(n_chunks):
        heads = [("h", bi, c, h) for bi in range(nb) for h in range(C_HEADS)]
        tasks.append((("f", c), "final", heads, functools.partial(final_task, c)))
        for bi in range(nb):
            tasks.append((("g", bi, c), "gates", [("p", c)], functools.partial(gate_task, bi, c)))
            for h in range(C_HEADS):
                deps = [("p", c), ("g", bi, c)] + ([("h", bi, c - 1, h)] if c else [])
                tasks.append((("h", bi, c, h), "heads", deps, functools.partial(head_chain, bi, c, h)))
    _run_tasks(tasks, {"proj": 1, "final": 1})

    @pl.when(i == pl.num_programs(1) - 1)
    def _():
        for bi in range(nb):
            for h in range(C_HEADS):
                j = h % 2
                cout_ref[bi, h] = caug_s[bi, h, j * C_DQK:(j + 1) * C_DQK, 0:C_DV]
            for p in range(C_HEADS // 2):
                n_col = caug_s[bi, 2 * p, :, C_DV:C_DV + 1] + caug_s[bi, 2 * p + 1, :, C_DV:C_DV + 1]
                nout_ref[bi, p:p + 1, :] = _col_to_row(n_col)
        mout_ref[...] = m_s[...]


def _layer1(x3d, mix3d, c0, n0, m0, q, nb, tb, chunk):
    bsz, seq, _ = x3d.shape
    blk = lambda b, i: (b, i, 0)
    per_b3 = lambda b, i: (b, 0, 0)
    per_b4 = lambda b, i: (b, 0, 0, 0)
    kern = functools.partial(_layer1_kernel, nb=nb, tb=tb, chunk=chunk)
    pairs = C_HEADS // 2
    return pl.pallas_call(
        kern,
        grid=(bsz // nb, seq // tb),
        in_specs=[pl.BlockSpec((nb, tb, D_MODEL), blk),
                  pl.BlockSpec((nb, tb, MIX0), blk),
                  pl.BlockSpec((nb, C_HEADS, C_DQK, C_DV), per_b4),
                  pl.BlockSpec((nb, pairs, LANES), per_b3),
                  pl.BlockSpec((nb, 1, C_HEADS), per_b3),
                  _resident((MIX0, D_MODEL)),
                  _resident((1, D_MODEL)),
                  _resident((D_MODEL, P1_PAD)),
                  _resident((1, LANES)),
                  _resident((C_V, D_MODEL)),
                  _resident((1, D_MODEL))],
        out_specs=[pl.BlockSpec((nb, tb, D_MODEL), blk),
                   pl.BlockSpec((nb, C_HEADS, C_DQK, C_DV), per_b4),
                   pl.BlockSpec((nb, pairs, LANES), per_b3),
                   pl.BlockSpec((nb, C_HEADS, LANES), per_b3)],
        out_shape=[jax.ShapeDtypeStruct((bsz, seq, D_MODEL), F32),
                   jax.ShapeDtypeStruct((bsz, C_HEADS, C_DQK, C_DV), F32),
                   jax.ShapeDtypeStruct((bsz, pairs, LANES), F32),
                   jax.ShapeDtypeStruct((bsz, C_HEADS, LANES), F32)],
        scratch_shapes=[pltpu.VMEM((nb, tb, P1_PAD), F32),
                        pltpu.VMEM((nb, tb, D_MODEL), F32),
                        pltpu.VMEM((nb, tb, C_V), F32),
                        pltpu.VMEM((nb, tb, LANES), F32),
                        pltpu.VMEM((nb, C_HEADS, LANES, 2 * C_DV), F32),
                        pltpu.VMEM((nb, C_HEADS, LANES), F32)],
        compiler_params=pltpu.CompilerParams(dimension_semantics=("parallel", "arbitrary"),
                                             vmem_limit_bytes=VMEM_LIMIT),
        name="layer1",
    )(x3d, mix3d, c0, n0, m0, q["wo0"], q["norm1"], q["w1"], q["gate_bias"], q["wo1"], q["normf"])


def _gate_tile(cols):
    tiled = jnp.concatenate([cols.astype(F32)] * GATE_COPIES, axis=1)
    return jnp.pad(tiled, ((0, 0), (0, LANES - tiled.shape[1])))


def _prepare_params(p):
    w_in0 = p["w_in0"]
    a_cols = A_MAIN + 2 * A_HEADS
    w0 = jnp.concatenate([w_in0[:, 0:A_MAIN], _gate_tile(w_in0[:, A_MAIN:a_cols]), w_in0[:, a_cols:]], axis=1)
    w_in1 = p["w_in1"]
    w1 = jnp.concatenate([w_in1[:, 0:P1_GATES], _gate_tile(w_in1[:, P1_GATES:])], axis=1)
    zeros_h = jnp.zeros((A_HEADS,), F32)
    alog = _gate_tile(jnp.concatenate([zeros_h, p["gdn_a_log"]]).reshape(1, -1))
    dtb = _gate_tile(jnp.concatenate([zeros_h, p["gdn_dt_bias"]]).reshape(1, -1))
    gate_bias = _gate_tile(jnp.concatenate([p["mlstm_b_i"], p["mlstm_b_f"]]).reshape(1, -1))
    w2a2 = jnp.zeros((2 * B_LORA, 2 * B_W), F32)
    w2a2 = w2a2.at[0:B_LORA, 0:B_W].set(p["rwkv_w2"]).at[B_LORA:, B_W:].set(p["rwkv_a2"])
    row = lambda v: v.reshape(1, -1).astype(F32)
    return dict(
        w0=w0.astype(BF16), w1=w1.astype(BF16), wb=w_in0[:, a_cols:].astype(BF16),
        wo0=p["w_out0"].astype(BF16), wo1=p["w_out1"].astype(BF16), w2a2=w2a2.astype(BF16),
        norm0=row(p["norm0_w"]), norm1=row(p["norm1_w"]), normf=row(p["norm_f_w"]),
        conv_w=p["gdn_conv_w"].astype(F32), alog=alog, dtb=dtb, gnw=row(p["gdn_norm_w"]),
        mu=row(p["rwkv_mu"]), rw0=row(p["rwkv_w0"]), ra0=row(p["rwkv_a0"]),
        k_k=row(p["rwkv_k_k"]), k_a=row(p["rwkv_k_a"]), r_k=row(p["rwkv_r_k"]),
        ln_w=row(p["rwkv_ln_w"]), ln_b=row(p["rwkv_ln_b"]), gate_bias=gate_bias,
    )


def _tiles(bsz, seq):
    chunk = math.gcd(seq, MAX_CHUNK)
    tb = min(seq, 256)
    nb = 2 if seq > MAX_CHUNK else 8
    nb = math.gcd(nb, bsz)
    tt = min(seq, PROJ_ROWS)
    ns = math.gcd(min(PROJ_ROWS // tt, PROJ_MAX_SEQS), bsz)
    return chunk, tb, nb, tt, ns


def _trunk(x, state, q):
    conv0, s_gdn, x_prev, s_rwkv, c0, n0, m0 = state
    bsz, seq, _ = x.shape
    chunk, tb, nb, tt, ns = _tiles(bsz, seq)
    xn_last, prev_row = _aux(x[:, -1, :], x_prev, q["norm0"], q["wb"])
    fa, fb, conv_new = _proj0(x, conv0, prev_row.reshape(bsz, 1, B_COLS), q, ns, tt)
    mix, s_gdn_new, s_rwkv_new = _layer0_mixers(fa, fb, s_gdn, s_rwkv, q, nb, tb, chunk)
    y, c_new, n_new, m_new = _layer1(x, mix, c0, n0.reshape(bsz, C_HEADS // 2, LANES),
                                     m0.reshape(bsz, 1, C_HEADS), q, nb, tb, chunk)
    return y, (conv_new, s_gdn_new, xn_last, s_rwkv_new, c_new,
               n_new.reshape(bsz, C_HEADS, C_DQK), m_new[:, :, 0])


def kernel(x_prompt, x_sample, state_gdn_conv, state_gdn, state_rwkv_shift, state_rwkv,
           state_mlstm_c, state_mlstm_n, state_mlstm_m,
           norm0_w, w_in0, w_out0, gdn_conv_w, gdn_a_log, gdn_dt_bias, gdn_norm_w,
           rwkv_mu, rwkv_w0, rwkv_w2, rwkv_a0, rwkv_a2, rwkv_k_k, rwkv_k_a, rwkv_r_k,
           rwkv_ln_w, rwkv_ln_b, norm1_w, w_in1, w_out1, mlstm_b_i, mlstm_b_f, norm_f_w):
    q = _prepare_params(dict(
        norm0_w=norm0_w, w_in0=w_in0, w_out0=w_out0, gdn_conv_w=gdn_conv_w, gdn_a_log=gdn_a_log,
        gdn_dt_bias=gdn_dt_bias, gdn_norm_w=gdn_norm_w, rwkv_mu=rwkv_mu, rwkv_w0=rwkv_w0,
        rwkv_w2=rwkv_w2, rwkv_a0=rwkv_a0, rwkv_a2=rwkv_a2, rwkv_k_k=rwkv_k_k, rwkv_k_a=rwkv_k_a,
        rwkv_r_k=rwkv_r_k, rwkv_ln_w=rwkv_ln_w, rwkv_ln_b=rwkv_ln_b, norm1_w=norm1_w, w_in1=w_in1,
        w_out1=w_out1, mlstm_b_i=mlstm_b_i, mlstm_b_f=mlstm_b_f, norm_f_w=norm_f_w))
    bsz = x_prompt.shape[0]
    prompt_state = (
        jnp.zeros((bsz, A_CONV - 1, A_QKV), F32),
        jnp.zeros((bsz, A_HEADS, A_DK, A_DV), F32),
        jnp.zeros((bsz, D_MODEL), F32),
        jnp.zeros((bsz, B_HEADS, B_N, B_N), F32),
        jnp.zeros((bsz, C_HEADS, C_DQK, C_DV), F32),
        jnp.zeros((bsz, C_HEADS, C_DQK), F32),
        jnp.zeros((bsz, C_HEADS), F32),
    )
    y_p, sp = _trunk(x_prompt, prompt_state, q)
    sample_state = (state_gdn_conv, state_gdn, state_rwkv_shift, state_rwkv,
                    state_mlstm_c, state_mlstm_n, state_mlstm_m)
    y_s, ss = _trunk(x_sample, sample_state, q)
    return (y_p, y_s) + tuple(sp) + tuple(ss)
```

```python
import functools
import math

import jax
import jax.numpy as jnp
from jax import lax
from jax.experimental import pallas as pl
from jax.experimental.pallas import tpu as pltpu

F32 = jnp.float32
BF16 = jnp.bfloat16

D_MODEL = 1024
EPS = 1e-6
LANES = 128
SUBLANES = 8
A_HEADS, A_DK, A_DV, A_CONV = 4, 128, 128, 4
A_KEY = A_HEADS * A_DK
A_VAL = A_HEADS * A_DV
A_QKV = 2 * A_KEY + A_VAL
A_MAIN = A_QKV + A_VAL
B_HEADS, B_N, B_LORA = 8, 64, 64
B_W = B_HEADS * B_N
B_COLS = 4 * B_W + 2 * B_LORA
B_PAIRS = B_HEADS // 2
B_GN_EPS = 64e-5
B_DECAY_SCALE = math.exp(-0.5)
MIX0 = A_VAL + B_W
C_HEADS, C_DQK, C_DV = 8, 64, 128
C_QK = C_HEADS * C_DQK
C_V = C_HEADS * C_DV
C_GATE_CAP = 15.0
P0_GATES = A_MAIN
P0_B = A_MAIN + LANES
P0_PAD = P0_B + B_COLS
P1_GATES = 2 * C_QK + 3 * C_V
P1_PAD = P1_GATES + LANES
MAX_CHUNK = 64
GATE_COPIES = 3
CUMSUM_TERMS = 3
SUM_TERMS = 2
VMEM_LIMIT = 56 * 1024 * 1024


def _dot(a, b):
    return jnp.dot(a.astype(BF16), b.astype(BF16), preferred_element_type=F32)


def _dot_nt(a, b):
    return lax.dot_general(a.astype(BF16), b.astype(BF16), (((1,), (1,)), ((), ())),
                           preferred_element_type=F32)


def _dot_tn(a, b):
    return lax.dot_general(a.astype(BF16), b.astype(BF16), (((0,), (0,)), ((), ())),
                           preferred_element_type=F32)


def _split(x, terms):
    out = []
    for _ in range(terms - 1):
        hi = x.astype(BF16)
        out.append(hi)
        x = x - hi.astype(F32)
    out.append(x.astype(BF16))
    return out


def _stack_selector(sel, terms, axis):
    s = jnp.where(sel, 1.0, 0.0).astype(BF16)
    return jnp.concatenate([s] * terms, axis=axis)


def _dot_exact_lhs(sel_stacked, x, terms=CUMSUM_TERMS):
    return jnp.dot(sel_stacked, jnp.concatenate(_split(x, terms), axis=0), preferred_element_type=F32)


def _dot_exact_rhs(x, sel_stacked, terms=SUM_TERMS):
    return jnp.dot(jnp.concatenate(_split(x, terms), axis=1), sel_stacked, preferred_element_type=F32)


def _iota2(shape, dim):
    return lax.broadcasted_iota(jnp.int32, shape, dim)


def _tri_incl(n):
    return (_iota2((n, n), 0) >= _iota2((n, n), 1))


def _eye(n):
    return jnp.where(_iota2((n, n), 0) == _iota2((n, n), 1), 1.0, 0.0)


def _col_to_row(col):
    n = col.shape[0]
    eye = _iota2((n, n), 0) == _iota2((n, n), 1)
    return jnp.sum(jnp.where(eye, col, 0.0), axis=0, keepdims=True)


def _row_to_col(row):
    n = row.shape[1]
    eye = _iota2((n, n), 0) == _iota2((n, n), 1)
    return jnp.sum(jnp.where(eye, row, 0.0), axis=1, keepdims=True)


def _replicate(x, sel, group):
    hi, mid, lo = _split(x, GATE_COPIES)
    lane = _iota2(x.shape, 1)
    packed = jnp.where(lane < group, hi, jnp.where(lane < 2 * group, mid, lo))
    packed = jnp.where(lane < GATE_COPIES * group, packed, jnp.zeros_like(packed))
    return jnp.dot(packed, sel, preferred_element_type=F32)


def _replicate_selector(group):
    assert group & (group - 1) == 0 and LANES == 128
    k = _iota2((LANES, group * LANES), 0)
    col = _iota2((LANES, group * LANES), 1)
    hit = (k < GATE_COPIES * group) & (jnp.bitwise_and(k, group - 1) == jnp.right_shift(col, 7))
    return jnp.where(hit, 1.0, 0.0).astype(BF16)


def _cummax_rows(x):
    n = x.shape[0]
    row = _iota2(x.shape, 0)
    sh = 1
    while sh < n:
        x = jnp.maximum(x, jnp.where(row >= sh, pltpu.roll(x, sh, 0), -jnp.inf))
        sh *= 2
    return x


def _inv_unit_lower(p, levels, eye):
    n = p.shape[0]
    t = eye + p
    m = 2
    if m < levels:
        p = _dot(p, p)
        yield
    while m < levels:
        m *= 2
        if m >= levels:
            t = t + _dot(p, t)
        elif n % LANES == 0:
            both = _dot(p, jnp.concatenate([p, t], axis=1))
            p, t = both[:, 0:n], t + both[:, n:2 * n]
            yield
        else:
            both = _dot(jnp.concatenate([p, t], axis=0), p)
            p, t = both[0:n, :], t + both[n:2 * n, :]
            yield
    return t


def _run_tasks(tasks, caps):
    done, live, pending = set(), [], list(tasks)
    while pending or live:
        count = {}
        for _, group, _ in live:
            count[group] = count.get(group, 0) + 1
        still = []
        for key, group, deps, make in pending:
            if all(d in done for d in deps) and count.get(group, 0) < caps.get(group, len(tasks)):
                live.append((key, group, make()))
                count[group] = count.get(group, 0) + 1
            else:
                still.append((key, group, deps, make))
        assert live, "task graph has a cycle or a missing dependency"
        pending = still
        nxt = []
        for key, group, gen in live:
            try:
                next(gen)
                nxt.append((key, group, gen))
            except StopIteration:
                done.add(key)
        live = nxt


def _sigmoid(x):
    return 1.0 / (1.0 + jnp.exp(-x))


def _silu(x):
    return x * _sigmoid(x)


def _softplus(x):
    return jnp.maximum(x, 0.0) + jnp.log1p(jnp.exp(-jnp.abs(x)))


def _rms(x, w):
    return x * lax.rsqrt(jnp.mean(x * x, axis=-1, keepdims=True) + EPS) * w


def _resident(shape):
    zeros = (0,) * len(shape)
    return pl.BlockSpec(shape, lambda *_: zeros, pipeline_mode=pl.Buffered(1))


FA_Z = A_QKV
FA_BETA = A_MAIN
FA_DECAY = A_MAIN + LANES
FA_COLS = A_MAIN + 2 * LANES
_FR, _FLW, _FK, _FV, _FA, _FB, _FG = range(7)
FB_COLS = 7 * B_W
SUB_ROWS = 64
PROJ_ROWS = 512
PROJ_MAX_SEQS = 32


def _proj0_kernel(x_ref, conv0_ref, xprev_ref, nw_ref, w_ref, convw_ref, alog_ref, dtb_ref,
                  mu_ref, w0_ref, a0_ref, w2a2_ref, kk_ref, ka_ref,
                  fa_ref, fb_ref, convn_ref, xnl_ref, ext_a, ext_b, *, ns, tt):
    i = pl.program_id(1)
    tail = SUBLANES
    rows = ns * tt
    lane = _iota2((1, LANES), 1)
    head0 = lane < B_N
    ones_blk = (_iota2((LANES, LANES), 0) // B_N == _iota2((LANES, LANES), 1) // B_N)
    ones_sel = _stack_selector(ones_blk, SUM_TERMS, 0)

    @pl.when(i == 0)
    def _():
        ext_a[:, 0:tail, :] = jnp.zeros((ns, tail, A_QKV), F32)
        ext_a[:, tail - (A_CONV - 1):tail, :] = conv0_ref[...]
        ext_b[:, 0:tail, :] = jnp.zeros((ns, tail, B_COLS), F32)
        ext_b[:, tail - 1:tail, :] = jnp.dot(
            xprev_ref[...].reshape(ns, D_MODEL).astype(BF16), w_ref[:, P0_B:P0_PAD],
            preferred_element_type=F32).reshape(ns, 1, B_COLS)

    @pl.when(i > 0)
    def _():
        ext_a[:, 0:tail, :] = ext_a[:, tt:tt + tail, :]
        ext_b[:, 0:tail, :] = ext_b[:, tt:tt + tail, :]

    xn = _rms(x_ref[...].reshape(rows, D_MODEL), nw_ref[...]).astype(BF16)
    piece = 4 * LANES

    def cols(lo, hi):
        return jnp.dot(xn, w_ref[:, lo:hi], preferred_element_type=F32).reshape(ns, tt, hi - lo)

    for lo in range(0, A_QKV, piece):
        ext_a[:, tail:tail + tt, lo:lo + piece] = cols(lo, lo + piece)
    fa_ref[:, :, FA_Z:FA_Z + A_VAL] = cols(A_QKV, A_MAIN)
    g = cols(P0_GATES, P0_B)
    fa_ref[:, :, FA_BETA:FA_BETA + LANES] = _sigmoid(g)
    fa_ref[:, :, FA_DECAY:FA_DECAY + LANES] = -jnp.exp(alog_ref[...]) * _softplus(g + dtb_ref[...])
    for lo in range(0, B_COLS, piece):
        hi = min(lo + piece, B_COLS)
        ext_b[:, tail:tail + tt, lo:hi] = cols(P0_B + lo, P0_B + hi)

    sub_t = min(tt, SUB_ROWS)
    sub_s = min(ns, SUB_ROWS // sub_t)
    cw = convw_ref[...]
    mu = mu_ref[...]
    for s0 in range(0, ns, sub_s):
        sq = slice(s0, s0 + sub_s)
        for t0 in range(0, tt, sub_t):
            tr = slice(t0, t0 + sub_t)
            y = ext_a[sq, tail + t0:tail + t0 + sub_t, :] * cw[A_CONV - 1:A_CONV, :]
            for j in range(A_CONV - 1):
                sh = A_CONV - 1 - j
                y = y + ext_a[sq, tail + t0 - sh:tail + t0 - sh + sub_t, :] * cw[j:j + 1, :]
            y = _silu(y)
            for h in range(A_HEADS):
                q = y[:, :, h * A_DK:(h + 1) * A_DK]
                fa_ref[sq, tr, h * A_DK:(h + 1) * A_DK] = (
                    q * lax.rsqrt(jnp.sum(q * q, axis=-1, keepdims=True) + EPS) * (A_DK ** -0.5))
                k = y[:, :, A_KEY + h * A_DK:A_KEY + (h + 1) * A_DK]
                fa_ref[sq, tr, A_KEY + h * A_DK:A_KEY + (h + 1) * A_DK] = (
                    k * lax.rsqrt(jnp.sum(k * k, axis=-1, keepdims=True) + EPS))
            fa_ref[sq, tr, 2 * A_KEY:A_QKV] = y[:, :, 2 * A_KEY:A_QKV]
            pb = ext_b[sq, tail + t0:tail + t0 + sub_t, :]
            pb_prev = ext_b[sq, tail + t0 - 1:tail + t0 - 1 + sub_t, :]
            pbs = (pb + (pb_prev - pb) * mu).reshape(sub_s * sub_t, B_COLS)
            lo_ = pbs[:, 4 * B_W:B_COLS]
            lo_ = jnp.where(head0, jnp.tanh(lo_), lo_)
            wa = _dot(lo_, w2a2_ref[...])
            logw = -B_DECAY_SCALE * _sigmoid(w0_ref[...] + wa[:, 0:B_W])
            a = _sigmoid(a0_ref[...] + wa[:, B_W:2 * B_W])
            kb = pbs[:, B_W:2 * B_W]
            kkx = kb * kk_ref[...]

            def put(slot, val):
                fb_ref[sq, tr, slot * B_W:(slot + 1) * B_W] = val.reshape(sub_s, sub_t, B_W)

            kkn = jnp.concatenate(
                [kkx[:, p * LANES:(p + 1) * LANES]
                 * lax.rsqrt(_dot_exact_rhs(kkx[:, p * LANES:(p + 1) * LANES] ** 2, ones_sel) + EPS)
                 for p in range(B_PAIRS)], axis=1)
            put(_FR, pbs[:, 0:B_W])
            put(_FLW, logw)
            put(_FK, kb * (1.0 + (a - 1.0) * ka_ref[...]))
            put(_FV, pbs[:, 2 * B_W:3 * B_W])
            put(_FA, -kkn)
            put(_FB, kkn * a)
            put(_FG, pbs[:, 3 * B_W:4 * B_W])

    @pl.when(i == pl.num_programs(1) - 1)
    def _():
        convn_ref[...] = ext_a[:, tt + tail - (A_CONV - 1):tt + tail, :]
        xnl_ref[...] = _rms(x_ref[:, tt - 1:tt, :], nw_ref[...])


def _proj0(x3d, conv0, x_prev, q, ns, tt):
    bsz, seq, _ = x3d.shape
    blk = lambda b, i: (b, i, 0)
    per_b3 = lambda b, i: (b, 0, 0)
    kern = functools.partial(_proj0_kernel, ns=ns, tt=tt)
    return pl.pallas_call(
        kern,
        grid=(bsz // ns, seq // tt),
        in_specs=[pl.BlockSpec((ns, tt, D_MODEL), blk),
                  pl.BlockSpec((ns, A_CONV - 1, A_QKV), per_b3),
                  pl.BlockSpec((ns, 1, D_MODEL), per_b3),
                  _resident((1, D_MODEL)),
                  _resident((D_MODEL, P0_PAD)),
                  _resident((A_CONV, A_QKV)), _resident((1, LANES)), _resident((1, LANES)),
                  _resident((1, B_COLS)), _resident((1, B_W)), _resident((1, B_W)),
                  _resident((2 * B_LORA, 2 * B_W)), _resident((1, B_W)), _resident((1, B_W))],
        out_specs=[pl.BlockSpec((ns, tt, FA_COLS), blk),
                   pl.BlockSpec((ns, tt, FB_COLS), blk),
                   pl.BlockSpec((ns, A_CONV - 1, A_QKV), per_b3),
                   pl.BlockSpec((ns, 1, D_MODEL), per_b3)],
        out_shape=[jax.ShapeDtypeStruct((bsz, seq, FA_COLS), F32),
                   jax.ShapeDtypeStruct((bsz, seq, FB_COLS), F32),
                   jax.ShapeDtypeStruct((bsz, A_CONV - 1, A_QKV), F32),
                   jax.ShapeDtypeStruct((bsz, 1, D_MODEL), F32)],
        scratch_shapes=[pltpu.VMEM((ns, tt + SUBLANES, A_QKV), F32),
                        pltpu.VMEM((ns, tt + SUBLANES, B_COLS), F32)],
        compiler_params=pltpu.CompilerParams(dimension_semantics=("parallel", "arbitrary"),
                                             vmem_limit_bytes=VMEM_LIMIT),
        name="proj0_features",
    )(x3d, conv0, x_prev, q["norm0"], q["w0"], q["conv_w"], q["alog"], q["dtb"],
      q["mu"], q["rw0"], q["ra0"], q["w2a2"], q["k_k"], q["k_a"])


def _layer0_kernel(fa_ref, fb_ref, sg0_ref, sr0_ref, gnw_ref, rk_ref, lnw_ref, lnb_ref,
                   o_ref, sgout_ref, srout_ref,
                   stg_s, uw_s, attn_s, qg_s, kd_s, ge_s, str_s, tmp_s, *, nb, tb, chunk):
    i = pl.program_id(1)
    n_chunks = tb // chunk
    lane = _iota2((1, LANES), 1)
    head0 = lane < B_N
    ones_blk = (_iota2((LANES, LANES), 0) // B_N == _iota2((LANES, LANES), 1) // B_N)

    @pl.when(i == 0)
    def _():
        for bi in range(nb):
            for p in range(B_PAIRS):
                tmp_s[...] = jnp.zeros((B_N, LANES), F32)
                tmp_s[:, 0:B_N] = sr0_ref[bi, 2 * p + 1]
                shifted = pltpu.roll(tmp_s[...], B_N, 1)
                str_s[bi, p, B_N:LANES, :] = jnp.where(head0, 0.0, shifted)
                tmp_s[:, 0:B_N] = sr0_ref[bi, 2 * p]
                str_s[bi, p, 0:B_N, :] = tmp_s[...]
        stg_s[...] = sg0_ref[...]

    tri = _tri_incl(chunk)
    tri_sel = _stack_selector(tri, CUMSUM_TERMS, 1)
    ones_sel = _stack_selector(ones_blk, SUM_TERMS, 0)
    eye_c = _eye(chunk)
    eye_2c = _eye(2 * chunk)

    n_gates = 2 * A_HEADS
    is_decay = jnp.bitwise_and(lane, n_gates - 1) >= A_HEADS
    sel = _replicate_selector(n_gates)
    strict = _iota2((chunk, chunk), 0) > _iota2((chunk, chunk), 1)
    gnw = gnw_ref[...]
    qkv_s = fa_ref
    rep = {}

    def gdn_gates(bi, c):
        rows = slice(c * chunk, (c + 1) * chunk)
        gcum = _dot_exact_lhs(tri_sel, fa_ref[bi, rows, FA_DECAY:FA_DECAY + LANES])
        yield
        rep[bi, c] = _replicate(jnp.where(is_decay, gcum, fa_ref[bi, rows, FA_BETA:FA_BETA + LANES]),
                                sel, n_gates)

    wpk = A_HEADS * chunk
    head_pk = _iota2((1, wpk), 1) // chunk
    s_pk = _iota2((1, wpk), 1) - head_pk * chunk
    t_pk = _iota2((chunk, wpk), 0)
    diag_pk = t_pk == s_pk
    tri_pk = t_pk >= s_pk
    strict_pk = t_pk > s_pk
    head_of_klane = _iota2((1, A_KEY), 1) // A_DK
    head_of_uwlane = _iota2((1, A_HEADS * (A_DV + A_DK)), 1) // (A_DV + A_DK)

    def widen(x):
        reps = -(-wpk // LANES)
        w = jnp.concatenate([x] * reps, axis=1) if reps > 1 else x
        return w if w.shape[1] == wpk else w[:, 0:wpk]

    def pack_heads(xs):
        out = xs[A_HEADS - 1]
        for h in range(A_HEADS - 2, -1, -1):
            out = jnp.where(head_pk == h, xs[h], out)
        return out

    def block_diag(x):
        return jnp.concatenate([jnp.where(head_pk == h, x, 0.0) for h in range(A_HEADS)], axis=0)

    def gdn_local(bi, c):
        rows = slice(c * chunk, (c + 1) * chunk)
        heads = range(A_HEADS)
        q = [qkv_s[bi, rows, h * A_DK:(h + 1) * A_DK] for h in heads]
        k = [qkv_s[bi, rows, A_KEY + h * A_DK:A_KEY + (h + 1) * A_DK] for h in heads]
        v = [qkv_s[bi, rows, 2 * A_KEY + h * A_DV:2 * A_KEY + (h + 1) * A_DV] for h in heads]
        qk = [_dot_nt(q[h], k[h]) for h in heads]
        k_bd = jnp.concatenate([jnp.where(head_of_klane == h, jnp.concatenate([k[h]] * A_HEADS, axis=1), 0.0)
                                for h in heads], axis=0)
        kk_pk = _dot_nt(jnp.concatenate(k, axis=1), k_bd)
        yield
        b_rep = [rep[bi, c][:, h * LANES:(h + 1) * LANES] for h in heads]
        g_rep = [rep[bi, c][:, (A_HEADS + h) * LANES:(A_HEADS + h + 1) * LANES] for h in heads]
        eg = [jnp.exp(g) for g in g_rep]
        for h in heads:
            g_row = _col_to_row(g_rep[h][:, 0:chunk])
            g_last = g_rep[h][chunk - 1:chunk, :]
            decay = jnp.where(tri, jnp.exp(jnp.where(tri, g_rep[h][:, 0:chunk] - g_row, 0.0)), 0.0)
            qg_s[bi, rows, h * A_DK:(h + 1) * A_DK] = q[h] * eg[h]
            kd_s[bi, rows, h * A_DK:(h + 1) * A_DK] = k[h] * jnp.exp(g_last - g_rep[h])
            ge_s[bi, c, h:h + 1, :] = jnp.exp(g_last)
            attn_s[bi, rows, h * LANES:h * LANES + chunk] = decay * qk[h]
        g_col_pk = pack_heads([widen(g) for g in g_rep])
        b_pk = pack_heads([widen(b) for b in b_rep])
        g_row_pk = jnp.sum(jnp.where(diag_pk, g_col_pk, 0.0), axis=0, keepdims=True)
        decay_pk = jnp.where(tri_pk, jnp.exp(jnp.where(tri_pk, g_col_pk - g_row_pk, 0.0)), 0.0)
        p = -jnp.where(strict_pk, b_pk * decay_pk * kk_pk, 0.0)
        t = jnp.where(diag_pk, 1.0, 0.0) + p
        m = 2
        if m < chunk:
            p = _dot(p, block_diag(p))
            yield
        while m < chunk:
            m *= 2
            if m >= chunk:
                t = t + _dot(t, block_diag(p))
            else:
                both = _dot(jnp.concatenate([p, t], axis=0), block_diag(p))
                p, t = both[0:chunk, :], t + both[chunk:2 * chunk, :]
                yield
        rhs = jnp.concatenate(
            [jnp.where(head_of_uwlane == h,
                       jnp.concatenate([b_rep[h] * v[h], (b_rep[h] * eg[h]) * k[h]] * A_HEADS, axis=1), 0.0)
             for h in heads], axis=0)
        uw_s[bi, rows, :] = _dot(t, rhs)

    def gdn_state(bi, c, h):
        rows = slice(c * chunk, (c + 1) * chunk)
        uw = uw_s[bi, rows, 2 * h * LANES:2 * (h + 1) * LANES]
        s = stg_s[bi, h]
        wq_s = _dot(jnp.concatenate([uw[:, A_DV:A_DV + A_DK], qg_s[bi, rows, h * A_DK:(h + 1) * A_DK]], axis=0), s)
        yield
        nv = uw[:, 0:A_DV] - wq_s[0:chunk, :]
        o = wq_s[chunk:2 * chunk, :] + _dot(attn_s[bi, rows, h * LANES:h * LANES + chunk], nv)
        stg_s[bi, h] = ge_s[bi, c, h:h + 1, :] * s + _dot_tn(kd_s[bi, rows, h * A_DK:(h + 1) * A_DK], nv)
        yield
        z = fa_ref[bi, rows, FA_Z + h * A_DV:FA_Z + (h + 1) * A_DV]
        o_ref[bi, rows, h * A_DV:(h + 1) * A_DV] = _rms(o, gnw) * _silu(z)

    feat_s = fb_ref
    c2 = 2 * chunk
    ri = _iota2((c2, c2), 0)
    ci = _iota2((c2, c2), 1)
    same_head = (ri // chunk) == (ci // chunk)
    incl_bd = same_head & (ri >= ci)
    strict_bd = same_head & (ri > ci)
    ri4 = _iota2((2 * c2, c2), 0)
    ci4 = _iota2((2 * c2, c2), 1)
    rr4 = ri4 % c2
    lm_mask = ((rr4 // chunk) == (ci4 // chunk)) & ((rr4 > ci4) | ((ri4 >= c2) & (rr4 == ci4)))
    lane_st = (_iota2((c2, LANES), 0) // chunk) == (_iota2((c2, LANES), 1) // B_N)
    local = {}

    def rwkv_local(bi, c, p):
        rows = slice(c * chunk, (c + 1) * chunk)

        def feat(slot):
            return feat_s[bi, rows, slot * B_W + p * LANES:slot * B_W + (p + 1) * LANES]
        r, lw, k, v, a_, b_ = (feat(s) for s in (_FR, _FLW, _FK, _FV, _FA, _FB))
        cs = _dot_exact_lhs(tri_sel, lw)
        yield
        cs_last = cs[chunk - 1:chunk, :]
        e_inv = jnp.exp(-cs)
        e_rem = jnp.exp(cs_last - cs)
        at = a_ * jnp.exp(cs - lw)
        rt = r * jnp.exp(cs)
        bt = b_ * e_inv
        kt = k * e_inv
        a_st = jnp.concatenate([jnp.where(head0, at, 0.0), jnp.where(head0, 0.0, at)], axis=0)
        r_st = jnp.concatenate([jnp.where(head0, rt, 0.0), jnp.where(head0, 0.0, rt)], axis=0)
        b2 = jnp.concatenate([bt, bt], axis=0)
        k2 = jnp.concatenate([kt, kt], axis=0)
        v2 = jnp.concatenate([v, v], axis=0)
        ar_st = jnp.concatenate([a_st, r_st], axis=0)
        if c2 % LANES == 0:
            ar_bk = _dot_nt(ar_st, jnp.concatenate([b2, k2], axis=0))
            ar_b, ar_k = ar_bk[:, 0:c2], ar_bk[:, c2:2 * c2]
        else:
            ar_b = _dot_nt(ar_st, b2)
            ar_k = _dot_nt(ar_st, k2)
        yield
        l_ab = jnp.where(strict_bd, ar_b[0:c2, :], 0.0)
        m_rb = jnp.where(incl_bd, ar_b[c2:2 * c2, :], 0.0)
        lm_k = jnp.where(lm_mask, ar_k, 0.0)
        lm_v = _dot(lm_k, v2)
        sl = slice(p * LANES, (p + 1) * LANES)
        bonus = _dot_exact_rhs(r * k * rk_ref[:, sl], ones_sel) * v
        t_inv = yield from _inv_unit_lower(l_ab, chunk, eye_2c)
        local[bi, c, p] = (t_inv, lm_v, m_rb, jnp.concatenate([at, rt], axis=0),
                           jnp.concatenate([b_ * e_rem, k * e_rem], axis=0), jnp.exp(cs_last), bonus)

    def rwkv_state(bi, c, p):
        rows = slice(c * chunk, (c + 1) * chunk)
        sl = slice(p * LANES, (p + 1) * LANES)
        t_inv, lm_v, m_rb, at_rt, bk_rem, decay_end, bonus = local.pop((bi, c, p))
        v = feat_s[bi, rows, _FV * B_W + p * LANES:_FV * B_W + (p + 1) * LANES]
        gate = feat_s[bi, rows, _FG * B_W + p * LANES:_FG * B_W + (p + 1) * LANES]
        s = str_s[bi, p]
        ar = _dot_nt(at_rt, s)
        yield
        ah = ar[0:chunk, :]
        rh = ar[chunk:c2, :]
        x_st = jnp.where(lane_st, jnp.concatenate([ah, ah], axis=0) + lm_v[0:c2, :], 0.0)
        u_st = _dot(t_inv, x_st)
        yield
        o_st = jnp.where(lane_st, _dot(m_rb, u_st) + lm_v[c2:2 * c2, :], 0.0)
        u = u_st[0:chunk, :] + u_st[chunk:c2, :]
        upd = _dot_tn(jnp.concatenate([u, v], axis=0), bk_rem)
        yield
        o = rh + o_st[0:chunk, :] + o_st[chunk:c2, :]
        str_s[bi, p] = s * decay_end + jnp.where(ones_blk, upd, 0.0)
        mean = _dot_exact_rhs(o, ones_sel) * (1.0 / B_N)
        yield
        d = o - mean
        var = _dot_exact_rhs(d * d, ones_sel) * (1.0 / B_N)
        yield
        ob = d * lax.rsqrt(var + B_GN_EPS) * lnw_ref[:, sl] + lnb_ref[:, sl] + bonus
        o_ref[bi, rows, A_VAL + p * LANES:A_VAL + (p + 1) * LANES] = ob * _silu(gate)

    tasks = []
    blocks = [(bi, c) for c in range(n_chunks) for bi in range(nb)]
    for bi, c in blocks:
        for p in range(B_PAIRS):
            deps = [("rl", bi, c, p)] + ([("rs", bi, c - 1, p)] if c else [])
            tasks.append((("rs", bi, c, p), "rwkv_state", deps, functools.partial(rwkv_state, bi, c, p)))
        for h in range(A_HEADS):
            deps = [("gl", bi, c)] + ([("gs", bi, c - 1, h)] if c else [])
            tasks.append((("gs", bi, c, h), "gdn_state", deps, functools.partial(gdn_state, bi, c, h)))
    for bi, c in blocks:
        for p in range(B_PAIRS):
            tasks.append((("rl", bi, c, p), "rwkv_local", [], functools.partial(rwkv_local, bi, c, p)))
        tasks.append((("gl", bi, c), "gdn_local", [("gg", bi, c)], functools.partial(gdn_local, bi, c)))
    for bi, c in blocks:
        tasks.append((("gg", bi, c), "gdn_gates", [], functools.partial(gdn_gates, bi, c)))
    _run_tasks(tasks, {"gdn_gates": 2, "rwkv_local": 2 * B_PAIRS, "gdn_local": 4})

    @pl.when(i == pl.num_programs(1) - 1)
    def _():
        for bi in range(nb):
            for p in range(B_PAIRS):
                srout_ref[bi, 2 * p] = str_s[bi, p, 0:B_N, 0:B_N]
                shifted = pltpu.roll(str_s[bi, p, B_N:LANES, :], B_N, 1)
                srout_ref[bi, 2 * p + 1] = shifted[:, 0:B_N]
        sgout_ref[...] = stg_s[...]


def _layer0_mixers(fa, fb, s_gdn, s_rwkv, q, nb, tb, chunk):
    bsz, seq, _ = fa.shape
    blk = lambda b, i: (b, i, 0)
    per_b4 = lambda b, i: (b, 0, 0, 0)
    fix2 = lambda b, i: (0, 0)
    kern = functools.partial(_layer0_kernel, nb=nb, tb=tb, chunk=chunk)
    vec = pl.BlockSpec((1, B_W), fix2)
    return pl.pallas_call(
        kern,
        grid=(bsz // nb, seq // tb),
        in_specs=[pl.BlockSpec((nb, tb, FA_COLS), blk),
                  pl.BlockSpec((nb, tb, FB_COLS), blk),
                  pl.BlockSpec((nb, A_HEADS, A_DK, A_DV), per_b4),
                  pl.BlockSpec((nb, B_HEADS, B_N, B_N), per_b4),
                  pl.BlockSpec((1, A_DV), fix2), vec, vec, vec],
        out_specs=[pl.BlockSpec((nb, tb, MIX0), blk),
                   pl.BlockSpec((nb, A_HEADS, A_DK, A_DV), per_b4),
                   pl.BlockSpec((nb, B_HEADS, B_N, B_N), per_b4)],
        out_shape=[jax.ShapeDtypeStruct((bsz, seq, MIX0), F32),
                   jax.ShapeDtypeStruct((bsz, A_HEADS, A_DK, A_DV), F32),
                   jax.ShapeDtypeStruct((bsz, B_HEADS, B_N, B_N), F32)],
        scratch_shapes=[pltpu.VMEM((nb, A_HEADS, A_DK, A_DV), F32),
                        pltpu.VMEM((nb, tb, A_HEADS * (A_DV + A_DK)), F32),
                        pltpu.VMEM((nb, tb, A_HEADS * LANES), F32),
                        pltpu.VMEM((nb, tb, A_KEY), F32),
                        pltpu.VMEM((nb, tb, A_KEY), F32),
                        pltpu.VMEM((nb, tb // chunk, SUBLANES, LANES), F32),
                        pltpu.VMEM((nb, B_PAIRS, LANES, LANES), F32),
                        pltpu.VMEM((B_N, LANES), F32)],
        compiler_params=pltpu.CompilerParams(dimension_semantics=("parallel", "arbitrary"),
                                             vmem_limit_bytes=VMEM_LIMIT),
        name="layer0_mixers",
    )(fa, fb, s_gdn, s_rwkv, q["gnw"], q["r_k"], q["ln_w"], q["ln_b"])


def _layer1_kernel(x_ref, mix_ref, c0_ref, n0_ref, m0_ref, wo0_ref, n1_ref, w1_ref, bias_ref, wo1_ref, nf_ref,
                   y_ref, cout_ref, nout_ref, mout_ref,
                   proj_s, x1_s, h_s, gsc_s, caug_s, m_s, *, nb, tb, chunk):
    i = pl.program_id(1)
    n_chunks = tb // chunk
    lane = _iota2((1, LANES), 1)
    krow = _iota2((LANES, 1), 0)
    n_gates = 2 * C_HEADS
    is_f = jnp.bitwise_and(lane, n_gates - 1) >= C_HEADS

    @pl.when(i == 0)
    def _():
        for bi in range(nb):
            m0 = m0_ref[bi]
            for h in range(C_HEADS):
                j = h % 2
                caug_s[bi, h] = jnp.zeros((LANES, 2 * C_DV), F32)
                caug_s[bi, h, j * C_DQK:(j + 1) * C_DQK, 0:C_DV] = c0_ref[bi, h]
                n_col = _row_to_col(n0_ref[bi, h // 2:h // 2 + 1, :])
                own = (krow // C_DQK) == j
                caug_s[bi, h, :, C_DV:2 * C_DV] = jnp.where(own, jnp.broadcast_to(n_col, (LANES, C_DV)), 0.0)
                m_s[bi, h:h + 1, :] = jnp.broadcast_to(m0[:, h:h + 1], (1, LANES))

    tri = _tri_incl(chunk)
    tri_sel = _stack_selector(tri, CUMSUM_TERMS, 1)
    sel = _replicate_selector(n_gates)
    ones_blk = jnp.ones((chunk, C_DV), F32)
    off = 2 * C_QK
    piece = 4 * LANES
    rep = {}

    def stack(ref, rows):
        return jnp.concatenate([ref[bi, rows, :] for bi in range(nb)], axis=0)

    def unstack(ref, rows, cols, val):
        for bi in range(nb):
            ref[bi, rows, cols] = val[bi * chunk:(bi + 1) * chunk, :]

    def proj_task(c):
        rows = slice(c * chunk, (c + 1) * chunk)
        x1 = stack(x_ref, rows) + jnp.dot(stack(mix_ref, rows).astype(BF16), wo0_ref[...],
                                          preferred_element_type=F32)
        unstack(x1_s, rows, slice(0, D_MODEL), x1)
        yield
        xn = _rms(x1, n1_ref[...]).astype(BF16)
        g = jnp.dot(xn, w1_ref[:, P1_GATES:P1_PAD], preferred_element_type=F32) + bias_ref[...]
        cap = C_GATE_CAP * jnp.tanh(g / C_GATE_CAP)
        unstack(gsc_s, rows, slice(0, LANES), jnp.where(is_f, -_softplus(-cap), cap))
        yield
        for j in range(0, P1_GATES, piece):
            unstack(proj_s, rows, slice(j, j + piece),
                    jnp.dot(xn, w1_ref[:, j:j + piece], preferred_element_type=F32))
            yield

    def gate_task(bi, c):
        rows = slice(c * chunk, (c + 1) * chunk)
        gl = gsc_s[bi, rows, :]
        bcum = _dot_exact_lhs(tri_sel, gl)
        yield
        rep[bi, c] = _replicate(jnp.where(is_f, bcum, gl), sel, n_gates)

    def head_chain(bi, c, h):
        rows = slice(c * chunk, (c + 1) * chunk)
        p, j = h // 2, h % 2
        own_lane = (lane // C_DQK) == j
        own_row = (krow // C_DQK) == j
        qp = proj_s[bi, rows, p * LANES:(p + 1) * LANES] * (C_DQK ** -0.5)
        kp = proj_s[bi, rows, C_QK + p * LANES:C_QK + (p + 1) * LANES]
        v = proj_s[bi, rows, off + h * C_DV:off + (h + 1) * C_DV]
        vaug = jnp.concatenate([v, ones_blk], axis=1)
        qk = _dot_nt(jnp.where(own_lane, qp, 0.0), kp)
        cm = caug_s[bi, h]
        qc = _dot(qp, cm)
        i_rep = rep[bi, c][:, h * LANES:(h + 1) * LANES]
        b_rep = rep[bi, c][:, (C_HEADS + h) * LANES:(C_HEADS + h + 1) * LANES]
        b_row = _col_to_row(b_rep[:, 0:chunk])
        i_row = _col_to_row(i_rep[:, 0:chunk])
        dmat = jnp.where(tri, b_rep[:, 0:chunk] - b_row + i_row, -jnp.inf)
        dmax = b_rep + _cummax_rows(i_rep - b_rep)
        m = m_s[bi, h:h + 1, :]
        inter = b_rep + m
        mt = jnp.maximum(inter, dmax)
        m_new = mt[chunk - 1:chunk, :]
        b_last = b_rep[chunk - 1:chunk, :]
        w_s = jnp.exp(b_last - b_rep + i_rep - m_new)
        e_c = jnp.exp(b_last + m - m_new)
        e_inter = jnp.exp(inter - mt)
        own_k = slice(j * C_DQK, (j + 1) * C_DQK)
        upd = _dot((w_s * kp).T[own_k, :], vaug)
        yield
        pm = jnp.exp(dmat - mt[:, 0:chunk]) * qk
        pv = _dot(pm, vaug)
        caug_s[bi, h, own_k, :] = jnp.concatenate([e_c, e_c], axis=1) * cm[own_k, :] + upd
        m_s[bi, h:h + 1, :] = m_new
        yield
        num = e_inter * qc[:, 0:C_DV] + pv[:, 0:C_DV]
        den = e_inter * qc[:, C_DV:2 * C_DV] + pv[:, C_DV:2 * C_DV]
        hh = num / jnp.maximum(jnp.abs(den), jnp.exp(-mt))
        og = proj_s[bi, rows, off + C_V + h * C_DV:off + C_V + (h + 1) * C_DV]
        z = proj_s[bi, rows, off + 2 * C_V + h * C_DV:off + 2 * C_V + (h + 1) * C_DV]
        h_s[bi, rows, h * C_DV:(h + 1) * C_DV] = hh * _sigmoid(og) * _silu(z)

    def final_task(c):
        rows = slice(c * chunk, (c + 1) * chunk)
        x2 = stack(x1_s, rows) + jnp.dot(stack(h_s, rows).astype(BF16), wo1_ref[...],
                                         preferred_element_type=F32)
        yield
        unstack(y_ref, rows, slice(0, D_MODEL), _rms(x2, nf_ref[...]))

    tasks = []
    for c in range(n_chunks):
        tasks.append((("p", c), "proj", [], functools.partial(proj_task, c)))
    for c in range(n_chunks):
        heads = [("h", bi, c, h) for bi in range(nb) for h in range(C_HEADS)]
        tasks.append((("f", c), "final", heads, functools.partial(final_task, c)))
        for bi in range(nb):
            tasks.append((("g", bi, c), "gates", [("p", c)], functools.partial(gate_task, bi, c)))
            for h in range(C_HEADS):
                deps = [("p", c), ("g", bi, c)] + ([("h", bi, c - 1, h)] if c else [])
                tasks.append((("h", bi, c, h), "heads", deps, functools.partial(head_chain, bi, c, h)))
    _run_tasks(tasks, {"proj": 1, "final": 1})

    @pl.when(i == pl.num_programs(1) - 1)
    def _():
        for bi in range(nb):
            for h in range(C_HEADS):
                j = h % 2
                cout_ref[bi, h] = caug_s[bi, h, j * C_DQK:(j + 1) * C_DQK, 0:C_DV]
            for p in range(C_HEADS // 2):
                n_col = caug_s[bi, 2 * p, :, C_DV:C_DV + 1] + caug_s[bi, 2 * p + 1, :, C_DV:C_DV + 1]
                nout_ref[bi, p:p + 1, :] = _col_to_row(n_col)
        mout_ref[...] = m_s[...]


def _layer1(x3d, mix3d, c0, n0, m0, q, nb, tb, chunk):
    bsz, seq, _ = x3d.shape
    blk = lambda b, i: (b, i, 0)
    per_b3 = lambda b, i: (b, 0, 0)
    per_b4 = lambda b, i: (b, 0, 0, 0)
    kern = functools.partial(_layer1_kernel, nb=nb, tb=tb, chunk=chunk)
    pairs = C_HEADS // 2
    return pl.pallas_call(
        kern,
        grid=(bsz // nb, seq // tb),
        in_specs=[pl.BlockSpec((nb, tb, D_MODEL), blk),
                  pl.BlockSpec((nb, tb, MIX0), blk),
                  pl.BlockSpec((nb, C_HEADS, C_DQK, C_DV), per_b4),
                  pl.BlockSpec((nb, pairs, LANES), per_b3),
                  pl.BlockSpec((nb, 1, C_HEADS), per_b3),
                  _resident((MIX0, D_MODEL)),
                  _resident((1, D_MODEL)),
                  _resident((D_MODEL, P1_PAD)),
                  _resident((1, LANES)),
                  _resident((C_V, D_MODEL)),
                  _resident((1, D_MODEL))],
        out_specs=[pl.BlockSpec((nb, tb, D_MODEL), blk),
                   pl.BlockSpec((nb, C_HEADS, C_DQK, C_DV), per_b4),
                   pl.BlockSpec((nb, pairs, LANES), per_b3),
                   pl.BlockSpec((nb, C_HEADS, LANES), per_b3)],
        out_shape=[jax.ShapeDtypeStruct((bsz, seq, D_MODEL), F32),
                   jax.ShapeDtypeStruct((bsz, C_HEADS, C_DQK, C_DV), F32),
                   jax.ShapeDtypeStruct((bsz, pairs, LANES), F32),
                   jax.ShapeDtypeStruct((bsz, C_HEADS, LANES), F32)],
        scratch_shapes=[pltpu.VMEM((nb, tb, P1_PAD), F32),
                        pltpu.VMEM((nb, tb, D_MODEL), F32),
                        pltpu.VMEM((nb, tb, C_V), F32),
                        pltpu.VMEM((nb, tb, LANES), F32),
                        pltpu.VMEM((nb, C_HEADS, LANES, 2 * C_DV), F32),
                        pltpu.VMEM((nb, C_HEADS, LANES), F32)],
        compiler_params=pltpu.CompilerParams(dimension_semantics=("parallel", "arbitrary"),
                                             vmem_limit_bytes=VMEM_LIMIT),
        name="layer1",
    )(x3d, mix3d, c0, n0, m0, q["wo0"], q["norm1"], q["w1"], q["gate_bias"], q["wo1"], q["normf"])


def _gate_tile(cols):
    tiled = jnp.concatenate([cols.astype(F32)] * GATE_COPIES, axis=1)
    return jnp.pad(tiled, ((0, 0), (0, LANES - tiled.shape[1])))


def _prepare_params(p):
    w_in0 = p["w_in0"]
    a_cols = A_MAIN + 2 * A_HEADS
    w0 = jnp.concatenate([w_in0[:, 0:A_MAIN], _gate_tile(w_in0[:, A_MAIN:a_cols]), w_in0[:, a_cols:]], axis=1)
    w_in1 = p["w_in1"]
    w1 = jnp.concatenate([w_in1[:, 0:P1_GATES], _gate_tile(w_in1[:, P1_GATES:])], axis=1)
    zeros_h = jnp.zeros((A_HEADS,), F32)
    alog = _gate_tile(jnp.concatenate([zeros_h, p["gdn_a_log"]]).reshape(1, -1))
    dtb = _gate_tile(jnp.concatenate([zeros_h, p["gdn_dt_bias"]]).reshape(1, -1))
    gate_bias = _gate_tile(jnp.concatenate([p["mlstm_b_i"], p["mlstm_b_f"]]).reshape(1, -1))
    w2a2 = jnp.zeros((2 * B_LORA, 2 * B_W), F32)
    w2a2 = w2a2.at[0:B_LORA, 0:B_W].set(p["rwkv_w2"]).at[B_LORA:, B_W:].set(p["rwkv_a2"])
    row = lambda v: v.reshape(1, -1).astype(F32)
    return dict(
        w0=w0.astype(BF16), w1=w1.astype(BF16),
        wo0=p["w_out0"].astype(BF16), wo1=p["w_out1"].astype(BF16), w2a2=w2a2.astype(BF16),
        norm0=row(p["norm0_w"]), norm1=row(p["norm1_w"]), normf=row(p["norm_f_w"]),
        conv_w=p["gdn_conv_w"].astype(F32), alog=alog, dtb=dtb, gnw=row(p["gdn_norm_w"]),
        mu=row(p["rwkv_mu"]), rw0=row(p["rwkv_w0"]), ra0=row(p["rwkv_a0"]),
        k_k=row(p["rwkv_k_k"]), k_a=row(p["rwkv_k_a"]), r_k=row(p["rwkv_r_k"]),
        ln_w=row(p["rwkv_ln_w"]), ln_b=row(p["rwkv_ln_b"]), gate_bias=gate_bias,
    )


def _tiles(bsz, seq):
    chunk = math.gcd(seq, MAX_CHUNK)
    tb = min(seq, 256)
    nb = 2 if seq > MAX_CHUNK else 8
    nb = math.gcd(nb, bsz)
    tt = min(seq, PROJ_ROWS)
    ns = math.gcd(min(PROJ_ROWS // tt, PROJ_MAX_SEQS), bsz)
    return chunk, tb, nb, tt, ns


def _trunk(x, state, q):
    conv0, s_gdn, x_prev, s_rwkv, c0, n0, m0 = state
    bsz, seq, _ = x.shape
    chunk, tb, nb, tt, ns = _tiles(bsz, seq)
    fa, fb, conv_new, xn_last = _proj0(x, conv0, x_prev.reshape(bsz, 1, D_MODEL), q, ns, tt)
    mix, s_gdn_new, s_rwkv_new = _layer0_mixers(fa, fb, s_gdn, s_rwkv, q, nb, tb, chunk)
    y, c_new, n_new, m_new = _layer1(x, mix, c0, n0.reshape(bsz, C_HEADS // 2, LANES),
                                     m0.reshape(bsz, 1, C_HEADS), q, nb, tb, chunk)
    return y, (conv_new, s_gdn_new, xn_last.reshape(bsz, D_MODEL), s_rwkv_new, c_new,
               n_new.reshape(bsz, C_HEADS, C_DQK), m_new[:, :, 0])


def kernel(x_prompt, x_sample, state_gdn_conv, state_gdn, state_rwkv_shift, state_rwkv,
           state_mlstm_c, state_mlstm_n, state_mlstm_m,
           norm0_w, w_in0, w_out0, gdn_conv_w, gdn_a_log, gdn_dt_bias, gdn_norm_w,
           rwkv_mu, rwkv_w0, rwkv_w2, rwkv_a0, rwkv_a2, rwkv_k_k, rwkv_k_a, rwkv_r_k,
           rwkv_ln_w, rwkv_ln_b, norm1_w, w_in1, w_out1, mlstm_b_i, mlstm_b_f, norm_f_w):
    q = _prepare_params(dict(
        norm0_w=norm0_w, w_in0=w_in0, w_out0=w_out0, gdn_conv_w=gdn_conv_w, gdn_a_log=gdn_a_log,
        gdn_dt_bias=gdn_dt_bias, gdn_norm_w=gdn_norm_w, rwkv_mu=rwkv_mu, rwkv_w0=rwkv_w0,
        rwkv_w2=rwkv_w2, rwkv_a0=rwkv_a0, rwkv_a2=rwkv_a2, rwkv_k_k=rwkv_k_k, rwkv_k_a=rwkv_k_a,
        rwkv_r_k=rwkv_r_k, rwkv_ln_w=rwkv_ln_w, rwkv_ln_b=rwkv_ln_b, norm1_w=norm1_w, w_in1=w_in1,
        w_out1=w_out1, mlstm_b_i=mlstm_b_i, mlstm_b_f=mlstm_b_f, norm_f_w=norm_f_w))
    bsz = x_prompt.shape[0]
    prompt_state = (
        jnp.zeros((bsz, A_CONV - 1, A_QKV), F32),
        jnp.zeros((bsz, A_HEADS, A_DK, A_DV), F32),
        jnp.zeros((bsz, D_MODEL), F32),
        jnp.zeros((bsz, B_HEADS, B_N, B_N), F32),
        jnp.zeros((bsz, C_HEADS, C_DQK, C_DV), F32),
        jnp.zeros((bsz, C_HEADS, C_DQK), F32),
        jnp.zeros((bsz, C_HEADS), F32),
    )
    y_p, sp = _trunk(x_prompt, prompt_state, q)
    sample_state = (state_gdn_conv, state_gdn, state_rwkv_shift, state_rwkv,
                    state_mlstm_c, state_mlstm_n, state_mlstm_m)
    y_s, ss = _trunk(x_sample, sample_state, q)
    return (y_p, y_s) + tuple(sp) + tuple(ss)
```

```python
import functools
import math

import jax
import jax.numpy as jnp
from jax import lax
from jax.experimental import pallas as pl
from jax.experimental.pallas import tpu as pltpu

F32 = jnp.float32
BF16 = jnp.bfloat16

D_MODEL = 1024
EPS = 1e-6
LANES = 128
SUBLANES = 8
A_HEADS, A_DK, A_DV, A_CONV = 4, 128, 128, 4
A_KEY = A_HEADS * A_DK
A_VAL = A_HEADS * A_DV
A_QKV = 2 * A_KEY + A_VAL
A_MAIN = A_QKV + A_VAL
B_HEADS, B_N, B_LORA = 8, 64, 64
B_W = B_HEADS * B_N
B_COLS = 4 * B_W + 2 * B_LORA
B_PAIRS = B_HEADS // 2
B_GN_EPS = 64e-5
B_DECAY_SCALE = math.exp(-0.5)
MIX0 = A_VAL + B_W
C_HEADS, C_DQK, C_DV = 8, 64, 128
C_QK = C_HEADS * C_DQK
C_V = C_HEADS * C_DV
C_GATE_CAP = 15.0
P0_GATES = A_MAIN
P0_B = A_MAIN + LANES
P0_PAD = P0_B + B_COLS
P1_GATES = 2 * C_QK + 3 * C_V
P1_PAD = P1_GATES + LANES
MAX_CHUNK = 64
GATE_COPIES = 3
CUMSUM_TERMS = 3
SUM_TERMS = 2
VMEM_LIMIT = 56 * 1024 * 1024


def _dot(a, b):
    return jnp.dot(a.astype(BF16), b.astype(BF16), preferred_element_type=F32)


def _dot_nt(a, b):
    return lax.dot_general(a.astype(BF16), b.astype(BF16), (((1,), (1,)), ((), ())),
                           preferred_element_type=F32)


def _dot_tn(a, b):
    return lax.dot_general(a.astype(BF16), b.astype(BF16), (((0,), (0,)), ((), ())),
                           preferred_element_type=F32)


def _split(x, terms):
    out = []
    for _ in range(terms - 1):
        hi = x.astype(BF16)
        out.append(hi)
        x = x - hi.astype(F32)
    out.append(x.astype(BF16))
    return out


def _stack_selector(sel, terms, axis):
    s = jnp.where(sel, 1.0, 0.0).astype(BF16)
    return jnp.concatenate([s] * terms, axis=axis)


def _dot_exact_lhs(sel_stacked, x, terms=CUMSUM_TERMS):
    return jnp.dot(sel_stacked, jnp.concatenate(_split(x, terms), axis=0), preferred_element_type=F32)


def _dot_exact_rhs(x, sel_stacked, terms=SUM_TERMS):
    return jnp.dot(jnp.concatenate(_split(x, terms), axis=1), sel_stacked, preferred_element_type=F32)


def _iota2(shape, dim):
    return lax.broadcasted_iota(jnp.int32, shape, dim)


def _tri_incl(n):
    return (_iota2((n, n), 0) >= _iota2((n, n), 1))


def _eye(n):
    return jnp.where(_iota2((n, n), 0) == _iota2((n, n), 1), 1.0, 0.0)


def _col_to_row(col):
    n = col.shape[0]
    eye = _iota2((n, n), 0) == _iota2((n, n), 1)
    return jnp.sum(jnp.where(eye, col, 0.0), axis=0, keepdims=True)


def _row_to_col(row):
    n = row.shape[1]
    eye = _iota2((n, n), 0) == _iota2((n, n), 1)
    return jnp.sum(jnp.where(eye, row, 0.0), axis=1, keepdims=True)


def _replicate(x, sel, group):
    hi, mid, lo = _split(x, GATE_COPIES)
    lane = _iota2(x.shape, 1)
    packed = jnp.where(lane < group, hi, jnp.where(lane < 2 * group, mid, lo))
    packed = jnp.where(lane < GATE_COPIES * group, packed, jnp.zeros_like(packed))
    return jnp.dot(packed, sel, preferred_element_type=F32)


def _replicate_selector(group):
    assert group & (group - 1) == 0 and LANES == 128
    k = _iota2((LANES, group * LANES), 0)
    col = _iota2((LANES, group * LANES), 1)
    hit = (k < GATE_COPIES * group) & (jnp.bitwise_and(k, group - 1) == jnp.right_shift(col, 7))
    return jnp.where(hit, 1.0, 0.0).astype(BF16)


def _cummax_rows(x):
    n = x.shape[0]
    row = _iota2(x.shape, 0)
    sh = 1
    while sh < n:
        x = jnp.maximum(x, jnp.where(row >= sh, pltpu.roll(x, sh, 0), -jnp.inf))
        sh *= 2
    return x


def _inv_unit_lower(p, levels, eye):
    n = p.shape[0]
    t = eye + p
    m = 2
    if m < levels:
        p = _dot(p, p)
        yield
    while m < levels:
        m *= 2
        if m >= levels:
            t = t + _dot(p, t)
        elif n % LANES == 0:
            both = _dot(p, jnp.concatenate([p, t], axis=1))
            p, t = both[:, 0:n], t + both[:, n:2 * n]
            yield
        else:
            both = _dot(jnp.concatenate([p, t], axis=0), p)
            p, t = both[0:n, :], t + both[n:2 * n, :]
            yield
    return t


def _run_tasks(tasks, caps):
    done, live, pending = set(), [], list(tasks)
    while pending or live:
        count = {}
        for _, group, _ in live:
            count[group] = count.get(group, 0) + 1
        still = []
        for key, group, deps, make in pending:
            if all(d in done for d in deps) and count.get(group, 0) < caps.get(group, len(tasks)):
                live.append((key, group, make()))
                count[group] = count.get(group, 0) + 1
            else:
                still.append((key, group, deps, make))
        assert live, "task graph has a cycle or a missing dependency"
        pending = still
        nxt = []
        for key, group, gen in live:
            try:
                next(gen)
                nxt.append((key, group, gen))
            except StopIteration:
                done.add(key)
        live = nxt


def _sigmoid(x):
    return 1.0 / (1.0 + jnp.exp(-x))


def _silu(x):
    return x * _sigmoid(x)


def _softplus(x):
    return jnp.maximum(x, 0.0) + jnp.log1p(jnp.exp(-jnp.abs(x)))


def _rms(x, w):
    return x * lax.rsqrt(jnp.mean(x * x, axis=-1, keepdims=True) + EPS) * w


def _resident(shape):
    zeros = (0,) * len(shape)
    return pl.BlockSpec(shape, lambda *_: zeros, pipeline_mode=pl.Buffered(1))


FA_Z = A_QKV
FA_BETA = A_MAIN
FA_DECAY = A_MAIN + LANES
FA_COLS = A_MAIN + 2 * LANES
_FR, _FLW, _FK, _FV, _FA, _FB, _FG = range(7)
FB_COLS = 7 * B_W
SUB_ROWS = 64
PROJ_ROWS = 512
PROJ_MAX_SEQS = 32


def _proj0_kernel(x_ref, conv0_ref, prev_ref, nw_ref, w_ref, convw_ref, alog_ref, dtb_ref,
                  mu_ref, w0_ref, a0_ref, w2a2_ref, kk_ref, ka_ref,
                  fa_ref, fb_ref, convn_ref, ext_a, ext_b, *, ns, tt):
    i = pl.program_id(1)
    tail = SUBLANES
    rows = ns * tt
    lane = _iota2((1, LANES), 1)
    head0 = lane < B_N
    ones_blk = (_iota2((LANES, LANES), 0) // B_N == _iota2((LANES, LANES), 1) // B_N)
    ones_sel = _stack_selector(ones_blk, SUM_TERMS, 0)

    @pl.when(i == 0)
    def _():
        ext_a[:, 0:tail, :] = jnp.zeros((ns, tail, A_QKV), F32)
        ext_a[:, tail - (A_CONV - 1):tail, :] = conv0_ref[...]
        ext_b[:, 0:tail, :] = jnp.zeros((ns, tail, B_COLS), F32)
        ext_b[:, tail - 1:tail, :] = prev_ref[...]

    @pl.when(i > 0)
    def _():
        ext_a[:, 0:tail, :] = ext_a[:, tt:tt + tail, :]
        ext_b[:, 0:tail, :] = ext_b[:, tt:tt + tail, :]

    xn = _rms(x_ref[...].reshape(rows, D_MODEL), nw_ref[...]).astype(BF16)
    piece = 4 * LANES

    def cols(lo, hi):
        return jnp.dot(xn, w_ref[:, lo:hi], preferred_element_type=F32).reshape(ns, tt, hi - lo)

    for lo in range(0, A_QKV, piece):
        ext_a[:, tail:tail + tt, lo:lo + piece] = cols(lo, lo + piece)
    fa_ref[:, :, FA_Z:FA_Z + A_VAL] = cols(A_QKV, A_MAIN)
    g = cols(P0_GATES, P0_B)
    fa_ref[:, :, FA_BETA:FA_BETA + LANES] = _sigmoid(g)
    fa_ref[:, :, FA_DECAY:FA_DECAY + LANES] = -jnp.exp(alog_ref[...]) * _softplus(g + dtb_ref[...])
    for lo in range(0, B_COLS, piece):
        hi = min(lo + piece, B_COLS)
        ext_b[:, tail:tail + tt, lo:hi] = cols(P0_B + lo, P0_B + hi)

    sub_t = min(tt, SUB_ROWS)
    sub_s = min(ns, SUB_ROWS // sub_t)
    cw = convw_ref[...]
    mu = mu_ref[...]
    for s0 in range(0, ns, sub_s):
        sq = slice(s0, s0 + sub_s)
        for t0 in range(0, tt, sub_t):
            tr = slice(t0, t0 + sub_t)
            y = ext_a[sq, tail + t0:tail + t0 + sub_t, :] * cw[A_CONV - 1:A_CONV, :]
            for j in range(A_CONV - 1):
                sh = A_CONV - 1 - j
                y = y + ext_a[sq, tail + t0 - sh:tail + t0 - sh + sub_t, :] * cw[j:j + 1, :]
            y = _silu(y)
            for h in range(A_HEADS):
                q = y[:, :, h * A_DK:(h + 1) * A_DK]
                fa_ref[sq, tr, h * A_DK:(h + 1) * A_DK] = (
                    q * lax.rsqrt(jnp.sum(q * q, axis=-1, keepdims=True) + EPS) * (A_DK ** -0.5))
                k = y[:, :, A_KEY + h * A_DK:A_KEY + (h + 1) * A_DK]
                fa_ref[sq, tr, A_KEY + h * A_DK:A_KEY + (h + 1) * A_DK] = (
                    k * lax.rsqrt(jnp.sum(k * k, axis=-1, keepdims=True) + EPS))
            fa_ref[sq, tr, 2 * A_KEY:A_QKV] = y[:, :, 2 * A_KEY:A_QKV]
            pb = ext_b[sq, tail + t0:tail + t0 + sub_t, :]
            pb_prev = ext_b[sq, tail + t0 - 1:tail + t0 - 1 + sub_t, :]
            pbs = (pb + (pb_prev - pb) * mu).reshape(sub_s * sub_t, B_COLS)
            lo_ = pbs[:, 4 * B_W:B_COLS]
            lo_ = jnp.where(head0, jnp.tanh(lo_), lo_)
            wa = _dot(lo_, w2a2_ref[...])
            logw = -B_DECAY_SCALE * _sigmoid(w0_ref[...] + wa[:, 0:B_W])
            a = _sigmoid(a0_ref[...] + wa[:, B_W:2 * B_W])
            kb = pbs[:, B_W:2 * B_W]
            kkx = kb * kk_ref[...]

            def put(slot, val):
                fb_ref[sq, tr, slot * B_W:(slot + 1) * B_W] = val.reshape(sub_s, sub_t, B_W)

            kkn = jnp.concatenate(
                [kkx[:, p * LANES:(p + 1) * LANES]
                 * lax.rsqrt(_dot_exact_rhs(kkx[:, p * LANES:(p + 1) * LANES] ** 2, ones_sel) + EPS)
                 for p in range(B_PAIRS)], axis=1)
            put(_FR, pbs[:, 0:B_W])
            put(_FLW, logw)
            put(_FK, kb * (1.0 + (a - 1.0) * ka_ref[...]))
            put(_FV, pbs[:, 2 * B_W:3 * B_W])
            put(_FA, -kkn)
            put(_FB, kkn * a)
            put(_FG, pbs[:, 3 * B_W:4 * B_W])

    @pl.when(i == pl.num_programs(1) - 1)
    def _():
        convn_ref[...] = ext_a[:, tt + tail - (A_CONV - 1):tt + tail, :]


def _proj0(x3d, conv0, prev_row, q, ns, tt):
    bsz, seq, _ = x3d.shape
    blk = lambda b, i: (b, i, 0)
    per_b3 = lambda b, i: (b, 0, 0)
    kern = functools.partial(_proj0_kernel, ns=ns, tt=tt)
    return pl.pallas_call(
        kern,
        grid=(bsz // ns, seq // tt),
        in_specs=[pl.BlockSpec((ns, tt, D_MODEL), blk),
                  pl.BlockSpec((ns, A_CONV - 1, A_QKV), per_b3),
                  pl.BlockSpec((ns, 1, B_COLS), per_b3),
                  _resident((1, D_MODEL)),
                  _resident((D_MODEL, P0_PAD)),
                  _resident((A_CONV, A_QKV)), _resident((1, LANES)), _resident((1, LANES)),
                  _resident((1, B_COLS)), _resident((1, B_W)), _resident((1, B_W)),
                  _resident((2 * B_LORA, 2 * B_W)), _resident((1, B_W)), _resident((1, B_W))],
        out_specs=[pl.BlockSpec((ns, tt, FA_COLS), blk),
                   pl.BlockSpec((ns, tt, FB_COLS), blk),
                   pl.BlockSpec((ns, A_CONV - 1, A_QKV), per_b3)],
        out_shape=[jax.ShapeDtypeStruct((bsz, seq, FA_COLS), F32),
                   jax.ShapeDtypeStruct((bsz, seq, FB_COLS), F32),
                   jax.ShapeDtypeStruct((bsz, A_CONV - 1, A_QKV), F32)],
        scratch_shapes=[pltpu.VMEM((ns, tt + SUBLANES, A_QKV), F32),
                        pltpu.VMEM((ns, tt + SUBLANES, B_COLS), F32)],
        compiler_params=pltpu.CompilerParams(dimension_semantics=("parallel", "arbitrary"),
                                             vmem_limit_bytes=VMEM_LIMIT),
        name="proj0_features",
    )(x3d, conv0, prev_row, q["norm0"], q["w0"], q["conv_w"], q["alog"], q["dtb"],
      q["mu"], q["rw0"], q["ra0"], q["w2a2"], q["k_k"], q["k_a"])


def _aux_kernel(xl_ref, xp_ref, nw_ref, wb_ref, xn_ref, prev_ref):
    xn_ref[...] = _rms(xl_ref[...], nw_ref[...])
    prev_ref[...] = jnp.dot(xp_ref[...].astype(BF16), wb_ref[:, P0_B:P0_PAD], preferred_element_type=F32)


def _aux(x_last, x_prev, norm_w, wb):
    b = x_last.shape[0]
    return pl.pallas_call(
        _aux_kernel,
        out_shape=[jax.ShapeDtypeStruct((b, D_MODEL), F32), jax.ShapeDtypeStruct((b, B_COLS), F32)],
        compiler_params=pltpu.CompilerParams(vmem_limit_bytes=VMEM_LIMIT),
        name="shift_rows",
    )(x_last, x_prev, norm_w, wb)


def _layer0_kernel(fa_ref, fb_ref, sg0_ref, sr0_ref, gnw_ref, rk_ref, lnw_ref, lnb_ref,
                   o_ref, sgout_ref, srout_ref,
                   stg_s, uw_s, attn_s, qg_s, kd_s, ge_s, str_s, tmp_s, *, nb, tb, chunk):
    i = pl.program_id(1)
    n_chunks = tb // chunk
    lane = _iota2((1, LANES), 1)
    head0 = lane < B_N
    ones_blk = (_iota2((LANES, LANES), 0) // B_N == _iota2((LANES, LANES), 1) // B_N)

    @pl.when(i == 0)
    def _():
        for bi in range(nb):
            for p in range(B_PAIRS):
                tmp_s[...] = jnp.zeros((B_N, LANES), F32)
                tmp_s[:, 0:B_N] = sr0_ref[bi, 2 * p + 1]
                shifted = pltpu.roll(tmp_s[...], B_N, 1)
                str_s[bi, p, B_N:LANES, :] = jnp.where(head0, 0.0, shifted)
                tmp_s[:, 0:B_N] = sr0_ref[bi, 2 * p]
                str_s[bi, p, 0:B_N, :] = tmp_s[...]
        stg_s[...] = sg0_ref[...]

    tri = _tri_incl(chunk)
    tri_sel = _stack_selector(tri, CUMSUM_TERMS, 1)
    ones_sel = _stack_selector(ones_blk, SUM_TERMS, 0)
    eye_2c = _eye(2 * chunk)

    n_gates = 2 * A_HEADS
    is_decay = jnp.bitwise_and(lane, n_gates - 1) >= A_HEADS
    sel = _replicate_selector(n_gates)
    gnw = gnw_ref[...]
    qkv_s = fa_ref
    rep = {}

    def gdn_gates(bi, c):
        rows = slice(c * chunk, (c + 1) * chunk)
        gcum = _dot_exact_lhs(tri_sel, fa_ref[bi, rows, FA_DECAY:FA_DECAY + LANES])
        yield
        rep[bi, c] = _replicate(jnp.where(is_decay, gcum, fa_ref[bi, rows, FA_BETA:FA_BETA + LANES]),
                                sel, n_gates)

    wpk = A_HEADS * chunk
    head_pk = _iota2((1, wpk), 1) // chunk
    s_pk = _iota2((1, wpk), 1) - head_pk * chunk
    t_pk = _iota2((chunk, wpk), 0)
    diag_pk = t_pk == s_pk
    tri_pk = t_pk >= s_pk
    strict_pk = t_pk > s_pk
    head_of_klane = _iota2((1, A_KEY), 1) // A_DK
    head_of_uwlane = _iota2((1, A_HEADS * (A_DV + A_DK)), 1) // (A_DV + A_DK)

    def widen(x):
        reps = -(-wpk // LANES)
        w = jnp.concatenate([x] * reps, axis=1) if reps > 1 else x
        return w if w.shape[1] == wpk else w[:, 0:wpk]

    def pack_heads(xs):
        out = xs[A_HEADS - 1]
        for h in range(A_HEADS - 2, -1, -1):
            out = jnp.where(head_pk == h, xs[h], out)
        return out

    def block_diag(x):
        return jnp.concatenate([jnp.where(head_pk == h, x, 0.0) for h in range(A_HEADS)], axis=0)

    def gdn_local(bi, c):
        rows = slice(c * chunk, (c + 1) * chunk)
        heads = range(A_HEADS)
        q = [qkv_s[bi, rows, h * A_DK:(h + 1) * A_DK] for h in heads]
        k = [qkv_s[bi, rows, A_KEY + h * A_DK:A_KEY + (h + 1) * A_DK] for h in heads]
        v = [qkv_s[bi, rows, 2 * A_KEY + h * A_DV:2 * A_KEY + (h + 1) * A_DV] for h in heads]
        qk = [_dot_nt(q[h], k[h]) for h in heads]
        k_bd = jnp.concatenate([jnp.where(head_of_klane == h, jnp.concatenate([k[h]] * A_HEADS, axis=1), 0.0)
                                for h in heads], axis=0)
        kk_pk = _dot_nt(jnp.concatenate(k, axis=1), k_bd)
        yield
        b_rep = [rep[bi, c][:, h * LANES:(h + 1) * LANES] for h in heads]
        g_rep = [rep[bi, c][:, (A_HEADS + h) * LANES:(A_HEADS + h + 1) * LANES] for h in heads]
        eg = [jnp.exp(g) for g in g_rep]
        for h in heads:
            g_row = _col_to_row(g_rep[h][:, 0:chunk])
            g_last = g_rep[h][chunk - 1:chunk, :]
            decay = jnp.where(tri, jnp.exp(jnp.where(tri, g_rep[h][:, 0:chunk] - g_row, 0.0)), 0.0)
            qg_s[bi, rows, h * A_DK:(h + 1) * A_DK] = q[h] * eg[h]
            kd_s[bi, rows, h * A_DK:(h + 1) * A_DK] = k[h] * jnp.exp(g_last - g_rep[h])
            ge_s[bi, c, h:h + 1, :] = jnp.exp(g_last)
            attn_s[bi, rows, h * LANES:h * LANES + chunk] = decay * qk[h]
        g_col_pk = pack_heads([widen(g) for g in g_rep])
        b_pk = pack_heads([widen(b) for b in b_rep])
        g_row_pk = jnp.sum(jnp.where(diag_pk, g_col_pk, 0.0), axis=0, keepdims=True)
        decay_pk = jnp.where(tri_pk, jnp.exp(jnp.where(tri_pk, g_col_pk - g_row_pk, 0.0)), 0.0)
        p = -jnp.where(strict_pk, b_pk * decay_pk * kk_pk, 0.0)
        t = jnp.where(diag_pk, 1.0, 0.0) + p
        m = 2
        if m < chunk:
            p = _dot(p, block_diag(p))
            yield
        while m < chunk:
            m *= 2
            if m >= chunk:
                t = t + _dot(t, block_diag(p))
            else:
                both = _dot(jnp.concatenate([p, t], axis=0), block_diag(p))
                p, t = both[0:chunk, :], t + both[chunk:2 * chunk, :]
                yield
        rhs = jnp.concatenate(
            [jnp.where(head_of_uwlane == h,
                       jnp.concatenate([b_rep[h] * v[h], (b_rep[h] * eg[h]) * k[h]] * A_HEADS, axis=1), 0.0)
             for h in heads], axis=0)
        uw_s[bi, rows, :] = _dot(t, rhs)

    def gdn_state(bi, c, h):
        rows = slice(c * chunk, (c + 1) * chunk)
        uw = uw_s[bi, rows, 2 * h * LANES:2 * (h + 1) * LANES]
        s = stg_s[bi, h]
        wq_s = _dot(jnp.concatenate([uw[:, A_DV:A_DV + A_DK], qg_s[bi, rows, h * A_DK:(h + 1) * A_DK]], axis=0), s)
        yield
        nv = uw[:, 0:A_DV] - wq_s[0:chunk, :]
        o = wq_s[chunk:2 * chunk, :] + _dot(attn_s[bi, rows, h * LANES:h * LANES + chunk], nv)
        stg_s[bi, h] = ge_s[bi, c, h:h + 1, :] * s + _dot_tn(kd_s[bi, rows, h * A_DK:(h + 1) * A_DK], nv)
        yield
        z = fa_ref[bi, rows, FA_Z + h * A_DV:FA_Z + (h + 1) * A_DV]
        o_ref[bi, rows, h * A_DV:(h + 1) * A_DV] = _rms(o, gnw) * _silu(z)

    feat_s = fb_ref
    c2 = 2 * chunk
    ri = _iota2((c2, c2), 0)
    ci = _iota2((c2, c2), 1)
    same_head = (ri // chunk) == (ci // chunk)
    incl_bd = same_head & (ri >= ci)
    strict_bd = same_head & (ri > ci)
    ri4 = _iota2((2 * c2, c2), 0)
    ci4 = _iota2((2 * c2, c2), 1)
    rr4 = ri4 % c2
    lm_mask = ((rr4 // chunk) == (ci4 // chunk)) & ((rr4 > ci4) | ((ri4 >= c2) & (rr4 == ci4)))
    lane_st = (_iota2((c2, LANES), 0) // chunk) == (_iota2((c2, LANES), 1) // B_N)
    local = {}

    def rwkv_local(bi, c, p):
        rows = slice(c * chunk, (c + 1) * chunk)

        def feat(slot):
            return feat_s[bi, rows, slot * B_W + p * LANES:slot * B_W + (p + 1) * LANES]
        r, lw, k, v, a_, b_ = (feat(s) for s in (_FR, _FLW, _FK, _FV, _FA, _FB))
        cs = _dot_exact_lhs(tri_sel, lw)
        yield
        cs_last = cs[chunk - 1:chunk, :]
        e_inv = jnp.exp(-cs)
        e_rem = jnp.exp(cs_last - cs)
        at = a_ * jnp.exp(cs - lw)
        rt = r * jnp.exp(cs)
        bt = b_ * e_inv
        kt = k * e_inv
        a_st = jnp.concatenate([jnp.where(head0, at, 0.0), jnp.where(head0, 0.0, at)], axis=0)
        r_st = jnp.concatenate([jnp.where(head0, rt, 0.0), jnp.where(head0, 0.0, rt)], axis=0)
        b2 = jnp.concatenate([bt, bt], axis=0)
        k2 = jnp.concatenate([kt, kt], axis=0)
        v2 = jnp.concatenate([v, v], axis=0)
        ar_st = jnp.concatenate([a_st, r_st], axis=0)
        if c2 % LANES == 0:
            ar_bk = _dot_nt(ar_st, jnp.concatenate([b2, k2], axis=0))
            ar_b, ar_k = ar_bk[:, 0:c2], ar_bk[:, c2:2 * c2]
        else:
            ar_b = _dot_nt(ar_st, b2)
            ar_k = _dot_nt(ar_st, k2)
        yield
        l_ab = jnp.where(strict_bd, ar_b[0:c2, :], 0.0)
        m_rb = jnp.where(incl_bd, ar_b[c2:2 * c2, :], 0.0)
        lm_k = jnp.where(lm_mask, ar_k, 0.0)
        lm_v = _dot(lm_k, v2)
        sl = slice(p * LANES, (p + 1) * LANES)
        bonus = _dot_exact_rhs(r * k * rk_ref[:, sl], ones_sel) * v
        t_inv = yield from _inv_unit_lower(l_ab, chunk, eye_2c)
        local[bi, c, p] = (t_inv, lm_v, m_rb, jnp.concatenate([at, rt], axis=0),
                           jnp.concatenate([b_ * e_rem, k * e_rem], axis=0), jnp.exp(cs_last), bonus)

    def rwkv_state(bi, c, p):
        rows = slice(c * chunk, (c + 1) * chunk)
        sl = slice(p * LANES, (p + 1) * LANES)
        t_inv, lm_v, m_rb, at_rt, bk_rem, decay_end, bonus = local.pop((bi, c, p))
        v = feat_s[bi, rows, _FV * B_W + p * LANES:_FV * B_W + (p + 1) * LANES]
        gate = feat_s[bi, rows, _FG * B_W + p * LANES:_FG * B_W + (p + 1) * LANES]
        s = str_s[bi, p]
        ar = _dot_nt(at_rt, s)
        yield
        ah = ar[0:chunk, :]
        rh = ar[chunk:c2, :]
        x_st = jnp.where(lane_st, jnp.concatenate([ah, ah], axis=0) + lm_v[0:c2, :], 0.0)
        u_st = _dot(t_inv, x_st)
        yield
        o_st = jnp.where(lane_st, _dot(m_rb, u_st) + lm_v[c2:2 * c2, :], 0.0)
        u = u_st[0:chunk, :] + u_st[chunk:c2, :]
        upd = _dot_tn(jnp.concatenate([u, v], axis=0), bk_rem)
        yield
        o = rh + o_st[0:chunk, :] + o_st[chunk:c2, :]
        str_s[bi, p] = s * decay_end + jnp.where(ones_blk, upd, 0.0)
        mean = _dot_exact_rhs(o, ones_sel) * (1.0 / B_N)
        yield
        d = o - mean
        var = _dot_exact_rhs(d * d, ones_sel) * (1.0 / B_N)
        yield
        ob = d * lax.rsqrt(var + B_GN_EPS) * lnw_ref[:, sl] + lnb_ref[:, sl] + bonus
        o_ref[bi, rows, A_VAL + p * LANES:A_VAL + (p + 1) * LANES] = ob * _silu(gate)

    tasks = []
    blocks = [(bi, c) for c in range(n_chunks) for bi in range(nb)]
    for bi, c in blocks:
        for p in range(B_PAIRS):
            deps = [("rl", bi, c, p)] + ([("rs", bi, c - 1, p)] if c else [])
            tasks.append((("rs", bi, c, p), "rwkv_state", deps, functools.partial(rwkv_state, bi, c, p)))
        for h in range(A_HEADS):
            deps = [("gl", bi, c)] + ([("gs", bi, c - 1, h)] if c else [])
            tasks.append((("gs", bi, c, h), "gdn_state", deps, functools.partial(gdn_state, bi, c, h)))
    for bi, c in blocks:
        for p in range(B_PAIRS):
            tasks.append((("rl", bi, c, p), "rwkv_local", [], functools.partial(rwkv_local, bi, c, p)))
        tasks.append((("gl", bi, c), "gdn_local", [("gg", bi, c)], functools.partial(gdn_local, bi, c)))
    for bi, c in blocks:
        tasks.append((("gg", bi, c), "gdn_gates", [], functools.partial(gdn_gates, bi, c)))
    _run_tasks(tasks, {"gdn_gates": 2, "rwkv_local": 2 * B_PAIRS, "gdn_local": 4})

    @pl.when(i == pl.num_programs(1) - 1)
    def _():
        for bi in range(nb):
            for p in range(B_PAIRS):
                srout_ref[bi, 2 * p] = str_s[bi, p, 0:B_N, 0:B_N]
                shifted = pltpu.roll(str_s[bi, p, B_N:LANES, :], B_N, 1)
                srout_ref[bi, 2 * p + 1] = shifted[:, 0:B_N]
        sgout_ref[...] = stg_s[...]


def _layer0_mixers(fa, fb, s_gdn, s_rwkv, q, nb, tb, chunk):
    bsz, seq, _ = fa.shape
    blk = lambda b, i: (b, i, 0)
    per_b4 = lambda b, i: (b, 0, 0, 0)
    fix2 = lambda b, i: (0, 0)
    kern = functools.partial(_layer0_kernel, nb=nb, tb=tb, chunk=chunk)
    vec = pl.BlockSpec((1, B_W), fix2)
    return pl.pallas_call(
        kern,
        grid=(bsz // nb, seq // tb),
        in_specs=[pl.BlockSpec((nb, tb, FA_COLS), blk),
                  pl.BlockSpec((nb, tb, FB_COLS), blk),
                  pl.BlockSpec((nb, A_HEADS, A_DK, A_DV), per_b4),
                  pl.BlockSpec((nb, B_HEADS, B_N, B_N), per_b4),
                  pl.BlockSpec((1, A_DV), fix2), vec, vec, vec],
        out_specs=[pl.BlockSpec((nb, tb, MIX0), blk),
                   pl.BlockSpec((nb, A_HEADS, A_DK, A_DV), per_b4),
                   pl.BlockSpec((nb, B_HEADS, B_N, B_N), per_b4)],
        out_shape=[jax.ShapeDtypeStruct((bsz, seq, MIX0), F32),
                   jax.ShapeDtypeStruct((bsz, A_HEADS, A_DK, A_DV), F32),
                   jax.ShapeDtypeStruct((bsz, B_HEADS, B_N, B_N), F32)],
        scratch_shapes=[pltpu.VMEM((nb, A_HEADS, A_DK, A_DV), F32),
                        pltpu.VMEM((nb, tb, A_HEADS * (A_DV + A_DK)), F32),
                        pltpu.VMEM((nb, tb, A_HEADS * LANES), F32),
                        pltpu.VMEM((nb, tb, A_KEY), F32),
                        pltpu.VMEM((nb, tb, A_KEY), F32),
                        pltpu.VMEM((nb, tb // chunk, SUBLANES, LANES), F32),
                        pltpu.VMEM((nb, B_PAIRS, LANES, LANES), F32),
                        pltpu.VMEM((B_N, LANES), F32)],
        compiler_params=pltpu.CompilerParams(dimension_semantics=("parallel", "arbitrary"),
                                             vmem_limit_bytes=VMEM_LIMIT),
        name="layer0_mixers",
    )(fa, fb, s_gdn, s_rwkv, q["gnw"], q["r_k"], q["ln_w"], q["ln_b"])


def _layer1_kernel(x_ref, mix_ref, c0_ref, n0_ref, m0_ref, wo0_ref, n1_ref, w1_ref, bias_ref, wo1_ref, nf_ref,
                   y_ref, cout_ref, nout_ref, mout_ref,
                   proj_s, x1_s, h_s, gsc_s, caug_s, m_s, *, nb, tb, chunk):
    i = pl.program_id(1)
    n_chunks = tb // chunk
    lane = _iota2((1, LANES), 1)
    krow = _iota2((LANES, 1), 0)
    n_gates = 2 * C_HEADS
    is_f = jnp.bitwise_and(lane, n_gates - 1) >= C_HEADS

    @pl.when(i == 0)
    def _():
        for bi in range(nb):
            m0 = m0_ref[bi]
            for h in range(C_HEADS):
                j = h % 2
                caug_s[bi, h] = jnp.zeros((LANES, 2 * C_DV), F32)
                caug_s[bi, h, j * C_DQK:(j + 1) * C_DQK, 0:C_DV] = c0_ref[bi, h]
                n_col = _row_to_col(n0_ref[bi, h // 2:h // 2 + 1, :])
                own = (krow // C_DQK) == j
                caug_s[bi, h, :, C_DV:2 * C_DV] = jnp.where(own, jnp.broadcast_to(n_col, (LANES, C_DV)), 0.0)
                m_s[bi, h:h + 1, :] = jnp.broadcast_to(m0[:, h:h + 1], (1, LANES))

    tri = _tri_incl(chunk)
    tri_sel = _stack_selector(tri, CUMSUM_TERMS, 1)
    sel = _replicate_selector(n_gates)
    ones_blk = jnp.ones((chunk, C_DV), F32)
    off = 2 * C_QK
    piece = 4 * LANES
    rep = {}

    def stack(ref, rows):
        return jnp.concatenate([ref[bi, rows, :] for bi in range(nb)], axis=0)

    def unstack(ref, rows, cols, val):
        for bi in range(nb):
            ref[bi, rows, cols] = val[bi * chunk:(bi + 1) * chunk, :]

    def proj_task(c):
        rows = slice(c * chunk, (c + 1) * chunk)
        x1 = stack(x_ref, rows) + jnp.dot(stack(mix_ref, rows).astype(BF16), wo0_ref[...],
                                          preferred_element_type=F32)
        unstack(x1_s, rows, slice(0, D_MODEL), x1)
        yield
        xn = _rms(x1, n1_ref[...]).astype(BF16)
        g = jnp.dot(xn, w1_ref[:, P1_GATES:P1_PAD], preferred_element_type=F32) + bias_ref[...]
        cap = C_GATE_CAP * jnp.tanh(g / C_GATE_CAP)
        unstack(gsc_s, rows, slice(0, LANES), jnp.where(is_f, -_softplus(-cap), cap))
        yield
        for j in range(0, P1_GATES, piece):
            unstack(proj_s, rows, slice(j, j + piece),
                    jnp.dot(xn, w1_ref[:, j:j + piece], preferred_element_type=F32))
            yield

    def gate_task(bi, c):
        rows = slice(c * chunk, (c + 1) * chunk)
        gl = gsc_s[bi, rows, :]
        bcum = _dot_exact_lhs(tri_sel, gl)
        yield
        rep[bi, c] = _replicate(jnp.where(is_f, bcum, gl), sel, n_gates)

    def head_chain(bi, c, h):
        rows = slice(c * chunk, (c + 1) * chunk)
        p, j = h // 2, h % 2
        own_lane = (lane // C_DQK) == j
        own_row = (krow // C_DQK) == j
        qp = proj_s[bi, rows, p * LANES:(p + 1) * LANES] * (C_DQK ** -0.5)
        kp = proj_s[bi, rows, C_QK + p * LANES:C_QK + (p + 1) * LANES]
        v = proj_s[bi, rows, off + h * C_DV:off + (h + 1) * C_DV]
        vaug = jnp.concatenate([v, ones_blk], axis=1)
        qk = _dot_nt(jnp.where(own_lane, qp, 0.0), kp)
        cm = caug_s[bi, h]
        qc = _dot(qp, cm)
        i_rep = rep[bi, c][:, h * LANES:(h + 1) * LANES]
        b_rep = rep[bi, c][:, (C_HEADS + h) * LANES:(C_HEADS + h + 1) * LANES]
        b_row = _col_to_row(b_rep[:, 0:chunk])
        i_row = _col_to_row(i_rep[:, 0:chunk])
        dmat = jnp.where(tri, b_rep[:, 0:chunk] - b_row + i_row, -jnp.inf)
        dmax = b_rep + _cummax_rows(i_rep - b_rep)
        m = m_s[bi, h:h + 1, :]
        inter = b_rep + m
        mt = jnp.maximum(inter, dmax)
        m_new = mt[chunk - 1:chunk, :]
        b_last = b_rep[chunk - 1:chunk, :]
        w_s = jnp.exp(b_last - b_rep + i_rep - m_new)
        e_c = jnp.exp(b_last + m - m_new)
        e_inter = jnp.exp(inter - mt)
        own_k = slice(j * C_DQK, (j + 1) * C_DQK)
        upd = _dot((w_s * kp).T[own_k, :], vaug)
        yield
        pm = jnp.exp(dmat - mt[:, 0:chunk]) * qk
        pv = _dot(pm, vaug)
        caug_s[bi, h, own_k, :] = jnp.concatenate([e_c, e_c], axis=1) * cm[own_k, :] + upd
        m_s[bi, h:h + 1, :] = m_new
        yield
        num = e_inter * qc[:, 0:C_DV] + pv[:, 0:C_DV]
        den = e_inter * qc[:, C_DV:2 * C_DV] + pv[:, C_DV:2 * C_DV]
        hh = num / jnp.maximum(jnp.abs(den), jnp.exp(-mt))
        og = proj_s[bi, rows, off + C_V + h * C_DV:off + C_V + (h + 1) * C_DV]
        z = proj_s[bi, rows, off + 2 * C_V + h * C_DV:off + 2 * C_V + (h + 1) * C_DV]
        h_s[bi, rows, h * C_DV:(h + 1) * C_DV] = hh * _sigmoid(og) * _silu(z)

    def final_task(c):
        rows = slice(c * chunk, (c + 1) * chunk)
        x2 = stack(x1_s, rows) + jnp.dot(stack(h_s, rows).astype(BF16), wo1_ref[...],
                                         preferred_element_type=F32)
        yield
        unstack(y_ref, rows, slice(0, D_MODEL), _rms(x2, nf_ref[...]))

    tasks = []
    for c in range(n_chunks):
        tasks.append((("p", c), "proj", [], functools.partial(proj_task, c)))
    for c in range(n_chunks):
        heads = [("h", bi, c, h) for bi in range(nb) for h in range(C_HEADS)]
        tasks.append((("f", c), "final", heads, functools.partial(final_task, c)))
        for bi in range(nb):
            tasks.append((("g", bi, c), "gates", [("p", c)], functools.partial(gate_task, bi, c)))
            for h in range(C_HEADS):
                deps = [("p", c), ("g", bi, c)] + ([("h", bi, c - 1, h)] if c else [])
                tasks.append((("h", bi, c, h), "heads", deps, functools.partial(head_chain, bi, c, h)))
    _run_tasks(tasks, {"proj": 1, "final": 1})

    @pl.when(i == pl.num_programs(1) - 1)
    def _():
        for bi in range(nb):
            for h in range(C_HEADS):
                j = h % 2
                cout_ref[bi, h] = caug_s[bi, h, j * C_DQK:(j + 1) * C_DQK, 0:C_DV]
            for p in range(C_HEADS // 2):
                n_col = caug_s[bi, 2 * p, :, C_DV:C_DV + 1] + caug_s[bi, 2 * p + 1, :, C_DV:C_DV + 1]
                nout_ref[bi, p:p + 1, :] = _col_to_row(n_col)
        mout_ref[...] = m_s[...]


def _layer1(x3d, mix3d, c0, n0, m0, q, nb, tb, chunk):
    bsz, seq, _ = x3d.shape
    blk = lambda b, i: (b, i, 0)
    per_b3 = lambda b, i: (b, 0, 0)
    per_b4 = lambda b, i: (b, 0, 0, 0)
    kern = functools.partial(_layer1_kernel, nb=nb, tb=tb, chunk=chunk)
    pairs = C_HEADS // 2
    return pl.pallas_call(
        kern,
        grid=(bsz // nb, seq // tb),
        in_specs=[pl.BlockSpec((nb, tb, D_MODEL), blk),
                  pl.BlockSpec((nb, tb, MIX0), blk),
                  pl.BlockSpec((nb, C_HEADS, C_DQK, C_DV), per_b4),
                  pl.BlockSpec((nb, pairs, LANES), per_b3),
                  pl.BlockSpec((nb, 1, C_HEADS), per_b3),
                  _resident((MIX0, D_MODEL)),
                  _resident((1, D_MODEL)),
                  _resident((D_MODEL, P1_PAD)),
                  _resident((1, LANES)),
                  _resident((C_V, D_MODEL)),
                  _resident((1, D_MODEL))],
        out_specs=[pl.BlockSpec((nb, tb, D_MODEL), blk),
                   pl.BlockSpec((nb, C_HEADS, C_DQK, C_DV), per_b4),
                   pl.BlockSpec((nb, pairs, LANES), per_b3),
                   pl.BlockSpec((nb, C_HEADS, LANES), per_b3)],
        out_shape=[jax.ShapeDtypeStruct((bsz, seq, D_MODEL), F32),
                   jax.ShapeDtypeStruct((bsz, C_HEADS, C_DQK, C_DV), F32),
                   jax.ShapeDtypeStruct((bsz, pairs, LANES), F32),
                   jax.ShapeDtypeStruct((bsz, C_HEADS, LANES), F32)],
        scratch_shapes=[pltpu.VMEM((nb, tb, P1_PAD), F32),
                        pltpu.VMEM((nb, tb, D_MODEL), F32),
                        pltpu.VMEM((nb, tb, C_V), F32),
                        pltpu.VMEM((nb, tb, LANES), F32),
                        pltpu.VMEM((nb, C_HEADS, LANES, 2 * C_DV), F32),
                        pltpu.VMEM((nb, C_HEADS, LANES), F32)],
        compiler_params=pltpu.CompilerParams(dimension_semantics=("parallel", "arbitrary"),
                                             vmem_limit_bytes=VMEM_LIMIT),
        name="layer1",
    )(x3d, mix3d, c0, n0, m0, q["wo0"], q["norm1"], q["w1"], q["gate_bias"], q["wo1"], q["normf"])


def _gate_tile(cols):
    tiled = jnp.concatenate([cols.astype(F32)] * GATE_COPIES, axis=1)
    return jnp.pad(tiled, ((0, 0), (0, LANES - tiled.shape[1])))


def _prepare_params(p):
    w_in0 = p["w_in0"]
    a_cols = A_MAIN + 2 * A_HEADS
    w0 = jnp.concatenate([w_in0[:, 0:A_MAIN].astype(BF16), _gate_tile(w_in0[:, A_MAIN:a_cols]).astype(BF16),
                          w_in0[:, a_cols:].astype(BF16)], axis=1)
    w_in1 = p["w_in1"]
    w1 = jnp.concatenate([w_in1[:, 0:P1_GATES].astype(BF16), _gate_tile(w_in1[:, P1_GATES:]).astype(BF16)], axis=1)
    zeros_h = jnp.zeros((A_HEADS,), F32)
    alog = _gate_tile(jnp.concatenate([zeros_h, p["gdn_a_log"]]).reshape(1, -1))
    dtb = _gate_tile(jnp.concatenate([zeros_h, p["gdn_dt_bias"]]).reshape(1, -1))
    gate_bias = _gate_tile(jnp.concatenate([p["mlstm_b_i"], p["mlstm_b_f"]]).reshape(1, -1))
    w2a2 = jnp.zeros((2 * B_LORA, 2 * B_W), F32)
    w2a2 = w2a2.at[0:B_LORA, 0:B_W].set(p["rwkv_w2"]).at[B_LORA:, B_W:].set(p["rwkv_a2"])
    row = lambda v: v.reshape(1, -1).astype(F32)
    return dict(
        w0=w0, w1=w1,
        wo0=p["w_out0"].astype(BF16), wo1=p["w_out1"].astype(BF16), w2a2=w2a2.astype(BF16),
        norm0=row(p["norm0_w"]), norm1=row(p["norm1_w"]), normf=row(p["norm_f_w"]),
        conv_w=p["gdn_conv_w"].astype(F32), alog=alog, dtb=dtb, gnw=row(p["gdn_norm_w"]),
        mu=row(p["rwkv_mu"]), rw0=row(p["rwkv_w0"]), ra0=row(p["rwkv_a0"]),
        k_k=row(p["rwkv_k_k"]), k_a=row(p["rwkv_k_a"]), r_k=row(p["rwkv_r_k"]),
        ln_w=row(p["rwkv_ln_w"]), ln_b=row(p["rwkv_ln_b"]), gate_bias=gate_bias,
    )


def _tiles(bsz, seq):
    chunk = math.gcd(seq, MAX_CHUNK)
    tb = min(seq, 256)
    nb = 2 if seq > MAX_CHUNK else 8
    nb = math.gcd(nb, bsz)
    tt = min(seq, PROJ_ROWS)
    ns = math.gcd(min(PROJ_ROWS // tt, PROJ_MAX_SEQS), bsz)
    return chunk, tb, nb, tt, ns


def _trunk(x, state, q):
    conv0, s_gdn, x_prev, s_rwkv, c0, n0, m0 = state
    bsz, seq, _ = x.shape
    chunk, tb, nb, tt, ns = _tiles(bsz, seq)
    xn_last, prev_row = _aux(x[:, -1, :], x_prev, q["norm0"], q["w0"])
    fa, fb, conv_new = _proj0(x, conv0, prev_row.reshape(bsz, 1, B_COLS), q, ns, tt)
    mix, s_gdn_new, s_rwkv_new = _layer0_mixers(fa, fb, s_gdn, s_rwkv, q, nb, tb, chunk)
    y, c_new, n_new, m_new = _layer1(x, mix, c0, n0.reshape(bsz, C_HEADS // 2, LANES),
                                     m0.reshape(bsz, 1, C_HEADS), q, nb, tb, chunk)
    return y, (conv_new, s_gdn_new, xn_last, s_rwkv_new, c_new,
               n_new.reshape(bsz, C_HEADS, C_DQK), m_new[:, :, 0])


def kernel(x_prompt, x_sample, state_gdn_conv, state_gdn, state_rwkv_shift, state_rwkv,
           state_mlstm_c, state_mlstm_n, state_mlstm_m,
           norm0_w, w_in0, w_out0, gdn_conv_w, gdn_a_log, gdn_dt_bias, gdn_norm_w,
           rwkv_mu, rwkv_w0, rwkv_w2, rwkv_a0, rwkv_a2, rwkv_k_k, rwkv_k_a, rwkv_r_k,
           rwkv_ln_w, rwkv_ln_b, norm1_w, w_in1, w_out1, mlstm_b_i, mlstm_b_f, norm_f_w):
    q = _prepare_params(dict(
        norm0_w=norm0_w, w_in0=w_in0, w_out0=w_out0, gdn_conv_w=gdn_conv_w, gdn_a_log=gdn_a_log,
        gdn_dt_bias=gdn_dt_bias, gdn_norm_w=gdn_norm_w, rwkv_mu=rwkv_mu, rwkv_w0=rwkv_w0,
        rwkv_w2=rwkv_w2, rwkv_a0=rwkv_a0, rwkv_a2=rwkv_a2, rwkv_k_k=rwkv_k_k, rwkv_k_a=rwkv_k_a,
        rwkv_r_k=rwkv_r_k, rwkv_ln_w=rwkv_ln_w, rwkv_ln_b=rwkv_ln_b, norm1_w=norm1_w, w_in1=w_in1,
        w_out1=w_out1, mlstm_b_i=mlstm_b_i, mlstm_b_f=mlstm_b_f, norm_f_w=norm_f_w))
    bsz = x_prompt.shape[0]
    prompt_state = (
        jnp.zeros((bsz, A_CONV - 1, A_QKV), F32),
        jnp.zeros((bsz, A_HEADS, A_DK, A_DV), F32),
        jnp.zeros((bsz, D_MODEL), F32),
        jnp.zeros((bsz, B_HEADS, B_N, B_N), F32),
        jnp.zeros((bsz, C_HEADS, C_DQK, C_DV), F32),
        jnp.zeros((bsz, C_HEADS, C_DQK), F32),
        jnp.zeros((bsz, C_HEADS), F32),
    )
    y_p, sp = _trunk(x_prompt, prompt_state, q)
    sample_state = (state_gdn_conv, state_gdn, state_rwkv_shift, state_rwkv,
                    state_mlstm_c, state_mlstm_n, state_mlstm_m)
    y_s, ss = _trunk(x_sample, sample_state, q)
    return (y_p, y_s) + tuple(sp) + tuple(ss)
```

```python
import functools
import math

import jax
import jax.numpy as jnp
from jax import lax
from jax.experimental import pallas as pl
from jax.experimental.pallas import tpu as pltpu

F32 = jnp.float32
BF16 = jnp.bfloat16

D_MODEL = 1024
EPS = 1e-6
LANES = 128
SUBLANES = 8
A_HEADS, A_DK, A_DV, A_CONV = 4, 128, 128, 4
A_KEY = A_HEADS * A_DK
A_VAL = A_HEADS * A_DV
A_QKV = 2 * A_KEY + A_VAL
A_MAIN = A_QKV + A_VAL
B_HEADS, B_N, B_LORA = 8, 64, 64
B_W = B_HEADS * B_N
B_COLS = 4 * B_W + 2 * B_LORA
B_PAIRS = B_HEADS // 2
B_GN_EPS = 64e-5
B_DECAY_SCALE = math.exp(-0.5)
MIX0 = A_VAL + B_W
C_HEADS, C_DQK, C_DV = 8, 64, 128
C_QK = C_HEADS * C_DQK
C_V = C_HEADS * C_DV
C_GATE_CAP = 15.0
P0_GATES = A_MAIN
P0_B = A_MAIN + LANES
P0_PAD = P0_B + B_COLS
P1_GATES = 2 * C_QK + 3 * C_V
P1_PAD = P1_GATES + LANES
MAX_CHUNK = 64
GATE_COPIES = 3
CUMSUM_TERMS = 3
SUM_TERMS = 2
VMEM_LIMIT = 56 * 1024 * 1024


def _dot(a, b):
    return jnp.dot(a.astype(BF16), b.astype(BF16), preferred_element_type=F32)


def _dot_nt(a, b):
    return lax.dot_general(a.astype(BF16), b.astype(BF16), (((1,), (1,)), ((), ())),
                           preferred_element_type=F32)


def _dot_tn(a, b):
    return lax.dot_general(a.astype(BF16), b.astype(BF16), (((0,), (0,)), ((), ())),
                           preferred_element_type=F32)


def _split(x, terms):
    out = []
    for _ in range(terms - 1):
        hi = x.astype(BF16)
        out.append(hi)
        x = x - hi.astype(F32)
    out.append(x.astype(BF16))
    return out


def _stack_selector(sel, terms, axis):
    s = jnp.where(sel, 1.0, 0.0).astype(BF16)
    return jnp.concatenate([s] * terms, axis=axis)


def _dot_exact_lhs(sel_stacked, x, terms=CUMSUM_TERMS):
    return jnp.dot(sel_stacked, jnp.concatenate(_split(x, terms), axis=0), preferred_element_type=F32)


def _dot_exact_rhs(x, sel_stacked, terms=SUM_TERMS):
    return jnp.dot(jnp.concatenate(_split(x, terms), axis=1), sel_stacked, preferred_element_type=F32)


def _iota2(shape, dim):
    return lax.broadcasted_iota(jnp.int32, shape, dim)


def _tri_incl(n):
    return (_iota2((n, n), 0) >= _iota2((n, n), 1))


def _eye(n):
    return jnp.where(_iota2((n, n), 0) == _iota2((n, n), 1), 1.0, 0.0)


def _col_to_row(col):
    n = col.shape[0]
    eye = _iota2((n, n), 0) == _iota2((n, n), 1)
    return jnp.sum(jnp.where(eye, col, 0.0), axis=0, keepdims=True)


def _row_to_col(row):
    n = row.shape[1]
    eye = _iota2((n, n), 0) == _iota2((n, n), 1)
    return jnp.sum(jnp.where(eye, row, 0.0), axis=1, keepdims=True)


def _replicate(x, sel, group):
    hi, mid, lo = _split(x, GATE_COPIES)
    lane = _iota2(x.shape, 1)
    packed = jnp.where(lane < group, hi, jnp.where(lane < 2 * group, mid, lo))
    packed = jnp.where(lane < GATE_COPIES * group, packed, jnp.zeros_like(packed))
    return jnp.dot(packed, sel, preferred_element_type=F32)


def _replicate_selector(group):
    assert group & (group - 1) == 0 and LANES == 128
    k = _iota2((LANES, group * LANES), 0)
    col = _iota2((LANES, group * LANES), 1)
    hit = (k < GATE_COPIES * group) & (jnp.bitwise_and(k, group - 1) == jnp.right_shift(col, 7))
    return jnp.where(hit, 1.0, 0.0).astype(BF16)


def _cummax_rows(x):
    n = x.shape[0]
    row = _iota2(x.shape, 0)
    sh = 1
    while sh < n:
        x = jnp.maximum(x, jnp.where(row >= sh, pltpu.roll(x, sh, 0), -jnp.inf))
        sh *= 2
    return x


def _inv_unit_lower(p, levels, eye):
    n = p.shape[0]
    t = eye + p
    m = 2
    if m < levels:
        p = _dot(p, p)
        yield
    while m < levels:
        m *= 2
        if m >= levels:
            t = t + _dot(p, t)
        elif n % LANES == 0:
            both = _dot(p, jnp.concatenate([p, t], axis=1))
            p, t = both[:, 0:n], t + both[:, n:2 * n]
            yield
        else:
            both = _dot(jnp.concatenate([p, t], axis=0), p)
            p, t = both[0:n, :], t + both[n:2 * n, :]
            yield
    return t


def _run_tasks(tasks, caps):
    done, live, pending = set(), [], list(tasks)
    while pending or live:
        count = {}
        for _, group, _ in live:
            count[group] = count.get(group, 0) + 1
        still = []
        for key, group, deps, make in pending:
            if all(d in done for d in deps) and count.get(group, 0) < caps.get(group, len(tasks)):
                live.append((key, group, make()))
                count[group] = count.get(group, 0) + 1
            else:
                still.append((key, group, deps, make))
        assert live, "task graph has a cycle or a missing dependency"
        pending = still
        nxt = []
        for key, group, gen in live:
            try:
                next(gen)
                nxt.append((key, group, gen))
            except StopIteration:
                done.add(key)
        live = nxt


def _sigmoid(x):
    return 1.0 / (1.0 + jnp.exp(-x))


def _silu(x):
    return x * _sigmoid(x)


def _softplus(x):
    return jnp.maximum(x, 0.0) + jnp.log1p(jnp.exp(-jnp.abs(x)))


def _rms(x, w):
    return x * lax.rsqrt(jnp.mean(x * x, axis=-1, keepdims=True) + EPS) * w


def _resident(shape):
    zeros = (0,) * len(shape)
    return pl.BlockSpec(shape, lambda *_: zeros, pipeline_mode=pl.Buffered(1))


FA_Z = A_QKV
FA_BETA = A_MAIN
FA_DECAY = A_MAIN + LANES
FA_COLS = A_MAIN + 2 * LANES
_FR, _FLW, _FK, _FV, _FA, _FB, _FG = range(7)
FB_COLS = 7 * B_W
SUB_ROWS = 64
PROJ_ROWS = 512
PROJ_MAX_SEQS = 32


def _proj0_kernel(x_ref, conv0_ref, prev_ref, nw_ref, w_ref, convw_ref, alog_ref, dtb_ref,
                  mu_ref, w0_ref, a0_ref, w2a2_ref, kk_ref, ka_ref,
                  fa_ref, fb_ref, convn_ref, ext_a, ext_b, *, ns, tt):
    i = pl.program_id(1)
    tail = SUBLANES
    rows = ns * tt
    lane = _iota2((1, LANES), 1)
    head0 = lane < B_N
    ones_blk = (_iota2((LANES, LANES), 0) // B_N == _iota2((LANES, LANES), 1) // B_N)
    ones_sel = _stack_selector(ones_blk, SUM_TERMS, 0)

    @pl.when(i == 0)
    def _():
        ext_a[:, 0:tail, :] = jnp.zeros((ns, tail, A_QKV), F32)
        ext_a[:, tail - (A_CONV - 1):tail, :] = conv0_ref[...]
        ext_b[:, 0:tail, :] = jnp.zeros((ns, tail, B_COLS), F32)
        ext_b[:, tail - 1:tail, :] = prev_ref[...]

    @pl.when(i > 0)
    def _():
        ext_a[:, 0:tail, :] = ext_a[:, tt:tt + tail, :]
        ext_b[:, 0:tail, :] = ext_b[:, tt:tt + tail, :]

    xn = _rms(x_ref[...].reshape(rows, D_MODEL), nw_ref[...]).astype(BF16)
    piece = 4 * LANES

    def cols(lo, hi):
        return jnp.dot(xn, w_ref[:, lo:hi], preferred_element_type=F32).reshape(ns, tt, hi - lo)

    for lo in range(0, A_QKV, piece):
        ext_a[:, tail:tail + tt, lo:lo + piece] = cols(lo, lo + piece)
    fa_ref[:, :, FA_Z:FA_Z + A_VAL] = cols(A_QKV, A_MAIN)
    g = cols(P0_GATES, P0_B)
    fa_ref[:, :, FA_BETA:FA_BETA + LANES] = _sigmoid(g)
    fa_ref[:, :, FA_DECAY:FA_DECAY + LANES] = -jnp.exp(alog_ref[...]) * _softplus(g + dtb_ref[...])
    for lo in range(0, B_COLS, piece):
        hi = min(lo + piece, B_COLS)
        ext_b[:, tail:tail + tt, lo:hi] = cols(P0_B + lo, P0_B + hi)

    sub_t = min(tt, SUB_ROWS)
    sub_s = min(ns, SUB_ROWS // sub_t)
    cw = convw_ref[...]
    mu = mu_ref[...]
    for s0 in range(0, ns, sub_s):
        sq = slice(s0, s0 + sub_s)
        for t0 in range(0, tt, sub_t):
            tr = slice(t0, t0 + sub_t)
            y = ext_a[sq, tail + t0:tail + t0 + sub_t, :] * cw[A_CONV - 1:A_CONV, :]
            for j in range(A_CONV - 1):
                sh = A_CONV - 1 - j
                y = y + ext_a[sq, tail + t0 - sh:tail + t0 - sh + sub_t, :] * cw[j:j + 1, :]
            y = _silu(y)
            for h in range(A_HEADS):
                q = y[:, :, h * A_DK:(h + 1) * A_DK]
                fa_ref[sq, tr, h * A_DK:(h + 1) * A_DK] = (
                    q * lax.rsqrt(jnp.sum(q * q, axis=-1, keepdims=True) + EPS) * (A_DK ** -0.5))
                k = y[:, :, A_KEY + h * A_DK:A_KEY + (h + 1) * A_DK]
                fa_ref[sq, tr, A_KEY + h * A_DK:A_KEY + (h + 1) * A_DK] = (
                    k * lax.rsqrt(jnp.sum(k * k, axis=-1, keepdims=True) + EPS))
            fa_ref[sq, tr, 2 * A_KEY:A_QKV] = y[:, :, 2 * A_KEY:A_QKV]
            pb = ext_b[sq, tail + t0:tail + t0 + sub_t, :]
            pb_prev = ext_b[sq, tail + t0 - 1:tail + t0 - 1 + sub_t, :]
            pbs = (pb + (pb_prev - pb) * mu).reshape(sub_s * sub_t, B_COLS)
            lo_ = pbs[:, 4 * B_W:B_COLS]
            lo_ = jnp.where(head0, jnp.tanh(lo_), lo_)
            wa = _dot(lo_, w2a2_ref[...])
            logw = -B_DECAY_SCALE * _sigmoid(w0_ref[...] + wa[:, 0:B_W])
            a = _sigmoid(a0_ref[...] + wa[:, B_W:2 * B_W])
            kb = pbs[:, B_W:2 * B_W]
            kkx = kb * kk_ref[...]

            def put(slot, val):
                fb_ref[sq, tr, slot * B_W:(slot + 1) * B_W] = val.reshape(sub_s, sub_t, B_W)

            kkn = jnp.concatenate(
                [kkx[:, p * LANES:(p + 1) * LANES]
                 * lax.rsqrt(_dot_exact_rhs(kkx[:, p * LANES:(p + 1) * LANES] ** 2, ones_sel) + EPS)
                 for p in range(B_PAIRS)], axis=1)
            put(_FR, pbs[:, 0:B_W])
            put(_FLW, logw)
            put(_FK, kb * (1.0 + (a - 1.0) * ka_ref[...]))
            put(_FV, pbs[:, 2 * B_W:3 * B_W])
            put(_FA, -kkn)
            put(_FB, kkn * a)
            put(_FG, pbs[:, 3 * B_W:4 * B_W])

    @pl.when(i == pl.num_programs(1) - 1)
    def _():
        convn_ref[...] = ext_a[:, tt + tail - (A_CONV - 1):tt + tail, :]


def _proj0(x3d, conv0, prev_row, q, ns, tt):
    bsz, seq, _ = x3d.shape
    blk = lambda b, i: (b, i, 0)
    per_b3 = lambda b, i: (b, 0, 0)
    kern = functools.partial(_proj0_kernel, ns=ns, tt=tt)
    return pl.pallas_call(
        kern,
        grid=(bsz // ns, seq // tt),
        in_specs=[pl.BlockSpec((ns, tt, D_MODEL), blk),
                  pl.BlockSpec((ns, A_CONV - 1, A_QKV), per_b3),
                  pl.BlockSpec((ns, 1, B_COLS), per_b3),
                  _resident((1, D_MODEL)),
                  _resident((D_MODEL, P0_PAD)),
                  _resident((A_CONV, A_QKV)), _resident((1, LANES)), _resident((1, LANES)),
                  _resident((1, B_COLS)), _resident((1, B_W)), _resident((1, B_W)),
                  _resident((2 * B_LORA, 2 * B_W)), _resident((1, B_W)), _resident((1, B_W))],
        out_specs=[pl.BlockSpec((ns, tt, FA_COLS), blk),
                   pl.BlockSpec((ns, tt, FB_COLS), blk),
                   pl.BlockSpec((ns, A_CONV - 1, A_QKV), per_b3)],
        out_shape=[jax.ShapeDtypeStruct((bsz, seq, FA_COLS), F32),
                   jax.ShapeDtypeStruct((bsz, seq, FB_COLS), F32),
                   jax.ShapeDtypeStruct((bsz, A_CONV - 1, A_QKV), F32)],
        scratch_shapes=[pltpu.VMEM((ns, tt + SUBLANES, A_QKV), F32),
                        pltpu.VMEM((ns, tt + SUBLANES, B_COLS), F32)],
        compiler_params=pltpu.CompilerParams(dimension_semantics=("parallel", "arbitrary"),
                                             vmem_limit_bytes=VMEM_LIMIT),
        name="proj0_features",
    )(x3d, conv0, prev_row, q["norm0"], q["w0"], q["conv_w"], q["alog"], q["dtb"],
      q["mu"], q["rw0"], q["ra0"], q["w2a2"], q["k_k"], q["k_a"])


def _aux_kernel(xl_ref, xp_ref, nw_ref, wb_ref, xn_ref, prev_ref):
    xn_ref[...] = _rms(xl_ref[...], nw_ref[...])
    prev_ref[...] = jnp.dot(xp_ref[...].astype(BF16), wb_ref[...], preferred_element_type=F32)


def _aux(x_last, x_prev, norm_w, w0):
    b = x_last.shape[0]
    full = lambda i: (0, 0)
    return pl.pallas_call(
        _aux_kernel,
        grid=(1,),
        in_specs=[pl.BlockSpec((b, D_MODEL), full), pl.BlockSpec((b, D_MODEL), full),
                  pl.BlockSpec((1, D_MODEL), full),
                  pl.BlockSpec((D_MODEL, B_COLS), lambda i: (0, P0_B // B_COLS))],
        out_specs=[pl.BlockSpec((b, D_MODEL), full), pl.BlockSpec((b, B_COLS), full)],
        out_shape=[jax.ShapeDtypeStruct((b, D_MODEL), F32), jax.ShapeDtypeStruct((b, B_COLS), F32)],
        compiler_params=pltpu.CompilerParams(dimension_semantics=("arbitrary",), vmem_limit_bytes=VMEM_LIMIT),
        name="shift_rows",
    )(x_last, x_prev, norm_w, w0)


def _layer0_kernel(fa_ref, fb_ref, sg0_ref, sr0_ref, gnw_ref, rk_ref, lnw_ref, lnb_ref,
                   o_ref, sgout_ref, srout_ref,
                   stg_s, uw_s, attn_s, qg_s, kd_s, ge_s, str_s, tmp_s, *, nb, tb, chunk):
    i = pl.program_id(1)
    n_chunks = tb // chunk
    lane = _iota2((1, LANES), 1)
    head0 = lane < B_N
    ones_blk = (_iota2((LANES, LANES), 0) // B_N == _iota2((LANES, LANES), 1) // B_N)

    @pl.when(i == 0)
    def _():
        for bi in range(nb):
            for p in range(B_PAIRS):
                tmp_s[...] = jnp.zeros((B_N, LANES), F32)
                tmp_s[:, 0:B_N] = sr0_ref[bi, 2 * p + 1]
                shifted = pltpu.roll(tmp_s[...], B_N, 1)
                str_s[bi, p, B_N:LANES, :] = jnp.where(head0, 0.0, shifted)
                tmp_s[:, 0:B_N] = sr0_ref[bi, 2 * p]
                str_s[bi, p, 0:B_N, :] = tmp_s[...]
        stg_s[...] = sg0_ref[...]

    tri = _tri_incl(chunk)
    tri_sel = _stack_selector(tri, CUMSUM_TERMS, 1)
    ones_sel = _stack_selector(ones_blk, SUM_TERMS, 0)
    eye_2c = _eye(2 * chunk)

    n_gates = 2 * A_HEADS
    is_decay = jnp.bitwise_and(lane, n_gates - 1) >= A_HEADS
    sel = _replicate_selector(n_gates)
    gnw = gnw_ref[...]
    qkv_s = fa_ref
    rep = {}

    def gdn_gates(bi, c):
        rows = slice(c * chunk, (c + 1) * chunk)
        gcum = _dot_exact_lhs(tri_sel, fa_ref[bi, rows, FA_DECAY:FA_DECAY + LANES])
        yield
        rep[bi, c] = _replicate(jnp.where(is_decay, gcum, fa_ref[bi, rows, FA_BETA:FA_BETA + LANES]),
                                sel, n_gates)

    wpk = A_HEADS * chunk
    head_pk = _iota2((1, wpk), 1) // chunk
    s_pk = _iota2((1, wpk), 1) - head_pk * chunk
    t_pk = _iota2((chunk, wpk), 0)
    diag_pk = t_pk == s_pk
    tri_pk = t_pk >= s_pk
    strict_pk = t_pk > s_pk
    head_of_klane = _iota2((1, A_KEY), 1) // A_DK
    head_of_uwlane = _iota2((1, A_HEADS * (A_DV + A_DK)), 1) // (A_DV + A_DK)

    def widen(x):
        reps = -(-wpk // LANES)
        w = jnp.concatenate([x] * reps, axis=1) if reps > 1 else x
        return w if w.shape[1] == wpk else w[:, 0:wpk]

    def pack_heads(xs):
        out = xs[A_HEADS - 1]
        for h in range(A_HEADS - 2, -1, -1):
            out = jnp.where(head_pk == h, xs[h], out)
        return out

    def block_diag(x):
        return jnp.concatenate([jnp.where(head_pk == h, x, 0.0) for h in range(A_HEADS)], axis=0)

    def gdn_local(bi, c):
        rows = slice(c * chunk, (c + 1) * chunk)
        heads = range(A_HEADS)
        q = [qkv_s[bi, rows, h * A_DK:(h + 1) * A_DK] for h in heads]
        k = [qkv_s[bi, rows, A_KEY + h * A_DK:A_KEY + (h + 1) * A_DK] for h in heads]
        v = [qkv_s[bi, rows, 2 * A_KEY + h * A_DV:2 * A_KEY + (h + 1) * A_DV] for h in heads]
        qk = [_dot_nt(q[h], k[h]) for h in heads]
        k_bd = jnp.concatenate([jnp.where(head_of_klane == h, jnp.concatenate([k[h]] * A_HEADS, axis=1), 0.0)
                                for h in heads], axis=0)
        kk_pk = _dot_nt(jnp.concatenate(k, axis=1), k_bd)
        yield
        b_rep = [rep[bi, c][:, h * LANES:(h + 1) * LANES] for h in heads]
        g_rep = [rep[bi, c][:, (A_HEADS + h) * LANES:(A_HEADS + h + 1) * LANES] for h in heads]
        eg = [jnp.exp(g) for g in g_rep]
        for h in heads:
            g_row = _col_to_row(g_rep[h][:, 0:chunk])
            g_last = g_rep[h][chunk - 1:chunk, :]
            decay = jnp.where(tri, jnp.exp(jnp.where(tri, g_rep[h][:, 0:chunk] - g_row, 0.0)), 0.0)
            qg_s[bi, rows, h * A_DK:(h + 1) * A_DK] = q[h] * eg[h]
            kd_s[bi, rows, h * A_DK:(h + 1) * A_DK] = k[h] * jnp.exp(g_last - g_rep[h])
            ge_s[bi, c, h:h + 1, :] = jnp.exp(g_last)
            attn_s[bi, rows, h * LANES:h * LANES + chunk] = decay * qk[h]
        g_col_pk = pack_heads([widen(g) for g in g_rep])
        b_pk = pack_heads([widen(b) for b in b_rep])
        g_row_pk = jnp.sum(jnp.where(diag_pk, g_col_pk, 0.0), axis=0, keepdims=True)
        decay_pk = jnp.where(tri_pk, jnp.exp(jnp.where(tri_pk, g_col_pk - g_row_pk, 0.0)), 0.0)
        p = -jnp.where(strict_pk, b_pk * decay_pk * kk_pk, 0.0)
        t = jnp.where(diag_pk, 1.0, 0.0) + p
        m = 2
        if m < chunk:
            p = _dot(p, block_diag(p))
            yield
        while m < chunk:
            m *= 2
            if m >= chunk:
                t = t + _dot(t, block_diag(p))
            else:
                both = _dot(jnp.concatenate([p, t], axis=0), block_diag(p))
                p, t = both[0:chunk, :], t + both[chunk:2 * chunk, :]
                yield
        rhs = jnp.concatenate(
            [jnp.where(head_of_uwlane == h,
                       jnp.concatenate([b_rep[h] * v[h], (b_rep[h] * eg[h]) * k[h]] * A_HEADS, axis=1), 0.0)
             for h in heads], axis=0)
        uw_s[bi, rows, :] = _dot(t, rhs)

    def gdn_state(bi, c, h):
        rows = slice(c * chunk, (c + 1) * chunk)
        uw = uw_s[bi, rows, 2 * h * LANES:2 * (h + 1) * LANES]
        s = stg_s[bi, h]
        wq_s = _dot(jnp.concatenate([uw[:, A_DV:A_DV + A_DK], qg_s[bi, rows, h * A_DK:(h + 1) * A_DK]], axis=0), s)
        yield
        nv = uw[:, 0:A_DV] - wq_s[0:chunk, :]
        o = wq_s[chunk:2 * chunk, :] + _dot(attn_s[bi, rows, h * LANES:h * LANES + chunk], nv)
        stg_s[bi, h] = ge_s[bi, c, h:h + 1, :] * s + _dot_tn(kd_s[bi, rows, h * A_DK:(h + 1) * A_DK], nv)
        yield
        z = fa_ref[bi, rows, FA_Z + h * A_DV:FA_Z + (h + 1) * A_DV]
        o_ref[bi, rows, h * A_DV:(h + 1) * A_DV] = _rms(o, gnw) * _silu(z)

    feat_s = fb_ref
    c2 = 2 * chunk
    ri = _iota2((c2, c2), 0)
    ci = _iota2((c2, c2), 1)
    same_head = (ri // chunk) == (ci // chunk)
    incl_bd = same_head & (ri >= ci)
    strict_bd = same_head & (ri > ci)
    ri4 = _iota2((2 * c2, c2), 0)
    ci4 = _iota2((2 * c2, c2), 1)
    rr4 = ri4 % c2
    lm_mask = ((rr4 // chunk) == (ci4 // chunk)) & ((rr4 > ci4) | ((ri4 >= c2) & (rr4 == ci4)))
    lane_st = (_iota2((c2, LANES), 0) // chunk) == (_iota2((c2, LANES), 1) // B_N)
    local = {}

    def rwkv_local(bi, c, p):
        rows = slice(c * chunk, (c + 1) * chunk)

        def feat(slot):
            return feat_s[bi, rows, slot * B_W + p * LANES:slot * B_W + (p + 1) * LANES]
        r, lw, k, v, a_, b_ = (feat(s) for s in (_FR, _FLW, _FK, _FV, _FA, _FB))
        cs = _dot_exact_lhs(tri_sel, lw)
        yield
        cs_last = cs[chunk - 1:chunk, :]
        e_inv = jnp.exp(-cs)
        e_rem = jnp.exp(cs_last - cs)
        at = a_ * jnp.exp(cs - lw)
        rt = r * jnp.exp(cs)
        bt = b_ * e_inv
        kt = k * e_inv
        a_st = jnp.concatenate([jnp.where(head0, at, 0.0), jnp.where(head0, 0.0, at)], axis=0)
        r_st = jnp.concatenate([jnp.where(head0, rt, 0.0), jnp.where(head0, 0.0, rt)], axis=0)
        b2 = jnp.concatenate([bt, bt], axis=0)
        k2 = jnp.concatenate([kt, kt], axis=0)
        v2 = jnp.concatenate([v, v], axis=0)
        ar_st = jnp.concatenate([a_st, r_st], axis=0)
        if c2 % LANES == 0:
            ar_bk = _dot_nt(ar_st, jnp.concatenate([b2, k2], axis=0))
            ar_b, ar_k = ar_bk[:, 0:c2], ar_bk[:, c2:2 * c2]
        else:
            ar_b = _dot_nt(ar_st, b2)
            ar_k = _dot_nt(ar_st, k2)
        yield
        l_ab = jnp.where(strict_bd, ar_b[0:c2, :], 0.0)
        m_rb = jnp.where(incl_bd, ar_b[c2:2 * c2, :], 0.0)
        lm_k = jnp.where(lm_mask, ar_k, 0.0)
        lm_v = _dot(lm_k, v2)
        sl = slice(p * LANES, (p + 1) * LANES)
        bonus = _dot_exact_rhs(r * k * rk_ref[:, sl], ones_sel) * v
        t_inv = yield from _inv_unit_lower(l_ab, chunk, eye_2c)
        local[bi, c, p] = (t_inv, lm_v, m_rb, jnp.concatenate([at, rt], axis=0),
                           jnp.concatenate([b_ * e_rem, k * e_rem], axis=0), jnp.exp(cs_last), bonus)

    def rwkv_state(bi, c, p):
        rows = slice(c * chunk, (c + 1) * chunk)
        sl = slice(p * LANES, (p + 1) * LANES)
        t_inv, lm_v, m_rb, at_rt, bk_rem, decay_end, bonus = local.pop((bi, c, p))
        v = feat_s[bi, rows, _FV * B_W + p * LANES:_FV * B_W + (p + 1) * LANES]
        gate = feat_s[bi, rows, _FG * B_W + p * LANES:_FG * B_W + (p + 1) * LANES]
        s = str_s[bi, p]
        ar = _dot_nt(at_rt, s)
        yield
        ah = ar[0:chunk, :]
        rh = ar[chunk:c2, :]
        x_st = jnp.where(lane_st, jnp.concatenate([ah, ah], axis=0) + lm_v[0:c2, :], 0.0)
        u_st = _dot(t_inv, x_st)
        yield
        o_st = jnp.where(lane_st, _dot(m_rb, u_st) + lm_v[c2:2 * c2, :], 0.0)
        u = u_st[0:chunk, :] + u_st[chunk:c2, :]
        upd = _dot_tn(jnp.concatenate([u, v], axis=0), bk_rem)
        yield
        o = rh + o_st[0:chunk, :] + o_st[chunk:c2, :]
        str_s[bi, p] = s * decay_end + jnp.where(ones_blk, upd, 0.0)
        mean = _dot_exact_rhs(o, ones_sel) * (1.0 / B_N)
        yield
        d = o - mean
        var = _dot_exact_rhs(d * d, ones_sel) * (1.0 / B_N)
        yield
        ob = d * lax.rsqrt(var + B_GN_EPS) * lnw_ref[:, sl] + lnb_ref[:, sl] + bonus
        o_ref[bi, rows, A_VAL + p * LANES:A_VAL + (p + 1) * LANES] = ob * _silu(gate)

    tasks = []
    blocks = [(bi, c) for c in range(n_chunks) for bi in range(nb)]
    for bi, c in blocks:
        for p in range(B_PAIRS):
            deps = [("rl", bi, c, p)] + ([("rs", bi, c - 1, p)] if c else [])
            tasks.append((("rs", bi, c, p), "rwkv_state", deps, functools.partial(rwkv_state, bi, c, p)))
        for h in range(A_HEADS):
            deps = [("gl", bi, c)] + ([("gs", bi, c - 1, h)] if c else [])
            tasks.append((("gs", bi, c, h), "gdn_state", deps, functools.partial(gdn_state, bi, c, h)))
    for bi, c in blocks:
        for p in range(B_PAIRS):
            tasks.append((("rl", bi, c, p), "rwkv_local", [], functools.partial(rwkv_local, bi, c, p)))
        tasks.append((("gl", bi, c), "gdn_local", [("gg", bi, c)], functools.partial(gdn_local, bi, c)))
    for bi, c in blocks:
        tasks.append((("gg", bi, c), "gdn_gates", [], functools.partial(gdn_gates, bi, c)))
    _run_tasks(tasks, {"gdn_gates": 2, "rwkv_local": 2 * B_PAIRS, "gdn_local": 4})

    @pl.when(i == pl.num_programs(1) - 1)
    def _():
        for bi in range(nb):
            for p in range(B_PAIRS):
                srout_ref[bi, 2 * p] = str_s[bi, p, 0:B_N, 0:B_N]
                shifted = pltpu.roll(str_s[bi, p, B_N:LANES, :], B_N, 1)
                srout_ref[bi, 2 * p + 1] = shifted[:, 0:B_N]
        sgout_ref[...] = stg_s[...]


def _layer0_mixers(fa, fb, s_gdn, s_rwkv, q, nb, tb, chunk):
    bsz, seq, _ = fa.shape
    blk = lambda b, i: (b, i, 0)
    per_b4 = lambda b, i: (b, 0, 0, 0)
    fix2 = lambda b, i: (0, 0)
    kern = functools.partial(_layer0_kernel, nb=nb, tb=tb, chunk=chunk)
    vec = pl.BlockSpec((1, B_W), fix2)
    return pl.pallas_call(
        kern,
        grid=(bsz // nb, seq // tb),
        in_specs=[pl.BlockSpec((nb, tb, FA_COLS), blk),
                  pl.BlockSpec((nb, tb, FB_COLS), blk),
                  pl.BlockSpec((nb, A_HEADS, A_DK, A_DV), per_b4),
                  pl.BlockSpec((nb, B_HEADS, B_N, B_N), per_b4),
                  pl.BlockSpec((1, A_DV), fix2), vec, vec, vec],
        out_specs=[pl.BlockSpec((nb, tb, MIX0), blk),
                   pl.BlockSpec((nb, A_HEADS, A_DK, A_DV), per_b4),
                   pl.BlockSpec((nb, B_HEADS, B_N, B_N), per_b4)],
        out_shape=[jax.ShapeDtypeStruct((bsz, seq, MIX0), F32),
                   jax.ShapeDtypeStruct((bsz, A_HEADS, A_DK, A_DV), F32),
                   jax.ShapeDtypeStruct((bsz, B_HEADS, B_N, B_N), F32)],
        scratch_shapes=[pltpu.VMEM((nb, A_HEADS, A_DK, A_DV), F32),
                        pltpu.VMEM((nb, tb, A_HEADS * (A_DV + A_DK)), F32),
                        pltpu.VMEM((nb, tb, A_HEADS * LANES), F32),
                        pltpu.VMEM((nb, tb, A_KEY), F32),
                        pltpu.VMEM((nb, tb, A_KEY), F32),
                        pltpu.VMEM((nb, tb // chunk, SUBLANES, LANES), F32),
                        pltpu.VMEM((nb, B_PAIRS, LANES, LANES), F32),
                        pltpu.VMEM((B_N, LANES), F32)],
        compiler_params=pltpu.CompilerParams(dimension_semantics=("parallel", "arbitrary"),
                                             vmem_limit_bytes=VMEM_LIMIT),
        name="layer0_mixers",
    )(fa, fb, s_gdn, s_rwkv, q["gnw"], q["r_k"], q["ln_w"], q["ln_b"])


def _layer1_kernel(x_ref, mix_ref, c0_ref, n0_ref, m0_ref, wo0_ref, n1_ref, w1_ref, bias_ref, wo1_ref, nf_ref,
                   y_ref, cout_ref, nout_ref, mout_ref,
                   proj_s, x1_s, h_s, gsc_s, caug_s, m_s, *, nb, tb, chunk):
    i = pl.program_id(1)
    n_chunks = tb // chunk
    lane = _iota2((1, LANES), 1)
    krow = _iota2((LANES, 1), 0)
    n_gates = 2 * C_HEADS
    is_f = jnp.bitwise_and(lane, n_gates - 1) >= C_HEADS

    @pl.when(i == 0)
    def _():
        for bi in range(nb):
            m0 = m0_ref[bi]
            for h in range(C_HEADS):
                j = h % 2
                caug_s[bi, h] = jnp.zeros((LANES, 2 * C_DV), F32)
                caug_s[bi, h, j * C_DQK:(j + 1) * C_DQK, 0:C_DV] = c0_ref[bi, h]
                n_col = _row_to_col(n0_ref[bi, h // 2:h // 2 + 1, :])
                own = (krow // C_DQK) == j
                caug_s[bi, h, :, C_DV:2 * C_DV] = jnp.where(own, jnp.broadcast_to(n_col, (LANES, C_DV)), 0.0)
                m_s[bi, h:h + 1, :] = jnp.broadcast_to(m0[:, h:h + 1], (1, LANES))

    tri = _tri_incl(chunk)
    tri_sel = _stack_selector(tri, CUMSUM_TERMS, 1)
    sel = _replicate_selector(n_gates)
    ones_blk = jnp.ones((chunk, C_DV), F32)
    off = 2 * C_QK
    piece = 4 * LANES
    rep = {}

    def stack(ref, rows):
        return jnp.concatenate([ref[bi, rows, :] for bi in range(nb)], axis=0)

    def unstack(ref, rows, cols, val):
        for bi in range(nb):
            ref[bi, rows, cols] = val[bi * chunk:(bi + 1) * chunk, :]

    def proj_task(c):
        rows = slice(c * chunk, (c + 1) * chunk)
        x1 = stack(x_ref, rows) + jnp.dot(stack(mix_ref, rows).astype(BF16), wo0_ref[...],
                                          preferred_element_type=F32)
        unstack(x1_s, rows, slice(0, D_MODEL), x1)
        yield
        xn = _rms(x1, n1_ref[...]).astype(BF16)
        g = jnp.dot(xn, w1_ref[:, P1_GATES:P1_PAD], preferred_element_type=F32) + bias_ref[...]
        cap = C_GATE_CAP * jnp.tanh(g / C_GATE_CAP)
        unstack(gsc_s, rows, slice(0, LANES), jnp.where(is_f, -_softplus(-cap), cap))
        yield
        for j in range(0, P1_GATES, piece):
            unstack(proj_s, rows, slice(j, j + piece),
                    jnp.dot(xn, w1_ref[:, j:j + piece], preferred_element_type=F32))
            yield

    def gate_task(bi, c):
        rows = slice(c * chunk, (c + 1) * chunk)
        gl = gsc_s[bi, rows, :]
        bcum = _dot_exact_lhs(tri_sel, gl)
        yield
        rep[bi, c] = _replicate(jnp.where(is_f, bcum, gl), sel, n_gates)

    def head_chain(bi, c, h):
        rows = slice(c * chunk, (c + 1) * chunk)
        p, j = h // 2, h % 2
        own_lane = (lane // C_DQK) == j
        own_row = (krow // C_DQK) == j
        qp = proj_s[bi, rows, p * LANES:(p + 1) * LANES] * (C_DQK ** -0.5)
        kp = proj_s[bi, rows, C_QK + p * LANES:C_QK + (p + 1) * LANES]
        v = proj_s[bi, rows, off + h * C_DV:off + (h + 1) * C_DV]
        vaug = jnp.concatenate([v, ones_blk], axis=1)
        qk = _dot_nt(jnp.where(own_lane, qp, 0.0), kp)
        cm = caug_s[bi, h]
        qc = _dot(qp, cm)
        i_rep = rep[bi, c][:, h * LANES:(h + 1) * LANES]
        b_rep = rep[bi, c][:, (C_HEADS + h) * LANES:(C_HEADS + h + 1) * LANES]
        b_row = _col_to_row(b_rep[:, 0:chunk])
        i_row = _col_to_row(i_rep[:, 0:chunk])
        dmat = jnp.where(tri, b_rep[:, 0:chunk] - b_row + i_row, -jnp.inf)
        dmax = b_rep + _cummax_rows(i_rep - b_rep)
        m = m_s[bi, h:h + 1, :]
        inter = b_rep + m
        mt = jnp.maximum(inter, dmax)
        m_new = mt[chunk - 1:chunk, :]
        b_last = b_rep[chunk - 1:chunk, :]
        w_s = jnp.exp(b_last - b_rep + i_rep - m_new)
        e_c = jnp.exp(b_last + m - m_new)
        e_inter = jnp.exp(inter - mt)
        own_k = slice(j * C_DQK, (j + 1) * C_DQK)
        upd = _dot((w_s * kp).T[own_k, :], vaug)
        yield
        pm = jnp.exp(dmat - mt[:, 0:chunk]) * qk
        pv = _dot(pm, vaug)
        caug_s[bi, h, own_k, :] = jnp.concatenate([e_c, e_c], axis=1) * cm[own_k, :] + upd
        m_s[bi, h:h + 1, :] = m_new
        yield
        num = e_inter * qc[:, 0:C_DV] + pv[:, 0:C_DV]
        den = e_inter * qc[:, C_DV:2 * C_DV] + pv[:, C_DV:2 * C_DV]
        hh = num / jnp.maximum(jnp.abs(den), jnp.exp(-mt))
        og = proj_s[bi, rows, off + C_V + h * C_DV:off + C_V + (h + 1) * C_DV]
        z = proj_s[bi, rows, off + 2 * C_V + h * C_DV:off + 2 * C_V + (h + 1) * C_DV]
        h_s[bi, rows, h * C_DV:(h + 1) * C_DV] = hh * _sigmoid(og) * _silu(z)

    def final_task(c):
        rows = slice(c * chunk, (c + 1) * chunk)
        x2 = stack(x1_s, rows) + jnp.dot(stack(h_s, rows).astype(BF16), wo1_ref[...],
                                         preferred_element_type=F32)
        yield
        unstack(y_ref, rows, slice(0, D_MODEL), _rms(x2, nf_ref[...]))

    tasks = []
    for c in range(n_chunks):
        tasks.append((("p", c), "proj", [], functools.partial(proj_task, c)))
    for c in range(n_chunks):
        heads = [("h", bi, c, h) for bi in range(nb) for h in range(C_HEADS)]
        tasks.append((("f", c), "final", heads, functools.partial(final_task, c)))
        for bi in range(nb):
            tasks.append((("g", bi, c), "gates", [("p", c)], functools.partial(gate_task, bi, c)))
            for h in range(C_HEADS):
                deps = [("p", c), ("g", bi, c)] + ([("h", bi, c - 1, h)] if c else [])
                tasks.append((("h", bi, c, h), "heads", deps, functools.partial(head_chain, bi, c, h)))
    _run_tasks(tasks, {"proj": 1, "final": 1})

    @pl.when(i == pl.num_programs(1) - 1)
    def _():
        for bi in range(nb):
            for h in range(C_HEADS):
                j = h % 2
                cout_ref[bi, h] = caug_s[bi, h, j * C_DQK:(j + 1) * C_DQK, 0:C_DV]
            for p in range(C_HEADS // 2):
                n_col = caug_s[bi, 2 * p, :, C_DV:C_DV + 1] + caug_s[bi, 2 * p + 1, :, C_DV:C_DV + 1]
                nout_ref[bi, p:p + 1, :] = _col_to_row(n_col)
        mout_ref[...] = m_s[...]


def _layer1(x3d, mix3d, c0, n0, m0, q, nb, tb, chunk):
    bsz, seq, _ = x3d.shape
    blk = lambda b, i: (b, i, 0)
    per_b3 = lambda b, i: (b, 0, 0)
    per_b4 = lambda b, i: (b, 0, 0, 0)
    kern = functools.partial(_layer1_kernel, nb=nb, tb=tb, chunk=chunk)
    pairs = C_HEADS // 2
    return pl.pallas_call(
        kern,
        grid=(bsz // nb, seq // tb),
        in_specs=[pl.BlockSpec((nb, tb, D_MODEL), blk),
                  pl.BlockSpec((nb, tb, MIX0), blk),
                  pl.BlockSpec((nb, C_HEADS, C_DQK, C_DV), per_b4),
                  pl.BlockSpec((nb, pairs, LANES), per_b3),
                  pl.BlockSpec((nb, 1, C_HEADS), per_b3),
                  _resident((MIX0, D_MODEL)),
                  _resident((1, D_MODEL)),
                  _resident((D_MODEL, P1_PAD)),
                  _resident((1, LANES)),
                  _resident((C_V, D_MODEL)),
                  _resident((1, D_MODEL))],
        out_specs=[pl.BlockSpec((nb, tb, D_MODEL), blk),
                   pl.BlockSpec((nb, C_HEADS, C_DQK, C_DV), per_b4),
                   pl.BlockSpec((nb, pairs, LANES), per_b3),
                   pl.BlockSpec((nb, C_HEADS, LANES), per_b3)],
        out_shape=[jax.ShapeDtypeStruct((bsz, seq, D_MODEL), F32),
                   jax.ShapeDtypeStruct((bsz, C_HEADS, C_DQK, C_DV), F32),
                   jax.ShapeDtypeStruct((bsz, pairs, LANES), F32),
                   jax.ShapeDtypeStruct((bsz, C_HEADS, LANES), F32)],
        scratch_shapes=[pltpu.VMEM((nb, tb, P1_PAD), F32),
                        pltpu.VMEM((nb, tb, D_MODEL), F32),
                        pltpu.VMEM((nb, tb, C_V), F32),
                        pltpu.VMEM((nb, tb, LANES), F32),
                        pltpu.VMEM((nb, C_HEADS, LANES, 2 * C_DV), F32),
                        pltpu.VMEM((nb, C_HEADS, LANES), F32)],
        compiler_params=pltpu.CompilerParams(dimension_semantics=("parallel", "arbitrary"),
                                             vmem_limit_bytes=VMEM_LIMIT),
        name="layer1",
    )(x3d, mix3d, c0, n0, m0, q["wo0"], q["norm1"], q["w1"], q["gate_bias"], q["wo1"], q["normf"])


def _gate_tile(cols):
    tiled = jnp.concatenate([cols.astype(F32)] * GATE_COPIES, axis=1)
    return jnp.pad(tiled, ((0, 0), (0, LANES - tiled.shape[1])))


def _prepare_params(p):
    w_in0 = p["w_in0"]
    a_cols = A_MAIN + 2 * A_HEADS
    def place(cols, offset, width):
        return jnp.pad(cols, ((0, 0), (offset, width - offset - cols.shape[1])))

    n0 = 2 * A_HEADS
    g0 = w_in0[:, A_MAIN:a_cols]
    w0 = (place(w_in0[:, 0:a_cols], 0, P0_PAD) + place(g0, P0_GATES + n0, P0_PAD)
          + place(g0, P0_GATES + 2 * n0, P0_PAD) + place(w_in0[:, a_cols:], P0_B, P0_PAD)).astype(BF16)
    w_in1 = p["w_in1"]
    n1 = 2 * C_HEADS
    g1 = w_in1[:, P1_GATES:]
    w1 = (place(w_in1, 0, P1_PAD) + place(g1, P1_GATES + n1, P1_PAD)
          + place(g1, P1_GATES + 2 * n1, P1_PAD)).astype(BF16)
    zeros_h = jnp.zeros((A_HEADS,), F32)
    alog = _gate_tile(jnp.concatenate([zeros_h, p["gdn_a_log"]]).reshape(1, -1))
    dtb = _gate_tile(jnp.concatenate([zeros_h, p["gdn_dt_bias"]]).reshape(1, -1))
    gate_bias = _gate_tile(jnp.concatenate([p["mlstm_b_i"], p["mlstm_b_f"]]).reshape(1, -1))
    w2a2 = jnp.zeros((2 * B_LORA, 2 * B_W), F32)
    w2a2 = w2a2.at[0:B_LORA, 0:B_W].set(p["rwkv_w2"]).at[B_LORA:, B_W:].set(p["rwkv_a2"])
    row = lambda v: v.reshape(1, -1).astype(F32)
    return dict(
        w0=w0, w1=w1,
        wo0=p["w_out0"].astype(BF16), wo1=p["w_out1"].astype(BF16), w2a2=w2a2.astype(BF16),
        norm0=row(p["norm0_w"]), norm1=row(p["norm1_w"]), normf=row(p["norm_f_w"]),
        conv_w=p["gdn_conv_w"].astype(F32), alog=alog, dtb=dtb, gnw=row(p["gdn_norm_w"]),
        mu=row(p["rwkv_mu"]), rw0=row(p["rwkv_w0"]), ra0=row(p["rwkv_a0"]),
        k_k=row(p["rwkv_k_k"]), k_a=row(p["rwkv_k_a"]), r_k=row(p["rwkv_r_k"]),
        ln_w=row(p["rwkv_ln_w"]), ln_b=row(p["rwkv_ln_b"]), gate_bias=gate_bias,
    )


def _tiles(bsz, seq):
    chunk = math.gcd(seq, MAX_CHUNK)
    tb = min(seq, 256)
    nb = 2 if seq > MAX_CHUNK else 8
    nb = math.gcd(nb, bsz)
    tt = min(seq, PROJ_ROWS)
    ns = math.gcd(min(PROJ_ROWS // tt, PROJ_MAX_SEQS), bsz)
    return chunk, tb, nb, tt, ns


def _trunk(x, state, q):
    conv0, s_gdn, x_prev, s_rwkv, c0, n0, m0 = state
    bsz, seq, _ = x.shape
    chunk, tb, nb, tt, ns = _tiles(bsz, seq)
    xn_last, prev_row = _aux(x[:, -1, :], x_prev, q["norm0"], q["w0"])
    fa, fb, conv_new = _proj0(x, conv0, prev_row.reshape(bsz, 1, B_COLS), q, ns, tt)
    mix, s_gdn_new, s_rwkv_new = _layer0_mixers(fa, fb, s_gdn, s_rwkv, q, nb, tb, chunk)
    y, c_new, n_new, m_new = _layer1(x, mix, c0, n0.reshape(bsz, C_HEADS // 2, LANES),
                                     m0.reshape(bsz, 1, C_HEADS), q, nb, tb, chunk)
    return y, (conv_new, s_gdn_new, xn_last, s_rwkv_new, c_new,
               n_new.reshape(bsz, C_HEADS, C_DQK), m_new[:, :, 0])


def kernel(x_prompt, x_sample, state_gdn_conv, state_gdn, state_rwkv_shift, state_rwkv,
           state_mlstm_c, state_mlstm_n, state_mlstm_m,
           norm0_w, w_in0, w_out0, gdn_conv_w, gdn_a_log, gdn_dt_bias, gdn_norm_w,
           rwkv_mu, rwkv_w0, rwkv_w2, rwkv_a0, rwkv_a2, rwkv_k_k, rwkv_k_a, rwkv_r_k,
           rwkv_ln_w, rwkv_ln_b, norm1_w, w_in1, w_out1, mlstm_b_i, mlstm_b_f, norm_f_w):
    q = _prepare_params(dict(
        norm0_w=norm0_w, w_in0=w_in0, w_out0=w_out0, gdn_conv_w=gdn_conv_w, gdn_a_log=gdn_a_log,
        gdn_dt_bias=gdn_dt_bias, gdn_norm_w=gdn_norm_w, rwkv_mu=rwkv_mu, rwkv_w0=rwkv_w0,
        rwkv_w2=rwkv_w2, rwkv_a0=rwkv_a0, rwkv_a2=rwkv_a2, rwkv_k_k=rwkv_k_k, rwkv_k_a=rwkv_k_a,
        rwkv_r_k=rwkv_r_k, rwkv_ln_w=rwkv_ln_w, rwkv_ln_b=rwkv_ln_b, norm1_w=norm1_w, w_in1=w_in1,
        w_out1=w_out1, mlstm_b_i=mlstm_b_i, mlstm_b_f=mlstm_b_f, norm_f_w=norm_f_w))
    bsz = x_prompt.shape[0]
    prompt_state = (
        jnp.zeros((bsz, A_CONV - 1, A_QKV), F32),
        jnp.zeros((bsz, A_HEADS, A_DK, A_DV), F32),
        jnp.zeros((bsz, D_MODEL), F32),
        jnp.zeros((bsz, B_HEADS, B_N, B_N), F32),
        jnp.zeros((bsz, C_HEADS, C_DQK, C_DV), F32),
        jnp.zeros((bsz, C_HEADS, C_DQK), F32),
        jnp.zeros((bsz, C_HEADS), F32),
    )
    y_p, sp = _trunk(x_prompt, prompt_state, q)
    sample_state = (state_gdn_conv, state_gdn, state_rwkv_shift, state_rwkv,
                    state_mlstm_c, state_mlstm_n, state_mlstm_m)
    y_s, ss = _trunk(x_sample, sample_state, q)
    return (y_p, y_s) + tuple(sp) + tuple(ss)
```

```python
import functools
import math

import jax
import jax.numpy as jnp
from jax import lax
from jax.experimental import pallas as pl
from jax.experimental.pallas import tpu as pltpu

F32 = jnp.float32
BF16 = jnp.bfloat16

D_MODEL = 1024
EPS = 1e-6
LANES = 128
SUBLANES = 8
A_HEADS, A_DK, A_DV, A_CONV = 4, 128, 128, 4
A_KEY = A_HEADS * A_DK
A_VAL = A_HEADS * A_DV
A_QKV = 2 * A_KEY + A_VAL
A_MAIN = A_QKV + A_VAL
B_HEADS, B_N, B_LORA = 8, 64, 64
B_W = B_HEADS * B_N
B_COLS = 4 * B_W + 2 * B_LORA
B_PAIRS = B_HEADS // 2
B_GN_EPS = 64e-5
B_DECAY_SCALE = math.exp(-0.5)
MIX0 = A_VAL + B_W
C_HEADS, C_DQK, C_DV = 8, 64, 128
C_QK = C_HEADS * C_DQK
C_V = C_HEADS * C_DV
C_GATE_CAP = 15.0
P0_GATES = A_MAIN
P0_B = A_MAIN + LANES
P0_PAD = P0_B + B_COLS
P1_GATES = 2 * C_QK + 3 * C_V
P1_PAD = P1_GATES + LANES
MAX_CHUNK = 64
GATE_COPIES = 3
CUMSUM_TERMS = 3
SUM_TERMS = 2
VMEM_LIMIT = 56 * 1024 * 1024


def _dot(a, b):
    return jnp.dot(a.astype(BF16), b.astype(BF16), preferred_element_type=F32)


def _dot_nt(a, b):
    return lax.dot_general(a.astype(BF16), b.astype(BF16), (((1,), (1,)), ((), ())),
                           preferred_element_type=F32)


def _dot_tn(a, b):
    return lax.dot_general(a.astype(BF16), b.astype(BF16), (((0,), (0,)), ((), ())),
                           preferred_element_type=F32)


def _split(x, terms):
    out = []
    for _ in range(terms - 1):
        hi = x.astype(BF16)
        out.append(hi)
        x = x - hi.astype(F32)
    out.append(x.astype(BF16))
    return out


def _stack_selector(sel, terms, axis):
    s = jnp.where(sel, 1.0, 0.0).astype(BF16)
    return jnp.concatenate([s] * terms, axis=axis)


def _dot_exact_lhs(sel_stacked, x, terms=CUMSUM_TERMS):
    return jnp.dot(sel_stacked, jnp.concatenate(_split(x, terms), axis=0), preferred_element_type=F32)


def _dot_exact_rhs(x, sel_stacked, terms=SUM_TERMS):
    return jnp.dot(jnp.concatenate(_split(x, terms), axis=1), sel_stacked, preferred_element_type=F32)


def _iota2(shape, dim):
    return lax.broadcasted_iota(jnp.int32, shape, dim)


def _tri_incl(n):
    return (_iota2((n, n), 0) >= _iota2((n, n), 1))


def _eye(n):
    return jnp.where(_iota2((n, n), 0) == _iota2((n, n), 1), 1.0, 0.0)


def _col_to_row(col):
    n = col.shape[0]
    eye = _iota2((n, n), 0) == _iota2((n, n), 1)
    return jnp.sum(jnp.where(eye, col, 0.0), axis=0, keepdims=True)


def _row_to_col(row):
    n = row.shape[1]
    eye = _iota2((n, n), 0) == _iota2((n, n), 1)
    return jnp.sum(jnp.where(eye, row, 0.0), axis=1, keepdims=True)


def _replicate(x, sel, group):
    hi, mid, lo = _split(x, GATE_COPIES)
    lane = _iota2(x.shape, 1)
    packed = jnp.where(lane < group, hi, jnp.where(lane < 2 * group, mid, lo))
    packed = jnp.where(lane < GATE_COPIES * group, packed, jnp.zeros_like(packed))
    return jnp.dot(packed, sel, preferred_element_type=F32)


def _replicate_selector(group):
    assert group & (group - 1) == 0 and LANES == 128
    k = _iota2((LANES, group * LANES), 0)
    col = _iota2((LANES, group * LANES), 1)
    hit = (k < GATE_COPIES * group) & (jnp.bitwise_and(k, group - 1) == jnp.right_shift(col, 7))
    return jnp.where(hit, 1.0, 0.0).astype(BF16)


def _cummax_rows(x):
    n = x.shape[0]
    row = _iota2(x.shape, 0)
    sh = 1
    while sh < n:
        x = jnp.maximum(x, jnp.where(row >= sh, pltpu.roll(x, sh, 0), -jnp.inf))
        sh *= 2
    return x


def _inv_unit_lower(p, levels, eye):
    n = p.shape[0]
    t = eye + p
    m = 2
    if m < levels:
        p = _dot(p, p)
        yield
    while m < levels:
        m *= 2
        if m >= levels:
            t = t + _dot(p, t)
        elif n % LANES == 0:
            both = _dot(p, jnp.concatenate([p, t], axis=1))
            p, t = both[:, 0:n], t + both[:, n:2 * n]
            yield
        else:
            both = _dot(jnp.concatenate([p, t], axis=0), p)
            p, t = both[0:n, :], t + both[n:2 * n, :]
            yield
    return t


def _run_tasks(tasks, caps):
    done, live, pending = set(), [], list(tasks)
    while pending or live:
        count = {}
        for _, group, _ in live:
            count[group] = count.get(group, 0) + 1
        still = []
        for key, group, deps, make in pending:
            if all(d in done for d in deps) and count.get(group, 0) < caps.get(group, len(tasks)):
                live.append((key, group, make()))
                count[group] = count.get(group, 0) + 1
            else:
                still.append((key, group, deps, make))
        assert live, "task graph has a cycle or a missing dependency"
        pending = still
        nxt = []
        for key, group, gen in live:
            try:
                next(gen)
                nxt.append((key, group, gen))
            except StopIteration:
                done.add(key)
        live = nxt


def _sigmoid(x):
    return 1.0 / (1.0 + jnp.exp(-x))


def _silu(x):
    return x * _sigmoid(x)


def _softplus(x):
    return jnp.maximum(x, 0.0) + jnp.log1p(jnp.exp(-jnp.abs(x)))


def _rms(x, w):
    return x * lax.rsqrt(jnp.mean(x * x, axis=-1, keepdims=True) + EPS) * w


def _resident(shape):
    zeros = (0,) * len(shape)
    return pl.BlockSpec(shape, lambda *_: zeros, pipeline_mode=pl.Buffered(1))


FA_Z = A_QKV
FA_BETA = A_MAIN
FA_DECAY = A_MAIN + LANES
FA_COLS = A_MAIN + 2 * LANES
_FR, _FLW, _FK, _FV, _FA, _FB, _FG = range(7)
FB_COLS = 7 * B_W
SUB_ROWS = 64
PROJ_ROWS = 512
PROJ_MAX_SEQS = 32


def _proj0_kernel(x_ref, conv0_ref, prev_ref, nw_ref, w_ref, convw_ref, alog_ref, dtb_ref,
                  mu_ref, w0_ref, a0_ref, w2a2_ref, kk_ref, ka_ref,
                  fa_ref, fb_ref, convn_ref, ext_a, ext_b, *, ns, tt):
    i = pl.program_id(1)
    tail = SUBLANES
    rows = ns * tt
    lane = _iota2((1, LANES), 1)
    head0 = lane < B_N
    ones_blk = (_iota2((LANES, LANES), 0) // B_N == _iota2((LANES, LANES), 1) // B_N)
    ones_sel = _stack_selector(ones_blk, SUM_TERMS, 0)

    @pl.when(i == 0)
    def _():
        ext_a[:, 0:tail, :] = jnp.zeros((ns, tail, A_QKV), F32)
        ext_a[:, tail - (A_CONV - 1):tail, :] = conv0_ref[...]
        ext_b[:, 0:tail, :] = jnp.zeros((ns, tail, B_COLS), F32)
        ext_b[:, tail - 1:tail, :] = prev_ref[...]

    @pl.when(i > 0)
    def _():
        ext_a[:, 0:tail, :] = ext_a[:, tt:tt + tail, :]
        ext_b[:, 0:tail, :] = ext_b[:, tt:tt + tail, :]

    xn = _rms(x_ref[...].reshape(rows, D_MODEL), nw_ref[...]).astype(BF16)
    piece = 4 * LANES

    def cols(lo, hi):
        return jnp.dot(xn, w_ref[:, lo:hi], preferred_element_type=F32).reshape(ns, tt, hi - lo)

    for lo in range(0, A_QKV, piece):
        ext_a[:, tail:tail + tt, lo:lo + piece] = cols(lo, lo + piece)
    fa_ref[:, :, FA_Z:FA_Z + A_VAL] = cols(A_QKV, A_MAIN)
    g = cols(P0_GATES, P0_B)
    fa_ref[:, :, FA_BETA:FA_BETA + LANES] = _sigmoid(g)
    fa_ref[:, :, FA_DECAY:FA_DECAY + LANES] = -jnp.exp(alog_ref[...]) * _softplus(g + dtb_ref[...])
    for lo in range(0, B_COLS, piece):
        hi = min(lo + piece, B_COLS)
        ext_b[:, tail:tail + tt, lo:hi] = cols(P0_B + lo, P0_B + hi)

    sub_t = min(tt, SUB_ROWS)
    sub_s = min(ns, SUB_ROWS // sub_t)
    cw = convw_ref[...]
    mu = mu_ref[...]
    for s0 in range(0, ns, sub_s):
        sq = slice(s0, s0 + sub_s)
        for t0 in range(0, tt, sub_t):
            tr = slice(t0, t0 + sub_t)
            y = ext_a[sq, tail + t0:tail + t0 + sub_t, :] * cw[A_CONV - 1:A_CONV, :]
            for j in range(A_CONV - 1):
                sh = A_CONV - 1 - j
                y = y + ext_a[sq, tail + t0 - sh:tail + t0 - sh + sub_t, :] * cw[j:j + 1, :]
            y = _silu(y)
            for h in range(A_HEADS):
                q = y[:, :, h * A_DK:(h + 1) * A_DK]
                fa_ref[sq, tr, h * A_DK:(h + 1) * A_DK] = (
                    q * lax.rsqrt(jnp.sum(q * q, axis=-1, keepdims=True) + EPS) * (A_DK ** -0.5))
                k = y[:, :, A_KEY + h * A_DK:A_KEY + (h + 1) * A_DK]
                fa_ref[sq, tr, A_KEY + h * A_DK:A_KEY + (h + 1) * A_DK] = (
                    k * lax.rsqrt(jnp.sum(k * k, axis=-1, keepdims=True) + EPS))
            fa_ref[sq, tr, 2 * A_KEY:A_QKV] = y[:, :, 2 * A_KEY:A_QKV]
            pb = ext_b[sq, tail + t0:tail + t0 + sub_t, :]
            pb_prev = ext_b[sq, tail + t0 - 1:tail + t0 - 1 + sub_t, :]
            pbs = (pb + (pb_prev - pb) * mu).reshape(sub_s * sub_t, B_COLS)
            lo_ = pbs[:, 4 * B_W:B_COLS]
            lo_ = jnp.where(head0, jnp.tanh(lo_), lo_)
            wa = _dot(lo_, w2a2_ref[...])
            logw = -B_DECAY_SCALE * _sigmoid(w0_ref[...] + wa[:, 0:B_W])
            a = _sigmoid(a0_ref[...] + wa[:, B_W:2 * B_W])
            kb = pbs[:, B_W:2 * B_W]
            kkx = kb * kk_ref[...]

            def put(slot, val):
                fb_ref[sq, tr, slot * B_W:(slot + 1) * B_W] = val.reshape(sub_s, sub_t, B_W)

            kkn = jnp.concatenate(
                [kkx[:, p * LANES:(p + 1) * LANES]
                 * lax.rsqrt(_dot_exact_rhs(kkx[:, p * LANES:(p + 1) * LANES] ** 2, ones_sel) + EPS)
                 for p in range(B_PAIRS)], axis=1)
            put(_FR, pbs[:, 0:B_W])
            put(_FLW, logw)
            put(_FK, kb * (1.0 + (a - 1.0) * ka_ref[...]))
            put(_FV, pbs[:, 2 * B_W:3 * B_W])
            put(_FA, -kkn)
            put(_FB, kkn * a)
            put(_FG, pbs[:, 3 * B_W:4 * B_W])

    @pl.when(i == pl.num_programs(1) - 1)
    def _():
        convn_ref[...] = ext_a[:, tt + tail - (A_CONV - 1):tt + tail, :]


def _proj0(x3d, conv0, prev_row, q, ns, tt):
    bsz, seq, _ = x3d.shape
    blk = lambda b, i: (b, i, 0)
    per_b3 = lambda b, i: (b, 0, 0)
    kern = functools.partial(_proj0_kernel, ns=ns, tt=tt)
    return pl.pallas_call(
        kern,
        grid=(bsz // ns, seq // tt),
        in_specs=[pl.BlockSpec((ns, tt, D_MODEL), blk),
                  pl.BlockSpec((ns, A_CONV - 1, A_QKV), per_b3),
                  pl.BlockSpec((ns, 1, B_COLS), per_b3),
                  _resident((1, D_MODEL)),
                  _resident((D_MODEL, P0_PAD)),
                  _resident((A_CONV, A_QKV)), _resident((1, LANES)), _resident((1, LANES)),
                  _resident((1, B_COLS)), _resident((1, B_W)), _resident((1, B_W)),
                  _resident((2 * B_LORA, 2 * B_W)), _resident((1, B_W)), _resident((1, B_W))],
        out_specs=[pl.BlockSpec((ns, tt, FA_COLS), blk),
                   pl.BlockSpec((ns, tt, FB_COLS), blk),
                   pl.BlockSpec((ns, A_CONV - 1, A_QKV), per_b3)],
        out_shape=[jax.ShapeDtypeStruct((bsz, seq, FA_COLS), F32),
                   jax.ShapeDtypeStruct((bsz, seq, FB_COLS), F32),
                   jax.ShapeDtypeStruct((bsz, A_CONV - 1, A_QKV), F32)],
        scratch_shapes=[pltpu.VMEM((ns, tt + SUBLANES, A_QKV), F32),
                        pltpu.VMEM((ns, tt + SUBLANES, B_COLS), F32)],
        compiler_params=pltpu.CompilerParams(dimension_semantics=("parallel", "arbitrary"),
                                             vmem_limit_bytes=VMEM_LIMIT),
        name="proj0_features",
    )(x3d, conv0, prev_row, q["norm0"], q["w0"], q["conv_w"], q["alog"], q["dtb"],
      q["mu"], q["rw0"], q["ra0"], q["w2a2"], q["k_k"], q["k_a"])


def _aux_kernel(xl_ref, xp_ref, nw_ref, wb_ref, xn_ref, prev_ref):
    xn_ref[...] = _rms(xl_ref[...], nw_ref[...])
    prev_ref[...] = jnp.dot(xp_ref[...].astype(BF16), wb_ref[...], preferred_element_type=F32)


def _aux(x_last, x_prev, norm_w, wb):
    b = x_last.shape[0]
    return pl.pallas_call(
        _aux_kernel,
        out_shape=[jax.ShapeDtypeStruct((b, D_MODEL), F32), jax.ShapeDtypeStruct((b, B_COLS), F32)],
        compiler_params=pltpu.CompilerParams(vmem_limit_bytes=VMEM_LIMIT),
        name="shift_rows",
    )(x_last, x_prev, norm_w, wb)


def _layer0_kernel(fa_ref, fb_ref, sg0_ref, sr0_ref, gnw_ref, rk_ref, lnw_ref, lnb_ref,
                   o_ref, sgout_ref, srout_ref,
                   stg_s, uw_s, attn_s, qg_s, kd_s, ge_s, str_s, tmp_s, *, nb, tb, chunk):
    i = pl.program_id(1)
    n_chunks = tb // chunk
    lane = _iota2((1, LANES), 1)
    head0 = lane < B_N
    ones_blk = (_iota2((LANES, LANES), 0) // B_N == _iota2((LANES, LANES), 1) // B_N)

    @pl.when(i == 0)
    def _():
        for bi in range(nb):
            for p in range(B_PAIRS):
                tmp_s[...] = jnp.zeros((B_N, LANES), F32)
                tmp_s[:, 0:B_N] = sr0_ref[bi, 2 * p + 1]
                shifted = pltpu.roll(tmp_s[...], B_N, 1)
                str_s[bi, p, B_N:LANES, :] = jnp.where(head0, 0.0, shifted)
                tmp_s[:, 0:B_N] = sr0_ref[bi, 2 * p]
                str_s[bi, p, 0:B_N, :] = tmp_s[...]
        stg_s[...] = sg0_ref[...]

    tri = _tri_incl(chunk)
    tri_sel = _stack_selector(tri, CUMSUM_TERMS, 1)
    ones_sel = _stack_selector(ones_blk, SUM_TERMS, 0)
    eye_c = _eye(chunk)
    eye_2c = _eye(2 * chunk)

    n_gates = 2 * A_HEADS
    is_decay = jnp.bitwise_and(lane, n_gates - 1) >= A_HEADS
    sel = _replicate_selector(n_gates)
    strict = _iota2((chunk, chunk), 0) > _iota2((chunk, chunk), 1)
    gnw = gnw_ref[...]
    qkv_s = fa_ref
    rep = {}

    def gdn_gates(bi, c):
        rows = slice(c * chunk, (c + 1) * chunk)
        gcum = _dot_exact_lhs(tri_sel, fa_ref[bi, rows, FA_DECAY:FA_DECAY + LANES])
        yield
        rep[bi, c] = _replicate(jnp.where(is_decay, gcum, fa_ref[bi, rows, FA_BETA:FA_BETA + LANES]),
                                sel, n_gates)

    wpk = A_HEADS * chunk
    head_pk = _iota2((1, wpk), 1) // chunk
    s_pk = _iota2((1, wpk), 1) - head_pk * chunk
    t_pk = _iota2((chunk, wpk), 0)
    diag_pk = t_pk == s_pk
    tri_pk = t_pk >= s_pk
    strict_pk = t_pk > s_pk
    head_of_klane = _iota2((1, A_KEY), 1) // A_DK
    head_of_uwlane = _iota2((1, A_HEADS * (A_DV + A_DK)), 1) // (A_DV + A_DK)

    def widen(x):
        reps = -(-wpk // LANES)
        w = jnp.concatenate([x] * reps, axis=1) if reps > 1 else x
        return w if w.shape[1] == wpk else w[:, 0:wpk]

    def pack_heads(xs):
        out = xs[A_HEADS - 1]
        for h in range(A_HEADS - 2, -1, -1):
            out = jnp.where(head_pk == h, xs[h], out)
        return out

    def block_diag(x):
        return jnp.concatenate([jnp.where(head_pk == h, x, 0.0) for h in range(A_HEADS)], axis=0)

    def gdn_local_head(bi, c, h):
        rows = slice(c * chunk, (c + 1) * chunk)
        q = qkv_s[bi, rows, h * A_DK:(h + 1) * A_DK]
        k = qkv_s[bi, rows, A_KEY + h * A_DK:A_KEY + (h + 1) * A_DK]
        v = qkv_s[bi, rows, 2 * A_KEY + h * A_DV:2 * A_KEY + (h + 1) * A_DV]
        qk_k = _dot_nt(jnp.concatenate([q, k], axis=0), k)
        yield
        b_rep = rep[bi, c][:, h * LANES:(h + 1) * LANES]
        g_rep = rep[bi, c][:, (A_HEADS + h) * LANES:(A_HEADS + h + 1) * LANES]
        g_row = _col_to_row(g_rep[:, 0:chunk])
        g_last = g_rep[chunk - 1:chunk, :]
        decay = jnp.where(tri, jnp.exp(jnp.where(tri, g_rep[:, 0:chunk] - g_row, 0.0)), 0.0)
        eg = jnp.exp(g_rep)
        qg_s[bi, rows, h * A_DK:(h + 1) * A_DK] = q * eg
        kd_s[bi, rows, h * A_DK:(h + 1) * A_DK] = k * jnp.exp(g_last - g_rep)
        ge_s[bi, c, h:h + 1, :] = jnp.exp(g_last)
        attn_s[bi, rows, h * LANES:h * LANES + chunk] = decay * qk_k[0:chunk, :]
        amat = jnp.where(strict, b_rep[:, 0:chunk] * decay * qk_k[chunk:2 * chunk, :], 0.0)
        t_inv = yield from _inv_unit_lower(-amat, chunk, eye_c)
        uw_s[bi, rows, 2 * h * LANES:2 * (h + 1) * LANES] = _dot(
            t_inv, jnp.concatenate([b_rep * v, (b_rep * eg) * k], axis=1))

    def gdn_local(bi, c):
        rows = slice(c * chunk, (c + 1) * chunk)
        heads = range(A_HEADS)
        q = [qkv_s[bi, rows, h * A_DK:(h + 1) * A_DK] for h in heads]
        k = [qkv_s[bi, rows, A_KEY + h * A_DK:A_KEY + (h + 1) * A_DK] for h in heads]
        v = [qkv_s[bi, rows, 2 * A_KEY + h * A_DV:2 * A_KEY + (h + 1) * A_DV] for h in heads]
        qk = [_dot_nt(q[h], k[h]) for h in heads]
        k_bd = jnp.concatenate([jnp.where(head_of_klane == h, jnp.concatenate([k[h]] * A_HEADS, axis=1), 0.0)
                                for h in heads], axis=0)
        kk_pk = _dot_nt(jnp.concatenate(k, axis=1), k_bd)
        yield
        b_rep = [rep[bi, c][:, h * LANES:(h + 1) * LANES] for h in heads]
        g_rep = [rep[bi, c][:, (A_HEADS + h) * LANES:(A_HEADS + h + 1) * LANES] for h in heads]
        eg = [jnp.exp(g) for g in g_rep]
        for h in heads:
            g_row = _col_to_row(g_rep[h][:, 0:chunk])
            g_last = g_rep[h][chunk - 1:chunk, :]
            decay = jnp.where(tri, jnp.exp(jnp.where(tri, g_rep[h][:, 0:chunk] - g_row, 0.0)), 0.0)
            qg_s[bi, rows, h * A_DK:(h + 1) * A_DK] = q[h] * eg[h]
            kd_s[bi, rows, h * A_DK:(h + 1) * A_DK] = k[h] * jnp.exp(g_last - g_rep[h])
            ge_s[bi, c, h:h + 1, :] = jnp.exp(g_last)
            attn_s[bi, rows, h * LANES:h * LANES + chunk] = decay * qk[h]
        g_col_pk = pack_heads([widen(g) for g in g_rep])
        b_pk = pack_heads([widen(b) for b in b_rep])
        g_row_pk = jnp.sum(jnp.where(diag_pk, g_col_pk, 0.0), axis=0, keepdims=True)
        decay_pk = jnp.where(tri_pk, jnp.exp(jnp.where(tri_pk, g_col_pk - g_row_pk, 0.0)), 0.0)
        p = -jnp.where(strict_pk, b_pk * decay_pk * kk_pk, 0.0)
        t = jnp.where(diag_pk, 1.0, 0.0) + p
        m = 2
        if m < chunk:
            p = _dot(p, block_diag(p))
            yield
        while m < chunk:
            m *= 2
            if m >= chunk:
                t = t + _dot(t, block_diag(p))
            else:
                both = _dot(jnp.concatenate([p, t], axis=0), block_diag(p))
                p, t = both[0:chunk, :], t + both[chunk:2 * chunk, :]
                yield
        rhs = jnp.concatenate(
            [jnp.where(head_of_uwlane == h,
                       jnp.concatenate([b_rep[h] * v[h], (b_rep[h] * eg[h]) * k[h]] * A_HEADS, axis=1), 0.0)
             for h in heads], axis=0)
        uw_s[bi, rows, :] = _dot(t, rhs)

    def gdn_state(bi, c, h):
        rows = slice(c * chunk, (c + 1) * chunk)
        uw = uw_s[bi, rows, 2 * h * LANES:2 * (h + 1) * LANES]
        s = stg_s[bi, h]
        wq_s = _dot(jnp.concatenate([uw[:, A_DV:A_DV + A_DK], qg_s[bi, rows, h * A_DK:(h + 1) * A_DK]], axis=0), s)
        yield
        nv = uw[:, 0:A_DV] - wq_s[0:chunk, :]
        o = wq_s[chunk:2 * chunk, :] + _dot(attn_s[bi, rows, h * LANES:h * LANES + chunk], nv)
        stg_s[bi, h] = ge_s[bi, c, h:h + 1, :] * s + _dot_tn(kd_s[bi, rows, h * A_DK:(h + 1) * A_DK], nv)
        yield
        z = fa_ref[bi, rows, FA_Z + h * A_DV:FA_Z + (h + 1) * A_DV]
        o_ref[bi, rows, h * A_DV:(h + 1) * A_DV] = _rms(o, gnw) * _silu(z)

    feat_s = fb_ref
    c2 = 2 * chunk
    ri = _iota2((c2, c2), 0)
    ci = _iota2((c2, c2), 1)
    same_head = (ri // chunk) == (ci // chunk)
    incl_bd = same_head & (ri >= ci)
    strict_bd = same_head & (ri > ci)
    ri4 = _iota2((2 * c2, c2), 0)
    ci4 = _iota2((2 * c2, c2), 1)
    rr4 = ri4 % c2
    lm_mask = ((rr4 // chunk) == (ci4 // chunk)) & ((rr4 > ci4) | ((ri4 >= c2) & (rr4 == ci4)))
    lane_st = (_iota2((c2, LANES), 0) // chunk) == (_iota2((c2, LANES), 1) // B_N)
    local = {}

    def rwkv_local(bi, c, p):
        rows = slice(c * chunk, (c + 1) * chunk)

        def feat(slot):
            return feat_s[bi, rows, slot * B_W + p * LANES:slot * B_W + (p + 1) * LANES]
        r, lw, k, v, a_, b_ = (feat(s) for s in (_FR, _FLW, _FK, _FV, _FA, _FB))
        cs = _dot_exact_lhs(tri_sel, lw)
        yield
        cs_last = cs[chunk - 1:chunk, :]
        e_inv = jnp.exp(-cs)
        e_rem = jnp.exp(cs_last - cs)
        at = a_ * jnp.exp(cs - lw)
        rt = r * jnp.exp(cs)
        bt = b_ * e_inv
        kt = k * e_inv
        a_st = jnp.concatenate([jnp.where(head0, at, 0.0), jnp.where(head0, 0.0, at)], axis=0)
        r_st = jnp.concatenate([jnp.where(head0, rt, 0.0), jnp.where(head0, 0.0, rt)], axis=0)
        b2 = jnp.concatenate([bt, bt], axis=0)
        k2 = jnp.concatenate([kt, kt], axis=0)
        v2 = jnp.concatenate([v, v], axis=0)
        ar_st = jnp.concatenate([a_st, r_st], axis=0)
        if c2 % LANES == 0:
            ar_bk = _dot_nt(ar_st, jnp.concatenate([b2, k2], axis=0))
            ar_b, ar_k = ar_bk[:, 0:c2], ar_bk[:, c2:2 * c2]
        else:
            ar_b = _dot_nt(ar_st, b2)
            ar_k = _dot_nt(ar_st, k2)
        yield
        l_ab = jnp.where(strict_bd, ar_b[0:c2, :], 0.0)
        m_rb = jnp.where(incl_bd, ar_b[c2:2 * c2, :], 0.0)
        lm_k = jnp.where(lm_mask, ar_k, 0.0)
        lm_v = _dot(lm_k, v2)
        sl = slice(p * LANES, (p + 1) * LANES)
        bonus = _dot_exact_rhs(r * k * rk_ref[:, sl], ones_sel) * v
        t_inv = yield from _inv_unit_lower(l_ab, chunk, eye_2c)
        local[bi, c, p] = (t_inv, lm_v, m_rb, jnp.concatenate([at, rt], axis=0),
                           jnp.concatenate([b_ * e_rem, k * e_rem], axis=0), jnp.exp(cs_last), bonus)

    def rwkv_state(bi, c, p):
        rows = slice(c * chunk, (c + 1) * chunk)
        sl = slice(p * LANES, (p + 1) * LANES)
        t_inv, lm_v, m_rb, at_rt, bk_rem, decay_end, bonus = local.pop((bi, c, p))
        v = feat_s[bi, rows, _FV * B_W + p * LANES:_FV * B_W + (p + 1) * LANES]
        gate = feat_s[bi, rows, _FG * B_W + p * LANES:_FG * B_W + (p + 1) * LANES]
        s = str_s[bi, p]
        ar = _dot_nt(at_rt, s)
        yield
        ah = ar[0:chunk, :]
        rh = ar[chunk:c2, :]
        x_st = jnp.where(lane_st, jnp.concatenate([ah, ah], axis=0) + lm_v[0:c2, :], 0.0)
        u_st = _dot(t_inv, x_st)
        yield
        o_st = jnp.where(lane_st, _dot(m_rb, u_st) + lm_v[c2:2 * c2, :], 0.0)
        u = u_st[0:chunk, :] + u_st[chunk:c2, :]
        upd = _dot_tn(jnp.concatenate([u, v], axis=0), bk_rem)
        yield
        o = rh + o_st[0:chunk, :] + o_st[chunk:c2, :]
        str_s[bi, p] = s * decay_end + jnp.where(ones_blk, upd, 0.0)
        mean = _dot_exact_rhs(o, ones_sel) * (1.0 / B_N)
        yield
        d = o - mean
        var = _dot_exact_rhs(d * d, ones_sel) * (1.0 / B_N)
        yield
        ob = d * lax.rsqrt(var + B_GN_EPS) * lnw_ref[:, sl] + lnb_ref[:, sl] + bonus
        o_ref[bi, rows, A_VAL + p * LANES:A_VAL + (p + 1) * LANES] = ob * _silu(gate)

    packed = wpk >= LANES
    tasks = []
    blocks = [(bi, c) for c in range(n_chunks) for bi in range(nb)]
    for bi, c in blocks:
        for p in range(B_PAIRS):
            deps = [("rl", bi, c, p)] + ([("rs", bi, c - 1, p)] if c else [])
            tasks.append((("rs", bi, c, p), "rwkv_state", deps, functools.partial(rwkv_state, bi, c, p)))
        for h in range(A_HEADS):
            local_key = ("gl", bi, c) if packed else ("gl", bi, c, h)
            deps = [local_key] + ([("gs", bi, c - 1, h)] if c else [])
            tasks.append((("gs", bi, c, h), "gdn_state", deps, functools.partial(gdn_state, bi, c, h)))
    for bi, c in blocks:
        for p in range(B_PAIRS):
            tasks.append((("rl", bi, c, p), "rwkv_local", [], functools.partial(rwkv_local, bi, c, p)))
        if packed:
            tasks.append((("gl", bi, c), "gdn_local", [("gg", bi, c)], functools.partial(gdn_local, bi, c)))
        else:
            for h in range(A_HEADS):
                tasks.append((("gl", bi, c, h), "gdn_local", [("gg", bi, c)],
                              functools.partial(gdn_local_head, bi, c, h)))
    for bi, c in blocks:
        tasks.append((("gg", bi, c), "gdn_gates", [], functools.partial(gdn_gates, bi, c)))
    _run_tasks(tasks, {"gdn_gates": 2, "rwkv_local": 2 * B_PAIRS, "gdn_local": 4 if packed else 2 * A_HEADS})

    @pl.when(i == pl.num_programs(1) - 1)
    def _():
        for bi in range(nb):
            for p in range(B_PAIRS):
                srout_ref[bi, 2 * p] = str_s[bi, p, 0:B_N, 0:B_N]
                shifted = pltpu.roll(str_s[bi, p, B_N:LANES, :], B_N, 1)
                srout_ref[bi, 2 * p + 1] = shifted[:, 0:B_N]
        sgout_ref[...] = stg_s[...]


def _layer0_mixers(fa, fb, s_gdn, s_rwkv, q, nb, tb, chunk):
    bsz, seq, _ = fa.shape
    blk = lambda b, i: (b, i, 0)
    per_b4 = lambda b, i: (b, 0, 0, 0)
    fix2 = lambda b, i: (0, 0)
    kern = functools.partial(_layer0_kernel, nb=nb, tb=tb, chunk=chunk)
    vec = pl.BlockSpec((1, B_W), fix2)
    return pl.pallas_call(
        kern,
        grid=(bsz // nb, seq // tb),
        in_specs=[pl.BlockSpec((nb, tb, FA_COLS), blk),
                  pl.BlockSpec((nb, tb, FB_COLS), blk),
                  pl.BlockSpec((nb, A_HEADS, A_DK, A_DV), per_b4),
                  pl.BlockSpec((nb, B_HEADS, B_N, B_N), per_b4),
                  pl.BlockSpec((1, A_DV), fix2), vec, vec, vec],
        out_specs=[pl.BlockSpec((nb, tb, MIX0), blk),
                   pl.BlockSpec((nb, A_HEADS, A_DK, A_DV), per_b4),
                   pl.BlockSpec((nb, B_HEADS, B_N, B_N), per_b4)],
        out_shape=[jax.ShapeDtypeStruct((bsz, seq, MIX0), F32),
                   jax.ShapeDtypeStruct((bsz, A_HEADS, A_DK, A_DV), F32),
                   jax.ShapeDtypeStruct((bsz, B_HEADS, B_N, B_N), F32)],
        scratch_shapes=[pltpu.VMEM((nb, A_HEADS, A_DK, A_DV), F32),
                        pltpu.VMEM((nb, tb, A_HEADS * (A_DV + A_DK)), F32),
                        pltpu.VMEM((nb, tb, A_HEADS * LANES), F32),
                        pltpu.VMEM((nb, tb, A_KEY), F32),
                        pltpu.VMEM((nb, tb, A_KEY), F32),
                        pltpu.VMEM((nb, tb // chunk, SUBLANES, LANES), F32),
                        pltpu.VMEM((nb, B_PAIRS, LANES, LANES), F32),
                        pltpu.VMEM((B_N, LANES), F32)],
        compiler_params=pltpu.CompilerParams(dimension_semantics=("parallel", "arbitrary"),
                                             vmem_limit_bytes=VMEM_LIMIT),
        name="layer0_mixers",
    )(fa, fb, s_gdn, s_rwkv, q["gnw"], q["r_k"], q["ln_w"], q["ln_b"])


def _layer1_kernel(x_ref, mix_ref, c0_ref, n0_ref, m0_ref, wo0_ref, n1_ref, w1_ref, bias_ref, wo1_ref, nf_ref,
                   y_ref, cout_ref, nout_ref, mout_ref,
                   proj_s, x1_s, h_s, gsc_s, caug_s, m_s, *, nb, tb, chunk):
    i = pl.program_id(1)
    n_chunks = tb // chunk
    lane = _iota2((1, LANES), 1)
    krow = _iota2((LANES, 1), 0)
    n_gates = 2 * C_HEADS
    is_f = jnp.bitwise_and(lane, n_gates - 1) >= C_HEADS

    @pl.when(i == 0)
    def _():
        for bi in range(nb):
            m0 = m0_ref[bi]
            for h in range(C_HEADS):
                j = h % 2
                caug_s[bi, h] = jnp.zeros((LANES, 2 * C_DV), F32)
                caug_s[bi, h, j * C_DQK:(j + 1) * C_DQK, 0:C_DV] = c0_ref[bi, h]
                n_col = _row_to_col(n0_ref[bi, h // 2:h // 2 + 1, :])
                own = (krow // C_DQK) == j
                caug_s[bi, h, :, C_DV:2 * C_DV] = jnp.where(own, jnp.broadcast_to(n_col, (LANES, C_DV)), 0.0)
                m_s[bi, h:h + 1, :] = jnp.broadcast_to(m0[:, h:h + 1], (1, LANES))

    tri = _tri_incl(chunk)
    tri_sel = _stack_selector(tri, CUMSUM_TERMS, 1)
    sel = _replicate_selector(n_gates)
    ones_blk = jnp.ones((chunk, C_DV), F32)
    off = 2 * C_QK
    piece = 4 * LANES
    rep = {}

    def stack(ref, rows):
        return jnp.concatenate([ref[bi, rows, :] for bi in range(nb)], axis=0)

    def unstack(ref, rows, cols, val):
        for bi in range(nb):
            ref[bi, rows, cols] = val[bi * chunk:(bi + 1) * chunk, :]

    def proj_task(c):
        rows = slice(c * chunk, (c + 1) * chunk)
        x1 = stack(x_ref, rows) + jnp.dot(stack(mix_ref, rows).astype(BF16), wo0_ref[...],
                                          preferred_element_type=F32)
        unstack(x1_s, rows, slice(0, D_MODEL), x1)
        yield
        xn = _rms(x1, n1_ref[...]).astype(BF16)
        g = jnp.dot(xn, w1_ref[:, P1_GATES:P1_PAD], preferred_element_type=F32) + bias_ref[...]
        cap = C_GATE_CAP * jnp.tanh(g / C_GATE_CAP)
        unstack(gsc_s, rows, slice(0, LANES), jnp.where(is_f, -_softplus(-cap), cap))
        yield
        for j in range(0, P1_GATES, piece):
            unstack(proj_s, rows, slice(j, j + piece),
                    jnp.dot(xn, w1_ref[:, j:j + piece], preferred_element_type=F32))
            yield

    def gate_task(bi, c):
        rows = slice(c * chunk, (c + 1) * chunk)
        gl = gsc_s[bi, rows, :]
        bcum = _dot_exact_lhs(tri_sel, gl)
        yield
        rep[bi, c] = _replicate(jnp.where(is_f, bcum, gl), sel, n_gates)

    def head_chain(bi, c, h):
        rows = slice(c * chunk, (c + 1) * chunk)
        p, j = h // 2, h % 2
        own_lane = (lane // C_DQK) == j
        own_row = (krow // C_DQK) == j
        qp = proj_s[bi, rows, p * LANES:(p + 1) * LANES] * (C_DQK ** -0.5)
        kp = proj_s[bi, rows, C_QK + p * LANES:C_QK + (p + 1) * LANES]
        v = proj_s[bi, rows, off + h * C_DV:off + (h + 1) * C_DV]
        vaug = jnp.concatenate([v, ones_blk], axis=1)
        qk = _dot_nt(jnp.where(own_lane, qp, 0.0), kp)
        cm = caug_s[bi, h]
        qc = _dot(qp, cm)
        i_rep = rep[bi, c][:, h * LANES:(h + 1) * LANES]
        b_rep = rep[bi, c][:, (C_HEADS + h) * LANES:(C_HEADS + h + 1) * LANES]
        b_row = _col_to_row(b_rep[:, 0:chunk])
        i_row = _col_to_row(i_rep[:, 0:chunk])
        dmat = jnp.where(tri, b_rep[:, 0:chunk] - b_row + i_row, -jnp.inf)
        dmax = b_rep + _cummax_rows(i_rep - b_rep)
        m = m_s[bi, h:h + 1, :]
        inter = b_rep + m
        mt = jnp.maximum(inter, dmax)
        m_new = mt[chunk - 1:chunk, :]
        b_last = b_rep[chunk - 1:chunk, :]
        w_s = jnp.exp(b_last - b_rep + i_rep - m_new)
        e_c = jnp.exp(b_last + m - m_new)
        e_inter = jnp.exp(inter - mt)
        own_k = slice(j * C_DQK, (j + 1) * C_DQK)
        upd = _dot((w_s * kp).T[own_k, :], vaug)
        yield
        pm = jnp.exp(dmat - mt[:, 0:chunk]) * qk
        pv = _dot(pm, vaug)
        caug_s[bi, h, own_k, :] = jnp.concatenate([e_c, e_c], axis=1) * cm[own_k, :] + upd
        m_s[bi, h:h + 1, :] = m_new
        yield
        num = e_inter * qc[:, 0:C_DV] + pv[:, 0:C_DV]
        den = e_inter * qc[:, C_DV:2 * C_DV] + pv[:, C_DV:2 * C_DV]
        hh = num / jnp.maximum(jnp.abs(den), jnp.exp(-mt))
        og = proj_s[bi, rows, off + C_V + h * C_DV:off + C_V + (h + 1) * C_DV]
        z = proj_s[bi, rows, off + 2 * C_V + h * C_DV:off + 2 * C_V + (h + 1) * C_DV]
        h_s[bi, rows, h * C_DV:(h + 1) * C_DV] = hh * _sigmoid(og) * _silu(z)

    def final_task(c):
        rows = slice(c * chunk, (c + 1) * chunk)
        x2 = stack(x1_s, rows) + jnp.dot(stack(h_s, rows).astype(BF16), wo1_ref[...],
                                         preferred_element_type=F32)
        yield
        unstack(y_ref, rows, slice(0, D_MODEL), _rms(x2, nf_ref[...]))

    tasks = []
    for c in range(n_chunks):
        tasks.append((("p", c), "proj", [], functools.partial(proj_task, c)))
    for c in range(n_chunks):
        heads = [("h", bi, c, h) for bi in range(nb) for h in range(C_HEADS)]
        tasks.append((("f", c), "final", heads, functools.partial(final_task, c)))
        for bi in range(nb):
            tasks.append((("g", bi, c), "gates", [("p", c)], functools.partial(gate_task, bi, c)))
            for h in range(C_HEADS):
                deps = [("p", c), ("g", bi, c)] + ([("h", bi, c - 1, h)] if c else [])
                tasks.append((("h", bi, c, h), "heads", deps, functools.partial(head_chain, bi, c, h)))
    _run_tasks(tasks, {"proj": 1, "final": 1})

    @pl.when(i == pl.num_programs(1) - 1)
    def _():
        for bi in range(nb):
            for h in range(C_HEADS):
                j = h % 2
                cout_ref[bi, h] = caug_s[bi, h, j * C_DQK:(j + 1) * C_DQK, 0:C_DV]
            for p in range(C_HEADS // 2):
                n_col = caug_s[bi, 2 * p, :, C_DV:C_DV + 1] + caug_s[bi, 2 * p + 1, :, C_DV:C_DV + 1]
                nout_ref[bi, p:p + 1, :] = _col_to_row(n_col)
        mout_ref[...] = m_s[...]


def _layer1(x3d, mix3d, c0, n0, m0, q, nb, tb, chunk):
    bsz, seq, _ = x3d.shape
    blk = lambda b, i: (b, i, 0)
    per_b3 = lambda b, i: (b, 0, 0)
    per_b4 = lambda b, i: (b, 0, 0, 0)
    kern = functools.partial(_layer1_kernel, nb=nb, tb=tb, chunk=chunk)
    pairs = C_HEADS // 2
    return pl.pallas_call(
        kern,
        grid=(bsz // nb, seq // tb),
        in_specs=[pl.BlockSpec((nb, tb, D_MODEL), blk),
                  pl.BlockSpec((nb, tb, MIX0), blk),
                  pl.BlockSpec((nb, C_HEADS, C_DQK, C_DV), per_b4),
                  pl.BlockSpec((nb, pairs, LANES), per_b3),
                  pl.BlockSpec((nb, 1, C_HEADS), per_b3),
                  _resident((MIX0, D_MODEL)),
                  _resident((1, D_MODEL)),
                  _resident((D_MODEL, P1_PAD)),
                  _resident((1, LANES)),
                  _resident((C_V, D_MODEL)),
                  _resident((1, D_MODEL))],
        out_specs=[pl.BlockSpec((nb, tb, D_MODEL), blk),
                   pl.BlockSpec((nb, C_HEADS, C_DQK, C_DV), per_b4),
                   pl.BlockSpec((nb, pairs, LANES), per_b3),
                   pl.BlockSpec((nb, C_HEADS, LANES), per_b3)],
        out_shape=[jax.ShapeDtypeStruct((bsz, seq, D_MODEL), F32),
                   jax.ShapeDtypeStruct((bsz, C_HEADS, C_DQK, C_DV), F32),
                   jax.ShapeDtypeStruct((bsz, pairs, LANES), F32),
                   jax.ShapeDtypeStruct((bsz, C_HEADS, LANES), F32)],
        scratch_shapes=[pltpu.VMEM((nb, tb, P1_PAD), F32),
                        pltpu.VMEM((nb, tb, D_MODEL), F32),
                        pltpu.VMEM((nb, tb, C_V), F32),
                        pltpu.VMEM((nb, tb, LANES), F32),
                        pltpu.VMEM((nb, C_HEADS, LANES, 2 * C_DV), F32),
                        pltpu.VMEM((nb, C_HEADS, LANES), F32)],
        compiler_params=pltpu.CompilerParams(dimension_semantics=("parallel", "arbitrary"),
                                             vmem_limit_bytes=VMEM_LIMIT),
        name="layer1",
    )(x3d, mix3d, c0, n0, m0, q["wo0"], q["norm1"], q["w1"], q["gate_bias"], q["wo1"], q["normf"])


def _gate_tile(cols):
    tiled = jnp.concatenate([cols.astype(F32)] * GATE_COPIES, axis=1)
    return jnp.pad(tiled, ((0, 0), (0, LANES - tiled.shape[1])))


def _prepare_params(p):
    w_in0 = p["w_in0"]
    a_cols = A_MAIN + 2 * A_HEADS
    w0 = jnp.concatenate([w_in0[:, 0:A_MAIN], _gate_tile(w_in0[:, A_MAIN:a_cols]), w_in0[:, a_cols:]], axis=1)
    w_in1 = p["w_in1"]
    w1 = jnp.concatenate([w_in1[:, 0:P1_GATES], _gate_tile(w_in1[:, P1_GATES:])], axis=1)
    zeros_h = jnp.zeros((A_HEADS,), F32)
    alog = _gate_tile(jnp.concatenate([zeros_h, p["gdn_a_log"]]).reshape(1, -1))
    dtb = _gate_tile(jnp.concatenate([zeros_h, p["gdn_dt_bias"]]).reshape(1, -1))
    gate_bias = _gate_tile(jnp.concatenate([p["mlstm_b_i"], p["mlstm_b_f"]]).reshape(1, -1))
    w2a2 = jnp.zeros((2 * B_LORA, 2 * B_W), F32)
    w2a2 = w2a2.at[0:B_LORA, 0:B_W].set(p["rwkv_w2"]).at[B_LORA:, B_W:].set(p["rwkv_a2"])
    row = lambda v: v.reshape(1, -1).astype(F32)
    return dict(
        w0=w0.astype(BF16), w1=w1.astype(BF16), wb=w_in0[:, a_cols:].astype(BF16),
        wo0=p["w_out0"].astype(BF16), wo1=p["w_out1"].astype(BF16), w2a2=w2a2.astype(BF16),
        norm0=row(p["norm0_w"]), norm1=row(p["norm1_w"]), normf=row(p["norm_f_w"]),
        conv_w=p["gdn_conv_w"].astype(F32), alog=alog, dtb=dtb, gnw=row(p["gdn_norm_w"]),
        mu=row(p["rwkv_mu"]), rw0=row(p["rwkv_w0"]), ra0=row(p["rwkv_a0"]),
        k_k=row(p["rwkv_k_k"]), k_a=row(p["rwkv_k_a"]), r_k=row(p["rwkv_r_k"]),
        ln_w=row(p["rwkv_ln_w"]), ln_b=row(p["rwkv_ln_b"]), gate_bias=gate_bias,
    )


def _tiles(bsz, seq):
    chunk = math.gcd(seq, MAX_CHUNK)
    tb = min(seq, 256)
    nb = 2 if seq > MAX_CHUNK else 8
    nb = math.gcd(nb, bsz)
    tt = min(seq, PROJ_ROWS)
    ns = math.gcd(min(PROJ_ROWS // tt, PROJ_MAX_SEQS), bsz)
    return chunk, tb, nb, tt, ns


def _trunk(x, state, q):
    conv0, s_gdn, x_prev, s_rwkv, c0, n0, m0 = state
    bsz, seq, _ = x.shape
    chunk, tb, nb, tt, ns = _tiles(bsz, seq)
    xn_last, prev_row = _aux(x[:, -1, :], x_prev, q["norm0"], q["wb"])
    fa, fb, conv_new = _proj0(x, conv0, prev_row.reshape(bsz, 1, B_COLS), q, ns, tt)
    mix, s_gdn_new, s_rwkv_new = _layer0_mixers(fa, fb, s_gdn, s_rwkv, q, nb, tb, chunk)
    y, c_new, n_new, m_new = _layer1(x, mix, c0, n0.reshape(bsz, C_HEADS // 2, LANES),
                                     m0.reshape(bsz, 1, C_HEADS), q, nb, tb, chunk)
    return y, (conv_new, s_gdn_new, xn_last, s_rwkv_new, c_new,
               n_new.reshape(bsz, C_HEADS, C_DQK), m_new[:, :, 0])


def kernel(x_prompt, x_sample, state_gdn_conv, state_gdn, state_rwkv_shift, state_rwkv,
           state_mlstm_c, state_mlstm_n, state_mlstm_m,
           norm0_w, w_in0, w_out0, gdn_conv_w, gdn_a_log, gdn_dt_bias, gdn_norm_w,
           rwkv_mu, rwkv_w0, rwkv_w2, rwkv_a0, rwkv_a2, rwkv_k_k, rwkv_k_a, rwkv_r_k,
           rwkv_ln_w, rwkv_ln_b, norm1_w, w_in1, w_out1, mlstm_b_i, mlstm_b_f, norm_f_w):
    q = _prepare_params(dict(
        norm0_w=norm0_w, w_in0=w_in0, w_out0=w_out0, gdn_conv_w=gdn_conv_w, gdn_a_log=gdn_a_log,
        gdn_dt_bias=gdn_dt_bias, gdn_norm_w=gdn_norm_w, rwkv_mu=rwkv_mu, rwkv_w0=rwkv_w0,
        rwkv_w2=rwkv_w2, rwkv_a0=rwkv_a0, rwkv_a2=rwkv_a2, rwkv_k_k=rwkv_k_k, rwkv_k_a=rwkv_k_a,
        rwkv_r_k=rwkv_r_k, rwkv_ln_w=rwkv_ln_w, rwkv_ln_b=rwkv_ln_b, norm1_w=norm1_w, w_in1=w_in1,
        w_out1=w_out1, mlstm_b_i=mlstm_b_i, mlstm_b_f=mlstm_b_f, norm_f_w=norm_f_w))
    bsz = x_prompt.shape[0]
    prompt_state = (
        jnp.zeros((bsz, A_CONV - 1, A_QKV), F32),
        jnp.zeros((bsz, A_HEADS, A_DK, A_DV), F32),
        jnp.zeros((bsz, D_MODEL), F32),
        jnp.zeros((bsz, B_HEADS, B_N, B_N), F32),
        jnp.zeros((bsz, C_HEADS, C_DQK, C_DV), F32),
        jnp.zeros((bsz, C_HEADS, C_DQK), F32),
        jnp.zeros((bsz, C_HEADS), F32),
    )
    y_p, sp = _trunk(x_prompt, prompt_state, q)
    sample_state = (state_gdn_conv, state_gdn, state_rwkv_shift, state_rwkv,
                    state_mlstm_c, state_mlstm_n, state_mlstm_m)
    y_s, ss = _trunk(x_sample, sample_state, q)
    return (y_p, y_s) + tuple(sp) + tuple(ss)
```

```python
import functools
import math

import jax
import jax.numpy as jnp
from jax import lax
from jax.experimental import pallas as pl
from jax.experimental.pallas import tpu as pltpu

F32 = jnp.float32
BF16 = jnp.bfloat16

D_MODEL = 1024
EPS = 1e-6
LANES = 128
SUBLANES = 8
A_HEADS, A_DK, A_DV, A_CONV = 4, 128, 128, 4
A_KEY = A_HEADS * A_DK
A_VAL = A_HEADS * A_DV
A_QKV = 2 * A_KEY + A_VAL
A_MAIN = A_QKV + A_VAL
B_HEADS, B_N, B_LORA = 8, 64, 64
B_W = B_HEADS * B_N
B_COLS = 4 * B_W + 2 * B_LORA
B_PAIRS = B_HEADS // 2
B_GN_EPS = 64e-5
B_DECAY_SCALE = math.exp(-0.5)
MIX0 = A_VAL + B_W
C_HEADS, C_DQK, C_DV = 8, 64, 128
C_QK = C_HEADS * C_DQK
C_V = C_HEADS * C_DV
C_GATE_CAP = 15.0
P0_GATES = A_MAIN
P0_B = A_MAIN + LANES
P0_PAD = P0_B + B_COLS
P1_GATES = 2 * C_QK + 3 * C_V
P1_PAD = P1_GATES + LANES
MAX_CHUNK = 64
GATE_COPIES = 3
CUMSUM_TERMS = 3
SUM_TERMS = 2
VMEM_LIMIT = 56 * 1024 * 1024


def _dot(a, b):
    return jnp.dot(a.astype(BF16), b.astype(BF16), preferred_element_type=F32)


def _dot_nt(a, b):
    return lax.dot_general(a.astype(BF16), b.astype(BF16), (((1,), (1,)), ((), ())),
                           preferred_element_type=F32)


def _dot_tn(a, b):
    return lax.dot_general(a.astype(BF16), b.astype(BF16), (((0,), (0,)), ((), ())),
                           preferred_element_type=F32)


def _split(x, terms):
    out = []
    for _ in range(terms - 1):
        hi = x.astype(BF16)
        out.append(hi)
        x = x - hi.astype(F32)
    out.append(x.astype(BF16))
    return out


def _stack_selector(sel, terms, axis):
    s = jnp.where(sel, 1.0, 0.0).astype(BF16)
    return jnp.concatenate([s] * terms, axis=axis)


def _dot_exact_lhs(sel_stacked, x, terms=CUMSUM_TERMS):
    return jnp.dot(sel_stacked, jnp.concatenate(_split(x, terms), axis=0), preferred_element_type=F32)


def _dot_exact_rhs(x, sel_stacked, terms=SUM_TERMS):
    return jnp.dot(jnp.concatenate(_split(x, terms), axis=1), sel_stacked, preferred_element_type=F32)


def _iota2(shape, dim):
    return lax.broadcasted_iota(jnp.int32, shape, dim)


def _tri_incl(n):
    return (_iota2((n, n), 0) >= _iota2((n, n), 1))


def _eye(n):
    return jnp.where(_iota2((n, n), 0) == _iota2((n, n), 1), 1.0, 0.0)


def _col_to_row(col):
    n = col.shape[0]
    eye = _iota2((n, n), 0) == _iota2((n, n), 1)
    return jnp.sum(jnp.where(eye, col, 0.0), axis=0, keepdims=True)


def _row_to_col(row):
    n = row.shape[1]
    eye = _iota2((n, n), 0) == _iota2((n, n), 1)
    return jnp.sum(jnp.where(eye, row, 0.0), axis=1, keepdims=True)


def _replicate(x, sel, group):
    hi, mid, lo = _split(x, GATE_COPIES)
    lane = _iota2(x.shape, 1)
    packed = jnp.where(lane < group, hi, jnp.where(lane < 2 * group, mid, lo))
    packed = jnp.where(lane < GATE_COPIES * group, packed, jnp.zeros_like(packed))
    return jnp.dot(packed, sel, preferred_element_type=F32)


def _replicate_selector(group):
    assert group & (group - 1) == 0 and LANES == 128
    k = _iota2((LANES, group * LANES), 0)
    col = _iota2((LANES, group * LANES), 1)
    hit = (k < GATE_COPIES * group) & (jnp.bitwise_and(k, group - 1) == jnp.right_shift(col, 7))
    return jnp.where(hit, 1.0, 0.0).astype(BF16)


def _cummax_rows(x):
    n = x.shape[0]
    row = _iota2(x.shape, 0)
    sh = 1
    while sh < n:
        x = jnp.maximum(x, jnp.where(row >= sh, pltpu.roll(x, sh, 0), -jnp.inf))
        sh *= 2
    return x


def _inv_unit_lower(p, levels, eye):
    n = p.shape[0]
    t = eye + p
    m = 2
    if m < levels:
        p = _dot(p, p)
        yield
    while m < levels:
        m *= 2
        if m >= levels:
            t = t + _dot(p, t)
        elif n % LANES == 0:
            both = _dot(p, jnp.concatenate([p, t], axis=1))
            p, t = both[:, 0:n], t + both[:, n:2 * n]
            yield
        else:
            both = _dot(jnp.concatenate([p, t], axis=0), p)
            p, t = both[0:n, :], t + both[n:2 * n, :]
            yield
    return t


def _run_tasks(tasks, caps):
    done, live, pending = set(), [], list(tasks)
    while pending or live:
        count = {}
        for _, group, _ in live:
            count[group] = count.get(group, 0) + 1
        still = []
        for key, group, deps, make in pending:
            if all(d in done for d in deps) and count.get(group, 0) < caps.get(group, len(tasks)):
                live.append((key, group, make()))
                count[group] = count.get(group, 0) + 1
            else:
                still.append((key, group, deps, make))
        assert live, "task graph has a cycle or a missing dependency"
        pending = still
        nxt = []
        for key, group, gen in live:
            try:
                next(gen)
                nxt.append((key, group, gen))
            except StopIteration:
                done.add(key)
        live = nxt


def _sigmoid(x):
    return 1.0 / (1.0 + jnp.exp(-x))


def _silu(x):
    return x * _sigmoid(x)


def _softplus(x):
    return jnp.maximum(x, 0.0) + jnp.log1p(jnp.exp(-jnp.abs(x)))


def _rms(x, w):
    return x * lax.rsqrt(jnp.mean(x * x, axis=-1, keepdims=True) + EPS) * w


def _resident(shape):
    zeros = (0,) * len(shape)
    return pl.BlockSpec(shape, lambda *_: zeros, pipeline_mode=pl.Buffered(1))


FA_Z = A_QKV
FA_BETA = A_MAIN
FA_DECAY = A_MAIN + LANES
FA_COLS = A_MAIN + 2 * LANES
_FR, _FLW, _FK, _FV, _FA, _FB, _FG = range(7)
FB_COLS = 7 * B_W
SUB_ROWS = 64
PROJ_ROWS = 512
PROJ_MAX_SEQS = 32


def _proj0_kernel(x_ref, conv0_ref, prev_ref, nw_ref, w_ref, convw_ref, alog_ref, dtb_ref,
                  mu_ref, w0_ref, a0_ref, w2a2_ref, kk_ref, ka_ref,
                  fa_ref, fb_ref, convn_ref, ext_a, ext_b, *, ns, tt):
    i = pl.program_id(1)
    tail = SUBLANES
    rows = ns * tt
    lane = _iota2((1, LANES), 1)
    head0 = lane < B_N
    ones_blk = (_iota2((LANES, LANES), 0) // B_N == _iota2((LANES, LANES), 1) // B_N)
    ones_sel = _stack_selector(ones_blk, SUM_TERMS, 0)

    @pl.when(i == 0)
    def _():
        ext_a[:, 0:tail, :] = jnp.zeros((ns, tail, A_QKV), F32)
        ext_a[:, tail - (A_CONV - 1):tail, :] = conv0_ref[...]
        ext_b[:, 0:tail, :] = jnp.zeros((ns, tail, B_COLS), F32)
        ext_b[:, tail - 1:tail, :] = prev_ref[...]

    @pl.when(i > 0)
    def _():
        ext_a[:, 0:tail, :] = ext_a[:, tt:tt + tail, :]
        ext_b[:, 0:tail, :] = ext_b[:, tt:tt + tail, :]

    xn = _rms(x_ref[...].reshape(rows, D_MODEL), nw_ref[...]).astype(BF16)
    piece = 4 * LANES

    def cols(lo, hi):
        return jnp.dot(xn, w_ref[:, lo:hi], preferred_element_type=F32).reshape(ns, tt, hi - lo)

    for lo in range(0, A_QKV, piece):
        ext_a[:, tail:tail + tt, lo:lo + piece] = cols(lo, lo + piece)
    fa_ref[:, :, FA_Z:FA_Z + A_VAL] = cols(A_QKV, A_MAIN)
    g = cols(P0_GATES, P0_B)
    fa_ref[:, :, FA_BETA:FA_BETA + LANES] = _sigmoid(g)
    fa_ref[:, :, FA_DECAY:FA_DECAY + LANES] = -jnp.exp(alog_ref[...]) * _softplus(g + dtb_ref[...])
    for lo in range(0, B_COLS, piece):
        hi = min(lo + piece, B_COLS)
        ext_b[:, tail:tail + tt, lo:hi] = cols(P0_B + lo, P0_B + hi)

    sub_t = min(tt, SUB_ROWS)
    sub_s = min(ns, SUB_ROWS // sub_t)
    cw = convw_ref[...]
    mu = mu_ref[...]
    for s0 in range(0, ns, sub_s):
        sq = slice(s0, s0 + sub_s)
        for t0 in range(0, tt, sub_t):
            tr = slice(t0, t0 + sub_t)
            y = ext_a[sq, tail + t0:tail + t0 + sub_t, :] * cw[A_CONV - 1:A_CONV, :]
            for j in range(A_CONV - 1):
                sh = A_CONV - 1 - j
                y = y + ext_a[sq, tail + t0 - sh:tail + t0 - sh + sub_t, :] * cw[j:j + 1, :]
            y = _silu(y)
            for h in range(A_HEADS):
                q = y[:, :, h * A_DK:(h + 1) * A_DK]
                fa_ref[sq, tr, h * A_DK:(h + 1) * A_DK] = (
                    q * lax.rsqrt(jnp.sum(q * q, axis=-1, keepdims=True) + EPS) * (A_DK ** -0.5))
                k = y[:, :, A_KEY + h * A_DK:A_KEY + (h + 1) * A_DK]
                fa_ref[sq, tr, A_KEY + h * A_DK:A_KEY + (h + 1) * A_DK] = (
                    k * lax.rsqrt(jnp.sum(k * k, axis=-1, keepdims=True) + EPS))
            fa_ref[sq, tr, 2 * A_KEY:A_QKV] = y[:, :, 2 * A_KEY:A_QKV]
            pb = ext_b[sq, tail + t0:tail + t0 + sub_t, :]
            pb_prev = ext_b[sq, tail + t0 - 1:tail + t0 - 1 + sub_t, :]
            pbs = (pb + (pb_prev - pb) * mu).reshape(sub_s * sub_t, B_COLS)
            lo_ = pbs[:, 4 * B_W:B_COLS]
            lo_ = jnp.where(head0, jnp.tanh(lo_), lo_)
            wa = _dot(lo_, w2a2_ref[...])
            logw = -B_DECAY_SCALE * _sigmoid(w0_ref[...] + wa[:, 0:B_W])
            a = _sigmoid(a0_ref[...] + wa[:, B_W:2 * B_W])
            kb = pbs[:, B_W:2 * B_W]
            kkx = kb * kk_ref[...]

            def put(slot, val):
                fb_ref[sq, tr, slot * B_W:(slot + 1) * B_W] = val.reshape(sub_s, sub_t, B_W)

            kkn = jnp.concatenate(
                [kkx[:, p * LANES:(p + 1) * LANES]
                 * lax.rsqrt(_dot_exact_rhs(kkx[:, p * LANES:(p + 1) * LANES] ** 2, ones_sel) + EPS)
                 for p in range(B_PAIRS)], axis=1)
            put(_FR, pbs[:, 0:B_W])
            put(_FLW, logw)
            put(_FK, kb * (1.0 + (a - 1.0) * ka_ref[...]))
            put(_FV, pbs[:, 2 * B_W:3 * B_W])
            put(_FA, -kkn)
            put(_FB, kkn * a)
            put(_FG, pbs[:, 3 * B_W:4 * B_W])

    @pl.when(i == pl.num_programs(1) - 1)
    def _():
        convn_ref[...] = ext_a[:, tt + tail - (A_CONV - 1):tt + tail, :]


def _proj0(x3d, conv0, prev_row, q, ns, tt):
    bsz, seq, _ = x3d.shape
    blk = lambda b, i: (b, i, 0)
    per_b3 = lambda b, i: (b, 0, 0)
    kern = functools.partial(_proj0_kernel, ns=ns, tt=tt)
    return pl.pallas_call(
        kern,
        grid=(bsz // ns, seq // tt),
        in_specs=[pl.BlockSpec((ns, tt, D_MODEL), blk),
                  pl.BlockSpec((ns, A_CONV - 1, A_QKV), per_b3),
                  pl.BlockSpec((ns, 1, B_COLS), per_b3),
                  _resident((1, D_MODEL)),
                  _resident((D_MODEL, P0_PAD)),
                  _resident((A_CONV, A_QKV)), _resident((1, LANES)), _resident((1, LANES)),
                  _resident((1, B_COLS)), _resident((1, B_W)), _resident((1, B_W)),
                  _resident((2 * B_LORA, 2 * B_W)), _resident((1, B_W)), _resident((1, B_W))],
        out_specs=[pl.BlockSpec((ns, tt, FA_COLS), blk),
                   pl.BlockSpec((ns, tt, FB_COLS), blk),
                   pl.BlockSpec((ns, A_CONV - 1, A_QKV), per_b3)],
        out_shape=[jax.ShapeDtypeStruct((bsz, seq, FA_COLS), F32),
                   jax.ShapeDtypeStruct((bsz, seq, FB_COLS), F32),
                   jax.ShapeDtypeStruct((bsz, A_CONV - 1, A_QKV), F32)],
        scratch_shapes=[pltpu.VMEM((ns, tt + SUBLANES, A_QKV), F32),
                        pltpu.VMEM((ns, tt + SUBLANES, B_COLS), F32)],
        compiler_params=pltpu.CompilerParams(dimension_semantics=("parallel", "arbitrary"),
                                             vmem_limit_bytes=VMEM_LIMIT),
        name="proj0_features",
    )(x3d, conv0, prev_row, q["norm0"], q["w0"], q["conv_w"], q["alog"], q["dtb"],
      q["mu"], q["rw0"], q["ra0"], q["w2a2"], q["k_k"], q["k_a"])


def _aux_kernel(xl_ref, xp_ref, nw_ref, wb_ref, xn_ref, prev_ref):
    xn_ref[...] = _rms(xl_ref[...], nw_ref[...])
    prev_ref[...] = jnp.dot(xp_ref[...].astype(BF16), wb_ref[...], preferred_element_type=F32)


def _aux(x_last, x_prev, norm_w, wb):
    b = x_last.shape[0]
    return pl.pallas_call(
        _aux_kernel,
        out_shape=[jax.ShapeDtypeStruct((b, D_MODEL), F32), jax.ShapeDtypeStruct((b, B_COLS), F32)],
        compiler_params=pltpu.CompilerParams(vmem_limit_bytes=VMEM_LIMIT),
        name="shift_rows",
    )(x_last, x_prev, norm_w, wb)


def _layer0_kernel(fa_ref, fb_ref, sg0_ref, sr0_ref, gnw_ref, rk_ref, lnw_ref, lnb_ref,
                   o_ref, sgout_ref, srout_ref,
                   stg_s, uw_s, attn_s, qg_s, kd_s, ge_s, str_s, tmp_s, *, nb, tb, chunk):
    i = pl.program_id(1)
    n_chunks = tb // chunk
    lane = _iota2((1, LANES), 1)
    head0 = lane < B_N
    ones_blk = (_iota2((LANES, LANES), 0) // B_N == _iota2((LANES, LANES), 1) // B_N)

    @pl.when(i == 0)
    def _():
        for bi in range(nb):
            for p in range(B_PAIRS):
                tmp_s[...] = jnp.zeros((B_N, LANES), F32)
                tmp_s[:, 0:B_N] = sr0_ref[bi, 2 * p + 1]
                shifted = pltpu.roll(tmp_s[...], B_N, 1)
                str_s[bi, p, B_N:LANES, :] = jnp.where(head0, 0.0, shifted)
                tmp_s[:, 0:B_N] = sr0_ref[bi, 2 * p]
                str_s[bi, p, 0:B_N, :] = tmp_s[...]
        stg_s[...] = sg0_ref[...]

    tri = _tri_incl(chunk)
    tri_sel = _stack_selector(tri, CUMSUM_TERMS, 1)
    ones_sel = _stack_selector(ones_blk, SUM_TERMS, 0)
    eye_c = _eye(chunk)
    eye_2c = _eye(2 * chunk)

    n_gates = 2 * A_HEADS
    is_decay = jnp.bitwise_and(lane, n_gates - 1) >= A_HEADS
    sel = _replicate_selector(n_gates)
    strict = _iota2((chunk, chunk), 0) > _iota2((chunk, chunk), 1)
    gnw = gnw_ref[...]
    qkv_s = fa_ref
    rep = {}

    def gdn_gates(bi, c):
        rows = slice(c * chunk, (c + 1) * chunk)
        gcum = _dot_exact_lhs(tri_sel, fa_ref[bi, rows, FA_DECAY:FA_DECAY + LANES])
        yield
        rep[bi, c] = _replicate(jnp.where(is_decay, gcum, fa_ref[bi, rows, FA_BETA:FA_BETA + LANES]),
                                sel, n_gates)

    wpk = A_HEADS * chunk
    head_pk = _iota2((1, wpk), 1) // chunk
    s_pk = _iota2((1, wpk), 1) - head_pk * chunk
    t_pk = _iota2((chunk, wpk), 0)
    diag_pk = t_pk == s_pk
    tri_pk = t_pk >= s_pk
    strict_pk = t_pk > s_pk
    head_of_klane = _iota2((1, A_KEY), 1) // A_DK
    head_of_uwlane = _iota2((1, A_HEADS * (A_DV + A_DK)), 1) // (A_DV + A_DK)

    def widen(x):
        reps = -(-wpk // LANES)
        w = jnp.concatenate([x] * reps, axis=1) if reps > 1 else x
        return w if w.shape[1] == wpk else w[:, 0:wpk]

    def pack_heads(xs):
        out = xs[A_HEADS - 1]
        for h in range(A_HEADS - 2, -1, -1):
            out = jnp.where(head_pk == h, xs[h], out)
        return out

    def block_diag(x):
        return jnp.concatenate([jnp.where(head_pk == h, x, 0.0) for h in range(A_HEADS)], axis=0)

    def gdn_local_head(bi, c, h):
        rows = slice(c * chunk, (c + 1) * chunk)
        q = qkv_s[bi, rows, h * A_DK:(h + 1) * A_DK]
        k = qkv_s[bi, rows, A_KEY + h * A_DK:A_KEY + (h + 1) * A_DK]
        v = qkv_s[bi, rows, 2 * A_KEY + h * A_DV:2 * A_KEY + (h + 1) * A_DV]
        qk_k = _dot_nt(jnp.concatenate([q, k], axis=0), k)
        yield
        b_rep = rep[bi, c][:, h * LANES:(h + 1) * LANES]
        g_rep = rep[bi, c][:, (A_HEADS + h) * LANES:(A_HEADS + h + 1) * LANES]
        g_row = _col_to_row(g_rep[:, 0:chunk])
        g_last = g_rep[chunk - 1:chunk, :]
        decay = jnp.where(tri, jnp.exp(jnp.where(tri, g_rep[:, 0:chunk] - g_row, 0.0)), 0.0)
        eg = jnp.exp(g_rep)
        qg_s[bi, rows, h * A_DK:(h + 1) * A_DK] = q * eg
        kd_s[bi, rows, h * A_DK:(h + 1) * A_DK] = k * jnp.exp(g_last - g_rep)
        ge_s[bi, c, h:h + 1, :] = jnp.exp(g_last)
        attn_s[bi, rows, h * LANES:h * LANES + chunk] = decay * qk_k[0:chunk, :]
        amat = jnp.where(strict, b_rep[:, 0:chunk] * decay * qk_k[chunk:2 * chunk, :], 0.0)
        t_inv = yield from _inv_unit_lower(-amat, chunk, eye_c)
        uw_s[bi, rows, 2 * h * LANES:2 * (h + 1) * LANES] = _dot(
            t_inv, jnp.concatenate([b_rep * v, (b_rep * eg) * k], axis=1))

    def gdn_local(bi, c):
        rows = slice(c * chunk, (c + 1) * chunk)
        heads = range(A_HEADS)
        q = [qkv_s[bi, rows, h * A_DK:(h + 1) * A_DK] for h in heads]
        k = [qkv_s[bi, rows, A_KEY + h * A_DK:A_KEY + (h + 1) * A_DK] for h in heads]
        v = [qkv_s[bi, rows, 2 * A_KEY + h * A_DV:2 * A_KEY + (h + 1) * A_DV] for h in heads]
        qk = [_dot_nt(q[h], k[h]) for h in heads]
        k_bd = jnp.concatenate([jnp.where(head_of_klane == h, jnp.concatenate([k[h]] * A_HEADS, axis=1), 0.0)
                                for h in heads], axis=0)
        kk_pk = _dot_nt(jnp.concatenate(k, axis=1), k_bd)
        yield
        b_rep = [rep[bi, c][:, h * LANES:(h + 1) * LANES] for h in heads]
        g_rep = [rep[bi, c][:, (A_HEADS + h) * LANES:(A_HEADS + h + 1) * LANES] for h in heads]
        eg = [jnp.exp(g) for g in g_rep]
        for h in heads:
            g_row = _col_to_row(g_rep[h][:, 0:chunk])
            g_last = g_rep[h][chunk - 1:chunk, :]
            decay = jnp.where(tri, jnp.exp(jnp.where(tri, g_rep[h][:, 0:chunk] - g_row, 0.0)), 0.0)
            qg_s[bi, rows, h * A_DK:(h + 1) * A_DK] = q[h] * eg[h]
            kd_s[bi, rows, h * A_DK:(h + 1) * A_DK] = k[h] * jnp.exp(g_last - g_rep[h])
            ge_s[bi, c, h:h + 1, :] = jnp.exp(g_last)
            attn_s[bi, rows, h * LANES:h * LANES + chunk] = decay * qk[h]
        g_col_pk = pack_heads([widen(g) for g in g_rep])
        b_pk = pack_heads([widen(b) for b in b_rep])
        g_row_pk = jnp.sum(jnp.where(diag_pk, g_col_pk, 0.0), axis=0, keepdims=True)
        decay_pk = jnp.where(tri_pk, jnp.exp(jnp.where(tri_pk, g_col_pk - g_row_pk, 0.0)), 0.0)
        p = -jnp.where(strict_pk, b_pk * decay_pk * kk_pk, 0.0)
        t = jnp.where(diag_pk, 1.0, 0.0) + p
        m = 2
        if m < chunk:
            p = _dot(p, block_diag(p))
            yield
        while m < chunk:
            m *= 2
            if m >= chunk:
                t = t + _dot(t, block_diag(p))
            else:
                both = _dot(jnp.concatenate([p, t], axis=0), block_diag(p))
                p, t = both[0:chunk, :], t + both[chunk:2 * chunk, :]
                yield
        rhs = jnp.concatenate(
            [jnp.where(head_of_uwlane == h,
                       jnp.concatenate([b_rep[h] * v[h], (b_rep[h] * eg[h]) * k[h]] * A_HEADS, axis=1), 0.0)
             for h in heads], axis=0)
        uw_s[bi, rows, :] = _dot(t, rhs)

    def gdn_state(bi, c, h):
        rows = slice(c * chunk, (c + 1) * chunk)
        uw = uw_s[bi, rows, 2 * h * LANES:2 * (h + 1) * LANES]
        s = stg_s[bi, h]
        wq_s = _dot(jnp.concatenate([uw[:, A_DV:A_DV + A_DK], qg_s[bi, rows, h * A_DK:(h + 1) * A_DK]], axis=0), s)
        yield
        nv = uw[:, 0:A_DV] - wq_s[0:chunk, :]
        o = wq_s[chunk:2 * chunk, :] + _dot(attn_s[bi, rows, h * LANES:h * LANES + chunk], nv)
        stg_s[bi, h] = ge_s[bi, c, h:h + 1, :] * s + _dot_tn(kd_s[bi, rows, h * A_DK:(h + 1) * A_DK], nv)
        yield
        z = fa_ref[bi, rows, FA_Z + h * A_DV:FA_Z + (h + 1) * A_DV]
        o_ref[bi, rows, h * A_DV:(h + 1) * A_DV] = _rms(o, gnw) * _silu(z)

    feat_s = fb_ref
    c2 = 2 * chunk
    ri = _iota2((c2, c2), 0)
    ci = _iota2((c2, c2), 1)
    same_head = (ri // chunk) == (ci // chunk)
    incl_bd = same_head & (ri >= ci)
    strict_bd = same_head & (ri > ci)
    ri4 = _iota2((2 * c2, c2), 0)
    ci4 = _iota2((2 * c2, c2), 1)
    rr4 = ri4 % c2
    lm_mask = ((rr4 // chunk) == (ci4 // chunk)) & ((rr4 > ci4) | ((ri4 >= c2) & (rr4 == ci4)))
    lane_st = (_iota2((c2, LANES), 0) // chunk) == (_iota2((c2, LANES), 1) // B_N)
    local = {}

    def rwkv_local(bi, c, p):
        rows = slice(c * chunk, (c + 1) * chunk)

        def feat(slot):
            return feat_s[bi, rows, slot * B_W + p * LANES:slot * B_W + (p + 1) * LANES]
        r, lw, k, v, a_, b_ = (feat(s) for s in (_FR, _FLW, _FK, _FV, _FA, _FB))
        cs = _dot_exact_lhs(tri_sel, lw)
        yield
        cs_last = cs[chunk - 1:chunk, :]
        e_inv = jnp.exp(-cs)
        e_rem = jnp.exp(cs_last - cs)
        at = a_ * jnp.exp(cs - lw)
        rt = r * jnp.exp(cs)
        bt = b_ * e_inv
        kt = k * e_inv
        a_st = jnp.concatenate([jnp.where(head0, at, 0.0), jnp.where(head0, 0.0, at)], axis=0)
        r_st = jnp.concatenate([jnp.where(head0, rt, 0.0), jnp.where(head0, 0.0, rt)], axis=0)
        b2 = jnp.concatenate([bt, bt], axis=0)
        k2 = jnp.concatenate([kt, kt], axis=0)
        v2 = jnp.concatenate([v, v], axis=0)
        ar_st = jnp.concatenate([a_st, r_st], axis=0)
        if c2 % LANES == 0:
            ar_bk = _dot_nt(ar_st, jnp.concatenate([b2, k2], axis=0))
            ar_b, ar_k = ar_bk[:, 0:c2], ar_bk[:, c2:2 * c2]
        else:
            ar_b = _dot_nt(ar_st, b2)
            ar_k = _dot_nt(ar_st, k2)
        yield
        l_ab = jnp.where(strict_bd, ar_b[0:c2, :], 0.0)
        m_rb = jnp.where(incl_bd, ar_b[c2:2 * c2, :], 0.0)
        lm_k = jnp.where(lm_mask, ar_k, 0.0)
        lm_v = _dot(lm_k, v2)
        sl = slice(p * LANES, (p + 1) * LANES)
        bonus = _dot_exact_rhs(r * k * rk_ref[:, sl], ones_sel) * v
        t_inv = yield from _inv_unit_lower(l_ab, chunk, eye_2c)
        local[bi, c, p] = (t_inv, lm_v, m_rb, jnp.concatenate([at, rt], axis=0),
                           jnp.concatenate([b_ * e_rem, k * e_rem], axis=0), jnp.exp(cs_last), bonus)

    def rwkv_state(bi, c, p):
        rows = slice(c * chunk, (c + 1) * chunk)
        sl = slice(p * LANES, (p + 1) * LANES)
        t_inv, lm_v, m_rb, at_rt, bk_rem, decay_end, bonus = local.pop((bi, c, p))
        v = feat_s[bi, rows, _FV * B_W + p * LANES:_FV * B_W + (p + 1) * LANES]
        gate = feat_s[bi, rows, _FG * B_W + p * LANES:_FG * B_W + (p + 1) * LANES]
        s = str_s[bi, p]
        ar = _dot_nt(at_rt, s)
        yield
        ah = ar[0:chunk, :]
        rh = ar[chunk:c2, :]
        x_st = jnp.where(lane_st, jnp.concatenate([ah, ah], axis=0) + lm_v[0:c2, :], 0.0)
        u_st = _dot(t_inv, x_st)
        yield
        o_st = jnp.where(lane_st, _dot(m_rb, u_st) + lm_v[c2:2 * c2, :], 0.0)
        u = u_st[0:chunk, :] + u_st[chunk:c2, :]
        upd = _dot_tn(jnp.concatenate([u, v], axis=0), bk_rem)
        yield
        o = rh + o_st[0:chunk, :] + o_st[chunk:c2, :]
        str_s[bi, p] = s * decay_end + jnp.where(ones_blk, upd, 0.0)
        mean = _dot_exact_rhs(o, ones_sel) * (1.0 / B_N)
        yield
        d = o - mean
        var = _dot_exact_rhs(d * d, ones_sel) * (1.0 / B_N)
        yield
        ob = d * lax.rsqrt(var + B_GN_EPS) * lnw_ref[:, sl] + lnb_ref[:, sl] + bonus
        o_ref[bi, rows, A_VAL + p * LANES:A_VAL + (p + 1) * LANES] = ob * _silu(gate)

    packed = wpk >= LANES
    tasks = []
    blocks = [(bi, c) for c in range(n_chunks) for bi in range(nb)]
    for bi, c in blocks:
        for p in range(B_PAIRS):
            deps = [("rl", bi, c, p)] + ([("rs", bi, c - 1, p)] if c else [])
            tasks.append((("rs", bi, c, p), "rwkv_state", deps, functools.partial(rwkv_state, bi, c, p)))
        for h in range(A_HEADS):
            local_key = ("gl", bi, c) if packed else ("gl", bi, c, h)
            deps = [local_key] + ([("gs", bi, c - 1, h)] if c else [])
            tasks.append((("gs", bi, c, h), "gdn_state", deps, functools.partial(gdn_state, bi, c, h)))
    for bi, c in blocks:
        for p in range(B_PAIRS):
            tasks.append((("rl", bi, c, p), "rwkv_local", [], functools.partial(rwkv_local, bi, c, p)))
        if packed:
            tasks.append((("gl", bi, c), "gdn_local", [("gg", bi, c)], functools.partial(gdn_local, bi, c)))
        else:
            for h in range(A_HEADS):
                tasks.append((("gl", bi, c, h), "gdn_local", [("gg", bi, c)],
                              functools.partial(gdn_local_head, bi, c, h)))
    for bi, c in blocks:
        tasks.append((("gg", bi, c), "gdn_gates", [], functools.partial(gdn_gates, bi, c)))
    _run_tasks(tasks, {"gdn_gates": 2, "rwkv_local": 2 * B_PAIRS, "gdn_local": 4 if packed else 2 * A_HEADS})

    @pl.when(i == pl.num_programs(1) - 1)
    def _():
        for bi in range(nb):
            for p in range(B_PAIRS):
                srout_ref[bi, 2 * p] = str_s[bi, p, 0:B_N, 0:B_N]
                shifted = pltpu.roll(str_s[bi, p, B_N:LANES, :], B_N, 1)
                srout_ref[bi, 2 * p + 1] = shifted[:, 0:B_N]
        sgout_ref[...] = stg_s[...]


def _layer0_mixers(fa, fb, s_gdn, s_rwkv, q, nb, tb, chunk):
    bsz, seq, _ = fa.shape
    blk = lambda b, i: (b, i, 0)
    per_b4 = lambda b, i: (b, 0, 0, 0)
    fix2 = lambda b, i: (0, 0)
    kern = functools.partial(_layer0_kernel, nb=nb, tb=tb, chunk=chunk)
    vec = pl.BlockSpec((1, B_W), fix2)
    return pl.pallas_call(
        kern,
        grid=(bsz // nb, seq // tb),
        in_specs=[pl.BlockSpec((nb, tb, FA_COLS), blk),
                  pl.BlockSpec((nb, tb, FB_COLS), blk),
                  pl.BlockSpec((nb, A_HEADS, A_DK, A_DV), per_b4),
                  pl.BlockSpec((nb, B_HEADS, B_N, B_N), per_b4),
                  pl.BlockSpec((1, A_DV), fix2), vec, vec, vec],
        out_specs=[pl.BlockSpec((nb, tb, MIX0), blk),
                   pl.BlockSpec((nb, A_HEADS, A_DK, A_DV), per_b4),
                   pl.BlockSpec((nb, B_HEADS, B_N, B_N), per_b4)],
        out_shape=[jax.ShapeDtypeStruct((bsz, seq, MIX0), F32),
                   jax.ShapeDtypeStruct((bsz, A_HEADS, A_DK, A_DV), F32),
                   jax.ShapeDtypeStruct((bsz, B_HEADS, B_N, B_N), F32)],
        scratch_shapes=[pltpu.VMEM((nb, A_HEADS, A_DK, A_DV), F32),
                        pltpu.VMEM((nb, tb, A_HEADS * (A_DV + A_DK)), F32),
                        pltpu.VMEM((nb, tb, A_HEADS * LANES), F32),
                        pltpu.VMEM((nb, tb, A_KEY), F32),
                        pltpu.VMEM((nb, tb, A_KEY), F32),
                        pltpu.VMEM((nb, tb // chunk, SUBLANES, LANES), F32),
                        pltpu.VMEM((nb, B_PAIRS, LANES, LANES), F32),
                        pltpu.VMEM((B_N, LANES), F32)],
        compiler_params=pltpu.CompilerParams(dimension_semantics=("parallel", "arbitrary"),
                                             vmem_limit_bytes=VMEM_LIMIT),
        name="layer0_mixers",
    )(fa, fb, s_gdn, s_rwkv, q["gnw"], q["r_k"], q["ln_w"], q["ln_b"])


def _layer1_kernel(x_ref, mix_ref, c0_ref, n0_ref, m0_ref, wo0_ref, n1_ref, w1_ref, bias_ref, wo1_ref, nf_ref,
                   y_ref, cout_ref, nout_ref, mout_ref,
                   proj_s, x1_s, h_s, gsc_s, caug_s, m_s, *, nb, tb, chunk):
    i = pl.program_id(1)
    n_chunks = tb // chunk
    lane = _iota2((1, LANES), 1)
    krow = _iota2((LANES, 1), 0)
    n_gates = 2 * C_HEADS
    is_f = jnp.bitwise_and(lane, n_gates - 1) >= C_HEADS

    @pl.when(i == 0)
    def _():
        for bi in range(nb):
            m0 = m0_ref[bi]
            for h in range(C_HEADS):
                j = h % 2
                caug_s[bi, h] = jnp.zeros((LANES, 2 * C_DV), F32)
                caug_s[bi, h, j * C_DQK:(j + 1) * C_DQK, 0:C_DV] = c0_ref[bi, h]
                n_col = _row_to_col(n0_ref[bi, h // 2:h // 2 + 1, :])
                own = (krow // C_DQK) == j
                caug_s[bi, h, :, C_DV:2 * C_DV] = jnp.where(own, jnp.broadcast_to(n_col, (LANES, C_DV)), 0.0)
                m_s[bi, h:h + 1, :] = jnp.broadcast_to(m0[:, h:h + 1], (1, LANES))

    tri = _tri_incl(chunk)
    tri_sel = _stack_selector(tri, CUMSUM_TERMS, 1)
    sel = _replicate_selector(n_gates)
    ones_blk = jnp.ones((chunk, C_DV), F32)
    off = 2 * C_QK
    piece = 4 * LANES
    rep = {}

    def stack(ref, rows):
        return jnp.concatenate([ref[bi, rows, :] for bi in range(nb)], axis=0)

    def unstack(ref, rows, cols, val):
        for bi in range(nb):
            ref[bi, rows, cols] = val[bi * chunk:(bi + 1) * chunk, :]

    def proj_task(c):
        rows = slice(c * chunk, (c + 1) * chunk)
        x1 = stack(x_ref, rows) + jnp.dot(stack(mix_ref, rows).astype(BF16), wo0_ref[...],
                                          preferred_element_type=F32)
        unstack(x1_s, rows, slice(0, D_MODEL), x1)
        yield
        xn = _rms(x1, n1_ref[...]).astype(BF16)
        g = jnp.dot(xn, w1_ref[:, P1_GATES:P1_PAD], preferred_element_type=F32) + bias_ref[...]
        cap = C_GATE_CAP * jnp.tanh(g / C_GATE_CAP)
        unstack(gsc_s, rows, slice(0, LANES), jnp.where(is_f, -_softplus(-cap), cap))
        yield
        for j in range(0, P1_GATES, piece):
            unstack(proj_s, rows, slice(j, j + piece),
                    jnp.dot(xn, w1_ref[:, j:j + piece], preferred_element_type=F32))
            yield

    def gate_task(bi, c):
        rows = slice(c * chunk, (c + 1) * chunk)
        gl = gsc_s[bi, rows, :]
        bcum = _dot_exact_lhs(tri_sel, gl)
        yield
        rep[bi, c] = _replicate(jnp.where(is_f, bcum, gl), sel, n_gates)

    def head_chain(bi, c, h):
        rows = slice(c * chunk, (c + 1) * chunk)
        p, j = h // 2, h % 2
        own_lane = (lane // C_DQK) == j
        own_row = (krow // C_DQK) == j
        qp = proj_s[bi, rows, p * LANES:(p + 1) * LANES] * (C_DQK ** -0.5)
        kp = proj_s[bi, rows, C_QK + p * LANES:C_QK + (p + 1) * LANES]
        v = proj_s[bi, rows, off + h * C_DV:off + (h + 1) * C_DV]
        vaug = jnp.concatenate([v, ones_blk], axis=1)
        qk = _dot_nt(jnp.where(own_lane, qp, 0.0), kp)
        cm = caug_s[bi, h]
        qc = _dot(qp, cm)
        i_rep = rep[bi, c][:, h * LANES:(h + 1) * LANES]
        b_rep = rep[bi, c][:, (C_HEADS + h) * LANES:(C_HEADS + h + 1) * LANES]
        b_row = _col_to_row(b_rep[:, 0:chunk])
        i_row = _col_to_row(i_rep[:, 0:chunk])
        dmat = jnp.where(tri, b_rep[:, 0:chunk] - b_row + i_row, -jnp.inf)
        dmax = b_rep + _cummax_rows(i_rep - b_rep)
        m = m_s[bi, h:h + 1, :]
        inter = b_rep + m
        mt = jnp.maximum(inter, dmax)
        m_new = mt[chunk - 1:chunk, :]
        b_last = b_rep[chunk - 1:chunk, :]
        w_s = jnp.exp(b_last - b_rep + i_rep - m_new)
        e_c = jnp.exp(b_last + m - m_new)
        e_inter = jnp.exp(inter - mt)
        own_k = slice(j * C_DQK, (j + 1) * C_DQK)
        upd = _dot((w_s * kp).T[own_k, :], vaug)
        yield
        pm = jnp.exp(dmat - mt[:, 0:chunk]) * qk
        pv = _dot(pm, vaug)
        caug_s[bi, h, own_k, :] = jnp.concatenate([e_c, e_c], axis=1) * cm[own_k, :] + upd
        m_s[bi, h:h + 1, :] = m_new
        yield
        num = e_inter * qc[:, 0:C_DV] + pv[:, 0:C_DV]
        den = e_inter * qc[:, C_DV:2 * C_DV] + pv[:, C_DV:2 * C_DV]
        hh = num / jnp.maximum(jnp.abs(den), jnp.exp(-mt))
        og = proj_s[bi, rows, off + C_V + h * C_DV:off + C_V + (h + 1) * C_DV]
        z = proj_s[bi, rows, off + 2 * C_V + h * C_DV:off + 2 * C_V + (h + 1) * C_DV]
        h_s[bi, rows, h * C_DV:(h + 1) * C_DV] = hh * _sigmoid(og) * _silu(z)

    def final_task(c):
        rows = slice(c * chunk, (c + 1) * chunk)
        x2 = stack(x1_s, rows) + jnp.dot(stack(h_s, rows).astype(BF16), wo1_ref[...],
                                         preferred_element_type=F32)
        yield
        unstack(y_ref, rows, slice(0, D_MODEL), _rms(x2, nf_ref[...]))

    tasks = []
    for c in range(n_chunks):
        tasks.append((("p", c), "proj", [], functools.partial(proj_task, c)))
    for c in range(n_chunks):
        heads = [("h", bi, c, h) for bi in range(nb) for h in range(C_HEADS)]
        tasks.append((("f", c), "final", heads, functools.partial(final_task, c)))
        for bi in range(nb):
            tasks.append((("g", bi, c), "gates", [("p", c)], functools.partial(gate_task, bi, c)))
            for h in range(C_HEADS):
                deps = [("p", c), ("g", bi, c)] + ([("h", bi, c - 1, h)] if c else [])
                tasks.append((("h", bi, c, h), "heads", deps, functools.partial(head_chain, bi, c, h)))
    _run_tasks(tasks, {"proj": 1, "final": 1})

    @pl.when(i == pl.num_programs(1) - 1)
    def _():
        for bi in range(nb):
            for h in range(C_HEADS):
                j = h % 2
                cout_ref[bi, h] = caug_s[bi, h, j * C_DQK:(j + 1) * C_DQK, 0:C_DV]
            for p in range(C_HEADS // 2):
                n_col = caug_s[bi, 2 * p, :, C_DV:C_DV + 1] + caug_s[bi, 2 * p + 1, :, C_DV:C_DV + 1]
                nout_ref[bi, p:p + 1, :] = _col_to_row(n_col)
        mout_ref[...] = m_s[...]


def _layer1(x3d, mix3d, c0, n0, m0, q, nb, tb, chunk):
    bsz, seq, _ = x3d.shape
    blk = lambda b, i: (b, i, 0)
    per_b3 = lambda b, i: (b, 0, 0)
    per_b4 = lambda b, i: (b, 0, 0, 0)
    kern = functools.partial(_layer1_kernel, nb=nb, tb=tb, chunk=chunk)
    pairs = C_HEADS // 2
    return pl.pallas_call(
        kern,
        grid=(bsz // nb, seq // tb),
        in_specs=[pl.BlockSpec((nb, tb, D_MODEL), blk),
                  pl.BlockSpec((nb, tb, MIX0), blk),
                  pl.BlockSpec((nb, C_HEADS, C_DQK, C_DV), per_b4),
                  pl.BlockSpec((nb, pairs, LANES), per_b3),
                  pl.BlockSpec((nb, 1, C_HEADS), per_b3),
                  _resident((MIX0, D_MODEL)),
                  _resident((1, D_MODEL)),
                  _resident((D_MODEL, P1_PAD)),
                  _resident((1, LANES)),
                  _resident((C_V, D_MODEL)),
                  _resident((1, D_MODEL))],
        out_specs=[pl.BlockSpec((nb, tb, D_MODEL), blk),
                   pl.BlockSpec((nb, C_HEADS, C_DQK, C_DV), per_b4),
                   pl.BlockSpec((nb, pairs, LANES), per_b3),
                   pl.BlockSpec((nb, C_HEADS, LANES), per_b3)],
        out_shape=[jax.ShapeDtypeStruct((bsz, seq, D_MODEL), F32),
                   jax.ShapeDtypeStruct((bsz, C_HEADS, C_DQK, C_DV), F32),
                   jax.ShapeDtypeStruct((bsz, pairs, LANES), F32),
                   jax.ShapeDtypeStruct((bsz, C_HEADS, LANES), F32)],
        scratch_shapes=[pltpu.VMEM((nb, tb, P1_PAD), F32),
                        pltpu.VMEM((nb, tb, D_MODEL), F32),
                        pltpu.VMEM((nb, tb, C_V), F32),
                        pltpu.VMEM((nb, tb, LANES), F32),
                        pltpu.VMEM((nb, C_HEADS, LANES, 2 * C_DV), F32),
                        pltpu.VMEM((nb, C_HEADS, LANES), F32)],
        compiler_params=pltpu.CompilerParams(dimension_semantics=("parallel", "arbitrary"),
                                             vmem_limit_bytes=VMEM_LIMIT),
        name="layer1",
    )(x3d, mix3d, c0, n0, m0, q["wo0"], q["norm1"], q["w1"], q["gate_bias"], q["wo1"], q["normf"])


def _gate_tile(cols):
    tiled = jnp.concatenate([cols.astype(F32)] * GATE_COPIES, axis=1)
    return jnp.pad(tiled, ((0, 0), (0, LANES - tiled.shape[1])))


def _prepare_params(p):
    w_in0 = p["w_in0"]
    a_cols = A_MAIN + 2 * A_HEADS
    w0 = jnp.concatenate([w_in0[:, 0:A_MAIN], _gate_tile(w_in0[:, A_MAIN:a_cols]), w_in0[:, a_cols:]], axis=1)
    w_in1 = p["w_in1"]
    w1 = jnp.concatenate([w_in1[:, 0:P1_GATES], _gate_tile(w_in1[:, P1_GATES:])], axis=1)
    zeros_h = jnp.zeros((A_HEADS,), F32)
    alog = _gate_tile(jnp.concatenate([zeros_h, p["gdn_a_log"]]).reshape(1, -1))
    dtb = _gate_tile(jnp.concatenate([zeros_h, p["gdn_dt_bias"]]).reshape(1, -1))
    gate_bias = _gate_tile(jnp.concatenate([p["mlstm_b_i"], p["mlstm_b_f"]]).reshape(1, -1))
    w2a2 = jnp.zeros((2 * B_LORA, 2 * B_W), F32)
    w2a2 = w2a2.at[0:B_LORA, 0:B_W].set(p["rwkv_w2"]).at[B_LORA:, B_W:].set(p["rwkv_a2"])
    row = lambda v: v.reshape(1, -1).astype(F32)
    return dict(
        w0=w0.astype(BF16), w1=w1.astype(BF16), wb=w_in0[:, a_cols:].astype(BF16),
        wo0=p["w_out0"].astype(BF16), wo1=p["w_out1"].astype(BF16), w2a2=w2a2.astype(BF16),
        norm0=row(p["norm0_w"]), norm1=row(p["norm1_w"]), normf=row(p["norm_f_w"]),
        conv_w=p["gdn_conv_w"].astype(F32), alog=alog, dtb=dtb, gnw=row(p["gdn_norm_w"]),
        mu=row(p["rwkv_mu"]), rw0=row(p["rwkv_w0"]), ra0=row(p["rwkv_a0"]),
        k_k=row(p["rwkv_k_k"]), k_a=row(p["rwkv_k_a"]), r_k=row(p["rwkv_r_k"]),
        ln_w=row(p["rwkv_ln_w"]), ln_b=row(p["rwkv_ln_b"]), gate_bias=gate_bias,
    )


def _tiles(bsz, seq):
    chunk = math.gcd(seq, MAX_CHUNK)
    tb = min(seq, 256)
    nb = 2 if seq > MAX_CHUNK else 16
    nb = math.gcd(nb, bsz)
    tt = min(seq, PROJ_ROWS)
    ns = math.gcd(min(PROJ_ROWS // tt, PROJ_MAX_SEQS), bsz)
    return chunk, tb, nb, tt, ns


def _trunk(x, state, q):
    conv0, s_gdn, x_prev, s_rwkv, c0, n0, m0 = state
    bsz, seq, _ = x.shape
    chunk, tb, nb, tt, ns = _tiles(bsz, seq)
    xn_last, prev_row = _aux(x[:, -1, :], x_prev, q["norm0"], q["wb"])
    fa, fb, conv_new = _proj0(x, conv0, prev_row.reshape(bsz, 1, B_COLS), q, ns, tt)
    mix, s_gdn_new, s_rwkv_new = _layer0_mixers(fa, fb, s_gdn, s_rwkv, q, nb, tb, chunk)
    y, c_new, n_new, m_new = _layer1(x, mix, c0, n0.reshape(bsz, C_HEADS // 2, LANES),
                                     m0.reshape(bsz, 1, C_HEADS), q, nb, tb, chunk)
    return y, (conv_new, s_gdn_new, xn_last, s_rwkv_new, c_new,
               n_new.reshape(bsz, C_HEADS, C_DQK), m_new[:, :, 0])


def kernel(x_prompt, x_sample, state_gdn_conv, state_gdn, state_rwkv_shift, state_rwkv,
           state_mlstm_c, state_mlstm_n, state_mlstm_m,
           norm0_w, w_in0, w_out0, gdn_conv_w, gdn_a_log, gdn_dt_bias, gdn_norm_w,
           rwkv_mu, rwkv_w0, rwkv_w2, rwkv_a0, rwkv_a2, rwkv_k_k, rwkv_k_a, rwkv_r_k,
           rwkv_ln_w, rwkv_ln_b, norm1_w, w_in1, w_out1, mlstm_b_i, mlstm_b_f, norm_f_w):
    q = _prepare_params(dict(
        norm0_w=norm0_w, w_in0=w_in0, w_out0=w_out0, gdn_conv_w=gdn_conv_w, gdn_a_log=gdn_a_log,
        gdn_dt_bias=gdn_dt_bias, gdn_norm_w=gdn_norm_w, rwkv_mu=rwkv_mu, rwkv_w0=rwkv_w0,
        rwkv_w2=rwkv_w2, rwkv_a0=rwkv_a0, rwkv_a2=rwkv_a2, rwkv_k_k=rwkv_k_k, rwkv_k_a=rwkv_k_a,
        rwkv_r_k=rwkv_r_k, rwkv_ln_w=rwkv_ln_w, rwkv_ln_b=rwkv_ln_b, norm1_w=norm1_w, w_in1=w_in1,
        w_out1=w_out1, mlstm_b_i=mlstm_b_i, mlstm_b_f=mlstm_b_f, norm_f_w=norm_f_w))
    bsz = x_prompt.shape[0]
    prompt_state = (
        jnp.zeros((bsz, A_CONV - 1, A_QKV), F32),
        jnp.zeros((bsz, A_HEADS, A_DK, A_DV), F32),
        jnp.zeros((bsz, D_MODEL), F32),
        jnp.zeros((bsz, B_HEADS, B_N, B_N), F32),
        jnp.zeros((bsz, C_HEADS, C_DQK, C_DV), F32),
        jnp.zeros((bsz, C_HEADS, C_DQK), F32),
        jnp.zeros((bsz, C_HEADS), F32),
    )
    y_p, sp = _trunk(x_prompt, prompt_state, q)
    sample_state = (state_gdn_conv, state_gdn, state_rwkv_shift, state_rwkv,
                    state_mlstm_c, state_mlstm_n, state_mlstm_m)
    y_s, ss = _trunk(x_sample, sample_state, q)
    return (y_p, y_s) + tuple(sp) + tuple(ss)
```

```python
import functools
import math

import jax
import jax.numpy as jnp
from jax import lax
from jax.experimental import pallas as pl
from jax.experimental.pallas import tpu as pltpu

F32 = jnp.float32
BF16 = jnp.bfloat16

D_MODEL = 1024
EPS = 1e-6
LANES = 128
SUBLANES = 8
A_HEADS, A_DK, A_DV, A_CONV = 4, 128, 128, 4
A_KEY = A_HEADS * A_DK
A_VAL = A_HEADS * A_DV
A_QKV = 2 * A_KEY + A_VAL
A_MAIN = A_QKV + A_VAL
B_HEADS, B_N, B_LORA = 8, 64, 64
B_W = B_HEADS * B_N
B_COLS = 4 * B_W + 2 * B_LORA
B_PAIRS = B_HEADS // 2
B_GN_EPS = 64e-5
B_DECAY_SCALE = math.exp(-0.5)
MIX0 = A_VAL + B_W
C_HEADS, C_DQK, C_DV = 8, 64, 128
C_QK = C_HEADS * C_DQK
C_V = C_HEADS * C_DV
C_GATE_CAP = 15.0
P0_GATES = A_MAIN
P0_B = A_MAIN + LANES
P0_PAD = P0_B + B_COLS
P1_GATES = 2 * C_QK + 3 * C_V
P1_PAD = P1_GATES + LANES
MAX_CHUNK = 64
GATE_COPIES = 3
CUMSUM_TERMS = 3
SUM_TERMS = 2
VMEM_LIMIT = 56 * 1024 * 1024


def _dot(a, b):
    return jnp.dot(a.astype(BF16), b.astype(BF16), preferred_element_type=F32)


def _dot_nt(a, b):
    return lax.dot_general(a.astype(BF16), b.astype(BF16), (((1,), (1,)), ((), ())),
                           preferred_element_type=F32)


def _dot_tn(a, b):
    return lax.dot_general(a.astype(BF16), b.astype(BF16), (((0,), (0,)), ((), ())),
                           preferred_element_type=F32)


def _split(x, terms):
    out = []
    for _ in range(terms - 1):
        hi = x.astype(BF16)
        out.append(hi)
        x = x - hi.astype(F32)
    out.append(x.astype(BF16))
    return out


def _stack_selector(sel, terms, axis):
    s = jnp.where(sel, 1.0, 0.0).astype(BF16)
    return jnp.concatenate([s] * terms, axis=axis)


def _dot_exact_lhs(sel_stacked, x, terms=CUMSUM_TERMS):
    return jnp.dot(sel_stacked, jnp.concatenate(_split(x, terms), axis=0), preferred_element_type=F32)


def _dot_exact_rhs(x, sel_stacked, terms=SUM_TERMS):
    return jnp.dot(jnp.concatenate(_split(x, terms), axis=1), sel_stacked, preferred_element_type=F32)


def _iota2(shape, dim):
    return lax.broadcasted_iota(jnp.int32, shape, dim)


def _tri_incl(n):
    return (_iota2((n, n), 0) >= _iota2((n, n), 1))


def _eye(n):
    return jnp.where(_iota2((n, n), 0) == _iota2((n, n), 1), 1.0, 0.0)


def _col_to_row(col):
    n = col.shape[0]
    eye = _iota2((n, n), 0) == _iota2((n, n), 1)
    return jnp.sum(jnp.where(eye, col, 0.0), axis=0, keepdims=True)


def _row_to_col(row):
    n = row.shape[1]
    eye = _iota2((n, n), 0) == _iota2((n, n), 1)
    return jnp.sum(jnp.where(eye, row, 0.0), axis=1, keepdims=True)


def _replicate(x, sel, group):
    hi, mid, lo = _split(x, GATE_COPIES)
    lane = _iota2(x.shape, 1)
    packed = jnp.where(lane < group, hi, jnp.where(lane < 2 * group, mid, lo))
    packed = jnp.where(lane < GATE_COPIES * group, packed, jnp.zeros_like(packed))
    return jnp.dot(packed, sel, preferred_element_type=F32)


def _replicate_selector(group):
    assert group & (group - 1) == 0 and LANES == 128
    k = _iota2((LANES, group * LANES), 0)
    col = _iota2((LANES, group * LANES), 1)
    hit = (k < GATE_COPIES * group) & (jnp.bitwise_and(k, group - 1) == jnp.right_shift(col, 7))
    return jnp.where(hit, 1.0, 0.0).astype(BF16)


def _cummax_rows(x):
    n = x.shape[0]
    row = _iota2(x.shape, 0)
    sh = 1
    while sh < n:
        x = jnp.maximum(x, jnp.where(row >= sh, pltpu.roll(x, sh, 0), -jnp.inf))
        sh *= 2
    return x


def _inv_unit_lower(p, levels, eye):
    n = p.shape[0]
    t = eye + p
    m = 2
    if m < levels:
        p = _dot(p, p)
        yield
    while m < levels:
        m *= 2
        if m >= levels:
            t = t + _dot(p, t)
        elif n % LANES == 0:
            both = _dot(p, jnp.concatenate([p, t], axis=1))
            p, t = both[:, 0:n], t + both[:, n:2 * n]
            yield
        else:
            both = _dot(jnp.concatenate([p, t], axis=0), p)
            p, t = both[0:n, :], t + both[n:2 * n, :]
            yield
    return t


def _run_tasks(tasks, caps):
    done, live, pending = set(), [], list(tasks)
    while pending or live:
        count = {}
        for _, group, _ in live:
            count[group] = count.get(group, 0) + 1
        still = []
        for key, group, deps, make in pending:
            if all(d in done for d in deps) and count.get(group, 0) < caps.get(group, len(tasks)):
                live.append((key, group, make()))
                count[group] = count.get(group, 0) + 1
            else:
                still.append((key, group, deps, make))
        assert live, "task graph has a cycle or a missing dependency"
        pending = still
        nxt = []
        for key, group, gen in live:
            try:
                next(gen)
                nxt.append((key, group, gen))
            except StopIteration:
                done.add(key)
        live = nxt


def _sigmoid(x):
    return 1.0 / (1.0 + jnp.exp(-x))


def _silu(x):
    return x * _sigmoid(x)


def _softplus(x):
    return jnp.maximum(x, 0.0) + jnp.log1p(jnp.exp(-jnp.abs(x)))


def _rms(x, w):
    return x * lax.rsqrt(jnp.mean(x * x, axis=-1, keepdims=True) + EPS) * w


def _resident(shape):
    zeros = (0,) * len(shape)
    return pl.BlockSpec(shape, lambda *_: zeros, pipeline_mode=pl.Buffered(1))


FA_Z = A_QKV
FA_BETA = A_MAIN
FA_DECAY = A_MAIN + LANES
FA_COLS = A_MAIN + 2 * LANES
_FR, _FLW, _FK, _FV, _FA, _FB, _FG = range(7)
FB_COLS = 7 * B_W
SUB_ROWS = 64
PROJ_ROWS = 512
PROJ_MAX_SEQS = 32


def _proj0_kernel(x_ref, conv0_ref, prev_ref, nw_ref, w_ref, convw_ref, alog_ref, dtb_ref,
                  mu_ref, w0_ref, a0_ref, w2a2_ref, kk_ref, ka_ref,
                  fa_ref, fb_ref, convn_ref, ext_a, ext_b, *, ns, tt):
    i = pl.program_id(1)
    tail = SUBLANES
    rows = ns * tt
    lane = _iota2((1, LANES), 1)
    head0 = lane < B_N
    ones_blk = (_iota2((LANES, LANES), 0) // B_N == _iota2((LANES, LANES), 1) // B_N)
    ones_sel = _stack_selector(ones_blk, SUM_TERMS, 0)

    @pl.when(i == 0)
    def _():
        ext_a[:, 0:tail, :] = jnp.zeros((ns, tail, A_QKV), F32)
        ext_a[:, tail - (A_CONV - 1):tail, :] = conv0_ref[...]
        ext_b[:, 0:tail, :] = jnp.zeros((ns, tail, B_COLS), F32)
        ext_b[:, tail - 1:tail, :] = prev_ref[...]

    @pl.when(i > 0)
    def _():
        ext_a[:, 0:tail, :] = ext_a[:, tt:tt + tail, :]
        ext_b[:, 0:tail, :] = ext_b[:, tt:tt + tail, :]

    xn = _rms(x_ref[...].reshape(rows, D_MODEL), nw_ref[...]).astype(BF16)
    piece = 4 * LANES

    def cols(lo, hi):
        return jnp.dot(xn, w_ref[:, lo:hi], preferred_element_type=F32).reshape(ns, tt, hi - lo)

    for lo in range(0, A_QKV, piece):
        ext_a[:, tail:tail + tt, lo:lo + piece] = cols(lo, lo + piece)
    fa_ref[:, :, FA_Z:FA_Z + A_VAL] = cols(A_QKV, A_MAIN)
    g = cols(P0_GATES, P0_B)
    fa_ref[:, :, FA_BETA:FA_BETA + LANES] = _sigmoid(g)
    fa_ref[:, :, FA_DECAY:FA_DECAY + LANES] = -jnp.exp(alog_ref[...]) * _softplus(g + dtb_ref[...])
    for lo in range(0, B_COLS, piece):
        hi = min(lo + piece, B_COLS)
        ext_b[:, tail:tail + tt, lo:hi] = cols(P0_B + lo, P0_B + hi)

    sub_t = min(tt, SUB_ROWS)
    sub_s = min(ns, SUB_ROWS // sub_t)
    cw = convw_ref[...]
    mu = mu_ref[...]
    for s0 in range(0, ns, sub_s):
        sq = slice(s0, s0 + sub_s)
        for t0 in range(0, tt, sub_t):
            tr = slice(t0, t0 + sub_t)
            y = ext_a[sq, tail + t0:tail + t0 + sub_t, :] * cw[A_CONV - 1:A_CONV, :]
            for j in range(A_CONV - 1):
                sh = A_CONV - 1 - j
                y = y + ext_a[sq, tail + t0 - sh:tail + t0 - sh + sub_t, :] * cw[j:j + 1, :]
            y = _silu(y)
            for h in range(A_HEADS):
                q = y[:, :, h * A_DK:(h + 1) * A_DK]
                fa_ref[sq, tr, h * A_DK:(h + 1) * A_DK] = (
                    q * lax.rsqrt(jnp.sum(q * q, axis=-1, keepdims=True) + EPS) * (A_DK ** -0.5))
                k = y[:, :, A_KEY + h * A_DK:A_KEY + (h + 1) * A_DK]
                fa_ref[sq, tr, A_KEY + h * A_DK:A_KEY + (h + 1) * A_DK] = (
                    k * lax.rsqrt(jnp.sum(k * k, axis=-1, keepdims=True) + EPS))
            fa_ref[sq, tr, 2 * A_KEY:A_QKV] = y[:, :, 2 * A_KEY:A_QKV]
            pb = ext_b[sq, tail + t0:tail + t0 + sub_t, :]
            pb_prev = ext_b[sq, tail + t0 - 1:tail + t0 - 1 + sub_t, :]
            pbs = (pb + (pb_prev - pb) * mu).reshape(sub_s * sub_t, B_COLS)
            lo_ = pbs[:, 4 * B_W:B_COLS]
            lo_ = jnp.where(head0, jnp.tanh(lo_), lo_)
            wa = _dot(lo_, w2a2_ref[...])
            logw = -B_DECAY_SCALE * _sigmoid(w0_ref[...] + wa[:, 0:B_W])
            a = _sigmoid(a0_ref[...] + wa[:, B_W:2 * B_W])
            kb = pbs[:, B_W:2 * B_W]
            kkx = kb * kk_ref[...]

            def put(slot, val):
                fb_ref[sq, tr, slot * B_W:(slot + 1) * B_W] = val.reshape(sub_s, sub_t, B_W)

            kkn = jnp.concatenate(
                [kkx[:, p * LANES:(p + 1) * LANES]
                 * lax.rsqrt(_dot_exact_rhs(kkx[:, p * LANES:(p + 1) * LANES] ** 2, ones_sel) + EPS)
                 for p in range(B_PAIRS)], axis=1)
            put(_FR, pbs[:, 0:B_W])
            put(_FLW, logw)
            put(_FK, kb * (1.0 + (a - 1.0) * ka_ref[...]))
            put(_FV, pbs[:, 2 * B_W:3 * B_W])
            put(_FA, -kkn)
            put(_FB, kkn * a)
            put(_FG, pbs[:, 3 * B_W:4 * B_W])

    @pl.when(i == pl.num_programs(1) - 1)
    def _():
        convn_ref[...] = ext_a[:, tt + tail - (A_CONV - 1):tt + tail, :]


def _proj0(x3d, conv0, prev_row, q, ns, tt):
    bsz, seq, _ = x3d.shape
    blk = lambda b, i: (b, i, 0)
    per_b3 = lambda b, i: (b, 0, 0)
    kern = functools.partial(_proj0_kernel, ns=ns, tt=tt)
    return pl.pallas_call(
        kern,
        grid=(bsz // ns, seq // tt),
        in_specs=[pl.BlockSpec((ns, tt, D_MODEL), blk),
                  pl.BlockSpec((ns, A_CONV - 1, A_QKV), per_b3),
                  pl.BlockSpec((ns, 1, B_COLS), per_b3),
                  _resident((1, D_MODEL)),
                  _resident((D_MODEL, P0_PAD)),
                  _resident((A_CONV, A_QKV)), _resident((1, LANES)), _resident((1, LANES)),
                  _resident((1, B_COLS)), _resident((1, B_W)), _resident((1, B_W)),
                  _resident((2 * B_LORA, 2 * B_W)), _resident((1, B_W)), _resident((1, B_W))],
        out_specs=[pl.BlockSpec((ns, tt, FA_COLS), blk),
                   pl.BlockSpec((ns, tt, FB_COLS), blk),
                   pl.BlockSpec((ns, A_CONV - 1, A_QKV), per_b3)],
        out_shape=[jax.ShapeDtypeStruct((bsz, seq, FA_COLS), F32),
                   jax.ShapeDtypeStruct((bsz, seq, FB_COLS), F32),
                   jax.ShapeDtypeStruct((bsz, A_CONV - 1, A_QKV), F32)],
        scratch_shapes=[pltpu.VMEM((ns, tt + SUBLANES, A_QKV), F32),
                        pltpu.VMEM((ns, tt + SUBLANES, B_COLS), F32)],
        compiler_params=pltpu.CompilerParams(dimension_semantics=("parallel", "arbitrary"),
                                             vmem_limit_bytes=VMEM_LIMIT),
        name="proj0_features",
    )(x3d, conv0, prev_row, q["norm0"], q["w0"], q["conv_w"], q["alog"], q["dtb"],
      q["mu"], q["rw0"], q["ra0"], q["w2a2"], q["k_k"], q["k_a"])


def _aux_kernel(xl_ref, xp_ref, nw_ref, wb_ref, xn_ref, prev_ref):
    xn_ref[...] = _rms(xl_ref[...], nw_ref[...])
    prev_ref[...] = jnp.dot(xp_ref[...].astype(BF16), wb_ref[...], preferred_element_type=F32)


def _aux(x_last, x_prev, norm_w, wb):
    b = x_last.shape[0]
    return pl.pallas_call(
        _aux_kernel,
        out_shape=[jax.ShapeDtypeStruct((b, D_MODEL), F32), jax.ShapeDtypeStruct((b, B_COLS), F32)],
        compiler_params=pltpu.CompilerParams(vmem_limit_bytes=VMEM_LIMIT),
        name="shift_rows",
    )(x_last, x_prev, norm_w, wb)


def _layer0_kernel(fa_ref, fb_ref, sg0_ref, sr0_ref, gnw_ref, rk_ref, lnw_ref, lnb_ref,
                   o_ref, sgout_ref, srout_ref,
                   stg_s, uw_s, attn_s, qg_s, kd_s, ge_s, str_s, tmp_s, *, nb, tb, chunk):
    i = pl.program_id(1)
    n_chunks = tb // chunk
    lane = _iota2((1, LANES), 1)
    head0 = lane < B_N
    ones_blk = (_iota2((LANES, LANES), 0) // B_N == _iota2((LANES, LANES), 1) // B_N)

    @pl.when(i == 0)
    def _():
        for bi in range(nb):
            for p in range(B_PAIRS):
                tmp_s[...] = jnp.zeros((B_N, LANES), F32)
                tmp_s[:, 0:B_N] = sr0_ref[bi, 2 * p + 1]
                shifted = pltpu.roll(tmp_s[...], B_N, 1)
                str_s[bi, p, B_N:LANES, :] = jnp.where(head0, 0.0, shifted)
                tmp_s[:, 0:B_N] = sr0_ref[bi, 2 * p]
                str_s[bi, p, 0:B_N, :] = tmp_s[...]
        stg_s[...] = sg0_ref[...]

    tri = _tri_incl(chunk)
    tri_sel = _stack_selector(tri, CUMSUM_TERMS, 1)
    ones_sel = _stack_selector(ones_blk, SUM_TERMS, 0)
    eye_c = _eye(chunk)
    eye_2c = _eye(2 * chunk)

    n_gates = 2 * A_HEADS
    is_decay = jnp.bitwise_and(lane, n_gates - 1) >= A_HEADS
    sel = _replicate_selector(n_gates)
    strict = _iota2((chunk, chunk), 0) > _iota2((chunk, chunk), 1)
    gnw = gnw_ref[...]
    qkv_s = fa_ref
    rep = {}

    def gdn_gates(bi, c):
        rows = slice(c * chunk, (c + 1) * chunk)
        gcum = _dot_exact_lhs(tri_sel, fa_ref[bi, rows, FA_DECAY:FA_DECAY + LANES])
        yield
        rep[bi, c] = _replicate(jnp.where(is_decay, gcum, fa_ref[bi, rows, FA_BETA:FA_BETA + LANES]),
                                sel, n_gates)

    wpk = A_HEADS * chunk
    head_pk = _iota2((1, wpk), 1) // chunk
    s_pk = _iota2((1, wpk), 1) - head_pk * chunk
    t_pk = _iota2((chunk, wpk), 0)
    diag_pk = t_pk == s_pk
    tri_pk = t_pk >= s_pk
    strict_pk = t_pk > s_pk
    head_of_klane = _iota2((1, A_KEY), 1) // A_DK
    head_of_uwlane = _iota2((1, A_HEADS * (A_DV + A_DK)), 1) // (A_DV + A_DK)

    def widen(x):
        reps = -(-wpk // LANES)
        w = jnp.concatenate([x] * reps, axis=1) if reps > 1 else x
        return w if w.shape[1] == wpk else w[:, 0:wpk]

    def pack_heads(xs):
        out = xs[A_HEADS - 1]
        for h in range(A_HEADS - 2, -1, -1):
            out = jnp.where(head_pk == h, xs[h], out)
        return out

    def block_diag(x):
        return jnp.concatenate([jnp.where(head_pk == h, x, 0.0) for h in range(A_HEADS)], axis=0)

    def gdn_local_head(bi, c, h):
        rows = slice(c * chunk, (c + 1) * chunk)
        q = qkv_s[bi, rows, h * A_DK:(h + 1) * A_DK]
        k = qkv_s[bi, rows, A_KEY + h * A_DK:A_KEY + (h + 1) * A_DK]
        v = qkv_s[bi, rows, 2 * A_KEY + h * A_DV:2 * A_KEY + (h + 1) * A_DV]
        qk_k = _dot_nt(jnp.concatenate([q, k], axis=0), k)
        yield
        b_rep = rep[bi, c][:, h * LANES:(h + 1) * LANES]
        g_rep = rep[bi, c][:, (A_HEADS + h) * LANES:(A_HEADS + h + 1) * LANES]
        g_row = _col_to_row(g_rep[:, 0:chunk])
        g_last = g_rep[chunk - 1:chunk, :]
        decay = jnp.where(tri, jnp.exp(jnp.where(tri, g_rep[:, 0:chunk] - g_row, 0.0)), 0.0)
        eg = jnp.exp(g_rep)
        qg_s[bi, rows, h * A_DK:(h + 1) * A_DK] = q * eg
        kd_s[bi, rows, h * A_DK:(h + 1) * A_DK] = k * jnp.exp(g_last - g_rep)
        ge_s[bi, c, h:h + 1, :] = jnp.exp(g_last)
        attn_s[bi, rows, h * LANES:h * LANES + chunk] = decay * qk_k[0:chunk, :]
        amat = jnp.where(strict, b_rep[:, 0:chunk] * decay * qk_k[chunk:2 * chunk, :], 0.0)
        t_inv = yield from _inv_unit_lower(-amat, chunk, eye_c)
        uw_s[bi, rows, 2 * h * LANES:2 * (h + 1) * LANES] = _dot(
            t_inv, jnp.concatenate([b_rep * v, (b_rep * eg) * k], axis=1))

    def gdn_local(bi, c):
        rows = slice(c * chunk, (c + 1) * chunk)
        heads = range(A_HEADS)
        q = [qkv_s[bi, rows, h * A_DK:(h + 1) * A_DK] for h in heads]
        k = [qkv_s[bi, rows, A_KEY + h * A_DK:A_KEY + (h + 1) * A_DK] for h in heads]
        v = [qkv_s[bi, rows, 2 * A_KEY + h * A_DV:2 * A_KEY + (h + 1) * A_DV] for h in heads]
        qk = [_dot_nt(q[h], k[h]) for h in heads]
        k_bd = jnp.concatenate([jnp.where(head_of_klane == h, jnp.concatenate([k[h]] * A_HEADS, axis=1), 0.0)
                                for h in heads], axis=0)
        kk_pk = _dot_nt(jnp.concatenate(k, axis=1), k_bd)
        yield
        b_rep = [rep[bi, c][:, h * LANES:(h + 1) * LANES] for h in heads]
        g_rep = [rep[bi, c][:, (A_HEADS + h) * LANES:(A_HEADS + h + 1) * LANES] for h in heads]
        eg = [jnp.exp(g) for g in g_rep]
        for h in heads:
            g_row = _col_to_row(g_rep[h][:, 0:chunk])
            g_last = g_rep[h][chunk - 1:chunk, :]
            decay = jnp.where(tri, jnp.exp(jnp.where(tri, g_rep[h][:, 0:chunk] - g_row, 0.0)), 0.0)
            qg_s[bi, rows, h * A_DK:(h + 1) * A_DK] = q[h] * eg[h]
            kd_s[bi, rows, h * A_DK:(h + 1) * A_DK] = k[h] * jnp.exp(g_last - g_rep[h])
            ge_s[bi, c, h:h + 1, :] = jnp.exp(g_last)
            attn_s[bi, rows, h * LANES:h * LANES + chunk] = decay * qk[h]
        g_col_pk = pack_heads([widen(g) for g in g_rep])
        b_pk = pack_heads([widen(b) for b in b_rep])
        g_row_pk = jnp.sum(jnp.where(diag_pk, g_col_pk, 0.0), axis=0, keepdims=True)
        decay_pk = jnp.where(tri_pk, jnp.exp(jnp.where(tri_pk, g_col_pk - g_row_pk, 0.0)), 0.0)
        p = -jnp.where(strict_pk, b_pk * decay_pk * kk_pk, 0.0)
        t = jnp.where(diag_pk, 1.0, 0.0) + p
        m = 2
        if m < chunk:
            p = _dot(p, block_diag(p))
            yield
        while m < chunk:
            m *= 2
            if m >= chunk:
                t = t + _dot(t, block_diag(p))
            else:
                both = _dot(jnp.concatenate([p, t], axis=0), block_diag(p))
                p, t = both[0:chunk, :], t + both[chunk:2 * chunk, :]
                yield
        rhs = jnp.concatenate(
            [jnp.where(head_of_uwlane == h,
                       jnp.concatenate([b_rep[h] * v[h], (b_rep[h] * eg[h]) * k[h]] * A_HEADS, axis=1), 0.0)
             for h in heads], axis=0)
        uw_s[bi, rows, :] = _dot(t, rhs)

    def gdn_state(bi, c, h):
        rows = slice(c * chunk, (c + 1) * chunk)
        uw = uw_s[bi, rows, 2 * h * LANES:2 * (h + 1) * LANES]
        s = stg_s[bi, h]
        wq_s = _dot(jnp.concatenate([uw[:, A_DV:A_DV + A_DK], qg_s[bi, rows, h * A_DK:(h + 1) * A_DK]], axis=0), s)
        yield
        nv = uw[:, 0:A_DV] - wq_s[0:chunk, :]
        o = wq_s[chunk:2 * chunk, :] + _dot(attn_s[bi, rows, h * LANES:h * LANES + chunk], nv)
        stg_s[bi, h] = ge_s[bi, c, h:h + 1, :] * s + _dot_tn(kd_s[bi, rows, h * A_DK:(h + 1) * A_DK], nv)
        yield
        z = fa_ref[bi, rows, FA_Z + h * A_DV:FA_Z + (h + 1) * A_DV]
        o_ref[bi, rows, h * A_DV:(h + 1) * A_DV] = _rms(o, gnw) * _silu(z)

    feat_s = fb_ref
    c2 = 2 * chunk
    ri = _iota2((c2, c2), 0)
    ci = _iota2((c2, c2), 1)
    same_head = (ri // chunk) == (ci // chunk)
    incl_bd = same_head & (ri >= ci)
    strict_bd = same_head & (ri > ci)
    ri4 = _iota2((2 * c2, c2), 0)
    ci4 = _iota2((2 * c2, c2), 1)
    rr4 = ri4 % c2
    lm_mask = ((rr4 // chunk) == (ci4 // chunk)) & ((rr4 > ci4) | ((ri4 >= c2) & (rr4 == ci4)))
    lane_st = (_iota2((c2, LANES), 0) // chunk) == (_iota2((c2, LANES), 1) // B_N)
    local = {}

    def rwkv_local(bi, c, p):
        rows = slice(c * chunk, (c + 1) * chunk)

        def feat(slot):
            return feat_s[bi, rows, slot * B_W + p * LANES:slot * B_W + (p + 1) * LANES]
        r, lw, k, v, a_, b_ = (feat(s) for s in (_FR, _FLW, _FK, _FV, _FA, _FB))
        cs = _dot_exact_lhs(tri_sel, lw)
        yield
        cs_last = cs[chunk - 1:chunk, :]
        e_inv = jnp.exp(-cs)
        e_rem = jnp.exp(cs_last - cs)
        at = a_ * jnp.exp(cs - lw)
        rt = r * jnp.exp(cs)
        bt = b_ * e_inv
        kt = k * e_inv
        a_st = jnp.concatenate([jnp.where(head0, at, 0.0), jnp.where(head0, 0.0, at)], axis=0)
        r_st = jnp.concatenate([jnp.where(head0, rt, 0.0), jnp.where(head0, 0.0, rt)], axis=0)
        b2 = jnp.concatenate([bt, bt], axis=0)
        k2 = jnp.concatenate([kt, kt], axis=0)
        v2 = jnp.concatenate([v, v], axis=0)
        ar_st = jnp.concatenate([a_st, r_st], axis=0)
        if c2 % LANES == 0:
            ar_bk = _dot_nt(ar_st, jnp.concatenate([b2, k2], axis=0))
            ar_b, ar_k = ar_bk[:, 0:c2], ar_bk[:, c2:2 * c2]
        else:
            ar_b = _dot_nt(ar_st, b2)
            ar_k = _dot_nt(ar_st, k2)
        yield
        l_ab = jnp.where(strict_bd, ar_b[0:c2, :], 0.0)
        m_rb = jnp.where(incl_bd, ar_b[c2:2 * c2, :], 0.0)
        lm_k = jnp.where(lm_mask, ar_k, 0.0)
        lm_v = _dot(lm_k, v2)
        sl = slice(p * LANES, (p + 1) * LANES)
        bonus = _dot_exact_rhs(r * k * rk_ref[:, sl], ones_sel) * v
        t_inv = yield from _inv_unit_lower(l_ab, chunk, eye_2c)
        local[bi, c, p] = (t_inv, lm_v, m_rb, jnp.concatenate([at, rt], axis=0),
                           jnp.concatenate([b_ * e_rem, k * e_rem], axis=0), jnp.exp(cs_last), bonus)

    def rwkv_state(bi, c, p):
        rows = slice(c * chunk, (c + 1) * chunk)
        sl = slice(p * LANES, (p + 1) * LANES)
        t_inv, lm_v, m_rb, at_rt, bk_rem, decay_end, bonus = local.pop((bi, c, p))
        v = feat_s[bi, rows, _FV * B_W + p * LANES:_FV * B_W + (p + 1) * LANES]
        gate = feat_s[bi, rows, _FG * B_W + p * LANES:_FG * B_W + (p + 1) * LANES]
        s = str_s[bi, p]
        ar = _dot_nt(at_rt, s)
        yield
        ah = ar[0:chunk, :]
        rh = ar[chunk:c2, :]
        x_st = jnp.where(lane_st, jnp.concatenate([ah, ah], axis=0) + lm_v[0:c2, :], 0.0)
        u_st = _dot(t_inv, x_st)
        yield
        o_st = jnp.where(lane_st, _dot(m_rb, u_st) + lm_v[c2:2 * c2, :], 0.0)
        u = u_st[0:chunk, :] + u_st[chunk:c2, :]
        upd = _dot_tn(jnp.concatenate([u, v], axis=0), bk_rem)
        yield
        o = rh + o_st[0:chunk, :] + o_st[chunk:c2, :]
        str_s[bi, p] = s * decay_end + jnp.where(ones_blk, upd, 0.0)
        mean = _dot_exact_rhs(o, ones_sel) * (1.0 / B_N)
        yield
        d = o - mean
        var = _dot_exact_rhs(d * d, ones_sel) * (1.0 / B_N)
        yield
        ob = d * lax.rsqrt(var + B_GN_EPS) * lnw_ref[:, sl] + lnb_ref[:, sl] + bonus
        o_ref[bi, rows, A_VAL + p * LANES:A_VAL + (p + 1) * LANES] = ob * _silu(gate)

    packed = wpk >= LANES
    tasks = []
    blocks = [(bi, c) for c in range(n_chunks) for bi in range(nb)]
    for bi, c in blocks:
        for p in range(B_PAIRS):
            deps = [("rl", bi, c, p)] + ([("rs", bi, c - 1, p)] if c else [])
            tasks.append((("rs", bi, c, p), "rwkv_state", deps, functools.partial(rwkv_state, bi, c, p)))
        for h in range(A_HEADS):
            local_key = ("gl", bi, c) if packed else ("gl", bi, c, h)
            deps = [local_key] + ([("gs", bi, c - 1, h)] if c else [])
            tasks.append((("gs", bi, c, h), "gdn_state", deps, functools.partial(gdn_state, bi, c, h)))
    for bi, c in blocks:
        for p in range(B_PAIRS):
            tasks.append((("rl", bi, c, p), "rwkv_local", [], functools.partial(rwkv_local, bi, c, p)))
        if packed:
            tasks.append((("gl", bi, c), "gdn_local", [("gg", bi, c)], functools.partial(gdn_local, bi, c)))
        else:
            for h in range(A_HEADS):
                tasks.append((("gl", bi, c, h), "gdn_local", [("gg", bi, c)],
                              functools.partial(gdn_local_head, bi, c, h)))
    for bi, c in blocks:
        tasks.append((("gg", bi, c), "gdn_gates", [], functools.partial(gdn_gates, bi, c)))
    caps = {"gdn_gates": 2, "rwkv_local": 2 * B_PAIRS, "gdn_local": 4 if packed else 2 * A_HEADS}
    _run_tasks(tasks, caps if n_chunks > 1 else {})

    @pl.when(i == pl.num_programs(1) - 1)
    def _():
        for bi in range(nb):
            for p in range(B_PAIRS):
                srout_ref[bi, 2 * p] = str_s[bi, p, 0:B_N, 0:B_N]
                shifted = pltpu.roll(str_s[bi, p, B_N:LANES, :], B_N, 1)
                srout_ref[bi, 2 * p + 1] = shifted[:, 0:B_N]
        sgout_ref[...] = stg_s[...]


def _layer0_mixers(fa, fb, s_gdn, s_rwkv, q, nb, tb, chunk):
    bsz, seq, _ = fa.shape
    blk = lambda b, i: (b, i, 0)
    per_b4 = lambda b, i: (b, 0, 0, 0)
    fix2 = lambda b, i: (0, 0)
    kern = functools.partial(_layer0_kernel, nb=nb, tb=tb, chunk=chunk)
    vec = pl.BlockSpec((1, B_W), fix2)
    return pl.pallas_call(
        kern,
        grid=(bsz // nb, seq // tb),
        in_specs=[pl.BlockSpec((nb, tb, FA_COLS), blk),
                  pl.BlockSpec((nb, tb, FB_COLS), blk),
                  pl.BlockSpec((nb, A_HEADS, A_DK, A_DV), per_b4),
                  pl.BlockSpec((nb, B_HEADS, B_N, B_N), per_b4),
                  pl.BlockSpec((1, A_DV), fix2), vec, vec, vec],
        out_specs=[pl.BlockSpec((nb, tb, MIX0), blk),
                   pl.BlockSpec((nb, A_HEADS, A_DK, A_DV), per_b4),
                   pl.BlockSpec((nb, B_HEADS, B_N, B_N), per_b4)],
        out_shape=[jax.ShapeDtypeStruct((bsz, seq, MIX0), F32),
                   jax.ShapeDtypeStruct((bsz, A_HEADS, A_DK, A_DV), F32),
                   jax.ShapeDtypeStruct((bsz, B_HEADS, B_N, B_N), F32)],
        scratch_shapes=[pltpu.VMEM((nb, A_HEADS, A_DK, A_DV), F32),
                        pltpu.VMEM((nb, tb, A_HEADS * (A_DV + A_DK)), F32),
                        pltpu.VMEM((nb, tb, A_HEADS * LANES), F32),
                        pltpu.VMEM((nb, tb, A_KEY), F32),
                        pltpu.VMEM((nb, tb, A_KEY), F32),
                        pltpu.VMEM((nb, tb // chunk, SUBLANES, LANES), F32),
                        pltpu.VMEM((nb, B_PAIRS, LANES, LANES), F32),
                        pltpu.VMEM((B_N, LANES), F32)],
        compiler_params=pltpu.CompilerParams(dimension_semantics=("parallel", "arbitrary"),
                                             vmem_limit_bytes=VMEM_LIMIT),
        name="layer0_mixers",
    )(fa, fb, s_gdn, s_rwkv, q["gnw"], q["r_k"], q["ln_w"], q["ln_b"])


def _layer1_kernel(x_ref, mix_ref, c0_ref, n0_ref, m0_ref, wo0_ref, n1_ref, w1_ref, bias_ref, wo1_ref, nf_ref,
                   y_ref, cout_ref, nout_ref, mout_ref,
                   proj_s, x1_s, h_s, gsc_s, caug_s, m_s, *, nb, tb, chunk):
    i = pl.program_id(1)
    n_chunks = tb // chunk
    lane = _iota2((1, LANES), 1)
    krow = _iota2((LANES, 1), 0)
    n_gates = 2 * C_HEADS
    is_f = jnp.bitwise_and(lane, n_gates - 1) >= C_HEADS

    @pl.when(i == 0)
    def _():
        for bi in range(nb):
            m0 = m0_ref[bi]
            for h in range(C_HEADS):
                j = h % 2
                caug_s[bi, h] = jnp.zeros((LANES, 2 * C_DV), F32)
                caug_s[bi, h, j * C_DQK:(j + 1) * C_DQK, 0:C_DV] = c0_ref[bi, h]
                n_col = _row_to_col(n0_ref[bi, h // 2:h // 2 + 1, :])
                own = (krow // C_DQK) == j
                caug_s[bi, h, :, C_DV:2 * C_DV] = jnp.where(own, jnp.broadcast_to(n_col, (LANES, C_DV)), 0.0)
                m_s[bi, h:h + 1, :] = jnp.broadcast_to(m0[:, h:h + 1], (1, LANES))

    tri = _tri_incl(chunk)
    tri_sel = _stack_selector(tri, CUMSUM_TERMS, 1)
    sel = _replicate_selector(n_gates)
    ones_blk = jnp.ones((chunk, C_DV), F32)
    off = 2 * C_QK
    piece = 4 * LANES
    rep = {}

    def stack(ref, rows):
        return jnp.concatenate([ref[bi, rows, :] for bi in range(nb)], axis=0)

    def unstack(ref, rows, cols, val):
        for bi in range(nb):
            ref[bi, rows, cols] = val[bi * chunk:(bi + 1) * chunk, :]

    def proj_task(c):
        rows = slice(c * chunk, (c + 1) * chunk)
        x1 = stack(x_ref, rows) + jnp.dot(stack(mix_ref, rows).astype(BF16), wo0_ref[...],
                                          preferred_element_type=F32)
        unstack(x1_s, rows, slice(0, D_MODEL), x1)
        yield
        xn = _rms(x1, n1_ref[...]).astype(BF16)
        g = jnp.dot(xn, w1_ref[:, P1_GATES:P1_PAD], preferred_element_type=F32) + bias_ref[...]
        cap = C_GATE_CAP * jnp.tanh(g / C_GATE_CAP)
        unstack(gsc_s, rows, slice(0, LANES), jnp.where(is_f, -_softplus(-cap), cap))
        yield
        for j in range(0, P1_GATES, piece):
            unstack(proj_s, rows, slice(j, j + piece),
                    jnp.dot(xn, w1_ref[:, j:j + piece], preferred_element_type=F32))
            yield

    def gate_task(bi, c):
        rows = slice(c * chunk, (c + 1) * chunk)
        gl = gsc_s[bi, rows, :]
        bcum = _dot_exact_lhs(tri_sel, gl)
        yield
        rep[bi, c] = _replicate(jnp.where(is_f, bcum, gl), sel, n_gates)

    def head_chain(bi, c, h):
        rows = slice(c * chunk, (c + 1) * chunk)
        p, j = h // 2, h % 2
        own_lane = (lane // C_DQK) == j
        own_row = (krow // C_DQK) == j
        qp = proj_s[bi, rows, p * LANES:(p + 1) * LANES] * (C_DQK ** -0.5)
        kp = proj_s[bi, rows, C_QK + p * LANES:C_QK + (p + 1) * LANES]
        v = proj_s[bi, rows, off + h * C_DV:off + (h + 1) * C_DV]
        vaug = jnp.concatenate([v, ones_blk], axis=1)
        qk = _dot_nt(jnp.where(own_lane, qp, 0.0), kp)
        cm = caug_s[bi, h]
        qc = _dot(qp, cm)
        i_rep = rep[bi, c][:, h * LANES:(h + 1) * LANES]
        b_rep = rep[bi, c][:, (C_HEADS + h) * LANES:(C_HEADS + h + 1) * LANES]
        b_row = _col_to_row(b_rep[:, 0:chunk])
        i_row = _col_to_row(i_rep[:, 0:chunk])
        dmat = jnp.where(tri, b_rep[:, 0:chunk] - b_row + i_row, -jnp.inf)
        dmax = b_rep + _cummax_rows(i_rep - b_rep)
        m = m_s[bi, h:h + 1, :]
        inter = b_rep + m
        mt = jnp.maximum(inter, dmax)
        m_new = mt[chunk - 1:chunk, :]
        b_last = b_rep[chunk - 1:chunk, :]
        w_s = jnp.exp(b_last - b_rep + i_rep - m_new)
        e_c = jnp.exp(b_last + m - m_new)
        e_inter = jnp.exp(inter - mt)
        own_k = slice(j * C_DQK, (j + 1) * C_DQK)
        upd = _dot((w_s * kp).T[own_k, :], vaug)
        yield
        pm = jnp.exp(dmat - mt[:, 0:chunk]) * qk
        pv = _dot(pm, vaug)
        caug_s[bi, h, own_k, :] = jnp.concatenate([e_c, e_c], axis=1) * cm[own_k, :] + upd
        m_s[bi, h:h + 1, :] = m_new
        yield
        num = e_inter * qc[:, 0:C_DV] + pv[:, 0:C_DV]
        den = e_inter * qc[:, C_DV:2 * C_DV] + pv[:, C_DV:2 * C_DV]
        hh = num / jnp.maximum(jnp.abs(den), jnp.exp(-mt))
        og = proj_s[bi, rows, off + C_V + h * C_DV:off + C_V + (h + 1) * C_DV]
        z = proj_s[bi, rows, off + 2 * C_V + h * C_DV:off + 2 * C_V + (h + 1) * C_DV]
        h_s[bi, rows, h * C_DV:(h + 1) * C_DV] = hh * _sigmoid(og) * _silu(z)

    def final_task(c):
        rows = slice(c * chunk, (c + 1) * chunk)
        x2 = stack(x1_s, rows) + jnp.dot(stack(h_s, rows).astype(BF16), wo1_ref[...],
                                         preferred_element_type=F32)
        yield
        unstack(y_ref, rows, slice(0, D_MODEL), _rms(x2, nf_ref[...]))

    tasks = []
    for c in range(n_chunks):
        tasks.append((("p", c), "proj", [], functools.partial(proj_task, c)))
    for c in range(n_chunks):
        heads = [("h", bi, c, h) for bi in range(nb) for h in range(C_HEADS)]
        tasks.append((("f", c), "final", heads, functools.partial(final_task, c)))
        for bi in range(nb):
            tasks.append((("g", bi, c), "gates", [("p", c)], functools.partial(gate_task, bi, c)))
            for h in range(C_HEADS):
                deps = [("p", c), ("g", bi, c)] + ([("h", bi, c - 1, h)] if c else [])
                tasks.append((("h", bi, c, h), "heads", deps, functools.partial(head_chain, bi, c, h)))
    _run_tasks(tasks, {"proj": 1, "final": 1})

    @pl.when(i == pl.num_programs(1) - 1)
    def _():
        for bi in range(nb):
            for h in range(C_HEADS):
                j = h % 2
                cout_ref[bi, h] = caug_s[bi, h, j * C_DQK:(j + 1) * C_DQK, 0:C_DV]
            for p in range(C_HEADS // 2):
                n_col = caug_s[bi, 2 * p, :, C_DV:C_DV + 1] + caug_s[bi, 2 * p + 1, :, C_DV:C_DV + 1]
                nout_ref[bi, p:p + 1, :] = _col_to_row(n_col)
        mout_ref[...] = m_s[...]


def _layer1(x3d, mix3d, c0, n0, m0, q, nb, tb, chunk):
    bsz, seq, _ = x3d.shape
    blk = lambda b, i: (b, i, 0)
    per_b3 = lambda b, i: (b, 0, 0)
    per_b4 = lambda b, i: (b, 0, 0, 0)
    kern = functools.partial(_layer1_kernel, nb=nb, tb=tb, chunk=chunk)
    pairs = C_HEADS // 2
    return pl.pallas_call(
        kern,
        grid=(bsz // nb, seq // tb),
        in_specs=[pl.BlockSpec((nb, tb, D_MODEL), blk),
                  pl.BlockSpec((nb, tb, MIX0), blk),
                  pl.BlockSpec((nb, C_HEADS, C_DQK, C_DV), per_b4),
                  pl.BlockSpec((nb, pairs, LANES), per_b3),
                  pl.BlockSpec((nb, 1, C_HEADS), per_b3),
                  _resident((MIX0, D_MODEL)),
                  _resident((1, D_MODEL)),
                  _resident((D_MODEL, P1_PAD)),
                  _resident((1, LANES)),
                  _resident((C_V, D_MODEL)),
                  _resident((1, D_MODEL))],
        out_specs=[pl.BlockSpec((nb, tb, D_MODEL), blk),
                   pl.BlockSpec((nb, C_HEADS, C_DQK, C_DV), per_b4),
                   pl.BlockSpec((nb, pairs, LANES), per_b3),
                   pl.BlockSpec((nb, C_HEADS, LANES), per_b3)],
        out_shape=[jax.ShapeDtypeStruct((bsz, seq, D_MODEL), F32),
                   jax.ShapeDtypeStruct((bsz, C_HEADS, C_DQK, C_DV), F32),
                   jax.ShapeDtypeStruct((bsz, pairs, LANES), F32),
                   jax.ShapeDtypeStruct((bsz, C_HEADS, LANES), F32)],
        scratch_shapes=[pltpu.VMEM((nb, tb, P1_PAD), F32),
                        pltpu.VMEM((nb, tb, D_MODEL), F32),
                        pltpu.VMEM((nb, tb, C_V), F32),
                        pltpu.VMEM((nb, tb, LANES), F32),
                        pltpu.VMEM((nb, C_HEADS, LANES, 2 * C_DV), F32),
                        pltpu.VMEM((nb, C_HEADS, LANES), F32)],
        compiler_params=pltpu.CompilerParams(dimension_semantics=("parallel", "arbitrary"),
                                             vmem_limit_bytes=VMEM_LIMIT),
        name="layer1",
    )(x3d, mix3d, c0, n0, m0, q["wo0"], q["norm1"], q["w1"], q["gate_bias"], q["wo1"], q["normf"])


def _gate_tile(cols):
    tiled = jnp.concatenate([cols.astype(F32)] * GATE_COPIES, axis=1)
    return jnp.pad(tiled, ((0, 0), (0, LANES - tiled.shape[1])))


def _prepare_params(p):
    w_in0 = p["w_in0"]
    a_cols = A_MAIN + 2 * A_HEADS
    w0 = jnp.concatenate([w_in0[:, 0:A_MAIN], _gate_tile(w_in0[:, A_MAIN:a_cols]), w_in0[:, a_cols:]], axis=1)
    w_in1 = p["w_in1"]
    w1 = jnp.concatenate([w_in1[:, 0:P1_GATES], _gate_tile(w_in1[:, P1_GATES:])], axis=1)
    zeros_h = jnp.zeros((A_HEADS,), F32)
    alog = _gate_tile(jnp.concatenate([zeros_h, p["gdn_a_log"]]).reshape(1, -1))
    dtb = _gate_tile(jnp.concatenate([zeros_h, p["gdn_dt_bias"]]).reshape(1, -1))
    gate_bias = _gate_tile(jnp.concatenate([p["mlstm_b_i"], p["mlstm_b_f"]]).reshape(1, -1))
    w2a2 = jnp.zeros((2 * B_LORA, 2 * B_W), F32)
    w2a2 = w2a2.at[0:B_LORA, 0:B_W].set(p["rwkv_w2"]).at[B_LORA:, B_W:].set(p["rwkv_a2"])
    row = lambda v: v.reshape(1, -1).astype(F32)
    return dict(
        w0=w0.astype(BF16), w1=w1.astype(BF16), wb=w_in0[:, a_cols:].astype(BF16),
        wo0=p["w_out0"].astype(BF16), wo1=p["w_out1"].astype(BF16), w2a2=w2a2.astype(BF16),
        norm0=row(p["norm0_w"]), norm1=row(p["norm1_w"]), normf=row(p["norm_f_w"]),
        conv_w=p["gdn_conv_w"].astype(F32), alog=alog, dtb=dtb, gnw=row(p["gdn_norm_w"]),
        mu=row(p["rwkv_mu"]), rw0=row(p["rwkv_w0"]), ra0=row(p["rwkv_a0"]),
        k_k=row(p["rwkv_k_k"]), k_a=row(p["rwkv_k_a"]), r_k=row(p["rwkv_r_k"]),
        ln_w=row(p["rwkv_ln_w"]), ln_b=row(p["rwkv_ln_b"]), gate_bias=gate_bias,
    )


def _tiles(bsz, seq):
    chunk = math.gcd(seq, MAX_CHUNK)
    tb = min(seq, 256)
    nb = 2 if seq > MAX_CHUNK else 16
    nb = math.gcd(nb, bsz)
    tt = min(seq, PROJ_ROWS)
    ns = math.gcd(min(PROJ_ROWS // tt, PROJ_MAX_SEQS), bsz)
    return chunk, tb, nb, tt, ns


def _trunk(x, state, q):
    conv0, s_gdn, x_prev, s_rwkv, c0, n0, m0 = state
    bsz, seq, _ = x.shape
    chunk, tb, nb, tt, ns = _tiles(bsz, seq)
    xn_last, prev_row = _aux(x[:, -1, :], x_prev, q["norm0"], q["wb"])
    fa, fb, conv_new = _proj0(x, conv0, prev_row.reshape(bsz, 1, B_COLS), q, ns, tt)
    mix, s_gdn_new, s_rwkv_new = _layer0_mixers(fa, fb, s_gdn, s_rwkv, q, nb, tb, chunk)
    y, c_new, n_new, m_new = _layer1(x, mix, c0, n0.reshape(bsz, C_HEADS // 2, LANES),
                                     m0.reshape(bsz, 1, C_HEADS), q, nb, tb, chunk)
    return y, (conv_new, s_gdn_new, xn_last, s_rwkv_new, c_new,
               n_new.reshape(bsz, C_HEADS, C_DQK), m_new[:, :, 0])


def kernel(x_prompt, x_sample, state_gdn_conv, state_gdn, state_rwkv_shift, state_rwkv,
           state_mlstm_c, state_mlstm_n, state_mlstm_m,
           norm0_w, w_in0, w_out0, gdn_conv_w, gdn_a_log, gdn_dt_bias, gdn_norm_w,
           rwkv_mu, rwkv_w0, rwkv_w2, rwkv_a0, rwkv_a2, rwkv_k_k, rwkv_k_a, rwkv_r_k,
           rwkv_ln_w, rwkv_ln_b, norm1_w, w_in1, w_out1, mlstm_b_i, mlstm_b_f, norm_f_w):
    q = _prepare_params(dict(
        norm0_w=norm0_w, w_in0=w_in0, w_out0=w_out0, gdn_conv_w=gdn_conv_w, gdn_a_log=gdn_a_log,
        gdn_dt_bias=gdn_dt_bias, gdn_norm_w=gdn_norm_w, rwkv_mu=rwkv_mu, rwkv_w0=rwkv_w0,
        rwkv_w2=rwkv_w2, rwkv_a0=rwkv_a0, rwkv_a2=rwkv_a2, rwkv_k_k=rwkv_k_k, rwkv_k_a=rwkv_k_a,
        rwkv_r_k=rwkv_r_k, rwkv_ln_w=rwkv_ln_w, rwkv_ln_b=rwkv_ln_b, norm1_w=norm1_w, w_in1=w_in1,
        w_out1=w_out1, mlstm_b_i=mlstm_b_i, mlstm_b_f=mlstm_b_f, norm_f_w=norm_f_w))
    bsz = x_prompt.shape[0]
    prompt_state = (
        jnp.zeros((bsz, A_CONV - 1, A_QKV), F32),
        jnp.zeros((bsz, A_HEADS, A_DK, A_DV), F32),
        jnp.zeros((bsz, D_MODEL), F32),
        jnp.zeros((bsz, B_HEADS, B_N, B_N), F32),
        jnp.zeros((bsz, C_HEADS, C_DQK, C_DV), F32),
        jnp.zeros((bsz, C_HEADS, C_DQK), F32),
        jnp.zeros((bsz, C_HEADS), F32),
    )
    y_p, sp = _trunk(x_prompt, prompt_state, q)
    sample_state = (state_gdn_conv, state_gdn, state_rwkv_shift, state_rwkv,
                    state_mlstm_c, state_mlstm_n, state_mlstm_m)
    y_s, ss = _trunk(x_sample, sample_state, q)
    return (y_p, y_s) + tuple(sp) + tuple(ss)
```
